```python
import math
import jax
import jax.numpy as jnp
from jax import lax
import numpy as np

D_MODEL = 1024
BATCH = 8
SEQ = 2048
DEPTH = 1

PLE_DIM = 256
NSA_HEADS = 8
NSA_GROUPS = 2
NSA_HPG = NSA_HEADS // NSA_GROUPS
NSA_HEAD_DIM = 64
NSA_WIDTH = NSA_HEADS * NSA_HEAD_DIM
NSA_KV = NSA_GROUPS * NSA_HEAD_DIM
CMP_BLOCK = 32
CMP_STRIDE = 16
CMP_HIDDEN = 256
SEL_BLOCK = 64
SEL_TOPK = 8
WINDOW = 512
Q_BLOCK = 128
DN_HEADS = 4
DN_HEAD_DIM = 128
DN_WIDTH = DN_HEADS * DN_HEAD_DIM
DN_CONV = 4
DN_CHUNK = 64
NUM_BUCKETS = 32
REL_MAX_DIST = 1024
DEEPNORM_ALPHA = (2 * DEPTH) ** 0.25
DEEPNORM_BETA = (8 * DEPTH) ** -0.25
NEG = -1e30
FORCE = 1e6
IN_SPLITS = (NSA_WIDTH, 6 * NSA_KV, 3 * NSA_HEADS, NSA_WIDTH,
             3 * DN_WIDTH, DN_HEADS, DN_HEADS, DN_WIDTH, 2 * D_MODEL)
D_IN = sum(IN_SPLITS)

kernel_name = 'hybrid_nsa_gdn_deepnorm_layer'


def rel_bucket(dist):
    n = jnp.maximum(dist, 0)
    max_exact = NUM_BUCKETS // 2
    nf = jnp.maximum(n, 1).astype(jnp.float32)
    large = max_exact + (jnp.log(nf / max_exact) / math.log(REL_MAX_DIST / max_exact)
                         * (NUM_BUCKETS - max_exact)).astype(jnp.int32)
    large = jnp.minimum(large, NUM_BUCKETS - 1)
    return jnp.where(n < max_exact, n, large)


def masked_softmax(s, valid):
    s = jnp.where(valid, s.astype(jnp.float32), NEG)
    return jnp.where(valid, jax.nn.softmax(s, axis=-1), 0.0)


def layer_norm(h, g, b, eps=1e-5):
    h = h.astype(jnp.float32)
    mu = jnp.mean(h, -1, keepdims=True)
    var = jnp.mean(jnp.square(h - mu), -1, keepdims=True)
    return (h - mu) * lax.rsqrt(var + eps) * g.astype(jnp.float32) + b.astype(jnp.float32)


def l2norm(t):
    return t * lax.rsqrt(jnp.sum(jnp.square(t), -1, keepdims=True) + 1e-6)


def compress_blocks(t, pos, w1, w2):
    B, S, G, dh = t.shape
    r = CMP_BLOCK // CMP_STRIDE
    c = t.reshape(B, S // CMP_STRIDE, CMP_STRIDE, G, dh)
    n_cmp = S // CMP_STRIDE - r + 1
    blocks = jnp.concatenate([c[:, j:j + n_cmp] for j in range(r)], axis=2)
    blocks = blocks + pos[None, None, :, None, :]
    flat = blocks.transpose(0, 1, 3, 2, 4).reshape(B, n_cmp, G, CMP_BLOCK * dh)
    return jax.nn.gelu(flat @ w1) @ w2


def nsa_mixer(q, kv, gates, pos_k, pos_v, w1_k, w2_k, w1_v, w2_v, rel_bias):
    B, S, _ = q.shape
    G, Hg, dh = NSA_GROUPS, NSA_HPG, NSA_HEAD_DIM
    q = q.astype(jnp.float32).reshape(B, S, G, Hg, dh) * dh ** -0.5
    k_c, v_c, k_s, v_s, k_w, v_w = [t.reshape(B, S, G, dh) for t in jnp.split(kv, 6, axis=-1)]
    tab = rel_bias.astype(jnp.float32)
    tab_g = tab.reshape(NUM_BUCKETS, G, Hg)
    t_pos = jnp.arange(S)

    kc = compress_blocks(k_c, pos_k, w1_k, w2_k)
    vc = compress_blocks(v_c, pos_v, w1_v, w2_v)
    n_cmp = kc.shape[1]
    cmp_end = jnp.arange(n_cmp) * CMP_STRIDE + CMP_BLOCK - 1
    dist_c = t_pos[:, None] - cmp_end[None, :]
    bias_c = tab[rel_bucket(dist_c)].reshape(S, n_cmp, G, Hg).transpose(2, 3, 0, 1)
    p_c = masked_softmax(jnp.einsum('bsghd,bngd->bghsn', q, kc) + bias_c, dist_c >= 0)
    o_cmp = jnp.einsum('bghsn,bngd->bsghd', p_c, vc)

    n_sel = S // SEL_BLOCK
    ci = jnp.arange(n_cmp)[:, None]
    sj = jnp.arange(n_sel)[None, :]
    overlap = ((ci * CMP_STRIDE < (sj + 1) * SEL_BLOCK) &
               (ci * CMP_STRIDE + CMP_BLOCK > sj * SEL_BLOCK)).astype(jnp.float32)
    imp = jnp.einsum('bghsn,nj->bgsj', p_c, overlap)
    cur = (t_pos // SEL_BLOCK)[:, None]
    blk = jnp.arange(n_sel)[None, :]
    forced = (blk == 0) | (blk == cur) | (blk == cur - 1)
    imp = jnp.where(forced, FORCE, jnp.where(blk > cur, -FORCE, imp))
    n_top = min(SEL_TOPK, n_sel)
    _, sel_idx = lax.top_k(imp, n_top)

    kb_s = k_s.reshape(B, n_sel, SEL_BLOCK, G, dh).transpose(0, 3, 1, 2, 4)
    vb_s = v_s.reshape(B, n_sel, SEL_BLOCK, G, dh).transpose(0, 3, 1, 2, 4)
    pad = ((0, 0), (WINDOW, 0), (0, 0), (0, 0))
    kw_p = jnp.pad(k_w, pad)
    vw_p = jnp.pad(v_w, pad)
    bi = jnp.arange(B)[:, None, None, None]
    gi = jnp.arange(G)[None, :, None, None]
    gi5 = jnp.arange(G)[None, :, None, None, None]
    sel_off = jnp.arange(SEL_BLOCK)
    win_off = jnp.arange(WINDOW + Q_BLOCK)

    def query_block(i):
        q0 = i * Q_BLOCK
        tq = q0 + jnp.arange(Q_BLOCK)
        qb = lax.dynamic_slice_in_dim(q, q0, Q_BLOCK, axis=1)
        idx = lax.dynamic_slice_in_dim(sel_idx, q0, Q_BLOCK, axis=2)
        ks = kb_s[bi, gi, idx]
        vs = vb_s[bi, gi, idx]
        kpos = idx[..., None] * SEL_BLOCK + sel_off
        dist = tq[None, None, :, None, None] - kpos
        bias_s = jnp.moveaxis(tab_g[rel_bucket(dist), gi5], -1, 2)
        s_s = jnp.einsum('bqghd,bgqnld->bghqnl', qb, ks) + bias_s
        s_s = s_s.reshape(B, G, Hg, Q_BLOCK, n_top * SEL_BLOCK)
        valid_s = (dist >= 0).reshape(B, G, 1, Q_BLOCK, n_top * SEL_BLOCK)
        p_s = masked_softmax(s_s, valid_s).reshape(B, G, Hg, Q_BLOCK, n_top, SEL_BLOCK)
        o_s = jnp.einsum('bghqnl,bgqnld->bqghd', p_s, vs)
        kw = lax.dynamic_slice_in_dim(kw_p, q0, WINDOW + Q_BLOCK, axis=1)
        vw = lax.dynamic_slice_in_dim(vw_p, q0, WINDOW + Q_BLOCK, axis=1)
        kposw = q0 - WINDOW + win_off
        distw = tq[:, None] - kposw[None, :]
        validw = (distw >= 0) & (distw < WINDOW) & (kposw[None, :] >= 0)
        bias_w = tab[rel_bucket(distw)].reshape(Q_BLOCK, WINDOW + Q_BLOCK, G, Hg).transpose(2, 3, 0, 1)
        p_w = masked_softmax(jnp.einsum('bqghd,bkgd->bghqk', qb, kw) + bias_w, validw)
        o_w = jnp.einsum('bghqk,bkgd->bqghd', p_w, vw)
        return o_s, o_w

    o_sel, o_win = lax.map(query_block, jnp.arange(S // Q_BLOCK))
    o_sel = jnp.moveaxis(o_sel, 0, 1).reshape(B, S, G, Hg, dh)
    o_win = jnp.moveaxis(o_win, 0, 1).reshape(B, S, G, Hg, dh)

    g = jax.nn.sigmoid(gates.astype(jnp.float32)).reshape(B, S, 3, G, Hg, 1)
    o = g[:, :, 0] * o_cmp + g[:, :, 1] * o_sel + g[:, :, 2] * o_win
    return o.reshape(B, S, NSA_WIDTH)


def chunk_gated_delta_rule(q, k, v, g, beta):
    B, S, H, dk = q.shape
    dv = v.shape[-1]
    C = DN_CHUNK
    N = S // C

    def chunks(t):
        return jnp.moveaxis(t.reshape((B, N, C, H) + t.shape[3:]), 3, 1)

    qc = chunks(q * dk ** -0.5)
    kc = chunks(k)
    vc = chunks(v)
    bc = chunks(beta)
    gc = jnp.cumsum(chunks(g), axis=-1)
    pos = jnp.arange(C)
    incl = pos[:, None] >= pos[None, :]
    strict = pos[:, None] > pos[None, :]
    diff = gc[..., :, None] - gc[..., None, :]
    decay = jnp.where(incl, jnp.exp(jnp.where(incl, diff, 0.0)), 0.0)
    kb = kc * bc[..., None]
    lower = jnp.where(strict, jnp.einsum('bhncd,bhnjd->bhncj', kb, kc) * decay, 0.0)
    rhs = jnp.concatenate([vc * bc[..., None], kb * jnp.exp(gc)[..., None]], axis=-1)
    sol = lax.linalg.triangular_solve(lower, rhs, left_side=True, lower=True, unit_diagonal=True)
    u, w = sol[..., :dv], sol[..., dv:]
    a_qk = jnp.einsum('bhncd,bhnjd->bhncj', qc, kc) * decay

    def step(state, xs):
        q_i, k_i, u_i, w_i, a_i, g_i = xs
        v_new = u_i - jnp.einsum('bhck,bhkv->bhcv', w_i, state)
        o_i = (jnp.einsum('bhck,bhkv->bhcv', q_i * jnp.exp(g_i)[..., None], state)
               + jnp.einsum('bhcj,bhjv->bhcv', a_i, v_new))
        g_last = g_i[..., -1]
        k_dec = k_i * jnp.exp(g_last[..., None] - g_i)[..., None]
        state = state * jnp.exp(g_last)[..., None, None] + jnp.einsum('bhck,bhcv->bhkv', k_dec, v_new)
        return state, o_i

    xs = tuple(jnp.moveaxis(t, 2, 0) for t in (qc, kc, u, w, a_qk, gc))
    state0 = jnp.zeros((B, H, dk, dv), jnp.float32)
    _, o = lax.scan(step, state0, xs)
    return o.transpose(1, 0, 3, 2, 4).reshape(B, S, H, dv)


def deltanet_mixer(qkv, beta_in, a_in, z, conv_w, a_log, dt_bias, norm_w):
    B, S, C = qkv.shape
    H, d = DN_HEADS, DN_HEAD_DIM
    qkv = lax.conv_general_dilated(qkv, conv_w.astype(qkv.dtype)[:, None, :], window_strides=(1,),
                                   padding=((DN_CONV - 1, 0),),
                                   dimension_numbers=('NWC', 'WIO', 'NWC'), feature_group_count=C)
    qkv = jax.nn.silu(qkv.astype(jnp.float32))
    q, k, v = [t.reshape(B, S, H, d) for t in jnp.split(qkv, 3, axis=-1)]
    q, k = l2norm(q), l2norm(k)
    beta = jax.nn.sigmoid(beta_in.astype(jnp.float32))
    g = -jnp.exp(a_log.astype(jnp.float32)) * jax.nn.softplus(a_in.astype(jnp.float32) + dt_bias.astype(jnp.float32))
    o = chunk_gated_delta_rule(q, k, v, g, beta)
    o = o * lax.rsqrt(jnp.mean(jnp.square(o), -1, keepdims=True) + 1e-6) * norm_w.astype(jnp.float32)
    o = o * jax.nn.silu(z.astype(jnp.float32)).reshape(B, S, H, d)
    return o.reshape(B, S, DN_WIDTH)


def setup_inputs(seed: int = 0) -> dict:
    key = jax.random.key(seed)
    ks = jax.random.split(key, 24)
    L = DEPTH

    def nrm(k, shape, scale):
        return jax.random.normal(k, shape, jnp.float32) * scale

    dt = jnp.exp(jax.random.uniform(ks[12], (L, DN_HEADS), jnp.float32, math.log(1e-3), math.log(1e-1)))
    return {
        'x': nrm(ks[0], (BATCH, SEQ, D_MODEL), 1.0),
        'p': nrm(ks[1], (L, BATCH, SEQ, PLE_DIM), 1.0),
        'w_in': nrm(ks[2], (L, D_MODEL, D_IN), D_MODEL ** -0.5),
        'cmp_pos_k': nrm(ks[3], (L, CMP_BLOCK, NSA_HEAD_DIM), 0.1),
        'cmp_pos_v': nrm(ks[4], (L, CMP_BLOCK, NSA_HEAD_DIM), 0.1),
        'cmp_w1_k': nrm(ks[5], (L, CMP_BLOCK * NSA_HEAD_DIM, CMP_HIDDEN), (CMP_BLOCK * NSA_HEAD_DIM) ** -0.5),
        'cmp_w2_k': nrm(ks[6], (L, CMP_HIDDEN, NSA_HEAD_DIM), CMP_HIDDEN ** -0.5),
        'cmp_w1_v': nrm(ks[7], (L, CMP_BLOCK * NSA_HEAD_DIM, CMP_HIDDEN), (CMP_BLOCK * NSA_HEAD_DIM) ** -0.5),
        'cmp_w2_v': nrm(ks[8], (L, CMP_HIDDEN, NSA_HEAD_DIM), CMP_HIDDEN ** -0.5),
        'rel_bias': nrm(ks[9], (NUM_BUCKETS, NSA_HEADS), 0.2),
        'dn_conv_w': nrm(ks[10], (L, DN_CONV, 3 * DN_WIDTH), DN_CONV ** -0.5),
        'dn_a_log': jnp.log(jax.random.uniform(ks[11], (L, DN_HEADS), jnp.float32, 1.0, 16.0)),
        'dn_dt_bias': dt + jnp.log(-jnp.expm1(-dt)),
        'dn_norm_w': 1.0 + nrm(ks[13], (L, DN_HEAD_DIM), 0.05),
        'w_branch_a': nrm(ks[14], (L, NSA_WIDTH, D_MODEL), NSA_WIDTH ** -0.5),
        'w_branch_b': nrm(ks[15], (L, DN_WIDTH, D_MODEL), DN_WIDTH ** -0.5),
        'w_out': nrm(ks[16], (L, D_MODEL, D_MODEL), DEEPNORM_BETA * D_MODEL ** -0.5),
        'w_ple': nrm(ks[17], (L, PLE_DIM, D_MODEL), PLE_DIM ** -0.5),
        'w_ple_gate': nrm(ks[18], (L, D_MODEL, D_MODEL), D_MODEL ** -0.5),
        'ln_g': 1.0 + nrm(ks[19], (L, D_MODEL), 0.05),
        'ln_b': nrm(ks[20], (L, D_MODEL), 0.02),
    }


def reference(x, p, w_in, cmp_pos_k, cmp_pos_v, cmp_w1_k, cmp_w2_k, cmp_w1_v, cmp_w2_v, rel_bias,
              dn_conv_w, dn_a_log, dn_dt_bias, dn_norm_w, w_branch_a, w_branch_b, w_out,
              w_ple, w_ple_gate, ln_g, ln_b):
    out_dtype = x.dtype
    cuts = []
    acc = 0
    for s in IN_SPLITS[:-1]:
        acc += s
        cuts.append(acc)
    for i in range(DEPTH):
        proj = x @ w_in[i]
        q_a, kv_a, g_a, z_a, qkv_b, beta_b, a_b, z_b, g_merge = jnp.split(proj, cuts, axis=-1)
        o_a = nsa_mixer(q_a, kv_a, g_a, cmp_pos_k[i], cmp_pos_v[i], cmp_w1_k[i], cmp_w2_k[i],
                        cmp_w1_v[i], cmp_w2_v[i], rel_bias)
        o_a = o_a * jax.nn.silu(z_a.astype(jnp.float32))
        o_b = deltanet_mixer(qkv_b, beta_b, a_b, z_b, dn_conv_w[i], dn_a_log[i], dn_dt_bias[i], dn_norm_w[i])
        y_a = o_a @ w_branch_a[i]
        y_b = o_b @ w_branch_b[i]
        gm_a, gm_b = jnp.split(jax.nn.sigmoid(g_merge.astype(jnp.float32)), 2, axis=-1)
        mixed = (gm_a * y_a + gm_b * y_b) @ w_out[i]
        h = DEEPNORM_ALPHA * x.astype(jnp.float32) + mixed
        h = h + jax.nn.sigmoid(h @ w_ple_gate[i]) * (p[i] @ w_ple[i])
        x = layer_norm(h, ln_g[i], ln_b[i]).astype(out_dtype)
    return x
```

```python
import functools
import math

import numpy as np
import jax
import jax.numpy as jnp
from jax import lax
from jax.experimental import pallas as pl
from jax.experimental.pallas import tpu as pltpu

D_MODEL = 1024
PLE_DIM = 256
NSA_HEADS = 8
NSA_GROUPS = 2
NSA_HPG = NSA_HEADS // NSA_GROUPS
NSA_HEAD_DIM = 64
NSA_WIDTH = NSA_HEADS * NSA_HEAD_DIM
NSA_KV = NSA_GROUPS * NSA_HEAD_DIM
CMP_BLOCK = 32
CMP_STRIDE = 16
CMP_HIDDEN = 256
SEL_BLOCK = 64
SEL_TOPK = 8
WINDOW = 512
DN_HEADS = 4
DN_HEAD_DIM = 128
DN_WIDTH = DN_HEADS * DN_HEAD_DIM
DN_CONV = 4
DN_CHUNK = 64
NUM_BUCKETS = 32
REL_MAX_DIST = 1024
DEEPNORM_ALPHA = 2.0 ** 0.25
NEG = -1e30
FORCE = 1e6

LANES = 128
TILE = 128
SEL_LANE0 = 64
VMEM_LIMIT = 56 * 1024 * 1024

HIGHEST = lax.Precision.HIGHEST

PB_QP = 0
PB_KS = PB_QP + NSA_HEADS * LANES
PB_KW = PB_KS + NSA_GROUPS * LANES
PB_VS = PB_KW + NSA_GROUPS * LANES
PB_VW = PB_VS + LANES
PB_WIDTH = PB_VW + LANES
PF_KC = 0
PF_VC = 128
PF_SMALL = 256
PF_ZA = 512
PF_ZB = 1024
PF_QKVB = 1536
PF_GM = 3072
PF_WIDTH = 5120
SMALL_BETA = 3 * NSA_HEADS
SMALL_A = SMALL_BETA + DN_HEADS


def _bucket_thresholds():
    max_exact = NUM_BUCKETS // 2
    span = NUM_BUCKETS - max_exact
    ratio = REL_MAX_DIST // max_exact
    thr = list(range(1, max_exact + 1))
    for k in range(1, span):
        n = max_exact
        while n ** span < max_exact ** span * ratio ** k:
            n += 1
        thr.append(n)
    return tuple(thr)


_THR = _bucket_thresholds()


def _cparams(n_axes):
    return pltpu.CompilerParams(dimension_semantics=("arbitrary",) * n_axes, vmem_limit_bytes=VMEM_LIMIT)


def _mm_kernel(x_ref, w_ref, o_ref):
    o_ref[...] = jnp.dot(x_ref[...], w_ref[...], preferred_element_type=jnp.float32).astype(o_ref.dtype)


def _matmul(x, w, out_dtype, tm, tn):
    m, k = x.shape
    n = w.shape[1]
    return pl.pallas_call(
        _mm_kernel,
        grid=(n // tn, m // tm),
        in_specs=[pl.BlockSpec((tm, k), lambda j, i: (i, 0)),
                  pl.BlockSpec((k, tn), lambda j, i: (0, j))],
        out_specs=pl.BlockSpec((tm, tn), lambda j, i: (i, j)),
        out_shape=jax.ShapeDtypeStruct((m, n), out_dtype),
        compiler_params=_cparams(2),
    )(x, w)


def _prep_w_in(w):
    d = w.shape[0]
    o = 0
    wq = w[:, o:o + NSA_WIDTH]; o += NSA_WIDTH
    wkv = w[:, o:o + 6 * NSA_KV]; o += 6 * NSA_KV
    wg = w[:, o:o + 3 * NSA_HEADS]; o += 3 * NSA_HEADS
    wza = w[:, o:o + NSA_WIDTH]; o += NSA_WIDTH
    wqkvb = w[:, o:o + 3 * DN_WIDTH]; o += 3 * DN_WIDTH
    wbeta = w[:, o:o + DN_HEADS]; o += DN_HEADS
    wa = w[:, o:o + DN_HEADS]; o += DN_HEADS
    wzb = w[:, o:o + DN_WIDTH]; o += DN_WIDTH
    wgm = w[:, o:o + 2 * D_MODEL]
    wkc, wvc, wks, wvs, wkw, wvw = [wkv[:, i * NSA_KV:(i + 1) * NSA_KV] for i in range(6)]

    def pad_heads(t, nh):
        t = t.reshape(d, nh, NSA_HEAD_DIM)
        return jnp.concatenate([t, jnp.zeros_like(t)], axis=-1).reshape(d, nh * LANES)

    wb = jnp.concatenate([pad_heads(wq * NSA_HEAD_DIM ** -0.5, NSA_HEADS), pad_heads(wks, NSA_GROUPS),
                          pad_heads(wkw, NSA_GROUPS), wvs, wvw], axis=1).astype(jnp.bfloat16)
    small = jnp.concatenate([wg, wbeta, wa, jnp.zeros((d, LANES - SMALL_A - DN_HEADS), w.dtype)], axis=1)
    wf = jnp.concatenate([wkc, wvc, small, jnp.zeros((d, LANES), w.dtype), wza, wzb, wqkvb, wgm],
                         axis=1).astype(jnp.bfloat16)
    return wb, wf


def _bias_kernel(tab_ref, sel_ref, win_ref, cmp_ref, *, n_cmp):
    h = pl.program_id(0)

    def lookup(n):
        val = jnp.full(n.shape, tab_ref[0, h], jnp.float32)
        for b in range(1, NUM_BUCKETS):
            val = jnp.where(n >= _THR[b - 1], tab_ref[b, h], val)
        return val

    i = lax.broadcasted_iota(jnp.int32, (TILE, TILE), 0)
    j = lax.broadcasted_iota(jnp.int32, (TILE, TILE), 1)
    for dt in range(sel_ref.shape[1]):
        dist = dt * TILE + i - j
        v = lookup(jnp.maximum(dist, 0))
        sel_ref[0, dt] = jnp.where(dist >= 0, v, NEG)
        if dt < win_ref.shape[1]:
            win_ref[0, dt] = jnp.where((dist >= 0) & (dist < WINDOW), v, NEG)
    s = lax.broadcasted_iota(jnp.int32, cmp_ref.shape[1:], 0)
    c = lax.broadcasted_iota(jnp.int32, cmp_ref.shape[1:], 1)
    dist = s - (c * CMP_STRIDE + CMP_BLOCK - 1)
    cmp_ref[0] = jnp.where((dist >= 0) & (c < n_cmp), lookup(jnp.maximum(dist, 0)), NEG)


def _bias_tables(rel_bias, seq):
    nq = seq // TILE
    nwin = min(WINDOW // TILE + 1, nq)
    nc = seq // CMP_STRIDE
    n_cmp = nc - CMP_BLOCK // CMP_STRIDE + 1
    return pl.pallas_call(
        functools.partial(_bias_kernel, n_cmp=n_cmp),
        grid=(NSA_HEADS,),
        in_specs=[pl.BlockSpec(memory_space=pltpu.SMEM)],
        out_specs=[pl.BlockSpec((1, nq, TILE, TILE), lambda h: (h, 0, 0, 0)),
                   pl.BlockSpec((1, nwin, TILE, TILE), lambda h: (h, 0, 0, 0)),
                   pl.BlockSpec((1, seq, nc), lambda h: (h, 0, 0))],
        out_shape=[jax.ShapeDtypeStruct((NSA_HEADS, nq, TILE, TILE), jnp.float32),
                   jax.ShapeDtypeStruct((NSA_HEADS, nwin, TILE, TILE), jnp.float32),
                   jax.ShapeDtypeStruct((NSA_HEADS, seq, nc), jnp.float32)],
        compiler_params=_cparams(1),
    )(rel_bias.astype(jnp.float32))


def _gelu_tanh(x):
    return x * (0.5 * (1.0 + jnp.tanh(math.sqrt(2.0 / math.pi) * (x + 0.044715 * (x * x * x)))))


def _compress_kernel(r_ref, pos_ref, w1_ref, w2_ref, o_ref):
    r = r_ref[0, 0]
    nc = r.shape[0]
    a = jnp.dot((r + pos_ref[0, 0:1, :]).astype(jnp.bfloat16), w1_ref[0, 0], preferred_element_type=jnp.float32)
    b = jnp.dot((r + pos_ref[0, 1:2, :]).astype(jnp.bfloat16), w1_ref[0, 1], preferred_element_type=jnp.float32)
    hid = a + pltpu.roll(b, nc - 1, 0)
    o_ref[0, 0] = jnp.dot(_gelu_tanh(hid).astype(jnp.bfloat16), w2_ref[0], preferred_element_type=jnp.float32)


def _compress(rkv, pos2, w1p, w2p):
    bsz, _, nc, width = rkv.shape
    hid = w1p.shape[-1]
    return pl.pallas_call(
        _compress_kernel,
        grid=(2, bsz),
        in_specs=[pl.BlockSpec((1, 1, nc, width), lambda k, b: (b, k, 0, 0)),
                  pl.BlockSpec((1, 2, width), lambda k, b: (k, 0, 0)),
                  pl.BlockSpec((1, 2, width, hid), lambda k, b: (k, 0, 0, 0)),
                  pl.BlockSpec((1, hid, LANES), lambda k, b: (k, 0, 0))],
        out_specs=pl.BlockSpec((1, 1, nc, LANES), lambda k, b: (b, k, 0, 0)),
        out_shape=jax.ShapeDtypeStruct((bsz, 2, nc, LANES), jnp.float32),
        compiler_params=_cparams(2),
    )(rkv, pos2, w1p, w2p)


def _prep_compress_weights(pos_k, pos_v, w1_k, w1_v, w2_k, w2_v):
    eye = jnp.eye(NSA_GROUPS, dtype=jnp.float32)
    half = CMP_BLOCK // 2

    def w1_both(w1):
        w = w1.reshape(2, half, NSA_HEAD_DIM, CMP_HIDDEN)
        w = jnp.einsum('aldj,gh->algdhj', w, eye)
        return w.reshape(2, half * NSA_GROUPS * NSA_HEAD_DIM, NSA_GROUPS * CMP_HIDDEN)

    def w2_both(w2):
        return jnp.einsum('jd,gh->gjhd', w2, eye).reshape(NSA_GROUPS * CMP_HIDDEN, NSA_GROUPS * NSA_HEAD_DIM)

    def pos_both(pos):
        p = pos.reshape(2, half, 1, NSA_HEAD_DIM)
        return jnp.broadcast_to(p, (2, half, NSA_GROUPS, NSA_HEAD_DIM)).reshape(2, half * NSA_KV)

    pos2 = jnp.stack([pos_both(pos_k), pos_both(pos_v)]).astype(jnp.float32)
    w1p = jnp.stack([w1_both(w1_k), w1_both(w1_v)]).astype(jnp.bfloat16)
    w2p = jnp.stack([w2_both(w2_k), w2_both(w2_v)]).astype(jnp.bfloat16)
    return pos2, w1p, w2p


def _pair_lanes(x, g):
    sw = pltpu.roll(x, LANES // 2, 1)
    lane = lax.broadcasted_iota(jnp.int32, x.shape, 1)
    own = (lane < LANES // 2) == (g == 0)
    return jnp.where(own, x, sw)


def _nsa_kernel(q_ref, ks_ref, kw_ref, vs_ref, vw_ref, kcvc_ref, bc_ref, bsel_ref, bwin_ref,
                small_ref, za_ref, gexp_ref, o_ref,
                ksa, vsp, vwp, kcs, vcp, qa, qp, m_s, l_s, acc_s, *, n_cmp, n_sel, n_top):
    g = pl.program_id(1)
    qt = pl.program_id(2)
    tq = TILE
    seq = ks_ref.shape[0]
    nc = kcvc_ref.shape[2]

    @pl.when(qt == 0)
    def _():
        row = lax.broadcasted_iota(jnp.int32, (seq, LANES), 0)
        lane = lax.broadcasted_iota(jnp.int32, (seq, LANES), 1)
        onehot = (lane - SEL_LANE0) == (row >> 6)
        ksa[...] = jnp.where(onehot, 1.0, ks_ref[...].astype(jnp.float32)).astype(jnp.bfloat16)
        vsp[...] = _pair_lanes(vs_ref[...].astype(jnp.float32), g).astype(jnp.bfloat16)
        vwp[...] = _pair_lanes(vw_ref[...].astype(jnp.float32), g).astype(jnp.bfloat16)
        kcs[...] = _pair_lanes(kcvc_ref[0, 0], g).astype(jnp.bfloat16)
        vcp[...] = _pair_lanes(kcvc_ref[0, 1], g).astype(jnp.bfloat16)

    qh = [q_ref[:, hh * LANES:(hh + 1) * LANES] for hh in range(NSA_HPG)]
    kc = kcs[...]
    vc = vcp[...]
    o_cmp = []
    psum = jnp.zeros((tq, nc), jnp.float32)
    for hh in range(NSA_HPG):
        bias = bc_ref[hh]
        valid = bias > 0.5 * NEG
        s = lax.dot_general(qh[hh], kc, (((1,), (1,)), ((), ())), preferred_element_type=jnp.float32) + bias
        m = jnp.max(s, axis=-1, keepdims=True)
        e = jnp.where(valid, jnp.exp(s - m), 0.0)
        den = jnp.maximum(jnp.sum(e, axis=-1, keepdims=True), 1e-30)
        p = e / den
        psum = psum + p
        o_cmp.append(jnp.dot(p.astype(jnp.bfloat16), vc, preferred_element_type=jnp.float32))

    ci = lax.broadcasted_iota(jnp.int32, (nc, LANES), 0)
    sj = lax.broadcasted_iota(jnp.int32, (nc, LANES), 1) - SEL_LANE0
    overlap = ((ci * CMP_STRIDE < (sj + 1) * SEL_BLOCK) & (ci * CMP_STRIDE + CMP_BLOCK > sj * SEL_BLOCK)
               & (sj >= 0) & (sj < n_sel) & (ci < n_cmp))
    imp = jnp.dot(psum, overlap.astype(jnp.float32), precision=HIGHEST, preferred_element_type=jnp.float32)
    imp_t = imp.T[SEL_LANE0:SEL_LANE0 + n_sel, :]
    blk = lax.broadcasted_iota(jnp.int32, (n_sel, tq), 0)
    cur = (qt * tq + lax.broadcasted_iota(jnp.int32, (n_sel, tq), 1)) >> 6
    forced = (blk == 0) | (blk == cur) | (blk == cur - 1)
    imp_t = jnp.where(forced, FORCE, jnp.where(blk > cur, -FORCE, imp_t))
    rank = jnp.zeros((n_sel, tq), jnp.int32)
    for j in range(n_sel):
        other = imp_t[j:j + 1, :]
        ahead = (other > imp_t) | ((other == imp_t) & (blk > j))
        rank = rank + ahead.astype(jnp.int32)
    selb = jnp.where(rank < n_top, 0.0, NEG)
    pieces = [jnp.zeros((SEL_LANE0, tq), jnp.float32), selb]
    if LANES - SEL_LANE0 - n_sel > 0:
        pieces.append(jnp.zeros((LANES - SEL_LANE0 - n_sel, tq), jnp.float32))
    selb_r = jnp.concatenate(pieces, axis=0).T.astype(jnp.bfloat16)

    for hh in range(NSA_HPG):
        qa[hh * tq:(hh + 1) * tq, :] = qh[hh] + selb_r
        qp[hh * tq:(hh + 1) * tq, :] = qh[hh]

    def flash(q_tiles, k_tiles, v_tiles, bias_ref, n_steps, kt_of):
        m_s[...] = jnp.full(m_s.shape, NEG, jnp.float32)
        l_s[...] = jnp.zeros(l_s.shape, jnp.float32)
        acc_s[...] = jnp.zeros(acc_s.shape, jnp.float32)

        def body(step, carry):
            kt = kt_of(step)
            dt = qt - kt
            r0 = pl.multiple_of(kt * TILE, TILE)
            k = k_tiles[pl.ds(r0, TILE), :]
            v = v_tiles[pl.ds(r0, TILE), :]
            s = lax.dot_general(q_tiles[...], k, (((1,), (1,)), ((), ())), preferred_element_type=jnp.float32)
            s = s + jnp.concatenate([bias_ref[hh, dt] for hh in range(NSA_HPG)], axis=0)
            m_old = m_s[...]
            m_new = jnp.maximum(m_old, jnp.max(s, axis=-1, keepdims=True))
            alpha = jnp.exp(m_old - m_new)
            p = jnp.exp(s - m_new)
            l_s[...] = alpha * l_s[...] + jnp.sum(p, axis=-1, keepdims=True)
            acc_s[...] = alpha * acc_s[...] + jnp.dot(p.astype(jnp.bfloat16), v,
                                                      preferred_element_type=jnp.float32)
            m_s[...] = m_new
            return carry

        lax.fori_loop(0, n_steps, body, 0)
        return acc_s[...] / l_s[...]

    def by_head_pairs(x):
        lane = lax.broadcasted_iota(jnp.int32, (tq, LANES), 1)
        lo = lane < LANES // 2
        return jnp.concatenate([jnp.where(lo, x[(2 * j) * tq:(2 * j + 1) * tq], x[(2 * j + 1) * tq:(2 * j + 2) * tq])
                                for j in range(NSA_HPG // 2)], axis=1)

    o_sel = by_head_pairs(flash(qa, ksa, vsp, bsel_ref, qt + 1, lambda st: st))
    n_win = jnp.minimum(qt, bwin_ref.shape[1] - 1) + 1
    o_win = by_head_pairs(flash(qp, kw_ref, vwp, bwin_ref, n_win, lambda st: qt - st))
    o_cmp = by_head_pairs(jnp.concatenate(o_cmp, axis=0))

    gates = jax.nn.sigmoid(small_ref[...])
    gx = jnp.dot(gates, gexp_ref[0], precision=HIGHEST, preferred_element_type=jnp.float32)
    w = NSA_HPG * NSA_HEAD_DIM
    o = gx[:, 0:w] * o_cmp + gx[:, w:2 * w] * o_sel + gx[:, 2 * w:3 * w] * o_win
    z = za_ref[...]
    o_ref[...] = (o * (z * jax.nn.sigmoid(z))).astype(o_ref.dtype)


def _gate_expand():
    e = np.zeros((NSA_GROUPS, LANES, 3 * NSA_HPG * NSA_HEAD_DIM), np.float32)
    for g in range(NSA_GROUPS):
        for br in range(3):
            for hh in range(NSA_HPG):
                c0 = br * NSA_HPG * NSA_HEAD_DIM + hh * NSA_HEAD_DIM
                e[g, br * NSA_HEADS + g * NSA_HPG + hh, c0:c0 + NSA_HEAD_DIM] = 1.0
    return jnp.asarray(e)


def _nsa(pb, pf, kcvc, bias_sel, bias_win, bias_cmp, bsz, seq):
    nq = seq // TILE
    nc = seq // CMP_STRIDE
    n_cmp = nc - CMP_BLOCK // CMP_STRIDE + 1
    n_sel = seq // SEL_BLOCK
    n_top = min(SEL_TOPK, n_sel)
    nwin = bias_win.shape[1]
    gw = NSA_HPG * NSA_HEAD_DIM
    kern = functools.partial(_nsa_kernel, n_cmp=n_cmp, n_sel=n_sel, n_top=n_top)
    return pl.pallas_call(
        kern,
        grid=(bsz, NSA_GROUPS, nq),
        in_specs=[
            pl.BlockSpec((TILE, NSA_HPG * LANES), lambda b, g, t: (b * nq + t, g)),
            pl.BlockSpec((seq, LANES), lambda b, g, t: (b, PB_KS // LANES + g)),
            pl.BlockSpec((seq, LANES), lambda b, g, t: (b, PB_KW // LANES + g)),
            pl.BlockSpec((seq, LANES), lambda b, g, t: (b, PB_VS // LANES)),
            pl.BlockSpec((seq, LANES), lambda b, g, t: (b, PB_VW // LANES)),
            pl.BlockSpec((1, 2, nc, LANES), lambda b, g, t: (b, 0, 0, 0)),
            pl.BlockSpec((NSA_HPG, TILE, nc), lambda b, g, t: (g, t, 0)),
            pl.BlockSpec((NSA_HPG, nq, TILE, TILE), lambda b, g, t: (g, 0, 0, 0)),
            pl.BlockSpec((NSA_HPG, nwin, TILE, TILE), lambda b, g, t: (g, 0, 0, 0)),
            pl.BlockSpec((TILE, LANES), lambda b, g, t: (b * nq + t, PF_SMALL // LANES)),
            pl.BlockSpec((TILE, gw), lambda b, g, t: (b * nq + t, PF_ZA // gw + g)),
            pl.BlockSpec((1, LANES, 3 * gw), lambda b, g, t: (g, 0, 0)),
        ],
        out_specs=pl.BlockSpec((TILE, gw), lambda b, g, t: (b * nq + t, g)),
        out_shape=jax.ShapeDtypeStruct((bsz * seq, NSA_WIDTH), jnp.bfloat16),
        scratch_shapes=[
            pltpu.VMEM((seq, LANES), jnp.bfloat16),
            pltpu.VMEM((seq, LANES), jnp.bfloat16),
            pltpu.VMEM((seq, LANES), jnp.bfloat16),
            pltpu.VMEM((nc, LANES), jnp.bfloat16),
            pltpu.VMEM((nc, LANES), jnp.bfloat16),
            pltpu.VMEM((NSA_HPG * TILE, LANES), jnp.bfloat16),
            pltpu.VMEM((NSA_HPG * TILE, LANES), jnp.bfloat16),
            pltpu.VMEM((NSA_HPG * TILE, 1), jnp.float32),
            pltpu.VMEM((NSA_HPG * TILE, 1), jnp.float32),
            pltpu.VMEM((NSA_HPG * TILE, LANES), jnp.float32),
        ],
        compiler_params=_cparams(3),
    )(pb, pb, pb, pb, pb, kcvc, bias_cmp, bias_sel, bias_win, pf, pf, _gate_expand())


def _softplus(x):
    return jnp.maximum(x, 0.0) + jnp.log(1.0 + jnp.exp(-jnp.abs(x)))


def _dn_kernel(scal_ref, q_ref, k_ref, v_ref, small_ref, z_ref, cw_ref, nw_ref, o_ref,
               qn, kn, vn, bet, gl, mm, nn, qq, oo, dd):
    h = pl.program_id(1)
    seq = q_ref.shape[0]
    c = DN_CHUNK
    n_chunks = seq // c
    d = DN_HEAD_DIM

    row = lax.broadcasted_iota(jnp.int32, (seq, d), 0)

    def conv_silu(x_ref, which):
        x = x_ref[...]
        y = x * cw_ref[which, DN_CONV - 1:DN_CONV, :]
        for j in range(DN_CONV - 1):
            sh = DN_CONV - 1 - j
            y = y + jnp.where(row >= sh, pltpu.roll(x, sh, 0), 0.0) * cw_ref[which, j:j + 1, :]
        return y * jax.nn.sigmoid(y)

    def l2n(t):
        return t * lax.rsqrt(jnp.sum(t * t, axis=-1, keepdims=True) + 1e-6)

    qn[...] = l2n(conv_silu(q_ref, 0)) * (d ** -0.5)
    kn[...] = l2n(conv_silu(k_ref, 1))
    vn[...] = conv_silu(v_ref, 2)
    small = small_ref[...]
    lane = lax.broadcasted_iota(jnp.int32, small.shape, 1)
    beta_in = jnp.sum(jnp.where(lane == SMALL_BETA + h, small, 0.0), axis=-1, keepdims=True)
    a_in = jnp.sum(jnp.where(lane == SMALL_A + h, small, 0.0), axis=-1, keepdims=True)
    bet[...] = jnp.broadcast_to(jax.nn.sigmoid(beta_in), (seq, d))
    gl[...] = jnp.broadcast_to(-jnp.exp(scal_ref[0, h]) * _softplus(a_in + scal_ref[1, h]), (seq, d))

    ri = lax.broadcasted_iota(jnp.int32, (c, c), 0)
    cj = lax.broadcasted_iota(jnp.int32, (c, c), 1)
    incl = ri >= cj
    strict = ri > cj
    tril = incl.astype(jnp.float32)
    eye = (ri == cj).astype(jnp.float32)

    def chunk_prep(i, carry):
        r0 = pl.multiple_of(i * c, c)
        q = qn[pl.ds(r0, c), :]
        k = kn[pl.ds(r0, c), :]
        v = vn[pl.ds(r0, c), :]
        beta = bet[pl.ds(r0, c), :]
        gcb = jnp.dot(tril, gl[pl.ds(r0, c), :], precision=HIGHEST, preferred_element_type=jnp.float32)
        gct = jnp.concatenate([gcb, gcb], axis=0).T
        diff = gcb[:, 0:c] - gct[0:c, 0:c]
        decay = jnp.where(incl, jnp.exp(jnp.where(incl, diff, 0.0)), 0.0)
        kb = k * beta
        kbf = k.astype(jnp.bfloat16)
        a_kk = lax.dot_general(kb.astype(jnp.bfloat16), kbf, (((1,), (1,)), ((), ())),
                               preferred_element_type=jnp.float32)
        neg_l = jnp.where(strict, -(a_kk * decay), 0.0)
        inv = eye + neg_l
        pw = neg_l
        steps = max(1, (c - 1).bit_length())
        for _ in range(steps - 1):
            pw = jnp.dot(pw, pw, precision=HIGHEST, preferred_element_type=jnp.float32)
            inv = inv + jnp.dot(inv, pw, precision=HIGHEST, preferred_element_type=jnp.float32)
        egc = jnp.exp(gcb)
        u = jnp.dot(inv, v * beta, precision=HIGHEST, preferred_element_type=jnp.float32)
        w = jnp.dot(inv, kb * egc, precision=HIGHEST, preferred_element_type=jnp.float32)
        a_qk = lax.dot_general(q.astype(jnp.bfloat16), kbf, (((1,), (1,)), ((), ())),
                               preferred_element_type=jnp.float32) * decay
        g_last = gcb[c - 1:c, :]
        kdec = k * jnp.exp(g_last - gcb)
        kdec_t = jnp.concatenate([kdec, jnp.zeros_like(kdec)], axis=0).T[:, 0:c].astype(jnp.bfloat16)
        wb = w.astype(jnp.bfloat16)
        ub = u.astype(jnp.bfloat16)
        ab = a_qk.astype(jnp.bfloat16)
        m0 = pl.multiple_of(i * d, d)
        mm[pl.ds(m0, d), :] = jnp.dot(kdec_t, wb, preferred_element_type=jnp.float32)
        nn[pl.ds(m0, d), :] = jnp.dot(kdec_t, ub, preferred_element_type=jnp.float32)
        qq[pl.ds(r0, c), :] = q * egc - jnp.dot(ab, wb, preferred_element_type=jnp.float32)
        oo[pl.ds(r0, c), :] = jnp.dot(ab, ub, preferred_element_type=jnp.float32)
        dd[pl.ds(pl.multiple_of(i * 8, 8), 8), :] = jnp.broadcast_to(jnp.exp(g_last), (8, d))
        return carry

    lax.fori_loop(0, n_chunks, chunk_prep, 0)

    nw = nw_ref[...]

    def chunk_scan(i, state):
        r0 = pl.multiple_of(i * c, c)
        m0 = pl.multiple_of(i * d, d)
        sb = state.astype(jnp.bfloat16)
        o = jnp.dot(qq[pl.ds(r0, c), :].astype(jnp.bfloat16), sb, preferred_element_type=jnp.float32) \
            + oo[pl.ds(r0, c), :]
        dec = dd[pl.ds(pl.multiple_of(i * 8, 8), 1), :]
        new_state = state * dec - jnp.dot(mm[pl.ds(m0, d), :].astype(jnp.bfloat16), sb,
                                          preferred_element_type=jnp.float32) + nn[pl.ds(m0, d), :]
        o = o * lax.rsqrt(jnp.mean(o * o, axis=-1, keepdims=True) + 1e-6) * nw
        z = z_ref[pl.ds(r0, c), :]
        o_ref[pl.ds(r0, c), :] = (o * (z * jax.nn.sigmoid(z))).astype(o_ref.dtype)
        return new_state

    lax.fori_loop(0, n_chunks, chunk_scan, jnp.zeros((d, d), jnp.float32))


def _deltanet(pf, conv_w, a_log, dt_bias, norm_w, bsz, seq):
    d = DN_HEAD_DIM
    n_chunks = seq // DN_CHUNK
    qkv0 = PF_QKVB // d
    scal = jnp.stack([a_log, dt_bias]).astype(jnp.float32)
    f32 = jnp.float32
    cw4 = conv_w.astype(f32).reshape(DN_CONV, 3, DN_HEADS, d).transpose(2, 1, 0, 3)
    assert 2 * DN_CHUNK == d
    return pl.pallas_call(
        _dn_kernel,
        grid=(bsz, DN_HEADS),
        in_specs=[
            pl.BlockSpec(memory_space=pltpu.SMEM),
            pl.BlockSpec((seq, d), lambda b, h: (b, qkv0 + h)),
            pl.BlockSpec((seq, d), lambda b, h: (b, qkv0 + DN_HEADS + h)),
            pl.BlockSpec((seq, d), lambda b, h: (b, qkv0 + 2 * DN_HEADS + h)),
            pl.BlockSpec((seq, LANES), lambda b, h: (b, PF_SMALL // LANES)),
            pl.BlockSpec((seq, d), lambda b, h: (b, PF_ZB // d + h)),
            pl.BlockSpec((None, 3, DN_CONV, d), lambda b, h: (h, 0, 0, 0)),
            pl.BlockSpec((1, d), lambda b, h: (0, 0)),
        ],
        out_specs=pl.BlockSpec((seq, d), lambda b, h: (b, h)),
        out_shape=jax.ShapeDtypeStruct((bsz * seq, DN_WIDTH), jnp.bfloat16),
        scratch_shapes=[
            pltpu.VMEM((seq, d), f32), pltpu.VMEM((seq, d), f32), pltpu.VMEM((seq, d), f32),
            pltpu.VMEM((seq, d), f32), pltpu.VMEM((seq, d), f32),
            pltpu.VMEM((n_chunks * d, d), f32), pltpu.VMEM((n_chunks * d, d), f32),
            pltpu.VMEM((seq, d), f32), pltpu.VMEM((seq, d), f32),
            pltpu.VMEM((n_chunks * 8, d), f32),
        ],
        compiler_params=_cparams(2),
    )(scal, pf, pf, pf, pf, pf, cw4, norm_w.astype(f32).reshape(1, d))


def _out_kernel(oa_ref, ob_ref, gma_ref, gmb_ref, x_ref, p_ref, wa_ref, wb_ref, wo_ref, wpg_ref, wp_ref,
                lng_ref, lnb_ref, o_ref):
    f32 = jnp.float32
    y_a = jnp.dot(oa_ref[...], wa_ref[...], preferred_element_type=f32)
    y_b = jnp.dot(ob_ref[...], wb_ref[...], preferred_element_type=f32)
    mix = jax.nn.sigmoid(gma_ref[...]) * y_a + jax.nn.sigmoid(gmb_ref[...]) * y_b
    mixed = jnp.dot(mix.astype(jnp.bfloat16), wo_ref[...], preferred_element_type=f32)
    h = DEEPNORM_ALPHA * x_ref[...] + mixed
    gate = jax.nn.sigmoid(jnp.dot(h.astype(jnp.bfloat16), wpg_ref[...], preferred_element_type=f32))
    h = h + gate * jnp.dot(p_ref[...].astype(jnp.bfloat16), wp_ref[...], preferred_element_type=f32)
    mu = jnp.mean(h, axis=-1, keepdims=True)
    hc = h - mu
    var = jnp.mean(hc * hc, axis=-1, keepdims=True)
    o_ref[...] = (hc * lax.rsqrt(var + 1e-5) * lng_ref[...] + lnb_ref[...]).astype(o_ref.dtype)


def _out_block(o_a, o_b, pf, x2, p2, wa, wb, wo, wpg, wp, ln_g, ln_b, tm):
    t = x2.shape[0]
    bf = jnp.bfloat16

    def full(shape):
        return pl.BlockSpec(shape, lambda i: (0, 0))

    return pl.pallas_call(
        _out_kernel,
        grid=(t // tm,),
        in_specs=[
            pl.BlockSpec((tm, NSA_WIDTH), lambda i: (i, 0)),
            pl.BlockSpec((tm, DN_WIDTH), lambda i: (i, 0)),
            pl.BlockSpec((tm, D_MODEL), lambda i: (i, PF_GM // D_MODEL)),
            pl.BlockSpec((tm, D_MODEL), lambda i: (i, PF_GM // D_MODEL + 1)),
            pl.BlockSpec((tm, D_MODEL), lambda i: (i, 0)),
            pl.BlockSpec((tm, PLE_DIM), lambda i: (i, 0)),
            full((NSA_WIDTH, D_MODEL)), full((DN_WIDTH, D_MODEL)), full((D_MODEL, D_MODEL)),
            full((D_MODEL, D_MODEL)), full((PLE_DIM, D_MODEL)), full((1, D_MODEL)), full((1, D_MODEL)),
        ],
        out_specs=pl.BlockSpec((tm, D_MODEL), lambda i: (i, 0)),
        out_shape=jax.ShapeDtypeStruct((t, D_MODEL), x2.dtype),
        compiler_params=_cparams(1),
    )(o_a, o_b, pf, pf, x2, p2, wa.astype(bf), wb.astype(bf), wo.astype(bf), wpg.astype(bf), wp.astype(bf),
      ln_g.astype(jnp.float32).reshape(1, D_MODEL), ln_b.astype(jnp.float32).reshape(1, D_MODEL))


def _layer(x, p, w_in, pos_k, pos_v, w1_k, w2_k, w1_v, w2_v, bias_tabs, conv_w, a_log, dt_bias, norm_w,
           w_a, w_b, w_o, w_ple, w_pg, ln_g, ln_b):
    bsz, seq, _ = x.shape
    t = bsz * seq
    x2 = x.reshape(t, D_MODEL)
    xb = x2.astype(jnp.bfloat16)
    wb16, wf16 = _prep_w_in(w_in)
    tm = 512 if t % 512 == 0 else seq
    pb = _matmul(xb, wb16, jnp.bfloat16, tm, PB_WIDTH // 2)
    pf = _matmul(xb, wf16, jnp.float32, tm, 1024)

    nc = seq // CMP_STRIDE
    rkv = pf[:, PF_KC:PF_KC + 2 * NSA_KV].reshape(bsz, nc, CMP_STRIDE, 2, NSA_KV)
    rkv = rkv.transpose(0, 3, 1, 2, 4).reshape(bsz, 2, nc, CMP_STRIDE * NSA_KV)
    pos2, w1p, w2p = _prep_compress_weights(pos_k, pos_v, w1_k, w1_v, w2_k, w2_v)
    kcvc = _compress(rkv, pos2, w1p, w2p)

    bias_sel, bias_win, bias_cmp = bias_tabs
    o_a = _nsa(pb, pf, kcvc, bias_sel, bias_win, bias_cmp, bsz, seq)
    o_b = _deltanet(pf, conv_w, a_log, dt_bias, norm_w, bsz, seq)
    out = _out_block(o_a, o_b, pf, x2, p.reshape(t, PLE_DIM), w_a, w_b, w_o, w_pg, w_ple, ln_g, ln_b,
                     256 if t % 256 == 0 else seq)
    return out.reshape(bsz, seq, D_MODEL)


def kernel(x, p, w_in, cmp_pos_k, cmp_pos_v, cmp_w1_k, cmp_w2_k, cmp_w1_v, cmp_w2_v, rel_bias, dn_conv_w,
           dn_a_log, dn_dt_bias, dn_norm_w, w_branch_a, w_branch_b, w_out, w_ple, w_ple_gate, ln_g, ln_b):
    depth = w_in.shape[0]
    bias_tabs = _bias_tables(rel_bias, x.shape[1])
    for i in range(depth):
        x = _layer(x, p[i], w_in[i], cmp_pos_k[i], cmp_pos_v[i], cmp_w1_k[i], cmp_w2_k[i], cmp_w1_v[i],
                   cmp_w2_v[i], bias_tabs, dn_conv_w[i], dn_a_log[i], dn_dt_bias[i], dn_norm_w[i],
                   w_branch_a[i], w_branch_b[i], w_out[i], w_ple[i], w_ple_gate[i], ln_g[i], ln_b[i])
    return x
```

```python
import functools
import math

import numpy as np
import jax
import jax.numpy as jnp
from jax import lax
from jax.experimental import pallas as pl
from jax.experimental.pallas import tpu as pltpu

D_MODEL = 1024
PLE_DIM = 256
NSA_HEADS = 8
NSA_GROUPS = 2
NSA_HPG = NSA_HEADS // NSA_GROUPS
NSA_HEAD_DIM = 64
NSA_WIDTH = NSA_HEADS * NSA_HEAD_DIM
NSA_KV = NSA_GROUPS * NSA_HEAD_DIM
CMP_BLOCK = 32
CMP_STRIDE = 16
CMP_HIDDEN = 256
SEL_BLOCK = 64
SEL_TOPK = 8
WINDOW = 512
DN_HEADS = 4
DN_HEAD_DIM = 128
DN_WIDTH = DN_HEADS * DN_HEAD_DIM
DN_CONV = 4
DN_CHUNK = 64
NUM_BUCKETS = 32
REL_MAX_DIST = 1024
DEEPNORM_ALPHA = 2.0 ** 0.25
NEG = -1e30
FORCE = 1e6

LANES = 128
TILE = 128
SEL_LANE0 = 64
VMEM_LIMIT = 56 * 1024 * 1024
DN_INV_BLOCK = 16
DN_PREP_UNROLL = 8

HIGHEST = lax.Precision.HIGHEST

PB_QP = 0
PB_KS = PB_QP + NSA_HEADS * LANES
PB_KW = PB_KS + NSA_GROUPS * LANES
PB_VS = PB_KW + NSA_GROUPS * LANES
PB_VW = PB_VS + LANES
PB_WIDTH = PB_VW + LANES
PF_KC = 0
PF_VC = 128
PF_SMALL = 256
PF_ZA = 512
PF_ZB = 1024
PF_QKVB = 1536
PF_GM = 3072
PF_WIDTH = 5120
SMALL_BETA = 3 * NSA_HEADS
SMALL_A = SMALL_BETA + DN_HEADS


def _bucket_thresholds():
    max_exact = NUM_BUCKETS // 2
    span = NUM_BUCKETS - max_exact
    ratio = REL_MAX_DIST // max_exact
    thr = list(range(1, max_exact + 1))
    for k in range(1, span):
        n = max_exact
        while n ** span < max_exact ** span * ratio ** k:
            n += 1
        thr.append(n)
    return tuple(thr)


_THR = _bucket_thresholds()


def _cparams(n_axes):
    return pltpu.CompilerParams(dimension_semantics=("arbitrary",) * n_axes, vmem_limit_bytes=VMEM_LIMIT)


def _mm_kernel(x_ref, w_ref, o_ref):
    o_ref[...] = jnp.dot(x_ref[...], w_ref[...], preferred_element_type=jnp.float32).astype(o_ref.dtype)


def _matmul(x, w, out_dtype, tm, tn, name):
    m, k = x.shape
    n = w.shape[1]
    return pl.pallas_call(
        _mm_kernel,
        grid=(n // tn, m // tm),
        in_specs=[pl.BlockSpec((tm, k), lambda j, i: (i, 0)),
                  pl.BlockSpec((k, tn), lambda j, i: (0, j))],
        out_specs=pl.BlockSpec((tm, tn), lambda j, i: (i, j)),
        out_shape=jax.ShapeDtypeStruct((m, n), out_dtype),
        compiler_params=_cparams(2), name=name,
    )(x, w)


def _prep_w_in(w):
    d = w.shape[0]
    o = 0
    wq = w[:, o:o + NSA_WIDTH]; o += NSA_WIDTH
    wkv = w[:, o:o + 6 * NSA_KV]; o += 6 * NSA_KV
    wg = w[:, o:o + 3 * NSA_HEADS]; o += 3 * NSA_HEADS
    wza = w[:, o:o + NSA_WIDTH]; o += NSA_WIDTH
    wqkvb = w[:, o:o + 3 * DN_WIDTH]; o += 3 * DN_WIDTH
    wbeta = w[:, o:o + DN_HEADS]; o += DN_HEADS
    wa = w[:, o:o + DN_HEADS]; o += DN_HEADS
    wzb = w[:, o:o + DN_WIDTH]; o += DN_WIDTH
    wgm = w[:, o:o + 2 * D_MODEL]
    wkc, wvc, wks, wvs, wkw, wvw = [wkv[:, i * NSA_KV:(i + 1) * NSA_KV] for i in range(6)]

    def pad_heads(t, nh):
        t = t.reshape(d, nh, NSA_HEAD_DIM)
        return jnp.concatenate([t, jnp.zeros_like(t)], axis=-1).reshape(d, nh * LANES)

    wb = jnp.concatenate([pad_heads(wq * NSA_HEAD_DIM ** -0.5, NSA_HEADS), pad_heads(wks, NSA_GROUPS),
                          pad_heads(wkw, NSA_GROUPS), wvs, wvw], axis=1).astype(jnp.bfloat16)
    small = jnp.concatenate([wg, wbeta, wa, jnp.zeros((d, LANES - SMALL_A - DN_HEADS), w.dtype)], axis=1)
    wf = jnp.concatenate([wkc, wvc, small, jnp.zeros((d, LANES), w.dtype), wza, wzb, wqkvb, wgm],
                         axis=1).astype(jnp.bfloat16)
    return wb, wf


def _bias_kernel(tab_ref, sel_ref, win_ref, cmp_ref, *, n_cmp):
    h = pl.program_id(0)

    def lookup(n):
        val = jnp.full(n.shape, tab_ref[0, h], jnp.float32)
        for b in range(1, NUM_BUCKETS):
            val = jnp.where(n >= _THR[b - 1], tab_ref[b, h], val)
        return val

    kj = lax.broadcasted_iota(jnp.int32, (TILE, TILE), 0)
    qi = lax.broadcasted_iota(jnp.int32, (TILE, TILE), 1)
    for dt in range(sel_ref.shape[1]):
        dist = dt * TILE + qi - kj
        v = lookup(jnp.maximum(dist, 0))
        sel_ref[0, dt] = jnp.where(dist >= 0, v, NEG)
        if dt < win_ref.shape[1]:
            win_ref[0, dt] = jnp.where((dist >= 0) & (dist < WINDOW), v, NEG)
    c = lax.broadcasted_iota(jnp.int32, cmp_ref.shape[1:], 0)
    s = lax.broadcasted_iota(jnp.int32, cmp_ref.shape[1:], 1)
    dist = s - (c * CMP_STRIDE + CMP_BLOCK - 1)
    cmp_ref[0] = jnp.where((dist >= 0) & (c < n_cmp), lookup(jnp.maximum(dist, 0)), NEG)


def _bias_tables(rel_bias, seq):
    nq = seq // TILE
    nwin = min(WINDOW // TILE + 1, nq)
    nc = seq // CMP_STRIDE
    n_cmp = nc - CMP_BLOCK // CMP_STRIDE + 1
    return pl.pallas_call(
        functools.partial(_bias_kernel, n_cmp=n_cmp),
        grid=(NSA_HEADS,),
        in_specs=[pl.BlockSpec(memory_space=pltpu.SMEM)],
        out_specs=[pl.BlockSpec((1, nq, TILE, TILE), lambda h: (h, 0, 0, 0)),
                   pl.BlockSpec((1, nwin, TILE, TILE), lambda h: (h, 0, 0, 0)),
                   pl.BlockSpec((1, nc, seq), lambda h: (h, 0, 0))],
        out_shape=[jax.ShapeDtypeStruct((NSA_HEADS, nq, TILE, TILE), jnp.float32),
                   jax.ShapeDtypeStruct((NSA_HEADS, nwin, TILE, TILE), jnp.float32),
                   jax.ShapeDtypeStruct((NSA_HEADS, nc, seq), jnp.float32)],
        compiler_params=_cparams(1), name="bias_tables",
    )(rel_bias.astype(jnp.float32))


def _gelu_tanh(x):
    return x * (0.5 * (1.0 + jnp.tanh(math.sqrt(2.0 / math.pi) * (x + 0.044715 * (x * x * x)))))


def _compress_kernel(r_ref, pos_ref, w1_ref, w2_ref, o_ref):
    r = r_ref[0, 0]
    nc = r.shape[0]
    a = jnp.dot((r + pos_ref[0, 0:1, :]).astype(jnp.bfloat16), w1_ref[0, 0], preferred_element_type=jnp.float32)
    b = jnp.dot((r + pos_ref[0, 1:2, :]).astype(jnp.bfloat16), w1_ref[0, 1], preferred_element_type=jnp.float32)
    hid = a + pltpu.roll(b, nc - 1, 0)
    o_ref[0, 0] = jnp.dot(_gelu_tanh(hid).astype(jnp.bfloat16), w2_ref[0], preferred_element_type=jnp.float32)


def _compress(rkv, pos2, w1p, w2p):
    bsz, _, nc, width = rkv.shape
    hid = w1p.shape[-1]
    return pl.pallas_call(
        _compress_kernel,
        grid=(2, bsz),
        in_specs=[pl.BlockSpec((1, 1, nc, width), lambda k, b: (b, k, 0, 0)),
                  pl.BlockSpec((1, 2, width), lambda k, b: (k, 0, 0)),
                  pl.BlockSpec((1, 2, width, hid), lambda k, b: (k, 0, 0, 0)),
                  pl.BlockSpec((1, hid, LANES), lambda k, b: (k, 0, 0))],
        out_specs=pl.BlockSpec((1, 1, nc, LANES), lambda k, b: (b, k, 0, 0)),
        out_shape=jax.ShapeDtypeStruct((bsz, 2, nc, LANES), jnp.float32),
        compiler_params=_cparams(2), name="compress",
    )(rkv, pos2, w1p, w2p)


def _prep_compress_weights(pos_k, pos_v, w1_k, w1_v, w2_k, w2_v):
    eye = jnp.eye(NSA_GROUPS, dtype=jnp.float32)
    half = CMP_BLOCK // 2

    def w1_both(w1):
        w = w1.reshape(2, half, NSA_HEAD_DIM, CMP_HIDDEN)
        w = jnp.einsum('aldj,gh->algdhj', w, eye)
        return w.reshape(2, half * NSA_GROUPS * NSA_HEAD_DIM, NSA_GROUPS * CMP_HIDDEN)

    def w2_both(w2):
        return jnp.einsum('jd,gh->gjhd', w2, eye).reshape(NSA_GROUPS * CMP_HIDDEN, NSA_GROUPS * NSA_HEAD_DIM)

    def pos_both(pos):
        p = pos.reshape(2, half, 1, NSA_HEAD_DIM)
        return jnp.broadcast_to(p, (2, half, NSA_GROUPS, NSA_HEAD_DIM)).reshape(2, half * NSA_KV)

    pos2 = jnp.stack([pos_both(pos_k), pos_both(pos_v)]).astype(jnp.float32)
    w1p = jnp.stack([w1_both(w1_k), w1_both(w1_v)]).astype(jnp.bfloat16)
    w2p = jnp.stack([w2_both(w2_k), w2_both(w2_v)]).astype(jnp.bfloat16)
    return pos2, w1p, w2p


def _pair_lanes(x, g):
    sw = pltpu.roll(x, LANES // 2, 1)
    lane = lax.broadcasted_iota(jnp.int32, x.shape, 1)
    own = (lane < LANES // 2) == (g == 0)
    return jnp.where(own, x, sw)


def _group_rows_t(x, g):
    xt = x.T
    half = LANES // 2
    return jnp.where(g == 0, xt[0:half, :], xt[half:LANES, :])


def _nsa_kernel(q_ref, ks_ref, kw_ref, vs_ref, vw_ref, kcvc_ref, bc_ref, bsel_ref, bwin_ref,
                small_ref, za_ref, o_ref,
                ksa, vst, vwt, kcs, vct, qa, qp, m_s, l_s, acc_s, *, n_cmp, n_sel, n_top):
    g = pl.program_id(1)
    qt = pl.program_id(2)
    tq = TILE
    dh = NSA_HEAD_DIM
    seq = ks_ref.shape[0]
    nc = kcvc_ref.shape[2]
    f32 = jnp.float32
    bf16 = jnp.bfloat16
    nt_dims = (((1,), (1,)), ((), ()))

    @pl.when(qt == 0)
    def _():
        row = lax.broadcasted_iota(jnp.int32, (seq, LANES), 0)
        lane = lax.broadcasted_iota(jnp.int32, (seq, LANES), 1)
        onehot = (lane - SEL_LANE0) == (row >> 6)
        ksa[...] = jnp.where(onehot, 1.0, ks_ref[...].astype(f32)).astype(bf16)
        for kt in range(seq // TILE):
            rows = slice(kt * TILE, (kt + 1) * TILE)
            vst[kt] = _group_rows_t(vs_ref[rows, :].astype(f32), g).astype(bf16)
            vwt[kt] = _group_rows_t(vw_ref[rows, :].astype(f32), g).astype(bf16)
        kcs[...] = _pair_lanes(kcvc_ref[0, 0], g).astype(bf16)
        for ct in range(nc // TILE):
            rows = slice(ct * TILE, (ct + 1) * TILE)
            vct[:, rows] = _group_rows_t(kcvc_ref[0, 1, rows, :], g).astype(bf16)

    qh = [q_ref[:, hh * LANES:(hh + 1) * LANES] for hh in range(NSA_HPG)]
    kc = kcs[...]
    vc_t = vct[...]
    o_cmp = []
    psum = jnp.zeros((nc, tq), f32)
    for hh in range(NSA_HPG):
        bias = bc_ref[hh]
        valid = bias > 0.5 * NEG
        s = lax.dot_general(kc, qh[hh], nt_dims, preferred_element_type=f32) + bias
        m = jnp.max(s, axis=0, keepdims=True)
        e = jnp.where(valid, jnp.exp(s - m), 0.0)
        den = jnp.maximum(jnp.sum(e, axis=0, keepdims=True), 1e-30)
        p = e * (1.0 / den)
        psum = psum + p
        o_cmp.append(jnp.dot(vc_t, p.astype(bf16), preferred_element_type=f32))

    sj = lax.broadcasted_iota(jnp.int32, (n_sel, nc), 0)
    ci = lax.broadcasted_iota(jnp.int32, (n_sel, nc), 1)
    overlap = ((ci * CMP_STRIDE < (sj + 1) * SEL_BLOCK) & (ci * CMP_STRIDE + CMP_BLOCK > sj * SEL_BLOCK)
               & (ci < n_cmp)).astype(bf16)
    imp_t = sum(jnp.dot(overlap, part, preferred_element_type=f32)
                for part in _split_bf16(psum, 3))
    blk = lax.broadcasted_iota(jnp.int32, (n_sel, tq), 0)
    cur = (qt * tq + lax.broadcasted_iota(jnp.int32, (n_sel, tq), 1)) >> 6
    forced = (blk == 0) | (blk == cur) | (blk == cur - 1)
    imp_t = jnp.where(forced, FORCE, jnp.where(blk > cur, -FORCE, imp_t))
    rank = jnp.zeros((n_sel, tq), jnp.int32)
    for j in range(n_sel):
        other = imp_t[j:j + 1, :]
        ahead = (other > imp_t) | ((other == imp_t) & (blk > j))
        rank = rank + ahead.astype(jnp.int32)
    selb = jnp.where(rank < n_top, 0.0, NEG)
    pieces = [jnp.zeros((SEL_LANE0, tq), jnp.float32), selb]
    if LANES - SEL_LANE0 - n_sel > 0:
        pieces.append(jnp.zeros((LANES - SEL_LANE0 - n_sel, tq), jnp.float32))
    selb_r = jnp.concatenate(pieces, axis=0).T.astype(jnp.bfloat16)

    for hh in range(NSA_HPG):
        qa[hh * tq:(hh + 1) * tq, :] = qh[hh] + selb_r
        qp[hh * tq:(hh + 1) * tq, :] = qh[hh]

    def flash(q_tiles, k_tiles, v_tiles, bias_ref, n_steps, kt_of):
        m_s[...] = jnp.full(m_s.shape, NEG, f32)
        l_s[...] = jnp.zeros(l_s.shape, f32)
        acc_s[...] = jnp.zeros(acc_s.shape, f32)

        def body(step, carry):
            kt = kt_of(step)
            dt = qt - kt
            k = k_tiles[pl.ds(pl.multiple_of(kt * TILE, TILE), TILE), :]
            s = lax.dot_general(k, q_tiles[...], nt_dims, preferred_element_type=f32)
            s = s + jnp.concatenate([bias_ref[hh, dt] for hh in range(NSA_HPG)], axis=1)
            m_old = m_s[...]
            m_new = jnp.maximum(m_old, jnp.max(s, axis=0, keepdims=True))
            alpha = jnp.exp(m_old - m_new)
            p = jnp.exp(s - m_new)
            l_s[...] = alpha * l_s[...] + jnp.sum(p, axis=0, keepdims=True)
            acc_s[...] = alpha * acc_s[...] + jnp.dot(v_tiles[kt], p.astype(bf16), preferred_element_type=f32)
            m_s[...] = m_new
            return carry

        lax.fori_loop(0, n_steps, body, 0)
        return acc_s[...] * (1.0 / l_s[...])

    o_sel = flash(qa, ksa, vst, bsel_ref, qt + 1, lambda st: st)
    n_win = jnp.minimum(qt, bwin_ref.shape[1] - 1) + 1
    o_win = flash(qp, kw_ref, vwt, bwin_ref, n_win, lambda st: qt - st)

    gates_t = jax.nn.sigmoid(small_ref[...]).T
    heads = []
    for hh in range(NSA_HPG):
        def gate(br):
            r = br * NSA_HEADS + hh
            return jnp.where(g == 0, gates_t[r:r + 1, :], gates_t[r + NSA_HPG:r + NSA_HPG + 1, :])
        cols = slice(hh * tq, (hh + 1) * tq)
        heads.append(gate(0) * o_cmp[hh] + gate(1) * o_sel[:, cols] + gate(2) * o_win[:, cols])
    o = jnp.concatenate([jnp.concatenate(heads[2 * j:2 * j + 2], axis=0).T for j in range(NSA_HPG // 2)], axis=1)
    z = za_ref[...]
    o_ref[...] = (o * (z * jax.nn.sigmoid(z))).astype(o_ref.dtype)


def _nsa(pb, pf, kcvc, bias_sel, bias_win, bias_cmp, bsz, seq):
    nq = seq // TILE
    nc = seq // CMP_STRIDE
    n_cmp = nc - CMP_BLOCK // CMP_STRIDE + 1
    n_sel = seq // SEL_BLOCK
    n_top = min(SEL_TOPK, n_sel)
    nwin = bias_win.shape[1]
    gw = NSA_HPG * NSA_HEAD_DIM
    kern = functools.partial(_nsa_kernel, n_cmp=n_cmp, n_sel=n_sel, n_top=n_top)
    return pl.pallas_call(
        kern,
        grid=(bsz, NSA_GROUPS, nq),
        in_specs=[
            pl.BlockSpec((TILE, NSA_HPG * LANES), lambda b, g, t: (b * nq + t, g)),
            pl.BlockSpec((seq, LANES), lambda b, g, t: (b, PB_KS // LANES + g)),
            pl.BlockSpec((seq, LANES), lambda b, g, t: (b, PB_KW // LANES + g)),
            pl.BlockSpec((seq, LANES), lambda b, g, t: (b, PB_VS // LANES)),
            pl.BlockSpec((seq, LANES), lambda b, g, t: (b, PB_VW // LANES)),
            pl.BlockSpec((1, 2, nc, LANES), lambda b, g, t: (b, 0, 0, 0)),
            pl.BlockSpec((NSA_HPG, nc, TILE), lambda b, g, t: (g, 0, t)),
            pl.BlockSpec((NSA_HPG, nq, TILE, TILE), lambda b, g, t: (g, 0, 0, 0)),
            pl.BlockSpec((NSA_HPG, nwin, TILE, TILE), lambda b, g, t: (g, 0, 0, 0)),
            pl.BlockSpec((TILE, LANES), lambda b, g, t: (b * nq + t, PF_SMALL // LANES)),
            pl.BlockSpec((TILE, gw), lambda b, g, t: (b * nq + t, PF_ZA // gw + g)),
        ],
        out_specs=pl.BlockSpec((TILE, gw), lambda b, g, t: (b * nq + t, g)),
        out_shape=jax.ShapeDtypeStruct((bsz * seq, NSA_WIDTH), jnp.bfloat16),
        scratch_shapes=[
            pltpu.VMEM((seq, LANES), jnp.bfloat16),
            pltpu.VMEM((nq, NSA_HEAD_DIM, TILE), jnp.bfloat16),
            pltpu.VMEM((nq, NSA_HEAD_DIM, TILE), jnp.bfloat16),
            pltpu.VMEM((nc, LANES), jnp.bfloat16),
            pltpu.VMEM((NSA_HEAD_DIM, nc), jnp.bfloat16),
            pltpu.VMEM((NSA_HPG * TILE, LANES), jnp.bfloat16),
            pltpu.VMEM((NSA_HPG * TILE, LANES), jnp.bfloat16),
            pltpu.VMEM((1, NSA_HPG * TILE), jnp.float32),
            pltpu.VMEM((1, NSA_HPG * TILE), jnp.float32),
            pltpu.VMEM((NSA_HEAD_DIM, NSA_HPG * TILE), jnp.float32),
        ],
        compiler_params=_cparams(3), name="nsa",
    )(pb, pb, pb, pb, pb, kcvc, bias_cmp, bias_sel, bias_win, pf, pf)


def _split_bf16(a, n):
    parts = []
    for _ in range(n - 1):
        hi = a.astype(jnp.bfloat16)
        parts.append(hi)
        a = a - hi.astype(jnp.float32)
    parts.append(a.astype(jnp.bfloat16))
    return parts


def _dot3(a, b):
    ah, al = _split_bf16(a, 2)
    bh, bl = _split_bf16(b, 2)
    f32 = jnp.float32
    return (jnp.dot(ah, bh, preferred_element_type=f32) + jnp.dot(al, bh, preferred_element_type=f32)
            + jnp.dot(ah, bl, preferred_element_type=f32))


def _softplus(x):
    return jnp.maximum(x, 0.0) + jnp.log(1.0 + jnp.exp(-jnp.abs(x)))


def _dn_kernel(scal_ref, q_ref, k_ref, v_ref, small_ref, z_ref, cw_ref, nw_ref, o_ref,
               qn, kn, vn, bet, gl, mm, nn, qq, oo, dd):
    h = pl.program_id(1)
    seq = q_ref.shape[0]
    c = DN_CHUNK
    n_chunks = seq // c
    d = DN_HEAD_DIM

    row = lax.broadcasted_iota(jnp.int32, (seq, d), 0)

    def conv_silu(x_ref, which):
        x = x_ref[...]
        y = x * cw_ref[which, DN_CONV - 1:DN_CONV, :]
        for j in range(DN_CONV - 1):
            sh = DN_CONV - 1 - j
            y = y + jnp.where(row >= sh, pltpu.roll(x, sh, 0), 0.0) * cw_ref[which, j:j + 1, :]
        return y * jax.nn.sigmoid(y)

    def l2n(t):
        return t * lax.rsqrt(jnp.sum(t * t, axis=-1, keepdims=True) + 1e-6)

    qn[...] = l2n(conv_silu(q_ref, 0)) * (d ** -0.5)
    kn[...] = l2n(conv_silu(k_ref, 1))
    vn[...] = conv_silu(v_ref, 2)
    small = small_ref[...]
    lane = lax.broadcasted_iota(jnp.int32, small.shape, 1)
    beta_in = jnp.sum(jnp.where(lane == SMALL_BETA + h, small, 0.0), axis=-1, keepdims=True)
    a_in = jnp.sum(jnp.where(lane == SMALL_A + h, small, 0.0), axis=-1, keepdims=True)
    bet[...] = jnp.broadcast_to(jax.nn.sigmoid(beta_in), (seq, d))
    gl[...] = jnp.broadcast_to(-jnp.exp(scal_ref[0, h]) * _softplus(a_in + scal_ref[1, h]), (seq, d))

    ri = lax.broadcasted_iota(jnp.int32, (c, c), 0)
    cj = lax.broadcasted_iota(jnp.int32, (c, c), 1)
    incl = ri >= cj
    strict = ri > cj
    tril = incl.astype(jnp.float32)
    eye = (ri == cj).astype(jnp.float32)

    tril_b = tril.astype(jnp.bfloat16)
    bs = DN_INV_BLOCK
    sh = bs.bit_length() - 1
    same_diag = (ri >> sh) == (cj >> sh)
    level_masks = []
    while (1 << sh) < c:
        level_masks.append(((ri >> (sh + 1)) == (cj >> (sh + 1))) & ((ri >> sh) > (cj >> sh)))
        sh += 1

    bf16 = jnp.bfloat16
    nt_dims = (((1,), (1,)), ((), ()))

    def dot1(a, b):
        return jnp.dot(a.astype(bf16), b.astype(bf16), preferred_element_type=jnp.float32)

    def chunk_prep(it, carry):
        ids = [it * DN_PREP_UNROLL + cc for cc in range(DN_PREP_UNROLL)]
        rows = [pl.ds(pl.multiple_of(i * c, c), c) for i in ids]
        ks = [kn[r, :] for r in rows]
        betas = [bet[r, :] for r in rows]
        gcbs = [sum(jnp.dot(tril_b, part, preferred_element_type=jnp.float32)
                    for part in _split_bf16(gl[r, :], 3)) for r in rows]
        kbs = [k * beta for k, beta in zip(ks, betas)]
        kbfs = [k.astype(bf16) for k in ks]
        a_kks = [lax.dot_general(kb.astype(bf16), kbf, nt_dims, preferred_element_type=jnp.float32)
                 for kb, kbf in zip(kbs, kbfs)]
        decays = []
        for gcb in gcbs:
            gct = jnp.concatenate([gcb, gcb], axis=0).T
            diff = gcb[:, 0:c] - gct[0:c, 0:c]
            decays.append(jnp.where(incl, jnp.exp(jnp.where(incl, diff, 0.0)), 0.0))
        lows = [jnp.where(strict, a * dec, 0.0) for a, dec in zip(a_kks, decays)]
        pws = [jnp.where(same_diag, -low, 0.0) for low in lows]
        es = list(pws)
        for _ in range(max(1, (bs - 1).bit_length()) - 1):
            pws = [dot1(pw, pw) for pw in pws]
            es = [e + pw + dot1(e, pw) for e, pw in zip(es, pws)]
        for below in level_masks:
            offs = [jnp.where(below, low, 0.0) for low in lows]
            xs = [off + dot1(e, off) for e, off in zip(es, offs)]
            es = [e - (x + dot1(x, e)) for e, x in zip(es, xs)]
        egcs = [jnp.exp(gcb) for gcb in gcbs]
        rhss = [jnp.concatenate([vn[r, :] * beta, kb * egc], axis=1)
                for r, beta, kb, egc in zip(rows, betas, kbs, egcs)]
        uws = [rhs + _dot3(e, rhs) for e, rhs in zip(es, rhss)]
        qs = [qn[r, :] for r in rows]
        a_qks = [lax.dot_general(q.astype(bf16), kbf, nt_dims, preferred_element_type=jnp.float32) * dec
                 for q, kbf, dec in zip(qs, kbfs, decays)]
        g_lasts = [gcb[c - 1:c, :] for gcb in gcbs]
        kdec_ts = []
        for k, gcb, g_last in zip(ks, gcbs, g_lasts):
            kdec = k * jnp.exp(g_last - gcb)
            kdec_ts.append(jnp.concatenate([kdec, jnp.zeros_like(kdec)], axis=0).T[:, 0:c].astype(bf16))
        uwbs = [uw.astype(bf16) for uw in uws]
        nms = [jnp.dot(kt, uwb, preferred_element_type=jnp.float32) for kt, uwb in zip(kdec_ts, uwbs)]
        oqs = [jnp.dot(a.astype(bf16), uwb, preferred_element_type=jnp.float32) for a, uwb in zip(a_qks, uwbs)]
        for i, r, nm, oq, q, egc, g_last in zip(ids, rows, nms, oqs, qs, egcs, g_lasts):
            m0 = pl.ds(pl.multiple_of(i * d, d), d)
            nn[m0, :] = nm[:, 0:d]
            mm[m0, :] = nm[:, d:2 * d]
            oo[r, :] = oq[:, 0:d]
            qq[r, :] = q * egc - oq[:, d:2 * d]
            dd[pl.ds(pl.multiple_of(i * 8, 8), 8), :] = jnp.broadcast_to(jnp.exp(g_last), (8, d))
        return carry

    lax.fori_loop(0, n_chunks // DN_PREP_UNROLL, chunk_prep, 0)

    nw = nw_ref[...]

    def chunk_scan(i, state):
        r0 = pl.multiple_of(i * c, c)
        m0 = pl.multiple_of(i * d, d)
        sb = state.astype(jnp.bfloat16)
        o = jnp.dot(qq[pl.ds(r0, c), :].astype(jnp.bfloat16), sb, preferred_element_type=jnp.float32) \
            + oo[pl.ds(r0, c), :]
        dec = dd[pl.ds(pl.multiple_of(i * 8, 8), 1), :]
        new_state = state * dec - jnp.dot(mm[pl.ds(m0, d), :].astype(jnp.bfloat16), sb,
                                          preferred_element_type=jnp.float32) + nn[pl.ds(m0, d), :]
        o = o * lax.rsqrt(jnp.mean(o * o, axis=-1, keepdims=True) + 1e-6) * nw
        z = z_ref[pl.ds(r0, c), :]
        o_ref[pl.ds(r0, c), :] = (o * (z * jax.nn.sigmoid(z))).astype(o_ref.dtype)
        return new_state

    lax.fori_loop(0, n_chunks, chunk_scan, jnp.zeros((d, d), jnp.float32))


def _deltanet(pf, conv_w, a_log, dt_bias, norm_w, bsz, seq):
    d = DN_HEAD_DIM
    n_chunks = seq // DN_CHUNK
    qkv0 = PF_QKVB // d
    scal = jnp.stack([a_log, dt_bias]).astype(jnp.float32)
    f32 = jnp.float32
    cw4 = conv_w.astype(f32).reshape(DN_CONV, 3, DN_HEADS, d).transpose(2, 1, 0, 3)
    assert 2 * DN_CHUNK == d
    return pl.pallas_call(
        _dn_kernel,
        grid=(bsz, DN_HEADS),
        in_specs=[
            pl.BlockSpec(memory_space=pltpu.SMEM),
            pl.BlockSpec((seq, d), lambda b, h: (b, qkv0 + h)),
            pl.BlockSpec((seq, d), lambda b, h: (b, qkv0 + DN_HEADS + h)),
            pl.BlockSpec((seq, d), lambda b, h: (b, qkv0 + 2 * DN_HEADS + h)),
            pl.BlockSpec((seq, LANES), lambda b, h: (b, PF_SMALL // LANES)),
            pl.BlockSpec((seq, d), lambda b, h: (b, PF_ZB // d + h)),
            pl.BlockSpec((None, 3, DN_CONV, d), lambda b, h: (h, 0, 0, 0)),
            pl.BlockSpec((1, d), lambda b, h: (0, 0)),
        ],
        out_specs=pl.BlockSpec((seq, d), lambda b, h: (b, h)),
        out_shape=jax.ShapeDtypeStruct((bsz * seq, DN_WIDTH), jnp.bfloat16),
        scratch_shapes=[
            pltpu.VMEM((seq, d), f32), pltpu.VMEM((seq, d), f32), pltpu.VMEM((seq, d), f32),
            pltpu.VMEM((seq, d), f32), pltpu.VMEM((seq, d), f32),
            pltpu.VMEM((n_chunks * d, d), f32), pltpu.VMEM((n_chunks * d, d), f32),
            pltpu.VMEM((seq, d), f32), pltpu.VMEM((seq, d), f32),
            pltpu.VMEM((n_chunks * 8, d), f32),
        ],
        compiler_params=_cparams(2), name="deltanet",
    )(scal, pf, pf, pf, pf, pf, cw4, norm_w.astype(f32).reshape(1, d))


def _out_kernel(oa_ref, ob_ref, gma_ref, gmb_ref, x_ref, p_ref, wa_ref, wb_ref, wo_ref, wpg_ref, wp_ref,
                lng_ref, lnb_ref, o_ref):
    f32 = jnp.float32
    y_a = jnp.dot(oa_ref[...], wa_ref[...], preferred_element_type=f32)
    y_b = jnp.dot(ob_ref[...], wb_ref[...], preferred_element_type=f32)
    mix = jax.nn.sigmoid(gma_ref[...]) * y_a + jax.nn.sigmoid(gmb_ref[...]) * y_b
    mixed = jnp.dot(mix.astype(jnp.bfloat16), wo_ref[...], preferred_element_type=f32)
    h = DEEPNORM_ALPHA * x_ref[...] + mixed
    gate = jax.nn.sigmoid(jnp.dot(h.astype(jnp.bfloat16), wpg_ref[...], preferred_element_type=f32))
    h = h + gate * jnp.dot(p_ref[...].astype(jnp.bfloat16), wp_ref[...], preferred_element_type=f32)
    mu = jnp.mean(h, axis=-1, keepdims=True)
    hc = h - mu
    var = jnp.mean(hc * hc, axis=-1, keepdims=True)
    o_ref[...] = (hc * lax.rsqrt(var + 1e-5) * lng_ref[...] + lnb_ref[...]).astype(o_ref.dtype)


def _out_block(o_a, o_b, pf, x2, p2, wa, wb, wo, wpg, wp, ln_g, ln_b, tm):
    t = x2.shape[0]
    bf = jnp.bfloat16

    def full(shape):
        return pl.BlockSpec(shape, lambda i: (0, 0))

    return pl.pallas_call(
        _out_kernel,
        grid=(t // tm,),
        in_specs=[
            pl.BlockSpec((tm, NSA_WIDTH), lambda i: (i, 0)),
            pl.BlockSpec((tm, DN_WIDTH), lambda i: (i, 0)),
            pl.BlockSpec((tm, D_MODEL), lambda i: (i, PF_GM // D_MODEL)),
            pl.BlockSpec((tm, D_MODEL), lambda i: (i, PF_GM // D_MODEL + 1)),
            pl.BlockSpec((tm, D_MODEL), lambda i: (i, 0)),
            pl.BlockSpec((tm, PLE_DIM), lambda i: (i, 0)),
            full((NSA_WIDTH, D_MODEL)), full((DN_WIDTH, D_MODEL)), full((D_MODEL, D_MODEL)),
            full((D_MODEL, D_MODEL)), full((PLE_DIM, D_MODEL)), full((1, D_MODEL)), full((1, D_MODEL)),
        ],
        out_specs=pl.BlockSpec((tm, D_MODEL), lambda i: (i, 0)),
        out_shape=jax.ShapeDtypeStruct((t, D_MODEL), x2.dtype),
        compiler_params=_cparams(1), name="out_block",
    )(o_a, o_b, pf, pf, x2, p2, wa.astype(bf), wb.astype(bf), wo.astype(bf), wpg.astype(bf), wp.astype(bf),
      ln_g.astype(jnp.float32).reshape(1, D_MODEL), ln_b.astype(jnp.float32).reshape(1, D_MODEL))


def _layer(x, p, w_in, pos_k, pos_v, w1_k, w2_k, w1_v, w2_v, bias_tabs, conv_w, a_log, dt_bias, norm_w,
           w_a, w_b, w_o, w_ple, w_pg, ln_g, ln_b):
    bsz, seq, _ = x.shape
    t = bsz * seq
    x2 = x.reshape(t, D_MODEL)
    xb = x2.astype(jnp.bfloat16)
    wb16, wf16 = _prep_w_in(w_in)
    tm = 512 if t % 512 == 0 else seq
    pb = _matmul(xb, wb16, jnp.bfloat16, tm, PB_WIDTH // 2, "proj_bf16")
    pf = _matmul(xb, wf16, jnp.float32, tm, 1024, "proj_f32")

    nc = seq // CMP_STRIDE
    rkv = pf[:, PF_KC:PF_KC + 2 * NSA_KV].reshape(bsz, nc, CMP_STRIDE, 2, NSA_KV)
    rkv = rkv.transpose(0, 3, 1, 2, 4).reshape(bsz, 2, nc, CMP_STRIDE * NSA_KV)
    pos2, w1p, w2p = _prep_compress_weights(pos_k, pos_v, w1_k, w1_v, w2_k, w2_v)
    kcvc = _compress(rkv, pos2, w1p, w2p)

    bias_sel, bias_win, bias_cmp = bias_tabs
    o_a = _nsa(pb, pf, kcvc, bias_sel, bias_win, bias_cmp, bsz, seq)
    o_b = _deltanet(pf, conv_w, a_log, dt_bias, norm_w, bsz, seq)
    out = _out_block(o_a, o_b, pf, x2, p.reshape(t, PLE_DIM), w_a, w_b, w_o, w_pg, w_ple, ln_g, ln_b,
                     256 if t % 256 == 0 else seq)
    return out.reshape(bsz, seq, D_MODEL)


def kernel(x, p, w_in, cmp_pos_k, cmp_pos_v, cmp_w1_k, cmp_w2_k, cmp_w1_v, cmp_w2_v, rel_bias, dn_conv_w,
           dn_a_log, dn_dt_bias, dn_norm_w, w_branch_a, w_branch_b, w_out, w_ple, w_ple_gate, ln_g, ln_b):
    depth = w_in.shape[0]
    bias_tabs = _bias_tables(rel_bias, x.shape[1])
    for i in range(depth):
        x = _layer(x, p[i], w_in[i], cmp_pos_k[i], cmp_pos_v[i], cmp_w1_k[i], cmp_w2_k[i], cmp_w1_v[i],
                   cmp_w2_v[i], bias_tabs, dn_conv_w[i], dn_a_log[i], dn_dt_bias[i], dn_norm_w[i],
                   w_branch_a[i], w_branch_b[i], w_out[i], w_ple[i], w_ple_gate[i], ln_g[i], ln_b[i])
    return x
```

```python
import functools
import math

import numpy as np
import jax
import jax.numpy as jnp
from jax import lax
from jax.experimental import pallas as pl
from jax.experimental.pallas import tpu as pltpu

D_MODEL = 1024
PLE_DIM = 256
NSA_HEADS = 8
NSA_GROUPS = 2
NSA_HPG = NSA_HEADS // NSA_GROUPS
NSA_HEAD_DIM = 64
NSA_WIDTH = NSA_HEADS * NSA_HEAD_DIM
NSA_KV = NSA_GROUPS * NSA_HEAD_DIM
CMP_BLOCK = 32
CMP_STRIDE = 16
CMP_HIDDEN = 256
SEL_BLOCK = 64
SEL_TOPK = 8
WINDOW = 512
DN_HEADS = 4
DN_HEAD_DIM = 128
DN_WIDTH = DN_HEADS * DN_HEAD_DIM
DN_CONV = 4
DN_CHUNK = 64
NUM_BUCKETS = 32
REL_MAX_DIST = 1024
DEEPNORM_ALPHA = 2.0 ** 0.25
NEG = -1e30
FORCE = 1e6

LANES = 128
TILE = 128
SEL_LANE0 = 64
SEL_GROUP = 4
VMEM_LIMIT = 56 * 1024 * 1024
DN_INV_BLOCK = 16
DN_PREP_UNROLL = 8

HIGHEST = lax.Precision.HIGHEST

PB_QP = 0
PB_KS = PB_QP + NSA_HEADS * LANES
PB_KW = PB_KS + NSA_GROUPS * LANES
PB_VS = PB_KW + NSA_GROUPS * LANES
PB_VW = PB_VS + LANES
PB_WIDTH = PB_VW + LANES
PF_KC = 0
PF_VC = 128
PF_SMALL = 256
PF_ZA = 512
PF_ZB = 1024
PF_QKVB = 1536
PF_GM = 3072
PF_WIDTH = 5120
SMALL_BETA = 3 * NSA_HEADS
SMALL_A = SMALL_BETA + DN_HEADS


def _bucket_thresholds():
    max_exact = NUM_BUCKETS // 2
    span = NUM_BUCKETS - max_exact
    ratio = REL_MAX_DIST // max_exact
    thr = list(range(1, max_exact + 1))
    for k in range(1, span):
        n = max_exact
        while n ** span < max_exact ** span * ratio ** k:
            n += 1
        thr.append(n)
    return tuple(thr)


_THR = _bucket_thresholds()


def _cparams(n_axes):
    return pltpu.CompilerParams(dimension_semantics=("arbitrary",) * n_axes, vmem_limit_bytes=VMEM_LIMIT)


def _mm_kernel(x_ref, w_ref, o_ref):
    o_ref[...] = jnp.dot(x_ref[...], w_ref[...], preferred_element_type=jnp.float32).astype(o_ref.dtype)


def _matmul(x, w, out_dtype, tm, tn, name):
    m, k = x.shape
    n = w.shape[1]
    return pl.pallas_call(
        _mm_kernel,
        grid=(n // tn, m // tm),
        in_specs=[pl.BlockSpec((tm, k), lambda j, i: (i, 0)),
                  pl.BlockSpec((k, tn), lambda j, i: (0, j))],
        out_specs=pl.BlockSpec((tm, tn), lambda j, i: (i, j)),
        out_shape=jax.ShapeDtypeStruct((m, n), out_dtype),
        compiler_params=_cparams(2), name=name,
    )(x, w)


def _prep_w_in(w):
    d = w.shape[0]
    o = 0
    wq = w[:, o:o + NSA_WIDTH]; o += NSA_WIDTH
    wkv = w[:, o:o + 6 * NSA_KV]; o += 6 * NSA_KV
    wg = w[:, o:o + 3 * NSA_HEADS]; o += 3 * NSA_HEADS
    wza = w[:, o:o + NSA_WIDTH]; o += NSA_WIDTH
    wqkvb = w[:, o:o + 3 * DN_WIDTH]; o += 3 * DN_WIDTH
    wbeta = w[:, o:o + DN_HEADS]; o += DN_HEADS
    wa = w[:, o:o + DN_HEADS]; o += DN_HEADS
    wzb = w[:, o:o + DN_WIDTH]; o += DN_WIDTH
    wgm = w[:, o:o + 2 * D_MODEL]
    wkc, wvc, wks, wvs, wkw, wvw = [wkv[:, i * NSA_KV:(i + 1) * NSA_KV] for i in range(6)]

    def pad_heads(t, nh):
        t = t.reshape(d, nh, NSA_HEAD_DIM)
        return jnp.concatenate([t, jnp.zeros_like(t)], axis=-1).reshape(d, nh * LANES)

    wb = jnp.concatenate([pad_heads(wq * NSA_HEAD_DIM ** -0.5, NSA_HEADS), pad_heads(wks, NSA_GROUPS),
                          pad_heads(wkw, NSA_GROUPS), wvs, wvw], axis=1).astype(jnp.bfloat16)
    small = jnp.concatenate([wg, wbeta, wa, jnp.zeros((d, LANES - SMALL_A - DN_HEADS), w.dtype)], axis=1)
    wf = jnp.concatenate([wkc, wvc, small, jnp.zeros((d, LANES), w.dtype), wza, wzb, wqkvb, wgm],
                         axis=1).astype(jnp.bfloat16)
    return wb, wf


def _bias_kernel(tab_ref, sel_ref, win_ref, cmp_ref, *, n_cmp):
    h = pl.program_id(0)

    def lookup(n):
        val = jnp.full(n.shape, tab_ref[0, h], jnp.float32)
        for b in range(1, NUM_BUCKETS):
            val = jnp.where(n >= _THR[b - 1], tab_ref[b, h], val)
        return val

    kj = lax.broadcasted_iota(jnp.int32, (TILE, TILE), 0)
    qi = lax.broadcasted_iota(jnp.int32, (TILE, TILE), 1)
    n_sel_tiles = sel_ref.shape[1] - 1
    n_win_tiles = win_ref.shape[1] - 1
    for dt in range(max(n_sel_tiles, n_win_tiles)):
        dist = dt * TILE + qi - kj
        v = lookup(jnp.maximum(dist, 0))
        if dt < n_sel_tiles:
            sel_ref[0, dt] = jnp.where(dist >= 0, v, NEG)
        if dt < n_win_tiles:
            win_ref[0, dt] = jnp.where((dist >= 0) & (dist < WINDOW), v, NEG)
    sel_ref[0, n_sel_tiles] = jnp.full((TILE, TILE), NEG, jnp.float32)
    win_ref[0, n_win_tiles] = jnp.full((TILE, TILE), NEG, jnp.float32)
    c = lax.broadcasted_iota(jnp.int32, cmp_ref.shape[1:], 0)
    s = lax.broadcasted_iota(jnp.int32, cmp_ref.shape[1:], 1)
    dist = s - (c * CMP_STRIDE + CMP_BLOCK - 1)
    cmp_ref[0] = jnp.where((dist >= 0) & (c < n_cmp), lookup(jnp.maximum(dist, 0)), NEG)


def _bias_tables(rel_bias, seq):
    nq = seq // TILE
    nwin = WINDOW // TILE + 1
    nc = seq // CMP_STRIDE
    n_cmp = nc - CMP_BLOCK // CMP_STRIDE + 1
    return pl.pallas_call(
        functools.partial(_bias_kernel, n_cmp=n_cmp),
        grid=(NSA_HEADS,),
        in_specs=[pl.BlockSpec(memory_space=pltpu.SMEM)],
        out_specs=[pl.BlockSpec((1, nq + 1, TILE, TILE), lambda h: (h, 0, 0, 0)),
                   pl.BlockSpec((1, nwin + 1, TILE, TILE), lambda h: (h, 0, 0, 0)),
                   pl.BlockSpec((1, nc, seq), lambda h: (h, 0, 0))],
        out_shape=[jax.ShapeDtypeStruct((NSA_HEADS, nq + 1, TILE, TILE), jnp.float32),
                   jax.ShapeDtypeStruct((NSA_HEADS, nwin + 1, TILE, TILE), jnp.float32),
                   jax.ShapeDtypeStruct((NSA_HEADS, nc, seq), jnp.float32)],
        compiler_params=_cparams(1), name="bias_tables",
    )(rel_bias.astype(jnp.float32))


def _gelu_tanh(x):
    return x * (0.5 * (1.0 + jnp.tanh(math.sqrt(2.0 / math.pi) * (x + 0.044715 * (x * x * x)))))


def _compress_kernel(r_ref, pos_ref, w1_ref, w2_ref, o_ref):
    r = r_ref[0, 0]
    nc = r.shape[0]
    a = jnp.dot((r + pos_ref[0, 0:1, :]).astype(jnp.bfloat16), w1_ref[0, 0], preferred_element_type=jnp.float32)
    b = jnp.dot((r + pos_ref[0, 1:2, :]).astype(jnp.bfloat16), w1_ref[0, 1], preferred_element_type=jnp.float32)
    hid = a + pltpu.roll(b, nc - 1, 0)
    o_ref[0, 0] = jnp.dot(_gelu_tanh(hid).astype(jnp.bfloat16), w2_ref[0], preferred_element_type=jnp.float32)


def _compress(rkv, pos2, w1p, w2p):
    bsz, _, nc, width = rkv.shape
    hid = w1p.shape[-1]
    return pl.pallas_call(
        _compress_kernel,
        grid=(2, bsz),
        in_specs=[pl.BlockSpec((1, 1, nc, width), lambda k, b: (b, k, 0, 0)),
                  pl.BlockSpec((1, 2, width), lambda k, b: (k, 0, 0)),
                  pl.BlockSpec((1, 2, width, hid), lambda k, b: (k, 0, 0, 0)),
                  pl.BlockSpec((1, hid, LANES), lambda k, b: (k, 0, 0))],
        out_specs=pl.BlockSpec((1, 1, nc, LANES), lambda k, b: (b, k, 0, 0)),
        out_shape=jax.ShapeDtypeStruct((bsz, 2, nc, LANES), jnp.float32),
        compiler_params=_cparams(2), name="compress",
    )(rkv, pos2, w1p, w2p)


def _prep_compress_weights(pos_k, pos_v, w1_k, w1_v, w2_k, w2_v):
    eye = jnp.eye(NSA_GROUPS, dtype=jnp.float32)
    half = CMP_BLOCK // 2

    def w1_both(w1):
        w = w1.reshape(2, half, NSA_HEAD_DIM, CMP_HIDDEN)
        w = jnp.einsum('aldj,gh->algdhj', w, eye)
        return w.reshape(2, half * NSA_GROUPS * NSA_HEAD_DIM, NSA_GROUPS * CMP_HIDDEN)

    def w2_both(w2):
        return jnp.einsum('jd,gh->gjhd', w2, eye).reshape(NSA_GROUPS * CMP_HIDDEN, NSA_GROUPS * NSA_HEAD_DIM)

    def pos_both(pos):
        p = pos.reshape(2, half, 1, NSA_HEAD_DIM)
        return jnp.broadcast_to(p, (2, half, NSA_GROUPS, NSA_HEAD_DIM)).reshape(2, half * NSA_KV)

    pos2 = jnp.stack([pos_both(pos_k), pos_both(pos_v)]).astype(jnp.float32)
    w1p = jnp.stack([w1_both(w1_k), w1_both(w1_v)]).astype(jnp.bfloat16)
    w2p = jnp.stack([w2_both(w2_k), w2_both(w2_v)]).astype(jnp.bfloat16)
    return pos2, w1p, w2p


def _pair_lanes(x, g):
    sw = pltpu.roll(x, LANES // 2, 1)
    lane = lax.broadcasted_iota(jnp.int32, x.shape, 1)
    own = (lane < LANES // 2) == (g == 0)
    return jnp.where(own, x, sw)


def _group_rows_t(x, g):
    xt = x.T
    half = LANES // 2
    return jnp.where(g == 0, xt[0:half, :], xt[half:LANES, :])


def _nsa_kernel(q_ref, ks_ref, kw_ref, vs_ref, vw_ref, kcvc_ref, bc_ref, bsel_ref, bwin_ref,
                small_ref, za_ref, o_ref,
                ksa, kwp, vst, vwt, kcs, vct, qa, qp, sbuf, acc_s, l_s, *, n_cmp, n_sel, n_top):
    g = pl.program_id(1)
    qt = pl.program_id(2)
    tq = TILE
    seq = ks_ref.shape[0]
    nq = seq // TILE
    nc = kcvc_ref.shape[2]
    n_wt = bwin_ref.shape[1] - 1
    heads = range(NSA_HPG)
    f32 = jnp.float32
    bf16 = jnp.bfloat16
    nt_dims = (((1,), (1,)), ((), ()))

    @pl.when(qt == 0)
    def _():
        row = lax.broadcasted_iota(jnp.int32, (seq, LANES), 0)
        lane = lax.broadcasted_iota(jnp.int32, (seq, LANES), 1)
        onehot = (lane - SEL_LANE0) == (row >> 6)
        ksa[...] = jnp.where(onehot, 1.0, ks_ref[...].astype(f32)).astype(bf16)
        pad = (n_wt - 1) * TILE
        kwp[0:pad, :] = jnp.zeros((pad, LANES), bf16)
        kwp[pad:pad + seq, :] = kw_ref[...]
        for kt in range(n_wt - 1):
            vwt[kt] = jnp.zeros(vwt.shape[1:], bf16)
        for kt in range(nq):
            rows = slice(kt * TILE, (kt + 1) * TILE)
            vst[kt] = _group_rows_t(vs_ref[rows, :].astype(f32), g).astype(bf16)
            vwt[kt + n_wt - 1] = _group_rows_t(vw_ref[rows, :].astype(f32), g).astype(bf16)
        kcs[...] = _pair_lanes(kcvc_ref[0, 0], g).astype(bf16)
        for ct in range(nc // TILE):
            rows = slice(ct * TILE, (ct + 1) * TILE)
            vct[:, rows] = _group_rows_t(kcvc_ref[0, 1, rows, :], g).astype(bf16)

    qh = [q_ref[:, hh * LANES:(hh + 1) * LANES] for hh in heads]
    for hh in heads:
        qp[hh * tq:(hh + 1) * tq, :] = qh[hh]
    q_all = qp[...]

    def softmax_pv(s, v_t):
        m = jnp.max(s, axis=0, keepdims=True)
        p = jnp.exp(s - m)
        return jnp.dot(v_t, p.astype(bf16), preferred_element_type=f32), jnp.sum(p, axis=0, keepdims=True)

    s_c = lax.dot_general(kcs[...], q_all, nt_dims, preferred_element_type=f32)
    kw_rows = kwp[pl.ds(pl.multiple_of(qt * TILE, TILE), n_wt * TILE), :]
    s_w = lax.dot_general(kw_rows, q_all, nt_dims, preferred_element_type=f32)

    bias_c = jnp.concatenate([bc_ref[hh] for hh in heads], axis=1)
    valid = bias_c > 0.5 * NEG
    s_c = s_c + bias_c
    e = jnp.where(valid, jnp.exp(s_c - jnp.max(s_c, axis=0, keepdims=True)), 0.0)
    den = jnp.maximum(jnp.sum(e, axis=0, keepdims=True), 1e-30)
    p_c = e * (1.0 / den)
    o_cmp = jnp.dot(vct[...], p_c.astype(bf16), preferred_element_type=f32)
    psum = sum(p_c[:, hh * tq:(hh + 1) * tq] for hh in heads)

    w_tiles = []
    for t in range(n_wt):
        dt = n_wt - 1 - t
        idx = jnp.where(qt >= dt, dt, n_wt)
        w_tiles.append(jnp.concatenate([bwin_ref[hh, idx] for hh in heads], axis=1))
    acc_w, l_w = softmax_pv(s_w + jnp.concatenate(w_tiles, axis=0),
                            jnp.concatenate([vwt[qt + t] for t in range(n_wt)], axis=1))
    o_win = acc_w * (1.0 / l_w)

    sj = lax.broadcasted_iota(jnp.int32, (n_sel, nc), 0)
    ci = lax.broadcasted_iota(jnp.int32, (n_sel, nc), 1)
    overlap = ((ci * CMP_STRIDE < (sj + 1) * SEL_BLOCK) & (ci * CMP_STRIDE + CMP_BLOCK > sj * SEL_BLOCK)
               & (ci < n_cmp)).astype(bf16)
    imp_t = sum(jnp.dot(overlap, part, preferred_element_type=f32)
                for part in _split_bf16(psum, 3))
    blk = lax.broadcasted_iota(jnp.int32, (n_sel, tq), 0)
    cur = (qt * tq + lax.broadcasted_iota(jnp.int32, (n_sel, tq), 1)) >> 6
    forced = (blk == 0) | (blk == cur) | (blk == cur - 1)
    imp_t = jnp.where(forced, FORCE, jnp.where(blk > cur, -FORCE, imp_t))
    rank = jnp.zeros((n_sel, tq), jnp.int32)
    for j in range(n_sel):
        other = imp_t[j:j + 1, :]
        ahead = (other > imp_t) | ((other == imp_t) & (blk > j))
        rank = rank + ahead.astype(jnp.int32)
    selb = jnp.where(rank < n_top, 0.0, NEG)
    pieces = [jnp.zeros((SEL_LANE0, tq), jnp.float32), selb]
    if LANES - SEL_LANE0 - n_sel > 0:
        pieces.append(jnp.zeros((LANES - SEL_LANE0 - n_sel, tq), jnp.float32))
    selb_r = jnp.concatenate(pieces, axis=0).T.astype(jnp.bfloat16)

    for hh in heads:
        qa[hh * tq:(hh + 1) * tq, :] = qh[hh] + selb_r
    gk = SEL_GROUP * TILE

    def sel_case(n_groups):
        q_aug = qa[...]
        col_max = []
        for gi in range(n_groups):
            s = lax.dot_general(ksa[gi * gk:(gi + 1) * gk, :], q_aug, nt_dims, preferred_element_type=f32)
            tiles = []
            for t in range(SEL_GROUP):
                dt = qt - (gi * SEL_GROUP + t)
                idx = jnp.where(dt >= 0, dt, nq)
                tiles.append(jnp.concatenate([bsel_ref[hh, idx] for hh in heads], axis=1))
            s = s + jnp.concatenate(tiles, axis=0)
            sbuf[gi] = s
            col_max.append(jnp.max(s, axis=0, keepdims=True))
        m = functools.reduce(jnp.maximum, col_max)
        acc = jnp.zeros(acc_s.shape, f32)
        l = jnp.zeros(l_s.shape, f32)
        for gi in range(n_groups):
            p = jnp.exp(sbuf[gi] - m)
            v_t = jnp.concatenate([vst[gi * SEL_GROUP + t] for t in range(SEL_GROUP)], axis=1)
            acc = acc + jnp.dot(v_t, p.astype(bf16), preferred_element_type=f32)
            l = l + jnp.sum(p, axis=0, keepdims=True)
        acc_s[...] = acc
        l_s[...] = l

    for n_groups in range(1, nq // SEL_GROUP + 1):
        pl.when(qt // SEL_GROUP == n_groups - 1)(functools.partial(sel_case, n_groups))
    o_sel = acc_s[...] * (1.0 / l_s[...])

    gates_t = jax.nn.sigmoid(small_ref[...]).T
    mixed = []
    for hh in heads:
        def gate(br):
            r = br * NSA_HEADS + hh
            return jnp.where(g == 0, gates_t[r:r + 1, :], gates_t[r + NSA_HPG:r + NSA_HPG + 1, :])
        cols = slice(hh * tq, (hh + 1) * tq)
        mixed.append(gate(0) * o_cmp[:, cols] + gate(1) * o_sel[:, cols] + gate(2) * o_win[:, cols])
    o = jnp.concatenate([jnp.concatenate(mixed[2 * j:2 * j + 2], axis=0).T for j in range(NSA_HPG // 2)], axis=1)
    z = za_ref[...]
    o_ref[...] = (o * (z * jax.nn.sigmoid(z))).astype(o_ref.dtype)


def _nsa(pb, pf, kcvc, bias_sel, bias_win, bias_cmp, bsz, seq):
    nq = seq // TILE
    nc = seq // CMP_STRIDE
    n_cmp = nc - CMP_BLOCK // CMP_STRIDE + 1
    n_sel = seq // SEL_BLOCK
    n_top = min(SEL_TOPK, n_sel)
    nwin = bias_win.shape[1]
    assert nq % SEL_GROUP == 0 and nc % TILE == 0 and bias_sel.shape[1] == nq + 1
    gw = NSA_HPG * NSA_HEAD_DIM
    kern = functools.partial(_nsa_kernel, n_cmp=n_cmp, n_sel=n_sel, n_top=n_top)
    return pl.pallas_call(
        kern,
        grid=(bsz, NSA_GROUPS, nq),
        in_specs=[
            pl.BlockSpec((TILE, NSA_HPG * LANES), lambda b, g, t: (b * nq + t, g)),
            pl.BlockSpec((seq, LANES), lambda b, g, t: (b, PB_KS // LANES + g)),
            pl.BlockSpec((seq, LANES), lambda b, g, t: (b, PB_KW // LANES + g)),
            pl.BlockSpec((seq, LANES), lambda b, g, t: (b, PB_VS // LANES)),
            pl.BlockSpec((seq, LANES), lambda b, g, t: (b, PB_VW // LANES)),
            pl.BlockSpec((1, 2, nc, LANES), lambda b, g, t: (b, 0, 0, 0)),
            pl.BlockSpec((NSA_HPG, nc, TILE), lambda b, g, t: (g, 0, t)),
            pl.BlockSpec((NSA_HPG, nq + 1, TILE, TILE), lambda b, g, t: (g, 0, 0, 0)),
            pl.BlockSpec((NSA_HPG, nwin, TILE, TILE), lambda b, g, t: (g, 0, 0, 0)),
            pl.BlockSpec((TILE, LANES), lambda b, g, t: (b * nq + t, PF_SMALL // LANES)),
            pl.BlockSpec((TILE, gw), lambda b, g, t: (b * nq + t, PF_ZA // gw + g)),
        ],
        out_specs=pl.BlockSpec((TILE, gw), lambda b, g, t: (b * nq + t, g)),
        out_shape=jax.ShapeDtypeStruct((bsz * seq, NSA_WIDTH), jnp.bfloat16),
        scratch_shapes=[
            pltpu.VMEM((seq, LANES), jnp.bfloat16),
            pltpu.VMEM((seq + (nwin - 2) * TILE, LANES), jnp.bfloat16),
            pltpu.VMEM((nq, NSA_HEAD_DIM, TILE), jnp.bfloat16),
            pltpu.VMEM((nq + nwin - 2, NSA_HEAD_DIM, TILE), jnp.bfloat16),
            pltpu.VMEM((nc, LANES), jnp.bfloat16),
            pltpu.VMEM((NSA_HEAD_DIM, nc), jnp.bfloat16),
            pltpu.VMEM((NSA_HPG * TILE, LANES), jnp.bfloat16),
            pltpu.VMEM((NSA_HPG * TILE, LANES), jnp.bfloat16),
            pltpu.VMEM((nq // SEL_GROUP, SEL_GROUP * TILE, NSA_HPG * TILE), jnp.float32),
            pltpu.VMEM((NSA_HEAD_DIM, NSA_HPG * TILE), jnp.float32),
            pltpu.VMEM((1, NSA_HPG * TILE), jnp.float32),
        ],
        compiler_params=_cparams(3), name="nsa",
    )(pb, pb, pb, pb, pb, kcvc, bias_cmp, bias_sel, bias_win, pf, pf)


def _split_bf16(a, n):
    parts = []
    for _ in range(n - 1):
        hi = a.astype(jnp.bfloat16)
        parts.append(hi)
        a = a - hi.astype(jnp.float32)
    parts.append(a.astype(jnp.bfloat16))
    return parts


def _dot3(a, b):
    ah, al = _split_bf16(a, 2)
    bh, bl = _split_bf16(b, 2)
    f32 = jnp.float32
    return (jnp.dot(ah, bh, preferred_element_type=f32) + jnp.dot(al, bh, preferred_element_type=f32)
            + jnp.dot(ah, bl, preferred_element_type=f32))


def _softplus(x):
    return jnp.maximum(x, 0.0) + jnp.log(1.0 + jnp.exp(-jnp.abs(x)))


def _dn_kernel(scal_ref, q_ref, k_ref, v_ref, small_ref, z_ref, cw_ref, nw_ref, o_ref,
               qn, kn, vn, bet, gl, mm, nn, qq, oo, dd):
    h = pl.program_id(1)
    seq = q_ref.shape[0]
    c = DN_CHUNK
    n_chunks = seq // c
    d = DN_HEAD_DIM

    row = lax.broadcasted_iota(jnp.int32, (seq, d), 0)

    def conv_silu(x_ref, which):
        x = x_ref[...]
        y = x * cw_ref[which, DN_CONV - 1:DN_CONV, :]
        for j in range(DN_CONV - 1):
            sh = DN_CONV - 1 - j
            y = y + jnp.where(row >= sh, pltpu.roll(x, sh, 0), 0.0) * cw_ref[which, j:j + 1, :]
        return y * jax.nn.sigmoid(y)

    def l2n(t):
        return t * lax.rsqrt(jnp.sum(t * t, axis=-1, keepdims=True) + 1e-6)

    qn[...] = l2n(conv_silu(q_ref, 0)) * (d ** -0.5)
    kn[...] = l2n(conv_silu(k_ref, 1))
    vn[...] = conv_silu(v_ref, 2)
    small = small_ref[...]
    lane = lax.broadcasted_iota(jnp.int32, small.shape, 1)
    beta_in = jnp.sum(jnp.where(lane == SMALL_BETA + h, small, 0.0), axis=-1, keepdims=True)
    a_in = jnp.sum(jnp.where(lane == SMALL_A + h, small, 0.0), axis=-1, keepdims=True)
    bet[...] = jnp.broadcast_to(jax.nn.sigmoid(beta_in), (seq, d))
    gl[...] = jnp.broadcast_to(-jnp.exp(scal_ref[0, h]) * _softplus(a_in + scal_ref[1, h]), (seq, d))

    ri = lax.broadcasted_iota(jnp.int32, (c, c), 0)
    cj = lax.broadcasted_iota(jnp.int32, (c, c), 1)
    incl = ri >= cj
    strict = ri > cj
    tril = incl.astype(jnp.float32)
    eye = (ri == cj).astype(jnp.float32)

    tril_b = tril.astype(jnp.bfloat16)
    bs = DN_INV_BLOCK
    sh = bs.bit_length() - 1
    same_diag = (ri >> sh) == (cj >> sh)
    level_masks = []
    while (1 << sh) < c:
        level_masks.append(((ri >> (sh + 1)) == (cj >> (sh + 1))) & ((ri >> sh) > (cj >> sh)))
        sh += 1

    bf16 = jnp.bfloat16
    nt_dims = (((1,), (1,)), ((), ()))

    def dot1(a, b):
        return jnp.dot(a.astype(bf16), b.astype(bf16), preferred_element_type=jnp.float32)

    def chunk_prep(it, carry):
        ids = [it * DN_PREP_UNROLL + cc for cc in range(DN_PREP_UNROLL)]
        rows = [pl.ds(pl.multiple_of(i * c, c), c) for i in ids]
        ks = [kn[r, :] for r in rows]
        betas = [bet[r, :] for r in rows]
        gcbs = [sum(jnp.dot(tril_b, part, preferred_element_type=jnp.float32)
                    for part in _split_bf16(gl[r, :], 3)) for r in rows]
        kbs = [k * beta for k, beta in zip(ks, betas)]
        kbfs = [k.astype(bf16) for k in ks]
        a_kks = [lax.dot_general(kb.astype(bf16), kbf, nt_dims, preferred_element_type=jnp.float32)
                 for kb, kbf in zip(kbs, kbfs)]
        decays = []
        for gcb in gcbs:
            gct = jnp.concatenate([gcb, gcb], axis=0).T
            diff = gcb[:, 0:c] - gct[0:c, 0:c]
            decays.append(jnp.where(incl, jnp.exp(jnp.where(incl, diff, 0.0)), 0.0))
        lows = [jnp.where(strict, a * dec, 0.0) for a, dec in zip(a_kks, decays)]
        pws = [jnp.where(same_diag, -low, 0.0) for low in lows]
        es = list(pws)
        for _ in range(max(1, (bs - 1).bit_length()) - 1):
            pws = [dot1(pw, pw) for pw in pws]
            es = [e + pw + dot1(e, pw) for e, pw in zip(es, pws)]
        for below in level_masks:
            offs = [jnp.where(below, low, 0.0) for low in lows]
            xs = [off + dot1(e, off) for e, off in zip(es, offs)]
            es = [e - (x + dot1(x, e)) for e, x in zip(es, xs)]
        egcs = [jnp.exp(gcb) for gcb in gcbs]
        rhss = [jnp.concatenate([vn[r, :] * beta, kb * egc], axis=1)
                for r, beta, kb, egc in zip(rows, betas, kbs, egcs)]
        uws = [rhs + _dot3(e, rhs) for e, rhs in zip(es, rhss)]
        qs = [qn[r, :] for r in rows]
        a_qks = [lax.dot_general(q.astype(bf16), kbf, nt_dims, preferred_element_type=jnp.float32) * dec
                 for q, kbf, dec in zip(qs, kbfs, decays)]
        g_lasts = [gcb[c - 1:c, :] for gcb in gcbs]
        kdec_ts = []
        for k, gcb, g_last in zip(ks, gcbs, g_lasts):
            kdec = k * jnp.exp(g_last - gcb)
            kdec_ts.append(jnp.concatenate([kdec, jnp.zeros_like(kdec)], axis=0).T[:, 0:c].astype(bf16))
        uwbs = [uw.astype(bf16) for uw in uws]
        nms = [jnp.dot(kt, uwb, preferred_element_type=jnp.float32) for kt, uwb in zip(kdec_ts, uwbs)]
        oqs = [jnp.dot(a.astype(bf16), uwb, preferred_element_type=jnp.float32) for a, uwb in zip(a_qks, uwbs)]
        for i, r, nm, oq, q, egc, g_last in zip(ids, rows, nms, oqs, qs, egcs, g_lasts):
            m0 = pl.ds(pl.multiple_of(i * d, d), d)
            nn[m0, :] = nm[:, 0:d]
            mm[m0, :] = nm[:, d:2 * d]
            oo[r, :] = oq[:, 0:d]
            qq[r, :] = q * egc - oq[:, d:2 * d]
            dd[pl.ds(pl.multiple_of(i * 8, 8), 8), :] = jnp.broadcast_to(jnp.exp(g_last), (8, d))
        return carry

    lax.fori_loop(0, n_chunks // DN_PREP_UNROLL, chunk_prep, 0)

    nw = nw_ref[...]

    def chunk_scan(i, state):
        r0 = pl.multiple_of(i * c, c)
        m0 = pl.multiple_of(i * d, d)
        sb = state.astype(jnp.bfloat16)
        o = jnp.dot(qq[pl.ds(r0, c), :].astype(jnp.bfloat16), sb, preferred_element_type=jnp.float32) \
            + oo[pl.ds(r0, c), :]
        dec = dd[pl.ds(pl.multiple_of(i * 8, 8), 1), :]
        new_state = state * dec - jnp.dot(mm[pl.ds(m0, d), :].astype(jnp.bfloat16), sb,
                                          preferred_element_type=jnp.float32) + nn[pl.ds(m0, d), :]
        o = o * lax.rsqrt(jnp.mean(o * o, axis=-1, keepdims=True) + 1e-6) * nw
        z = z_ref[pl.ds(r0, c), :]
        o_ref[pl.ds(r0, c), :] = (o * (z * jax.nn.sigmoid(z))).astype(o_ref.dtype)
        return new_state

    lax.fori_loop(0, n_chunks, chunk_scan, jnp.zeros((d, d), jnp.float32))


def _deltanet(pf, conv_w, a_log, dt_bias, norm_w, bsz, seq):
    d = DN_HEAD_DIM
    n_chunks = seq // DN_CHUNK
    qkv0 = PF_QKVB // d
    scal = jnp.stack([a_log, dt_bias]).astype(jnp.float32)
    f32 = jnp.float32
    cw4 = conv_w.astype(f32).reshape(DN_CONV, 3, DN_HEADS, d).transpose(2, 1, 0, 3)
    assert 2 * DN_CHUNK == d
    return pl.pallas_call(
        _dn_kernel,
        grid=(bsz, DN_HEADS),
        in_specs=[
            pl.BlockSpec(memory_space=pltpu.SMEM),
            pl.BlockSpec((seq, d), lambda b, h: (b, qkv0 + h)),
            pl.BlockSpec((seq, d), lambda b, h: (b, qkv0 + DN_HEADS + h)),
            pl.BlockSpec((seq, d), lambda b, h: (b, qkv0 + 2 * DN_HEADS + h)),
            pl.BlockSpec((seq, LANES), lambda b, h: (b, PF_SMALL // LANES)),
            pl.BlockSpec((seq, d), lambda b, h: (b, PF_ZB // d + h)),
            pl.BlockSpec((None, 3, DN_CONV, d), lambda b, h: (h, 0, 0, 0)),
            pl.BlockSpec((1, d), lambda b, h: (0, 0)),
        ],
        out_specs=pl.BlockSpec((seq, d), lambda b, h: (b, h)),
        out_shape=jax.ShapeDtypeStruct((bsz * seq, DN_WIDTH), jnp.bfloat16),
        scratch_shapes=[
            pltpu.VMEM((seq, d), f32), pltpu.VMEM((seq, d), f32), pltpu.VMEM((seq, d), f32),
            pltpu.VMEM((seq, d), f32), pltpu.VMEM((seq, d), f32),
            pltpu.VMEM((n_chunks * d, d), f32), pltpu.VMEM((n_chunks * d, d), f32),
            pltpu.VMEM((seq, d), f32), pltpu.VMEM((seq, d), f32),
            pltpu.VMEM((n_chunks * 8, d), f32),
        ],
        compiler_params=_cparams(2), name="deltanet",
    )(scal, pf, pf, pf, pf, pf, cw4, norm_w.astype(f32).reshape(1, d))


def _out_kernel(oa_ref, ob_ref, gma_ref, gmb_ref, x_ref, p_ref, wa_ref, wb_ref, wo_ref, wpg_ref, wp_ref,
                lng_ref, lnb_ref, o_ref):
    f32 = jnp.float32
    y_a = jnp.dot(oa_ref[...], wa_ref[...], preferred_element_type=f32)
    y_b = jnp.dot(ob_ref[...], wb_ref[...], preferred_element_type=f32)
    mix = jax.nn.sigmoid(gma_ref[...]) * y_a + jax.nn.sigmoid(gmb_ref[...]) * y_b
    mixed = jnp.dot(mix.astype(jnp.bfloat16), wo_ref[...], preferred_element_type=f32)
    h = DEEPNORM_ALPHA * x_ref[...] + mixed
    gate = jax.nn.sigmoid(jnp.dot(h.astype(jnp.bfloat16), wpg_ref[...], preferred_element_type=f32))
    h = h + gate * jnp.dot(p_ref[...].astype(jnp.bfloat16), wp_ref[...], preferred_element_type=f32)
    mu = jnp.mean(h, axis=-1, keepdims=True)
    hc = h - mu
    var = jnp.mean(hc * hc, axis=-1, keepdims=True)
    o_ref[...] = (hc * lax.rsqrt(var + 1e-5) * lng_ref[...] + lnb_ref[...]).astype(o_ref.dtype)


def _out_block(o_a, o_b, pf, x2, p2, wa, wb, wo, wpg, wp, ln_g, ln_b, tm):
    t = x2.shape[0]
    bf = jnp.bfloat16

    def full(shape):
        return pl.BlockSpec(shape, lambda i: (0, 0))

    return pl.pallas_call(
        _out_kernel,
        grid=(t // tm,),
        in_specs=[
            pl.BlockSpec((tm, NSA_WIDTH), lambda i: (i, 0)),
            pl.BlockSpec((tm, DN_WIDTH), lambda i: (i, 0)),
            pl.BlockSpec((tm, D_MODEL), lambda i: (i, PF_GM // D_MODEL)),
            pl.BlockSpec((tm, D_MODEL), lambda i: (i, PF_GM // D_MODEL + 1)),
            pl.BlockSpec((tm, D_MODEL), lambda i: (i, 0)),
            pl.BlockSpec((tm, PLE_DIM), lambda i: (i, 0)),
            full((NSA_WIDTH, D_MODEL)), full((DN_WIDTH, D_MODEL)), full((D_MODEL, D_MODEL)),
            full((D_MODEL, D_MODEL)), full((PLE_DIM, D_MODEL)), full((1, D_MODEL)), full((1, D_MODEL)),
        ],
        out_specs=pl.BlockSpec((tm, D_MODEL), lambda i: (i, 0)),
        out_shape=jax.ShapeDtypeStruct((t, D_MODEL), x2.dtype),
        compiler_params=_cparams(1), name="out_block",
    )(o_a, o_b, pf, pf, x2, p2, wa.astype(bf), wb.astype(bf), wo.astype(bf), wpg.astype(bf), wp.astype(bf),
      ln_g.astype(jnp.float32).reshape(1, D_MODEL), ln_b.astype(jnp.float32).reshape(1, D_MODEL))


def _layer(x, p, w_in, pos_k, pos_v, w1_k, w2_k, w1_v, w2_v, bias_tabs, conv_w, a_log, dt_bias, norm_w,
           w_a, w_b, w_o, w_ple, w_pg, ln_g, ln_b):
    bsz, seq, _ = x.shape
    t = bsz * seq
    x2 = x.reshape(t, D_MODEL)
    xb = x2.astype(jnp.bfloat16)
    wb16, wf16 = _prep_w_in(w_in)
    tm = 512 if t % 512 == 0 else seq
    pb = _matmul(xb, wb16, jnp.bfloat16, tm, PB_WIDTH // 2, "proj_bf16")
    pf = _matmul(xb, wf16, jnp.float32, tm, 1024, "proj_f32")

    nc = seq // CMP_STRIDE
    rkv = pf[:, PF_KC:PF_KC + 2 * NSA_KV].reshape(bsz, nc, CMP_STRIDE, 2, NSA_KV)
    rkv = rkv.transpose(0, 3, 1, 2, 4).reshape(bsz, 2, nc, CMP_STRIDE * NSA_KV)
    pos2, w1p, w2p = _prep_compress_weights(pos_k, pos_v, w1_k, w1_v, w2_k, w2_v)
    kcvc = _compress(rkv, pos2, w1p, w2p)

    bias_sel, bias_win, bias_cmp = bias_tabs
    o_a = _nsa(pb, pf, kcvc, bias_sel, bias_win, bias_cmp, bsz, seq)
    o_b = _deltanet(pf, conv_w, a_log, dt_bias, norm_w, bsz, seq)
    out = _out_block(o_a, o_b, pf, x2, p.reshape(t, PLE_DIM), w_a, w_b, w_o, w_pg, w_ple, ln_g, ln_b,
                     256 if t % 256 == 0 else seq)
    return out.reshape(bsz, seq, D_MODEL)


def kernel(x, p, w_in, cmp_pos_k, cmp_pos_v, cmp_w1_k, cmp_w2_k, cmp_w1_v, cmp_w2_v, rel_bias, dn_conv_w,
           dn_a_log, dn_dt_bias, dn_norm_w, w_branch_a, w_branch_b, w_out, w_ple, w_ple_gate, ln_g, ln_b):
    depth = w_in.shape[0]
    bias_tabs = _bias_tables(rel_bias, x.shape[1])
    for i in range(depth):
        x = _layer(x, p[i], w_in[i], cmp_pos_k[i], cmp_pos_v[i], cmp_w1_k[i], cmp_w2_k[i], cmp_w1_v[i],
                   cmp_w2_v[i], bias_tabs, dn_conv_w[i], dn_a_log[i], dn_dt_bias[i], dn_norm_w[i],
                   w_branch_a[i], w_branch_b[i], w_out[i], w_ple[i], w_ple_gate[i], ln_g[i], ln_b[i])
    return x
```

```python
import functools
import math

import numpy as np
import jax
import jax.numpy as jnp
from jax import lax
from jax.experimental import pallas as pl
from jax.experimental.pallas import tpu as pltpu

D_MODEL = 1024
PLE_DIM = 256
NSA_HEADS = 8
NSA_GROUPS = 2
NSA_HPG = NSA_HEADS // NSA_GROUPS
NSA_HEAD_DIM = 64
NSA_WIDTH = NSA_HEADS * NSA_HEAD_DIM
NSA_KV = NSA_GROUPS * NSA_HEAD_DIM
CMP_BLOCK = 32
CMP_STRIDE = 16
CMP_HIDDEN = 256
SEL_BLOCK = 64
SEL_TOPK = 8
WINDOW = 512
DN_HEADS = 4
DN_HEAD_DIM = 128
DN_WIDTH = DN_HEADS * DN_HEAD_DIM
DN_CONV = 4
DN_CHUNK = 64
NUM_BUCKETS = 32
REL_MAX_DIST = 1024
DEEPNORM_ALPHA = 2.0 ** 0.25
NEG = -1e30
FORCE = 1e6
LOG2E = 1.4426950408889634

LANES = 128
TILE = 128
SEL_LANE0 = 64
SEL_GROUP = 4
VMEM_LIMIT = 56 * 1024 * 1024
PROJ_TM = 2048
PROJ_TN = 1024
OUT_TM = 512
DN_INV_BLOCK = 16
DN_BATCHES = 2
DN_OUT_UNROLL = 8
DN_PREP_UNROLL = 8

HIGHEST = lax.Precision.HIGHEST

PB_QP = 0
PB_KS = PB_QP + NSA_HEADS * LANES
PB_KW = PB_KS + NSA_GROUPS * LANES
PB_VS = PB_KW + NSA_GROUPS * LANES
PB_VW = PB_VS + LANES
PB_WIDTH = PB_VW + LANES
PF_KC = 0
PF_VC = 128
PF_SMALL = 256
PF_ZA = 512
PF_ZB = 1024
PF_QKVB = 1536
PF_GM = 3072
PF_WIDTH = 5120
SMALL_BETA = 3 * NSA_HEADS
SMALL_A = SMALL_BETA + DN_HEADS


def _bucket_thresholds():
    max_exact = NUM_BUCKETS // 2
    span = NUM_BUCKETS - max_exact
    ratio = REL_MAX_DIST // max_exact
    thr = list(range(1, max_exact + 1))
    for k in range(1, span):
        n = max_exact
        while n ** span < max_exact ** span * ratio ** k:
            n += 1
        thr.append(n)
    return tuple(thr)


_THR = _bucket_thresholds()


def _cparams(n_axes):
    return pltpu.CompilerParams(dimension_semantics=("arbitrary",) * n_axes, vmem_limit_bytes=VMEM_LIMIT)


def _mm_kernel(x_ref, w_ref, o_ref):
    o_ref[...] = jnp.dot(x_ref[...], w_ref[...], preferred_element_type=jnp.float32).astype(o_ref.dtype)


def _matmul(x, w, out_dtype, tm, tn, name):
    m, k = x.shape
    n = w.shape[1]
    return pl.pallas_call(
        _mm_kernel,
        grid=(n // tn, m // tm),
        in_specs=[pl.BlockSpec((tm, k), lambda j, i: (i, 0)),
                  pl.BlockSpec((k, tn), lambda j, i: (0, j))],
        out_specs=pl.BlockSpec((tm, tn), lambda j, i: (i, j)),
        out_shape=jax.ShapeDtypeStruct((m, n), out_dtype),
        compiler_params=_cparams(2), name=name,
    )(x, w)


def _prep_w_in(w):
    d = w.shape[0]
    o = 0
    wq = w[:, o:o + NSA_WIDTH]; o += NSA_WIDTH
    wkv = w[:, o:o + 6 * NSA_KV]; o += 6 * NSA_KV
    wg = w[:, o:o + 3 * NSA_HEADS]; o += 3 * NSA_HEADS
    wza = w[:, o:o + NSA_WIDTH]; o += NSA_WIDTH
    wqkvb = w[:, o:o + 3 * DN_WIDTH]; o += 3 * DN_WIDTH
    wbeta = w[:, o:o + DN_HEADS]; o += DN_HEADS
    wa = w[:, o:o + DN_HEADS]; o += DN_HEADS
    wzb = w[:, o:o + DN_WIDTH]; o += DN_WIDTH
    wgm = w[:, o:o + 2 * D_MODEL]
    wkc, wvc, wks, wvs, wkw, wvw = [wkv[:, i * NSA_KV:(i + 1) * NSA_KV] for i in range(6)]

    def pad_heads(t, nh):
        t = t.reshape(d, nh, NSA_HEAD_DIM)
        return jnp.concatenate([t, jnp.zeros_like(t)], axis=-1).reshape(d, nh * LANES)

    wb = jnp.concatenate([pad_heads(wq * (NSA_HEAD_DIM ** -0.5 * LOG2E), NSA_HEADS), pad_heads(wks, NSA_GROUPS),
                          pad_heads(wkw, NSA_GROUPS), wvs, wvw], axis=1).astype(jnp.bfloat16)
    small = jnp.concatenate([wg, wbeta, wa, jnp.zeros((d, LANES - SMALL_A - DN_HEADS), w.dtype)], axis=1)
    wf = jnp.concatenate([wkc, wvc, small, jnp.zeros((d, LANES), w.dtype), wza, wzb, wqkvb, wgm],
                         axis=1).astype(jnp.bfloat16)
    return wb, wf


def _bias_kernel(tab_ref, sel_ref, win_ref, cmp_ref, *, n_cmp):
    h = pl.program_id(0)

    def lookup(n):
        val = jnp.full(n.shape, tab_ref[0, h], jnp.float32)
        for b in range(1, NUM_BUCKETS):
            val = jnp.where(n >= _THR[b - 1], tab_ref[b, h], val)
        return val * LOG2E

    kj = lax.broadcasted_iota(jnp.int32, (TILE, TILE), 0)
    qi = lax.broadcasted_iota(jnp.int32, (TILE, TILE), 1)
    n_sel_tiles = sel_ref.shape[1] - 1
    n_win_tiles = win_ref.shape[1] - 1
    for dt in range(max(n_sel_tiles, n_win_tiles)):
        dist = dt * TILE + qi - kj
        v = lookup(jnp.maximum(dist, 0))
        if dt < n_sel_tiles:
            sel_ref[0, dt] = jnp.where(dist >= 0, v, NEG)
        if dt < n_win_tiles:
            win_ref[0, dt] = jnp.where((dist >= 0) & (dist < WINDOW), v, NEG)
    sel_ref[0, n_sel_tiles] = jnp.full((TILE, TILE), NEG, jnp.float32)
    win_ref[0, n_win_tiles] = jnp.full((TILE, TILE), NEG, jnp.float32)
    c = lax.broadcasted_iota(jnp.int32, cmp_ref.shape[1:], 0)
    s = lax.broadcasted_iota(jnp.int32, cmp_ref.shape[1:], 1)
    dist = s - (c * CMP_STRIDE + CMP_BLOCK - 1)
    cmp_ref[0] = jnp.where((dist >= 0) & (c < n_cmp), lookup(jnp.maximum(dist, 0)), NEG)


def _bias_tables(rel_bias, seq):
    nq = seq // TILE
    nwin = WINDOW // TILE + 1
    nc = seq // CMP_STRIDE
    n_cmp = nc - CMP_BLOCK // CMP_STRIDE + 1
    return pl.pallas_call(
        functools.partial(_bias_kernel, n_cmp=n_cmp),
        grid=(NSA_HEADS,),
        in_specs=[pl.BlockSpec(memory_space=pltpu.SMEM)],
        out_specs=[pl.BlockSpec((1, nq + 1, TILE, TILE), lambda h: (h, 0, 0, 0)),
                   pl.BlockSpec((1, nwin + 1, TILE, TILE), lambda h: (h, 0, 0, 0)),
                   pl.BlockSpec((1, nc, seq), lambda h: (h, 0, 0))],
        out_shape=[jax.ShapeDtypeStruct((NSA_HEADS, nq + 1, TILE, TILE), jnp.float32),
                   jax.ShapeDtypeStruct((NSA_HEADS, nwin + 1, TILE, TILE), jnp.float32),
                   jax.ShapeDtypeStruct((NSA_HEADS, nc, seq), jnp.float32)],
        compiler_params=_cparams(1), name="bias_tables",
    )(rel_bias.astype(jnp.float32))


def _gelu_tanh(x):
    return x * (0.5 * (1.0 + jnp.tanh(math.sqrt(2.0 / math.pi) * (x + 0.044715 * (x * x * x)))))


def _compress_kernel(r_ref, pos_ref, w1_ref, w2_ref, o_ref):
    r = r_ref[0, 0]
    nc = r.shape[0]
    a = jnp.dot((r + pos_ref[0, 0:1, :]).astype(jnp.bfloat16), w1_ref[0, 0], preferred_element_type=jnp.float32)
    b = jnp.dot((r + pos_ref[0, 1:2, :]).astype(jnp.bfloat16), w1_ref[0, 1], preferred_element_type=jnp.float32)
    hid = a + pltpu.roll(b, nc - 1, 0)
    o_ref[0, 0] = jnp.dot(_gelu_tanh(hid).astype(jnp.bfloat16), w2_ref[0], preferred_element_type=jnp.float32)


def _compress(rkv, pos2, w1p, w2p):
    bsz, _, nc, width = rkv.shape
    hid = w1p.shape[-1]
    return pl.pallas_call(
        _compress_kernel,
        grid=(2, bsz),
        in_specs=[pl.BlockSpec((1, 1, nc, width), lambda k, b: (b, k, 0, 0)),
                  pl.BlockSpec((1, 2, width), lambda k, b: (k, 0, 0)),
                  pl.BlockSpec((1, 2, width, hid), lambda k, b: (k, 0, 0, 0)),
                  pl.BlockSpec((1, hid, LANES), lambda k, b: (k, 0, 0))],
        out_specs=pl.BlockSpec((1, 1, nc, LANES), lambda k, b: (b, k, 0, 0)),
        out_shape=jax.ShapeDtypeStruct((bsz, 2, nc, LANES), jnp.float32),
        compiler_params=_cparams(2), name="compress",
    )(rkv, pos2, w1p, w2p)


def _prep_compress_weights(pos_k, pos_v, w1_k, w1_v, w2_k, w2_v):
    eye = jnp.eye(NSA_GROUPS, dtype=jnp.float32)
    half = CMP_BLOCK // 2

    def w1_both(w1):
        w = w1.reshape(2, half, NSA_HEAD_DIM, CMP_HIDDEN)
        w = jnp.einsum('aldj,gh->algdhj', w, eye)
        return w.reshape(2, half * NSA_GROUPS * NSA_HEAD_DIM, NSA_GROUPS * CMP_HIDDEN)

    def w2_both(w2):
        return jnp.einsum('jd,gh->gjhd', w2, eye).reshape(NSA_GROUPS * CMP_HIDDEN, NSA_GROUPS * NSA_HEAD_DIM)

    def pos_both(pos):
        p = pos.reshape(2, half, 1, NSA_HEAD_DIM)
        return jnp.broadcast_to(p, (2, half, NSA_GROUPS, NSA_HEAD_DIM)).reshape(2, half * NSA_KV)

    pos2 = jnp.stack([pos_both(pos_k), pos_both(pos_v)]).astype(jnp.float32)
    w1p = jnp.stack([w1_both(w1_k), w1_both(w1_v)]).astype(jnp.bfloat16)
    w2p = jnp.stack([w2_both(w2_k), w2_both(w2_v)]).astype(jnp.bfloat16)
    return pos2, w1p, w2p


def _pair_lanes(x, g):
    sw = pltpu.roll(x, LANES // 2, 1)
    lane = lax.broadcasted_iota(jnp.int32, x.shape, 1)
    own = (lane < LANES // 2) == (g == 0)
    return jnp.where(own, x, sw)


def _group_rows_t(x, g):
    xt = x.T
    half = LANES // 2
    return jnp.where(g == 0, xt[0:half, :], xt[half:LANES, :])


def _nsa_kernel(q_ref, ks_ref, kw_ref, vs_ref, vw_ref, kcvc_ref, bc_ref, bsel_ref, bwin_ref,
                small_ref, za_ref, o_ref,
                ksa, kwp, vst, vwt, kcs, vct, qa, qp, sbuf, acc_s, l_s, *, n_cmp, n_sel, n_top):
    g = pl.program_id(1)
    qt = pl.program_id(2)
    tq = TILE
    seq = ks_ref.shape[0]
    nq = seq // TILE
    nc = kcvc_ref.shape[2]
    n_wt = bwin_ref.shape[1] - 1
    heads = range(NSA_HPG)
    f32 = jnp.float32
    bf16 = jnp.bfloat16
    nt_dims = (((1,), (1,)), ((), ()))

    @pl.when(qt == 0)
    def _():
        row = lax.broadcasted_iota(jnp.int32, (seq, LANES), 0)
        lane = lax.broadcasted_iota(jnp.int32, (seq, LANES), 1)
        onehot = (lane - SEL_LANE0) == (row >> 6)
        ksa[...] = jnp.where(onehot, 1.0, ks_ref[...].astype(f32)).astype(bf16)
        pad = (n_wt - 1) * TILE
        kwp[0:pad, :] = jnp.zeros((pad, LANES), bf16)
        kwp[pad:pad + seq, :] = kw_ref[...]
        for kt in range(n_wt - 1):
            vwt[kt] = jnp.zeros(vwt.shape[1:], bf16)
        for kt in range(nq):
            rows = slice(kt * TILE, (kt + 1) * TILE)
            vst[kt] = _group_rows_t(vs_ref[rows, :].astype(f32), g).astype(bf16)
            vwt[kt + n_wt - 1] = _group_rows_t(vw_ref[rows, :].astype(f32), g).astype(bf16)
        kcs[...] = _pair_lanes(kcvc_ref[0, 0], g).astype(bf16)
        for ct in range(nc // TILE):
            rows = slice(ct * TILE, (ct + 1) * TILE)
            vct[:, rows] = _group_rows_t(kcvc_ref[0, 1, rows, :], g).astype(bf16)

    qh = [q_ref[:, hh * LANES:(hh + 1) * LANES] for hh in heads]
    for hh in heads:
        qp[hh * tq:(hh + 1) * tq, :] = qh[hh]
    q_all = qp[...]

    def softmax_pv(s, v_t):
        m = jnp.max(s, axis=0, keepdims=True)
        p = jnp.exp2(s - m)
        return jnp.dot(v_t, p.astype(bf16), preferred_element_type=f32), jnp.sum(p, axis=0, keepdims=True)

    s_c = lax.dot_general(kcs[...], q_all, nt_dims, preferred_element_type=f32)
    kw_rows = kwp[pl.ds(pl.multiple_of(qt * TILE, TILE), n_wt * TILE), :]
    s_w = lax.dot_general(kw_rows, q_all, nt_dims, preferred_element_type=f32)

    bias_c = jnp.concatenate([bc_ref[hh] for hh in heads], axis=1)
    valid = bias_c > 0.5 * NEG
    s_c = s_c + bias_c
    e = jnp.where(valid, jnp.exp2(s_c - jnp.max(s_c, axis=0, keepdims=True)), 0.0)
    den = jnp.maximum(jnp.sum(e, axis=0, keepdims=True), 1e-30)
    p_c = e * (1.0 / den)
    o_cmp = jnp.dot(vct[...], p_c.astype(bf16), preferred_element_type=f32)
    psum = sum(p_c[:, hh * tq:(hh + 1) * tq] for hh in heads)

    w_tiles = []
    for t in range(n_wt):
        dt = n_wt - 1 - t
        idx = jnp.where(qt >= dt, dt, n_wt)
        w_tiles.append(jnp.concatenate([bwin_ref[hh, idx] for hh in heads], axis=1))
    acc_w, l_w = softmax_pv(s_w + jnp.concatenate(w_tiles, axis=0),
                            jnp.concatenate([vwt[qt + t] for t in range(n_wt)], axis=1))
    o_win = acc_w * (1.0 / l_w)

    sj = lax.broadcasted_iota(jnp.int32, (n_sel, nc), 0)
    ci = lax.broadcasted_iota(jnp.int32, (n_sel, nc), 1)
    overlap = ((ci * CMP_STRIDE < (sj + 1) * SEL_BLOCK) & (ci * CMP_STRIDE + CMP_BLOCK > sj * SEL_BLOCK)
               & (ci < n_cmp)).astype(bf16)
    imp_t = sum(jnp.dot(overlap, part, preferred_element_type=f32)
                for part in _split_bf16(psum, 3))
    blk = lax.broadcasted_iota(jnp.int32, (n_sel, tq), 0)
    cur = (qt * tq + lax.broadcasted_iota(jnp.int32, (n_sel, tq), 1)) >> 6
    forced = (blk == 0) | (blk == cur) | (blk == cur - 1)
    imp_t = jnp.where(forced, FORCE, jnp.where(blk > cur, -FORCE, imp_t))
    rank = jnp.zeros((n_sel, tq), jnp.int32)
    for j in range(n_sel):
        other = imp_t[j:j + 1, :]
        ahead = (other > imp_t) | ((other == imp_t) & (blk > j))
        rank = rank + ahead.astype(jnp.int32)
    selb = jnp.where(rank < n_top, 0.0, NEG)
    pieces = [jnp.zeros((SEL_LANE0, tq), jnp.float32), selb]
    if LANES - SEL_LANE0 - n_sel > 0:
        pieces.append(jnp.zeros((LANES - SEL_LANE0 - n_sel, tq), jnp.float32))
    selb_r = jnp.concatenate(pieces, axis=0).T.astype(jnp.bfloat16)

    for hh in heads:
        qa[hh * tq:(hh + 1) * tq, :] = qh[hh] + selb_r
    gk = SEL_GROUP * TILE

    def sel_case(n_groups):
        q_aug = qa[...]
        col_max = []
        for gi in range(n_groups):
            s = lax.dot_general(ksa[gi * gk:(gi + 1) * gk, :], q_aug, nt_dims, preferred_element_type=f32)
            tiles = []
            for t in range(SEL_GROUP):
                dt = qt - (gi * SEL_GROUP + t)
                idx = jnp.where(dt >= 0, dt, nq)
                tiles.append(jnp.concatenate([bsel_ref[hh, idx] for hh in heads], axis=1))
            s = s + jnp.concatenate(tiles, axis=0)
            sbuf[gi] = s
            col_max.append(jnp.max(s, axis=0, keepdims=True))
        m = functools.reduce(jnp.maximum, col_max)
        acc = jnp.zeros(acc_s.shape, f32)
        l = jnp.zeros(l_s.shape, f32)
        for gi in range(n_groups):
            p = jnp.exp2(sbuf[gi] - m)
            v_t = jnp.concatenate([vst[gi * SEL_GROUP + t] for t in range(SEL_GROUP)], axis=1)
            acc = acc + jnp.dot(v_t, p.astype(bf16), preferred_element_type=f32)
            l = l + jnp.sum(p, axis=0, keepdims=True)
        acc_s[...] = acc
        l_s[...] = l

    for n_groups in range(1, nq // SEL_GROUP + 1):
        pl.when(qt // SEL_GROUP == n_groups - 1)(functools.partial(sel_case, n_groups))
    o_sel = acc_s[...] * (1.0 / l_s[...])

    gates_t = jax.nn.sigmoid(small_ref[...]).T
    mixed = []
    for hh in heads:
        def gate(br):
            r = br * NSA_HEADS + hh
            return jnp.where(g == 0, gates_t[r:r + 1, :], gates_t[r + NSA_HPG:r + NSA_HPG + 1, :])
        cols = slice(hh * tq, (hh + 1) * tq)
        mixed.append(gate(0) * o_cmp[:, cols] + gate(1) * o_sel[:, cols] + gate(2) * o_win[:, cols])
    o = jnp.concatenate([jnp.concatenate(mixed[2 * j:2 * j + 2], axis=0).T for j in range(NSA_HPG // 2)], axis=1)
    z = za_ref[...]
    o_ref[...] = (o * (z * jax.nn.sigmoid(z))).astype(o_ref.dtype)


def _nsa(pb, pf, kcvc, bias_sel, bias_win, bias_cmp, bsz, seq):
    nq = seq // TILE
    nc = seq // CMP_STRIDE
    n_cmp = nc - CMP_BLOCK // CMP_STRIDE + 1
    n_sel = seq // SEL_BLOCK
    n_top = min(SEL_TOPK, n_sel)
    nwin = bias_win.shape[1]
    assert nq % SEL_GROUP == 0 and nc % TILE == 0 and bias_sel.shape[1] == nq + 1
    gw = NSA_HPG * NSA_HEAD_DIM
    kern = functools.partial(_nsa_kernel, n_cmp=n_cmp, n_sel=n_sel, n_top=n_top)
    return pl.pallas_call(
        kern,
        grid=(bsz, NSA_GROUPS, nq),
        in_specs=[
            pl.BlockSpec((TILE, NSA_HPG * LANES), lambda b, g, t: (b * nq + t, g)),
            pl.BlockSpec((seq, LANES), lambda b, g, t: (b, PB_KS // LANES + g)),
            pl.BlockSpec((seq, LANES), lambda b, g, t: (b, PB_KW // LANES + g)),
            pl.BlockSpec((seq, LANES), lambda b, g, t: (b, PB_VS // LANES)),
            pl.BlockSpec((seq, LANES), lambda b, g, t: (b, PB_VW // LANES)),
            pl.BlockSpec((1, 2, nc, LANES), lambda b, g, t: (b, 0, 0, 0)),
            pl.BlockSpec((NSA_HPG, nc, TILE), lambda b, g, t: (g, 0, t)),
            pl.BlockSpec((NSA_HPG, nq + 1, TILE, TILE), lambda b, g, t: (g, 0, 0, 0)),
            pl.BlockSpec((NSA_HPG, nwin, TILE, TILE), lambda b, g, t: (g, 0, 0, 0)),
            pl.BlockSpec((TILE, LANES), lambda b, g, t: (b * nq + t, PF_SMALL // LANES)),
            pl.BlockSpec((TILE, gw), lambda b, g, t: (b * nq + t, PF_ZA // gw + g)),
        ],
        out_specs=pl.BlockSpec((TILE, gw), lambda b, g, t: (b * nq + t, g)),
        out_shape=jax.ShapeDtypeStruct((bsz * seq, NSA_WIDTH), jnp.bfloat16),
        scratch_shapes=[
            pltpu.VMEM((seq, LANES), jnp.bfloat16),
            pltpu.VMEM((seq + (nwin - 2) * TILE, LANES), jnp.bfloat16),
            pltpu.VMEM((nq, NSA_HEAD_DIM, TILE), jnp.bfloat16),
            pltpu.VMEM((nq + nwin - 2, NSA_HEAD_DIM, TILE), jnp.bfloat16),
            pltpu.VMEM((nc, LANES), jnp.bfloat16),
            pltpu.VMEM((NSA_HEAD_DIM, nc), jnp.bfloat16),
            pltpu.VMEM((NSA_HPG * TILE, LANES), jnp.bfloat16),
            pltpu.VMEM((NSA_HPG * TILE, LANES), jnp.bfloat16),
            pltpu.VMEM((nq // SEL_GROUP, SEL_GROUP * TILE, NSA_HPG * TILE), jnp.float32),
            pltpu.VMEM((NSA_HEAD_DIM, NSA_HPG * TILE), jnp.float32),
            pltpu.VMEM((1, NSA_HPG * TILE), jnp.float32),
        ],
        compiler_params=_cparams(3), name="nsa",
    )(pb, pb, pb, pb, pb, kcvc, bias_cmp, bias_sel, bias_win, pf, pf)


def _split_bf16(a, n):
    parts = []
    for _ in range(n - 1):
        hi = a.astype(jnp.bfloat16)
        parts.append(hi)
        a = a - hi.astype(jnp.float32)
    parts.append(a.astype(jnp.bfloat16))
    return parts


def _dot3(a, b):
    ah, al = _split_bf16(a, 2)
    bh, bl = _split_bf16(b, 2)
    f32 = jnp.float32
    return (jnp.dot(ah, bh, preferred_element_type=f32) + jnp.dot(al, bh, preferred_element_type=f32)
            + jnp.dot(ah, bl, preferred_element_type=f32))


def _softplus(x):
    return jnp.maximum(x, 0.0) + jnp.log(1.0 + jnp.exp(-jnp.abs(x)))


def _dn_kernel(scal_ref, q_ref, k_ref, v_ref, small_ref, z_ref, cw_ref, nw_ref, o_ref,
               qn, kn, vn, bet, gl, mm, nn, qq, oo, dd, ss, *, seq):
    h = pl.program_id(1)
    n_rows = q_ref.shape[0]
    nb = n_rows // seq
    c = DN_CHUNK
    n_chunks = seq // c
    d = DN_HEAD_DIM

    row = lax.broadcasted_iota(jnp.int32, (n_rows, d), 0)
    pos = row
    for b in range(1, nb):
        pos = jnp.where(row >= b * seq, row - b * seq, pos)

    def conv_silu(x_ref, which):
        x = x_ref[...]
        y = x * cw_ref[which, DN_CONV - 1:DN_CONV, :]
        for j in range(DN_CONV - 1):
            sh = DN_CONV - 1 - j
            y = y + jnp.where(pos >= sh, pltpu.roll(x, sh, 0), 0.0) * cw_ref[which, j:j + 1, :]
        return y * jax.nn.sigmoid(y)

    def l2n(t):
        return t * lax.rsqrt(jnp.sum(t * t, axis=-1, keepdims=True) + 1e-6)

    qn[...] = l2n(conv_silu(q_ref, 0)) * (d ** -0.5)
    kn[...] = l2n(conv_silu(k_ref, 1))
    vn[...] = conv_silu(v_ref, 2)
    small = small_ref[...]
    lane = lax.broadcasted_iota(jnp.int32, small.shape, 1)
    beta_in = jnp.sum(jnp.where(lane == SMALL_BETA + h, small, 0.0), axis=-1, keepdims=True)
    a_in = jnp.sum(jnp.where(lane == SMALL_A + h, small, 0.0), axis=-1, keepdims=True)
    bet[...] = jnp.broadcast_to(jax.nn.sigmoid(beta_in), (n_rows, d))
    gl[...] = jnp.broadcast_to(-jnp.exp(scal_ref[0, h]) * _softplus(a_in + scal_ref[1, h]), (n_rows, d))

    ri = lax.broadcasted_iota(jnp.int32, (c, c), 0)
    cj = lax.broadcasted_iota(jnp.int32, (c, c), 1)
    incl = ri >= cj
    strict = ri > cj
    tril = incl.astype(jnp.float32)
    eye = (ri == cj).astype(jnp.float32)

    tril_b = tril.astype(jnp.bfloat16)
    bs = DN_INV_BLOCK
    sh = bs.bit_length() - 1
    same_diag = (ri >> sh) == (cj >> sh)
    level_masks = []
    while (1 << sh) < c:
        level_masks.append(((ri >> (sh + 1)) == (cj >> (sh + 1))) & ((ri >> sh) > (cj >> sh)))
        sh += 1

    bf16 = jnp.bfloat16
    nt_dims = (((1,), (1,)), ((), ()))

    def dot1(a, b):
        return jnp.dot(a.astype(bf16), b.astype(bf16), preferred_element_type=jnp.float32)

    def chunk_prep(it, carry):
        ids = [it * DN_PREP_UNROLL + cc for cc in range(DN_PREP_UNROLL)]
        rows = [pl.ds(pl.multiple_of(i * c, c), c) for i in ids]
        ks = [kn[r, :] for r in rows]
        betas = [bet[r, :] for r in rows]
        gcbs = [sum(jnp.dot(tril_b, part, preferred_element_type=jnp.float32)
                    for part in _split_bf16(gl[r, :], 3)) for r in rows]
        kbs = [k * beta for k, beta in zip(ks, betas)]
        kbfs = [k.astype(bf16) for k in ks]
        a_kks = [lax.dot_general(kb.astype(bf16), kbf, nt_dims, preferred_element_type=jnp.float32)
                 for kb, kbf in zip(kbs, kbfs)]
        decays = []
        for gcb in gcbs:
            gct = jnp.concatenate([gcb, gcb], axis=0).T
            diff = gcb[:, 0:c] - gct[0:c, 0:c]
            decays.append(jnp.where(incl, jnp.exp(jnp.where(incl, diff, 0.0)), 0.0))
        lows = [jnp.where(strict, a * dec, 0.0) for a, dec in zip(a_kks, decays)]
        pws = [jnp.where(same_diag, -low, 0.0) for low in lows]
        es = list(pws)
        for _ in range(max(1, (bs - 1).bit_length()) - 1):
            pws = [dot1(pw, pw) for pw in pws]
            es = [e + pw + dot1(e, pw) for e, pw in zip(es, pws)]
        for below in level_masks:
            offs = [jnp.where(below, low, 0.0) for low in lows]
            xs = [off + dot1(e, off) for e, off in zip(es, offs)]
            es = [e - (x + dot1(x, e)) for e, x in zip(es, xs)]
        egcs = [jnp.exp(gcb) for gcb in gcbs]
        rhss = [jnp.concatenate([vn[r, :] * beta, kb * egc], axis=1)
                for r, beta, kb, egc in zip(rows, betas, kbs, egcs)]
        uws = [rhs + _dot3(e, rhs) for e, rhs in zip(es, rhss)]
        qs = [qn[r, :] for r in rows]
        a_qks = [lax.dot_general(q.astype(bf16), kbf, nt_dims, preferred_element_type=jnp.float32) * dec
                 for q, kbf, dec in zip(qs, kbfs, decays)]
        g_lasts = [gcb[c - 1:c, :] for gcb in gcbs]
        kdec_ts = []
        for k, gcb, g_last in zip(ks, gcbs, g_lasts):
            kdec = k * jnp.exp(g_last - gcb)
            kdec_ts.append(jnp.concatenate([kdec, jnp.zeros_like(kdec)], axis=0).T[:, 0:c].astype(bf16))
        uwbs = [uw.astype(bf16) for uw in uws]
        nms = [jnp.dot(kt, uwb, preferred_element_type=jnp.float32) for kt, uwb in zip(kdec_ts, uwbs)]
        oqs = [jnp.dot(a.astype(bf16), uwb, preferred_element_type=jnp.float32) for a, uwb in zip(a_qks, uwbs)]
        for i, r, nm, oq, q, egc, g_last in zip(ids, rows, nms, oqs, qs, egcs, g_lasts):
            m0 = pl.ds(pl.multiple_of(i * d, d), d)
            nn[m0, :] = nm[:, 0:d]
            mm[m0, :] = nm[:, d:2 * d].astype(bf16)
            oo[r, :] = oq[:, 0:d]
            qq[r, :] = (q * egc - oq[:, d:2 * d]).astype(bf16)
            dd[pl.ds(pl.multiple_of(i * 8, 8), 8), :] = jnp.broadcast_to(jnp.exp(g_last), (8, d))
        return carry

    lax.fori_loop(0, nb * n_chunks // DN_PREP_UNROLL, chunk_prep, 0)

    def chunk_scan(i, states):
        ids = [b * n_chunks + i for b in range(nb)]
        blocks = [pl.ds(pl.multiple_of(j * d, d), d) for j in ids]
        sbs = [s.astype(bf16) for s in states]
        for blk, sb in zip(blocks, sbs):
            ss[blk, :] = sb
        prods = [jnp.dot(mm[blk, :], sb, preferred_element_type=jnp.float32) for blk, sb in zip(blocks, sbs)]
        return tuple(s * dd[pl.ds(pl.multiple_of(j * 8, 8), 1), :] - pr + nn[blk, :]
                     for s, j, blk, pr in zip(states, ids, blocks, prods))

    lax.fori_loop(0, n_chunks, chunk_scan, tuple(jnp.zeros((d, d), jnp.float32) for _ in range(nb)))

    nw = nw_ref[...]

    def chunk_out(it, carry):
        ids = [it * DN_OUT_UNROLL + cc for cc in range(DN_OUT_UNROLL)]
        rows = [pl.ds(pl.multiple_of(i * c, c), c) for i in ids]
        outs = [jnp.dot(qq[r, :], ss[pl.ds(pl.multiple_of(i * d, d), d), :], preferred_element_type=jnp.float32)
                + oo[r, :] for i, r in zip(ids, rows)]
        for r, o in zip(rows, outs):
            o = o * lax.rsqrt(jnp.mean(o * o, axis=-1, keepdims=True) + 1e-6) * nw
            z = z_ref[r, :]
            o_ref[r, :] = (o * (z * jax.nn.sigmoid(z))).astype(o_ref.dtype)
        return carry

    lax.fori_loop(0, nb * n_chunks // DN_OUT_UNROLL, chunk_out, 0)


def _deltanet(pf, conv_w, a_log, dt_bias, norm_w, bsz, seq):
    d = DN_HEAD_DIM
    nb = DN_BATCHES if bsz % DN_BATCHES == 0 else 1
    rows = nb * seq
    n_chunks = rows // DN_CHUNK
    qkv0 = PF_QKVB // d
    scal = jnp.stack([a_log, dt_bias]).astype(jnp.float32)
    f32 = jnp.float32
    bf16 = jnp.bfloat16
    cw4 = conv_w.astype(f32).reshape(DN_CONV, 3, DN_HEADS, d).transpose(2, 1, 0, 3)
    assert 2 * DN_CHUNK == d and n_chunks % DN_PREP_UNROLL == 0 and n_chunks % DN_OUT_UNROLL == 0
    return pl.pallas_call(
        functools.partial(_dn_kernel, seq=seq),
        grid=(bsz // nb, DN_HEADS),
        in_specs=[
            pl.BlockSpec(memory_space=pltpu.SMEM),
            pl.BlockSpec((rows, d), lambda b, h: (b, qkv0 + h)),
            pl.BlockSpec((rows, d), lambda b, h: (b, qkv0 + DN_HEADS + h)),
            pl.BlockSpec((rows, d), lambda b, h: (b, qkv0 + 2 * DN_HEADS + h)),
            pl.BlockSpec((rows, LANES), lambda b, h: (b, PF_SMALL // LANES)),
            pl.BlockSpec((rows, d), lambda b, h: (b, PF_ZB // d + h)),
            pl.BlockSpec((None, 3, DN_CONV, d), lambda b, h: (h, 0, 0, 0)),
            pl.BlockSpec((1, d), lambda b, h: (0, 0)),
        ],
        out_specs=pl.BlockSpec((rows, d), lambda b, h: (b, h)),
        out_shape=jax.ShapeDtypeStruct((bsz * seq, DN_WIDTH), jnp.bfloat16),
        scratch_shapes=[
            pltpu.VMEM((rows, d), f32), pltpu.VMEM((rows, d), f32), pltpu.VMEM((rows, d), f32),
            pltpu.VMEM((rows, d), f32), pltpu.VMEM((rows, d), f32),
            pltpu.VMEM((n_chunks * d, d), bf16), pltpu.VMEM((n_chunks * d, d), f32),
            pltpu.VMEM((rows, d), bf16), pltpu.VMEM((rows, d), f32),
            pltpu.VMEM((n_chunks * 8, d), f32),
            pltpu.VMEM((n_chunks * d, d), bf16),
        ],
        compiler_params=_cparams(2), name="deltanet",
    )(scal, pf, pf, pf, pf, pf, cw4, norm_w.astype(f32).reshape(1, d))


def _out_kernel(oa_ref, ob_ref, gma_ref, gmb_ref, x_ref, p_ref, wa_ref, wb_ref, wo_ref, wpg_ref, wp_ref,
                lng_ref, lnb_ref, o_ref):
    f32 = jnp.float32
    y_a = jnp.dot(oa_ref[...], wa_ref[...], preferred_element_type=f32)
    y_b = jnp.dot(ob_ref[...], wb_ref[...], preferred_element_type=f32)
    mix = jax.nn.sigmoid(gma_ref[...]) * y_a + jax.nn.sigmoid(gmb_ref[...]) * y_b
    mixed = jnp.dot(mix.astype(jnp.bfloat16), wo_ref[...], preferred_element_type=f32)
    h = DEEPNORM_ALPHA * x_ref[...] + mixed
    gate = jax.nn.sigmoid(jnp.dot(h.astype(jnp.bfloat16), wpg_ref[...], preferred_element_type=f32))
    h = h + gate * jnp.dot(p_ref[...].astype(jnp.bfloat16), wp_ref[...], preferred_element_type=f32)
    mu = jnp.mean(h, axis=-1, keepdims=True)
    hc = h - mu
    var = jnp.mean(hc * hc, axis=-1, keepdims=True)
    o_ref[...] = (hc * lax.rsqrt(var + 1e-5) * lng_ref[...] + lnb_ref[...]).astype(o_ref.dtype)


def _out_block(o_a, o_b, pf, x2, p2, wa, wb, wo, wpg, wp, ln_g, ln_b, tm):
    t = x2.shape[0]
    bf = jnp.bfloat16

    def full(shape):
        return pl.BlockSpec(shape, lambda i: (0, 0))

    return pl.pallas_call(
        _out_kernel,
        grid=(t // tm,),
        in_specs=[
            pl.BlockSpec((tm, NSA_WIDTH), lambda i: (i, 0)),
            pl.BlockSpec((tm, DN_WIDTH), lambda i: (i, 0)),
            pl.BlockSpec((tm, D_MODEL), lambda i: (i, PF_GM // D_MODEL)),
            pl.BlockSpec((tm, D_MODEL), lambda i: (i, PF_GM // D_MODEL + 1)),
            pl.BlockSpec((tm, D_MODEL), lambda i: (i, 0)),
            pl.BlockSpec((tm, PLE_DIM), lambda i: (i, 0)),
            full((NSA_WIDTH, D_MODEL)), full((DN_WIDTH, D_MODEL)), full((D_MODEL, D_MODEL)),
            full((D_MODEL, D_MODEL)), full((PLE_DIM, D_MODEL)), full((1, D_MODEL)), full((1, D_MODEL)),
        ],
        out_specs=pl.BlockSpec((tm, D_MODEL), lambda i: (i, 0)),
        out_shape=jax.ShapeDtypeStruct((t, D_MODEL), x2.dtype),
        compiler_params=_cparams(1), name="out_block",
    )(o_a, o_b, pf, pf, x2, p2, wa.astype(bf), wb.astype(bf), wo.astype(bf), wpg.astype(bf), wp.astype(bf),
      ln_g.astype(jnp.float32).reshape(1, D_MODEL), ln_b.astype(jnp.float32).reshape(1, D_MODEL))


def _layer(x, p, w_in, pos_k, pos_v, w1_k, w2_k, w1_v, w2_v, bias_tabs, conv_w, a_log, dt_bias, norm_w,
           w_a, w_b, w_o, w_ple, w_pg, ln_g, ln_b):
    bsz, seq, _ = x.shape
    t = bsz * seq
    x2 = x.reshape(t, D_MODEL)
    xb = x2.astype(jnp.bfloat16)
    wb16, wf16 = _prep_w_in(w_in)
    tm = PROJ_TM if t % PROJ_TM == 0 else seq
    pb = _matmul(xb, wb16, jnp.bfloat16, tm, PB_WIDTH // 2, "proj_bf16")
    pf = _matmul(xb, wf16, jnp.float32, tm, PROJ_TN, "proj_f32")

    nc = seq // CMP_STRIDE
    rkv = pf[:, PF_KC:PF_KC + 2 * NSA_KV].reshape(bsz, nc, CMP_STRIDE, 2, NSA_KV)
    rkv = rkv.transpose(0, 3, 1, 2, 4).reshape(bsz, 2, nc, CMP_STRIDE * NSA_KV)
    pos2, w1p, w2p = _prep_compress_weights(pos_k, pos_v, w1_k, w1_v, w2_k, w2_v)
    kcvc = _compress(rkv, pos2, w1p, w2p)

    bias_sel, bias_win, bias_cmp = bias_tabs
    o_a = _nsa(pb, pf, kcvc, bias_sel, bias_win, bias_cmp, bsz, seq)
    o_b = _deltanet(pf, conv_w, a_log, dt_bias, norm_w, bsz, seq)
    out = _out_block(o_a, o_b, pf, x2, p.reshape(t, PLE_DIM), w_a, w_b, w_o, w_pg, w_ple, ln_g, ln_b,
                     OUT_TM if t % OUT_TM == 0 else seq)
    return out.reshape(bsz, seq, D_MODEL)


def kernel(x, p, w_in, cmp_pos_k, cmp_pos_v, cmp_w1_k, cmp_w2_k, cmp_w1_v, cmp_w2_v, rel_bias, dn_conv_w,
           dn_a_log, dn_dt_bias, dn_norm_w, w_branch_a, w_branch_b, w_out, w_ple, w_ple_gate, ln_g, ln_b):
    depth = w_in.shape[0]
    bias_tabs = _bias_tables(rel_bias, x.shape[1])
    for i in range(depth):
        x = _layer(x, p[i], w_in[i], cmp_pos_k[i], cmp_pos_v[i], cmp_w1_k[i], cmp_w2_k[i], cmp_w1_v[i],
                   cmp_w2_v[i], bias_tabs, dn_conv_w[i], dn_a_log[i], dn_dt_bias[i], dn_norm_w[i],
                   w_branch_a[i], w_branch_b[i], w_out[i], w_ple[i], w_ple_gate[i], ln_g[i], ln_b[i])
    return x
```

```python
import functools
import math

import numpy as np
import jax
import jax.numpy as jnp
from jax import lax
from jax.experimental import pallas as pl
from jax.experimental.pallas import tpu as pltpu

D_MODEL = 1024
PLE_DIM = 256
NSA_HEADS = 8
NSA_GROUPS = 2
NSA_HPG = NSA_HEADS // NSA_GROUPS
NSA_HEAD_DIM = 64
NSA_WIDTH = NSA_HEADS * NSA_HEAD_DIM
NSA_KV = NSA_GROUPS * NSA_HEAD_DIM
CMP_BLOCK = 32
CMP_STRIDE = 16
CMP_HIDDEN = 256
SEL_BLOCK = 64
SEL_TOPK = 8
WINDOW = 512
DN_HEADS = 4
DN_HEAD_DIM = 128
DN_WIDTH = DN_HEADS * DN_HEAD_DIM
DN_CONV = 4
DN_CHUNK = 64
NUM_BUCKETS = 32
REL_MAX_DIST = 1024
DEEPNORM_ALPHA = 2.0 ** 0.25
NEG = -1e30
FORCE = 1e6
LOG2E = 1.4426950408889634

LANES = 128
TILE = 128
SEL_LANE0 = 64
SEL_GROUP = 4
VMEM_LIMIT = 56 * 1024 * 1024
PROJ_TM = 2048
PROJ_TN = 1024
OUT_TM = 512
DN_INV_BLOCK = 16
DN_BATCHES = 2
DN_OUT_UNROLL = 8
DN_PREP_UNROLL = 8

HIGHEST = lax.Precision.HIGHEST

PB_QP = 0
PB_KS = PB_QP + NSA_HEADS * LANES
PB_KW = PB_KS + NSA_GROUPS * LANES
PB_VS = PB_KW + NSA_GROUPS * LANES
PB_VW = PB_VS + LANES
PB_WIDTH = PB_VW + LANES
PF_KC = 0
PF_VC = 128
PF_SMALL = 256
PF_ZA = 512
PF_ZB = 1024
PF_QKVB = 1536
PF_GM = 3072
PF_WIDTH = 5120
SMALL_BETA = 3 * NSA_HEADS
SMALL_A = SMALL_BETA + DN_HEADS


def _bucket_thresholds():
    max_exact = NUM_BUCKETS // 2
    span = NUM_BUCKETS - max_exact
    ratio = REL_MAX_DIST // max_exact
    thr = list(range(1, max_exact + 1))
    for k in range(1, span):
        n = max_exact
        while n ** span < max_exact ** span * ratio ** k:
            n += 1
        thr.append(n)
    return tuple(thr)


_THR = _bucket_thresholds()


def _cparams(n_axes):
    return pltpu.CompilerParams(dimension_semantics=("arbitrary",) * n_axes, vmem_limit_bytes=VMEM_LIMIT)


def _mm_kernel(x_ref, w_ref, o_ref):
    o_ref[...] = jnp.dot(x_ref[...], w_ref[...], preferred_element_type=jnp.float32).astype(o_ref.dtype)


def _matmul(x, w, out_dtype, tm, tn, name):
    m, k = x.shape
    n = w.shape[1]
    return pl.pallas_call(
        _mm_kernel,
        grid=(n // tn, m // tm),
        in_specs=[pl.BlockSpec((tm, k), lambda j, i: (i, 0)),
                  pl.BlockSpec((k, tn), lambda j, i: (0, j))],
        out_specs=pl.BlockSpec((tm, tn), lambda j, i: (i, j)),
        out_shape=jax.ShapeDtypeStruct((m, n), out_dtype),
        compiler_params=_cparams(2), name=name,
    )(x, w)


def _prep_w_in(w):
    d = w.shape[0]
    o = 0
    wq = w[:, o:o + NSA_WIDTH]; o += NSA_WIDTH
    wkv = w[:, o:o + 6 * NSA_KV]; o += 6 * NSA_KV
    wg = w[:, o:o + 3 * NSA_HEADS]; o += 3 * NSA_HEADS
    wza = w[:, o:o + NSA_WIDTH]; o += NSA_WIDTH
    wqkvb = w[:, o:o + 3 * DN_WIDTH]; o += 3 * DN_WIDTH
    wbeta = w[:, o:o + DN_HEADS]; o += DN_HEADS
    wa = w[:, o:o + DN_HEADS]; o += DN_HEADS
    wzb = w[:, o:o + DN_WIDTH]; o += DN_WIDTH
    wgm = w[:, o:o + 2 * D_MODEL]
    wkc, wvc, wks, wvs, wkw, wvw = [wkv[:, i * NSA_KV:(i + 1) * NSA_KV] for i in range(6)]

    def pad_heads(t, nh):
        t = t.reshape(d, nh, NSA_HEAD_DIM)
        return jnp.concatenate([t, jnp.zeros_like(t)], axis=-1).reshape(d, nh * LANES)

    wb = jnp.concatenate([pad_heads(wq * (NSA_HEAD_DIM ** -0.5 * LOG2E), NSA_HEADS), pad_heads(wks, NSA_GROUPS),
                          pad_heads(wkw, NSA_GROUPS), wvs, wvw], axis=1).astype(jnp.bfloat16)
    small = jnp.concatenate([wg, wbeta, wa, jnp.zeros((d, LANES - SMALL_A - DN_HEADS), w.dtype)], axis=1)
    wf = jnp.concatenate([wkc, wvc, small, jnp.zeros((d, LANES), w.dtype), wza, wzb, wqkvb, wgm],
                         axis=1).astype(jnp.bfloat16)
    return wb, wf


def _bias_kernel(tab_ref, sel_ref, win_ref, cmp_ref, *, n_cmp):
    h = pl.program_id(0)

    def lookup(n):
        val = jnp.full(n.shape, tab_ref[0, h], jnp.float32)
        for b in range(1, NUM_BUCKETS):
            val = jnp.where(n >= _THR[b - 1], tab_ref[b, h], val)
        return val * LOG2E

    kj = lax.broadcasted_iota(jnp.int32, (TILE, TILE), 0)
    qi = lax.broadcasted_iota(jnp.int32, (TILE, TILE), 1)
    n_sel_tiles = sel_ref.shape[1] - 1
    n_win_tiles = win_ref.shape[1] - 1
    for dt in range(max(n_sel_tiles, n_win_tiles)):
        dist = dt * TILE + qi - kj
        v = lookup(jnp.maximum(dist, 0))
        if dt < n_sel_tiles:
            sel_ref[0, dt] = jnp.where(dist >= 0, v, NEG)
        if dt < n_win_tiles:
            win_ref[0, dt] = jnp.where((dist >= 0) & (dist < WINDOW), v, NEG)
    sel_ref[0, n_sel_tiles] = jnp.full((TILE, TILE), NEG, jnp.float32)
    win_ref[0, n_win_tiles] = jnp.full((TILE, TILE), NEG, jnp.float32)
    c = lax.broadcasted_iota(jnp.int32, cmp_ref.shape[2:], 0)
    for t in range(cmp_ref.shape[1]):
        s = t * TILE + lax.broadcasted_iota(jnp.int32, cmp_ref.shape[2:], 1)
        dist = s - (c * CMP_STRIDE + CMP_BLOCK - 1)
        cmp_ref[0, t] = jnp.where((dist >= 0) & (c < n_cmp), lookup(jnp.maximum(dist, 0)), NEG)


def _bias_tables(rel_bias, seq):
    nq = seq // TILE
    nwin = WINDOW // TILE + 1
    nc = seq // CMP_STRIDE
    n_cmp = nc - CMP_BLOCK // CMP_STRIDE + 1
    return pl.pallas_call(
        functools.partial(_bias_kernel, n_cmp=n_cmp),
        grid=(NSA_HEADS,),
        in_specs=[pl.BlockSpec(memory_space=pltpu.SMEM)],
        out_specs=[pl.BlockSpec((1, nq + 1, TILE, TILE), lambda h: (h, 0, 0, 0)),
                   pl.BlockSpec((1, nwin + 1, TILE, TILE), lambda h: (h, 0, 0, 0)),
                   pl.BlockSpec((1, nq, nc, TILE), lambda h: (h, 0, 0, 0))],
        out_shape=[jax.ShapeDtypeStruct((NSA_HEADS, nq + 1, TILE, TILE), jnp.float32),
                   jax.ShapeDtypeStruct((NSA_HEADS, nwin + 1, TILE, TILE), jnp.float32),
                   jax.ShapeDtypeStruct((NSA_HEADS, nq, nc, TILE), jnp.float32)],
        compiler_params=_cparams(1), name="bias_tables",
    )(rel_bias.astype(jnp.float32))


def _gelu_tanh(x):
    return x * (0.5 * (1.0 + jnp.tanh(math.sqrt(2.0 / math.pi) * (x + 0.044715 * (x * x * x)))))


def _compress_kernel(r_ref, pos_ref, w1_ref, w2_ref, o_ref):
    r = r_ref[0, 0]
    nc = r.shape[0]
    a = jnp.dot((r + pos_ref[0, 0:1, :]).astype(jnp.bfloat16), w1_ref[0, 0], preferred_element_type=jnp.float32)
    b = jnp.dot((r + pos_ref[0, 1:2, :]).astype(jnp.bfloat16), w1_ref[0, 1], preferred_element_type=jnp.float32)
    hid = a + pltpu.roll(b, nc - 1, 0)
    o_ref[0, 0] = jnp.dot(_gelu_tanh(hid).astype(jnp.bfloat16), w2_ref[0], preferred_element_type=jnp.float32)


def _compress(rkv, pos2, w1p, w2p):
    bsz, _, nc, width = rkv.shape
    hid = w1p.shape[-1]
    return pl.pallas_call(
        _compress_kernel,
        grid=(2, bsz),
        in_specs=[pl.BlockSpec((1, 1, nc, width), lambda k, b: (b, k, 0, 0)),
                  pl.BlockSpec((1, 2, width), lambda k, b: (k, 0, 0)),
                  pl.BlockSpec((1, 2, width, hid), lambda k, b: (k, 0, 0, 0)),
                  pl.BlockSpec((1, hid, LANES), lambda k, b: (k, 0, 0))],
        out_specs=pl.BlockSpec((1, 1, nc, LANES), lambda k, b: (b, k, 0, 0)),
        out_shape=jax.ShapeDtypeStruct((bsz, 2, nc, LANES), jnp.float32),
        compiler_params=_cparams(2), name="compress",
    )(rkv, pos2, w1p, w2p)


def _prep_compress_weights(pos_k, pos_v, w1_k, w1_v, w2_k, w2_v):
    eye = jnp.eye(NSA_GROUPS, dtype=jnp.float32)
    half = CMP_BLOCK // 2

    def w1_both(w1):
        w = w1.reshape(2, half, NSA_HEAD_DIM, CMP_HIDDEN)
        w = jnp.einsum('aldj,gh->algdhj', w, eye)
        return w.reshape(2, half * NSA_GROUPS * NSA_HEAD_DIM, NSA_GROUPS * CMP_HIDDEN)

    def w2_both(w2):
        return jnp.einsum('jd,gh->gjhd', w2, eye).reshape(NSA_GROUPS * CMP_HIDDEN, NSA_GROUPS * NSA_HEAD_DIM)

    def pos_both(pos):
        p = pos.reshape(2, half, 1, NSA_HEAD_DIM)
        return jnp.broadcast_to(p, (2, half, NSA_GROUPS, NSA_HEAD_DIM)).reshape(2, half * NSA_KV)

    pos2 = jnp.stack([pos_both(pos_k), pos_both(pos_v)]).astype(jnp.float32)
    w1p = jnp.stack([w1_both(w1_k), w1_both(w1_v)]).astype(jnp.bfloat16)
    w2p = jnp.stack([w2_both(w2_k), w2_both(w2_v)]).astype(jnp.bfloat16)
    return pos2, w1p, w2p


def _pair_lanes(x, g):
    sw = pltpu.roll(x, LANES // 2, 1)
    lane = lax.broadcasted_iota(jnp.int32, x.shape, 1)
    own = (lane < LANES // 2) == (g == 0)
    return jnp.where(own, x, sw)


def _group_rows_t(x, g):
    xt = x.T
    half = LANES // 2
    return jnp.where(g == 0, xt[0:half, :], xt[half:LANES, :])


def _nsa_kernel(q_ref, ks_ref, kw_ref, vs_ref, vw_ref, kcvc_ref, bc_ref, bsel_ref, bwin_ref,
                small_ref, za_ref, o_ref,
                ksa, kwp, vst, vwt, kcs, vct, qa, qp, sbuf, acc_s, l_s, *, n_cmp, n_sel, n_top):
    g = pl.program_id(1)
    qt = pl.program_id(2)
    tq = TILE
    seq = ks_ref.shape[0]
    nq = seq // TILE
    nc = kcvc_ref.shape[2]
    n_wt = bwin_ref.shape[1] - 1
    heads = range(NSA_HPG)
    f32 = jnp.float32
    bf16 = jnp.bfloat16
    nt_dims = (((1,), (1,)), ((), ()))

    @pl.when(qt == 0)
    def _():
        row = lax.broadcasted_iota(jnp.int32, (seq, LANES), 0)
        lane = lax.broadcasted_iota(jnp.int32, (seq, LANES), 1)
        onehot = (lane - SEL_LANE0) == (row >> 6)
        ksa[...] = jnp.where(onehot, 1.0, ks_ref[...].astype(f32)).astype(bf16)
        pad = (n_wt - 1) * TILE
        kwp[0:pad, :] = jnp.zeros((pad, LANES), bf16)
        kwp[pad:pad + seq, :] = kw_ref[...]
        for kt in range(n_wt - 1):
            vwt[kt] = jnp.zeros(vwt.shape[1:], bf16)
        for kt in range(nq):
            rows = slice(kt * TILE, (kt + 1) * TILE)
            vst[kt] = _group_rows_t(vs_ref[rows, :].astype(f32), g).astype(bf16)
            vwt[kt + n_wt - 1] = _group_rows_t(vw_ref[rows, :].astype(f32), g).astype(bf16)
        kcs[...] = _pair_lanes(kcvc_ref[0, 0], g).astype(bf16)
        for ct in range(nc // TILE):
            rows = slice(ct * TILE, (ct + 1) * TILE)
            vct[:, rows] = _group_rows_t(kcvc_ref[0, 1, rows, :], g).astype(bf16)

    qh = [q_ref[:, hh * LANES:(hh + 1) * LANES] for hh in heads]
    for hh in heads:
        qp[hh * tq:(hh + 1) * tq, :] = qh[hh]
    q_all = qp[...]

    def softmax_pv(s, v_t):
        m = jnp.max(s, axis=0, keepdims=True)
        p = jnp.exp2(s - m)
        return jnp.dot(v_t, p.astype(bf16), preferred_element_type=f32), jnp.sum(p, axis=0, keepdims=True)

    s_c = lax.dot_general(kcs[...], q_all, nt_dims, preferred_element_type=f32)
    kw_rows = kwp[pl.ds(pl.multiple_of(qt * TILE, TILE), n_wt * TILE), :]
    s_w = lax.dot_general(kw_rows, q_all, nt_dims, preferred_element_type=f32)

    bias_c = jnp.concatenate([bc_ref[hh] for hh in heads], axis=1)
    valid = bias_c > 0.5 * NEG
    s_c = s_c + bias_c
    e = jnp.where(valid, jnp.exp2(s_c - jnp.max(s_c, axis=0, keepdims=True)), 0.0)
    den = jnp.maximum(jnp.sum(e, axis=0, keepdims=True), 1e-30)
    p_c = e * (1.0 / den)
    o_cmp = jnp.dot(vct[...], p_c.astype(bf16), preferred_element_type=f32)
    psum = sum(p_c[:, hh * tq:(hh + 1) * tq] for hh in heads)

    w_tiles = []
    for t in range(n_wt):
        dt = n_wt - 1 - t
        idx = jnp.where(qt >= dt, dt, n_wt)
        w_tiles.append(jnp.concatenate([bwin_ref[hh, idx] for hh in heads], axis=1))
    acc_w, l_w = softmax_pv(s_w + jnp.concatenate(w_tiles, axis=0),
                            jnp.concatenate([vwt[qt + t] for t in range(n_wt)], axis=1))
    o_win = acc_w * (1.0 / l_w)

    sj = lax.broadcasted_iota(jnp.int32, (n_sel, nc), 0)
    ci = lax.broadcasted_iota(jnp.int32, (n_sel, nc), 1)
    overlap = ((ci * CMP_STRIDE < (sj + 1) * SEL_BLOCK) & (ci * CMP_STRIDE + CMP_BLOCK > sj * SEL_BLOCK)
               & (ci < n_cmp)).astype(bf16)
    imp_t = sum(jnp.dot(overlap, part, preferred_element_type=f32)
                for part in _split_bf16(psum, 3))
    blk = lax.broadcasted_iota(jnp.int32, (n_sel, tq), 0)
    cur = (qt * tq + lax.broadcasted_iota(jnp.int32, (n_sel, tq), 1)) >> 6
    forced = (blk == 0) | (blk == cur) | (blk == cur - 1)
    imp_t = jnp.where(forced, FORCE, jnp.where(blk > cur, -FORCE, imp_t))
    rank = jnp.zeros((n_sel, tq), jnp.int32)
    for j in range(n_sel):
        other = imp_t[j:j + 1, :]
        ahead = (other > imp_t) | ((other == imp_t) & (blk > j))
        rank = rank + ahead.astype(jnp.int32)
    selb = jnp.where(rank < n_top, 0.0, NEG)
    pieces = [jnp.zeros((SEL_LANE0, tq), jnp.float32), selb]
    if LANES - SEL_LANE0 - n_sel > 0:
        pieces.append(jnp.zeros((LANES - SEL_LANE0 - n_sel, tq), jnp.float32))
    selb_r = jnp.concatenate(pieces, axis=0).T.astype(jnp.bfloat16)

    for hh in heads:
        qa[hh * tq:(hh + 1) * tq, :] = qh[hh] + selb_r
    gk = SEL_GROUP * TILE

    def sel_case(n_groups):
        q_aug = qa[...]
        col_max = []
        for gi in range(n_groups):
            s = lax.dot_general(ksa[gi * gk:(gi + 1) * gk, :], q_aug, nt_dims, preferred_element_type=f32)
            tiles = []
            for t in range(SEL_GROUP):
                dt = qt - (gi * SEL_GROUP + t)
                idx = jnp.where(dt >= 0, dt, nq)
                tiles.append(jnp.concatenate([bsel_ref[hh, idx] for hh in heads], axis=1))
            s = s + jnp.concatenate(tiles, axis=0)
            sbuf[gi] = s
            col_max.append(jnp.max(s, axis=0, keepdims=True))
        m = functools.reduce(jnp.maximum, col_max)
        acc = jnp.zeros(acc_s.shape, f32)
        l = jnp.zeros(l_s.shape, f32)
        for gi in range(n_groups):
            p = jnp.exp2(sbuf[gi] - m)
            v_t = jnp.concatenate([vst[gi * SEL_GROUP + t] for t in range(SEL_GROUP)], axis=1)
            acc = acc + jnp.dot(v_t, p.astype(bf16), preferred_element_type=f32)
            l = l + jnp.sum(p, axis=0, keepdims=True)
        acc_s[...] = acc
        l_s[...] = l

    for n_groups in range(1, nq // SEL_GROUP + 1):
        pl.when(qt // SEL_GROUP == n_groups - 1)(functools.partial(sel_case, n_groups))
    o_sel = acc_s[...] * (1.0 / l_s[...])

    gates_t = jax.nn.sigmoid(small_ref[...]).T
    mixed = []
    for hh in heads:
        def gate(br):
            r = br * NSA_HEADS + hh
            return jnp.where(g == 0, gates_t[r:r + 1, :], gates_t[r + NSA_HPG:r + NSA_HPG + 1, :])
        cols = slice(hh * tq, (hh + 1) * tq)
        mixed.append(gate(0) * o_cmp[:, cols] + gate(1) * o_sel[:, cols] + gate(2) * o_win[:, cols])
    o = jnp.concatenate([jnp.concatenate(mixed[2 * j:2 * j + 2], axis=0).T for j in range(NSA_HPG // 2)], axis=1)
    z = za_ref[...]
    o_ref[...] = (o * (z * jax.nn.sigmoid(z))).astype(o_ref.dtype)


def _nsa_step_kernel(q_ref, qn_ref, ks_ref, kw_ref, vs_ref, vw_ref, kcvc_ref, bc_ref, bsel_ref, bwin_ref,
                     small_ref, za_ref, o_ref,
                     ksa, kwp, vst, vwt, kcs, vct, qa_s, qp_s, ocmp_s, sbuf, *, n_cmp, n_sel, n_top):
    g = pl.program_id(1)
    qt = pl.program_id(2)
    tq = TILE
    seq = ks_ref.shape[0]
    nq = seq // TILE
    nc = kcvc_ref.shape[2]
    n_wt = bwin_ref.shape[1] - 1
    heads = range(NSA_HPG)
    f32 = jnp.float32
    bf16 = jnp.bfloat16
    nt_dims = (((1,), (1,)), ((), ()))
    gk = SEL_GROUP * TILE

    def select_tile(src_ref, tile, slot):
        qh = [src_ref[:, hh * LANES:(hh + 1) * LANES] for hh in heads]
        for hh in heads:
            qp_s[slot, hh * tq:(hh + 1) * tq, :] = qh[hh]
        q_all = qp_s[slot]
        s_c = lax.dot_general(kcs[...], q_all, nt_dims, preferred_element_type=f32)
        bias_c = jnp.concatenate([bc_ref[hh, tile] for hh in heads], axis=1)
        valid = bias_c > 0.5 * NEG
        s_c = s_c + bias_c
        e = jnp.where(valid, jnp.exp2(s_c - jnp.max(s_c, axis=0, keepdims=True)), 0.0)
        den = jnp.maximum(jnp.sum(e, axis=0, keepdims=True), 1e-30)
        p_c = e * (1.0 / den)
        psum = sum(p_c[:, hh * tq:(hh + 1) * tq] for hh in heads)
        sj = lax.broadcasted_iota(jnp.int32, (n_sel, nc), 0)
        ci = lax.broadcasted_iota(jnp.int32, (n_sel, nc), 1)
        overlap = ((ci * CMP_STRIDE < (sj + 1) * SEL_BLOCK) & (ci * CMP_STRIDE + CMP_BLOCK > sj * SEL_BLOCK)
                   & (ci < n_cmp)).astype(bf16)
        imp_t = sum(jnp.dot(overlap, part, preferred_element_type=f32)
                    for part in _split_bf16(psum, 3))
        ocmp_s[slot] = jnp.dot(vct[...], p_c.astype(bf16), preferred_element_type=f32)
        blk = lax.broadcasted_iota(jnp.int32, (n_sel, tq), 0)
        cur = (tile * tq + lax.broadcasted_iota(jnp.int32, (n_sel, tq), 1)) >> 6
        forced = (blk == 0) | (blk == cur) | (blk == cur - 1)
        imp_t = jnp.where(forced, FORCE, jnp.where(blk > cur, -FORCE, imp_t))
        rank = jnp.zeros((n_sel, tq), jnp.int32)
        for j in range(n_sel):
            other = imp_t[j:j + 1, :]
            ahead = (other > imp_t) | ((other == imp_t) & (blk > j))
            rank = rank + ahead.astype(jnp.int32)
        selb = jnp.where(rank < n_top, 0.0, NEG)
        pieces = [jnp.zeros((SEL_LANE0, tq), f32), selb]
        if LANES - SEL_LANE0 - n_sel > 0:
            pieces.append(jnp.zeros((LANES - SEL_LANE0 - n_sel, tq), f32))
        selb_r = jnp.concatenate(pieces, axis=0).T.astype(bf16)
        for hh in heads:
            qa_s[slot, hh * tq:(hh + 1) * tq, :] = qh[hh] + selb_r

    @pl.when(qt == 0)
    def _():
        row = lax.broadcasted_iota(jnp.int32, (seq, LANES), 0)
        lane = lax.broadcasted_iota(jnp.int32, (seq, LANES), 1)
        onehot = (lane - SEL_LANE0) == (row >> 6)
        ksa[...] = jnp.where(onehot, 1.0, ks_ref[...].astype(f32)).astype(bf16)
        pad = (n_wt - 1) * TILE
        kwp[0:pad, :] = jnp.zeros((pad, LANES), bf16)
        kwp[pad:pad + seq, :] = kw_ref[...]
        for kt in range(n_wt - 1):
            vwt[kt] = jnp.zeros(vwt.shape[1:], bf16)
        for kt in range(nq):
            rows = slice(kt * TILE, (kt + 1) * TILE)
            vst[kt] = _group_rows_t(vs_ref[rows, :].astype(f32), g).astype(bf16)
            vwt[kt + n_wt - 1] = _group_rows_t(vw_ref[rows, :].astype(f32), g).astype(bf16)
        kcs[...] = _pair_lanes(kcvc_ref[0, 0], g).astype(bf16)
        for ct in range(nc // TILE):
            rows = slice(ct * TILE, (ct + 1) * TILE)
            vct[:, rows] = _group_rows_t(kcvc_ref[0, 1, rows, :], g).astype(bf16)
        select_tile(q_ref, 0, 0)

    slot = qt % 2

    def step(n_groups):
        q_all = qp_s[slot]
        q_aug = qa_s[slot]
        kw_rows = kwp[pl.ds(pl.multiple_of(qt * TILE, TILE), n_wt * TILE), :]
        s_w = lax.dot_general(kw_rows, q_all, nt_dims, preferred_element_type=f32)

        def sel_scores(gi):
            s = lax.dot_general(ksa[gi * gk:(gi + 1) * gk, :], q_aug, nt_dims, preferred_element_type=f32)
            tiles = []
            for t in range(SEL_GROUP):
                dt = qt - (gi * SEL_GROUP + t)
                idx = jnp.where(dt >= 0, dt, nq)
                tiles.append(jnp.concatenate([bsel_ref[hh, idx] for hh in heads], axis=1))
            sbuf[gi % 2] = s + jnp.concatenate(tiles, axis=0)

        sel_scores(0)
        select_tile(qn_ref, jnp.minimum(qt + 1, nq - 1), 1 - slot)

        w_tiles = []
        for t in range(n_wt):
            dt = n_wt - 1 - t
            idx = jnp.where(qt >= dt, dt, n_wt)
            w_tiles.append(jnp.concatenate([bwin_ref[hh, idx] for hh in heads], axis=1))
        s_w = s_w + jnp.concatenate(w_tiles, axis=0)
        p_w = jnp.exp2(s_w - jnp.max(s_w, axis=0, keepdims=True))
        v_w = jnp.concatenate([vwt[qt + t] for t in range(n_wt)], axis=1)
        o_win = (jnp.dot(v_w, p_w.astype(bf16), preferred_element_type=f32)
                 * (1.0 / jnp.sum(p_w, axis=0, keepdims=True)))

        m = jnp.full((1, NSA_HPG * tq), NEG, f32)
        l = jnp.zeros((1, NSA_HPG * tq), f32)
        acc = jnp.zeros((NSA_HEAD_DIM, NSA_HPG * tq), f32)
        for gi in range(n_groups):
            if gi + 1 < n_groups:
                sel_scores(gi + 1)
            s = sbuf[gi % 2]
            m_new = jnp.maximum(m, jnp.max(s, axis=0, keepdims=True))
            alpha = jnp.exp2(m - m_new)
            p = jnp.exp2(s - m_new)
            v_t = jnp.concatenate([vst[gi * SEL_GROUP + t] for t in range(SEL_GROUP)], axis=1)
            acc = alpha * acc + jnp.dot(v_t, p.astype(bf16), preferred_element_type=f32)
            l = alpha * l + jnp.sum(p, axis=0, keepdims=True)
            m = m_new
        o_sel = acc * (1.0 / l)
        o_cmp = ocmp_s[slot]

        gates_t = jax.nn.sigmoid(small_ref[...]).T
        mixed = []
        for hh in heads:
            def gate(br):
                r = br * NSA_HEADS + hh
                return jnp.where(g == 0, gates_t[r:r + 1, :], gates_t[r + NSA_HPG:r + NSA_HPG + 1, :])
            cols = slice(hh * tq, (hh + 1) * tq)
            mixed.append(gate(0) * o_cmp[:, cols] + gate(1) * o_sel[:, cols] + gate(2) * o_win[:, cols])
        o = jnp.concatenate([jnp.concatenate(mixed[2 * j:2 * j + 2], axis=0).T for j in range(NSA_HPG // 2)],
                            axis=1)
        z = za_ref[...]
        o_ref[...] = (o * (z * jax.nn.sigmoid(z))).astype(o_ref.dtype)

    for n_groups in range(1, nq // SEL_GROUP + 1):
        pl.when(qt // SEL_GROUP == n_groups - 1)(functools.partial(step, n_groups))


def _nsa(pb, pf, kcvc, bias_sel, bias_win, bias_cmp, bsz, seq):
    nq = seq // TILE
    nc = seq // CMP_STRIDE
    n_cmp = nc - CMP_BLOCK // CMP_STRIDE + 1
    n_sel = seq // SEL_BLOCK
    n_top = min(SEL_TOPK, n_sel)
    nwin = bias_win.shape[1]
    assert nq % SEL_GROUP == 0 and nc % TILE == 0 and bias_sel.shape[1] == nq + 1
    gw = NSA_HPG * NSA_HEAD_DIM
    kern = functools.partial(_nsa_step_kernel, n_cmp=n_cmp, n_sel=n_sel, n_top=n_top)
    return pl.pallas_call(
        kern,
        grid=(bsz, NSA_GROUPS, nq),
        in_specs=[
            pl.BlockSpec((TILE, NSA_HPG * LANES), lambda b, g, t: (b * nq + t, g)),
            pl.BlockSpec((TILE, NSA_HPG * LANES),
                         lambda b, g, t: (b * nq + jnp.minimum(t + 1, nq - 1), g)),
            pl.BlockSpec((seq, LANES), lambda b, g, t: (b, PB_KS // LANES + g)),
            pl.BlockSpec((seq, LANES), lambda b, g, t: (b, PB_KW // LANES + g)),
            pl.BlockSpec((seq, LANES), lambda b, g, t: (b, PB_VS // LANES)),
            pl.BlockSpec((seq, LANES), lambda b, g, t: (b, PB_VW // LANES)),
            pl.BlockSpec((1, 2, nc, LANES), lambda b, g, t: (b, 0, 0, 0)),
            pl.BlockSpec((NSA_HPG, nq, nc, TILE), lambda b, g, t: (g, 0, 0, 0)),
            pl.BlockSpec((NSA_HPG, nq + 1, TILE, TILE), lambda b, g, t: (g, 0, 0, 0)),
            pl.BlockSpec((NSA_HPG, nwin, TILE, TILE), lambda b, g, t: (g, 0, 0, 0)),
            pl.BlockSpec((TILE, LANES), lambda b, g, t: (b * nq + t, PF_SMALL // LANES)),
            pl.BlockSpec((TILE, gw), lambda b, g, t: (b * nq + t, PF_ZA // gw + g)),
        ],
        out_specs=pl.BlockSpec((TILE, gw), lambda b, g, t: (b * nq + t, g)),
        out_shape=jax.ShapeDtypeStruct((bsz * seq, NSA_WIDTH), jnp.bfloat16),
        scratch_shapes=[
            pltpu.VMEM((seq, LANES), jnp.bfloat16),
            pltpu.VMEM((seq + (nwin - 2) * TILE, LANES), jnp.bfloat16),
            pltpu.VMEM((nq, NSA_HEAD_DIM, TILE), jnp.bfloat16),
            pltpu.VMEM((nq + nwin - 2, NSA_HEAD_DIM, TILE), jnp.bfloat16),
            pltpu.VMEM((nc, LANES), jnp.bfloat16),
            pltpu.VMEM((NSA_HEAD_DIM, nc), jnp.bfloat16),
            pltpu.VMEM((2, NSA_HPG * TILE, LANES), jnp.bfloat16),
            pltpu.VMEM((2, NSA_HPG * TILE, LANES), jnp.bfloat16),
            pltpu.VMEM((2, NSA_HEAD_DIM, NSA_HPG * TILE), jnp.float32),
            pltpu.VMEM((2, SEL_GROUP * TILE, NSA_HPG * TILE), jnp.float32),
        ],
        compiler_params=_cparams(3), name="nsa",
    )(pb, pb, pb, pb, pb, pb, kcvc, bias_cmp, bias_sel, bias_win, pf, pf)


def _split_bf16(a, n):
    parts = []
    for _ in range(n - 1):
        hi = a.astype(jnp.bfloat16)
        parts.append(hi)
        a = a - hi.astype(jnp.float32)
    parts.append(a.astype(jnp.bfloat16))
    return parts


def _dot3(a, b):
    ah, al = _split_bf16(a, 2)
    bh, bl = _split_bf16(b, 2)
    f32 = jnp.float32
    return (jnp.dot(ah, bh, preferred_element_type=f32) + jnp.dot(al, bh, preferred_element_type=f32)
            + jnp.dot(ah, bl, preferred_element_type=f32))


def _softplus(x):
    return jnp.maximum(x, 0.0) + jnp.log(1.0 + jnp.exp(-jnp.abs(x)))


def _dn_kernel(scal_ref, q_ref, k_ref, v_ref, small_ref, z_ref, cw_ref, nw_ref, o_ref,
               qn, kn, vn, bet, gl, mm, nn, qq, oo, dd, ss, *, seq):
    h = pl.program_id(1)
    n_rows = q_ref.shape[0]
    nb = n_rows // seq
    c = DN_CHUNK
    n_chunks = seq // c
    d = DN_HEAD_DIM

    row = lax.broadcasted_iota(jnp.int32, (n_rows, d), 0)
    pos = row
    for b in range(1, nb):
        pos = jnp.where(row >= b * seq, row - b * seq, pos)

    def conv_silu(x_ref, which):
        x = x_ref[...]
        y = x * cw_ref[which, DN_CONV - 1:DN_CONV, :]
        for j in range(DN_CONV - 1):
            sh = DN_CONV - 1 - j
            y = y + jnp.where(pos >= sh, pltpu.roll(x, sh, 0), 0.0) * cw_ref[which, j:j + 1, :]
        return y * jax.nn.sigmoid(y)

    def l2n(t):
        return t * lax.rsqrt(jnp.sum(t * t, axis=-1, keepdims=True) + 1e-6)

    qn[...] = l2n(conv_silu(q_ref, 0)) * (d ** -0.5)
    kn[...] = l2n(conv_silu(k_ref, 1))
    vn[...] = conv_silu(v_ref, 2)
    small = small_ref[...]
    lane = lax.broadcasted_iota(jnp.int32, small.shape, 1)
    beta_in = jnp.sum(jnp.where(lane == SMALL_BETA + h, small, 0.0), axis=-1, keepdims=True)
    a_in = jnp.sum(jnp.where(lane == SMALL_A + h, small, 0.0), axis=-1, keepdims=True)
    bet[...] = jnp.broadcast_to(jax.nn.sigmoid(beta_in), (n_rows, d))
    gl[...] = jnp.broadcast_to(-jnp.exp(scal_ref[0, h]) * _softplus(a_in + scal_ref[1, h]), (n_rows, d))

    ri = lax.broadcasted_iota(jnp.int32, (c, c), 0)
    cj = lax.broadcasted_iota(jnp.int32, (c, c), 1)
    incl = ri >= cj
    strict = ri > cj
    tril = incl.astype(jnp.float32)
    eye = (ri == cj).astype(jnp.float32)

    tril_b = tril.astype(jnp.bfloat16)
    bs = DN_INV_BLOCK
    sh = bs.bit_length() - 1
    same_diag = (ri >> sh) == (cj >> sh)
    level_masks = []
    while (1 << sh) < c:
        level_masks.append(((ri >> (sh + 1)) == (cj >> (sh + 1))) & ((ri >> sh) > (cj >> sh)))
        sh += 1

    bf16 = jnp.bfloat16
    nt_dims = (((1,), (1,)), ((), ()))

    def dot1(a, b):
        return jnp.dot(a.astype(bf16), b.astype(bf16), preferred_element_type=jnp.float32)

    def chunk_prep(it, carry):
        ids = [it * DN_PREP_UNROLL + cc for cc in range(DN_PREP_UNROLL)]
        rows = [pl.ds(pl.multiple_of(i * c, c), c) for i in ids]
        ks = [kn[r, :] for r in rows]
        betas = [bet[r, :] for r in rows]
        gcbs = [sum(jnp.dot(tril_b, part, preferred_element_type=jnp.float32)
                    for part in _split_bf16(gl[r, :], 3)) for r in rows]
        kbs = [k * beta for k, beta in zip(ks, betas)]
        kbfs = [k.astype(bf16) for k in ks]
        a_kks = [lax.dot_general(kb.astype(bf16), kbf, nt_dims, preferred_element_type=jnp.float32)
                 for kb, kbf in zip(kbs, kbfs)]
        decays = []
        for gcb in gcbs:
            gct = jnp.concatenate([gcb, gcb], axis=0).T
            diff = gcb[:, 0:c] - gct[0:c, 0:c]
            decays.append(jnp.where(incl, jnp.exp(jnp.where(incl, diff, 0.0)), 0.0))
        lows = [jnp.where(strict, a * dec, 0.0) for a, dec in zip(a_kks, decays)]
        pws = [jnp.where(same_diag, -low, 0.0) for low in lows]
        es = list(pws)
        for _ in range(max(1, (bs - 1).bit_length()) - 1):
            pws = [dot1(pw, pw) for pw in pws]
            es = [e + pw + dot1(e, pw) for e, pw in zip(es, pws)]
        for below in level_masks:
            offs = [jnp.where(below, low, 0.0) for low in lows]
            xs = [off + dot1(e, off) for e, off in zip(es, offs)]
            es = [e - (x + dot1(x, e)) for e, x in zip(es, xs)]
        egcs = [jnp.exp(gcb) for gcb in gcbs]
        rhss = [jnp.concatenate([vn[r, :] * beta, kb * egc], axis=1)
                for r, beta, kb, egc in zip(rows, betas, kbs, egcs)]
        uws = [rhs + _dot3(e, rhs) for e, rhs in zip(es, rhss)]
        qs = [qn[r, :] for r in rows]
        a_qks = [lax.dot_general(q.astype(bf16), kbf, nt_dims, preferred_element_type=jnp.float32) * dec
                 for q, kbf, dec in zip(qs, kbfs, decays)]
        g_lasts = [gcb[c - 1:c, :] for gcb in gcbs]
        kdec_ts = []
        for k, gcb, g_last in zip(ks, gcbs, g_lasts):
            kdec = k * jnp.exp(g_last - gcb)
            kdec_ts.append(jnp.concatenate([kdec, jnp.zeros_like(kdec)], axis=0).T[:, 0:c].astype(bf16))
        uwbs = [uw.astype(bf16) for uw in uws]
        nms = [jnp.dot(kt, uwb, preferred_element_type=jnp.float32) for kt, uwb in zip(kdec_ts, uwbs)]
        oqs = [jnp.dot(a.astype(bf16), uwb, preferred_element_type=jnp.float32) for a, uwb in zip(a_qks, uwbs)]
        for i, r, nm, oq, q, egc, g_last in zip(ids, rows, nms, oqs, qs, egcs, g_lasts):
            m0 = pl.ds(pl.multiple_of(i * d, d), d)
            nn[m0, :] = nm[:, 0:d]
            mm[m0, :] = nm[:, d:2 * d].astype(bf16)
            oo[r, :] = oq[:, 0:d]
            qq[r, :] = (q * egc - oq[:, d:2 * d]).astype(bf16)
            dd[pl.ds(pl.multiple_of(i * 8, 8), 8), :] = jnp.broadcast_to(jnp.exp(g_last), (8, d))
        return carry

    lax.fori_loop(0, nb * n_chunks // DN_PREP_UNROLL, chunk_prep, 0)

    def chunk_scan(i, states):
        ids = [b * n_chunks + i for b in range(nb)]
        blocks = [pl.ds(pl.multiple_of(j * d, d), d) for j in ids]
        sbs = [s.astype(bf16) for s in states]
        for blk, sb in zip(blocks, sbs):
            ss[blk, :] = sb
        prods = [jnp.dot(mm[blk, :], sb, preferred_element_type=jnp.float32) for blk, sb in zip(blocks, sbs)]
        return tuple(s * dd[pl.ds(pl.multiple_of(j * 8, 8), 1), :] - pr + nn[blk, :]
                     for s, j, blk, pr in zip(states, ids, blocks, prods))

    lax.fori_loop(0, n_chunks, chunk_scan, tuple(jnp.zeros((d, d), jnp.float32) for _ in range(nb)))

    nw = nw_ref[...]

    def chunk_out(it, carry):
        ids = [it * DN_OUT_UNROLL + cc for cc in range(DN_OUT_UNROLL)]
        rows = [pl.ds(pl.multiple_of(i * c, c), c) for i in ids]
        outs = [jnp.dot(qq[r, :], ss[pl.ds(pl.multiple_of(i * d, d), d), :], preferred_element_type=jnp.float32)
                + oo[r, :] for i, r in zip(ids, rows)]
        for r, o in zip(rows, outs):
            o = o * lax.rsqrt(jnp.mean(o * o, axis=-1, keepdims=True) + 1e-6) * nw
            z = z_ref[r, :]
            o_ref[r, :] = (o * (z * jax.nn.sigmoid(z))).astype(o_ref.dtype)
        return carry

    lax.fori_loop(0, nb * n_chunks // DN_OUT_UNROLL, chunk_out, 0)


def _deltanet(pf, conv_w, a_log, dt_bias, norm_w, bsz, seq):
    d = DN_HEAD_DIM
    nb = DN_BATCHES if bsz % DN_BATCHES == 0 else 1
    rows = nb * seq
    n_chunks = rows // DN_CHUNK
    qkv0 = PF_QKVB // d
    scal = jnp.stack([a_log, dt_bias]).astype(jnp.float32)
    f32 = jnp.float32
    bf16 = jnp.bfloat16
    cw4 = conv_w.astype(f32).reshape(DN_CONV, 3, DN_HEADS, d).transpose(2, 1, 0, 3)
    assert 2 * DN_CHUNK == d and n_chunks % DN_PREP_UNROLL == 0 and n_chunks % DN_OUT_UNROLL == 0
    return pl.pallas_call(
        functools.partial(_dn_kernel, seq=seq),
        grid=(bsz // nb, DN_HEADS),
        in_specs=[
            pl.BlockSpec(memory_space=pltpu.SMEM),
            pl.BlockSpec((rows, d), lambda b, h: (b, qkv0 + h)),
            pl.BlockSpec((rows, d), lambda b, h: (b, qkv0 + DN_HEADS + h)),
            pl.BlockSpec((rows, d), lambda b, h: (b, qkv0 + 2 * DN_HEADS + h)),
            pl.BlockSpec((rows, LANES), lambda b, h: (b, PF_SMALL // LANES)),
            pl.BlockSpec((rows, d), lambda b, h: (b, PF_ZB // d + h)),
            pl.BlockSpec((None, 3, DN_CONV, d), lambda b, h: (h, 0, 0, 0)),
            pl.BlockSpec((1, d), lambda b, h: (0, 0)),
        ],
        out_specs=pl.BlockSpec((rows, d), lambda b, h: (b, h)),
        out_shape=jax.ShapeDtypeStruct((bsz * seq, DN_WIDTH), jnp.bfloat16),
        scratch_shapes=[
            pltpu.VMEM((rows, d), f32), pltpu.VMEM((rows, d), f32), pltpu.VMEM((rows, d), f32),
            pltpu.VMEM((rows, d), f32), pltpu.VMEM((rows, d), f32),
            pltpu.VMEM((n_chunks * d, d), bf16), pltpu.VMEM((n_chunks * d, d), f32),
            pltpu.VMEM((rows, d), bf16), pltpu.VMEM((rows, d), f32),
            pltpu.VMEM((n_chunks * 8, d), f32),
            pltpu.VMEM((n_chunks * d, d), bf16),
        ],
        compiler_params=_cparams(2), name="deltanet",
    )(scal, pf, pf, pf, pf, pf, cw4, norm_w.astype(f32).reshape(1, d))


def _out_kernel(oa_ref, ob_ref, gma_ref, gmb_ref, x_ref, p_ref, wa_ref, wb_ref, wo_ref, wpg_ref, wp_ref,
                lng_ref, lnb_ref, o_ref):
    f32 = jnp.float32
    y_a = jnp.dot(oa_ref[...], wa_ref[...], preferred_element_type=f32)
    y_b = jnp.dot(ob_ref[...], wb_ref[...], preferred_element_type=f32)
    mix = jax.nn.sigmoid(gma_ref[...]) * y_a + jax.nn.sigmoid(gmb_ref[...]) * y_b
    mixed = jnp.dot(mix.astype(jnp.bfloat16), wo_ref[...], preferred_element_type=f32)
    h = DEEPNORM_ALPHA * x_ref[...] + mixed
    gate = jax.nn.sigmoid(jnp.dot(h.astype(jnp.bfloat16), wpg_ref[...], preferred_element_type=f32))
    h = h + gate * jnp.dot(p_ref[...].astype(jnp.bfloat16), wp_ref[...], preferred_element_type=f32)
    mu = jnp.mean(h, axis=-1, keepdims=True)
    hc = h - mu
    var = jnp.mean(hc * hc, axis=-1, keepdims=True)
    o_ref[...] = (hc * lax.rsqrt(var + 1e-5) * lng_ref[...] + lnb_ref[...]).astype(o_ref.dtype)


def _out_block(o_a, o_b, pf, x2, p2, wa, wb, wo, wpg, wp, ln_g, ln_b, tm):
    t = x2.shape[0]
    bf = jnp.bfloat16

    def full(shape):
        return pl.BlockSpec(shape, lambda i: (0, 0))

    return pl.pallas_call(
        _out_kernel,
        grid=(t // tm,),
        in_specs=[
            pl.BlockSpec((tm, NSA_WIDTH), lambda i: (i, 0)),
            pl.BlockSpec((tm, DN_WIDTH), lambda i: (i, 0)),
            pl.BlockSpec((tm, D_MODEL), lambda i: (i, PF_GM // D_MODEL)),
            pl.BlockSpec((tm, D_MODEL), lambda i: (i, PF_GM // D_MODEL + 1)),
            pl.BlockSpec((tm, D_MODEL), lambda i: (i, 0)),
            pl.BlockSpec((tm, PLE_DIM), lambda i: (i, 0)),
            full((NSA_WIDTH, D_MODEL)), full((DN_WIDTH, D_MODEL)), full((D_MODEL, D_MODEL)),
            full((D_MODEL, D_MODEL)), full((PLE_DIM, D_MODEL)), full((1, D_MODEL)), full((1, D_MODEL)),
        ],
        out_specs=pl.BlockSpec((tm, D_MODEL), lambda i: (i, 0)),
        out_shape=jax.ShapeDtypeStruct((t, D_MODEL), x2.dtype),
        compiler_params=_cparams(1), name="out_block",
    )(o_a, o_b, pf, pf, x2, p2, wa.astype(bf), wb.astype(bf), wo.astype(bf), wpg.astype(bf), wp.astype(bf),
      ln_g.astype(jnp.float32).reshape(1, D_MODEL), ln_b.astype(jnp.float32).reshape(1, D_MODEL))


def _layer(x, p, w_in, pos_k, pos_v, w1_k, w2_k, w1_v, w2_v, bias_tabs, conv_w, a_log, dt_bias, norm_w,
           w_a, w_b, w_o, w_ple, w_pg, ln_g, ln_b):
    bsz, seq, _ = x.shape
    t = bsz * seq
    x2 = x.reshape(t, D_MODEL)
    xb = x2.astype(jnp.bfloat16)
    wb16, wf16 = _prep_w_in(w_in)
    tm = PROJ_TM if t % PROJ_TM == 0 else seq
    pb = _matmul(xb, wb16, jnp.bfloat16, tm, PB_WIDTH // 2, "proj_bf16")
    pf = _matmul(xb, wf16, jnp.float32, tm, PROJ_TN, "proj_f32")

    nc = seq // CMP_STRIDE
    rkv = pf[:, PF_KC:PF_KC + 2 * NSA_KV].reshape(bsz, nc, CMP_STRIDE, 2, NSA_KV)
    rkv = rkv.transpose(0, 3, 1, 2, 4).reshape(bsz, 2, nc, CMP_STRIDE * NSA_KV)
    pos2, w1p, w2p = _prep_compress_weights(pos_k, pos_v, w1_k, w1_v, w2_k, w2_v)
    kcvc = _compress(rkv, pos2, w1p, w2p)

    bias_sel, bias_win, bias_cmp = bias_tabs
    o_a = _nsa(pb, pf, kcvc, bias_sel, bias_win, bias_cmp, bsz, seq)
    o_b = _deltanet(pf, conv_w, a_log, dt_bias, norm_w, bsz, seq)
    out = _out_block(o_a, o_b, pf, x2, p.reshape(t, PLE_DIM), w_a, w_b, w_o, w_pg, w_ple, ln_g, ln_b,
                     OUT_TM if t % OUT_TM == 0 else seq)
    return out.reshape(bsz, seq, D_MODEL)


def kernel(x, p, w_in, cmp_pos_k, cmp_pos_v, cmp_w1_k, cmp_w2_k, cmp_w1_v, cmp_w2_v, rel_bias, dn_conv_w,
           dn_a_log, dn_dt_bias, dn_norm_w, w_branch_a, w_branch_b, w_out, w_ple, w_ple_gate, ln_g, ln_b):
    depth = w_in.shape[0]
    bias_tabs = _bias_tables(rel_bias, x.shape[1])
    for i in range(depth):
        x = _layer(x, p[i], w_in[i], cmp_pos_k[i], cmp_pos_v[i], cmp_w1_k[i], cmp_w2_k[i], cmp_w1_v[i],
                   cmp_w2_v[i], bias_tabs, dn_conv_w[i], dn_a_log[i], dn_dt_bias[i], dn_norm_w[i],
                   w_branch_a[i], w_branch_b[i], w_out[i], w_ple[i], w_ple_gate[i], ln_g[i], ln_b[i])
    return x
```

```python
import functools
import math

import numpy as np
import jax
import jax.numpy as jnp
from jax import lax
from jax.experimental import pallas as pl
from jax.experimental.pallas import tpu as pltpu

D_MODEL = 1024
PLE_DIM = 256
NSA_HEADS = 8
NSA_GROUPS = 2
NSA_HPG = NSA_HEADS // NSA_GROUPS
NSA_HEAD_DIM = 64
NSA_WIDTH = NSA_HEADS * NSA_HEAD_DIM
NSA_KV = NSA_GROUPS * NSA_HEAD_DIM
CMP_BLOCK = 32
CMP_STRIDE = 16
CMP_HIDDEN = 256
SEL_BLOCK = 64
SEL_TOPK = 8
WINDOW = 512
DN_HEADS = 4
DN_HEAD_DIM = 128
DN_WIDTH = DN_HEADS * DN_HEAD_DIM
DN_CONV = 4
DN_CHUNK = 64
NUM_BUCKETS = 32
REL_MAX_DIST = 1024
DEEPNORM_ALPHA = 2.0 ** 0.25
NEG = -1e30
FORCE = 1e6
LOG2E = 1.4426950408889634

LANES = 128
TILE = 128
SEL_LANE0 = 64
SEL_GROUP = 4
VMEM_LIMIT = 56 * 1024 * 1024
PROJ_TM = 2048
PROJ_TN = 1024
OUT_TM = 512
DN_INV_BLOCK = 16
DN_BATCHES = 2
DN_OUT_UNROLL = 8
DN_DECAY_PARTS = 2
DN_PREP_UNROLL = 16

HIGHEST = lax.Precision.HIGHEST

PB_QP = 0
PB_KS = PB_QP + NSA_HEADS * LANES
PB_KW = PB_KS + NSA_GROUPS * LANES
PB_VS = PB_KW + NSA_GROUPS * LANES
PB_VW = PB_VS + LANES
PB_WIDTH = PB_VW + LANES
PF_KC = 0
PF_VC = 128
PF_SMALL = 256
PF_ZA = 512
PF_ZB = 1024
PF_QKVB = 1536
PF_GM = 3072
PF_WIDTH = 5120
SMALL_BETA = 3 * NSA_HEADS
SMALL_A = SMALL_BETA + DN_HEADS


def _bucket_thresholds():
    max_exact = NUM_BUCKETS // 2
    span = NUM_BUCKETS - max_exact
    ratio = REL_MAX_DIST // max_exact
    thr = list(range(1, max_exact + 1))
    for k in range(1, span):
        n = max_exact
        while n ** span < max_exact ** span * ratio ** k:
            n += 1
        thr.append(n)
    return tuple(thr)


_THR = _bucket_thresholds()


def _cparams(n_axes):
    return pltpu.CompilerParams(dimension_semantics=("arbitrary",) * n_axes, vmem_limit_bytes=VMEM_LIMIT)


def _mm_kernel(x_ref, w_ref, o_ref):
    o_ref[...] = jnp.dot(x_ref[...], w_ref[...], preferred_element_type=jnp.float32).astype(o_ref.dtype)


def _matmul(x, w, out_dtype, tm, tn, name):
    m, k = x.shape
    n = w.shape[1]
    return pl.pallas_call(
        _mm_kernel,
        grid=(n // tn, m // tm),
        in_specs=[pl.BlockSpec((tm, k), lambda j, i: (i, 0)),
                  pl.BlockSpec((k, tn), lambda j, i: (0, j))],
        out_specs=pl.BlockSpec((tm, tn), lambda j, i: (i, j)),
        out_shape=jax.ShapeDtypeStruct((m, n), out_dtype),
        compiler_params=_cparams(2), name=name,
    )(x, w)


def _prep_w_in(w):
    d = w.shape[0]
    o = 0
    wq = w[:, o:o + NSA_WIDTH]; o += NSA_WIDTH
    wkv = w[:, o:o + 6 * NSA_KV]; o += 6 * NSA_KV
    wg = w[:, o:o + 3 * NSA_HEADS]; o += 3 * NSA_HEADS
    wza = w[:, o:o + NSA_WIDTH]; o += NSA_WIDTH
    wqkvb = w[:, o:o + 3 * DN_WIDTH]; o += 3 * DN_WIDTH
    wbeta = w[:, o:o + DN_HEADS]; o += DN_HEADS
    wa = w[:, o:o + DN_HEADS]; o += DN_HEADS
    wzb = w[:, o:o + DN_WIDTH]; o += DN_WIDTH
    wgm = w[:, o:o + 2 * D_MODEL]
    wkc, wvc, wks, wvs, wkw, wvw = [wkv[:, i * NSA_KV:(i + 1) * NSA_KV] for i in range(6)]

    def pad_heads(t, nh):
        t = t.reshape(d, nh, NSA_HEAD_DIM)
        return jnp.concatenate([t, jnp.zeros_like(t)], axis=-1).reshape(d, nh * LANES)

    wb = jnp.concatenate([pad_heads(wq * (NSA_HEAD_DIM ** -0.5 * LOG2E), NSA_HEADS), pad_heads(wks, NSA_GROUPS),
                          pad_heads(wkw, NSA_GROUPS), wvs, wvw], axis=1).astype(jnp.bfloat16)
    small = jnp.concatenate([wg, wbeta, wa, jnp.zeros((d, LANES - SMALL_A - DN_HEADS), w.dtype)], axis=1)
    wf = jnp.concatenate([wkc, wvc, small, jnp.zeros((d, LANES), w.dtype), wza, wzb, wqkvb, wgm],
                         axis=1).astype(jnp.bfloat16)
    return wb, wf


def _bias_kernel(tab_ref, sel_ref, win_ref, cmp_ref, *, n_cmp):
    h = pl.program_id(0)

    def lookup(n):
        val = jnp.full(n.shape, tab_ref[0, h], jnp.float32)
        for b in range(1, NUM_BUCKETS):
            val = jnp.where(n >= _THR[b - 1], tab_ref[b, h], val)
        return val * LOG2E

    kj = lax.broadcasted_iota(jnp.int32, (TILE, TILE), 0)
    qi = lax.broadcasted_iota(jnp.int32, (TILE, TILE), 1)
    n_sel_tiles = sel_ref.shape[1] - 1
    n_win_tiles = win_ref.shape[1] - 1
    for dt in range(max(n_sel_tiles, n_win_tiles)):
        dist = dt * TILE + qi - kj
        v = lookup(jnp.maximum(dist, 0))
        if dt < n_sel_tiles:
            sel_ref[0, dt] = jnp.where(dist >= 0, v, NEG)
        if dt < n_win_tiles:
            win_ref[0, dt] = jnp.where((dist >= 0) & (dist < WINDOW), v, NEG)
    sel_ref[0, n_sel_tiles] = jnp.full((TILE, TILE), NEG, jnp.float32)
    win_ref[0, n_win_tiles] = jnp.full((TILE, TILE), NEG, jnp.float32)
    c = lax.broadcasted_iota(jnp.int32, cmp_ref.shape[2:], 0)
    for t in range(cmp_ref.shape[1]):
        s = t * TILE + lax.broadcasted_iota(jnp.int32, cmp_ref.shape[2:], 1)
        dist = s - (c * CMP_STRIDE + CMP_BLOCK - 1)
        cmp_ref[0, t] = jnp.where((dist >= 0) & (c < n_cmp), lookup(jnp.maximum(dist, 0)), NEG)


def _bias_tables(rel_bias, seq):
    nq = seq // TILE
    nwin = WINDOW // TILE + 1
    nc = seq // CMP_STRIDE
    n_cmp = nc - CMP_BLOCK // CMP_STRIDE + 1
    return pl.pallas_call(
        functools.partial(_bias_kernel, n_cmp=n_cmp),
        grid=(NSA_HEADS,),
        in_specs=[pl.BlockSpec(memory_space=pltpu.SMEM)],
        out_specs=[pl.BlockSpec((1, nq + 1, TILE, TILE), lambda h: (h, 0, 0, 0)),
                   pl.BlockSpec((1, nwin + 1, TILE, TILE), lambda h: (h, 0, 0, 0)),
                   pl.BlockSpec((1, nq, nc, TILE), lambda h: (h, 0, 0, 0))],
        out_shape=[jax.ShapeDtypeStruct((NSA_HEADS, nq + 1, TILE, TILE), jnp.float32),
                   jax.ShapeDtypeStruct((NSA_HEADS, nwin + 1, TILE, TILE), jnp.float32),
                   jax.ShapeDtypeStruct((NSA_HEADS, nq, nc, TILE), jnp.float32)],
        compiler_params=_cparams(1), name="bias_tables",
    )(rel_bias.astype(jnp.float32))


def _gelu_tanh(x):
    return x * (0.5 * (1.0 + jnp.tanh(math.sqrt(2.0 / math.pi) * (x + 0.044715 * (x * x * x)))))


def _compress_kernel(r_ref, pos_ref, w1_ref, w2_ref, o_ref):
    r = r_ref[0, 0]
    nc = r.shape[0]
    a = jnp.dot((r + pos_ref[0, 0:1, :]).astype(jnp.bfloat16), w1_ref[0, 0], preferred_element_type=jnp.float32)
    b = jnp.dot((r + pos_ref[0, 1:2, :]).astype(jnp.bfloat16), w1_ref[0, 1], preferred_element_type=jnp.float32)
    hid = a + pltpu.roll(b, nc - 1, 0)
    o_ref[0, 0] = jnp.dot(_gelu_tanh(hid).astype(jnp.bfloat16), w2_ref[0], preferred_element_type=jnp.float32)


def _compress(rkv, pos2, w1p, w2p):
    bsz, _, nc, width = rkv.shape
    hid = w1p.shape[-1]
    return pl.pallas_call(
        _compress_kernel,
        grid=(2, bsz),
        in_specs=[pl.BlockSpec((1, 1, nc, width), lambda k, b: (b, k, 0, 0)),
                  pl.BlockSpec((1, 2, width), lambda k, b: (k, 0, 0)),
                  pl.BlockSpec((1, 2, width, hid), lambda k, b: (k, 0, 0, 0)),
                  pl.BlockSpec((1, hid, LANES), lambda k, b: (k, 0, 0))],
        out_specs=pl.BlockSpec((1, 1, nc, LANES), lambda k, b: (b, k, 0, 0)),
        out_shape=jax.ShapeDtypeStruct((bsz, 2, nc, LANES), jnp.float32),
        compiler_params=_cparams(2), name="compress",
    )(rkv, pos2, w1p, w2p)


def _prep_compress_weights(pos_k, pos_v, w1_k, w1_v, w2_k, w2_v):
    eye = jnp.eye(NSA_GROUPS, dtype=jnp.float32)
    half = CMP_BLOCK // 2

    def w1_both(w1):
        w = w1.reshape(2, half, NSA_HEAD_DIM, CMP_HIDDEN)
        w = jnp.einsum('aldj,gh->algdhj', w, eye)
        return w.reshape(2, half * NSA_GROUPS * NSA_HEAD_DIM, NSA_GROUPS * CMP_HIDDEN)

    def w2_both(w2):
        return jnp.einsum('jd,gh->gjhd', w2, eye).reshape(NSA_GROUPS * CMP_HIDDEN, NSA_GROUPS * NSA_HEAD_DIM)

    def pos_both(pos):
        p = pos.reshape(2, half, 1, NSA_HEAD_DIM)
        return jnp.broadcast_to(p, (2, half, NSA_GROUPS, NSA_HEAD_DIM)).reshape(2, half * NSA_KV)

    pos2 = jnp.stack([pos_both(pos_k), pos_both(pos_v)]).astype(jnp.float32)
    w1p = jnp.stack([w1_both(w1_k), w1_both(w1_v)]).astype(jnp.bfloat16)
    w2p = jnp.stack([w2_both(w2_k), w2_both(w2_v)]).astype(jnp.bfloat16)
    return pos2, w1p, w2p


def _pair_lanes(x, g):
    sw = pltpu.roll(x, LANES // 2, 1)
    lane = lax.broadcasted_iota(jnp.int32, x.shape, 1)
    own = (lane < LANES // 2) == (g == 0)
    return jnp.where(own, x, sw)


def _group_rows_t(x, g):
    xt = x.T
    half = LANES // 2
    return jnp.where(g == 0, xt[0:half, :], xt[half:LANES, :])


def _nsa_kernel(q_ref, ks_ref, kw_ref, vs_ref, vw_ref, kcvc_ref, bc_ref, bsel_ref, bwin_ref,
                small_ref, za_ref, o_ref,
                ksa, kwp, vst, vwt, kcs, vct, qa, qp, sbuf, acc_s, l_s, *, n_cmp, n_sel, n_top):
    g = pl.program_id(1)
    qt = pl.program_id(2)
    tq = TILE
    seq = ks_ref.shape[0]
    nq = seq // TILE
    nc = kcvc_ref.shape[2]
    n_wt = bwin_ref.shape[1] - 1
    heads = range(NSA_HPG)
    f32 = jnp.float32
    bf16 = jnp.bfloat16
    nt_dims = (((1,), (1,)), ((), ()))

    @pl.when(qt == 0)
    def _():
        row = lax.broadcasted_iota(jnp.int32, (seq, LANES), 0)
        lane = lax.broadcasted_iota(jnp.int32, (seq, LANES), 1)
        onehot = (lane - SEL_LANE0) == (row >> 6)
        ksa[...] = jnp.where(onehot, 1.0, ks_ref[...].astype(f32)).astype(bf16)
        pad = (n_wt - 1) * TILE
        kwp[0:pad, :] = jnp.zeros((pad, LANES), bf16)
        kwp[pad:pad + seq, :] = kw_ref[...]
        for kt in range(n_wt - 1):
            vwt[kt] = jnp.zeros(vwt.shape[1:], bf16)
        for kt in range(nq):
            rows = slice(kt * TILE, (kt + 1) * TILE)
            vst[kt] = _group_rows_t(vs_ref[rows, :].astype(f32), g).astype(bf16)
            vwt[kt + n_wt - 1] = _group_rows_t(vw_ref[rows, :].astype(f32), g).astype(bf16)
        kcs[...] = _pair_lanes(kcvc_ref[0, 0], g).astype(bf16)
        for ct in range(nc // TILE):
            rows = slice(ct * TILE, (ct + 1) * TILE)
            vct[:, rows] = _group_rows_t(kcvc_ref[0, 1, rows, :], g).astype(bf16)

    qh = [q_ref[:, hh * LANES:(hh + 1) * LANES] for hh in heads]
    for hh in heads:
        qp[hh * tq:(hh + 1) * tq, :] = qh[hh]
    q_all = qp[...]

    def softmax_pv(s, v_t):
        m = jnp.max(s, axis=0, keepdims=True)
        p = jnp.exp2(s - m)
        return jnp.dot(v_t, p.astype(bf16), preferred_element_type=f32), jnp.sum(p, axis=0, keepdims=True)

    s_c = lax.dot_general(kcs[...], q_all, nt_dims, preferred_element_type=f32)
    kw_rows = kwp[pl.ds(pl.multiple_of(qt * TILE, TILE), n_wt * TILE), :]
    s_w = lax.dot_general(kw_rows, q_all, nt_dims, preferred_element_type=f32)

    bias_c = jnp.concatenate([bc_ref[hh] for hh in heads], axis=1)
    valid = bias_c > 0.5 * NEG
    s_c = s_c + bias_c
    e = jnp.where(valid, jnp.exp2(s_c - jnp.max(s_c, axis=0, keepdims=True)), 0.0)
    den = jnp.maximum(jnp.sum(e, axis=0, keepdims=True), 1e-30)
    p_c = e * (1.0 / den)
    o_cmp = jnp.dot(vct[...], p_c.astype(bf16), preferred_element_type=f32)
    psum = sum(p_c[:, hh * tq:(hh + 1) * tq] for hh in heads)

    w_tiles = []
    for t in range(n_wt):
        dt = n_wt - 1 - t
        idx = jnp.where(qt >= dt, dt, n_wt)
        w_tiles.append(jnp.concatenate([bwin_ref[hh, idx] for hh in heads], axis=1))
    acc_w, l_w = softmax_pv(s_w + jnp.concatenate(w_tiles, axis=0),
                            jnp.concatenate([vwt[qt + t] for t in range(n_wt)], axis=1))
    o_win = acc_w * (1.0 / l_w)

    sj = lax.broadcasted_iota(jnp.int32, (n_sel, nc), 0)
    ci = lax.broadcasted_iota(jnp.int32, (n_sel, nc), 1)
    overlap = ((ci * CMP_STRIDE < (sj + 1) * SEL_BLOCK) & (ci * CMP_STRIDE + CMP_BLOCK > sj * SEL_BLOCK)
               & (ci < n_cmp)).astype(bf16)
    imp_t = sum(jnp.dot(overlap, part, preferred_element_type=f32)
                for part in _split_bf16(psum, 3))
    blk = lax.broadcasted_iota(jnp.int32, (n_sel, tq), 0)
    cur = (qt * tq + lax.broadcasted_iota(jnp.int32, (n_sel, tq), 1)) >> 6
    forced = (blk == 0) | (blk == cur) | (blk == cur - 1)
    imp_t = jnp.where(forced, FORCE, jnp.where(blk > cur, -FORCE, imp_t))
    rank = jnp.zeros((n_sel, tq), jnp.int32)
    for j in range(n_sel):
        other = imp_t[j:j + 1, :]
        ahead = (other > imp_t) | ((other == imp_t) & (blk > j))
        rank = rank + ahead.astype(jnp.int32)
    selb = jnp.where(rank < n_top, 0.0, NEG)
    pieces = [jnp.zeros((SEL_LANE0, tq), jnp.float32), selb]
    if LANES - SEL_LANE0 - n_sel > 0:
        pieces.append(jnp.zeros((LANES - SEL_LANE0 - n_sel, tq), jnp.float32))
    selb_r = jnp.concatenate(pieces, axis=0).T.astype(jnp.bfloat16)

    for hh in heads:
        qa[hh * tq:(hh + 1) * tq, :] = qh[hh] + selb_r
    gk = SEL_GROUP * TILE

    def sel_case(n_groups):
        q_aug = qa[...]
        col_max = []
        for gi in range(n_groups):
            s = lax.dot_general(ksa[gi * gk:(gi + 1) * gk, :], q_aug, nt_dims, preferred_element_type=f32)
            tiles = []
            for t in range(SEL_GROUP):
                dt = qt - (gi * SEL_GROUP + t)
                idx = jnp.where(dt >= 0, dt, nq)
                tiles.append(jnp.concatenate([bsel_ref[hh, idx] for hh in heads], axis=1))
            s = s + jnp.concatenate(tiles, axis=0)
            sbuf[gi] = s
            col_max.append(jnp.max(s, axis=0, keepdims=True))
        m = functools.reduce(jnp.maximum, col_max)
        acc = jnp.zeros(acc_s.shape, f32)
        l = jnp.zeros(l_s.shape, f32)
        for gi in range(n_groups):
            p = jnp.exp2(sbuf[gi] - m)
            v_t = jnp.concatenate([vst[gi * SEL_GROUP + t] for t in range(SEL_GROUP)], axis=1)
            acc = acc + jnp.dot(v_t, p.astype(bf16), preferred_element_type=f32)
            l = l + jnp.sum(p, axis=0, keepdims=True)
        acc_s[...] = acc
        l_s[...] = l

    for n_groups in range(1, nq // SEL_GROUP + 1):
        pl.when(qt // SEL_GROUP == n_groups - 1)(functools.partial(sel_case, n_groups))
    o_sel = acc_s[...] * (1.0 / l_s[...])

    gates_t = jax.nn.sigmoid(small_ref[...]).T
    mixed = []
    for hh in heads:
        def gate(br):
            r = br * NSA_HEADS + hh
            return jnp.where(g == 0, gates_t[r:r + 1, :], gates_t[r + NSA_HPG:r + NSA_HPG + 1, :])
        cols = slice(hh * tq, (hh + 1) * tq)
        mixed.append(gate(0) * o_cmp[:, cols] + gate(1) * o_sel[:, cols] + gate(2) * o_win[:, cols])
    o = jnp.concatenate([jnp.concatenate(mixed[2 * j:2 * j + 2], axis=0).T for j in range(NSA_HPG // 2)], axis=1)
    z = za_ref[...]
    o_ref[...] = (o * (z * jax.nn.sigmoid(z))).astype(o_ref.dtype)


def _nsa_step_kernel(q_ref, qn_ref, ks_ref, kw_ref, vs_ref, vw_ref, kcvc_ref, bc_ref, bsel_ref, bwin_ref,
                     small_ref, za_ref, o_ref,
                     ksa, kwp, vst, vwt, kcs, vct, qa_s, qp_s, ocmp_s, sbuf, *, n_cmp, n_sel, n_top):
    g = pl.program_id(0)
    qt = pl.program_id(2)
    tq = TILE
    seq = ks_ref.shape[0]
    nq = seq // TILE
    nc = kcvc_ref.shape[2]
    n_wt = bwin_ref.shape[1] - 1
    heads = range(NSA_HPG)
    f32 = jnp.float32
    bf16 = jnp.bfloat16
    nt_dims = (((1,), (1,)), ((), ()))
    gk = SEL_GROUP * TILE

    def select_tile(src_ref, tile, slot):
        qh = [src_ref[:, hh * LANES:(hh + 1) * LANES] for hh in heads]
        for hh in heads:
            qp_s[slot, hh * tq:(hh + 1) * tq, :] = qh[hh]
        q_all = qp_s[slot]
        s_c = lax.dot_general(kcs[...], q_all, nt_dims, preferred_element_type=f32)
        bias_c = jnp.concatenate([bc_ref[hh, tile] for hh in heads], axis=1)
        valid = bias_c > 0.5 * NEG
        s_c = s_c + bias_c
        e = jnp.where(valid, jnp.exp2(s_c - jnp.max(s_c, axis=0, keepdims=True)), 0.0)
        den = jnp.maximum(jnp.sum(e, axis=0, keepdims=True), 1e-30)
        p_c = e * (1.0 / den)
        psum = sum(p_c[:, hh * tq:(hh + 1) * tq] for hh in heads)
        sj = lax.broadcasted_iota(jnp.int32, (n_sel, nc), 0)
        ci = lax.broadcasted_iota(jnp.int32, (n_sel, nc), 1)
        overlap = ((ci * CMP_STRIDE < (sj + 1) * SEL_BLOCK) & (ci * CMP_STRIDE + CMP_BLOCK > sj * SEL_BLOCK)
                   & (ci < n_cmp)).astype(bf16)
        imp_t = sum(jnp.dot(overlap, part, preferred_element_type=f32)
                    for part in _split_bf16(psum, 3))
        ocmp_s[slot] = jnp.dot(vct[...], p_c.astype(bf16), preferred_element_type=f32)
        blk = lax.broadcasted_iota(jnp.int32, (n_sel, tq), 0)
        cur = (tile * tq + lax.broadcasted_iota(jnp.int32, (n_sel, tq), 1)) >> 6
        forced = (blk == 0) | (blk == cur) | (blk == cur - 1)
        imp_t = jnp.where(forced, FORCE, jnp.where(blk > cur, -FORCE, imp_t))
        rank = jnp.zeros((n_sel, tq), jnp.int32)
        for j in range(n_sel):
            other = imp_t[j:j + 1, :]
            ahead = (other > imp_t) | ((other == imp_t) & (blk > j))
            rank = rank + ahead.astype(jnp.int32)
        selb = jnp.where(rank < n_top, 0.0, NEG)
        pieces = [jnp.zeros((SEL_LANE0, tq), f32), selb]
        if LANES - SEL_LANE0 - n_sel > 0:
            pieces.append(jnp.zeros((LANES - SEL_LANE0 - n_sel, tq), f32))
        selb_r = jnp.concatenate(pieces, axis=0).T.astype(bf16)
        for hh in heads:
            qa_s[slot, hh * tq:(hh + 1) * tq, :] = qh[hh] + selb_r

    @pl.when(qt == 0)
    def _():
        row = lax.broadcasted_iota(jnp.int32, (seq, LANES), 0)
        lane = lax.broadcasted_iota(jnp.int32, (seq, LANES), 1)
        onehot = (lane - SEL_LANE0) == (row >> 6)
        ksa[...] = jnp.where(onehot, 1.0, ks_ref[...].astype(f32)).astype(bf16)
        pad = (n_wt - 1) * TILE
        kwp[0:pad, :] = jnp.zeros((pad, LANES), bf16)
        kwp[pad:pad + seq, :] = kw_ref[...]
        for kt in range(n_wt - 1):
            vwt[kt] = jnp.zeros(vwt.shape[1:], bf16)
        for kt in range(nq):
            rows = slice(kt * TILE, (kt + 1) * TILE)
            vst[kt] = _group_rows_t(vs_ref[rows, :].astype(f32), g).astype(bf16)
            vwt[kt + n_wt - 1] = _group_rows_t(vw_ref[rows, :].astype(f32), g).astype(bf16)
        kcs[...] = _pair_lanes(kcvc_ref[0, 0], g).astype(bf16)
        for ct in range(nc // TILE):
            rows = slice(ct * TILE, (ct + 1) * TILE)
            vct[:, rows] = _group_rows_t(kcvc_ref[0, 1, rows, :], g).astype(bf16)
        select_tile(q_ref, 0, 0)

    slot = qt % 2

    def step(n_groups):
        q_all = qp_s[slot]
        q_aug = qa_s[slot]
        kw_rows = kwp[pl.ds(pl.multiple_of(qt * TILE, TILE), n_wt * TILE), :]
        s_w = lax.dot_general(kw_rows, q_all, nt_dims, preferred_element_type=f32)

        def sel_scores(gi):
            s = lax.dot_general(ksa[gi * gk:(gi + 1) * gk, :], q_aug, nt_dims, preferred_element_type=f32)
            tiles = []
            for t in range(SEL_GROUP):
                dt = qt - (gi * SEL_GROUP + t)
                idx = jnp.where(dt >= 0, dt, nq)
                tiles.append(jnp.concatenate([bsel_ref[hh, idx] for hh in heads], axis=1))
            sbuf[gi % 2] = s + jnp.concatenate(tiles, axis=0)

        sel_scores(0)
        select_tile(qn_ref, jnp.minimum(qt + 1, nq - 1), 1 - slot)

        w_tiles = []
        for t in range(n_wt):
            dt = n_wt - 1 - t
            idx = jnp.where(qt >= dt, dt, n_wt)
            w_tiles.append(jnp.concatenate([bwin_ref[hh, idx] for hh in heads], axis=1))
        s_w = s_w + jnp.concatenate(w_tiles, axis=0)
        p_w = jnp.exp2(s_w - jnp.max(s_w, axis=0, keepdims=True))
        v_w = jnp.concatenate([vwt[qt + t] for t in range(n_wt)], axis=1)
        o_win = (jnp.dot(v_w, p_w.astype(bf16), preferred_element_type=f32)
                 * (1.0 / jnp.sum(p_w, axis=0, keepdims=True)))

        m = jnp.full((1, NSA_HPG * tq), NEG, f32)
        l = jnp.zeros((1, NSA_HPG * tq), f32)
        acc = jnp.zeros((NSA_HEAD_DIM, NSA_HPG * tq), f32)
        for gi in range(n_groups):
            if gi + 1 < n_groups:
                sel_scores(gi + 1)
            s = sbuf[gi % 2]
            m_new = jnp.maximum(m, jnp.max(s, axis=0, keepdims=True))
            alpha = jnp.exp2(m - m_new)
            p = jnp.exp2(s - m_new)
            v_t = jnp.concatenate([vst[gi * SEL_GROUP + t] for t in range(SEL_GROUP)], axis=1)
            acc = alpha * acc + jnp.dot(v_t, p.astype(bf16), preferred_element_type=f32)
            l = alpha * l + jnp.sum(p, axis=0, keepdims=True)
            m = m_new
        o_sel = acc * (1.0 / l)
        o_cmp = ocmp_s[slot]

        gates_t = jax.nn.sigmoid(small_ref[...]).T
        mixed = []
        for hh in heads:
            def gate(br):
                r = br * NSA_HEADS + hh
                return jnp.where(g == 0, gates_t[r:r + 1, :], gates_t[r + NSA_HPG:r + NSA_HPG + 1, :])
            cols = slice(hh * tq, (hh + 1) * tq)
            mixed.append(gate(0) * o_cmp[:, cols] + gate(1) * o_sel[:, cols] + gate(2) * o_win[:, cols])
        o = jnp.concatenate([jnp.concatenate(mixed[2 * j:2 * j + 2], axis=0).T for j in range(NSA_HPG // 2)],
                            axis=1)
        z = za_ref[...]
        o_ref[...] = (o * (z * jax.nn.sigmoid(z))).astype(o_ref.dtype)

    for n_groups in range(1, nq // SEL_GROUP + 1):
        pl.when(qt // SEL_GROUP == n_groups - 1)(functools.partial(step, n_groups))


def _nsa(pb, pf, kcvc, bias_sel, bias_win, bias_cmp, bsz, seq):
    nq = seq // TILE
    nc = seq // CMP_STRIDE
    n_cmp = nc - CMP_BLOCK // CMP_STRIDE + 1
    n_sel = seq // SEL_BLOCK
    n_top = min(SEL_TOPK, n_sel)
    nwin = bias_win.shape[1]
    assert nq % SEL_GROUP == 0 and nc % TILE == 0 and bias_sel.shape[1] == nq + 1
    gw = NSA_HPG * NSA_HEAD_DIM
    kern = functools.partial(_nsa_step_kernel, n_cmp=n_cmp, n_sel=n_sel, n_top=n_top)
    return pl.pallas_call(
        kern,
        grid=(NSA_GROUPS, bsz, nq),
        in_specs=[
            pl.BlockSpec((TILE, NSA_HPG * LANES), lambda g, b, t: (b * nq + t, g)),
            pl.BlockSpec((TILE, NSA_HPG * LANES),
                         lambda g, b, t: (b * nq + jnp.minimum(t + 1, nq - 1), g)),
            pl.BlockSpec((seq, LANES), lambda g, b, t: (b, PB_KS // LANES + g)),
            pl.BlockSpec((seq, LANES), lambda g, b, t: (b, PB_KW // LANES + g)),
            pl.BlockSpec((seq, LANES), lambda g, b, t: (b, PB_VS // LANES)),
            pl.BlockSpec((seq, LANES), lambda g, b, t: (b, PB_VW // LANES)),
            pl.BlockSpec((1, 2, nc, LANES), lambda g, b, t: (b, 0, 0, 0)),
            pl.BlockSpec((NSA_HPG, nq, nc, TILE), lambda g, b, t: (g, 0, 0, 0)),
            pl.BlockSpec((NSA_HPG, nq + 1, TILE, TILE), lambda g, b, t: (g, 0, 0, 0)),
            pl.BlockSpec((NSA_HPG, nwin, TILE, TILE), lambda g, b, t: (g, 0, 0, 0)),
            pl.BlockSpec((TILE, LANES), lambda g, b, t: (b * nq + t, PF_SMALL // LANES)),
            pl.BlockSpec((TILE, gw), lambda g, b, t: (b * nq + t, PF_ZA // gw + g)),
        ],
        out_specs=pl.BlockSpec((TILE, gw), lambda g, b, t: (b * nq + t, g)),
        out_shape=jax.ShapeDtypeStruct((bsz * seq, NSA_WIDTH), jnp.bfloat16),
        scratch_shapes=[
            pltpu.VMEM((seq, LANES), jnp.bfloat16),
            pltpu.VMEM((seq + (nwin - 2) * TILE, LANES), jnp.bfloat16),
            pltpu.VMEM((nq, NSA_HEAD_DIM, TILE), jnp.bfloat16),
            pltpu.VMEM((nq + nwin - 2, NSA_HEAD_DIM, TILE), jnp.bfloat16),
            pltpu.VMEM((nc, LANES), jnp.bfloat16),
            pltpu.VMEM((NSA_HEAD_DIM, nc), jnp.bfloat16),
            pltpu.VMEM((2, NSA_HPG * TILE, LANES), jnp.bfloat16),
            pltpu.VMEM((2, NSA_HPG * TILE, LANES), jnp.bfloat16),
            pltpu.VMEM((2, NSA_HEAD_DIM, NSA_HPG * TILE), jnp.float32),
            pltpu.VMEM((2, SEL_GROUP * TILE, NSA_HPG * TILE), jnp.float32),
        ],
        compiler_params=_cparams(3), name="nsa",
    )(pb, pb, pb, pb, pb, pb, kcvc, bias_cmp, bias_sel, bias_win, pf, pf)


def _split_bf16(a, n):
    parts = []
    for _ in range(n - 1):
        hi = a.astype(jnp.bfloat16)
        parts.append(hi)
        a = a - hi.astype(jnp.float32)
    parts.append(a.astype(jnp.bfloat16))
    return parts


def _dot3(a, b):
    ah, al = _split_bf16(a, 2)
    bh, bl = _split_bf16(b, 2)
    f32 = jnp.float32
    return (jnp.dot(ah, bh, preferred_element_type=f32) + jnp.dot(al, bh, preferred_element_type=f32)
            + jnp.dot(ah, bl, preferred_element_type=f32))


def _softplus(x):
    return jnp.maximum(x, 0.0) + jnp.log(1.0 + jnp.exp(-jnp.abs(x)))


def _dn_kernel(scal_ref, q_ref, k_ref, v_ref, small_ref, z_ref, cw_ref, nw_ref, o_ref,
               qn, kn, vn, bet, gl, mm, nn, qq, oo, dd, ss, *, seq):
    h = pl.program_id(1)
    n_rows = q_ref.shape[0]
    nb = n_rows // seq
    c = DN_CHUNK
    n_chunks = seq // c
    d = DN_HEAD_DIM

    head = 8

    def conv_body(x_ref, which):
        n = n_rows - head
        y = x_ref[pl.ds(head, n), :] * cw_ref[which, DN_CONV - 1:DN_CONV, :]
        for j in range(DN_CONV - 1):
            y = y + x_ref[pl.ds(head - (DN_CONV - 1 - j), n), :] * cw_ref[which, j:j + 1, :]
        return y

    def conv_head(x_ref, which, r0):
        x = x_ref[pl.ds(r0, head), :]
        rowi = lax.broadcasted_iota(jnp.int32, (head, d), 0)
        y = x * cw_ref[which, DN_CONV - 1:DN_CONV, :]
        for j in range(DN_CONV - 1):
            sh = DN_CONV - 1 - j
            y = y + jnp.where(rowi >= sh, pltpu.roll(x, sh, 0), 0.0) * cw_ref[which, j:j + 1, :]
        return y

    def silu(y):
        return y * jax.nn.sigmoid(y)

    def l2n(t):
        return t * lax.rsqrt(jnp.sum(t * t, axis=-1, keepdims=True) + 1e-6)

    finish = (lambda y: l2n(silu(y)) * (d ** -0.5), lambda y: l2n(silu(y)), silu)
    for which, (x_ref, dst) in enumerate(((q_ref, qn), (k_ref, kn), (v_ref, vn))):
        dst[pl.ds(head, n_rows - head), :] = finish[which](conv_body(x_ref, which))
        for b in range(nb):
            dst[pl.ds(b * seq, head), :] = finish[which](conv_head(x_ref, which, b * seq))
    small = small_ref[...]
    lane = lax.broadcasted_iota(jnp.int32, small.shape, 1)
    beta_in = jnp.sum(jnp.where(lane == SMALL_BETA + h, small, 0.0), axis=-1, keepdims=True)
    a_in = jnp.sum(jnp.where(lane == SMALL_A + h, small, 0.0), axis=-1, keepdims=True)
    bet[...] = jnp.broadcast_to(jax.nn.sigmoid(beta_in), (n_rows, d))
    gl[...] = jnp.broadcast_to(-jnp.exp(scal_ref[0, h]) * _softplus(a_in + scal_ref[1, h]), (n_rows, d))

    ri = lax.broadcasted_iota(jnp.int32, (c, c), 0)
    cj = lax.broadcasted_iota(jnp.int32, (c, c), 1)
    incl = ri >= cj
    strict = ri > cj
    tril = incl.astype(jnp.float32)
    eye = (ri == cj).astype(jnp.float32)

    tril_b = tril.astype(jnp.bfloat16)
    bs = DN_INV_BLOCK
    sh = bs.bit_length() - 1
    same_diag = (ri >> sh) == (cj >> sh)
    level_masks = []
    while (1 << sh) < c:
        level_masks.append(((ri >> (sh + 1)) == (cj >> (sh + 1))) & ((ri >> sh) > (cj >> sh)))
        sh += 1

    bf16 = jnp.bfloat16
    nt_dims = (((1,), (1,)), ((), ()))

    def dot1(a, b):
        return jnp.dot(a.astype(bf16), b.astype(bf16), preferred_element_type=jnp.float32)

    def chunk_prep(it, carry):
        ids = [it * DN_PREP_UNROLL + cc for cc in range(DN_PREP_UNROLL)]
        rows = [pl.ds(pl.multiple_of(i * c, c), c) for i in ids]
        ks = [kn[r, :] for r in rows]
        betas = [bet[r, :] for r in rows]
        gcbs = [sum(jnp.dot(tril_b, part, preferred_element_type=jnp.float32)
                    for part in _split_bf16(gl[r, :], DN_DECAY_PARTS)) for r in rows]
        kbs = [k * beta for k, beta in zip(ks, betas)]
        kbfs = [k.astype(bf16) for k in ks]
        a_kks = [lax.dot_general(kb.astype(bf16), kbf, nt_dims, preferred_element_type=jnp.float32)
                 for kb, kbf in zip(kbs, kbfs)]
        decays = []
        for gcb in gcbs:
            gct = jnp.concatenate([gcb, gcb], axis=0).T
            diff = gcb[:, 0:c] - gct[0:c, 0:c]
            decays.append(jnp.where(incl, jnp.exp(jnp.where(incl, diff, 0.0)), 0.0))
        lows = [jnp.where(strict, a * dec, 0.0) for a, dec in zip(a_kks, decays)]
        pws = [jnp.where(same_diag, -low, 0.0) for low in lows]
        es = list(pws)
        for _ in range(max(1, (bs - 1).bit_length()) - 1):
            pws = [dot1(pw, pw) for pw in pws]
            es = [e + pw + dot1(e, pw) for e, pw in zip(es, pws)]
        for below in level_masks:
            offs = [jnp.where(below, low, 0.0) for low in lows]
            xs = [off + dot1(e, off) for e, off in zip(es, offs)]
            es = [e - (x + dot1(x, e)) for e, x in zip(es, xs)]
        egcs = [jnp.exp(gcb) for gcb in gcbs]
        rhss = [jnp.concatenate([vn[r, :] * beta, kb * egc], axis=1)
                for r, beta, kb, egc in zip(rows, betas, kbs, egcs)]
        uws = [rhs + _dot3(e, rhs) for e, rhs in zip(es, rhss)]
        qs = [qn[r, :] for r in rows]
        a_qks = [lax.dot_general(q.astype(bf16), kbf, nt_dims, preferred_element_type=jnp.float32) * dec
                 for q, kbf, dec in zip(qs, kbfs, decays)]
        g_lasts = [gcb[c - 1:c, :] for gcb in gcbs]
        kdec_ts = []
        for k, gcb, g_last in zip(ks, gcbs, g_lasts):
            kdec = k * jnp.exp(g_last - gcb)
            kdec_ts.append(jnp.concatenate([kdec, jnp.zeros_like(kdec)], axis=0).T[:, 0:c].astype(bf16))
        uwbs = [uw.astype(bf16) for uw in uws]
        nms = [jnp.dot(kt, uwb, preferred_element_type=jnp.float32) for kt, uwb in zip(kdec_ts, uwbs)]
        oqs = [jnp.dot(a.astype(bf16), uwb, preferred_element_type=jnp.float32) for a, uwb in zip(a_qks, uwbs)]
        for i, r, nm, oq, q, egc, g_last in zip(ids, rows, nms, oqs, qs, egcs, g_lasts):
            m0 = pl.ds(pl.multiple_of(i * d, d), d)
            nn[m0, :] = nm[:, 0:d]
            mm[m0, :] = nm[:, d:2 * d].astype(bf16)
            oo[r, :] = oq[:, 0:d]
            qq[r, :] = (q * egc - oq[:, d:2 * d]).astype(bf16)
            dd[pl.ds(pl.multiple_of(i * 8, 8), 8), :] = jnp.broadcast_to(jnp.exp(g_last), (8, d))
        return carry

    lax.fori_loop(0, nb * n_chunks // DN_PREP_UNROLL, chunk_prep, 0)

    def chunk_scan(i, states):
        ids = [b * n_chunks + i for b in range(nb)]
        blocks = [pl.ds(pl.multiple_of(j * d, d), d) for j in ids]
        sbs = [s.astype(bf16) for s in states]
        for blk, sb in zip(blocks, sbs):
            ss[blk, :] = sb
        prods = [jnp.dot(mm[blk, :], sb, preferred_element_type=jnp.float32) for blk, sb in zip(blocks, sbs)]
        return tuple(s * dd[pl.ds(pl.multiple_of(j * 8, 8), 1), :] - pr + nn[blk, :]
                     for s, j, blk, pr in zip(states, ids, blocks, prods))

    lax.fori_loop(0, n_chunks, chunk_scan, tuple(jnp.zeros((d, d), jnp.float32) for _ in range(nb)))

    nw = nw_ref[...]

    def chunk_out(it, carry):
        ids = [it * DN_OUT_UNROLL + cc for cc in range(DN_OUT_UNROLL)]
        rows = [pl.ds(pl.multiple_of(i * c, c), c) for i in ids]
        outs = [jnp.dot(qq[r, :], ss[pl.ds(pl.multiple_of(i * d, d), d), :], preferred_element_type=jnp.float32)
                + oo[r, :] for i, r in zip(ids, rows)]
        for r, o in zip(rows, outs):
            o = o * lax.rsqrt(jnp.mean(o * o, axis=-1, keepdims=True) + 1e-6) * nw
            z = z_ref[r, :]
            o_ref[r, :] = (o * (z * jax.nn.sigmoid(z))).astype(o_ref.dtype)
        return carry

    lax.fori_loop(0, nb * n_chunks // DN_OUT_UNROLL, chunk_out, 0)


def _deltanet(pf, conv_w, a_log, dt_bias, norm_w, bsz, seq):
    d = DN_HEAD_DIM
    nb = DN_BATCHES if bsz % DN_BATCHES == 0 else 1
    rows = nb * seq
    n_chunks = rows // DN_CHUNK
    qkv0 = PF_QKVB // d
    scal = jnp.stack([a_log, dt_bias]).astype(jnp.float32)
    f32 = jnp.float32
    bf16 = jnp.bfloat16
    cw4 = conv_w.astype(f32).reshape(DN_CONV, 3, DN_HEADS, d).transpose(2, 1, 0, 3)
    assert 2 * DN_CHUNK == d and n_chunks % DN_PREP_UNROLL == 0 and n_chunks % DN_OUT_UNROLL == 0
    return pl.pallas_call(
        functools.partial(_dn_kernel, seq=seq),
        grid=(bsz // nb, DN_HEADS),
        in_specs=[
            pl.BlockSpec(memory_space=pltpu.SMEM),
            pl.BlockSpec((rows, d), lambda b, h: (b, qkv0 + h)),
            pl.BlockSpec((rows, d), lambda b, h: (b, qkv0 + DN_HEADS + h)),
            pl.BlockSpec((rows, d), lambda b, h: (b, qkv0 + 2 * DN_HEADS + h)),
            pl.BlockSpec((rows, LANES), lambda b, h: (b, PF_SMALL // LANES)),
            pl.BlockSpec((rows, d), lambda b, h: (b, PF_ZB // d + h)),
            pl.BlockSpec((None, 3, DN_CONV, d), lambda b, h: (h, 0, 0, 0)),
            pl.BlockSpec((1, d), lambda b, h: (0, 0)),
        ],
        out_specs=pl.BlockSpec((rows, d), lambda b, h: (b, h)),
        out_shape=jax.ShapeDtypeStruct((bsz * seq, DN_WIDTH), jnp.bfloat16),
        scratch_shapes=[
            pltpu.VMEM((rows, d), f32), pltpu.VMEM((rows, d), f32), pltpu.VMEM((rows, d), f32),
            pltpu.VMEM((rows, d), f32), pltpu.VMEM((rows, d), f32),
            pltpu.VMEM((n_chunks * d, d), bf16), pltpu.VMEM((n_chunks * d, d), f32),
            pltpu.VMEM((rows, d), bf16), pltpu.VMEM((rows, d), f32),
            pltpu.VMEM((n_chunks * 8, d), f32),
            pltpu.VMEM((n_chunks * d, d), bf16),
        ],
        compiler_params=_cparams(2), name="deltanet",
    )(scal, pf, pf, pf, pf, pf, cw4, norm_w.astype(f32).reshape(1, d))


def _out_kernel(oa_ref, ob_ref, gma_ref, gmb_ref, x_ref, p_ref, wa_ref, wb_ref, wo_ref, wpg_ref, wp_ref,
                lng_ref, lnb_ref, o_ref):
    f32 = jnp.float32
    y_a = jnp.dot(oa_ref[...], wa_ref[...], preferred_element_type=f32)
    y_b = jnp.dot(ob_ref[...], wb_ref[...], preferred_element_type=f32)
    mix = jax.nn.sigmoid(gma_ref[...]) * y_a + jax.nn.sigmoid(gmb_ref[...]) * y_b
    mixed = jnp.dot(mix.astype(jnp.bfloat16), wo_ref[...], preferred_element_type=f32)
    h = DEEPNORM_ALPHA * x_ref[...] + mixed
    gate = jax.nn.sigmoid(jnp.dot(h.astype(jnp.bfloat16), wpg_ref[...], preferred_element_type=f32))
    h = h + gate * jnp.dot(p_ref[...].astype(jnp.bfloat16), wp_ref[...], preferred_element_type=f32)
    mu = jnp.mean(h, axis=-1, keepdims=True)
    hc = h - mu
    var = jnp.mean(hc * hc, axis=-1, keepdims=True)
    o_ref[...] = (hc * lax.rsqrt(var + 1e-5) * lng_ref[...] + lnb_ref[...]).astype(o_ref.dtype)


def _out_block(o_a, o_b, pf, x2, p2, wa, wb, wo, wpg, wp, ln_g, ln_b, tm):
    t = x2.shape[0]
    bf = jnp.bfloat16

    def full(shape):
        return pl.BlockSpec(shape, lambda i: (0, 0))

    return pl.pallas_call(
        _out_kernel,
        grid=(t // tm,),
        in_specs=[
            pl.BlockSpec((tm, NSA_WIDTH), lambda i: (i, 0)),
            pl.BlockSpec((tm, DN_WIDTH), lambda i: (i, 0)),
            pl.BlockSpec((tm, D_MODEL), lambda i: (i, PF_GM // D_MODEL)),
            pl.BlockSpec((tm, D_MODEL), lambda i: (i, PF_GM // D_MODEL + 1)),
            pl.BlockSpec((tm, D_MODEL), lambda i: (i, 0)),
            pl.BlockSpec((tm, PLE_DIM), lambda i: (i, 0)),
            full((NSA_WIDTH, D_MODEL)), full((DN_WIDTH, D_MODEL)), full((D_MODEL, D_MODEL)),
            full((D_MODEL, D_MODEL)), full((PLE_DIM, D_MODEL)), full((1, D_MODEL)), full((1, D_MODEL)),
        ],
        out_specs=pl.BlockSpec((tm, D_MODEL), lambda i: (i, 0)),
        out_shape=jax.ShapeDtypeStruct((t, D_MODEL), x2.dtype),
        compiler_params=_cparams(1), name="out_block",
    )(o_a, o_b, pf, pf, x2, p2, wa.astype(bf), wb.astype(bf), wo.astype(bf), wpg.astype(bf), wp.astype(bf),
      ln_g.astype(jnp.float32).reshape(1, D_MODEL), ln_b.astype(jnp.float32).reshape(1, D_MODEL))


def _layer(x, p, w_in, pos_k, pos_v, w1_k, w2_k, w1_v, w2_v, bias_tabs, conv_w, a_log, dt_bias, norm_w,
           w_a, w_b, w_o, w_ple, w_pg, ln_g, ln_b):
    bsz, seq, _ = x.shape
    t = bsz * seq
    x2 = x.reshape(t, D_MODEL)
    xb = x2.astype(jnp.bfloat16)
    wb16, wf16 = _prep_w_in(w_in)
    tm = PROJ_TM if t % PROJ_TM == 0 else seq
    pb = _matmul(xb, wb16, jnp.bfloat16, tm, PB_WIDTH // 2, "proj_bf16")
    pf = _matmul(xb, wf16, jnp.float32, tm, PROJ_TN, "proj_f32")

    nc = seq // CMP_STRIDE
    rkv = pf[:, PF_KC:PF_KC + 2 * NSA_KV].reshape(bsz, nc, CMP_STRIDE, 2, NSA_KV)
    rkv = rkv.transpose(0, 3, 1, 2, 4).reshape(bsz, 2, nc, CMP_STRIDE * NSA_KV)
    pos2, w1p, w2p = _prep_compress_weights(pos_k, pos_v, w1_k, w1_v, w2_k, w2_v)
    kcvc = _compress(rkv, pos2, w1p, w2p)

    bias_sel, bias_win, bias_cmp = bias_tabs
    o_a = _nsa(pb, pf, kcvc, bias_sel, bias_win, bias_cmp, bsz, seq)
    o_b = _deltanet(pf, conv_w, a_log, dt_bias, norm_w, bsz, seq)
    out = _out_block(o_a, o_b, pf, x2, p.reshape(t, PLE_DIM), w_a, w_b, w_o, w_pg, w_ple, ln_g, ln_b,
                     OUT_TM if t % OUT_TM == 0 else seq)
    return out.reshape(bsz, seq, D_MODEL)


def kernel(x, p, w_in, cmp_pos_k, cmp_pos_v, cmp_w1_k, cmp_w2_k, cmp_w1_v, cmp_w2_v, rel_bias, dn_conv_w,
           dn_a_log, dn_dt_bias, dn_norm_w, w_branch_a, w_branch_b, w_out, w_ple, w_ple_gate, ln_g, ln_b):
    depth = w_in.shape[0]
    bias_tabs = _bias_tables(rel_bias, x.shape[1])
    for i in range(depth):
        x = _layer(x, p[i], w_in[i], cmp_pos_k[i], cmp_pos_v[i], cmp_w1_k[i], cmp_w2_k[i], cmp_w1_v[i],
                   cmp_w2_v[i], bias_tabs, dn_conv_w[i], dn_a_log[i], dn_dt_bias[i], dn_norm_w[i],
                   w_branch_a[i], w_branch_b[i], w_out[i], w_ple[i], w_ple_gate[i], ln_g[i], ln_b[i])
    return x
```

```python
import functools
import math

import numpy as np
import jax
import jax.numpy as jnp
from jax import lax
from jax.experimental import pallas as pl
from jax.experimental.pallas import tpu as pltpu

D_MODEL = 1024
PLE_DIM = 256
NSA_HEADS = 8
NSA_GROUPS = 2
NSA_HPG = NSA_HEADS // NSA_GROUPS
NSA_HEAD_DIM = 64
NSA_WIDTH = NSA_HEADS * NSA_HEAD_DIM
NSA_KV = NSA_GROUPS * NSA_HEAD_DIM
CMP_BLOCK = 32
CMP_STRIDE = 16
CMP_HIDDEN = 256
SEL_BLOCK = 64
SEL_TOPK = 8
WINDOW = 512
DN_HEADS = 4
DN_HEAD_DIM = 128
DN_WIDTH = DN_HEADS * DN_HEAD_DIM
DN_CONV = 4
DN_CHUNK = 64
NUM_BUCKETS = 32
REL_MAX_DIST = 1024
DEEPNORM_ALPHA = 2.0 ** 0.25
NEG = -1e30
FORCE = 1e6
LOG2E = 1.4426950408889634

LANES = 128
TILE = 128
SEL_LANE0 = 64
SEL_GROUP = 4
VMEM_LIMIT = 56 * 1024 * 1024
PROJ_TM = 512
PROJ_TN = 1024
OUT_TM = 512
DN_INV_BLOCK = 16
DN_BATCHES = 2
DN_OUT_UNROLL = 8
DN_DECAY_PARTS = 2
DN_PREP_UNROLL = 16

HIGHEST = lax.Precision.HIGHEST

PB_QP = 0
PB_KS = PB_QP + NSA_WIDTH
PB_KW = PB_KS + NSA_KV
PB_VS = PB_KW + NSA_KV
PB_VW = PB_VS + NSA_KV
PB_WIDTH = PB_VW + NSA_KV
PF_GM = 0
PF_QKVB = PF_GM + 2 * D_MODEL
PF_ZA = PF_QKVB + 3 * DN_WIDTH
PF_ZB = PF_ZA + NSA_WIDTH
PF_KC = PF_ZB + DN_WIDTH
PF_VC = PF_KC + NSA_KV
PF_SMALL = PF_VC + NSA_KV
PF_WIDTH = PF_SMALL + LANES
SMALL_BETA = 3 * NSA_HEADS
SMALL_A = SMALL_BETA + DN_HEADS


def _bucket_thresholds():
    max_exact = NUM_BUCKETS // 2
    span = NUM_BUCKETS - max_exact
    ratio = REL_MAX_DIST // max_exact
    thr = list(range(1, max_exact + 1))
    for k in range(1, span):
        n = max_exact
        while n ** span < max_exact ** span * ratio ** k:
            n += 1
        thr.append(n)
    return tuple(thr)


_THR = _bucket_thresholds()


def _cparams(n_axes):
    return pltpu.CompilerParams(dimension_semantics=("arbitrary",) * n_axes, vmem_limit_bytes=VMEM_LIMIT)


def _proj_kernel(x_ref, wb_ref, wf_ref, pb_ref, pf_ref):
    xb = x_ref[...].astype(jnp.bfloat16)
    pb_ref[...] = jnp.dot(xb, wb_ref[...], preferred_element_type=jnp.float32).astype(pb_ref.dtype)
    n = pf_ref.shape[1]
    for c0 in range(0, n, PROJ_TN):
        c1 = min(c0 + PROJ_TN, n)
        pf_ref[:, c0:c1] = jnp.dot(xb, wf_ref[:, c0:c1], preferred_element_type=jnp.float32)


def _proj(x2, wb, wf, tm):
    t, d = x2.shape
    resident = dict(pipeline_mode=pl.Buffered(1))
    return pl.pallas_call(
        _proj_kernel,
        grid=(t // tm,),
        in_specs=[pl.BlockSpec((tm, d), lambda i: (i, 0)),
                  pl.BlockSpec((d, PB_WIDTH), lambda i: (0, 0), **resident),
                  pl.BlockSpec((d, PF_WIDTH), lambda i: (0, 0), **resident)],
        out_specs=[pl.BlockSpec((tm, PB_WIDTH), lambda i: (i, 0)),
                   pl.BlockSpec((tm, PF_WIDTH), lambda i: (i, 0))],
        out_shape=[jax.ShapeDtypeStruct((t, PB_WIDTH), jnp.bfloat16),
                   jax.ShapeDtypeStruct((t, PF_WIDTH), jnp.float32)],
        compiler_params=_cparams(1), name="proj",
    )(x2, wb, wf)


def _prep_w_in(w):
    d = w.shape[0]
    o = 0
    wq = w[:, o:o + NSA_WIDTH]; o += NSA_WIDTH
    wkv = w[:, o:o + 6 * NSA_KV]; o += 6 * NSA_KV
    wg = w[:, o:o + 3 * NSA_HEADS]; o += 3 * NSA_HEADS
    wza = w[:, o:o + NSA_WIDTH]; o += NSA_WIDTH
    wqkvb = w[:, o:o + 3 * DN_WIDTH]; o += 3 * DN_WIDTH
    wbeta_a = w[:, o:o + 2 * DN_HEADS]; o += 2 * DN_HEADS
    wzb = w[:, o:o + DN_WIDTH]; o += DN_WIDTH
    wgm = w[:, o:o + 2 * D_MODEL]
    wkcvc, wks, wvs, wkw, wvw = (wkv[:, 0:2 * NSA_KV], wkv[:, 2 * NSA_KV:3 * NSA_KV], wkv[:, 3 * NSA_KV:4 * NSA_KV],
                                 wkv[:, 4 * NSA_KV:5 * NSA_KV], wkv[:, 5 * NSA_KV:6 * NSA_KV])
    wb = jnp.concatenate([wq * (NSA_HEAD_DIM ** -0.5 * LOG2E), wks, wkw, wvs, wvw], axis=1).astype(jnp.bfloat16)
    pad = jnp.zeros((d, LANES - 3 * NSA_HEADS - 2 * DN_HEADS), w.dtype)
    wf = jnp.concatenate([wgm, wqkvb, wza, wzb, wkcvc, wg, wbeta_a, pad], axis=1).astype(jnp.bfloat16)
    return wb, wf


def _bias_kernel(tab_ref, sel_ref, win_ref, cmp_ref, *, n_cmp):
    h = pl.program_id(0)

    def lookup(n):
        val = jnp.full(n.shape, tab_ref[0, h], jnp.float32)
        for b in range(1, NUM_BUCKETS):
            val = jnp.where(n >= _THR[b - 1], tab_ref[b, h], val)
        return val * LOG2E

    kj = lax.broadcasted_iota(jnp.int32, (TILE, TILE), 0)
    qi = lax.broadcasted_iota(jnp.int32, (TILE, TILE), 1)
    n_sel_tiles = sel_ref.shape[1] - 1
    n_win_tiles = win_ref.shape[1] - 1
    for dt in range(max(n_sel_tiles, n_win_tiles)):
        dist = dt * TILE + qi - kj
        v = lookup(jnp.maximum(dist, 0))
        if dt < n_sel_tiles:
            sel_ref[0, dt] = jnp.where(dist >= 0, v, NEG)
        if dt < n_win_tiles:
            win_ref[0, dt] = jnp.where((dist >= 0) & (dist < WINDOW), v, NEG)
    sel_ref[0, n_sel_tiles] = jnp.full((TILE, TILE), NEG, jnp.float32)
    win_ref[0, n_win_tiles] = jnp.full((TILE, TILE), NEG, jnp.float32)
    c = lax.broadcasted_iota(jnp.int32, cmp_ref.shape[2:], 0)
    for t in range(cmp_ref.shape[1]):
        s = t * TILE + lax.broadcasted_iota(jnp.int32, cmp_ref.shape[2:], 1)
        dist = s - (c * CMP_STRIDE + CMP_BLOCK - 1)
        cmp_ref[0, t] = jnp.where((dist >= 0) & (c < n_cmp), lookup(jnp.maximum(dist, 0)), NEG)


def _bias_tables(rel_bias, seq):
    nq = seq // TILE
    nwin = WINDOW // TILE + 1
    nc = seq // CMP_STRIDE
    n_cmp = nc - CMP_BLOCK // CMP_STRIDE + 1
    return pl.pallas_call(
        functools.partial(_bias_kernel, n_cmp=n_cmp),
        grid=(NSA_HEADS,),
        in_specs=[pl.BlockSpec(memory_space=pltpu.SMEM)],
        out_specs=[pl.BlockSpec((1, nq + 1, TILE, TILE), lambda h: (h, 0, 0, 0)),
                   pl.BlockSpec((1, nwin + 1, TILE, TILE), lambda h: (h, 0, 0, 0)),
                   pl.BlockSpec((1, nq, nc, TILE), lambda h: (h, 0, 0, 0))],
        out_shape=[jax.ShapeDtypeStruct((NSA_HEADS, nq + 1, TILE, TILE), jnp.float32),
                   jax.ShapeDtypeStruct((NSA_HEADS, nwin + 1, TILE, TILE), jnp.float32),
                   jax.ShapeDtypeStruct((NSA_HEADS, nq, nc, TILE), jnp.float32)],
        compiler_params=_cparams(1), name="bias_tables",
    )(rel_bias.astype(jnp.float32))


def _gelu_tanh(x):
    return x * (0.5 * (1.0 + jnp.tanh(math.sqrt(2.0 / math.pi) * (x + 0.044715 * (x * x * x)))))


def _compress_kernel(x_ref, pos_ref, w1k_ref, w1v_ref, w2_ref, o_ref, wbd):
    kv = pl.program_id(0)
    nc = o_ref.shape[2]
    half = CMP_BLOCK // 2
    dh = NSA_HEAD_DIM
    bf16 = jnp.bfloat16

    @pl.when(pl.program_id(1) == 0)
    def _():
        def build(w1_ref):
            zero = jnp.zeros((dh, CMP_HIDDEN), bf16)
            for a in range(2):
                for l in range(half):
                    r0 = (a * half + l) * dh
                    wl = w1_ref[r0:r0 + dh, :].astype(bf16)
                    wbd[a, l * NSA_KV:(l + 1) * NSA_KV, :] = jnp.concatenate(
                        [jnp.concatenate([wl, zero], axis=1), jnp.concatenate([zero, wl], axis=1)], axis=0)
        pl.when(kv == 0)(functools.partial(build, w1k_ref))
        pl.when(kv == 1)(functools.partial(build, w1v_ref))

    r = jnp.concatenate([x_ref[pl.ds(l, nc, stride=CMP_STRIDE), :] for l in range(CMP_STRIDE)], axis=1)
    a = jnp.dot((r + pos_ref[0, 0:1, :]).astype(bf16), wbd[0], preferred_element_type=jnp.float32)
    b = jnp.dot((r + pos_ref[0, 1:2, :]).astype(bf16), wbd[1], preferred_element_type=jnp.float32)
    hid = a + pltpu.roll(b, nc - 1, 0)
    o_ref[0, 0] = jnp.dot(_gelu_tanh(hid).astype(bf16), w2_ref[0], preferred_element_type=jnp.float32)


def _compress(pf, pos2, w1_k, w1_v, w2p, bsz, seq):
    nc = seq // CMP_STRIDE
    width = CMP_STRIDE * NSA_KV
    hid = NSA_GROUPS * CMP_HIDDEN
    assert CMP_BLOCK == 2 * CMP_STRIDE
    return pl.pallas_call(
        _compress_kernel,
        grid=(2, bsz),
        in_specs=[pl.BlockSpec((seq, NSA_KV), lambda k, b: (b, PF_KC // NSA_KV + k)),
                  pl.BlockSpec((1, 2, width), lambda k, b: (k, 0, 0)),
                  pl.BlockSpec(w1_k.shape, lambda k, b: (0, 0)),
                  pl.BlockSpec(w1_v.shape, lambda k, b: (0, 0)),
                  pl.BlockSpec((1, hid, LANES), lambda k, b: (k, 0, 0))],
        out_specs=pl.BlockSpec((1, 1, nc, LANES), lambda k, b: (b, k, 0, 0)),
        out_shape=jax.ShapeDtypeStruct((bsz, 2, nc, LANES), jnp.float32),
        scratch_shapes=[pltpu.VMEM((2, width, hid), jnp.bfloat16)],
        compiler_params=_cparams(2), name="compress",
    )(pf, pos2, w1_k, w1_v, w2p)


def _prep_compress_weights(pos_k, pos_v, w2_k, w2_v):
    eye = jnp.eye(NSA_GROUPS, dtype=jnp.float32)
    half = CMP_BLOCK // 2

    def w2_both(w2):
        return jnp.einsum('jd,gh->gjhd', w2, eye).reshape(NSA_GROUPS * CMP_HIDDEN, NSA_GROUPS * NSA_HEAD_DIM)

    def pos_both(pos):
        p = pos.reshape(2, half, 1, NSA_HEAD_DIM)
        return jnp.broadcast_to(p, (2, half, NSA_GROUPS, NSA_HEAD_DIM)).reshape(2, half * NSA_KV)

    pos2 = jnp.stack([pos_both(pos_k), pos_both(pos_v)]).astype(jnp.float32)
    w2p = jnp.stack([w2_both(w2_k), w2_both(w2_v)]).astype(jnp.bfloat16)
    return pos2, w2p


def _pair_lanes(x, g):
    sw = pltpu.roll(x, LANES // 2, 1)
    lane = lax.broadcasted_iota(jnp.int32, x.shape, 1)
    own = (lane < LANES // 2) == (g == 0)
    return jnp.where(own, x, sw)


def _group_rows_t(x, g):
    xt = x.T
    half = LANES // 2
    return jnp.where(g == 0, xt[0:half, :], xt[half:LANES, :])


def _nsa_kernel(q_ref, ks_ref, kw_ref, vs_ref, vw_ref, kcvc_ref, bc_ref, bsel_ref, bwin_ref,
                small_ref, za_ref, o_ref,
                ksa, kwp, vst, vwt, kcs, vct, qa, qp, sbuf, acc_s, l_s, *, n_cmp, n_sel, n_top):
    g = pl.program_id(1)
    qt = pl.program_id(2)
    tq = TILE
    seq = ks_ref.shape[0]
    nq = seq // TILE
    nc = kcvc_ref.shape[2]
    n_wt = bwin_ref.shape[1] - 1
    heads = range(NSA_HPG)
    f32 = jnp.float32
    bf16 = jnp.bfloat16
    nt_dims = (((1,), (1,)), ((), ()))

    @pl.when(qt == 0)
    def _():
        row = lax.broadcasted_iota(jnp.int32, (seq, LANES), 0)
        lane = lax.broadcasted_iota(jnp.int32, (seq, LANES), 1)
        onehot = (lane - SEL_LANE0) == (row >> 6)
        ksa[...] = jnp.where(onehot, 1.0, ks_ref[...].astype(f32)).astype(bf16)
        pad = (n_wt - 1) * TILE
        kwp[0:pad, :] = jnp.zeros((pad, LANES), bf16)
        kwp[pad:pad + seq, :] = kw_ref[...]
        for kt in range(n_wt - 1):
            vwt[kt] = jnp.zeros(vwt.shape[1:], bf16)
        for kt in range(nq):
            rows = slice(kt * TILE, (kt + 1) * TILE)
            vst[kt] = _group_rows_t(vs_ref[rows, :].astype(f32), g).astype(bf16)
            vwt[kt + n_wt - 1] = _group_rows_t(vw_ref[rows, :].astype(f32), g).astype(bf16)
        kcs[...] = _pair_lanes(kcvc_ref[0, 0], g).astype(bf16)
        for ct in range(nc // TILE):
            rows = slice(ct * TILE, (ct + 1) * TILE)
            vct[:, rows] = _group_rows_t(kcvc_ref[0, 1, rows, :], g).astype(bf16)

    qh = [q_ref[:, hh * LANES:(hh + 1) * LANES] for hh in heads]
    for hh in heads:
        qp[hh * tq:(hh + 1) * tq, :] = qh[hh]
    q_all = qp[...]

    def softmax_pv(s, v_t):
        m = jnp.max(s, axis=0, keepdims=True)
        p = jnp.exp2(s - m)
        return jnp.dot(v_t, p.astype(bf16), preferred_element_type=f32), jnp.sum(p, axis=0, keepdims=True)

    s_c = lax.dot_general(kcs[...], q_all, nt_dims, preferred_element_type=f32)
    kw_rows = kwp[pl.ds(pl.multiple_of(qt * TILE, TILE), n_wt * TILE), :]
    s_w = lax.dot_general(kw_rows, q_all, nt_dims, preferred_element_type=f32)

    bias_c = jnp.concatenate([bc_ref[hh] for hh in heads], axis=1)
    valid = bias_c > 0.5 * NEG
    s_c = s_c + bias_c
    e = jnp.where(valid, jnp.exp2(s_c - jnp.max(s_c, axis=0, keepdims=True)), 0.0)
    den = jnp.maximum(jnp.sum(e, axis=0, keepdims=True), 1e-30)
    p_c = e * (1.0 / den)
    o_cmp = jnp.dot(vct[...], p_c.astype(bf16), preferred_element_type=f32)
    psum = sum(p_c[:, hh * tq:(hh + 1) * tq] for hh in heads)

    w_tiles = []
    for t in range(n_wt):
        dt = n_wt - 1 - t
        idx = jnp.where(qt >= dt, dt, n_wt)
        w_tiles.append(jnp.concatenate([bwin_ref[hh, idx] for hh in heads], axis=1))
    acc_w, l_w = softmax_pv(s_w + jnp.concatenate(w_tiles, axis=0),
                            jnp.concatenate([vwt[qt + t] for t in range(n_wt)], axis=1))
    o_win = acc_w * (1.0 / l_w)

    sj = lax.broadcasted_iota(jnp.int32, (n_sel, nc), 0)
    ci = lax.broadcasted_iota(jnp.int32, (n_sel, nc), 1)
    overlap = ((ci * CMP_STRIDE < (sj + 1) * SEL_BLOCK) & (ci * CMP_STRIDE + CMP_BLOCK > sj * SEL_BLOCK)
               & (ci < n_cmp)).astype(bf16)
    imp_t = sum(jnp.dot(overlap, part, preferred_element_type=f32)
                for part in _split_bf16(psum, 3))
    blk = lax.broadcasted_iota(jnp.int32, (n_sel, tq), 0)
    cur = (qt * tq + lax.broadcasted_iota(jnp.int32, (n_sel, tq), 1)) >> 6
    forced = (blk == 0) | (blk == cur) | (blk == cur - 1)
    imp_t = jnp.where(forced, FORCE, jnp.where(blk > cur, -FORCE, imp_t))
    rank = jnp.zeros((n_sel, tq), jnp.int32)
    for j in range(n_sel):
        other = imp_t[j:j + 1, :]
        ahead = (other > imp_t) | ((other == imp_t) & (blk > j))
        rank = rank + ahead.astype(jnp.int32)
    selb = jnp.where(rank < n_top, 0.0, NEG)
    pieces = [jnp.zeros((SEL_LANE0, tq), jnp.float32), selb]
    if LANES - SEL_LANE0 - n_sel > 0:
        pieces.append(jnp.zeros((LANES - SEL_LANE0 - n_sel, tq), jnp.float32))
    selb_r = jnp.concatenate(pieces, axis=0).T.astype(jnp.bfloat16)

    for hh in heads:
        qa[hh * tq:(hh + 1) * tq, :] = qh[hh] + selb_r
    gk = SEL_GROUP * TILE

    def sel_case(n_groups):
        q_aug = qa[...]
        col_max = []
        for gi in range(n_groups):
            s = lax.dot_general(ksa[gi * gk:(gi + 1) * gk, :], q_aug, nt_dims, preferred_element_type=f32)
            tiles = []
            for t in range(SEL_GROUP):
                dt = qt - (gi * SEL_GROUP + t)
                idx = jnp.where(dt >= 0, dt, nq)
                tiles.append(jnp.concatenate([bsel_ref[hh, idx] for hh in heads], axis=1))
            s = s + jnp.concatenate(tiles, axis=0)
            sbuf[gi] = s
            col_max.append(jnp.max(s, axis=0, keepdims=True))
        m = functools.reduce(jnp.maximum, col_max)
        acc = jnp.zeros(acc_s.shape, f32)
        l = jnp.zeros(l_s.shape, f32)
        for gi in range(n_groups):
            p = jnp.exp2(sbuf[gi] - m)
            v_t = jnp.concatenate([vst[gi * SEL_GROUP + t] for t in range(SEL_GROUP)], axis=1)
            acc = acc + jnp.dot(v_t, p.astype(bf16), preferred_element_type=f32)
            l = l + jnp.sum(p, axis=0, keepdims=True)
        acc_s[...] = acc
        l_s[...] = l

    for n_groups in range(1, nq // SEL_GROUP + 1):
        pl.when(qt // SEL_GROUP == n_groups - 1)(functools.partial(sel_case, n_groups))
    o_sel = acc_s[...] * (1.0 / l_s[...])

    gates_t = jax.nn.sigmoid(small_ref[...]).T
    mixed = []
    for hh in heads:
        def gate(br):
            r = br * NSA_HEADS + hh
            return jnp.where(g == 0, gates_t[r:r + 1, :], gates_t[r + NSA_HPG:r + NSA_HPG + 1, :])
        cols = slice(hh * tq, (hh + 1) * tq)
        mixed.append(gate(0) * o_cmp[:, cols] + gate(1) * o_sel[:, cols] + gate(2) * o_win[:, cols])
    o = jnp.concatenate([jnp.concatenate(mixed[2 * j:2 * j + 2], axis=0).T for j in range(NSA_HPG // 2)], axis=1)
    z = za_ref[...]
    o_ref[...] = (o * (z * jax.nn.sigmoid(z))).astype(o_ref.dtype)


def _nsa_step_kernel(q_ref, qn_ref, ks_ref, kw_ref, vs_ref, vw_ref, kcvc_ref, bc_ref, bsel_ref, bwin_ref,
                     small_ref, za_ref, o_ref,
                     ksa, kwp, vst, vwt, kcs, vct, qa_s, qp_s, ocmp_s, sbuf, *, n_cmp, n_sel, n_top):
    g = pl.program_id(0)
    qt = pl.program_id(2)
    tq = TILE
    seq = ks_ref.shape[0]
    nq = seq // TILE
    nc = kcvc_ref.shape[2]
    n_wt = bwin_ref.shape[1] - 1
    heads = range(NSA_HPG)
    f32 = jnp.float32
    bf16 = jnp.bfloat16
    nt_dims = (((1,), (1,)), ((), ()))
    gk = SEL_GROUP * TILE

    def select_tile(src_ref, tile, slot):
        q32 = src_ref[...].astype(f32)
        low_half = lax.broadcasted_iota(jnp.int32, (tq, LANES), 1) < NSA_HEAD_DIM
        qh = []
        for hh in heads:
            blk = q32[:, (hh // 2) * LANES:(hh // 2 + 1) * LANES]
            if hh % 2:
                blk = pltpu.roll(blk, LANES // 2, 1)
            qh.append(jnp.where(low_half, blk, 0.0).astype(bf16))
        for hh in heads:
            qp_s[slot, hh * tq:(hh + 1) * tq, :] = qh[hh]
        q_all = qp_s[slot]
        s_c = lax.dot_general(kcs[...], q_all, nt_dims, preferred_element_type=f32)
        bias_c = jnp.concatenate([bc_ref[hh, tile] for hh in heads], axis=1)
        valid = bias_c > 0.5 * NEG
        s_c = s_c + bias_c
        e = jnp.where(valid, jnp.exp2(s_c - jnp.max(s_c, axis=0, keepdims=True)), 0.0)
        den = jnp.maximum(jnp.sum(e, axis=0, keepdims=True), 1e-30)
        p_c = e * (1.0 / den)
        psum = sum(p_c[:, hh * tq:(hh + 1) * tq] for hh in heads)
        sj = lax.broadcasted_iota(jnp.int32, (n_sel, nc), 0)
        ci = lax.broadcasted_iota(jnp.int32, (n_sel, nc), 1)
        overlap = ((ci * CMP_STRIDE < (sj + 1) * SEL_BLOCK) & (ci * CMP_STRIDE + CMP_BLOCK > sj * SEL_BLOCK)
                   & (ci < n_cmp)).astype(bf16)
        imp_t = sum(jnp.dot(overlap, part, preferred_element_type=f32)
                    for part in _split_bf16(psum, 3))
        ocmp_s[slot] = jnp.dot(vct[...], p_c.astype(bf16), preferred_element_type=f32)
        blk = lax.broadcasted_iota(jnp.int32, (n_sel, tq), 0)
        cur = (tile * tq + lax.broadcasted_iota(jnp.int32, (n_sel, tq), 1)) >> 6
        forced = (blk == 0) | (blk == cur) | (blk == cur - 1)
        imp_t = jnp.where(forced, FORCE, jnp.where(blk > cur, -FORCE, imp_t))
        rank = jnp.zeros((n_sel, tq), jnp.int32)
        for j in range(n_sel):
            other = imp_t[j:j + 1, :]
            ahead = (other > imp_t) | ((other == imp_t) & (blk > j))
            rank = rank + ahead.astype(jnp.int32)
        selb = jnp.where(rank < n_top, 0.0, NEG)
        pieces = [jnp.zeros((SEL_LANE0, tq), f32), selb]
        if LANES - SEL_LANE0 - n_sel > 0:
            pieces.append(jnp.zeros((LANES - SEL_LANE0 - n_sel, tq), f32))
        selb_r = jnp.concatenate(pieces, axis=0).T.astype(bf16)
        for hh in heads:
            qa_s[slot, hh * tq:(hh + 1) * tq, :] = qh[hh] + selb_r

    @pl.when(qt == 0)
    def _():
        row = lax.broadcasted_iota(jnp.int32, (seq, LANES), 0)
        lane = lax.broadcasted_iota(jnp.int32, (seq, LANES), 1)
        onehot = (lane - SEL_LANE0) == (row >> 6)
        own = lane < NSA_HEAD_DIM
        ks_g = jnp.where(own, _pair_lanes(ks_ref[...].astype(f32), g), 0.0)
        ksa[...] = jnp.where(onehot, 1.0, ks_g).astype(bf16)
        pad = (n_wt - 1) * TILE
        kwp[0:pad, :] = jnp.zeros((pad, LANES), bf16)
        kwp[pad:pad + seq, :] = jnp.where(own, _pair_lanes(kw_ref[...].astype(f32), g), 0.0).astype(bf16)
        for kt in range(n_wt - 1):
            vwt[kt] = jnp.zeros(vwt.shape[1:], bf16)
        for kt in range(nq):
            rows = slice(kt * TILE, (kt + 1) * TILE)
            vst[kt] = _group_rows_t(vs_ref[rows, :].astype(f32), g).astype(bf16)
            vwt[kt + n_wt - 1] = _group_rows_t(vw_ref[rows, :].astype(f32), g).astype(bf16)
        kcs[...] = _pair_lanes(kcvc_ref[0, 0], g).astype(bf16)
        for ct in range(nc // TILE):
            rows = slice(ct * TILE, (ct + 1) * TILE)
            vct[:, rows] = _group_rows_t(kcvc_ref[0, 1, rows, :], g).astype(bf16)
        select_tile(q_ref, 0, 0)

    slot = qt % 2

    def step(n_groups):
        q_all = qp_s[slot]
        q_aug = qa_s[slot]
        kw_rows = kwp[pl.ds(pl.multiple_of(qt * TILE, TILE), n_wt * TILE), :]
        s_w = lax.dot_general(kw_rows, q_all, nt_dims, preferred_element_type=f32)

        def sel_scores(gi):
            s = lax.dot_general(ksa[gi * gk:(gi + 1) * gk, :], q_aug, nt_dims, preferred_element_type=f32)
            tiles = []
            for t in range(SEL_GROUP):
                dt = qt - (gi * SEL_GROUP + t)
                idx = jnp.where(dt >= 0, dt, nq)
                tiles.append(jnp.concatenate([bsel_ref[hh, idx] for hh in heads], axis=1))
            sbuf[gi % 2] = s + jnp.concatenate(tiles, axis=0)

        sel_scores(0)
        select_tile(qn_ref, jnp.minimum(qt + 1, nq - 1), 1 - slot)

        w_tiles = []
        for t in range(n_wt):
            dt = n_wt - 1 - t
            idx = jnp.where(qt >= dt, dt, n_wt)
            w_tiles.append(jnp.concatenate([bwin_ref[hh, idx] for hh in heads], axis=1))
        s_w = s_w + jnp.concatenate(w_tiles, axis=0)
        p_w = jnp.exp2(s_w - jnp.max(s_w, axis=0, keepdims=True))
        v_w = jnp.concatenate([vwt[qt + t] for t in range(n_wt)], axis=1)
        o_win = (jnp.dot(v_w, p_w.astype(bf16), preferred_element_type=f32)
                 * (1.0 / jnp.sum(p_w, axis=0, keepdims=True)))

        m = jnp.full((1, NSA_HPG * tq), NEG, f32)
        l = jnp.zeros((1, NSA_HPG * tq), f32)
        acc = jnp.zeros((NSA_HEAD_DIM, NSA_HPG * tq), f32)
        for gi in range(n_groups):
            if gi + 1 < n_groups:
                sel_scores(gi + 1)
            s = sbuf[gi % 2]
            m_new = jnp.maximum(m, jnp.max(s, axis=0, keepdims=True))
            alpha = jnp.exp2(m - m_new)
            p = jnp.exp2(s - m_new)
            v_t = jnp.concatenate([vst[gi * SEL_GROUP + t] for t in range(SEL_GROUP)], axis=1)
            acc = alpha * acc + jnp.dot(v_t, p.astype(bf16), preferred_element_type=f32)
            l = alpha * l + jnp.sum(p, axis=0, keepdims=True)
            m = m_new
        o_sel = acc * (1.0 / l)
        o_cmp = ocmp_s[slot]

        gates_t = jax.nn.sigmoid(small_ref[...]).T
        mixed = []
        for hh in heads:
            def gate(br):
                r = br * NSA_HEADS + hh
                return jnp.where(g == 0, gates_t[r:r + 1, :], gates_t[r + NSA_HPG:r + NSA_HPG + 1, :])
            cols = slice(hh * tq, (hh + 1) * tq)
            mixed.append(gate(0) * o_cmp[:, cols] + gate(1) * o_sel[:, cols] + gate(2) * o_win[:, cols])
        o = jnp.concatenate([jnp.concatenate(mixed[2 * j:2 * j + 2], axis=0).T for j in range(NSA_HPG // 2)],
                            axis=1)
        z = za_ref[...]
        o_ref[...] = (o * (z * jax.nn.sigmoid(z))).astype(o_ref.dtype)

    for n_groups in range(1, nq // SEL_GROUP + 1):
        pl.when(qt // SEL_GROUP == n_groups - 1)(functools.partial(step, n_groups))


def _nsa(pb, pf, kcvc, bias_sel, bias_win, bias_cmp, bsz, seq):
    nq = seq // TILE
    nc = seq // CMP_STRIDE
    n_cmp = nc - CMP_BLOCK // CMP_STRIDE + 1
    n_sel = seq // SEL_BLOCK
    n_top = min(SEL_TOPK, n_sel)
    nwin = bias_win.shape[1]
    assert nq % SEL_GROUP == 0 and nc % TILE == 0 and bias_sel.shape[1] == nq + 1
    gw = NSA_HPG * NSA_HEAD_DIM
    kern = functools.partial(_nsa_step_kernel, n_cmp=n_cmp, n_sel=n_sel, n_top=n_top)
    return pl.pallas_call(
        kern,
        grid=(NSA_GROUPS, bsz, nq),
        in_specs=[
            pl.BlockSpec((TILE, gw), lambda g, b, t: (b * nq + t, PB_QP // gw + g)),
            pl.BlockSpec((TILE, gw),
                         lambda g, b, t: (b * nq + jnp.minimum(t + 1, nq - 1), PB_QP // gw + g)),
            pl.BlockSpec((seq, LANES), lambda g, b, t: (b, PB_KS // LANES)),
            pl.BlockSpec((seq, LANES), lambda g, b, t: (b, PB_KW // LANES)),
            pl.BlockSpec((seq, LANES), lambda g, b, t: (b, PB_VS // LANES)),
            pl.BlockSpec((seq, LANES), lambda g, b, t: (b, PB_VW // LANES)),
            pl.BlockSpec((1, 2, nc, LANES), lambda g, b, t: (b, 0, 0, 0)),
            pl.BlockSpec((NSA_HPG, nq, nc, TILE), lambda g, b, t: (g, 0, 0, 0)),
            pl.BlockSpec((NSA_HPG, nq + 1, TILE, TILE), lambda g, b, t: (g, 0, 0, 0)),
            pl.BlockSpec((NSA_HPG, nwin, TILE, TILE), lambda g, b, t: (g, 0, 0, 0)),
            pl.BlockSpec((TILE, LANES), lambda g, b, t: (b * nq + t, PF_SMALL // LANES)),
            pl.BlockSpec((TILE, gw), lambda g, b, t: (b * nq + t, PF_ZA // gw + g)),
        ],
        out_specs=pl.BlockSpec((TILE, gw), lambda g, b, t: (b * nq + t, g)),
        out_shape=jax.ShapeDtypeStruct((bsz * seq, NSA_WIDTH), jnp.bfloat16),
        scratch_shapes=[
            pltpu.VMEM((seq, LANES), jnp.bfloat16),
            pltpu.VMEM((seq + (nwin - 2) * TILE, LANES), jnp.bfloat16),
            pltpu.VMEM((nq, NSA_HEAD_DIM, TILE), jnp.bfloat16),
            pltpu.VMEM((nq + nwin - 2, NSA_HEAD_DIM, TILE), jnp.bfloat16),
            pltpu.VMEM((nc, LANES), jnp.bfloat16),
            pltpu.VMEM((NSA_HEAD_DIM, nc), jnp.bfloat16),
            pltpu.VMEM((2, NSA_HPG * TILE, LANES), jnp.bfloat16),
            pltpu.VMEM((2, NSA_HPG * TILE, LANES), jnp.bfloat16),
            pltpu.VMEM((2, NSA_HEAD_DIM, NSA_HPG * TILE), jnp.float32),
            pltpu.VMEM((2, SEL_GROUP * TILE, NSA_HPG * TILE), jnp.float32),
        ],
        compiler_params=_cparams(3), name="nsa",
    )(pb, pb, pb, pb, pb, pb, kcvc, bias_cmp, bias_sel, bias_win, pf, pf)


def _split_bf16(a, n):
    parts = []
    for _ in range(n - 1):
        hi = a.astype(jnp.bfloat16)
        parts.append(hi)
        a = a - hi.astype(jnp.float32)
    parts.append(a.astype(jnp.bfloat16))
    return parts


def _dot3(a, b):
    ah, al = _split_bf16(a, 2)
    bh, bl = _split_bf16(b, 2)
    f32 = jnp.float32
    return (jnp.dot(ah, bh, preferred_element_type=f32) + jnp.dot(al, bh, preferred_element_type=f32)
            + jnp.dot(ah, bl, preferred_element_type=f32))


def _softplus(x):
    return jnp.maximum(x, 0.0) + jnp.log(1.0 + jnp.exp(-jnp.abs(x)))


def _dn_kernel(scal_ref, q_ref, k_ref, v_ref, small_ref, z_ref, cw_ref, nw_ref, o_ref,
               qn, kn, vn, bet, gl, mm, nn, qq, oo, dd, ss, *, seq):
    h = pl.program_id(1)
    n_rows = q_ref.shape[0]
    nb = n_rows // seq
    c = DN_CHUNK
    n_chunks = seq // c
    d = DN_HEAD_DIM

    head = 8

    def conv_body(x_ref, which):
        n = n_rows - head
        y = x_ref[pl.ds(head, n), :] * cw_ref[which, DN_CONV - 1:DN_CONV, :]
        for j in range(DN_CONV - 1):
            y = y + x_ref[pl.ds(head - (DN_CONV - 1 - j), n), :] * cw_ref[which, j:j + 1, :]
        return y

    def conv_head(x_ref, which, r0):
        x = x_ref[pl.ds(r0, head), :]
        rowi = lax.broadcasted_iota(jnp.int32, (head, d), 0)
        y = x * cw_ref[which, DN_CONV - 1:DN_CONV, :]
        for j in range(DN_CONV - 1):
            sh = DN_CONV - 1 - j
            y = y + jnp.where(rowi >= sh, pltpu.roll(x, sh, 0), 0.0) * cw_ref[which, j:j + 1, :]
        return y

    def silu(y):
        return y * jax.nn.sigmoid(y)

    def l2n(t):
        return t * lax.rsqrt(jnp.sum(t * t, axis=-1, keepdims=True) + 1e-6)

    finish = (lambda y: l2n(silu(y)) * (d ** -0.5), lambda y: l2n(silu(y)), silu)
    for which, (x_ref, dst) in enumerate(((q_ref, qn), (k_ref, kn), (v_ref, vn))):
        dst[pl.ds(head, n_rows - head), :] = finish[which](conv_body(x_ref, which))
        for b in range(nb):
            dst[pl.ds(b * seq, head), :] = finish[which](conv_head(x_ref, which, b * seq))
    small = small_ref[...]
    lane = lax.broadcasted_iota(jnp.int32, small.shape, 1)
    beta_in = jnp.sum(jnp.where(lane == SMALL_BETA + h, small, 0.0), axis=-1, keepdims=True)
    a_in = jnp.sum(jnp.where(lane == SMALL_A + h, small, 0.0), axis=-1, keepdims=True)
    bet[...] = jnp.broadcast_to(jax.nn.sigmoid(beta_in), (n_rows, d))
    gl[...] = jnp.broadcast_to(-jnp.exp(scal_ref[0, h]) * _softplus(a_in + scal_ref[1, h]), (n_rows, d))

    ri = lax.broadcasted_iota(jnp.int32, (c, c), 0)
    cj = lax.broadcasted_iota(jnp.int32, (c, c), 1)
    incl = ri >= cj
    strict = ri > cj
    tril = incl.astype(jnp.float32)
    eye = (ri == cj).astype(jnp.float32)

    tril_b = tril.astype(jnp.bfloat16)
    bs = DN_INV_BLOCK
    sh = bs.bit_length() - 1
    same_diag = (ri >> sh) == (cj >> sh)
    level_masks = []
    while (1 << sh) < c:
        level_masks.append(((ri >> (sh + 1)) == (cj >> (sh + 1))) & ((ri >> sh) > (cj >> sh)))
        sh += 1

    bf16 = jnp.bfloat16
    nt_dims = (((1,), (1,)), ((), ()))

    def dot1(a, b):
        return jnp.dot(a.astype(bf16), b.astype(bf16), preferred_element_type=jnp.float32)

    def chunk_prep(it, carry):
        ids = [it * DN_PREP_UNROLL + cc for cc in range(DN_PREP_UNROLL)]
        rows = [pl.ds(pl.multiple_of(i * c, c), c) for i in ids]
        ks = [kn[r, :] for r in rows]
        betas = [bet[r, :] for r in rows]
        gcbs = [sum(jnp.dot(tril_b, part, preferred_element_type=jnp.float32)
                    for part in _split_bf16(gl[r, :], DN_DECAY_PARTS)) for r in rows]
        kbs = [k * beta for k, beta in zip(ks, betas)]
        kbfs = [k.astype(bf16) for k in ks]
        a_kks = [lax.dot_general(kb.astype(bf16), kbf, nt_dims, preferred_element_type=jnp.float32)
                 for kb, kbf in zip(kbs, kbfs)]
        decays = []
        for gcb in gcbs:
            gct = jnp.concatenate([gcb, gcb], axis=0).T
            diff = gcb[:, 0:c] - gct[0:c, 0:c]
            decays.append(jnp.where(incl, jnp.exp(jnp.where(incl, diff, 0.0)), 0.0))
        lows = [jnp.where(strict, a * dec, 0.0) for a, dec in zip(a_kks, decays)]
        pws = [jnp.where(same_diag, -low, 0.0) for low in lows]
        es = list(pws)
        for _ in range(max(1, (bs - 1).bit_length()) - 1):
            pws = [dot1(pw, pw) for pw in pws]
            es = [e + pw + dot1(e, pw) for e, pw in zip(es, pws)]
        for below in level_masks:
            offs = [jnp.where(below, low, 0.0) for low in lows]
            xs = [off + dot1(e, off) for e, off in zip(es, offs)]
            es = [e - (x + dot1(x, e)) for e, x in zip(es, xs)]
        egcs = [jnp.exp(gcb) for gcb in gcbs]
        rhss = [jnp.concatenate([vn[r, :] * beta, kb * egc], axis=1)
                for r, beta, kb, egc in zip(rows, betas, kbs, egcs)]
        uws = [rhs + _dot3(e, rhs) for e, rhs in zip(es, rhss)]
        qs = [qn[r, :] for r in rows]
        a_qks = [lax.dot_general(q.astype(bf16), kbf, nt_dims, preferred_element_type=jnp.float32) * dec
                 for q, kbf, dec in zip(qs, kbfs, decays)]
        g_lasts = [gcb[c - 1:c, :] for gcb in gcbs]
        kdec_ts = []
        for k, gcb, g_last in zip(ks, gcbs, g_lasts):
            kdec = k * jnp.exp(g_last - gcb)
            kdec_ts.append(jnp.concatenate([kdec, jnp.zeros_like(kdec)], axis=0).T[:, 0:c].astype(bf16))
        uwbs = [uw.astype(bf16) for uw in uws]
        nms = [jnp.dot(kt, uwb, preferred_element_type=jnp.float32) for kt, uwb in zip(kdec_ts, uwbs)]
        oqs = [jnp.dot(a.astype(bf16), uwb, preferred_element_type=jnp.float32) for a, uwb in zip(a_qks, uwbs)]
        for i, r, nm, oq, q, egc, g_last in zip(ids, rows, nms, oqs, qs, egcs, g_lasts):
            m0 = pl.ds(pl.multiple_of(i * d, d), d)
            nn[m0, :] = nm[:, 0:d]
            mm[m0, :] = nm[:, d:2 * d].astype(bf16)
            oo[r, :] = oq[:, 0:d]
            qq[r, :] = (q * egc - oq[:, d:2 * d]).astype(bf16)
            dd[pl.ds(pl.multiple_of(i * 8, 8), 8), :] = jnp.broadcast_to(jnp.exp(g_last), (8, d))
        return carry

    lax.fori_loop(0, nb * n_chunks // DN_PREP_UNROLL, chunk_prep, 0)

    def chunk_scan(i, states):
        ids = [b * n_chunks + i for b in range(nb)]
        blocks = [pl.ds(pl.multiple_of(j * d, d), d) for j in ids]
        sbs = [s.astype(bf16) for s in states]
        for blk, sb in zip(blocks, sbs):
            ss[blk, :] = sb
        prods = [jnp.dot(mm[blk, :], sb, preferred_element_type=jnp.float32) for blk, sb in zip(blocks, sbs)]
        return tuple(s * dd[pl.ds(pl.multiple_of(j * 8, 8), 1), :] - pr + nn[blk, :]
                     for s, j, blk, pr in zip(states, ids, blocks, prods))

    lax.fori_loop(0, n_chunks, chunk_scan, tuple(jnp.zeros((d, d), jnp.float32) for _ in range(nb)))

    nw = nw_ref[...]

    def chunk_out(it, carry):
        ids = [it * DN_OUT_UNROLL + cc for cc in range(DN_OUT_UNROLL)]
        rows = [pl.ds(pl.multiple_of(i * c, c), c) for i in ids]
        outs = [jnp.dot(qq[r, :], ss[pl.ds(pl.multiple_of(i * d, d), d), :], preferred_element_type=jnp.float32)
                + oo[r, :] for i, r in zip(ids, rows)]
        for r, o in zip(rows, outs):
            o = o * lax.rsqrt(jnp.mean(o * o, axis=-1, keepdims=True) + 1e-6) * nw
            z = z_ref[r, :]
            o_ref[r, :] = (o * (z * jax.nn.sigmoid(z))).astype(o_ref.dtype)
        return carry

    lax.fori_loop(0, nb * n_chunks // DN_OUT_UNROLL, chunk_out, 0)


def _deltanet(pf, conv_w, a_log, dt_bias, norm_w, bsz, seq):
    d = DN_HEAD_DIM
    nb = DN_BATCHES if bsz % DN_BATCHES == 0 else 1
    rows = nb * seq
    n_chunks = rows // DN_CHUNK
    qkv0 = PF_QKVB // d
    scal = jnp.stack([a_log, dt_bias]).astype(jnp.float32)
    f32 = jnp.float32
    bf16 = jnp.bfloat16
    cw4 = conv_w.astype(f32).reshape(DN_CONV, 3, DN_HEADS, d).transpose(2, 1, 0, 3)
    assert 2 * DN_CHUNK == d and n_chunks % DN_PREP_UNROLL == 0 and n_chunks % DN_OUT_UNROLL == 0
    return pl.pallas_call(
        functools.partial(_dn_kernel, seq=seq),
        grid=(bsz // nb, DN_HEADS),
        in_specs=[
            pl.BlockSpec(memory_space=pltpu.SMEM),
            pl.BlockSpec((rows, d), lambda b, h: (b, qkv0 + h)),
            pl.BlockSpec((rows, d), lambda b, h: (b, qkv0 + DN_HEADS + h)),
            pl.BlockSpec((rows, d), lambda b, h: (b, qkv0 + 2 * DN_HEADS + h)),
            pl.BlockSpec((rows, LANES), lambda b, h: (b, PF_SMALL // LANES)),
            pl.BlockSpec((rows, d), lambda b, h: (b, PF_ZB // d + h)),
            pl.BlockSpec((None, 3, DN_CONV, d), lambda b, h: (h, 0, 0, 0)),
            pl.BlockSpec((1, d), lambda b, h: (0, 0)),
        ],
        out_specs=pl.BlockSpec((rows, d), lambda b, h: (b, h)),
        out_shape=jax.ShapeDtypeStruct((bsz * seq, DN_WIDTH), jnp.bfloat16),
        scratch_shapes=[
            pltpu.VMEM((rows, d), f32), pltpu.VMEM((rows, d), f32), pltpu.VMEM((rows, d), f32),
            pltpu.VMEM((rows, d), f32), pltpu.VMEM((rows, d), f32),
            pltpu.VMEM((n_chunks * d, d), bf16), pltpu.VMEM((n_chunks * d, d), f32),
            pltpu.VMEM((rows, d), bf16), pltpu.VMEM((rows, d), f32),
            pltpu.VMEM((n_chunks * 8, d), f32),
            pltpu.VMEM((n_chunks * d, d), bf16),
        ],
        compiler_params=_cparams(2), name="deltanet",
    )(scal, pf, pf, pf, pf, pf, cw4, norm_w.astype(f32).reshape(1, d))


def _out_kernel(oa_ref, ob_ref, gma_ref, gmb_ref, x_ref, p_ref, wa_ref, wb_ref, wo_ref, wpg_ref, wp_ref,
                lng_ref, lnb_ref, o_ref):
    f32 = jnp.float32
    y_a = jnp.dot(oa_ref[...], wa_ref[...], preferred_element_type=f32)
    y_b = jnp.dot(ob_ref[...], wb_ref[...], preferred_element_type=f32)
    mix = jax.nn.sigmoid(gma_ref[...]) * y_a + jax.nn.sigmoid(gmb_ref[...]) * y_b
    mixed = jnp.dot(mix.astype(jnp.bfloat16), wo_ref[...], preferred_element_type=f32)
    h = DEEPNORM_ALPHA * x_ref[...] + mixed
    gate = jax.nn.sigmoid(jnp.dot(h.astype(jnp.bfloat16), wpg_ref[...], preferred_element_type=f32))
    h = h + gate * jnp.dot(p_ref[...].astype(jnp.bfloat16), wp_ref[...], preferred_element_type=f32)
    mu = jnp.mean(h, axis=-1, keepdims=True)
    hc = h - mu
    var = jnp.mean(hc * hc, axis=-1, keepdims=True)
    o_ref[...] = (hc * lax.rsqrt(var + 1e-5) * lng_ref[...] + lnb_ref[...]).astype(o_ref.dtype)


def _out_block(o_a, o_b, pf, x2, p2, wa, wb, wo, wpg, wp, ln_g, ln_b, tm):
    t = x2.shape[0]
    bf = jnp.bfloat16

    def full(shape):
        return pl.BlockSpec(shape, lambda i: (0, 0))

    return pl.pallas_call(
        _out_kernel,
        grid=(t // tm,),
        in_specs=[
            pl.BlockSpec((tm, NSA_WIDTH), lambda i: (i, 0)),
            pl.BlockSpec((tm, DN_WIDTH), lambda i: (i, 0)),
            pl.BlockSpec((tm, D_MODEL), lambda i: (i, PF_GM // D_MODEL)),
            pl.BlockSpec((tm, D_MODEL), lambda i: (i, PF_GM // D_MODEL + 1)),
            pl.BlockSpec((tm, D_MODEL), lambda i: (i, 0)),
            pl.BlockSpec((tm, PLE_DIM), lambda i: (i, 0)),
            full((NSA_WIDTH, D_MODEL)), full((DN_WIDTH, D_MODEL)), full((D_MODEL, D_MODEL)),
            full((D_MODEL, D_MODEL)), full((PLE_DIM, D_MODEL)), full((1, D_MODEL)), full((1, D_MODEL)),
        ],
        out_specs=pl.BlockSpec((tm, D_MODEL), lambda i: (i, 0)),
        out_shape=jax.ShapeDtypeStruct((t, D_MODEL), x2.dtype),
        compiler_params=_cparams(1), name="out_block",
    )(o_a, o_b, pf, pf, x2, p2, wa.astype(bf), wb.astype(bf), wo.astype(bf), wpg.astype(bf), wp.astype(bf),
      ln_g.astype(jnp.float32).reshape(1, D_MODEL), ln_b.astype(jnp.float32).reshape(1, D_MODEL))


def _layer(x, p, w_in, pos_k, pos_v, w1_k, w2_k, w1_v, w2_v, bias_tabs, conv_w, a_log, dt_bias, norm_w,
           w_a, w_b, w_o, w_ple, w_pg, ln_g, ln_b):
    bsz, seq, _ = x.shape
    t = bsz * seq
    x2 = x.reshape(t, D_MODEL)
    wb16, wf16 = _prep_w_in(w_in)
    pb, pf = _proj(x2, wb16, wf16, PROJ_TM if t % PROJ_TM == 0 else seq)

    pos2, w2p = _prep_compress_weights(pos_k, pos_v, w2_k, w2_v)
    kcvc = _compress(pf, pos2, w1_k, w1_v, w2p, bsz, seq)

    bias_sel, bias_win, bias_cmp = bias_tabs
    o_a = _nsa(pb, pf, kcvc, bias_sel, bias_win, bias_cmp, bsz, seq)
    o_b = _deltanet(pf, conv_w, a_log, dt_bias, norm_w, bsz, seq)
    out = _out_block(o_a, o_b, pf, x2, p.reshape(t, PLE_DIM), w_a, w_b, w_o, w_pg, w_ple, ln_g, ln_b,
                     OUT_TM if t % OUT_TM == 0 else seq)
    return out.reshape(bsz, seq, D_MODEL)


def kernel(x, p, w_in, cmp_pos_k, cmp_pos_v, cmp_w1_k, cmp_w2_k, cmp_w1_v, cmp_w2_v, rel_bias, dn_conv_w,
           dn_a_log, dn_dt_bias, dn_norm_w, w_branch_a, w_branch_b, w_out, w_ple, w_ple_gate, ln_g, ln_b):
    depth = w_in.shape[0]
    bias_tabs = _bias_tables(rel_bias, x.shape[1])
    for i in range(depth):
        x = _layer(x, p[i], w_in[i], cmp_pos_k[i], cmp_pos_v[i], cmp_w1_k[i], cmp_w2_k[i], cmp_w1_v[i],
                   cmp_w2_v[i], bias_tabs, dn_conv_w[i], dn_a_log[i], dn_dt_bias[i], dn_norm_w[i],
                   w_branch_a[i], w_branch_b[i], w_out[i], w_ple[i], w_ple_gate[i], ln_g[i], ln_b[i])
    return x
```

```python
import functools
import math

import numpy as np
import jax
import jax.numpy as jnp
from jax import lax
from jax.experimental import pallas as pl
from jax.experimental.pallas import tpu as pltpu

D_MODEL = 1024
PLE_DIM = 256
NSA_HEADS = 8
NSA_GROUPS = 2
NSA_HPG = NSA_HEADS // NSA_GROUPS
NSA_HEAD_DIM = 64
NSA_WIDTH = NSA_HEADS * NSA_HEAD_DIM
NSA_KV = NSA_GROUPS * NSA_HEAD_DIM
CMP_BLOCK = 32
CMP_STRIDE = 16
CMP_HIDDEN = 256
SEL_BLOCK = 64
SEL_TOPK = 8
WINDOW = 512
DN_HEADS = 4
DN_HEAD_DIM = 128
DN_WIDTH = DN_HEADS * DN_HEAD_DIM
DN_CONV = 4
DN_CHUNK = 64
NUM_BUCKETS = 32
REL_MAX_DIST = 1024
DEEPNORM_ALPHA = 2.0 ** 0.25
NEG = -1e30
FORCE = 1e6
LOG2E = 1.4426950408889634

LANES = 128
TILE = 128
SEL_LANE0 = 64
SEL_GROUP = 4
NSA_TILES_PER_STEP = 2
VMEM_LIMIT = 56 * 1024 * 1024
PROJ_TM = 512
PROJ_TN = 1024
OUT_TM = 512
DN_INV_BLOCK = 16
DN_BATCHES = 2
DN_OUT_UNROLL = 8
DN_DECAY_PARTS = 2
DN_PREP_UNROLL = 16

HIGHEST = lax.Precision.HIGHEST

PB_QP = 0
PB_KS = PB_QP + NSA_WIDTH
PB_KW = PB_KS + NSA_KV
PB_VS = PB_KW + NSA_KV
PB_VW = PB_VS + NSA_KV
PB_WIDTH = PB_VW + NSA_KV
PF_GM = 0
PF_QKVB = PF_GM + 2 * D_MODEL
PF_ZA = PF_QKVB + 3 * DN_WIDTH
PF_ZB = PF_ZA + NSA_WIDTH
PF_KC = PF_ZB + DN_WIDTH
PF_VC = PF_KC + NSA_KV
PF_SMALL = PF_VC + NSA_KV
PF_WIDTH = PF_SMALL + LANES
SMALL_BETA = 3 * NSA_HEADS
SMALL_A = SMALL_BETA + DN_HEADS


def _bucket_thresholds():
    max_exact = NUM_BUCKETS // 2
    span = NUM_BUCKETS - max_exact
    ratio = REL_MAX_DIST // max_exact
    thr = list(range(1, max_exact + 1))
    for k in range(1, span):
        n = max_exact
        while n ** span < max_exact ** span * ratio ** k:
            n += 1
        thr.append(n)
    return tuple(thr)


_THR = _bucket_thresholds()


def _cparams(n_axes):
    return pltpu.CompilerParams(dimension_semantics=("arbitrary",) * n_axes, vmem_limit_bytes=VMEM_LIMIT)


def _proj_kernel(x_ref, wb_ref, wf_ref, pb_ref, pf_ref):
    xb = x_ref[...].astype(jnp.bfloat16)
    pb_ref[...] = jnp.dot(xb, wb_ref[...], preferred_element_type=jnp.float32).astype(pb_ref.dtype)
    n = pf_ref.shape[1]
    for c0 in range(0, n, PROJ_TN):
        c1 = min(c0 + PROJ_TN, n)
        pf_ref[:, c0:c1] = jnp.dot(xb, wf_ref[:, c0:c1], preferred_element_type=jnp.float32)


def _proj(x2, wb, wf, tm):
    t, d = x2.shape
    resident = dict(pipeline_mode=pl.Buffered(1))
    return pl.pallas_call(
        _proj_kernel,
        grid=(t // tm,),
        in_specs=[pl.BlockSpec((tm, d), lambda i: (i, 0)),
                  pl.BlockSpec((d, PB_WIDTH), lambda i: (0, 0), **resident),
                  pl.BlockSpec((d, PF_WIDTH), lambda i: (0, 0), **resident)],
        out_specs=[pl.BlockSpec((tm, PB_WIDTH), lambda i: (i, 0)),
                   pl.BlockSpec((tm, PF_WIDTH), lambda i: (i, 0))],
        out_shape=[jax.ShapeDtypeStruct((t, PB_WIDTH), jnp.bfloat16),
                   jax.ShapeDtypeStruct((t, PF_WIDTH), jnp.float32)],
        compiler_params=_cparams(1), name="proj",
    )(x2, wb, wf)


def _prep_w_in(w):
    d = w.shape[0]
    o = 0
    wq = w[:, o:o + NSA_WIDTH]; o += NSA_WIDTH
    wkv = w[:, o:o + 6 * NSA_KV]; o += 6 * NSA_KV
    wg = w[:, o:o + 3 * NSA_HEADS]; o += 3 * NSA_HEADS
    wza = w[:, o:o + NSA_WIDTH]; o += NSA_WIDTH
    wqkvb = w[:, o:o + 3 * DN_WIDTH]; o += 3 * DN_WIDTH
    wbeta_a = w[:, o:o + 2 * DN_HEADS]; o += 2 * DN_HEADS
    wzb = w[:, o:o + DN_WIDTH]; o += DN_WIDTH
    wgm = w[:, o:o + 2 * D_MODEL]
    wkcvc, wks, wvs, wkw, wvw = (wkv[:, 0:2 * NSA_KV], wkv[:, 2 * NSA_KV:3 * NSA_KV], wkv[:, 3 * NSA_KV:4 * NSA_KV],
                                 wkv[:, 4 * NSA_KV:5 * NSA_KV], wkv[:, 5 * NSA_KV:6 * NSA_KV])
    wb = jnp.concatenate([wq * (NSA_HEAD_DIM ** -0.5 * LOG2E), wks, wkw, wvs, wvw], axis=1).astype(jnp.bfloat16)
    pad = jnp.zeros((d, LANES - 3 * NSA_HEADS - 2 * DN_HEADS), w.dtype)
    wf = jnp.concatenate([wgm, wqkvb, wza, wzb, wkcvc, wg, wbeta_a, pad], axis=1).astype(jnp.bfloat16)
    return wb, wf


def _bias_kernel(tab_ref, sel_ref, win_ref, cmp_ref, *, n_cmp):
    h = pl.program_id(0)

    def lookup(n):
        val = jnp.full(n.shape, tab_ref[0, h], jnp.float32)
        for b in range(1, NUM_BUCKETS):
            val = jnp.where(n >= _THR[b - 1], tab_ref[b, h], val)
        return val * LOG2E

    kj = lax.broadcasted_iota(jnp.int32, (TILE, TILE), 0)
    qi = lax.broadcasted_iota(jnp.int32, (TILE, TILE), 1)
    n_sel_tiles = sel_ref.shape[1] - 1
    n_win_tiles = win_ref.shape[1] - 1
    for dt in range(max(n_sel_tiles, n_win_tiles)):
        dist = dt * TILE + qi - kj
        v = lookup(jnp.maximum(dist, 0))
        if dt < n_sel_tiles:
            sel_ref[0, dt] = jnp.where(dist >= 0, v, NEG)
        if dt < n_win_tiles:
            win_ref[0, dt] = jnp.where((dist >= 0) & (dist < WINDOW), v, NEG)
    sel_ref[0, n_sel_tiles] = jnp.full((TILE, TILE), NEG, jnp.float32)
    win_ref[0, n_win_tiles] = jnp.full((TILE, TILE), NEG, jnp.float32)
    c = lax.broadcasted_iota(jnp.int32, cmp_ref.shape[2:], 0)
    for t in range(cmp_ref.shape[1]):
        s = t * TILE + lax.broadcasted_iota(jnp.int32, cmp_ref.shape[2:], 1)
        dist = s - (c * CMP_STRIDE + CMP_BLOCK - 1)
        cmp_ref[0, t] = jnp.where((dist >= 0) & (c < n_cmp), lookup(jnp.maximum(dist, 0)), NEG)


def _bias_tables(rel_bias, seq):
    nq = seq // TILE
    nwin = WINDOW // TILE + 1
    nc = seq // CMP_STRIDE
    n_cmp = nc - CMP_BLOCK // CMP_STRIDE + 1
    return pl.pallas_call(
        functools.partial(_bias_kernel, n_cmp=n_cmp),
        grid=(NSA_HEADS,),
        in_specs=[pl.BlockSpec(memory_space=pltpu.SMEM)],
        out_specs=[pl.BlockSpec((1, nq + 1, TILE, TILE), lambda h: (h, 0, 0, 0)),
                   pl.BlockSpec((1, nwin + 1, TILE, TILE), lambda h: (h, 0, 0, 0)),
                   pl.BlockSpec((1, nq, nc, TILE), lambda h: (h, 0, 0, 0))],
        out_shape=[jax.ShapeDtypeStruct((NSA_HEADS, nq + 1, TILE, TILE), jnp.float32),
                   jax.ShapeDtypeStruct((NSA_HEADS, nwin + 1, TILE, TILE), jnp.float32),
                   jax.ShapeDtypeStruct((NSA_HEADS, nq, nc, TILE), jnp.float32)],
        compiler_params=_cparams(1), name="bias_tables",
    )(rel_bias.astype(jnp.float32))


def _gelu_tanh(x):
    return x * (0.5 * (1.0 + jnp.tanh(math.sqrt(2.0 / math.pi) * (x + 0.044715 * (x * x * x)))))


def _compress_kernel(x_ref, pos_ref, w1k_ref, w1v_ref, w2_ref, o_ref, wbd):
    kv = pl.program_id(0)
    nc = o_ref.shape[2]
    half = CMP_BLOCK // 2
    dh = NSA_HEAD_DIM
    bf16 = jnp.bfloat16

    @pl.when(pl.program_id(1) == 0)
    def _():
        def build(w1_ref):
            zero = jnp.zeros((dh, CMP_HIDDEN), bf16)
            for a in range(2):
                for l in range(half):
                    r0 = (a * half + l) * dh
                    wl = w1_ref[r0:r0 + dh, :].astype(bf16)
                    wbd[a, l * NSA_KV:(l + 1) * NSA_KV, :] = jnp.concatenate(
                        [jnp.concatenate([wl, zero], axis=1), jnp.concatenate([zero, wl], axis=1)], axis=0)
        pl.when(kv == 0)(functools.partial(build, w1k_ref))
        pl.when(kv == 1)(functools.partial(build, w1v_ref))

    r = jnp.concatenate([x_ref[pl.ds(l, nc, stride=CMP_STRIDE), :] for l in range(CMP_STRIDE)], axis=1)
    a = jnp.dot((r + pos_ref[0, 0:1, :]).astype(bf16), wbd[0], preferred_element_type=jnp.float32)
    b = jnp.dot((r + pos_ref[0, 1:2, :]).astype(bf16), wbd[1], preferred_element_type=jnp.float32)
    hid = a + pltpu.roll(b, nc - 1, 0)
    o_ref[0, 0] = jnp.dot(_gelu_tanh(hid).astype(bf16), w2_ref[0], preferred_element_type=jnp.float32)


def _compress(pf, pos2, w1_k, w1_v, w2p, bsz, seq):
    nc = seq // CMP_STRIDE
    width = CMP_STRIDE * NSA_KV
    hid = NSA_GROUPS * CMP_HIDDEN
    assert CMP_BLOCK == 2 * CMP_STRIDE
    return pl.pallas_call(
        _compress_kernel,
        grid=(2, bsz),
        in_specs=[pl.BlockSpec((seq, NSA_KV), lambda k, b: (b, PF_KC // NSA_KV + k)),
                  pl.BlockSpec((1, 2, width), lambda k, b: (k, 0, 0)),
                  pl.BlockSpec(w1_k.shape, lambda k, b: (0, 0)),
                  pl.BlockSpec(w1_v.shape, lambda k, b: (0, 0)),
                  pl.BlockSpec((1, hid, LANES), lambda k, b: (k, 0, 0))],
        out_specs=pl.BlockSpec((1, 1, nc, LANES), lambda k, b: (b, k, 0, 0)),
        out_shape=jax.ShapeDtypeStruct((bsz, 2, nc, LANES), jnp.float32),
        scratch_shapes=[pltpu.VMEM((2, width, hid), jnp.bfloat16)],
        compiler_params=_cparams(2), name="compress",
    )(pf, pos2, w1_k, w1_v, w2p)


def _prep_compress_weights(pos_k, pos_v, w2_k, w2_v):
    eye = jnp.eye(NSA_GROUPS, dtype=jnp.float32)
    half = CMP_BLOCK // 2

    def w2_both(w2):
        return jnp.einsum('jd,gh->gjhd', w2, eye).reshape(NSA_GROUPS * CMP_HIDDEN, NSA_GROUPS * NSA_HEAD_DIM)

    def pos_both(pos):
        p = pos.reshape(2, half, 1, NSA_HEAD_DIM)
        return jnp.broadcast_to(p, (2, half, NSA_GROUPS, NSA_HEAD_DIM)).reshape(2, half * NSA_KV)

    pos2 = jnp.stack([pos_both(pos_k), pos_both(pos_v)]).astype(jnp.float32)
    w2p = jnp.stack([w2_both(w2_k), w2_both(w2_v)]).astype(jnp.bfloat16)
    return pos2, w2p


def _pair_lanes(x, g):
    sw = pltpu.roll(x, LANES // 2, 1)
    lane = lax.broadcasted_iota(jnp.int32, x.shape, 1)
    own = (lane < LANES // 2) == (g == 0)
    return jnp.where(own, x, sw)


def _group_rows_t(x, g):
    xt = x.T
    half = LANES // 2
    return jnp.where(g == 0, xt[0:half, :], xt[half:LANES, :])


def _nsa_step_kernel(q_ref, qn_ref, ks_ref, kw_ref, vs_ref, vw_ref, kcvc_ref, bc_ref, bsel_ref, bwin_ref,
                     small_ref, za_ref, o_ref,
                     ksa, kwp, vst, vwt, kcs, vct, qa_s, qp_s, ocmp_s, sbuf, *, n_cmp, n_sel, n_top):
    g = pl.program_id(0)
    step_id = pl.program_id(2)
    tq = TILE
    seq = ks_ref.shape[0]
    nq = seq // TILE
    nc = kcvc_ref.shape[2]
    n_wt = bwin_ref.shape[1] - 1
    heads = range(NSA_HPG)
    f32 = jnp.float32
    bf16 = jnp.bfloat16
    nt_dims = (((1,), (1,)), ((), ()))
    gk = SEL_GROUP * TILE

    def select_tile(src_ref, r0, tile, slot):
        q32 = src_ref[r0:r0 + tq, :].astype(f32)
        low_half = lax.broadcasted_iota(jnp.int32, (tq, LANES), 1) < NSA_HEAD_DIM
        qh = []
        for hh in heads:
            blk = q32[:, (hh // 2) * LANES:(hh // 2 + 1) * LANES]
            if hh % 2:
                blk = pltpu.roll(blk, LANES // 2, 1)
            qh.append(jnp.where(low_half, blk, 0.0).astype(bf16))
        for hh in heads:
            qp_s[slot, hh * tq:(hh + 1) * tq, :] = qh[hh]
        q_all = qp_s[slot]
        s_c = lax.dot_general(kcs[...], q_all, nt_dims, preferred_element_type=f32)
        bias_c = jnp.concatenate([bc_ref[hh, tile] for hh in heads], axis=1)
        valid = bias_c > 0.5 * NEG
        s_c = s_c + bias_c
        e = jnp.where(valid, jnp.exp2(s_c - jnp.max(s_c, axis=0, keepdims=True)), 0.0)
        den = jnp.maximum(jnp.sum(e, axis=0, keepdims=True), 1e-30)
        p_c = e * (1.0 / den)
        psum = sum(p_c[:, hh * tq:(hh + 1) * tq] for hh in heads)
        sj = lax.broadcasted_iota(jnp.int32, (n_sel, nc), 0)
        ci = lax.broadcasted_iota(jnp.int32, (n_sel, nc), 1)
        overlap = ((ci * CMP_STRIDE < (sj + 1) * SEL_BLOCK) & (ci * CMP_STRIDE + CMP_BLOCK > sj * SEL_BLOCK)
                   & (ci < n_cmp)).astype(bf16)
        imp_t = sum(jnp.dot(overlap, part, preferred_element_type=f32)
                    for part in _split_bf16(psum, 3))
        ocmp_s[slot] = jnp.dot(vct[...], p_c.astype(bf16), preferred_element_type=f32)
        blk = lax.broadcasted_iota(jnp.int32, (n_sel, tq), 0)
        cur = (tile * tq + lax.broadcasted_iota(jnp.int32, (n_sel, tq), 1)) >> 6
        forced = (blk == 0) | (blk == cur) | (blk == cur - 1)
        imp_t = jnp.where(forced, FORCE, jnp.where(blk > cur, -FORCE, imp_t))
        rank = jnp.zeros((n_sel, tq), jnp.int32)
        for j in range(n_sel):
            other = imp_t[j:j + 1, :]
            ahead = (other > imp_t) | ((other == imp_t) & (blk > j))
            rank = rank + ahead.astype(jnp.int32)
        selb = jnp.where(rank < n_top, 0.0, NEG)
        pieces = [jnp.zeros((SEL_LANE0, tq), f32), selb]
        if LANES - SEL_LANE0 - n_sel > 0:
            pieces.append(jnp.zeros((LANES - SEL_LANE0 - n_sel, tq), f32))
        selb_r = jnp.concatenate(pieces, axis=0).T.astype(bf16)
        for hh in heads:
            qa_s[slot, hh * tq:(hh + 1) * tq, :] = qh[hh] + selb_r

    @pl.when(step_id == 0)
    def _():
        row = lax.broadcasted_iota(jnp.int32, (seq, LANES), 0)
        lane = lax.broadcasted_iota(jnp.int32, (seq, LANES), 1)
        onehot = (lane - SEL_LANE0) == (row >> 6)
        own = lane < NSA_HEAD_DIM
        ks_g = jnp.where(own, _pair_lanes(ks_ref[...].astype(f32), g), 0.0)
        ksa[...] = jnp.where(onehot, 1.0, ks_g).astype(bf16)
        pad = (n_wt - 1) * TILE
        kwp[0:pad, :] = jnp.zeros((pad, LANES), bf16)
        kwp[pad:pad + seq, :] = jnp.where(own, _pair_lanes(kw_ref[...].astype(f32), g), 0.0).astype(bf16)
        for kt in range(n_wt - 1):
            vwt[kt] = jnp.zeros(vwt.shape[1:], bf16)
        for kt in range(nq):
            rows = slice(kt * TILE, (kt + 1) * TILE)
            vst[kt] = _group_rows_t(vs_ref[rows, :].astype(f32), g).astype(bf16)
            vwt[kt + n_wt - 1] = _group_rows_t(vw_ref[rows, :].astype(f32), g).astype(bf16)
        kcs[...] = _pair_lanes(kcvc_ref[0, 0], g).astype(bf16)
        for ct in range(nc // TILE):
            rows = slice(ct * TILE, (ct + 1) * TILE)
            vct[:, rows] = _group_rows_t(kcvc_ref[0, 1, rows, :], g).astype(bf16)
        select_tile(q_ref, 0, 0, 0)

    tps = NSA_TILES_PER_STEP

    def one_tile(n_groups, slot):
        qt = step_id * tps + slot
        rows = slice(slot * tq, (slot + 1) * tq)
        q_all = qp_s[slot]
        q_aug = qa_s[slot]
        kw_rows = kwp[pl.ds(pl.multiple_of(qt * TILE, TILE), n_wt * TILE), :]
        s_w = lax.dot_general(kw_rows, q_all, nt_dims, preferred_element_type=f32)

        def sel_scores(gi):
            s = lax.dot_general(ksa[gi * gk:(gi + 1) * gk, :], q_aug, nt_dims, preferred_element_type=f32)
            tiles = []
            for t in range(SEL_GROUP):
                dt = qt - (gi * SEL_GROUP + t)
                idx = jnp.where(dt >= 0, dt, nq)
                tiles.append(jnp.concatenate([bsel_ref[hh, idx] for hh in heads], axis=1))
            sbuf[gi % 2] = s + jnp.concatenate(tiles, axis=0)

        sel_scores(0)
        if slot + 1 < tps:
            select_tile(q_ref, (slot + 1) * tq, qt + 1, slot + 1)
        else:
            select_tile(qn_ref, 0, jnp.minimum(qt + 1, nq - 1), 0)

        w_tiles = []
        for t in range(n_wt):
            dt = n_wt - 1 - t
            idx = jnp.where(qt >= dt, dt, n_wt)
            w_tiles.append(jnp.concatenate([bwin_ref[hh, idx] for hh in heads], axis=1))
        s_w = s_w + jnp.concatenate(w_tiles, axis=0)
        p_w = jnp.exp2(s_w - jnp.max(s_w, axis=0, keepdims=True))
        v_w = jnp.concatenate([vwt[qt + t] for t in range(n_wt)], axis=1)
        o_win = (jnp.dot(v_w, p_w.astype(bf16), preferred_element_type=f32)
                 * (1.0 / jnp.sum(p_w, axis=0, keepdims=True)))

        m = jnp.full((1, NSA_HPG * tq), NEG, f32)
        l = jnp.zeros((1, NSA_HPG * tq), f32)
        acc = jnp.zeros((NSA_HEAD_DIM, NSA_HPG * tq), f32)
        for gi in range(n_groups):
            if gi + 1 < n_groups:
                sel_scores(gi + 1)
            s = sbuf[gi % 2]
            m_new = jnp.maximum(m, jnp.max(s, axis=0, keepdims=True))
            alpha = jnp.exp2(m - m_new)
            p = jnp.exp2(s - m_new)
            v_t = jnp.concatenate([vst[gi * SEL_GROUP + t] for t in range(SEL_GROUP)], axis=1)
            acc = alpha * acc + jnp.dot(v_t, p.astype(bf16), preferred_element_type=f32)
            l = alpha * l + jnp.sum(p, axis=0, keepdims=True)
            m = m_new
        o_sel = acc * (1.0 / l)
        o_cmp = ocmp_s[slot]

        gates_t = jax.nn.sigmoid(small_ref[rows, :]).T
        mixed = []
        for hh in heads:
            def gate(br):
                r = br * NSA_HEADS + hh
                return jnp.where(g == 0, gates_t[r:r + 1, :], gates_t[r + NSA_HPG:r + NSA_HPG + 1, :])
            cols = slice(hh * tq, (hh + 1) * tq)
            mixed.append(gate(0) * o_cmp[:, cols] + gate(1) * o_sel[:, cols] + gate(2) * o_win[:, cols])
        o = jnp.concatenate([jnp.concatenate(mixed[2 * j:2 * j + 2], axis=0).T for j in range(NSA_HPG // 2)],
                            axis=1)
        z = za_ref[rows, :]
        o_ref[rows, :] = (o * (z * jax.nn.sigmoid(z))).astype(o_ref.dtype)

    def step(n_groups):
        for slot in range(tps):
            one_tile(n_groups, slot)

    for n_groups in range(1, nq // SEL_GROUP + 1):
        pl.when(step_id // (SEL_GROUP // tps) == n_groups - 1)(functools.partial(step, n_groups))


def _nsa(pb, pf, kcvc, bias_sel, bias_win, bias_cmp, bsz, seq):
    nq = seq // TILE
    nc = seq // CMP_STRIDE
    n_cmp = nc - CMP_BLOCK // CMP_STRIDE + 1
    n_sel = seq // SEL_BLOCK
    n_top = min(SEL_TOPK, n_sel)
    nwin = bias_win.shape[1]
    tps = NSA_TILES_PER_STEP
    assert nq % SEL_GROUP == 0 and nc % TILE == 0 and bias_sel.shape[1] == nq + 1
    assert tps >= 2 and SEL_GROUP % tps == 0
    ns = nq // tps
    rows = tps * TILE
    gw = NSA_HPG * NSA_HEAD_DIM
    kern = functools.partial(_nsa_step_kernel, n_cmp=n_cmp, n_sel=n_sel, n_top=n_top)
    return pl.pallas_call(
        kern,
        grid=(NSA_GROUPS, bsz, ns),
        in_specs=[
            pl.BlockSpec((rows, gw), lambda g, b, t: (b * ns + t, PB_QP // gw + g)),
            pl.BlockSpec((TILE, gw), lambda g, b, t: (b * nq + jnp.minimum((t + 1) * tps, nq - 1),
                                                      PB_QP // gw + g)),
            pl.BlockSpec((seq, LANES), lambda g, b, t: (b, PB_KS // LANES)),
            pl.BlockSpec((seq, LANES), lambda g, b, t: (b, PB_KW // LANES)),
            pl.BlockSpec((seq, LANES), lambda g, b, t: (b, PB_VS // LANES)),
            pl.BlockSpec((seq, LANES), lambda g, b, t: (b, PB_VW // LANES)),
            pl.BlockSpec((1, 2, nc, LANES), lambda g, b, t: (b, 0, 0, 0)),
            pl.BlockSpec((NSA_HPG, nq, nc, TILE), lambda g, b, t: (g, 0, 0, 0)),
            pl.BlockSpec((NSA_HPG, nq + 1, TILE, TILE), lambda g, b, t: (g, 0, 0, 0)),
            pl.BlockSpec((NSA_HPG, nwin, TILE, TILE), lambda g, b, t: (g, 0, 0, 0)),
            pl.BlockSpec((rows, LANES), lambda g, b, t: (b * ns + t, PF_SMALL // LANES)),
            pl.BlockSpec((rows, gw), lambda g, b, t: (b * ns + t, PF_ZA // gw + g)),
        ],
        out_specs=pl.BlockSpec((rows, gw), lambda g, b, t: (b * ns + t, g)),
        out_shape=jax.ShapeDtypeStruct((bsz * seq, NSA_WIDTH), jnp.bfloat16),
        scratch_shapes=[
            pltpu.VMEM((seq, LANES), jnp.bfloat16),
            pltpu.VMEM((seq + (nwin - 2) * TILE, LANES), jnp.bfloat16),
            pltpu.VMEM((nq, NSA_HEAD_DIM, TILE), jnp.bfloat16),
            pltpu.VMEM((nq + nwin - 2, NSA_HEAD_DIM, TILE), jnp.bfloat16),
            pltpu.VMEM((nc, LANES), jnp.bfloat16),
            pltpu.VMEM((NSA_HEAD_DIM, nc), jnp.bfloat16),
            pltpu.VMEM((tps, NSA_HPG * TILE, LANES), jnp.bfloat16),
            pltpu.VMEM((tps, NSA_HPG * TILE, LANES), jnp.bfloat16),
            pltpu.VMEM((tps, NSA_HEAD_DIM, NSA_HPG * TILE), jnp.float32),
            pltpu.VMEM((2, SEL_GROUP * TILE, NSA_HPG * TILE), jnp.float32),
        ],
        compiler_params=_cparams(3), name="nsa",
    )(pb, pb, pb, pb, pb, pb, kcvc, bias_cmp, bias_sel, bias_win, pf, pf)


def _split_bf16(a, n):
    parts = []
    for _ in range(n - 1):
        hi = a.astype(jnp.bfloat16)
        parts.append(hi)
        a = a - hi.astype(jnp.float32)
    parts.append(a.astype(jnp.bfloat16))
    return parts


def _dot3(a, b):
    ah, al = _split_bf16(a, 2)
    bh, bl = _split_bf16(b, 2)
    f32 = jnp.float32
    return (jnp.dot(ah, bh, preferred_element_type=f32) + jnp.dot(al, bh, preferred_element_type=f32)
            + jnp.dot(ah, bl, preferred_element_type=f32))


def _softplus(x):
    return jnp.maximum(x, 0.0) + jnp.log(1.0 + jnp.exp(-jnp.abs(x)))


def _dn_kernel(scal_ref, q_ref, k_ref, v_ref, small_ref, z_ref, cw_ref, nw_ref, o_ref,
               qn, kn, vn, bet, gl, mm, nn, qq, oo, dd, ss, *, seq):
    h = pl.program_id(1)
    n_rows = q_ref.shape[0]
    nb = n_rows // seq
    c = DN_CHUNK
    n_chunks = seq // c
    d = DN_HEAD_DIM

    head = 8

    def conv_body(x_ref, which):
        n = n_rows - head
        y = x_ref[pl.ds(head, n), :] * cw_ref[which, DN_CONV - 1:DN_CONV, :]
        for j in range(DN_CONV - 1):
            y = y + x_ref[pl.ds(head - (DN_CONV - 1 - j), n), :] * cw_ref[which, j:j + 1, :]
        return y

    def conv_head(x_ref, which, r0):
        x = x_ref[pl.ds(r0, head), :]
        rowi = lax.broadcasted_iota(jnp.int32, (head, d), 0)
        y = x * cw_ref[which, DN_CONV - 1:DN_CONV, :]
        for j in range(DN_CONV - 1):
            sh = DN_CONV - 1 - j
            y = y + jnp.where(rowi >= sh, pltpu.roll(x, sh, 0), 0.0) * cw_ref[which, j:j + 1, :]
        return y

    def silu(y):
        return y * jax.nn.sigmoid(y)

    def l2n(t):
        return t * lax.rsqrt(jnp.sum(t * t, axis=-1, keepdims=True) + 1e-6)

    finish = (lambda y: l2n(silu(y)) * (d ** -0.5), lambda y: l2n(silu(y)), silu)
    for which, (x_ref, dst) in enumerate(((q_ref, qn), (k_ref, kn), (v_ref, vn))):
        dst[pl.ds(head, n_rows - head), :] = finish[which](conv_body(x_ref, which))
        for b in range(nb):
            dst[pl.ds(b * seq, head), :] = finish[which](conv_head(x_ref, which, b * seq))
    small = small_ref[...]
    lane = lax.broadcasted_iota(jnp.int32, small.shape, 1)
    beta_in = jnp.sum(jnp.where(lane == SMALL_BETA + h, small, 0.0), axis=-1, keepdims=True)
    a_in = jnp.sum(jnp.where(lane == SMALL_A + h, small, 0.0), axis=-1, keepdims=True)
    bet[...] = jnp.broadcast_to(jax.nn.sigmoid(beta_in), (n_rows, d))
    gl[...] = jnp.broadcast_to(-jnp.exp(scal_ref[0, h]) * _softplus(a_in + scal_ref[1, h]), (n_rows, d))

    ri = lax.broadcasted_iota(jnp.int32, (c, c), 0)
    cj = lax.broadcasted_iota(jnp.int32, (c, c), 1)
    incl = ri >= cj
    strict = ri > cj
    tril = incl.astype(jnp.float32)
    eye = (ri == cj).astype(jnp.float32)

    tril_b = tril.astype(jnp.bfloat16)
    bs = DN_INV_BLOCK
    sh = bs.bit_length() - 1
    same_diag = (ri >> sh) == (cj >> sh)
    level_masks = []
    while (1 << sh) < c:
        level_masks.append(((ri >> (sh + 1)) == (cj >> (sh + 1))) & ((ri >> sh) > (cj >> sh)))
        sh += 1

    bf16 = jnp.bfloat16
    nt_dims = (((1,), (1,)), ((), ()))

    def dot1(a, b):
        return jnp.dot(a.astype(bf16), b.astype(bf16), preferred_element_type=jnp.float32)

    def chunk_prep(it, carry):
        ids = [it * DN_PREP_UNROLL + cc for cc in range(DN_PREP_UNROLL)]
        rows = [pl.ds(pl.multiple_of(i * c, c), c) for i in ids]
        ks = [kn[r, :] for r in rows]
        betas = [bet[r, :] for r in rows]
        gcbs = [sum(jnp.dot(tril_b, part, preferred_element_type=jnp.float32)
                    for part in _split_bf16(gl[r, :], DN_DECAY_PARTS)) for r in rows]
        kbs = [k * beta for k, beta in zip(ks, betas)]
        kbfs = [k.astype(bf16) for k in ks]
        a_kks = [lax.dot_general(kb.astype(bf16), kbf, nt_dims, preferred_element_type=jnp.float32)
                 for kb, kbf in zip(kbs, kbfs)]
        decays = []
        for gcb in gcbs:
            gct = jnp.concatenate([gcb, gcb], axis=0).T
            diff = gcb[:, 0:c] - gct[0:c, 0:c]
            decays.append(jnp.where(incl, jnp.exp(jnp.where(incl, diff, 0.0)), 0.0))
        lows = [jnp.where(strict, a * dec, 0.0) for a, dec in zip(a_kks, decays)]
        pws = [jnp.where(same_diag, -low, 0.0) for low in lows]
        es = list(pws)
        for _ in range(max(1, (bs - 1).bit_length()) - 1):
            pws = [dot1(pw, pw) for pw in pws]
            es = [e + pw + dot1(e, pw) for e, pw in zip(es, pws)]
        for below in level_masks:
            offs = [jnp.where(below, low, 0.0) for low in lows]
            xs = [off + dot1(e, off) for e, off in zip(es, offs)]
            es = [e - (x + dot1(x, e)) for e, x in zip(es, xs)]
        egcs = [jnp.exp(gcb) for gcb in gcbs]
        rhss = [jnp.concatenate([vn[r, :] * beta, kb * egc], axis=1)
                for r, beta, kb, egc in zip(rows, betas, kbs, egcs)]
        uws = [rhs + _dot3(e, rhs) for e, rhs in zip(es, rhss)]
        qs = [qn[r, :] for r in rows]
        a_qks = [lax.dot_general(q.astype(bf16), kbf, nt_dims, preferred_element_type=jnp.float32) * dec
                 for q, kbf, dec in zip(qs, kbfs, decays)]
        g_lasts = [gcb[c - 1:c, :] for gcb in gcbs]
        kdec_ts = []
        for k, gcb, g_last in zip(ks, gcbs, g_lasts):
            kdec = k * jnp.exp(g_last - gcb)
            kdec_ts.append(jnp.concatenate([kdec, jnp.zeros_like(kdec)], axis=0).T[:, 0:c].astype(bf16))
        uwbs = [uw.astype(bf16) for uw in uws]
        nms = [jnp.dot(kt, uwb, preferred_element_type=jnp.float32) for kt, uwb in zip(kdec_ts, uwbs)]
        oqs = [jnp.dot(a.astype(bf16), uwb, preferred_element_type=jnp.float32) for a, uwb in zip(a_qks, uwbs)]
        for i, r, nm, oq, q, egc, g_last in zip(ids, rows, nms, oqs, qs, egcs, g_lasts):
            m0 = pl.ds(pl.multiple_of(i * d, d), d)
            nn[m0, :] = nm[:, 0:d]
            mm[m0, :] = nm[:, d:2 * d].astype(bf16)
            oo[r, :] = oq[:, 0:d]
            qq[r, :] = (q * egc - oq[:, d:2 * d]).astype(bf16)
            dd[pl.ds(pl.multiple_of(i * 8, 8), 8), :] = jnp.broadcast_to(jnp.exp(g_last), (8, d))
        return carry

    lax.fori_loop(0, nb * n_chunks // DN_PREP_UNROLL, chunk_prep, 0)

    def chunk_scan(i, states):
        ids = [b * n_chunks + i for b in range(nb)]
        blocks = [pl.ds(pl.multiple_of(j * d, d), d) for j in ids]
        sbs = [s.astype(bf16) for s in states]
        for blk, sb in zip(blocks, sbs):
            ss[blk, :] = sb
        prods = [jnp.dot(mm[blk, :], sb, preferred_element_type=jnp.float32) for blk, sb in zip(blocks, sbs)]
        return tuple(s * dd[pl.ds(pl.multiple_of(j * 8, 8), 1), :] - pr + nn[blk, :]
                     for s, j, blk, pr in zip(states, ids, blocks, prods))

    lax.fori_loop(0, n_chunks, chunk_scan, tuple(jnp.zeros((d, d), jnp.float32) for _ in range(nb)))

    nw = nw_ref[...]

    def chunk_out(it, carry):
        ids = [it * DN_OUT_UNROLL + cc for cc in range(DN_OUT_UNROLL)]
        rows = [pl.ds(pl.multiple_of(i * c, c), c) for i in ids]
        outs = [jnp.dot(qq[r, :], ss[pl.ds(pl.multiple_of(i * d, d), d), :], preferred_element_type=jnp.float32)
                + oo[r, :] for i, r in zip(ids, rows)]
        for r, o in zip(rows, outs):
            o = o * lax.rsqrt(jnp.mean(o * o, axis=-1, keepdims=True) + 1e-6) * nw
            z = z_ref[r, :]
            o_ref[r, :] = (o * (z * jax.nn.sigmoid(z))).astype(o_ref.dtype)
        return carry

    lax.fori_loop(0, nb * n_chunks // DN_OUT_UNROLL, chunk_out, 0)


def _deltanet(pf, conv_w, a_log, dt_bias, norm_w, bsz, seq):
    d = DN_HEAD_DIM
    nb = DN_BATCHES if bsz % DN_BATCHES == 0 else 1
    rows = nb * seq
    n_chunks = rows // DN_CHUNK
    qkv0 = PF_QKVB // d
    scal = jnp.stack([a_log, dt_bias]).astype(jnp.float32)
    f32 = jnp.float32
    bf16 = jnp.bfloat16
    cw4 = conv_w.astype(f32).reshape(DN_CONV, 3, DN_HEADS, d).transpose(2, 1, 0, 3)
    assert 2 * DN_CHUNK == d and n_chunks % DN_PREP_UNROLL == 0 and n_chunks % DN_OUT_UNROLL == 0
    return pl.pallas_call(
        functools.partial(_dn_kernel, seq=seq),
        grid=(bsz // nb, DN_HEADS),
        in_specs=[
            pl.BlockSpec(memory_space=pltpu.SMEM),
            pl.BlockSpec((rows, d), lambda b, h: (b, qkv0 + h)),
            pl.BlockSpec((rows, d), lambda b, h: (b, qkv0 + DN_HEADS + h)),
            pl.BlockSpec((rows, d), lambda b, h: (b, qkv0 + 2 * DN_HEADS + h)),
            pl.BlockSpec((rows, LANES), lambda b, h: (b, PF_SMALL // LANES)),
            pl.BlockSpec((rows, d), lambda b, h: (b, PF_ZB // d + h)),
            pl.BlockSpec((None, 3, DN_CONV, d), lambda b, h: (h, 0, 0, 0)),
            pl.BlockSpec((1, d), lambda b, h: (0, 0)),
        ],
        out_specs=pl.BlockSpec((rows, d), lambda b, h: (b, h)),
        out_shape=jax.ShapeDtypeStruct((bsz * seq, DN_WIDTH), jnp.bfloat16),
        scratch_shapes=[
            pltpu.VMEM((rows, d), f32), pltpu.VMEM((rows, d), f32), pltpu.VMEM((rows, d), f32),
            pltpu.VMEM((rows, d), f32), pltpu.VMEM((rows, d), f32),
            pltpu.VMEM((n_chunks * d, d), bf16), pltpu.VMEM((n_chunks * d, d), f32),
            pltpu.VMEM((rows, d), bf16), pltpu.VMEM((rows, d), f32),
            pltpu.VMEM((n_chunks * 8, d), f32),
            pltpu.VMEM((n_chunks * d, d), bf16),
        ],
        compiler_params=_cparams(2), name="deltanet",
    )(scal, pf, pf, pf, pf, pf, cw4, norm_w.astype(f32).reshape(1, d))


def _out_kernel(oa_ref, ob_ref, gma_ref, gmb_ref, x_ref, p_ref, wa_ref, wb_ref, wo_ref, wpg_ref, wp_ref,
                lng_ref, lnb_ref, o_ref):
    f32 = jnp.float32
    y_a = jnp.dot(oa_ref[...], wa_ref[...], preferred_element_type=f32)
    y_b = jnp.dot(ob_ref[...], wb_ref[...], preferred_element_type=f32)
    mix = jax.nn.sigmoid(gma_ref[...]) * y_a + jax.nn.sigmoid(gmb_ref[...]) * y_b
    mixed = jnp.dot(mix.astype(jnp.bfloat16), wo_ref[...], preferred_element_type=f32)
    h = DEEPNORM_ALPHA * x_ref[...] + mixed
    gate = jax.nn.sigmoid(jnp.dot(h.astype(jnp.bfloat16), wpg_ref[...], preferred_element_type=f32))
    h = h + gate * jnp.dot(p_ref[...].astype(jnp.bfloat16), wp_ref[...], preferred_element_type=f32)
    mu = jnp.mean(h, axis=-1, keepdims=True)
    hc = h - mu
    var = jnp.mean(hc * hc, axis=-1, keepdims=True)
    o_ref[...] = (hc * lax.rsqrt(var + 1e-5) * lng_ref[...] + lnb_ref[...]).astype(o_ref.dtype)


def _out_block(o_a, o_b, pf, x2, p2, wa, wb, wo, wpg, wp, ln_g, ln_b, tm):
    t = x2.shape[0]
    bf = jnp.bfloat16

    def full(shape):
        return pl.BlockSpec(shape, lambda i: (0, 0))

    return pl.pallas_call(
        _out_kernel,
        grid=(t // tm,),
        in_specs=[
            pl.BlockSpec((tm, NSA_WIDTH), lambda i: (i, 0)),
            pl.BlockSpec((tm, DN_WIDTH), lambda i: (i, 0)),
            pl.BlockSpec((tm, D_MODEL), lambda i: (i, PF_GM // D_MODEL)),
            pl.BlockSpec((tm, D_MODEL), lambda i: (i, PF_GM // D_MODEL + 1)),
            pl.BlockSpec((tm, D_MODEL), lambda i: (i, 0)),
            pl.BlockSpec((tm, PLE_DIM), lambda i: (i, 0)),
            full((NSA_WIDTH, D_MODEL)), full((DN_WIDTH, D_MODEL)), full((D_MODEL, D_MODEL)),
            full((D_MODEL, D_MODEL)), full((PLE_DIM, D_MODEL)), full((1, D_MODEL)), full((1, D_MODEL)),
        ],
        out_specs=pl.BlockSpec((tm, D_MODEL), lambda i: (i, 0)),
        out_shape=jax.ShapeDtypeStruct((t, D_MODEL), x2.dtype),
        compiler_params=_cparams(1), name="out_block",
    )(o_a, o_b, pf, pf, x2, p2, wa.astype(bf), wb.astype(bf), wo.astype(bf), wpg.astype(bf), wp.astype(bf),
      ln_g.astype(jnp.float32).reshape(1, D_MODEL), ln_b.astype(jnp.float32).reshape(1, D_MODEL))


def _layer(x, p, w_in, pos_k, pos_v, w1_k, w2_k, w1_v, w2_v, bias_tabs, conv_w, a_log, dt_bias, norm_w,
           w_a, w_b, w_o, w_ple, w_pg, ln_g, ln_b):
    bsz, seq, _ = x.shape
    t = bsz * seq
    x2 = x.reshape(t, D_MODEL)
    wb16, wf16 = _prep_w_in(w_in)
    pb, pf = _proj(x2, wb16, wf16, PROJ_TM if t % PROJ_TM == 0 else seq)

    pos2, w2p = _prep_compress_weights(pos_k, pos_v, w2_k, w2_v)
    kcvc = _compress(pf, pos2, w1_k, w1_v, w2p, bsz, seq)

    bias_sel, bias_win, bias_cmp = bias_tabs
    o_a = _nsa(pb, pf, kcvc, bias_sel, bias_win, bias_cmp, bsz, seq)
    o_b = _deltanet(pf, conv_w, a_log, dt_bias, norm_w, bsz, seq)
    out = _out_block(o_a, o_b, pf, x2, p.reshape(t, PLE_DIM), w_a, w_b, w_o, w_pg, w_ple, ln_g, ln_b,
                     OUT_TM if t % OUT_TM == 0 else seq)
    return out.reshape(bsz, seq, D_MODEL)


def kernel(x, p, w_in, cmp_pos_k, cmp_pos_v, cmp_w1_k, cmp_w2_k, cmp_w1_v, cmp_w2_v, rel_bias, dn_conv_w,
           dn_a_log, dn_dt_bias, dn_norm_w, w_branch_a, w_branch_b, w_out, w_ple, w_ple_gate, ln_g, ln_b):
    depth = w_in.shape[0]
    bias_tabs = _bias_tables(rel_bias, x.shape[1])
    for i in range(depth):
        x = _layer(x, p[i], w_in[i], cmp_pos_k[i], cmp_pos_v[i], cmp_w1_k[i], cmp_w2_k[i], cmp_w1_v[i],
                   cmp_w2_v[i], bias_tabs, dn_conv_w[i], dn_a_log[i], dn_dt_bias[i], dn_norm_w[i],
                   w_branch_a[i], w_branch_b[i], w_out[i], w_ple[i], w_ple_gate[i], ln_g[i], ln_b[i])
    return x
```

```python
import functools
import math

import numpy as np
import jax
import jax.numpy as jnp
from jax import lax
from jax.experimental import pallas as pl
from jax.experimental.pallas import tpu as pltpu

D_MODEL = 1024
PLE_DIM = 256
NSA_HEADS = 8
NSA_GROUPS = 2
NSA_HPG = NSA_HEADS // NSA_GROUPS
NSA_HEAD_DIM = 64
NSA_WIDTH = NSA_HEADS * NSA_HEAD_DIM
NSA_KV = NSA_GROUPS * NSA_HEAD_DIM
CMP_BLOCK = 32
CMP_STRIDE = 16
CMP_HIDDEN = 256
SEL_BLOCK = 64
SEL_TOPK = 8
WINDOW = 512
DN_HEADS = 4
DN_HEAD_DIM = 128
DN_WIDTH = DN_HEADS * DN_HEAD_DIM
DN_CONV = 4
DN_CHUNK = 64
NUM_BUCKETS = 32
REL_MAX_DIST = 1024
DEEPNORM_ALPHA = 2.0 ** 0.25
NEG = -1e30
FORCE = 1e6
LOG2E = 1.4426950408889634

LANES = 128
TILE = 128
SEL_LANE0 = 64
SEL_GROUP = 4
NSA_TILES_PER_STEP = 2
NSA_V_ROWS = NSA_HEAD_DIM + 16
VMEM_LIMIT = 56 * 1024 * 1024
PROJ_TM = 512
PROJ_TN = 1024
OUT_TM = 512
OUT_SUBTILES = 2
DN_INV_BLOCK = 16
DN_BATCHES = 2
DN_OUT_UNROLL = 8
DN_DECAY_PARTS = 2
DN_PREP_UNROLL = 16

HIGHEST = lax.Precision.HIGHEST

PB_QP = 0
PB_KS = PB_QP + NSA_WIDTH
PB_KW = PB_KS + NSA_KV
PB_VS = PB_KW + NSA_KV
PB_VW = PB_VS + NSA_KV
PB_WIDTH = PB_VW + NSA_KV
PF_GM = 0
PF_QKVB = PF_GM + 2 * D_MODEL
PF_ZA = PF_QKVB + 3 * DN_WIDTH
PF_ZB = PF_ZA + NSA_WIDTH
PF_KC = PF_ZB + DN_WIDTH
PF_VC = PF_KC + NSA_KV
PF_SMALL = PF_VC + NSA_KV
PF_WIDTH = PF_SMALL + LANES
SMALL_BETA = 3 * NSA_HEADS
SMALL_A = SMALL_BETA + DN_HEADS


def _bucket_thresholds():
    max_exact = NUM_BUCKETS // 2
    span = NUM_BUCKETS - max_exact
    ratio = REL_MAX_DIST // max_exact
    thr = list(range(1, max_exact + 1))
    for k in range(1, span):
        n = max_exact
        while n ** span < max_exact ** span * ratio ** k:
            n += 1
        thr.append(n)
    return tuple(thr)


_THR = _bucket_thresholds()


def _cparams(n_axes):
    return pltpu.CompilerParams(dimension_semantics=("arbitrary",) * n_axes, vmem_limit_bytes=VMEM_LIMIT)


def _proj_kernel(x_ref, wb_ref, wf_ref, pb_ref, pf_ref):
    xb = x_ref[...].astype(jnp.bfloat16)
    pb_ref[...] = jnp.dot(xb, wb_ref[...], preferred_element_type=jnp.float32).astype(pb_ref.dtype)
    n = pf_ref.shape[1]
    for c0 in range(0, n, PROJ_TN):
        c1 = min(c0 + PROJ_TN, n)
        pf_ref[:, c0:c1] = jnp.dot(xb, wf_ref[:, c0:c1], preferred_element_type=jnp.float32)


def _proj(x2, wb, wf, tm):
    t, d = x2.shape
    resident = dict(pipeline_mode=pl.Buffered(1))
    return pl.pallas_call(
        _proj_kernel,
        grid=(t // tm,),
        in_specs=[pl.BlockSpec((tm, d), lambda i: (i, 0)),
                  pl.BlockSpec((d, PB_WIDTH), lambda i: (0, 0), **resident),
                  pl.BlockSpec((d, PF_WIDTH), lambda i: (0, 0), **resident)],
        out_specs=[pl.BlockSpec((tm, PB_WIDTH), lambda i: (i, 0)),
                   pl.BlockSpec((tm, PF_WIDTH), lambda i: (i, 0))],
        out_shape=[jax.ShapeDtypeStruct((t, PB_WIDTH), jnp.bfloat16),
                   jax.ShapeDtypeStruct((t, PF_WIDTH), jnp.float32)],
        compiler_params=_cparams(1), name="proj",
    )(x2, wb, wf)


def _prep_w_in(w):
    d = w.shape[0]
    o = 0
    wq = w[:, o:o + NSA_WIDTH]; o += NSA_WIDTH
    wkv = w[:, o:o + 6 * NSA_KV]; o += 6 * NSA_KV
    wg = w[:, o:o + 3 * NSA_HEADS]; o += 3 * NSA_HEADS
    wza = w[:, o:o + NSA_WIDTH]; o += NSA_WIDTH
    wqkvb = w[:, o:o + 3 * DN_WIDTH]; o += 3 * DN_WIDTH
    wbeta_a = w[:, o:o + 2 * DN_HEADS]; o += 2 * DN_HEADS
    wzb = w[:, o:o + DN_WIDTH]; o += DN_WIDTH
    wgm = w[:, o:o + 2 * D_MODEL]
    wkcvc, wks, wvs, wkw, wvw = (wkv[:, 0:2 * NSA_KV], wkv[:, 2 * NSA_KV:3 * NSA_KV], wkv[:, 3 * NSA_KV:4 * NSA_KV],
                                 wkv[:, 4 * NSA_KV:5 * NSA_KV], wkv[:, 5 * NSA_KV:6 * NSA_KV])
    wb = jnp.concatenate([wq * (NSA_HEAD_DIM ** -0.5 * LOG2E), wks, wkw, wvs, wvw], axis=1).astype(jnp.bfloat16)
    pad = jnp.zeros((d, LANES - 3 * NSA_HEADS - 2 * DN_HEADS), w.dtype)
    wf = jnp.concatenate([wgm, wqkvb, wza, wzb, wkcvc, wg, wbeta_a, pad], axis=1).astype(jnp.bfloat16)
    return wb, wf


def _bias_kernel(tab_ref, sel_ref, win_ref, cmp_ref, *, n_cmp):
    h = pl.program_id(0)

    def lookup(n):
        val = jnp.full(n.shape, tab_ref[0, h], jnp.float32)
        for b in range(1, NUM_BUCKETS):
            val = jnp.where(n >= _THR[b - 1], tab_ref[b, h], val)
        return val * LOG2E

    kj = lax.broadcasted_iota(jnp.int32, (TILE, TILE), 0)
    qi = lax.broadcasted_iota(jnp.int32, (TILE, TILE), 1)
    n_sel_tiles = sel_ref.shape[1] - 1
    n_win_tiles = win_ref.shape[1] - 1
    for dt in range(max(n_sel_tiles, n_win_tiles)):
        dist = dt * TILE + qi - kj
        v = lookup(jnp.maximum(dist, 0))
        if dt < n_sel_tiles:
            sel_ref[0, dt] = jnp.where(dist >= 0, v, NEG)
        if dt < n_win_tiles:
            win_ref[0, dt] = jnp.where((dist >= 0) & (dist < WINDOW), v, NEG)
    sel_ref[0, n_sel_tiles] = jnp.full((TILE, TILE), NEG, jnp.float32)
    win_ref[0, n_win_tiles] = jnp.full((TILE, TILE), NEG, jnp.float32)
    c = lax.broadcasted_iota(jnp.int32, cmp_ref.shape[2:], 0)
    for t in range(cmp_ref.shape[1]):
        s = t * TILE + lax.broadcasted_iota(jnp.int32, cmp_ref.shape[2:], 1)
        dist = s - (c * CMP_STRIDE + CMP_BLOCK - 1)
        cmp_ref[0, t] = jnp.where((dist >= 0) & (c < n_cmp), lookup(jnp.maximum(dist, 0)), NEG)


def _bias_tables(rel_bias, seq):
    nq = seq // TILE
    nwin = WINDOW // TILE + 1
    nc = seq // CMP_STRIDE
    n_cmp = nc - CMP_BLOCK // CMP_STRIDE + 1
    return pl.pallas_call(
        functools.partial(_bias_kernel, n_cmp=n_cmp),
        grid=(NSA_HEADS,),
        in_specs=[pl.BlockSpec(memory_space=pltpu.SMEM)],
        out_specs=[pl.BlockSpec((1, nq + 1, TILE, TILE), lambda h: (h, 0, 0, 0)),
                   pl.BlockSpec((1, nwin + 1, TILE, TILE), lambda h: (h, 0, 0, 0)),
                   pl.BlockSpec((1, nq, nc, TILE), lambda h: (h, 0, 0, 0))],
        out_shape=[jax.ShapeDtypeStruct((NSA_HEADS, nq + 1, TILE, TILE), jnp.float32),
                   jax.ShapeDtypeStruct((NSA_HEADS, nwin + 1, TILE, TILE), jnp.float32),
                   jax.ShapeDtypeStruct((NSA_HEADS, nq, nc, TILE), jnp.float32)],
        compiler_params=_cparams(1), name="bias_tables",
    )(rel_bias.astype(jnp.float32))


def _gelu_tanh(x):
    return x * (0.5 * (1.0 + jnp.tanh(math.sqrt(2.0 / math.pi) * (x + 0.044715 * (x * x * x)))))


def _compress_kernel(x_ref, pos_ref, w1k_ref, w1v_ref, w2_ref, o_ref, wbd):
    kv = pl.program_id(0)
    nc = o_ref.shape[2]
    half = CMP_BLOCK // 2
    dh = NSA_HEAD_DIM
    bf16 = jnp.bfloat16

    @pl.when(pl.program_id(1) == 0)
    def _():
        def build(w1_ref):
            zero = jnp.zeros((dh, CMP_HIDDEN), bf16)
            for a in range(2):
                for l in range(half):
                    r0 = (a * half + l) * dh
                    wl = w1_ref[r0:r0 + dh, :].astype(bf16)
                    wbd[a, l * NSA_KV:(l + 1) * NSA_KV, :] = jnp.concatenate(
                        [jnp.concatenate([wl, zero], axis=1), jnp.concatenate([zero, wl], axis=1)], axis=0)
        pl.when(kv == 0)(functools.partial(build, w1k_ref))
        pl.when(kv == 1)(functools.partial(build, w1v_ref))

    r = jnp.concatenate([x_ref[pl.ds(l, nc, stride=CMP_STRIDE), :] for l in range(CMP_STRIDE)], axis=1)
    a = jnp.dot((r + pos_ref[0, 0:1, :]).astype(bf16), wbd[0], preferred_element_type=jnp.float32)
    b = jnp.dot((r + pos_ref[0, 1:2, :]).astype(bf16), wbd[1], preferred_element_type=jnp.float32)
    hid = a + pltpu.roll(b, nc - 1, 0)
    o_ref[0, 0] = jnp.dot(_gelu_tanh(hid).astype(bf16), w2_ref[0], preferred_element_type=jnp.float32)


def _compress(pf, pos2, w1_k, w1_v, w2p, bsz, seq):
    nc = seq // CMP_STRIDE
    width = CMP_STRIDE * NSA_KV
    hid = NSA_GROUPS * CMP_HIDDEN
    assert CMP_BLOCK == 2 * CMP_STRIDE
    return pl.pallas_call(
        _compress_kernel,
        grid=(2, bsz),
        in_specs=[pl.BlockSpec((seq, NSA_KV), lambda k, b: (b, PF_KC // NSA_KV + k)),
                  pl.BlockSpec((1, 2, width), lambda k, b: (k, 0, 0)),
                  pl.BlockSpec(w1_k.shape, lambda k, b: (0, 0)),
                  pl.BlockSpec(w1_v.shape, lambda k, b: (0, 0)),
                  pl.BlockSpec((1, hid, LANES), lambda k, b: (k, 0, 0))],
        out_specs=pl.BlockSpec((1, 1, nc, LANES), lambda k, b: (b, k, 0, 0)),
        out_shape=jax.ShapeDtypeStruct((bsz, 2, nc, LANES), jnp.float32),
        scratch_shapes=[pltpu.VMEM((2, width, hid), jnp.bfloat16)],
        compiler_params=_cparams(2), name="compress",
    )(pf, pos2, w1_k, w1_v, w2p)


def _prep_compress_weights(pos_k, pos_v, w2_k, w2_v):
    eye = jnp.eye(NSA_GROUPS, dtype=jnp.float32)
    half = CMP_BLOCK // 2

    def w2_both(w2):
        return jnp.einsum('jd,gh->gjhd', w2, eye).reshape(NSA_GROUPS * CMP_HIDDEN, NSA_GROUPS * NSA_HEAD_DIM)

    def pos_both(pos):
        p = pos.reshape(2, half, 1, NSA_HEAD_DIM)
        return jnp.broadcast_to(p, (2, half, NSA_GROUPS, NSA_HEAD_DIM)).reshape(2, half * NSA_KV)

    pos2 = jnp.stack([pos_both(pos_k), pos_both(pos_v)]).astype(jnp.float32)
    w2p = jnp.stack([w2_both(w2_k), w2_both(w2_v)]).astype(jnp.bfloat16)
    return pos2, w2p


def _pair_lanes(x, g):
    sw = pltpu.roll(x, LANES // 2, 1)
    lane = lax.broadcasted_iota(jnp.int32, x.shape, 1)
    own = (lane < LANES // 2) == (g == 0)
    return jnp.where(own, x, sw)


def _group_rows_t(x, g):
    xt = x.T
    half = LANES // 2
    return jnp.where(g == 0, xt[0:half, :], xt[half:LANES, :])


def _nsa_step_kernel(q_ref, qn_ref, ks_ref, kw_ref, vs_ref, vw_ref, kcvc_ref, bc_ref, bsel_ref, bwin_ref,
                     small_ref, za_ref, o_ref,
                     ksa, kwp, vst, vwt, kcs, vct, qa_s, qp_s, ocmp_s, sbuf, *, n_cmp, n_sel, n_top):
    g = pl.program_id(0)
    step_id = pl.program_id(2)
    tq = TILE
    dh = NSA_HEAD_DIM
    seq = ks_ref.shape[0]
    nq = seq // TILE
    nc = kcvc_ref.shape[2]
    n_wt = bwin_ref.shape[1] - 1
    heads = range(NSA_HPG)
    f32 = jnp.float32
    bf16 = jnp.bfloat16
    nt_dims = (((1,), (1,)), ((), ()))
    gk = SEL_GROUP * TILE

    def select_tile(src_ref, r0, tile, slot):
        q32 = src_ref[r0:r0 + tq, :].astype(f32)
        low_half = lax.broadcasted_iota(jnp.int32, (tq, LANES), 1) < NSA_HEAD_DIM
        qh = []
        for hh in heads:
            blk = q32[:, (hh // 2) * LANES:(hh // 2 + 1) * LANES]
            if hh % 2:
                blk = pltpu.roll(blk, LANES // 2, 1)
            qh.append(jnp.where(low_half, blk, 0.0).astype(bf16))
        for hh in heads:
            qp_s[slot, hh * tq:(hh + 1) * tq, :] = qh[hh]
        q_all = qp_s[slot]
        s_c = lax.dot_general(kcs[...], q_all, nt_dims, preferred_element_type=f32)
        bias_c = jnp.concatenate([bc_ref[hh, tile] for hh in heads], axis=1)
        valid = bias_c > 0.5 * NEG
        s_c = s_c + bias_c
        e = jnp.where(valid, jnp.exp2(s_c - jnp.max(s_c, axis=0, keepdims=True)), 0.0)
        den = jnp.maximum(jnp.sum(e, axis=0, keepdims=True), 1e-30)
        p_c = e * (1.0 / den)
        psum = sum(p_c[:, hh * tq:(hh + 1) * tq] for hh in heads)
        sj = lax.broadcasted_iota(jnp.int32, (n_sel, nc), 0)
        ci = lax.broadcasted_iota(jnp.int32, (n_sel, nc), 1)
        overlap = ((ci * CMP_STRIDE < (sj + 1) * SEL_BLOCK) & (ci * CMP_STRIDE + CMP_BLOCK > sj * SEL_BLOCK)
                   & (ci < n_cmp)).astype(bf16)
        imp_t = sum(jnp.dot(overlap, part, preferred_element_type=f32)
                    for part in _split_bf16(psum, 3))
        ocmp_s[slot] = jnp.dot(vct[...], p_c.astype(bf16), preferred_element_type=f32)
        blk = lax.broadcasted_iota(jnp.int32, (n_sel, tq), 0)
        cur = (tile * tq + lax.broadcasted_iota(jnp.int32, (n_sel, tq), 1)) >> 6
        forced = (blk == 0) | (blk == cur) | (blk == cur - 1)
        imp_t = jnp.where(forced, FORCE, jnp.where(blk > cur, -FORCE, imp_t))
        rank = jnp.zeros((n_sel, tq), jnp.int32)
        for j in range(n_sel):
            other = imp_t[j:j + 1, :]
            ahead = (other > imp_t) | ((other == imp_t) & (blk > j))
            rank = rank + ahead.astype(jnp.int32)
        selb = jnp.where(rank < n_top, 0.0, NEG)
        pieces = [jnp.zeros((SEL_LANE0, tq), f32), selb]
        if LANES - SEL_LANE0 - n_sel > 0:
            pieces.append(jnp.zeros((LANES - SEL_LANE0 - n_sel, tq), f32))
        selb_r = jnp.concatenate(pieces, axis=0).T.astype(bf16)
        for hh in heads:
            qa_s[slot, hh * tq:(hh + 1) * tq, :] = qh[hh] + selb_r

    @pl.when(step_id == 0)
    def _():
        row = lax.broadcasted_iota(jnp.int32, (seq, LANES), 0)
        lane = lax.broadcasted_iota(jnp.int32, (seq, LANES), 1)
        onehot = (lane - SEL_LANE0) == (row >> 6)
        own = lane < NSA_HEAD_DIM
        ks_g = jnp.where(own, _pair_lanes(ks_ref[...].astype(f32), g), 0.0)
        ksa[...] = jnp.where(onehot, 1.0, ks_g).astype(bf16)
        pad = (n_wt - 1) * TILE
        kwp[0:pad, :] = jnp.zeros((pad, LANES), bf16)
        kwp[pad:pad + seq, :] = jnp.where(own, _pair_lanes(kw_ref[...].astype(f32), g), 0.0).astype(bf16)
        for kt in range(n_wt - 1):
            vwt[kt] = jnp.zeros(vwt.shape[1:], bf16)
        extra = vst.shape[1] - dh
        ones_row = (lax.broadcasted_iota(jnp.int32, (extra, TILE), 0) == 0).astype(bf16)
        for kt in range(nq):
            rows = slice(kt * TILE, (kt + 1) * TILE)
            vst[kt, 0:dh, :] = _group_rows_t(vs_ref[rows, :].astype(f32), g).astype(bf16)
            vst[kt, dh:dh + extra, :] = ones_row
            vwt[kt + n_wt - 1, 0:dh, :] = _group_rows_t(vw_ref[rows, :].astype(f32), g).astype(bf16)
            vwt[kt + n_wt - 1, dh:dh + extra, :] = ones_row
        kcs[...] = _pair_lanes(kcvc_ref[0, 0], g).astype(bf16)
        for ct in range(nc // TILE):
            rows = slice(ct * TILE, (ct + 1) * TILE)
            vct[:, rows] = _group_rows_t(kcvc_ref[0, 1, rows, :], g).astype(bf16)
        select_tile(q_ref, 0, 0, 0)

    tps = NSA_TILES_PER_STEP

    def one_tile(n_groups, slot):
        qt = step_id * tps + slot
        rows = slice(slot * tq, (slot + 1) * tq)
        q_all = qp_s[slot]
        q_aug = qa_s[slot]
        kw_rows = kwp[pl.ds(pl.multiple_of(qt * TILE, TILE), n_wt * TILE), :]
        s_w = lax.dot_general(kw_rows, q_all, nt_dims, preferred_element_type=f32)

        def sel_scores(gi):
            s = lax.dot_general(ksa[gi * gk:(gi + 1) * gk, :], q_aug, nt_dims, preferred_element_type=f32)
            tiles = []
            for t in range(SEL_GROUP):
                dt = qt - (gi * SEL_GROUP + t)
                idx = jnp.where(dt >= 0, dt, nq)
                tiles.append(jnp.concatenate([bsel_ref[hh, idx] for hh in heads], axis=1))
            sbuf[gi % 2] = s + jnp.concatenate(tiles, axis=0)

        sel_scores(0)
        if slot + 1 < tps:
            select_tile(q_ref, (slot + 1) * tq, qt + 1, slot + 1)
        else:
            select_tile(qn_ref, 0, jnp.minimum(qt + 1, nq - 1), 0)

        w_tiles = []
        for t in range(n_wt):
            dt = n_wt - 1 - t
            idx = jnp.where(qt >= dt, dt, n_wt)
            w_tiles.append(jnp.concatenate([bwin_ref[hh, idx] for hh in heads], axis=1))
        s_w = s_w + jnp.concatenate(w_tiles, axis=0)
        p_w = jnp.exp2(s_w - jnp.max(s_w, axis=0, keepdims=True))
        v_w = jnp.concatenate([vwt[qt + t] for t in range(n_wt)], axis=1)
        acc_w = jnp.dot(v_w, p_w.astype(bf16), preferred_element_type=f32)
        o_win = acc_w[0:dh, :] * (1.0 / acc_w[dh:dh + 1, :])

        m = jnp.full((1, NSA_HPG * tq), NEG, f32)
        acc = jnp.zeros((vst.shape[1], NSA_HPG * tq), f32)
        for gi in range(n_groups):
            if gi + 1 < n_groups:
                sel_scores(gi + 1)
            s = sbuf[gi % 2]
            m_new = jnp.maximum(m, jnp.max(s, axis=0, keepdims=True))
            alpha = jnp.exp2(m - m_new)
            p = jnp.exp2(s - m_new)
            v_t = jnp.concatenate([vst[gi * SEL_GROUP + t] for t in range(SEL_GROUP)], axis=1)
            acc = alpha * acc + jnp.dot(v_t, p.astype(bf16), preferred_element_type=f32)
            m = m_new
        o_sel = acc[0:dh, :] * (1.0 / acc[dh:dh + 1, :])
        o_cmp = ocmp_s[slot]

        gates_t = jax.nn.sigmoid(small_ref[rows, :]).T
        mixed = []
        for hh in heads:
            def gate(br):
                r = br * NSA_HEADS + hh
                return jnp.where(g == 0, gates_t[r:r + 1, :], gates_t[r + NSA_HPG:r + NSA_HPG + 1, :])
            cols = slice(hh * tq, (hh + 1) * tq)
            mixed.append(gate(0) * o_cmp[:, cols] + gate(1) * o_sel[:, cols] + gate(2) * o_win[:, cols])
        o = jnp.concatenate([jnp.concatenate(mixed[2 * j:2 * j + 2], axis=0).T for j in range(NSA_HPG // 2)],
                            axis=1)
        z = za_ref[rows, :]
        o_ref[rows, :] = (o * (z * jax.nn.sigmoid(z))).astype(o_ref.dtype)

    def step(n_groups):
        for slot in range(tps):
            one_tile(n_groups, slot)

    for n_groups in range(1, nq // SEL_GROUP + 1):
        pl.when(step_id // (SEL_GROUP // tps) == n_groups - 1)(functools.partial(step, n_groups))


def _nsa(pb, pf, kcvc, bias_sel, bias_win, bias_cmp, bsz, seq):
    nq = seq // TILE
    nc = seq // CMP_STRIDE
    n_cmp = nc - CMP_BLOCK // CMP_STRIDE + 1
    n_sel = seq // SEL_BLOCK
    n_top = min(SEL_TOPK, n_sel)
    nwin = bias_win.shape[1]
    tps = NSA_TILES_PER_STEP
    assert nq % SEL_GROUP == 0 and nc % TILE == 0 and bias_sel.shape[1] == nq + 1
    assert tps >= 2 and SEL_GROUP % tps == 0
    ns = nq // tps
    rows = tps * TILE
    gw = NSA_HPG * NSA_HEAD_DIM
    kern = functools.partial(_nsa_step_kernel, n_cmp=n_cmp, n_sel=n_sel, n_top=n_top)
    return pl.pallas_call(
        kern,
        grid=(NSA_GROUPS, bsz, ns),
        in_specs=[
            pl.BlockSpec((rows, gw), lambda g, b, t: (b * ns + t, PB_QP // gw + g)),
            pl.BlockSpec((TILE, gw), lambda g, b, t: (b * nq + jnp.minimum((t + 1) * tps, nq - 1),
                                                      PB_QP // gw + g)),
            pl.BlockSpec((seq, LANES), lambda g, b, t: (b, PB_KS // LANES)),
            pl.BlockSpec((seq, LANES), lambda g, b, t: (b, PB_KW // LANES)),
            pl.BlockSpec((seq, LANES), lambda g, b, t: (b, PB_VS // LANES)),
            pl.BlockSpec((seq, LANES), lambda g, b, t: (b, PB_VW // LANES)),
            pl.BlockSpec((1, 2, nc, LANES), lambda g, b, t: (b, 0, 0, 0)),
            pl.BlockSpec((NSA_HPG, nq, nc, TILE), lambda g, b, t: (g, 0, 0, 0)),
            pl.BlockSpec((NSA_HPG, nq + 1, TILE, TILE), lambda g, b, t: (g, 0, 0, 0)),
            pl.BlockSpec((NSA_HPG, nwin, TILE, TILE), lambda g, b, t: (g, 0, 0, 0)),
            pl.BlockSpec((rows, LANES), lambda g, b, t: (b * ns + t, PF_SMALL // LANES)),
            pl.BlockSpec((rows, gw), lambda g, b, t: (b * ns + t, PF_ZA // gw + g)),
        ],
        out_specs=pl.BlockSpec((rows, gw), lambda g, b, t: (b * ns + t, g)),
        out_shape=jax.ShapeDtypeStruct((bsz * seq, NSA_WIDTH), jnp.bfloat16),
        scratch_shapes=[
            pltpu.VMEM((seq, LANES), jnp.bfloat16),
            pltpu.VMEM((seq + (nwin - 2) * TILE, LANES), jnp.bfloat16),
            pltpu.VMEM((nq, NSA_V_ROWS, TILE), jnp.bfloat16),
            pltpu.VMEM((nq + nwin - 2, NSA_V_ROWS, TILE), jnp.bfloat16),
            pltpu.VMEM((nc, LANES), jnp.bfloat16),
            pltpu.VMEM((NSA_HEAD_DIM, nc), jnp.bfloat16),
            pltpu.VMEM((tps, NSA_HPG * TILE, LANES), jnp.bfloat16),
            pltpu.VMEM((tps, NSA_HPG * TILE, LANES), jnp.bfloat16),
            pltpu.VMEM((tps, NSA_HEAD_DIM, NSA_HPG * TILE), jnp.float32),
            pltpu.VMEM((2, SEL_GROUP * TILE, NSA_HPG * TILE), jnp.float32),
        ],
        compiler_params=_cparams(3), name="nsa",
    )(pb, pb, pb, pb, pb, pb, kcvc, bias_cmp, bias_sel, bias_win, pf, pf)


def _split_bf16(a, n):
    parts = []
    for _ in range(n - 1):
        hi = a.astype(jnp.bfloat16)
        parts.append(hi)
        a = a - hi.astype(jnp.float32)
    parts.append(a.astype(jnp.bfloat16))
    return parts


def _dot3(a, b):
    ah, al = _split_bf16(a, 2)
    bh, bl = _split_bf16(b, 2)
    f32 = jnp.float32
    return (jnp.dot(ah, bh, preferred_element_type=f32) + jnp.dot(al, bh, preferred_element_type=f32)
            + jnp.dot(ah, bl, preferred_element_type=f32))


def _softplus(x):
    return jnp.maximum(x, 0.0) + jnp.log(1.0 + jnp.exp(-jnp.abs(x)))


def _dn_kernel(scal_ref, q_ref, k_ref, v_ref, small_ref, z_ref, cw_ref, nw_ref, o_ref,
               qn, kn, vn, bet, gl, mm, nn, qq, oo, dd, ss, *, seq):
    h = pl.program_id(1)
    n_rows = q_ref.shape[0]
    nb = n_rows // seq
    c = DN_CHUNK
    n_chunks = seq // c
    d = DN_HEAD_DIM

    head = 8

    def conv_body(x_ref, which):
        n = n_rows - head
        y = x_ref[pl.ds(head, n), :] * cw_ref[which, DN_CONV - 1:DN_CONV, :]
        for j in range(DN_CONV - 1):
            y = y + x_ref[pl.ds(head - (DN_CONV - 1 - j), n), :] * cw_ref[which, j:j + 1, :]
        return y

    def conv_head(x_ref, which, r0):
        x = x_ref[pl.ds(r0, head), :]
        rowi = lax.broadcasted_iota(jnp.int32, (head, d), 0)
        y = x * cw_ref[which, DN_CONV - 1:DN_CONV, :]
        for j in range(DN_CONV - 1):
            sh = DN_CONV - 1 - j
            y = y + jnp.where(rowi >= sh, pltpu.roll(x, sh, 0), 0.0) * cw_ref[which, j:j + 1, :]
        return y

    def silu(y):
        return y * jax.nn.sigmoid(y)

    def l2n(t):
        return t * lax.rsqrt(jnp.sum(t * t, axis=-1, keepdims=True) + 1e-6)

    finish = (lambda y: l2n(silu(y)) * (d ** -0.5), lambda y: l2n(silu(y)), silu)
    for which, (x_ref, dst) in enumerate(((q_ref, qn), (k_ref, kn), (v_ref, vn))):
        dst[pl.ds(head, n_rows - head), :] = finish[which](conv_body(x_ref, which))
        for b in range(nb):
            dst[pl.ds(b * seq, head), :] = finish[which](conv_head(x_ref, which, b * seq))
    small = small_ref[...]
    lane = lax.broadcasted_iota(jnp.int32, small.shape, 1)
    beta_in = jnp.sum(jnp.where(lane == SMALL_BETA + h, small, 0.0), axis=-1, keepdims=True)
    a_in = jnp.sum(jnp.where(lane == SMALL_A + h, small, 0.0), axis=-1, keepdims=True)
    bet[...] = jnp.broadcast_to(jax.nn.sigmoid(beta_in), (n_rows, d))
    gl[...] = jnp.broadcast_to(-jnp.exp(scal_ref[0, h]) * _softplus(a_in + scal_ref[1, h]), (n_rows, d))

    ri = lax.broadcasted_iota(jnp.int32, (c, c), 0)
    cj = lax.broadcasted_iota(jnp.int32, (c, c), 1)
    incl = ri >= cj
    strict = ri > cj
    tril = incl.astype(jnp.float32)
    eye = (ri == cj).astype(jnp.float32)

    tril_b = tril.astype(jnp.bfloat16)
    bs = DN_INV_BLOCK
    sh = bs.bit_length() - 1
    same_diag = (ri >> sh) == (cj >> sh)
    level_masks = []
    while (1 << sh) < c:
        level_masks.append(((ri >> (sh + 1)) == (cj >> (sh + 1))) & ((ri >> sh) > (cj >> sh)))
        sh += 1

    bf16 = jnp.bfloat16
    nt_dims = (((1,), (1,)), ((), ()))

    def dot1(a, b):
        return jnp.dot(a.astype(bf16), b.astype(bf16), preferred_element_type=jnp.float32)

    def chunk_prep(it, carry):
        ids = [it * DN_PREP_UNROLL + cc for cc in range(DN_PREP_UNROLL)]
        rows = [pl.ds(pl.multiple_of(i * c, c), c) for i in ids]
        ks = [kn[r, :] for r in rows]
        betas = [bet[r, :] for r in rows]
        gcbs = [sum(jnp.dot(tril_b, part, preferred_element_type=jnp.float32)
                    for part in _split_bf16(gl[r, :], DN_DECAY_PARTS)) for r in rows]
        kbs = [k * beta for k, beta in zip(ks, betas)]
        kbfs = [k.astype(bf16) for k in ks]
        a_kks = [lax.dot_general(kb.astype(bf16), kbf, nt_dims, preferred_element_type=jnp.float32)
                 for kb, kbf in zip(kbs, kbfs)]
        decays = []
        for gcb in gcbs:
            gct = jnp.concatenate([gcb, gcb], axis=0).T
            diff = gcb[:, 0:c] - gct[0:c, 0:c]
            decays.append(jnp.where(incl, jnp.exp(jnp.where(incl, diff, 0.0)), 0.0))
        lows = [jnp.where(strict, a * dec, 0.0) for a, dec in zip(a_kks, decays)]
        pws = [jnp.where(same_diag, -low, 0.0) for low in lows]
        es = list(pws)
        for _ in range(max(1, (bs - 1).bit_length()) - 1):
            pws = [dot1(pw, pw) for pw in pws]
            es = [e + pw + dot1(e, pw) for e, pw in zip(es, pws)]
        for below in level_masks:
            offs = [jnp.where(below, low, 0.0) for low in lows]
            xs = [off + dot1(e, off) for e, off in zip(es, offs)]
            es = [e - (x + dot1(x, e)) for e, x in zip(es, xs)]
        egcs = [jnp.exp(gcb) for gcb in gcbs]
        rhss = [jnp.concatenate([vn[r, :] * beta, kb * egc], axis=1)
                for r, beta, kb, egc in zip(rows, betas, kbs, egcs)]
        uws = [rhs + _dot3(e, rhs) for e, rhs in zip(es, rhss)]
        qs = [qn[r, :] for r in rows]
        a_qks = [lax.dot_general(q.astype(bf16), kbf, nt_dims, preferred_element_type=jnp.float32) * dec
                 for q, kbf, dec in zip(qs, kbfs, decays)]
        g_lasts = [gcb[c - 1:c, :] for gcb in gcbs]
        kdec_ts = []
        for k, gcb, g_last in zip(ks, gcbs, g_lasts):
            kdec = k * jnp.exp(g_last - gcb)
            kdec_ts.append(jnp.concatenate([kdec, jnp.zeros_like(kdec)], axis=0).T[:, 0:c].astype(bf16))
        uwbs = [uw.astype(bf16) for uw in uws]
        nms = [jnp.dot(kt, uwb, preferred_element_type=jnp.float32) for kt, uwb in zip(kdec_ts, uwbs)]
        oqs = [jnp.dot(a.astype(bf16), uwb, preferred_element_type=jnp.float32) for a, uwb in zip(a_qks, uwbs)]
        for i, r, nm, oq, q, egc, g_last in zip(ids, rows, nms, oqs, qs, egcs, g_lasts):
            m0 = pl.ds(pl.multiple_of(i * d, d), d)
            nn[m0, :] = nm[:, 0:d]
            mm[m0, :] = nm[:, d:2 * d].astype(bf16)
            oo[r, :] = oq[:, 0:d]
            qq[r, :] = (q * egc - oq[:, d:2 * d]).astype(bf16)
            dd[pl.ds(pl.multiple_of(i * 8, 8), 8), :] = jnp.broadcast_to(jnp.exp(g_last), (8, d))
        return carry

    lax.fori_loop(0, nb * n_chunks // DN_PREP_UNROLL, chunk_prep, 0)

    def chunk_scan(i, states):
        ids = [b * n_chunks + i for b in range(nb)]
        blocks = [pl.ds(pl.multiple_of(j * d, d), d) for j in ids]
        sbs = [s.astype(bf16) for s in states]
        for blk, sb in zip(blocks, sbs):
            ss[blk, :] = sb
        prods = [jnp.dot(mm[blk, :], sb, preferred_element_type=jnp.float32) for blk, sb in zip(blocks, sbs)]
        return tuple(s * dd[pl.ds(pl.multiple_of(j * 8, 8), 1), :] - pr + nn[blk, :]
                     for s, j, blk, pr in zip(states, ids, blocks, prods))

    lax.fori_loop(0, n_chunks, chunk_scan, tuple(jnp.zeros((d, d), jnp.float32) for _ in range(nb)))

    nw = nw_ref[...]

    def chunk_out(it, carry):
        ids = [it * DN_OUT_UNROLL + cc for cc in range(DN_OUT_UNROLL)]
        rows = [pl.ds(pl.multiple_of(i * c, c), c) for i in ids]
        outs = [jnp.dot(qq[r, :], ss[pl.ds(pl.multiple_of(i * d, d), d), :], preferred_element_type=jnp.float32)
                + oo[r, :] for i, r in zip(ids, rows)]
        for r, o in zip(rows, outs):
            o = o * lax.rsqrt(jnp.mean(o * o, axis=-1, keepdims=True) + 1e-6) * nw
            z = z_ref[r, :]
            o_ref[r, :] = (o * (z * jax.nn.sigmoid(z))).astype(o_ref.dtype)
        return carry

    lax.fori_loop(0, nb * n_chunks // DN_OUT_UNROLL, chunk_out, 0)


def _deltanet(pf, conv_w, a_log, dt_bias, norm_w, bsz, seq):
    d = DN_HEAD_DIM
    nb = DN_BATCHES if bsz % DN_BATCHES == 0 else 1
    rows = nb * seq
    n_chunks = rows // DN_CHUNK
    qkv0 = PF_QKVB // d
    scal = jnp.stack([a_log, dt_bias]).astype(jnp.float32)
    f32 = jnp.float32
    bf16 = jnp.bfloat16
    cw4 = conv_w.astype(f32).reshape(DN_CONV, 3, DN_HEADS, d).transpose(2, 1, 0, 3)
    assert 2 * DN_CHUNK == d and n_chunks % DN_PREP_UNROLL == 0 and n_chunks % DN_OUT_UNROLL == 0
    return pl.pallas_call(
        functools.partial(_dn_kernel, seq=seq),
        grid=(bsz // nb, DN_HEADS),
        in_specs=[
            pl.BlockSpec(memory_space=pltpu.SMEM),
            pl.BlockSpec((rows, d), lambda b, h: (b, qkv0 + h)),
            pl.BlockSpec((rows, d), lambda b, h: (b, qkv0 + DN_HEADS + h)),
            pl.BlockSpec((rows, d), lambda b, h: (b, qkv0 + 2 * DN_HEADS + h)),
            pl.BlockSpec((rows, LANES), lambda b, h: (b, PF_SMALL // LANES)),
            pl.BlockSpec((rows, d), lambda b, h: (b, PF_ZB // d + h)),
            pl.BlockSpec((None, 3, DN_CONV, d), lambda b, h: (h, 0, 0, 0)),
            pl.BlockSpec((1, d), lambda b, h: (0, 0)),
        ],
        out_specs=pl.BlockSpec((rows, d), lambda b, h: (b, h)),
        out_shape=jax.ShapeDtypeStruct((bsz * seq, DN_WIDTH), jnp.bfloat16),
        scratch_shapes=[
            pltpu.VMEM((rows, d), f32), pltpu.VMEM((rows, d), f32), pltpu.VMEM((rows, d), f32),
            pltpu.VMEM((rows, d), f32), pltpu.VMEM((rows, d), f32),
            pltpu.VMEM((n_chunks * d, d), bf16), pltpu.VMEM((n_chunks * d, d), f32),
            pltpu.VMEM((rows, d), bf16), pltpu.VMEM((rows, d), f32),
            pltpu.VMEM((n_chunks * 8, d), f32),
            pltpu.VMEM((n_chunks * d, d), bf16),
        ],
        compiler_params=_cparams(2), name="deltanet",
    )(scal, pf, pf, pf, pf, pf, cw4, norm_w.astype(f32).reshape(1, d))


def _out_kernel(oa_ref, ob_ref, gma_ref, gmb_ref, x_ref, p_ref, wa_ref, wb_ref, wo_ref, wpg_ref, wp_ref,
                lng_ref, lnb_ref, o_ref):
    f32 = jnp.float32
    bf16 = jnp.bfloat16
    tm = o_ref.shape[0]
    sub = tm // OUT_SUBTILES
    parts = [slice(i * sub, (i + 1) * sub) for i in range(OUT_SUBTILES)]
    y_a = [jnp.dot(oa_ref[r, :], wa_ref[...], preferred_element_type=f32) for r in parts]
    y_b = [jnp.dot(ob_ref[r, :], wb_ref[...], preferred_element_type=f32) for r in parts]
    pw = [jnp.dot(p_ref[r, :].astype(bf16), wp_ref[...], preferred_element_type=f32) for r in parts]
    mix = [(jax.nn.sigmoid(gma_ref[r, :]) * ya + jax.nn.sigmoid(gmb_ref[r, :]) * yb).astype(bf16)
           for r, ya, yb in zip(parts, y_a, y_b)]
    h = [DEEPNORM_ALPHA * x_ref[r, :] + jnp.dot(mx, wo_ref[...], preferred_element_type=f32)
         for r, mx in zip(parts, mix)]
    gate = [jnp.dot(hh.astype(bf16), wpg_ref[...], preferred_element_type=f32) for hh in h]
    for r, hh, gt, pp in zip(parts, h, gate, pw):
        hh = hh + jax.nn.sigmoid(gt) * pp
        mu = jnp.mean(hh, axis=-1, keepdims=True)
        hc = hh - mu
        var = jnp.mean(hc * hc, axis=-1, keepdims=True)
        o_ref[r, :] = (hc * lax.rsqrt(var + 1e-5) * lng_ref[...] + lnb_ref[...]).astype(o_ref.dtype)


def _out_block(o_a, o_b, pf, x2, p2, wa, wb, wo, wpg, wp, ln_g, ln_b, tm):
    t = x2.shape[0]
    bf = jnp.bfloat16

    def full(shape):
        return pl.BlockSpec(shape, lambda i: (0, 0))

    return pl.pallas_call(
        _out_kernel,
        grid=(t // tm,),
        in_specs=[
            pl.BlockSpec((tm, NSA_WIDTH), lambda i: (i, 0)),
            pl.BlockSpec((tm, DN_WIDTH), lambda i: (i, 0)),
            pl.BlockSpec((tm, D_MODEL), lambda i: (i, PF_GM // D_MODEL)),
            pl.BlockSpec((tm, D_MODEL), lambda i: (i, PF_GM // D_MODEL + 1)),
            pl.BlockSpec((tm, D_MODEL), lambda i: (i, 0)),
            pl.BlockSpec((tm, PLE_DIM), lambda i: (i, 0)),
            full((NSA_WIDTH, D_MODEL)), full((DN_WIDTH, D_MODEL)), full((D_MODEL, D_MODEL)),
            full((D_MODEL, D_MODEL)), full((PLE_DIM, D_MODEL)), full((1, D_MODEL)), full((1, D_MODEL)),
        ],
        out_specs=pl.BlockSpec((tm, D_MODEL), lambda i: (i, 0)),
        out_shape=jax.ShapeDtypeStruct((t, D_MODEL), x2.dtype),
        compiler_params=_cparams(1), name="out_block",
    )(o_a, o_b, pf, pf, x2, p2, wa.astype(bf), wb.astype(bf), wo.astype(bf), wpg.astype(bf), wp.astype(bf),
      ln_g.astype(jnp.float32).reshape(1, D_MODEL), ln_b.astype(jnp.float32).reshape(1, D_MODEL))


def _layer(x, p, w_in, pos_k, pos_v, w1_k, w2_k, w1_v, w2_v, bias_tabs, conv_w, a_log, dt_bias, norm_w,
           w_a, w_b, w_o, w_ple, w_pg, ln_g, ln_b):
    bsz, seq, _ = x.shape
    t = bsz * seq
    x2 = x.reshape(t, D_MODEL)
    wb16, wf16 = _prep_w_in(w_in)
    pb, pf = _proj(x2, wb16, wf16, PROJ_TM if t % PROJ_TM == 0 else seq)

    pos2, w2p = _prep_compress_weights(pos_k, pos_v, w2_k, w2_v)
    kcvc = _compress(pf, pos2, w1_k, w1_v, w2p, bsz, seq)

    bias_sel, bias_win, bias_cmp = bias_tabs
    o_a = _nsa(pb, pf, kcvc, bias_sel, bias_win, bias_cmp, bsz, seq)
    o_b = _deltanet(pf, conv_w, a_log, dt_bias, norm_w, bsz, seq)
    out = _out_block(o_a, o_b, pf, x2, p.reshape(t, PLE_DIM), w_a, w_b, w_o, w_pg, w_ple, ln_g, ln_b,
                     OUT_TM if t % OUT_TM == 0 else seq)
    return out.reshape(bsz, seq, D_MODEL)


def kernel(x, p, w_in, cmp_pos_k, cmp_pos_v, cmp_w1_k, cmp_w2_k, cmp_w1_v, cmp_w2_v, rel_bias, dn_conv_w,
           dn_a_log, dn_dt_bias, dn_norm_w, w_branch_a, w_branch_b, w_out, w_ple, w_ple_gate, ln_g, ln_b):
    depth = w_in.shape[0]
    bias_tabs = _bias_tables(rel_bias, x.shape[1])
    for i in range(depth):
        x = _layer(x, p[i], w_in[i], cmp_pos_k[i], cmp_pos_v[i], cmp_w1_k[i], cmp_w2_k[i], cmp_w1_v[i],
                   cmp_w2_v[i], bias_tabs, dn_conv_w[i], dn_a_log[i], dn_dt_bias[i], dn_norm_w[i],
                   w_branch_a[i], w_branch_b[i], w_out[i], w_ple[i], w_ple_gate[i], ln_g[i], ln_b[i])
    return x
```

```python
import functools
import math

import numpy as np
import jax
import jax.numpy as jnp
from jax import lax
from jax.experimental import pallas as pl
from jax.experimental.pallas import tpu as pltpu

D_MODEL = 1024
PLE_DIM = 256
NSA_HEADS = 8
NSA_GROUPS = 2
NSA_HPG = NSA_HEADS // NSA_GROUPS
NSA_HEAD_DIM = 64
NSA_WIDTH = NSA_HEADS * NSA_HEAD_DIM
NSA_KV = NSA_GROUPS * NSA_HEAD_DIM
CMP_BLOCK = 32
CMP_STRIDE = 16
CMP_HIDDEN = 256
SEL_BLOCK = 64
SEL_TOPK = 8
WINDOW = 512
DN_HEADS = 4
DN_HEAD_DIM = 128
DN_WIDTH = DN_HEADS * DN_HEAD_DIM
DN_CONV = 4
DN_CHUNK = 64
NUM_BUCKETS = 32
REL_MAX_DIST = 1024
DEEPNORM_ALPHA = 2.0 ** 0.25
NEG = -1e30
FORCE = 1e6
LOG2E = 1.4426950408889634

LANES = 128
TILE = 128
SEL_LANE0 = 64
SEL_GROUP = 4
NSA_TILES_PER_STEP = 4
NSA_V_ROWS = NSA_HEAD_DIM + 16
VMEM_LIMIT = 56 * 1024 * 1024
PROJ_TM = 512
PROJ_TN = 1024
OUT_TM = 512
OUT_SUBTILES = 2
DN_INV_BLOCK = 16
DN_BATCHES = 2
DN_OUT_UNROLL = 8
DN_DECAY_PARTS = 2
DN_PREP_UNROLL = 16

HIGHEST = lax.Precision.HIGHEST

PB_QP = 0
PB_KS = PB_QP + NSA_WIDTH
PB_KW = PB_KS + NSA_KV
PB_VS = PB_KW + NSA_KV
PB_VW = PB_VS + NSA_KV
PB_WIDTH = PB_VW + NSA_KV
PF_GM = 0
PF_QKVB = PF_GM + 2 * D_MODEL
PF_ZA = PF_QKVB + 3 * DN_WIDTH
PF_ZB = PF_ZA + NSA_WIDTH
PF_KC = PF_ZB + DN_WIDTH
PF_VC = PF_KC + NSA_KV
PF_SMALL = PF_VC + NSA_KV
PF_WIDTH = PF_SMALL + LANES
SMALL_BETA = 3 * NSA_HEADS
SMALL_A = SMALL_BETA + DN_HEADS


def _bucket_thresholds():
    max_exact = NUM_BUCKETS // 2
    span = NUM_BUCKETS - max_exact
    ratio = REL_MAX_DIST // max_exact
    thr = list(range(1, max_exact + 1))
    for k in range(1, span):
        n = max_exact
        while n ** span < max_exact ** span * ratio ** k:
            n += 1
        thr.append(n)
    return tuple(thr)


_THR = _bucket_thresholds()


def _cparams(n_axes):
    return pltpu.CompilerParams(dimension_semantics=("arbitrary",) * n_axes, vmem_limit_bytes=VMEM_LIMIT)


def _proj_kernel(x_ref, wb_ref, wf_ref, pb_ref, pf_ref):
    xb = x_ref[...].astype(jnp.bfloat16)
    pb_ref[...] = jnp.dot(xb, wb_ref[...], preferred_element_type=jnp.float32).astype(pb_ref.dtype)
    n = pf_ref.shape[1]
    for c0 in range(0, n, PROJ_TN):
        c1 = min(c0 + PROJ_TN, n)
        pf_ref[:, c0:c1] = jnp.dot(xb, wf_ref[:, c0:c1], preferred_element_type=jnp.float32)


def _proj(x2, wb, wf, tm):
    t, d = x2.shape
    resident = dict(pipeline_mode=pl.Buffered(1))
    return pl.pallas_call(
        _proj_kernel,
        grid=(t // tm,),
        in_specs=[pl.BlockSpec((tm, d), lambda i: (i, 0)),
                  pl.BlockSpec((d, PB_WIDTH), lambda i: (0, 0), **resident),
                  pl.BlockSpec((d, PF_WIDTH), lambda i: (0, 0), **resident)],
        out_specs=[pl.BlockSpec((tm, PB_WIDTH), lambda i: (i, 0)),
                   pl.BlockSpec((tm, PF_WIDTH), lambda i: (i, 0))],
        out_shape=[jax.ShapeDtypeStruct((t, PB_WIDTH), jnp.bfloat16),
                   jax.ShapeDtypeStruct((t, PF_WIDTH), jnp.float32)],
        compiler_params=_cparams(1), name="proj",
    )(x2, wb, wf)


def _prep_w_in(w):
    d = w.shape[0]
    o = 0
    wq = w[:, o:o + NSA_WIDTH]; o += NSA_WIDTH
    wkv = w[:, o:o + 6 * NSA_KV]; o += 6 * NSA_KV
    wg = w[:, o:o + 3 * NSA_HEADS]; o += 3 * NSA_HEADS
    wza = w[:, o:o + NSA_WIDTH]; o += NSA_WIDTH
    wqkvb = w[:, o:o + 3 * DN_WIDTH]; o += 3 * DN_WIDTH
    wbeta_a = w[:, o:o + 2 * DN_HEADS]; o += 2 * DN_HEADS
    wzb = w[:, o:o + DN_WIDTH]; o += DN_WIDTH
    wgm = w[:, o:o + 2 * D_MODEL]
    wkcvc, wks, wvs, wkw, wvw = (wkv[:, 0:2 * NSA_KV], wkv[:, 2 * NSA_KV:3 * NSA_KV], wkv[:, 3 * NSA_KV:4 * NSA_KV],
                                 wkv[:, 4 * NSA_KV:5 * NSA_KV], wkv[:, 5 * NSA_KV:6 * NSA_KV])
    wb = jnp.concatenate([wq * (NSA_HEAD_DIM ** -0.5 * LOG2E), wks, wkw, wvs, wvw], axis=1).astype(jnp.bfloat16)
    pad = jnp.zeros((d, LANES - 3 * NSA_HEADS - 2 * DN_HEADS), w.dtype)
    wf = jnp.concatenate([wgm, wqkvb, wza, wzb, wkcvc, wg, wbeta_a, pad], axis=1).astype(jnp.bfloat16)
    return wb, wf


def _bias_kernel(tab_ref, sel_ref, win_ref, cmp_ref, *, n_cmp):
    h = pl.program_id(0)

    def lookup(n):
        val = jnp.full(n.shape, tab_ref[0, h], jnp.float32)
        for b in range(1, NUM_BUCKETS):
            val = jnp.where(n >= _THR[b - 1], tab_ref[b, h], val)
        return val * LOG2E

    kj = lax.broadcasted_iota(jnp.int32, (TILE, TILE), 0)
    qi = lax.broadcasted_iota(jnp.int32, (TILE, TILE), 1)
    n_sel_tiles = sel_ref.shape[1] - 1
    n_win_tiles = win_ref.shape[1] - 1
    for dt in range(max(n_sel_tiles, n_win_tiles)):
        dist = dt * TILE + qi - kj
        v = lookup(jnp.maximum(dist, 0))
        if dt < n_sel_tiles:
            sel_ref[0, dt] = jnp.where(dist >= 0, v, NEG)
        if dt < n_win_tiles:
            win_ref[0, dt] = jnp.where((dist >= 0) & (dist < WINDOW), v, NEG)
    sel_ref[0, n_sel_tiles] = jnp.full((TILE, TILE), NEG, jnp.float32)
    win_ref[0, n_win_tiles] = jnp.full((TILE, TILE), NEG, jnp.float32)
    c = lax.broadcasted_iota(jnp.int32, cmp_ref.shape[2:], 0)
    for t in range(cmp_ref.shape[1]):
        s = t * TILE + lax.broadcasted_iota(jnp.int32, cmp_ref.shape[2:], 1)
        dist = s - (c * CMP_STRIDE + CMP_BLOCK - 1)
        cmp_ref[0, t] = jnp.where((dist >= 0) & (c < n_cmp), lookup(jnp.maximum(dist, 0)), NEG)


def _bias_tables(rel_bias, seq):
    nq = seq // TILE
    nwin = WINDOW // TILE + 1
    nc = seq // CMP_STRIDE
    n_cmp = nc - CMP_BLOCK // CMP_STRIDE + 1
    return pl.pallas_call(
        functools.partial(_bias_kernel, n_cmp=n_cmp),
        grid=(NSA_HEADS,),
        in_specs=[pl.BlockSpec(memory_space=pltpu.SMEM)],
        out_specs=[pl.BlockSpec((1, nq + 1, TILE, TILE), lambda h: (h, 0, 0, 0)),
                   pl.BlockSpec((1, nwin + 1, TILE, TILE), lambda h: (h, 0, 0, 0)),
                   pl.BlockSpec((1, nq, nc, TILE), lambda h: (h, 0, 0, 0))],
        out_shape=[jax.ShapeDtypeStruct((NSA_HEADS, nq + 1, TILE, TILE), jnp.float32),
                   jax.ShapeDtypeStruct((NSA_HEADS, nwin + 1, TILE, TILE), jnp.float32),
                   jax.ShapeDtypeStruct((NSA_HEADS, nq, nc, TILE), jnp.float32)],
        compiler_params=_cparams(1), name="bias_tables",
    )(rel_bias.astype(jnp.float32))


def _gelu_tanh(x):
    return x * (0.5 * (1.0 + jnp.tanh(math.sqrt(2.0 / math.pi) * (x + 0.044715 * (x * x * x)))))


def _compress_kernel(x_ref, pos_ref, w1k_ref, w1v_ref, w2_ref, o_ref, wbd):
    kv = pl.program_id(0)
    nc = o_ref.shape[2]
    half = CMP_BLOCK // 2
    dh = NSA_HEAD_DIM
    bf16 = jnp.bfloat16

    @pl.when(pl.program_id(1) == 0)
    def _():
        def build(w1_ref):
            zero = jnp.zeros((dh, CMP_HIDDEN), bf16)
            for a in range(2):
                for l in range(half):
                    r0 = (a * half + l) * dh
                    wl = w1_ref[r0:r0 + dh, :].astype(bf16)
                    wbd[a, l * NSA_KV:(l + 1) * NSA_KV, :] = jnp.concatenate(
                        [jnp.concatenate([wl, zero], axis=1), jnp.concatenate([zero, wl], axis=1)], axis=0)
        pl.when(kv == 0)(functools.partial(build, w1k_ref))
        pl.when(kv == 1)(functools.partial(build, w1v_ref))

    r = jnp.concatenate([x_ref[pl.ds(l, nc, stride=CMP_STRIDE), :] for l in range(CMP_STRIDE)], axis=1)
    a = jnp.dot((r + pos_ref[0, 0:1, :]).astype(bf16), wbd[0], preferred_element_type=jnp.float32)
    b = jnp.dot((r + pos_ref[0, 1:2, :]).astype(bf16), wbd[1], preferred_element_type=jnp.float32)
    hid = a + pltpu.roll(b, nc - 1, 0)
    o_ref[0, 0] = jnp.dot(_gelu_tanh(hid).astype(bf16), w2_ref[0], preferred_element_type=jnp.float32)


def _compress(pf, pos2, w1_k, w1_v, w2p, bsz, seq):
    nc = seq // CMP_STRIDE
    width = CMP_STRIDE * NSA_KV
    hid = NSA_GROUPS * CMP_HIDDEN
    assert CMP_BLOCK == 2 * CMP_STRIDE
    return pl.pallas_call(
        _compress_kernel,
        grid=(2, bsz),
        in_specs=[pl.BlockSpec((seq, NSA_KV), lambda k, b: (b, PF_KC // NSA_KV + k)),
                  pl.BlockSpec((1, 2, width), lambda k, b: (k, 0, 0)),
                  pl.BlockSpec(w1_k.shape, lambda k, b: (0, 0)),
                  pl.BlockSpec(w1_v.shape, lambda k, b: (0, 0)),
                  pl.BlockSpec((1, hid, LANES), lambda k, b: (k, 0, 0))],
        out_specs=pl.BlockSpec((1, 1, nc, LANES), lambda k, b: (b, k, 0, 0)),
        out_shape=jax.ShapeDtypeStruct((bsz, 2, nc, LANES), jnp.float32),
        scratch_shapes=[pltpu.VMEM((2, width, hid), jnp.bfloat16)],
        compiler_params=_cparams(2), name="compress",
    )(pf, pos2, w1_k, w1_v, w2p)


def _prep_compress_weights(pos_k, pos_v, w2_k, w2_v):
    eye = jnp.eye(NSA_GROUPS, dtype=jnp.float32)
    half = CMP_BLOCK // 2

    def w2_both(w2):
        return jnp.einsum('jd,gh->gjhd', w2, eye).reshape(NSA_GROUPS * CMP_HIDDEN, NSA_GROUPS * NSA_HEAD_DIM)

    def pos_both(pos):
        p = pos.reshape(2, half, 1, NSA_HEAD_DIM)
        return jnp.broadcast_to(p, (2, half, NSA_GROUPS, NSA_HEAD_DIM)).reshape(2, half * NSA_KV)

    pos2 = jnp.stack([pos_both(pos_k), pos_both(pos_v)]).astype(jnp.float32)
    w2p = jnp.stack([w2_both(w2_k), w2_both(w2_v)]).astype(jnp.bfloat16)
    return pos2, w2p


def _pair_lanes(x, g):
    sw = pltpu.roll(x, LANES // 2, 1)
    lane = lax.broadcasted_iota(jnp.int32, x.shape, 1)
    own = (lane < LANES // 2) == (g == 0)
    return jnp.where(own, x, sw)


def _group_rows_t(x, g):
    xt = x.T
    half = LANES // 2
    return jnp.where(g == 0, xt[0:half, :], xt[half:LANES, :])


def _nsa_step_kernel(q_ref, qn_ref, ks_ref, kw_ref, vs_ref, vw_ref, kcvc_ref, bc_ref, bsel_ref, bwin_ref,
                     small_ref, za_ref, o_ref,
                     ksa, kwp, vst, vwt, kcs, vct, qa_s, qp_s, ocmp_s, sbuf, *, n_cmp, n_sel, n_top):
    g = pl.program_id(0)
    step_id = pl.program_id(2)
    tq = TILE
    dh = NSA_HEAD_DIM
    seq = ks_ref.shape[0]
    nq = seq // TILE
    nc = kcvc_ref.shape[2]
    n_wt = bwin_ref.shape[1] - 1
    heads = range(NSA_HPG)
    f32 = jnp.float32
    bf16 = jnp.bfloat16
    nt_dims = (((1,), (1,)), ((), ()))
    gk = SEL_GROUP * TILE

    def select_tile(src_ref, r0, tile, slot):
        q32 = src_ref[r0:r0 + tq, :].astype(f32)
        low_half = lax.broadcasted_iota(jnp.int32, (tq, LANES), 1) < NSA_HEAD_DIM
        qh = []
        for hh in heads:
            blk = q32[:, (hh // 2) * LANES:(hh // 2 + 1) * LANES]
            if hh % 2:
                blk = pltpu.roll(blk, LANES // 2, 1)
            qh.append(jnp.where(low_half, blk, 0.0).astype(bf16))
        for hh in heads:
            qp_s[slot, hh * tq:(hh + 1) * tq, :] = qh[hh]
        q_all = qp_s[slot]
        s_c = lax.dot_general(kcs[...], q_all, nt_dims, preferred_element_type=f32)
        bias_c = jnp.concatenate([bc_ref[hh, tile] for hh in heads], axis=1)
        valid = bias_c > 0.5 * NEG
        s_c = s_c + bias_c
        e = jnp.where(valid, jnp.exp2(s_c - jnp.max(s_c, axis=0, keepdims=True)), 0.0)
        den = jnp.maximum(jnp.sum(e, axis=0, keepdims=True), 1e-30)
        p_c = e * (1.0 / den)
        psum = sum(p_c[:, hh * tq:(hh + 1) * tq] for hh in heads)
        sj = lax.broadcasted_iota(jnp.int32, (n_sel, nc), 0)
        ci = lax.broadcasted_iota(jnp.int32, (n_sel, nc), 1)
        overlap = ((ci * CMP_STRIDE < (sj + 1) * SEL_BLOCK) & (ci * CMP_STRIDE + CMP_BLOCK > sj * SEL_BLOCK)
                   & (ci < n_cmp)).astype(bf16)
        imp_t = sum(jnp.dot(overlap, part, preferred_element_type=f32)
                    for part in _split_bf16(psum, 3))
        ocmp_s[slot] = jnp.dot(vct[...], p_c.astype(bf16), preferred_element_type=f32)
        blk = lax.broadcasted_iota(jnp.int32, (n_sel, tq), 0)
        cur = (tile * tq + lax.broadcasted_iota(jnp.int32, (n_sel, tq), 1)) >> 6
        forced = (blk == 0) | (blk == cur) | (blk == cur - 1)
        imp_t = jnp.where(forced, FORCE, jnp.where(blk > cur, -FORCE, imp_t))
        rank = jnp.zeros((n_sel, tq), jnp.int32)
        for j in range(n_sel):
            other = imp_t[j:j + 1, :]
            ahead = (other > imp_t) | ((other == imp_t) & (blk > j))
            rank = rank + ahead.astype(jnp.int32)
        selb = jnp.where(rank < n_top, 0.0, NEG)
        pieces = [jnp.zeros((SEL_LANE0, tq), f32), selb]
        if LANES - SEL_LANE0 - n_sel > 0:
            pieces.append(jnp.zeros((LANES - SEL_LANE0 - n_sel, tq), f32))
        selb_r = jnp.concatenate(pieces, axis=0).T.astype(bf16)
        for hh in heads:
            qa_s[slot, hh * tq:(hh + 1) * tq, :] = qh[hh] + selb_r

    @pl.when(step_id == 0)
    def _():
        row = lax.broadcasted_iota(jnp.int32, (seq, LANES), 0)
        lane = lax.broadcasted_iota(jnp.int32, (seq, LANES), 1)
        onehot = (lane - SEL_LANE0) == (row >> 6)
        own = lane < NSA_HEAD_DIM
        ks_g = jnp.where(own, _pair_lanes(ks_ref[...].astype(f32), g), 0.0)
        ksa[...] = jnp.where(onehot, 1.0, ks_g).astype(bf16)
        pad = (n_wt - 1) * TILE
        kwp[0:pad, :] = jnp.zeros((pad, LANES), bf16)
        kwp[pad:pad + seq, :] = jnp.where(own, _pair_lanes(kw_ref[...].astype(f32), g), 0.0).astype(bf16)
        for kt in range(n_wt - 1):
            vwt[kt] = jnp.zeros(vwt.shape[1:], bf16)
        extra = vst.shape[1] - dh
        ones_row = (lax.broadcasted_iota(jnp.int32, (extra, TILE), 0) == 0).astype(bf16)
        for kt in range(nq):
            rows = slice(kt * TILE, (kt + 1) * TILE)
            vst[kt, 0:dh, :] = _group_rows_t(vs_ref[rows, :].astype(f32), g).astype(bf16)
            vst[kt, dh:dh + extra, :] = ones_row
            vwt[kt + n_wt - 1, 0:dh, :] = _group_rows_t(vw_ref[rows, :].astype(f32), g).astype(bf16)
            vwt[kt + n_wt - 1, dh:dh + extra, :] = ones_row
        kcs[...] = _pair_lanes(kcvc_ref[0, 0], g).astype(bf16)
        for ct in range(nc // TILE):
            rows = slice(ct * TILE, (ct + 1) * TILE)
            vct[:, rows] = _group_rows_t(kcvc_ref[0, 1, rows, :], g).astype(bf16)
        select_tile(q_ref, 0, 0, 0)

    tps = NSA_TILES_PER_STEP

    def one_tile(n_groups, slot):
        qt = step_id * tps + slot
        rows = slice(slot * tq, (slot + 1) * tq)
        q_all = qp_s[slot]
        q_aug = qa_s[slot]
        kw_rows = kwp[pl.ds(pl.multiple_of(qt * TILE, TILE), n_wt * TILE), :]
        s_w = lax.dot_general(kw_rows, q_all, nt_dims, preferred_element_type=f32)

        def sel_scores(gi):
            s = lax.dot_general(ksa[gi * gk:(gi + 1) * gk, :], q_aug, nt_dims, preferred_element_type=f32)
            tiles = []
            for t in range(SEL_GROUP):
                dt = qt - (gi * SEL_GROUP + t)
                idx = jnp.where(dt >= 0, dt, nq)
                tiles.append(jnp.concatenate([bsel_ref[hh, idx] for hh in heads], axis=1))
            sbuf[gi % 2] = s + jnp.concatenate(tiles, axis=0)

        sel_scores(0)
        if slot + 1 < tps:
            select_tile(q_ref, (slot + 1) * tq, qt + 1, slot + 1)
        else:
            select_tile(qn_ref, 0, jnp.minimum(qt + 1, nq - 1), 0)

        w_tiles = []
        for t in range(n_wt):
            dt = n_wt - 1 - t
            idx = jnp.where(qt >= dt, dt, n_wt)
            w_tiles.append(jnp.concatenate([bwin_ref[hh, idx] for hh in heads], axis=1))
        s_w = s_w + jnp.concatenate(w_tiles, axis=0)
        p_w = jnp.exp2(s_w - jnp.max(s_w, axis=0, keepdims=True))
        v_w = jnp.concatenate([vwt[qt + t] for t in range(n_wt)], axis=1)
        acc_w = jnp.dot(v_w, p_w.astype(bf16), preferred_element_type=f32)
        o_win = acc_w[0:dh, :] * (1.0 / acc_w[dh:dh + 1, :])

        m = jnp.full((1, NSA_HPG * tq), NEG, f32)
        acc = jnp.zeros((vst.shape[1], NSA_HPG * tq), f32)
        for gi in range(n_groups):
            if gi + 1 < n_groups:
                sel_scores(gi + 1)
            s = sbuf[gi % 2]
            m_new = jnp.maximum(m, jnp.max(s, axis=0, keepdims=True))
            alpha = jnp.exp2(m - m_new)
            p = jnp.exp2(s - m_new)
            v_t = jnp.concatenate([vst[gi * SEL_GROUP + t] for t in range(SEL_GROUP)], axis=1)
            acc = alpha * acc + jnp.dot(v_t, p.astype(bf16), preferred_element_type=f32)
            m = m_new
        o_sel = acc[0:dh, :] * (1.0 / acc[dh:dh + 1, :])
        o_cmp = ocmp_s[slot]

        gates_t = jax.nn.sigmoid(small_ref[rows, :]).T
        mixed = []
        for hh in heads:
            def gate(br):
                r = br * NSA_HEADS + hh
                return jnp.where(g == 0, gates_t[r:r + 1, :], gates_t[r + NSA_HPG:r + NSA_HPG + 1, :])
            cols = slice(hh * tq, (hh + 1) * tq)
            mixed.append(gate(0) * o_cmp[:, cols] + gate(1) * o_sel[:, cols] + gate(2) * o_win[:, cols])
        o = jnp.concatenate([jnp.concatenate(mixed[2 * j:2 * j + 2], axis=0).T for j in range(NSA_HPG // 2)],
                            axis=1)
        z = za_ref[rows, :]
        o_ref[rows, :] = (o * (z * jax.nn.sigmoid(z))).astype(o_ref.dtype)

    def step(n_groups):
        for slot in range(tps):
            one_tile(n_groups, slot)

    for n_groups in range(1, nq // SEL_GROUP + 1):
        pl.when(step_id // (SEL_GROUP // tps) == n_groups - 1)(functools.partial(step, n_groups))


def _nsa(pb, pf, kcvc, bias_sel, bias_win, bias_cmp, bsz, seq):
    nq = seq // TILE
    nc = seq // CMP_STRIDE
    n_cmp = nc - CMP_BLOCK // CMP_STRIDE + 1
    n_sel = seq // SEL_BLOCK
    n_top = min(SEL_TOPK, n_sel)
    nwin = bias_win.shape[1]
    tps = NSA_TILES_PER_STEP
    assert nq % SEL_GROUP == 0 and nc % TILE == 0 and bias_sel.shape[1] == nq + 1
    assert tps >= 2 and SEL_GROUP % tps == 0
    ns = nq // tps
    rows = tps * TILE
    gw = NSA_HPG * NSA_HEAD_DIM
    kern = functools.partial(_nsa_step_kernel, n_cmp=n_cmp, n_sel=n_sel, n_top=n_top)
    return pl.pallas_call(
        kern,
        grid=(NSA_GROUPS, bsz, ns),
        in_specs=[
            pl.BlockSpec((rows, gw), lambda g, b, t: (b * ns + t, PB_QP // gw + g)),
            pl.BlockSpec((TILE, gw), lambda g, b, t: (b * nq + jnp.minimum((t + 1) * tps, nq - 1),
                                                      PB_QP // gw + g)),
            pl.BlockSpec((seq, LANES), lambda g, b, t: (b, PB_KS // LANES)),
            pl.BlockSpec((seq, LANES), lambda g, b, t: (b, PB_KW // LANES)),
            pl.BlockSpec((seq, LANES), lambda g, b, t: (b, PB_VS // LANES)),
            pl.BlockSpec((seq, LANES), lambda g, b, t: (b, PB_VW // LANES)),
            pl.BlockSpec((1, 2, nc, LANES), lambda g, b, t: (b, 0, 0, 0)),
            pl.BlockSpec((NSA_HPG, nq, nc, TILE), lambda g, b, t: (g, 0, 0, 0)),
            pl.BlockSpec((NSA_HPG, nq + 1, TILE, TILE), lambda g, b, t: (g, 0, 0, 0)),
            pl.BlockSpec((NSA_HPG, nwin, TILE, TILE), lambda g, b, t: (g, 0, 0, 0)),
            pl.BlockSpec((rows, LANES), lambda g, b, t: (b * ns + t, PF_SMALL // LANES)),
            pl.BlockSpec((rows, gw), lambda g, b, t: (b * ns + t, PF_ZA // gw + g)),
        ],
        out_specs=pl.BlockSpec((rows, gw), lambda g, b, t: (b * ns + t, g)),
        out_shape=jax.ShapeDtypeStruct((bsz * seq, NSA_WIDTH), jnp.bfloat16),
        scratch_shapes=[
            pltpu.VMEM((seq, LANES), jnp.bfloat16),
            pltpu.VMEM((seq + (nwin - 2) * TILE, LANES), jnp.bfloat16),
            pltpu.VMEM((nq, NSA_V_ROWS, TILE), jnp.bfloat16),
            pltpu.VMEM((nq + nwin - 2, NSA_V_ROWS, TILE), jnp.bfloat16),
            pltpu.VMEM((nc, LANES), jnp.bfloat16),
            pltpu.VMEM((NSA_HEAD_DIM, nc), jnp.bfloat16),
            pltpu.VMEM((tps, NSA_HPG * TILE, LANES), jnp.bfloat16),
            pltpu.VMEM((tps, NSA_HPG * TILE, LANES), jnp.bfloat16),
            pltpu.VMEM((tps, NSA_HEAD_DIM, NSA_HPG * TILE), jnp.float32),
            pltpu.VMEM((2, SEL_GROUP * TILE, NSA_HPG * TILE), jnp.float32),
        ],
        compiler_params=_cparams(3), name="nsa",
    )(pb, pb, pb, pb, pb, pb, kcvc, bias_cmp, bias_sel, bias_win, pf, pf)


def _split_bf16(a, n):
    parts = []
    for _ in range(n - 1):
        hi = a.astype(jnp.bfloat16)
        parts.append(hi)
        a = a - hi.astype(jnp.float32)
    parts.append(a.astype(jnp.bfloat16))
    return parts


def _dot3(a, b):
    ah, al = _split_bf16(a, 2)
    bh, bl = _split_bf16(b, 2)
    f32 = jnp.float32
    return (jnp.dot(ah, bh, preferred_element_type=f32) + jnp.dot(al, bh, preferred_element_type=f32)
            + jnp.dot(ah, bl, preferred_element_type=f32))


def _softplus(x):
    return jnp.maximum(x, 0.0) + jnp.log(1.0 + jnp.exp(-jnp.abs(x)))


def _dn_kernel(scal_ref, q_ref, k_ref, v_ref, small_ref, z_ref, cw_ref, nw_ref, o_ref,
               qn, kn, vn, bet, gl, mm, nn, qq, oo, dd, ss, *, seq):
    h = pl.program_id(1)
    n_rows = q_ref.shape[0]
    nb = n_rows // seq
    c = DN_CHUNK
    n_chunks = seq // c
    d = DN_HEAD_DIM

    head = 8

    def conv_body(x_ref, which):
        n = n_rows - head
        y = x_ref[pl.ds(head, n), :] * cw_ref[which, DN_CONV - 1:DN_CONV, :]
        for j in range(DN_CONV - 1):
            y = y + x_ref[pl.ds(head - (DN_CONV - 1 - j), n), :] * cw_ref[which, j:j + 1, :]
        return y

    def conv_head(x_ref, which, r0):
        x = x_ref[pl.ds(r0, head), :]
        rowi = lax.broadcasted_iota(jnp.int32, (head, d), 0)
        y = x * cw_ref[which, DN_CONV - 1:DN_CONV, :]
        for j in range(DN_CONV - 1):
            sh = DN_CONV - 1 - j
            y = y + jnp.where(rowi >= sh, pltpu.roll(x, sh, 0), 0.0) * cw_ref[which, j:j + 1, :]
        return y

    def silu(y):
        return y * jax.nn.sigmoid(y)

    def l2n(t):
        return t * lax.rsqrt(jnp.sum(t * t, axis=-1, keepdims=True) + 1e-6)

    finish = (lambda y: l2n(silu(y)) * (d ** -0.5), lambda y: l2n(silu(y)), silu)
    for which, (x_ref, dst) in enumerate(((q_ref, qn), (k_ref, kn), (v_ref, vn))):
        dst[pl.ds(head, n_rows - head), :] = finish[which](conv_body(x_ref, which))
        for b in range(nb):
            dst[pl.ds(b * seq, head), :] = finish[which](conv_head(x_ref, which, b * seq))
    small = small_ref[...]
    lane = lax.broadcasted_iota(jnp.int32, small.shape, 1)
    beta_in = jnp.sum(jnp.where(lane == SMALL_BETA + h, small, 0.0), axis=-1, keepdims=True)
    a_in = jnp.sum(jnp.where(lane == SMALL_A + h, small, 0.0), axis=-1, keepdims=True)
    bet[...] = jnp.broadcast_to(jax.nn.sigmoid(beta_in), (n_rows, d))
    gl[...] = jnp.broadcast_to(-jnp.exp(scal_ref[0, h]) * _softplus(a_in + scal_ref[1, h]), (n_rows, d))

    ri = lax.broadcasted_iota(jnp.int32, (c, c), 0)
    cj = lax.broadcasted_iota(jnp.int32, (c, c), 1)
    incl = ri >= cj
    strict = ri > cj
    tril = incl.astype(jnp.float32)
    eye = (ri == cj).astype(jnp.float32)

    tril_b = tril.astype(jnp.bfloat16)
    bs = DN_INV_BLOCK
    sh = bs.bit_length() - 1
    same_diag = (ri >> sh) == (cj >> sh)
    level_masks = []
    while (1 << sh) < c:
        level_masks.append(((ri >> (sh + 1)) == (cj >> (sh + 1))) & ((ri >> sh) > (cj >> sh)))
        sh += 1

    bf16 = jnp.bfloat16
    nt_dims = (((1,), (1,)), ((), ()))

    def dot1(a, b):
        return jnp.dot(a.astype(bf16), b.astype(bf16), preferred_element_type=jnp.float32)

    def chunk_prep(it, carry):
        ids = [it * DN_PREP_UNROLL + cc for cc in range(DN_PREP_UNROLL)]
        rows = [pl.ds(pl.multiple_of(i * c, c), c) for i in ids]
        ks = [kn[r, :] for r in rows]
        betas = [bet[r, :] for r in rows]
        gcbs = [sum(jnp.dot(tril_b, part, preferred_element_type=jnp.float32)
                    for part in _split_bf16(gl[r, :], DN_DECAY_PARTS)) for r in rows]
        kbs = [k * beta for k, beta in zip(ks, betas)]
        kbfs = [k.astype(bf16) for k in ks]
        a_kks = [lax.dot_general(kb.astype(bf16), kbf, nt_dims, preferred_element_type=jnp.float32)
                 for kb, kbf in zip(kbs, kbfs)]
        decays = []
        for gcb in gcbs:
            gct = jnp.concatenate([gcb, gcb], axis=0).T
            diff = gcb[:, 0:c] - gct[0:c, 0:c]
            decays.append(jnp.where(incl, jnp.exp(jnp.where(incl, diff, 0.0)), 0.0))
        lows = [jnp.where(strict, a * dec, 0.0) for a, dec in zip(a_kks, decays)]
        pws = [jnp.where(same_diag, -low, 0.0) for low in lows]
        es = list(pws)
        for _ in range(max(1, (bs - 1).bit_length()) - 1):
            pws = [dot1(pw, pw) for pw in pws]
            es = [e + pw + dot1(e, pw) for e, pw in zip(es, pws)]
        for below in level_masks:
            offs = [jnp.where(below, low, 0.0) for low in lows]
            xs = [off + dot1(e, off) for e, off in zip(es, offs)]
            es = [e - (x + dot1(x, e)) for e, x in zip(es, xs)]
        egcs = [jnp.exp(gcb) for gcb in gcbs]
        rhss = [jnp.concatenate([vn[r, :] * beta, kb * egc], axis=1)
                for r, beta, kb, egc in zip(rows, betas, kbs, egcs)]
        uws = [rhs + _dot3(e, rhs) for e, rhs in zip(es, rhss)]
        qs = [qn[r, :] for r in rows]
        a_qks = [lax.dot_general(q.astype(bf16), kbf, nt_dims, preferred_element_type=jnp.float32) * dec
                 for q, kbf, dec in zip(qs, kbfs, decays)]
        g_lasts = [gcb[c - 1:c, :] for gcb in gcbs]
        kdec_ts = []
        for k, gcb, g_last in zip(ks, gcbs, g_lasts):
            kdec = k * jnp.exp(g_last - gcb)
            kdec_ts.append(jnp.concatenate([kdec, jnp.zeros_like(kdec)], axis=0).T[:, 0:c].astype(bf16))
        uwbs = [uw.astype(bf16) for uw in uws]
        nms = [jnp.dot(kt, uwb, preferred_element_type=jnp.float32) for kt, uwb in zip(kdec_ts, uwbs)]
        oqs = [jnp.dot(a.astype(bf16), uwb, preferred_element_type=jnp.float32) for a, uwb in zip(a_qks, uwbs)]
        for i, r, nm, oq, q, egc, g_last in zip(ids, rows, nms, oqs, qs, egcs, g_lasts):
            m0 = pl.ds(pl.multiple_of(i * d, d), d)
            nn[m0, :] = nm[:, 0:d]
            mm[m0, :] = nm[:, d:2 * d].astype(bf16)
            oo[r, :] = oq[:, 0:d]
            qq[r, :] = (q * egc - oq[:, d:2 * d]).astype(bf16)
            dd[pl.ds(pl.multiple_of(i * 8, 8), 8), :] = jnp.broadcast_to(jnp.exp(g_last), (8, d))
        return carry

    lax.fori_loop(0, nb * n_chunks // DN_PREP_UNROLL, chunk_prep, 0)

    def chunk_scan(i, states):
        ids = [b * n_chunks + i for b in range(nb)]
        blocks = [pl.ds(pl.multiple_of(j * d, d), d) for j in ids]
        sbs = [s.astype(bf16) for s in states]
        for blk, sb in zip(blocks, sbs):
            ss[blk, :] = sb
        prods = [jnp.dot(mm[blk, :], sb, preferred_element_type=jnp.float32) for blk, sb in zip(blocks, sbs)]
        return tuple(s * dd[pl.ds(pl.multiple_of(j * 8, 8), 1), :] - pr + nn[blk, :]
                     for s, j, blk, pr in zip(states, ids, blocks, prods))

    lax.fori_loop(0, n_chunks, chunk_scan, tuple(jnp.zeros((d, d), jnp.float32) for _ in range(nb)))

    nw = nw_ref[...]

    def chunk_out(it, carry):
        ids = [it * DN_OUT_UNROLL + cc for cc in range(DN_OUT_UNROLL)]
        rows = [pl.ds(pl.multiple_of(i * c, c), c) for i in ids]
        outs = [jnp.dot(qq[r, :], ss[pl.ds(pl.multiple_of(i * d, d), d), :], preferred_element_type=jnp.float32)
                + oo[r, :] for i, r in zip(ids, rows)]
        for r, o in zip(rows, outs):
            o = o * lax.rsqrt(jnp.mean(o * o, axis=-1, keepdims=True) + 1e-6) * nw
            z = z_ref[r, :]
            o_ref[r, :] = (o * (z * jax.nn.sigmoid(z))).astype(o_ref.dtype)
        return carry

    lax.fori_loop(0, nb * n_chunks // DN_OUT_UNROLL, chunk_out, 0)


def _deltanet(pf, conv_w, a_log, dt_bias, norm_w, bsz, seq):
    d = DN_HEAD_DIM
    nb = DN_BATCHES if bsz % DN_BATCHES == 0 else 1
    rows = nb * seq
    n_chunks = rows // DN_CHUNK
    qkv0 = PF_QKVB // d
    scal = jnp.stack([a_log, dt_bias]).astype(jnp.float32)
    f32 = jnp.float32
    bf16 = jnp.bfloat16
    cw4 = conv_w.astype(f32).reshape(DN_CONV, 3, DN_HEADS, d).transpose(2, 1, 0, 3)
    assert 2 * DN_CHUNK == d and n_chunks % DN_PREP_UNROLL == 0 and n_chunks % DN_OUT_UNROLL == 0
    return pl.pallas_call(
        functools.partial(_dn_kernel, seq=seq),
        grid=(bsz // nb, DN_HEADS),
        in_specs=[
            pl.BlockSpec(memory_space=pltpu.SMEM),
            pl.BlockSpec((rows, d), lambda b, h: (b, qkv0 + h)),
            pl.BlockSpec((rows, d), lambda b, h: (b, qkv0 + DN_HEADS + h)),
            pl.BlockSpec((rows, d), lambda b, h: (b, qkv0 + 2 * DN_HEADS + h)),
            pl.BlockSpec((rows, LANES), lambda b, h: (b, PF_SMALL // LANES)),
            pl.BlockSpec((rows, d), lambda b, h: (b, PF_ZB // d + h)),
            pl.BlockSpec((None, 3, DN_CONV, d), lambda b, h: (h, 0, 0, 0)),
            pl.BlockSpec((1, d), lambda b, h: (0, 0)),
        ],
        out_specs=pl.BlockSpec((rows, d), lambda b, h: (b, h)),
        out_shape=jax.ShapeDtypeStruct((bsz * seq, DN_WIDTH), jnp.bfloat16),
        scratch_shapes=[
            pltpu.VMEM((rows, d), f32), pltpu.VMEM((rows, d), f32), pltpu.VMEM((rows, d), f32),
            pltpu.VMEM((rows, d), f32), pltpu.VMEM((rows, d), f32),
            pltpu.VMEM((n_chunks * d, d), bf16), pltpu.VMEM((n_chunks * d, d), f32),
            pltpu.VMEM((rows, d), bf16), pltpu.VMEM((rows, d), f32),
            pltpu.VMEM((n_chunks * 8, d), f32),
            pltpu.VMEM((n_chunks * d, d), bf16),
        ],
        compiler_params=_cparams(2), name="deltanet",
    )(scal, pf, pf, pf, pf, pf, cw4, norm_w.astype(f32).reshape(1, d))


def _out_kernel(oa_ref, ob_ref, gma_ref, gmb_ref, x_ref, p_ref, wa_ref, wb_ref, wo_ref, wpg_ref, wp_ref,
                lng_ref, lnb_ref, o_ref):
    f32 = jnp.float32
    bf16 = jnp.bfloat16
    tm = o_ref.shape[0]
    sub = tm // OUT_SUBTILES
    parts = [slice(i * sub, (i + 1) * sub) for i in range(OUT_SUBTILES)]
    y_a = [jnp.dot(oa_ref[r, :], wa_ref[...], preferred_element_type=f32) for r in parts]
    y_b = [jnp.dot(ob_ref[r, :], wb_ref[...], preferred_element_type=f32) for r in parts]
    pw = [jnp.dot(p_ref[r, :].astype(bf16), wp_ref[...], preferred_element_type=f32) for r in parts]
    mix = [(jax.nn.sigmoid(gma_ref[r, :]) * ya + jax.nn.sigmoid(gmb_ref[r, :]) * yb).astype(bf16)
           for r, ya, yb in zip(parts, y_a, y_b)]
    h = [DEEPNORM_ALPHA * x_ref[r, :] + jnp.dot(mx, wo_ref[...], preferred_element_type=f32)
         for r, mx in zip(parts, mix)]
    gate = [jnp.dot(hh.astype(bf16), wpg_ref[...], preferred_element_type=f32) for hh in h]
    for r, hh, gt, pp in zip(parts, h, gate, pw):
        hh = hh + jax.nn.sigmoid(gt) * pp
        mu = jnp.mean(hh, axis=-1, keepdims=True)
        hc = hh - mu
        var = jnp.mean(hc * hc, axis=-1, keepdims=True)
        o_ref[r, :] = (hc * lax.rsqrt(var + 1e-5) * lng_ref[...] + lnb_ref[...]).astype(o_ref.dtype)


def _out_block(o_a, o_b, pf, x2, p2, wa, wb, wo, wpg, wp, ln_g, ln_b, tm):
    t = x2.shape[0]
    bf = jnp.bfloat16

    def full(shape):
        return pl.BlockSpec(shape, lambda i: (0, 0))

    return pl.pallas_call(
        _out_kernel,
        grid=(t // tm,),
        in_specs=[
            pl.BlockSpec((tm, NSA_WIDTH), lambda i: (i, 0)),
            pl.BlockSpec((tm, DN_WIDTH), lambda i: (i, 0)),
            pl.BlockSpec((tm, D_MODEL), lambda i: (i, PF_GM // D_MODEL)),
            pl.BlockSpec((tm, D_MODEL), lambda i: (i, PF_GM // D_MODEL + 1)),
            pl.BlockSpec((tm, D_MODEL), lambda i: (i, 0)),
            pl.BlockSpec((tm, PLE_DIM), lambda i: (i, 0)),
            full((NSA_WIDTH, D_MODEL)), full((DN_WIDTH, D_MODEL)), full((D_MODEL, D_MODEL)),
            full((D_MODEL, D_MODEL)), full((PLE_DIM, D_MODEL)), full((1, D_MODEL)), full((1, D_MODEL)),
        ],
        out_specs=pl.BlockSpec((tm, D_MODEL), lambda i: (i, 0)),
        out_shape=jax.ShapeDtypeStruct((t, D_MODEL), x2.dtype),
        compiler_params=_cparams(1), name="out_block",
    )(o_a, o_b, pf, pf, x2, p2, wa.astype(bf), wb.astype(bf), wo.astype(bf), wpg.astype(bf), wp.astype(bf),
      ln_g.astype(jnp.float32).reshape(1, D_MODEL), ln_b.astype(jnp.float32).reshape(1, D_MODEL))


def _layer(x, p, w_in, pos_k, pos_v, w1_k, w2_k, w1_v, w2_v, bias_tabs, conv_w, a_log, dt_bias, norm_w,
           w_a, w_b, w_o, w_ple, w_pg, ln_g, ln_b):
    bsz, seq, _ = x.shape
    t = bsz * seq
    x2 = x.reshape(t, D_MODEL)
    wb16, wf16 = _prep_w_in(w_in)
    pb, pf = _proj(x2, wb16, wf16, PROJ_TM if t % PROJ_TM == 0 else seq)

    pos2, w2p = _prep_compress_weights(pos_k, pos_v, w2_k, w2_v)
    kcvc = _compress(pf, pos2, w1_k, w1_v, w2p, bsz, seq)

    bias_sel, bias_win, bias_cmp = bias_tabs
    o_a = _nsa(pb, pf, kcvc, bias_sel, bias_win, bias_cmp, bsz, seq)
    o_b = _deltanet(pf, conv_w, a_log, dt_bias, norm_w, bsz, seq)
    out = _out_block(o_a, o_b, pf, x2, p.reshape(t, PLE_DIM), w_a, w_b, w_o, w_pg, w_ple, ln_g, ln_b,
                     OUT_TM if t % OUT_TM == 0 else seq)
    return out.reshape(bsz, seq, D_MODEL)


def kernel(x, p, w_in, cmp_pos_k, cmp_pos_v, cmp_w1_k, cmp_w2_k, cmp_w1_v, cmp_w2_v, rel_bias, dn_conv_w,
           dn_a_log, dn_dt_bias, dn_norm_w, w_branch_a, w_branch_b, w_out, w_ple, w_ple_gate, ln_g, ln_b):
    depth = w_in.shape[0]
    bias_tabs = _bias_tables(rel_bias, x.shape[1])
    for i in range(depth):
        x = _layer(x, p[i], w_in[i], cmp_pos_k[i], cmp_pos_v[i], cmp_w1_k[i], cmp_w2_k[i], cmp_w1_v[i],
                   cmp_w2_v[i], bias_tabs, dn_conv_w[i], dn_a_log[i], dn_dt_bias[i], dn_norm_w[i],
                   w_branch_a[i], w_branch_b[i], w_out[i], w_ple[i], w_ple_gate[i], ln_g[i], ln_b[i])
    return x
```

```python
import functools
import math

import numpy as np
import jax
import jax.numpy as jnp
from jax import lax
from jax.experimental import pallas as pl
from jax.experimental.pallas import tpu as pltpu

D_MODEL = 1024
PLE_DIM = 256
NSA_HEADS = 8
NSA_GROUPS = 2
NSA_HPG = NSA_HEADS // NSA_GROUPS
NSA_HEAD_DIM = 64
NSA_WIDTH = NSA_HEADS * NSA_HEAD_DIM
NSA_KV = NSA_GROUPS * NSA_HEAD_DIM
CMP_BLOCK = 32
CMP_STRIDE = 16
CMP_HIDDEN = 256
SEL_BLOCK = 64
SEL_TOPK = 8
WINDOW = 512
DN_HEADS = 4
DN_HEAD_DIM = 128
DN_WIDTH = DN_HEADS * DN_HEAD_DIM
DN_CONV = 4
DN_CHUNK = 64
NUM_BUCKETS = 32
REL_MAX_DIST = 1024
DEEPNORM_ALPHA = 2.0 ** 0.25
NEG = -1e30
FORCE = 1e6
LOG2E = 1.4426950408889634

LANES = 128
TILE = 128
SEL_LANE0 = 64
SEL_GROUP = 4
NSA_TILES_PER_STEP = 4
NSA_V_ROWS = NSA_HEAD_DIM + 16
VMEM_LIMIT = 56 * 1024 * 1024
PROJ_TM = 512
PROJ_TN = 1024
OUT_TM = 512
OUT_SUBTILES = 2
DN_INV_BLOCK = 16
DN_BATCHES = 2
DN_OUT_UNROLL = 8
DN_DECAY_PARTS = 2
DN_PREP_UNROLL = 32

HIGHEST = lax.Precision.HIGHEST

PB_QP = 0
PB_KS = PB_QP + NSA_WIDTH
PB_KW = PB_KS + NSA_KV
PB_VS = PB_KW + NSA_KV
PB_VW = PB_VS + NSA_KV
PB_WIDTH = PB_VW + NSA_KV
PF_GM = 0
PF_QKVB = PF_GM + 2 * D_MODEL
PF_ZA = PF_QKVB + 3 * DN_WIDTH
PF_ZB = PF_ZA + NSA_WIDTH
PF_KC = PF_ZB + DN_WIDTH
PF_VC = PF_KC + NSA_KV
PF_SMALL = PF_VC + NSA_KV
PF_WIDTH = PF_SMALL + LANES
SMALL_BETA = 3 * NSA_HEADS
SMALL_A = SMALL_BETA + DN_HEADS


def _bucket_thresholds():
    max_exact = NUM_BUCKETS // 2
    span = NUM_BUCKETS - max_exact
    ratio = REL_MAX_DIST // max_exact
    thr = list(range(1, max_exact + 1))
    for k in range(1, span):
        n = max_exact
        while n ** span < max_exact ** span * ratio ** k:
            n += 1
        thr.append(n)
    return tuple(thr)


_THR = _bucket_thresholds()


def _cparams(n_axes):
    return pltpu.CompilerParams(dimension_semantics=("arbitrary",) * n_axes, vmem_limit_bytes=VMEM_LIMIT)


def _proj_kernel(x_ref, wb_ref, wf_ref, pb_ref, pf_ref):
    xb = x_ref[...].astype(jnp.bfloat16)
    pb_ref[...] = jnp.dot(xb, wb_ref[...], preferred_element_type=jnp.float32).astype(pb_ref.dtype)
    n = pf_ref.shape[1]
    for c0 in range(0, n, PROJ_TN):
        c1 = min(c0 + PROJ_TN, n)
        pf_ref[:, c0:c1] = jnp.dot(xb, wf_ref[:, c0:c1], preferred_element_type=jnp.float32)


def _proj(x2, wb, wf, tm):
    t, d = x2.shape
    resident = dict(pipeline_mode=pl.Buffered(1))
    return pl.pallas_call(
        _proj_kernel,
        grid=(t // tm,),
        in_specs=[pl.BlockSpec((tm, d), lambda i: (i, 0)),
                  pl.BlockSpec((d, PB_WIDTH), lambda i: (0, 0), **resident),
                  pl.BlockSpec((d, PF_WIDTH), lambda i: (0, 0), **resident)],
        out_specs=[pl.BlockSpec((tm, PB_WIDTH), lambda i: (i, 0)),
                   pl.BlockSpec((tm, PF_WIDTH), lambda i: (i, 0))],
        out_shape=[jax.ShapeDtypeStruct((t, PB_WIDTH), jnp.bfloat16),
                   jax.ShapeDtypeStruct((t, PF_WIDTH), jnp.float32)],
        compiler_params=_cparams(1), name="proj",
    )(x2, wb, wf)


def _prep_w_in(w):
    d = w.shape[0]
    o = 0
    wq = w[:, o:o + NSA_WIDTH]; o += NSA_WIDTH
    wkv = w[:, o:o + 6 * NSA_KV]; o += 6 * NSA_KV
    wg = w[:, o:o + 3 * NSA_HEADS]; o += 3 * NSA_HEADS
    wza = w[:, o:o + NSA_WIDTH]; o += NSA_WIDTH
    wqkvb = w[:, o:o + 3 * DN_WIDTH]; o += 3 * DN_WIDTH
    wbeta_a = w[:, o:o + 2 * DN_HEADS]; o += 2 * DN_HEADS
    wzb = w[:, o:o + DN_WIDTH]; o += DN_WIDTH
    wgm = w[:, o:o + 2 * D_MODEL]
    wkcvc, wks, wvs, wkw, wvw = (wkv[:, 0:2 * NSA_KV], wkv[:, 2 * NSA_KV:3 * NSA_KV], wkv[:, 3 * NSA_KV:4 * NSA_KV],
                                 wkv[:, 4 * NSA_KV:5 * NSA_KV], wkv[:, 5 * NSA_KV:6 * NSA_KV])
    wb = jnp.concatenate([wq * (NSA_HEAD_DIM ** -0.5 * LOG2E), wks, wkw, wvs, wvw], axis=1).astype(jnp.bfloat16)
    pad = jnp.zeros((d, LANES - 3 * NSA_HEADS - 2 * DN_HEADS), w.dtype)
    wf = jnp.concatenate([wgm, wqkvb, wza, wzb, wkcvc, wg, wbeta_a, pad], axis=1).astype(jnp.bfloat16)
    return wb, wf


def _bias_kernel(tab_ref, sel_ref, win_ref, cmp_ref, *, n_cmp):
    h = pl.program_id(0)

    def lookup(n):
        val = jnp.full(n.shape, tab_ref[0, h], jnp.float32)
        for b in range(1, NUM_BUCKETS):
            val = jnp.where(n >= _THR[b - 1], tab_ref[b, h], val)
        return val * LOG2E

    kj = lax.broadcasted_iota(jnp.int32, (TILE, TILE), 0)
    qi = lax.broadcasted_iota(jnp.int32, (TILE, TILE), 1)
    n_sel_tiles = sel_ref.shape[1] - 1
    n_win_tiles = win_ref.shape[1] - 1
    for dt in range(max(n_sel_tiles, n_win_tiles)):
        dist = dt * TILE + qi - kj
        v = lookup(jnp.maximum(dist, 0))
        if dt < n_sel_tiles:
            sel_ref[0, dt] = jnp.where(dist >= 0, v, NEG)
        if dt < n_win_tiles:
            win_ref[0, dt] = jnp.where((dist >= 0) & (dist < WINDOW), v, NEG)
    sel_ref[0, n_sel_tiles] = jnp.full((TILE, TILE), NEG, jnp.float32)
    win_ref[0, n_win_tiles] = jnp.full((TILE, TILE), NEG, jnp.float32)
    c = lax.broadcasted_iota(jnp.int32, cmp_ref.shape[2:], 0)
    for t in range(cmp_ref.shape[1]):
        s = t * TILE + lax.broadcasted_iota(jnp.int32, cmp_ref.shape[2:], 1)
        dist = s - (c * CMP_STRIDE + CMP_BLOCK - 1)
        cmp_ref[0, t] = jnp.where((dist >= 0) & (c < n_cmp), lookup(jnp.maximum(dist, 0)), NEG)


def _bias_tables(rel_bias, seq):
    nq = seq // TILE
    nwin = WINDOW // TILE + 1
    nc = seq // CMP_STRIDE
    n_cmp = nc - CMP_BLOCK // CMP_STRIDE + 1
    return pl.pallas_call(
        functools.partial(_bias_kernel, n_cmp=n_cmp),
        grid=(NSA_HEADS,),
        in_specs=[pl.BlockSpec(memory_space=pltpu.SMEM)],
        out_specs=[pl.BlockSpec((1, nq + 1, TILE, TILE), lambda h: (h, 0, 0, 0)),
                   pl.BlockSpec((1, nwin + 1, TILE, TILE), lambda h: (h, 0, 0, 0)),
                   pl.BlockSpec((1, nq, nc, TILE), lambda h: (h, 0, 0, 0))],
        out_shape=[jax.ShapeDtypeStruct((NSA_HEADS, nq + 1, TILE, TILE), jnp.float32),
                   jax.ShapeDtypeStruct((NSA_HEADS, nwin + 1, TILE, TILE), jnp.float32),
                   jax.ShapeDtypeStruct((NSA_HEADS, nq, nc, TILE), jnp.float32)],
        compiler_params=_cparams(1), name="bias_tables",
    )(rel_bias.astype(jnp.float32))


def _gelu_tanh(x):
    return x * (0.5 * (1.0 + jnp.tanh(math.sqrt(2.0 / math.pi) * (x + 0.044715 * (x * x * x)))))


def _compress_kernel(x_ref, pos_ref, w1k_ref, w1v_ref, w2_ref, o_ref, wbd):
    kv = pl.program_id(0)
    nc = o_ref.shape[2]
    half = CMP_BLOCK // 2
    dh = NSA_HEAD_DIM
    bf16 = jnp.bfloat16

    @pl.when(pl.program_id(1) == 0)
    def _():
        def build(w1_ref):
            zero = jnp.zeros((dh, CMP_HIDDEN), bf16)
            for a in range(2):
                for l in range(half):
                    r0 = (a * half + l) * dh
                    wl = w1_ref[r0:r0 + dh, :].astype(bf16)
                    wbd[a, l * NSA_KV:(l + 1) * NSA_KV, :] = jnp.concatenate(
                        [jnp.concatenate([wl, zero], axis=1), jnp.concatenate([zero, wl], axis=1)], axis=0)
        pl.when(kv == 0)(functools.partial(build, w1k_ref))
        pl.when(kv == 1)(functools.partial(build, w1v_ref))

    r = jnp.concatenate([x_ref[pl.ds(l, nc, stride=CMP_STRIDE), :] for l in range(CMP_STRIDE)], axis=1)
    a = jnp.dot((r + pos_ref[0, 0:1, :]).astype(bf16), wbd[0], preferred_element_type=jnp.float32)
    b = jnp.dot((r + pos_ref[0, 1:2, :]).astype(bf16), wbd[1], preferred_element_type=jnp.float32)
    hid = a + pltpu.roll(b, nc - 1, 0)
    o_ref[0, 0] = jnp.dot(_gelu_tanh(hid).astype(bf16), w2_ref[0], preferred_element_type=jnp.float32)


def _compress(pf, pos2, w1_k, w1_v, w2p, bsz, seq):
    nc = seq // CMP_STRIDE
    width = CMP_STRIDE * NSA_KV
    hid = NSA_GROUPS * CMP_HIDDEN
    assert CMP_BLOCK == 2 * CMP_STRIDE
    return pl.pallas_call(
        _compress_kernel,
        grid=(2, bsz),
        in_specs=[pl.BlockSpec((seq, NSA_KV), lambda k, b: (b, PF_KC // NSA_KV + k)),
                  pl.BlockSpec((1, 2, width), lambda k, b: (k, 0, 0)),
                  pl.BlockSpec(w1_k.shape, lambda k, b: (0, 0)),
                  pl.BlockSpec(w1_v.shape, lambda k, b: (0, 0)),
                  pl.BlockSpec((1, hid, LANES), lambda k, b: (k, 0, 0))],
        out_specs=pl.BlockSpec((1, 1, nc, LANES), lambda k, b: (b, k, 0, 0)),
        out_shape=jax.ShapeDtypeStruct((bsz, 2, nc, LANES), jnp.float32),
        scratch_shapes=[pltpu.VMEM((2, width, hid), jnp.bfloat16)],
        compiler_params=_cparams(2), name="compress",
    )(pf, pos2, w1_k, w1_v, w2p)


def _prep_compress_weights(pos_k, pos_v, w2_k, w2_v):
    eye = jnp.eye(NSA_GROUPS, dtype=jnp.float32)
    half = CMP_BLOCK // 2

    def w2_both(w2):
        return jnp.einsum('jd,gh->gjhd', w2, eye).reshape(NSA_GROUPS * CMP_HIDDEN, NSA_GROUPS * NSA_HEAD_DIM)

    def pos_both(pos):
        p = pos.reshape(2, half, 1, NSA_HEAD_DIM)
        return jnp.broadcast_to(p, (2, half, NSA_GROUPS, NSA_HEAD_DIM)).reshape(2, half * NSA_KV)

    pos2 = jnp.stack([pos_both(pos_k), pos_both(pos_v)]).astype(jnp.float32)
    w2p = jnp.stack([w2_both(w2_k), w2_both(w2_v)]).astype(jnp.bfloat16)
    return pos2, w2p


def _pair_lanes(x, g):
    sw = pltpu.roll(x, LANES // 2, 1)
    lane = lax.broadcasted_iota(jnp.int32, x.shape, 1)
    own = (lane < LANES // 2) == (g == 0)
    return jnp.where(own, x, sw)


def _group_rows_t(x, g):
    xt = x.T
    half = LANES // 2
    return jnp.where(g == 0, xt[0:half, :], xt[half:LANES, :])


def _nsa_step_kernel(q_ref, qn_ref, ks_ref, kw_ref, vs_ref, vw_ref, kcvc_ref, bc_ref, bsel_ref, bwin_ref,
                     small_ref, za_ref, o_ref,
                     ksa, kwp, vst, vwt, kcs, vct, qa_s, qp_s, ocmp_s, sbuf, *, n_cmp, n_sel, n_top):
    g = pl.program_id(0)
    step_id = pl.program_id(2)
    tq = TILE
    dh = NSA_HEAD_DIM
    seq = ks_ref.shape[0]
    nq = seq // TILE
    nc = kcvc_ref.shape[2]
    n_wt = bwin_ref.shape[1] - 1
    heads = range(NSA_HPG)
    f32 = jnp.float32
    bf16 = jnp.bfloat16
    nt_dims = (((1,), (1,)), ((), ()))
    gk = SEL_GROUP * TILE

    def select_tile(src_ref, r0, tile, slot):
        q32 = src_ref[r0:r0 + tq, :].astype(f32)
        low_half = lax.broadcasted_iota(jnp.int32, (tq, LANES), 1) < NSA_HEAD_DIM
        qh = []
        for hh in heads:
            blk = q32[:, (hh // 2) * LANES:(hh // 2 + 1) * LANES]
            if hh % 2:
                blk = pltpu.roll(blk, LANES // 2, 1)
            qh.append(jnp.where(low_half, blk, 0.0).astype(bf16))
        for hh in heads:
            qp_s[slot, hh * tq:(hh + 1) * tq, :] = qh[hh]
        q_all = qp_s[slot]
        s_c = lax.dot_general(kcs[...], q_all, nt_dims, preferred_element_type=f32)
        bias_c = jnp.concatenate([bc_ref[hh, tile] for hh in heads], axis=1)
        valid = bias_c > 0.5 * NEG
        s_c = s_c + bias_c
        e = jnp.where(valid, jnp.exp2(s_c - jnp.max(s_c, axis=0, keepdims=True)), 0.0)
        den = jnp.maximum(jnp.sum(e, axis=0, keepdims=True), 1e-30)
        p_c = e * (1.0 / den)
        psum = sum(p_c[:, hh * tq:(hh + 1) * tq] for hh in heads)
        sj = lax.broadcasted_iota(jnp.int32, (n_sel, nc), 0)
        ci = lax.broadcasted_iota(jnp.int32, (n_sel, nc), 1)
        overlap = ((ci * CMP_STRIDE < (sj + 1) * SEL_BLOCK) & (ci * CMP_STRIDE + CMP_BLOCK > sj * SEL_BLOCK)
                   & (ci < n_cmp)).astype(bf16)
        imp_t = sum(jnp.dot(overlap, part, preferred_element_type=f32)
                    for part in _split_bf16(psum, 3))
        ocmp_s[slot] = jnp.dot(vct[...], p_c.astype(bf16), preferred_element_type=f32)
        blk = lax.broadcasted_iota(jnp.int32, (n_sel, tq), 0)
        cur = (tile * tq + lax.broadcasted_iota(jnp.int32, (n_sel, tq), 1)) >> 6
        forced = (blk == 0) | (blk == cur) | (blk == cur - 1)
        imp_t = jnp.where(forced, FORCE, jnp.where(blk > cur, -FORCE, imp_t))
        rank = jnp.zeros((n_sel, tq), jnp.int32)
        for j in range(n_sel):
            other = imp_t[j:j + 1, :]
            ahead = (other > imp_t) | ((other == imp_t) & (blk > j))
            rank = rank + ahead.astype(jnp.int32)
        selb = jnp.where(rank < n_top, 0.0, NEG)
        pieces = [jnp.zeros((SEL_LANE0, tq), f32), selb]
        if LANES - SEL_LANE0 - n_sel > 0:
            pieces.append(jnp.zeros((LANES - SEL_LANE0 - n_sel, tq), f32))
        selb_r = jnp.concatenate(pieces, axis=0).T.astype(bf16)
        for hh in heads:
            qa_s[slot, hh * tq:(hh + 1) * tq, :] = qh[hh] + selb_r

    @pl.when(step_id == 0)
    def _():
        row = lax.broadcasted_iota(jnp.int32, (seq, LANES), 0)
        lane = lax.broadcasted_iota(jnp.int32, (seq, LANES), 1)
        onehot = (lane - SEL_LANE0) == (row >> 6)
        own = lane < NSA_HEAD_DIM
        ks_g = jnp.where(own, _pair_lanes(ks_ref[...].astype(f32), g), 0.0)
        ksa[...] = jnp.where(onehot, 1.0, ks_g).astype(bf16)
        pad = (n_wt - 1) * TILE
        kwp[0:pad, :] = jnp.zeros((pad, LANES), bf16)
        kwp[pad:pad + seq, :] = jnp.where(own, _pair_lanes(kw_ref[...].astype(f32), g), 0.0).astype(bf16)
        for kt in range(n_wt - 1):
            vwt[kt] = jnp.zeros(vwt.shape[1:], bf16)
        extra = vst.shape[1] - dh
        ones_row = (lax.broadcasted_iota(jnp.int32, (extra, TILE), 0) == 0).astype(bf16)
        for kt in range(nq):
            rows = slice(kt * TILE, (kt + 1) * TILE)
            vst[kt, 0:dh, :] = _group_rows_t(vs_ref[rows, :].astype(f32), g).astype(bf16)
            vst[kt, dh:dh + extra, :] = ones_row
            vwt[kt + n_wt - 1, 0:dh, :] = _group_rows_t(vw_ref[rows, :].astype(f32), g).astype(bf16)
            vwt[kt + n_wt - 1, dh:dh + extra, :] = ones_row
        kcs[...] = _pair_lanes(kcvc_ref[0, 0], g).astype(bf16)
        for ct in range(nc // TILE):
            rows = slice(ct * TILE, (ct + 1) * TILE)
            vct[:, rows] = _group_rows_t(kcvc_ref[0, 1, rows, :], g).astype(bf16)
        select_tile(q_ref, 0, 0, 0)

    tps = NSA_TILES_PER_STEP

    def one_tile(n_groups, slot):
        qt = step_id * tps + slot
        rows = slice(slot * tq, (slot + 1) * tq)
        q_all = qp_s[slot]
        q_aug = qa_s[slot]
        kw_rows = kwp[pl.ds(pl.multiple_of(qt * TILE, TILE), n_wt * TILE), :]
        s_w = lax.dot_general(kw_rows, q_all, nt_dims, preferred_element_type=f32)

        def sel_scores(gi):
            s = lax.dot_general(ksa[gi * gk:(gi + 1) * gk, :], q_aug, nt_dims, preferred_element_type=f32)
            tiles = []
            for t in range(SEL_GROUP):
                dt = qt - (gi * SEL_GROUP + t)
                idx = jnp.where(dt >= 0, dt, nq)
                tiles.append(jnp.concatenate([bsel_ref[hh, idx] for hh in heads], axis=1))
            sbuf[gi % 2] = s + jnp.concatenate(tiles, axis=0)

        sel_scores(0)
        if slot + 1 < tps:
            select_tile(q_ref, (slot + 1) * tq, qt + 1, slot + 1)
        else:
            select_tile(qn_ref, 0, jnp.minimum(qt + 1, nq - 1), 0)

        w_tiles = []
        for t in range(n_wt):
            dt = n_wt - 1 - t
            idx = jnp.where(qt >= dt, dt, n_wt)
            w_tiles.append(jnp.concatenate([bwin_ref[hh, idx] for hh in heads], axis=1))
        s_w = s_w + jnp.concatenate(w_tiles, axis=0)
        p_w = jnp.exp2(s_w - jnp.max(s_w, axis=0, keepdims=True))
        v_w = jnp.concatenate([vwt[qt + t] for t in range(n_wt)], axis=1)
        acc_w = jnp.dot(v_w, p_w.astype(bf16), preferred_element_type=f32)
        o_win = acc_w[0:dh, :] * (1.0 / acc_w[dh:dh + 1, :])

        m = jnp.full((1, NSA_HPG * tq), NEG, f32)
        acc = jnp.zeros((vst.shape[1], NSA_HPG * tq), f32)
        for gi in range(n_groups):
            if gi + 1 < n_groups:
                sel_scores(gi + 1)
            s = sbuf[gi % 2]
            m_new = jnp.maximum(m, jnp.max(s, axis=0, keepdims=True))
            alpha = jnp.exp2(m - m_new)
            p = jnp.exp2(s - m_new)
            v_t = jnp.concatenate([vst[gi * SEL_GROUP + t] for t in range(SEL_GROUP)], axis=1)
            acc = alpha * acc + jnp.dot(v_t, p.astype(bf16), preferred_element_type=f32)
            m = m_new
        o_sel = acc[0:dh, :] * (1.0 / acc[dh:dh + 1, :])
        o_cmp = ocmp_s[slot]

        gates_t = jax.nn.sigmoid(small_ref[rows, :]).T
        mixed = []
        for hh in heads:
            def gate(br):
                r = br * NSA_HEADS + hh
                return jnp.where(g == 0, gates_t[r:r + 1, :], gates_t[r + NSA_HPG:r + NSA_HPG + 1, :])
            cols = slice(hh * tq, (hh + 1) * tq)
            mixed.append(gate(0) * o_cmp[:, cols] + gate(1) * o_sel[:, cols] + gate(2) * o_win[:, cols])
        o = jnp.concatenate([jnp.concatenate(mixed[2 * j:2 * j + 2], axis=0).T for j in range(NSA_HPG // 2)],
                            axis=1)
        z = za_ref[rows, :]
        o_ref[rows, :] = (o * (z * jax.nn.sigmoid(z))).astype(o_ref.dtype)

    def step(n_groups):
        for slot in range(tps):
            one_tile(n_groups, slot)

    for n_groups in range(1, nq // SEL_GROUP + 1):
        pl.when(step_id // (SEL_GROUP // tps) == n_groups - 1)(functools.partial(step, n_groups))


def _nsa(pb, pf, kcvc, bias_sel, bias_win, bias_cmp, bsz, seq):
    nq = seq // TILE
    nc = seq // CMP_STRIDE
    n_cmp = nc - CMP_BLOCK // CMP_STRIDE + 1
    n_sel = seq // SEL_BLOCK
    n_top = min(SEL_TOPK, n_sel)
    nwin = bias_win.shape[1]
    tps = NSA_TILES_PER_STEP
    assert nq % SEL_GROUP == 0 and nc % TILE == 0 and bias_sel.shape[1] == nq + 1
    assert tps >= 2 and SEL_GROUP % tps == 0
    ns = nq // tps
    rows = tps * TILE
    gw = NSA_HPG * NSA_HEAD_DIM
    kern = functools.partial(_nsa_step_kernel, n_cmp=n_cmp, n_sel=n_sel, n_top=n_top)
    return pl.pallas_call(
        kern,
        grid=(NSA_GROUPS, bsz, ns),
        in_specs=[
            pl.BlockSpec((rows, gw), lambda g, b, t: (b * ns + t, PB_QP // gw + g)),
            pl.BlockSpec((TILE, gw), lambda g, b, t: (b * nq + jnp.minimum((t + 1) * tps, nq - 1),
                                                      PB_QP // gw + g)),
            pl.BlockSpec((seq, LANES), lambda g, b, t: (b, PB_KS // LANES)),
            pl.BlockSpec((seq, LANES), lambda g, b, t: (b, PB_KW // LANES)),
            pl.BlockSpec((seq, LANES), lambda g, b, t: (b, PB_VS // LANES)),
            pl.BlockSpec((seq, LANES), lambda g, b, t: (b, PB_VW // LANES)),
            pl.BlockSpec((1, 2, nc, LANES), lambda g, b, t: (b, 0, 0, 0)),
            pl.BlockSpec((NSA_HPG, nq, nc, TILE), lambda g, b, t: (g, 0, 0, 0)),
            pl.BlockSpec((NSA_HPG, nq + 1, TILE, TILE), lambda g, b, t: (g, 0, 0, 0)),
            pl.BlockSpec((NSA_HPG, nwin, TILE, TILE), lambda g, b, t: (g, 0, 0, 0)),
            pl.BlockSpec((rows, LANES), lambda g, b, t: (b * ns + t, PF_SMALL // LANES)),
            pl.BlockSpec((rows, gw), lambda g, b, t: (b * ns + t, PF_ZA // gw + g)),
        ],
        out_specs=pl.BlockSpec((rows, gw), lambda g, b, t: (b * ns + t, g)),
        out_shape=jax.ShapeDtypeStruct((bsz * seq, NSA_WIDTH), jnp.bfloat16),
        scratch_shapes=[
            pltpu.VMEM((seq, LANES), jnp.bfloat16),
            pltpu.VMEM((seq + (nwin - 2) * TILE, LANES), jnp.bfloat16),
            pltpu.VMEM((nq, NSA_V_ROWS, TILE), jnp.bfloat16),
            pltpu.VMEM((nq + nwin - 2, NSA_V_ROWS, TILE), jnp.bfloat16),
            pltpu.VMEM((nc, LANES), jnp.bfloat16),
            pltpu.VMEM((NSA_HEAD_DIM, nc), jnp.bfloat16),
            pltpu.VMEM((tps, NSA_HPG * TILE, LANES), jnp.bfloat16),
            pltpu.VMEM((tps, NSA_HPG * TILE, LANES), jnp.bfloat16),
            pltpu.VMEM((tps, NSA_HEAD_DIM, NSA_HPG * TILE), jnp.float32),
            pltpu.VMEM((2, SEL_GROUP * TILE, NSA_HPG * TILE), jnp.float32),
        ],
        compiler_params=_cparams(3), name="nsa",
    )(pb, pb, pb, pb, pb, pb, kcvc, bias_cmp, bias_sel, bias_win, pf, pf)


def _split_bf16(a, n):
    parts = []
    for _ in range(n - 1):
        hi = a.astype(jnp.bfloat16)
        parts.append(hi)
        a = a - hi.astype(jnp.float32)
    parts.append(a.astype(jnp.bfloat16))
    return parts


def _dot3(a, b):
    ah, al = _split_bf16(a, 2)
    bh, bl = _split_bf16(b, 2)
    f32 = jnp.float32
    return (jnp.dot(ah, bh, preferred_element_type=f32) + jnp.dot(al, bh, preferred_element_type=f32)
            + jnp.dot(ah, bl, preferred_element_type=f32))


def _softplus(x):
    return jnp.maximum(x, 0.0) + jnp.log(1.0 + jnp.exp(-jnp.abs(x)))


def _dn_kernel(scal_ref, q_ref, k_ref, v_ref, small_ref, z_ref, cw_ref, nw_ref, o_ref,
               qn, kn, vn, bet, gl, mm, nn, qq, oo, dd, ss, *, seq):
    h = pl.program_id(1)
    n_rows = q_ref.shape[0]
    nb = n_rows // seq
    c = DN_CHUNK
    n_chunks = seq // c
    d = DN_HEAD_DIM

    head = 8

    def conv_body(x_ref, which):
        n = n_rows - head
        y = x_ref[pl.ds(head, n), :] * cw_ref[which, DN_CONV - 1:DN_CONV, :]
        for j in range(DN_CONV - 1):
            y = y + x_ref[pl.ds(head - (DN_CONV - 1 - j), n), :] * cw_ref[which, j:j + 1, :]
        return y

    def conv_head(x_ref, which, r0):
        x = x_ref[pl.ds(r0, head), :]
        rowi = lax.broadcasted_iota(jnp.int32, (head, d), 0)
        y = x * cw_ref[which, DN_CONV - 1:DN_CONV, :]
        for j in range(DN_CONV - 1):
            sh = DN_CONV - 1 - j
            y = y + jnp.where(rowi >= sh, pltpu.roll(x, sh, 0), 0.0) * cw_ref[which, j:j + 1, :]
        return y

    def silu(y):
        return y * jax.nn.sigmoid(y)

    def l2n(t):
        return t * lax.rsqrt(jnp.sum(t * t, axis=-1, keepdims=True) + 1e-6)

    finish = (lambda y: l2n(silu(y)) * (d ** -0.5), lambda y: l2n(silu(y)), silu)
    for which, (x_ref, dst) in enumerate(((q_ref, qn), (k_ref, kn), (v_ref, vn))):
        dst[pl.ds(head, n_rows - head), :] = finish[which](conv_body(x_ref, which))
        for b in range(nb):
            dst[pl.ds(b * seq, head), :] = finish[which](conv_head(x_ref, which, b * seq))
    small = small_ref[...]
    lane = lax.broadcasted_iota(jnp.int32, small.shape, 1)
    beta_in = jnp.sum(jnp.where(lane == SMALL_BETA + h, small, 0.0), axis=-1, keepdims=True)
    a_in = jnp.sum(jnp.where(lane == SMALL_A + h, small, 0.0), axis=-1, keepdims=True)
    bet[...] = jnp.broadcast_to(jax.nn.sigmoid(beta_in), (n_rows, d))
    gl[...] = jnp.broadcast_to(-jnp.exp(scal_ref[0, h]) * _softplus(a_in + scal_ref[1, h]), (n_rows, d))

    f32 = jnp.float32
    bf16 = jnp.bfloat16
    nt_dims = (((1,), (1,)), ((), ()))
    ri = lax.broadcasted_iota(jnp.int32, (c, 2 * c), 0)
    lane2 = lax.broadcasted_iota(jnp.int32, (c, 2 * c), 1)
    first = lane2 < c
    cj = lane2 & (c - 1)
    incl = ri >= cj
    strict = ri > cj
    tril_b = (lax.broadcasted_iota(jnp.int32, (c, c), 0) >= lax.broadcasted_iota(jnp.int32, (c, c), 1)).astype(bf16)
    bs = DN_INV_BLOCK
    sh = bs.bit_length() - 1
    same_diag = (ri >> sh) == (cj >> sh)
    level_masks = []
    while (1 << sh) < c:
        level_masks.append(((ri >> (sh + 1)) == (cj >> (sh + 1))) & ((ri >> sh) > (cj >> sh)))
        sh += 1
    bd_r = lax.broadcasted_iota(jnp.int32, (2 * c, 2 * c), 0)
    bd_c = lax.broadcasted_iota(jnp.int32, (2 * c, 2 * c), 1)
    block_diag = (bd_r >= c) == (bd_c >= c)

    def dot_pair(a, b):
        bb = b.astype(bf16)
        bb = jnp.where(block_diag, jnp.concatenate([bb, bb], axis=0), jnp.zeros((), bf16))
        return jnp.dot(a.astype(bf16), bb, preferred_element_type=f32)

    def stack_diag(x0, x1):
        z = jnp.zeros_like(x0)
        return jnp.concatenate([jnp.concatenate([x0, z], axis=1), jnp.concatenate([z, x1], axis=1)], axis=0)

    def chunk_prep(it, carry):
        ids = [it * DN_PREP_UNROLL + cc for cc in range(DN_PREP_UNROLL)]
        rows = [pl.ds(pl.multiple_of(i * c, c), c) for i in ids]
        pairs = range(0, DN_PREP_UNROLL, 2)
        ks = [kn[r, :] for r in rows]
        betas = [bet[r, :] for r in rows]
        gcbs = [sum(jnp.dot(tril_b, part, preferred_element_type=f32)
                    for part in _split_bf16(gl[r, :], DN_DECAY_PARTS)) for r in rows]
        kbs = [k * beta for k, beta in zip(ks, betas)]
        kbfs = [k.astype(bf16) for k in ks]
        k2s = [jnp.concatenate([kbfs[j], kbfs[j + 1]], axis=0) for j in pairs]

        def against_pair_keys(xs):
            return [jnp.where(first,
                              lax.dot_general(xs[j].astype(bf16), k2, nt_dims, preferred_element_type=f32),
                              lax.dot_general(xs[j + 1].astype(bf16), k2, nt_dims, preferred_element_type=f32))
                    for j, k2 in zip(pairs, k2s)]

        a_kks = against_pair_keys(kbs)
        decays = []
        for j in pairs:
            gc_col = jnp.where(first, gcbs[j], gcbs[j + 1])
            gc_row = jnp.concatenate([gcbs[j], gcbs[j + 1]], axis=0).T[0:c, :]
            decays.append(jnp.where(incl, jnp.exp(jnp.where(incl, gc_col - gc_row, 0.0)), 0.0))
        lows = [jnp.where(strict, a * dec, 0.0) for a, dec in zip(a_kks, decays)]
        pws = [jnp.where(same_diag, -low, 0.0) for low in lows]
        es = list(pws)
        for _ in range(max(1, (bs - 1).bit_length()) - 1):
            pws = [dot_pair(pw, pw) for pw in pws]
            es = [e + pw + dot_pair(e, pw) for e, pw in zip(es, pws)]
        for below in level_masks:
            offs = [jnp.where(below, low, 0.0) for low in lows]
            xs = [off + dot_pair(e, off) for e, off in zip(es, offs)]
            es = [e - (x + dot_pair(x, e)) for e, x in zip(es, xs)]
        egcs = [jnp.exp(gcb) for gcb in gcbs]
        rhss = [jnp.concatenate([vn[r, :] * beta, kb * egc], axis=1)
                for r, beta, kb, egc in zip(rows, betas, kbs, egcs)]
        uwbs = []
        for e, j in zip(es, pairs):
            e_hi, e_lo = _split_bf16(e, 2)
            r2 = stack_diag(rhss[j], rhss[j + 1]).astype(bf16)
            er = jnp.dot(e_hi, r2, preferred_element_type=f32) + jnp.dot(e_lo, r2, preferred_element_type=f32)
            uwbs.append((rhss[j] + er[:, 0:2 * d]).astype(bf16))
            uwbs.append((rhss[j + 1] + er[:, 2 * d:4 * d]).astype(bf16))
        qs = [qn[r, :] for r in rows]
        a_qks = [a * dec for a, dec in zip(against_pair_keys(qs), decays)]
        g_lasts = [gcb[c - 1:c, :] for gcb in gcbs]
        kdec_ts = []
        for k, gcb, g_last in zip(ks, gcbs, g_lasts):
            kdec = k * jnp.exp(g_last - gcb)
            kdec_ts.append(jnp.concatenate([kdec, jnp.zeros_like(kdec)], axis=0).T[:, 0:c].astype(bf16))
        nms = [jnp.dot(kt, uwb, preferred_element_type=f32) for kt, uwb in zip(kdec_ts, uwbs)]
        oq2s = [jnp.dot(a.astype(bf16), stack_diag(uwbs[j], uwbs[j + 1]), preferred_element_type=f32)
                for a, j in zip(a_qks, pairs)]
        oqs = []
        for oq2 in oq2s:
            oqs += [oq2[:, 0:2 * d], oq2[:, 2 * d:4 * d]]
        for i, r, nm, oq, q, egc, g_last in zip(ids, rows, nms, oqs, qs, egcs, g_lasts):
            m0 = pl.ds(pl.multiple_of(i * d, d), d)
            nn[m0, :] = nm[:, 0:d]
            mm[m0, :] = nm[:, d:2 * d].astype(bf16)
            oo[r, :] = oq[:, 0:d]
            qq[r, :] = (q * egc - oq[:, d:2 * d]).astype(bf16)
            dd[pl.ds(pl.multiple_of(i * 8, 8), 8), :] = jnp.broadcast_to(jnp.exp(g_last), (8, d))
        return carry

    lax.fori_loop(0, nb * n_chunks // DN_PREP_UNROLL, chunk_prep, 0)

    def chunk_scan(i, states):
        ids = [b * n_chunks + i for b in range(nb)]
        blocks = [pl.ds(pl.multiple_of(j * d, d), d) for j in ids]
        sbs = [s.astype(bf16) for s in states]
        for blk, sb in zip(blocks, sbs):
            ss[blk, :] = sb
        prods = [jnp.dot(mm[blk, :], sb, preferred_element_type=jnp.float32) for blk, sb in zip(blocks, sbs)]
        return tuple(s * dd[pl.ds(pl.multiple_of(j * 8, 8), 1), :] - pr + nn[blk, :]
                     for s, j, blk, pr in zip(states, ids, blocks, prods))

    lax.fori_loop(0, n_chunks, chunk_scan, tuple(jnp.zeros((d, d), jnp.float32) for _ in range(nb)))

    nw = nw_ref[...]

    def chunk_out(it, carry):
        ids = [it * DN_OUT_UNROLL + cc for cc in range(DN_OUT_UNROLL)]
        rows = [pl.ds(pl.multiple_of(i * c, c), c) for i in ids]
        outs = [jnp.dot(qq[r, :], ss[pl.ds(pl.multiple_of(i * d, d), d), :], preferred_element_type=jnp.float32)
                + oo[r, :] for i, r in zip(ids, rows)]
        for r, o in zip(rows, outs):
            o = o * lax.rsqrt(jnp.mean(o * o, axis=-1, keepdims=True) + 1e-6) * nw
            z = z_ref[r, :]
            o_ref[r, :] = (o * (z * jax.nn.sigmoid(z))).astype(o_ref.dtype)
        return carry

    lax.fori_loop(0, nb * n_chunks // DN_OUT_UNROLL, chunk_out, 0)


def _deltanet(pf, conv_w, a_log, dt_bias, norm_w, bsz, seq):
    d = DN_HEAD_DIM
    nb = DN_BATCHES if bsz % DN_BATCHES == 0 else 1
    rows = nb * seq
    n_chunks = rows // DN_CHUNK
    qkv0 = PF_QKVB // d
    scal = jnp.stack([a_log, dt_bias]).astype(jnp.float32)
    f32 = jnp.float32
    bf16 = jnp.bfloat16
    cw4 = conv_w.astype(f32).reshape(DN_CONV, 3, DN_HEADS, d).transpose(2, 1, 0, 3)
    assert 2 * DN_CHUNK == d and n_chunks % DN_PREP_UNROLL == 0 and n_chunks % DN_OUT_UNROLL == 0
    return pl.pallas_call(
        functools.partial(_dn_kernel, seq=seq),
        grid=(bsz // nb, DN_HEADS),
        in_specs=[
            pl.BlockSpec(memory_space=pltpu.SMEM),
            pl.BlockSpec((rows, d), lambda b, h: (b, qkv0 + h)),
            pl.BlockSpec((rows, d), lambda b, h: (b, qkv0 + DN_HEADS + h)),
            pl.BlockSpec((rows, d), lambda b, h: (b, qkv0 + 2 * DN_HEADS + h)),
            pl.BlockSpec((rows, LANES), lambda b, h: (b, PF_SMALL // LANES)),
            pl.BlockSpec((rows, d), lambda b, h: (b, PF_ZB // d + h)),
            pl.BlockSpec((None, 3, DN_CONV, d), lambda b, h: (h, 0, 0, 0)),
            pl.BlockSpec((1, d), lambda b, h: (0, 0)),
        ],
        out_specs=pl.BlockSpec((rows, d), lambda b, h: (b, h)),
        out_shape=jax.ShapeDtypeStruct((bsz * seq, DN_WIDTH), jnp.bfloat16),
        scratch_shapes=[
            pltpu.VMEM((rows, d), f32), pltpu.VMEM((rows, d), f32), pltpu.VMEM((rows, d), f32),
            pltpu.VMEM((rows, d), f32), pltpu.VMEM((rows, d), f32),
            pltpu.VMEM((n_chunks * d, d), bf16), pltpu.VMEM((n_chunks * d, d), f32),
            pltpu.VMEM((rows, d), bf16), pltpu.VMEM((rows, d), f32),
            pltpu.VMEM((n_chunks * 8, d), f32),
            pltpu.VMEM((n_chunks * d, d), bf16),
        ],
        compiler_params=_cparams(2), name="deltanet",
    )(scal, pf, pf, pf, pf, pf, cw4, norm_w.astype(f32).reshape(1, d))


def _out_kernel(oa_ref, ob_ref, gma_ref, gmb_ref, x_ref, p_ref, wa_ref, wb_ref, wo_ref, wpg_ref, wp_ref,
                lng_ref, lnb_ref, o_ref):
    f32 = jnp.float32
    bf16 = jnp.bfloat16
    tm = o_ref.shape[0]
    sub = tm // OUT_SUBTILES
    parts = [slice(i * sub, (i + 1) * sub) for i in range(OUT_SUBTILES)]
    y_a = [jnp.dot(oa_ref[r, :], wa_ref[...], preferred_element_type=f32) for r in parts]
    y_b = [jnp.dot(ob_ref[r, :], wb_ref[...], preferred_element_type=f32) for r in parts]
    pw = [jnp.dot(p_ref[r, :].astype(bf16), wp_ref[...], preferred_element_type=f32) for r in parts]
    mix = [(jax.nn.sigmoid(gma_ref[r, :]) * ya + jax.nn.sigmoid(gmb_ref[r, :]) * yb).astype(bf16)
           for r, ya, yb in zip(parts, y_a, y_b)]
    h = [DEEPNORM_ALPHA * x_ref[r, :] + jnp.dot(mx, wo_ref[...], preferred_element_type=f32)
         for r, mx in zip(parts, mix)]
    gate = [jnp.dot(hh.astype(bf16), wpg_ref[...], preferred_element_type=f32) for hh in h]
    for r, hh, gt, pp in zip(parts, h, gate, pw):
        hh = hh + jax.nn.sigmoid(gt) * pp
        mu = jnp.mean(hh, axis=-1, keepdims=True)
        hc = hh - mu
        var = jnp.mean(hc * hc, axis=-1, keepdims=True)
        o_ref[r, :] = (hc * lax.rsqrt(var + 1e-5) * lng_ref[...] + lnb_ref[...]).astype(o_ref.dtype)


def _out_block(o_a, o_b, pf, x2, p2, wa, wb, wo, wpg, wp, ln_g, ln_b, tm):
    t = x2.shape[0]
    bf = jnp.bfloat16

    def full(shape):
        return pl.BlockSpec(shape, lambda i: (0, 0))

    return pl.pallas_call(
        _out_kernel,
        grid=(t // tm,),
        in_specs=[
            pl.BlockSpec((tm, NSA_WIDTH), lambda i: (i, 0)),
            pl.BlockSpec((tm, DN_WIDTH), lambda i: (i, 0)),
            pl.BlockSpec((tm, D_MODEL), lambda i: (i, PF_GM // D_MODEL)),
            pl.BlockSpec((tm, D_MODEL), lambda i: (i, PF_GM // D_MODEL + 1)),
            pl.BlockSpec((tm, D_MODEL), lambda i: (i, 0)),
            pl.BlockSpec((tm, PLE_DIM), lambda i: (i, 0)),
            full((NSA_WIDTH, D_MODEL)), full((DN_WIDTH, D_MODEL)), full((D_MODEL, D_MODEL)),
            full((D_MODEL, D_MODEL)), full((PLE_DIM, D_MODEL)), full((1, D_MODEL)), full((1, D_MODEL)),
        ],
        out_specs=pl.BlockSpec((tm, D_MODEL), lambda i: (i, 0)),
        out_shape=jax.ShapeDtypeStruct((t, D_MODEL), x2.dtype),
        compiler_params=_cparams(1), name="out_block",
    )(o_a, o_b, pf, pf, x2, p2, wa.astype(bf), wb.astype(bf), wo.astype(bf), wpg.astype(bf), wp.astype(bf),
      ln_g.astype(jnp.float32).reshape(1, D_MODEL), ln_b.astype(jnp.float32).reshape(1, D_MODEL))


def _layer(x, p, w_in, pos_k, pos_v, w1_k, w2_k, w1_v, w2_v, bias_tabs, conv_w, a_log, dt_bias, norm_w,
           w_a, w_b, w_o, w_ple, w_pg, ln_g, ln_b):
    bsz, seq, _ = x.shape
    t = bsz * seq
    x2 = x.reshape(t, D_MODEL)
    wb16, wf16 = _prep_w_in(w_in)
    pb, pf = _proj(x2, wb16, wf16, PROJ_TM if t % PROJ_TM == 0 else seq)

    pos2, w2p = _prep_compress_weights(pos_k, pos_v, w2_k, w2_v)
    kcvc = _compress(pf, pos2, w1_k, w1_v, w2p, bsz, seq)

    bias_sel, bias_win, bias_cmp = bias_tabs
    o_a = _nsa(pb, pf, kcvc, bias_sel, bias_win, bias_cmp, bsz, seq)
    o_b = _deltanet(pf, conv_w, a_log, dt_bias, norm_w, bsz, seq)
    out = _out_block(o_a, o_b, pf, x2, p.reshape(t, PLE_DIM), w_a, w_b, w_o, w_pg, w_ple, ln_g, ln_b,
                     OUT_TM if t % OUT_TM == 0 else seq)
    return out.reshape(bsz, seq, D_MODEL)


def kernel(x, p, w_in, cmp_pos_k, cmp_pos_v, cmp_w1_k, cmp_w2_k, cmp_w1_v, cmp_w2_v, rel_bias, dn_conv_w,
           dn_a_log, dn_dt_bias, dn_norm_w, w_branch_a, w_branch_b, w_out, w_ple, w_ple_gate, ln_g, ln_b):
    depth = w_in.shape[0]
    bias_tabs = _bias_tables(rel_bias, x.shape[1])
    for i in range(depth):
        x = _layer(x, p[i], w_in[i], cmp_pos_k[i], cmp_pos_v[i], cmp_w1_k[i], cmp_w2_k[i], cmp_w1_v[i],
                   cmp_w2_v[i], bias_tabs, dn_conv_w[i], dn_a_log[i], dn_dt_bias[i], dn_norm_w[i],
                   w_branch_a[i], w_branch_b[i], w_out[i], w_ple[i], w_ple_gate[i], ln_g[i], ln_b[i])
    return x
```

```python
import functools
import math

import numpy as np
import jax
import jax.numpy as jnp
from jax import lax
from jax.experimental import pallas as pl
from jax.experimental.pallas import tpu as pltpu

D_MODEL = 1024
PLE_DIM = 256
NSA_HEADS = 8
NSA_GROUPS = 2
NSA_HPG = NSA_HEADS // NSA_GROUPS
NSA_HEAD_DIM = 64
NSA_WIDTH = NSA_HEADS * NSA_HEAD_DIM
NSA_KV = NSA_GROUPS * NSA_HEAD_DIM
CMP_BLOCK = 32
CMP_STRIDE = 16
CMP_HIDDEN = 256
SEL_BLOCK = 64
SEL_TOPK = 8
WINDOW = 512
DN_HEADS = 4
DN_HEAD_DIM = 128
DN_WIDTH = DN_HEADS * DN_HEAD_DIM
DN_CONV = 4
DN_CHUNK = 64
NUM_BUCKETS = 32
REL_MAX_DIST = 1024
DEEPNORM_ALPHA = 2.0 ** 0.25
NEG = -1e30
FORCE = 1e6
LOG2E = 1.4426950408889634

LANES = 128
TILE = 128
SEL_LANE0 = 64
SEL_GROUP = 4
NSA_TILES_PER_STEP = 4
NSA_V_ROWS = NSA_HEAD_DIM + 16
VMEM_LIMIT = 56 * 1024 * 1024
PROJ_TM = 512
PROJ_TN = 1024
OUT_TM = 512
OUT_SUBTILES = 2
DN_INV_BLOCK = 16
DN_BATCHES = 2
DN_OUT_UNROLL = 8
DN_DECAY_PARTS = 2
DN_PREP_UNROLL = 32

HIGHEST = lax.Precision.HIGHEST

PB_QP = 0
PB_KS = PB_QP + NSA_WIDTH
PB_KW = PB_KS + NSA_KV
PB_VS = PB_KW + NSA_KV
PB_VW = PB_VS + NSA_KV
PB_WIDTH = PB_VW + NSA_KV
PF_GM = 0
PF_QKVB = PF_GM + 2 * D_MODEL
PF_ZA = PF_QKVB + 3 * DN_WIDTH
PF_ZB = PF_ZA + NSA_WIDTH
PF_KC = PF_ZB + DN_WIDTH
PF_VC = PF_KC + NSA_KV
PF_SMALL = PF_VC + NSA_KV
PF_WIDTH = PF_SMALL + LANES
SMALL_BETA = 3 * NSA_HEADS
SMALL_A = SMALL_BETA + DN_HEADS


def _bucket_thresholds():
    max_exact = NUM_BUCKETS // 2
    span = NUM_BUCKETS - max_exact
    ratio = REL_MAX_DIST // max_exact
    thr = list(range(1, max_exact + 1))
    for k in range(1, span):
        n = max_exact
        while n ** span < max_exact ** span * ratio ** k:
            n += 1
        thr.append(n)
    return tuple(thr)


_THR = _bucket_thresholds()


def _cparams(n_axes):
    return pltpu.CompilerParams(dimension_semantics=("arbitrary",) * n_axes, vmem_limit_bytes=VMEM_LIMIT)


def _proj_kernel(x_ref, wb_ref, wf_ref, pb_ref, pf_ref):
    xb = x_ref[...].astype(jnp.bfloat16)
    pb_ref[...] = jnp.dot(xb, wb_ref[...], preferred_element_type=jnp.float32).astype(pb_ref.dtype)
    n = pf_ref.shape[1]
    for c0 in range(0, n, PROJ_TN):
        c1 = min(c0 + PROJ_TN, n)
        pf_ref[:, c0:c1] = jnp.dot(xb, wf_ref[:, c0:c1], preferred_element_type=jnp.float32)


def _proj(x2, wb, wf, tm):
    t, d = x2.shape
    resident = dict(pipeline_mode=pl.Buffered(1))
    return pl.pallas_call(
        _proj_kernel,
        grid=(t // tm,),
        in_specs=[pl.BlockSpec((tm, d), lambda i: (i, 0)),
                  pl.BlockSpec((d, PB_WIDTH), lambda i: (0, 0), **resident),
                  pl.BlockSpec((d, PF_WIDTH), lambda i: (0, 0), **resident)],
        out_specs=[pl.BlockSpec((tm, PB_WIDTH), lambda i: (i, 0)),
                   pl.BlockSpec((tm, PF_WIDTH), lambda i: (i, 0))],
        out_shape=[jax.ShapeDtypeStruct((t, PB_WIDTH), jnp.bfloat16),
                   jax.ShapeDtypeStruct((t, PF_WIDTH), jnp.float32)],
        compiler_params=_cparams(1), name="proj",
    )(x2, wb, wf)


def _prep_w_in(w):
    d = w.shape[0]
    o = 0
    wq_scaled = (w[:, 0:NSA_WIDTH] * (NSA_HEAD_DIM ** -0.5 * LOG2E)).astype(jnp.bfloat16)
    w = w.astype(jnp.bfloat16)
    wq = w[:, o:o + NSA_WIDTH]; o += NSA_WIDTH
    wkv = w[:, o:o + 6 * NSA_KV]; o += 6 * NSA_KV
    wg = w[:, o:o + 3 * NSA_HEADS]; o += 3 * NSA_HEADS
    wza = w[:, o:o + NSA_WIDTH]; o += NSA_WIDTH
    wqkvb = w[:, o:o + 3 * DN_WIDTH]; o += 3 * DN_WIDTH
    wbeta_a = w[:, o:o + 2 * DN_HEADS]; o += 2 * DN_HEADS
    wzb = w[:, o:o + DN_WIDTH]; o += DN_WIDTH
    wgm = w[:, o:o + 2 * D_MODEL]
    wkcvc, wks, wvs, wkw, wvw = (wkv[:, 0:2 * NSA_KV], wkv[:, 2 * NSA_KV:3 * NSA_KV], wkv[:, 3 * NSA_KV:4 * NSA_KV],
                                 wkv[:, 4 * NSA_KV:5 * NSA_KV], wkv[:, 5 * NSA_KV:6 * NSA_KV])
    del wq
    wb = jnp.concatenate([wq_scaled, wks, wkw, wvs, wvw], axis=1)
    pad = jnp.zeros((d, LANES - 3 * NSA_HEADS - 2 * DN_HEADS), w.dtype)
    wf = jnp.concatenate([wgm, wqkvb, wza, wzb, wkcvc, wg, wbeta_a, pad], axis=1)
    return wb, wf


def _bias_kernel(tab_ref, sel_ref, win_ref, cmp_ref, *, n_cmp):
    h = pl.program_id(0)

    def lookup(n):
        val = jnp.full(n.shape, tab_ref[0, h], jnp.float32)
        for b in range(1, NUM_BUCKETS):
            val = jnp.where(n >= _THR[b - 1], tab_ref[b, h], val)
        return val * LOG2E

    kj = lax.broadcasted_iota(jnp.int32, (TILE, TILE), 0)
    qi = lax.broadcasted_iota(jnp.int32, (TILE, TILE), 1)
    n_sel_tiles = sel_ref.shape[1] - 1
    n_win_tiles = win_ref.shape[1] - 1
    for dt in range(max(n_sel_tiles, n_win_tiles)):
        dist = dt * TILE + qi - kj
        v = lookup(jnp.maximum(dist, 0))
        if dt < n_sel_tiles:
            sel_ref[0, dt] = jnp.where(dist >= 0, v, NEG)
        if dt < n_win_tiles:
            win_ref[0, dt] = jnp.where((dist >= 0) & (dist < WINDOW), v, NEG)
    sel_ref[0, n_sel_tiles] = jnp.full((TILE, TILE), NEG, jnp.float32)
    win_ref[0, n_win_tiles] = jnp.full((TILE, TILE), NEG, jnp.float32)
    c = lax.broadcasted_iota(jnp.int32, cmp_ref.shape[2:], 0)
    for t in range(cmp_ref.shape[1]):
        s = t * TILE + lax.broadcasted_iota(jnp.int32, cmp_ref.shape[2:], 1)
        dist = s - (c * CMP_STRIDE + CMP_BLOCK - 1)
        cmp_ref[0, t] = jnp.where((dist >= 0) & (c < n_cmp), lookup(jnp.maximum(dist, 0)), NEG)


def _bias_tables(rel_bias, seq):
    nq = seq // TILE
    nwin = WINDOW // TILE + 1
    nc = seq // CMP_STRIDE
    n_cmp = nc - CMP_BLOCK // CMP_STRIDE + 1
    return pl.pallas_call(
        functools.partial(_bias_kernel, n_cmp=n_cmp),
        grid=(NSA_HEADS,),
        in_specs=[pl.BlockSpec(memory_space=pltpu.SMEM)],
        out_specs=[pl.BlockSpec((1, nq + 1, TILE, TILE), lambda h: (h, 0, 0, 0)),
                   pl.BlockSpec((1, nwin + 1, TILE, TILE), lambda h: (h, 0, 0, 0)),
                   pl.BlockSpec((1, nq, nc, TILE), lambda h: (h, 0, 0, 0))],
        out_shape=[jax.ShapeDtypeStruct((NSA_HEADS, nq + 1, TILE, TILE), jnp.float32),
                   jax.ShapeDtypeStruct((NSA_HEADS, nwin + 1, TILE, TILE), jnp.float32),
                   jax.ShapeDtypeStruct((NSA_HEADS, nq, nc, TILE), jnp.float32)],
        compiler_params=_cparams(1), name="bias_tables",
    )(rel_bias.astype(jnp.float32))


def _gelu_tanh(x):
    return x * (0.5 * (1.0 + jnp.tanh(math.sqrt(2.0 / math.pi) * (x + 0.044715 * (x * x * x)))))


def _compress_kernel(x_ref, pos_ref, w1k_ref, w1v_ref, w2_ref, o_ref, wbd):
    kv = pl.program_id(0)
    nc = o_ref.shape[2]
    half = CMP_BLOCK // 2
    dh = NSA_HEAD_DIM
    bf16 = jnp.bfloat16

    @pl.when(pl.program_id(1) == 0)
    def _():
        def build(w1_ref):
            zero = jnp.zeros((dh, CMP_HIDDEN), bf16)
            for a in range(2):
                for l in range(half):
                    r0 = (a * half + l) * dh
                    wl = w1_ref[r0:r0 + dh, :].astype(bf16)
                    wbd[a, l * NSA_KV:(l + 1) * NSA_KV, :] = jnp.concatenate(
                        [jnp.concatenate([wl, zero], axis=1), jnp.concatenate([zero, wl], axis=1)], axis=0)
        pl.when(kv == 0)(functools.partial(build, w1k_ref))
        pl.when(kv == 1)(functools.partial(build, w1v_ref))

    r = jnp.concatenate([x_ref[pl.ds(l, nc, stride=CMP_STRIDE), :] for l in range(CMP_STRIDE)], axis=1)
    a = jnp.dot((r + pos_ref[0, 0:1, :]).astype(bf16), wbd[0], preferred_element_type=jnp.float32)
    b = jnp.dot((r + pos_ref[0, 1:2, :]).astype(bf16), wbd[1], preferred_element_type=jnp.float32)
    hid = a + pltpu.roll(b, nc - 1, 0)
    o_ref[0, 0] = jnp.dot(_gelu_tanh(hid).astype(bf16), w2_ref[0], preferred_element_type=jnp.float32)


def _compress(pf, pos2, w1_k, w1_v, w2p, bsz, seq):
    nc = seq // CMP_STRIDE
    width = CMP_STRIDE * NSA_KV
    hid = NSA_GROUPS * CMP_HIDDEN
    assert CMP_BLOCK == 2 * CMP_STRIDE
    return pl.pallas_call(
        _compress_kernel,
        grid=(2, bsz),
        in_specs=[pl.BlockSpec((seq, NSA_KV), lambda k, b: (b, PF_KC // NSA_KV + k)),
                  pl.BlockSpec((1, 2, width), lambda k, b: (k, 0, 0)),
                  pl.BlockSpec(w1_k.shape, lambda k, b: (0, 0)),
                  pl.BlockSpec(w1_v.shape, lambda k, b: (0, 0)),
                  pl.BlockSpec((1, hid, LANES), lambda k, b: (k, 0, 0))],
        out_specs=pl.BlockSpec((1, 1, nc, LANES), lambda k, b: (b, k, 0, 0)),
        out_shape=jax.ShapeDtypeStruct((bsz, 2, nc, LANES), jnp.float32),
        scratch_shapes=[pltpu.VMEM((2, width, hid), jnp.bfloat16)],
        compiler_params=_cparams(2), name="compress",
    )(pf, pos2, w1_k, w1_v, w2p)


def _prep_compress_weights(pos_k, pos_v, w2_k, w2_v):
    eye = jnp.eye(NSA_GROUPS, dtype=jnp.float32)
    half = CMP_BLOCK // 2

    def w2_both(w2):
        return jnp.einsum('jd,gh->gjhd', w2, eye).reshape(NSA_GROUPS * CMP_HIDDEN, NSA_GROUPS * NSA_HEAD_DIM)

    def pos_both(pos):
        p = pos.reshape(2, half, 1, NSA_HEAD_DIM)
        return jnp.broadcast_to(p, (2, half, NSA_GROUPS, NSA_HEAD_DIM)).reshape(2, half * NSA_KV)

    pos2 = jnp.stack([pos_both(pos_k), pos_both(pos_v)]).astype(jnp.float32)
    w2p = jnp.stack([w2_both(w2_k), w2_both(w2_v)]).astype(jnp.bfloat16)
    return pos2, w2p


def _pair_lanes(x, g):
    sw = pltpu.roll(x, LANES // 2, 1)
    lane = lax.broadcasted_iota(jnp.int32, x.shape, 1)
    own = (lane < LANES // 2) == (g == 0)
    return jnp.where(own, x, sw)


def _group_rows_t(x, g):
    xt = x.T
    half = LANES // 2
    return jnp.where(g == 0, xt[0:half, :], xt[half:LANES, :])


def _nsa_step_kernel(q_ref, qn_ref, ks_ref, kw_ref, vs_ref, vw_ref, kcvc_ref, bc_ref, bsel_ref, bwin_ref,
                     small_ref, za_ref, o_ref,
                     ksa, kwp, vst, vwt, kcs, vct, qa_s, qp_s, ocmp_s, sbuf, *, n_cmp, n_sel, n_top):
    g = pl.program_id(0)
    step_id = pl.program_id(2)
    tq = TILE
    dh = NSA_HEAD_DIM
    seq = ks_ref.shape[0]
    nq = seq // TILE
    nc = kcvc_ref.shape[2]
    n_wt = bwin_ref.shape[1] - 1
    heads = range(NSA_HPG)
    f32 = jnp.float32
    bf16 = jnp.bfloat16
    nt_dims = (((1,), (1,)), ((), ()))
    gk = SEL_GROUP * TILE

    def select_tile(src_ref, r0, tile, slot):
        q32 = src_ref[r0:r0 + tq, :].astype(f32)
        low_half = lax.broadcasted_iota(jnp.int32, (tq, LANES), 1) < NSA_HEAD_DIM
        qh = []
        for hh in heads:
            blk = q32[:, (hh // 2) * LANES:(hh // 2 + 1) * LANES]
            if hh % 2:
                blk = pltpu.roll(blk, LANES // 2, 1)
            qh.append(jnp.where(low_half, blk, 0.0).astype(bf16))
        for hh in heads:
            qp_s[slot, hh * tq:(hh + 1) * tq, :] = qh[hh]
        q_all = qp_s[slot]
        s_c = lax.dot_general(kcs[...], q_all, nt_dims, preferred_element_type=f32)
        bias_c = jnp.concatenate([bc_ref[hh, tile] for hh in heads], axis=1)
        valid = bias_c > 0.5 * NEG
        s_c = s_c + bias_c
        e = jnp.where(valid, jnp.exp2(s_c - jnp.max(s_c, axis=0, keepdims=True)), 0.0)
        den = jnp.maximum(jnp.sum(e, axis=0, keepdims=True), 1e-30)
        p_c = e * (1.0 / den)
        psum = sum(p_c[:, hh * tq:(hh + 1) * tq] for hh in heads)
        sj = lax.broadcasted_iota(jnp.int32, (n_sel, nc), 0)
        ci = lax.broadcasted_iota(jnp.int32, (n_sel, nc), 1)
        overlap = ((ci * CMP_STRIDE < (sj + 1) * SEL_BLOCK) & (ci * CMP_STRIDE + CMP_BLOCK > sj * SEL_BLOCK)
                   & (ci < n_cmp)).astype(bf16)
        imp_t = sum(jnp.dot(overlap, part, preferred_element_type=f32)
                    for part in _split_bf16(psum, 3))
        ocmp_s[slot] = jnp.dot(vct[...], p_c.astype(bf16), preferred_element_type=f32)
        blk = lax.broadcasted_iota(jnp.int32, (n_sel, tq), 0)
        cur = (tile * tq + lax.broadcasted_iota(jnp.int32, (n_sel, tq), 1)) >> 6
        forced = (blk == 0) | (blk == cur) | (blk == cur - 1)
        imp_t = jnp.where(forced, FORCE, jnp.where(blk > cur, -FORCE, imp_t))
        rank = jnp.zeros((n_sel, tq), jnp.int32)
        for j in range(n_sel):
            other = imp_t[j:j + 1, :]
            ahead = (other > imp_t) | ((other == imp_t) & (blk > j))
            rank = rank + ahead.astype(jnp.int32)
        selb = jnp.where(rank < n_top, 0.0, NEG)
        pieces = [jnp.zeros((SEL_LANE0, tq), f32), selb]
        if LANES - SEL_LANE0 - n_sel > 0:
            pieces.append(jnp.zeros((LANES - SEL_LANE0 - n_sel, tq), f32))
        selb_r = jnp.concatenate(pieces, axis=0).T.astype(bf16)
        for hh in heads:
            qa_s[slot, hh * tq:(hh + 1) * tq, :] = qh[hh] + selb_r

    @pl.when(step_id == 0)
    def _():
        row = lax.broadcasted_iota(jnp.int32, (seq, LANES), 0)
        lane = lax.broadcasted_iota(jnp.int32, (seq, LANES), 1)
        onehot = (lane - SEL_LANE0) == (row >> 6)
        own = lane < NSA_HEAD_DIM
        ks_g = jnp.where(own, _pair_lanes(ks_ref[...].astype(f32), g), 0.0)
        ksa[...] = jnp.where(onehot, 1.0, ks_g).astype(bf16)
        pad = (n_wt - 1) * TILE
        kwp[0:pad, :] = jnp.zeros((pad, LANES), bf16)
        kwp[pad:pad + seq, :] = jnp.where(own, _pair_lanes(kw_ref[...].astype(f32), g), 0.0).astype(bf16)
        for kt in range(n_wt - 1):
            vwt[kt] = jnp.zeros(vwt.shape[1:], bf16)
        extra = vst.shape[1] - dh
        ones_row = (lax.broadcasted_iota(jnp.int32, (extra, TILE), 0) == 0).astype(bf16)
        for kt in range(nq):
            rows = slice(kt * TILE, (kt + 1) * TILE)
            vst[kt, 0:dh, :] = _group_rows_t(vs_ref[rows, :].astype(f32), g).astype(bf16)
            vst[kt, dh:dh + extra, :] = ones_row
            vwt[kt + n_wt - 1, 0:dh, :] = _group_rows_t(vw_ref[rows, :].astype(f32), g).astype(bf16)
            vwt[kt + n_wt - 1, dh:dh + extra, :] = ones_row
        kcs[...] = _pair_lanes(kcvc_ref[0, 0], g).astype(bf16)
        for ct in range(nc // TILE):
            rows = slice(ct * TILE, (ct + 1) * TILE)
            vct[:, rows] = _group_rows_t(kcvc_ref[0, 1, rows, :], g).astype(bf16)
        select_tile(q_ref, 0, 0, 0)

    tps = NSA_TILES_PER_STEP

    def one_tile(n_groups, slot):
        qt = step_id * tps + slot
        rows = slice(slot * tq, (slot + 1) * tq)
        q_all = qp_s[slot]
        q_aug = qa_s[slot]
        kw_rows = kwp[pl.ds(pl.multiple_of(qt * TILE, TILE), n_wt * TILE), :]
        s_w = lax.dot_general(kw_rows, q_all, nt_dims, preferred_element_type=f32)

        def sel_scores(gi):
            s = lax.dot_general(ksa[gi * gk:(gi + 1) * gk, :], q_aug, nt_dims, preferred_element_type=f32)
            tiles = []
            for t in range(SEL_GROUP):
                dt = qt - (gi * SEL_GROUP + t)
                idx = jnp.where(dt >= 0, dt, nq)
                tiles.append(jnp.concatenate([bsel_ref[hh, idx] for hh in heads], axis=1))
            sbuf[gi % 2] = s + jnp.concatenate(tiles, axis=0)

        sel_scores(0)
        if slot + 1 < tps:
            select_tile(q_ref, (slot + 1) * tq, qt + 1, slot + 1)
        else:
            select_tile(qn_ref, 0, jnp.minimum(qt + 1, nq - 1), 0)

        w_tiles = []
        for t in range(n_wt):
            dt = n_wt - 1 - t
            idx = jnp.where(qt >= dt, dt, n_wt)
            w_tiles.append(jnp.concatenate([bwin_ref[hh, idx] for hh in heads], axis=1))
        s_w = s_w + jnp.concatenate(w_tiles, axis=0)
        p_w = jnp.exp2(s_w - jnp.max(s_w, axis=0, keepdims=True))
        v_w = jnp.concatenate([vwt[qt + t] for t in range(n_wt)], axis=1)
        acc_w = jnp.dot(v_w, p_w.astype(bf16), preferred_element_type=f32)
        o_win = acc_w[0:dh, :] * (1.0 / acc_w[dh:dh + 1, :])

        m = jnp.full((1, NSA_HPG * tq), NEG, f32)
        acc = jnp.zeros((vst.shape[1], NSA_HPG * tq), f32)
        for gi in range(n_groups):
            if gi + 1 < n_groups:
                sel_scores(gi + 1)
            s = sbuf[gi % 2]
            m_new = jnp.maximum(m, jnp.max(s, axis=0, keepdims=True))
            alpha = jnp.exp2(m - m_new)
            p = jnp.exp2(s - m_new)
            v_t = jnp.concatenate([vst[gi * SEL_GROUP + t] for t in range(SEL_GROUP)], axis=1)
            acc = alpha * acc + jnp.dot(v_t, p.astype(bf16), preferred_element_type=f32)
            m = m_new
        o_sel = acc[0:dh, :] * (1.0 / acc[dh:dh + 1, :])
        o_cmp = ocmp_s[slot]

        gates_t = jax.nn.sigmoid(small_ref[rows, :]).T
        mixed = []
        for hh in heads:
            def gate(br):
                r = br * NSA_HEADS + hh
                return jnp.where(g == 0, gates_t[r:r + 1, :], gates_t[r + NSA_HPG:r + NSA_HPG + 1, :])
            cols = slice(hh * tq, (hh + 1) * tq)
            mixed.append(gate(0) * o_cmp[:, cols] + gate(1) * o_sel[:, cols] + gate(2) * o_win[:, cols])
        o = jnp.concatenate([jnp.concatenate(mixed[2 * j:2 * j + 2], axis=0).T for j in range(NSA_HPG // 2)],
                            axis=1)
        z = za_ref[rows, :]
        o_ref[rows, :] = (o * (z * jax.nn.sigmoid(z))).astype(o_ref.dtype)

    def step(n_groups):
        for slot in range(tps):
            one_tile(n_groups, slot)

    for n_groups in range(1, nq // SEL_GROUP + 1):
        pl.when(step_id // (SEL_GROUP // tps) == n_groups - 1)(functools.partial(step, n_groups))


def _nsa(pb, pf, kcvc, bias_sel, bias_win, bias_cmp, bsz, seq):
    nq = seq // TILE
    nc = seq // CMP_STRIDE
    n_cmp = nc - CMP_BLOCK // CMP_STRIDE + 1
    n_sel = seq // SEL_BLOCK
    n_top = min(SEL_TOPK, n_sel)
    nwin = bias_win.shape[1]
    tps = NSA_TILES_PER_STEP
    assert nq % SEL_GROUP == 0 and nc % TILE == 0 and bias_sel.shape[1] == nq + 1
    assert tps >= 2 and SEL_GROUP % tps == 0
    ns = nq // tps
    rows = tps * TILE
    gw = NSA_HPG * NSA_HEAD_DIM
    kern = functools.partial(_nsa_step_kernel, n_cmp=n_cmp, n_sel=n_sel, n_top=n_top)
    return pl.pallas_call(
        kern,
        grid=(NSA_GROUPS, bsz, ns),
        in_specs=[
            pl.BlockSpec((rows, gw), lambda g, b, t: (b * ns + t, PB_QP // gw + g)),
            pl.BlockSpec((TILE, gw), lambda g, b, t: (b * nq + jnp.minimum((t + 1) * tps, nq - 1),
                                                      PB_QP // gw + g)),
            pl.BlockSpec((seq, LANES), lambda g, b, t: (b, PB_KS // LANES)),
            pl.BlockSpec((seq, LANES), lambda g, b, t: (b, PB_KW // LANES)),
            pl.BlockSpec((seq, LANES), lambda g, b, t: (b, PB_VS // LANES)),
            pl.BlockSpec((seq, LANES), lambda g, b, t: (b, PB_VW // LANES)),
            pl.BlockSpec((1, 2, nc, LANES), lambda g, b, t: (b, 0, 0, 0)),
            pl.BlockSpec((NSA_HPG, nq, nc, TILE), lambda g, b, t: (g, 0, 0, 0)),
            pl.BlockSpec((NSA_HPG, nq + 1, TILE, TILE), lambda g, b, t: (g, 0, 0, 0)),
            pl.BlockSpec((NSA_HPG, nwin, TILE, TILE), lambda g, b, t: (g, 0, 0, 0)),
            pl.BlockSpec((rows, LANES), lambda g, b, t: (b * ns + t, PF_SMALL // LANES)),
            pl.BlockSpec((rows, gw), lambda g, b, t: (b * ns + t, PF_ZA // gw + g)),
        ],
        out_specs=pl.BlockSpec((rows, gw), lambda g, b, t: (b * ns + t, g)),
        out_shape=jax.ShapeDtypeStruct((bsz * seq, NSA_WIDTH), jnp.bfloat16),
        scratch_shapes=[
            pltpu.VMEM((seq, LANES), jnp.bfloat16),
            pltpu.VMEM((seq + (nwin - 2) * TILE, LANES), jnp.bfloat16),
            pltpu.VMEM((nq, NSA_V_ROWS, TILE), jnp.bfloat16),
            pltpu.VMEM((nq + nwin - 2, NSA_V_ROWS, TILE), jnp.bfloat16),
            pltpu.VMEM((nc, LANES), jnp.bfloat16),
            pltpu.VMEM((NSA_HEAD_DIM, nc), jnp.bfloat16),
            pltpu.VMEM((tps, NSA_HPG * TILE, LANES), jnp.bfloat16),
            pltpu.VMEM((tps, NSA_HPG * TILE, LANES), jnp.bfloat16),
            pltpu.VMEM((tps, NSA_HEAD_DIM, NSA_HPG * TILE), jnp.float32),
            pltpu.VMEM((2, SEL_GROUP * TILE, NSA_HPG * TILE), jnp.float32),
        ],
        compiler_params=_cparams(3), name="nsa",
    )(pb, pb, pb, pb, pb, pb, kcvc, bias_cmp, bias_sel, bias_win, pf, pf)


def _split_bf16(a, n):
    parts = []
    for _ in range(n - 1):
        hi = a.astype(jnp.bfloat16)
        parts.append(hi)
        a = a - hi.astype(jnp.float32)
    parts.append(a.astype(jnp.bfloat16))
    return parts


def _dot3(a, b):
    ah, al = _split_bf16(a, 2)
    bh, bl = _split_bf16(b, 2)
    f32 = jnp.float32
    return (jnp.dot(ah, bh, preferred_element_type=f32) + jnp.dot(al, bh, preferred_element_type=f32)
            + jnp.dot(ah, bl, preferred_element_type=f32))


def _softplus(x):
    return jnp.maximum(x, 0.0) + jnp.log(1.0 + jnp.exp(-jnp.abs(x)))


def _dn_kernel(scal_ref, q_ref, k_ref, v_ref, small_ref, z_ref, cw_ref, nw_ref, o_ref,
               qn, kn, vn, bet, gl, mm, nn, qq, oo, dd, ss, *, seq):
    h = pl.program_id(1)
    n_rows = q_ref.shape[0]
    nb = n_rows // seq
    c = DN_CHUNK
    n_chunks = seq // c
    d = DN_HEAD_DIM

    head = 8

    def conv_body(x_ref, which):
        n = n_rows - head
        y = x_ref[pl.ds(head, n), :] * cw_ref[which, DN_CONV - 1:DN_CONV, :]
        for j in range(DN_CONV - 1):
            y = y + x_ref[pl.ds(head - (DN_CONV - 1 - j), n), :] * cw_ref[which, j:j + 1, :]
        return y

    def conv_head(x_ref, which, r0):
        x = x_ref[pl.ds(r0, head), :]
        rowi = lax.broadcasted_iota(jnp.int32, (head, d), 0)
        y = x * cw_ref[which, DN_CONV - 1:DN_CONV, :]
        for j in range(DN_CONV - 1):
            sh = DN_CONV - 1 - j
            y = y + jnp.where(rowi >= sh, pltpu.roll(x, sh, 0), 0.0) * cw_ref[which, j:j + 1, :]
        return y

    def silu(y):
        return y * jax.nn.sigmoid(y)

    def l2n(t):
        return t * lax.rsqrt(jnp.sum(t * t, axis=-1, keepdims=True) + 1e-6)

    finish = (lambda y: l2n(silu(y)) * (d ** -0.5), lambda y: l2n(silu(y)), silu)
    for which, (x_ref, dst) in enumerate(((q_ref, qn), (k_ref, kn), (v_ref, vn))):
        dst[pl.ds(head, n_rows - head), :] = finish[which](conv_body(x_ref, which))
        for b in range(nb):
            dst[pl.ds(b * seq, head), :] = finish[which](conv_head(x_ref, which, b * seq))
    small = small_ref[...]
    lane = lax.broadcasted_iota(jnp.int32, small.shape, 1)
    beta_in = jnp.sum(jnp.where(lane == SMALL_BETA + h, small, 0.0), axis=-1, keepdims=True)
    a_in = jnp.sum(jnp.where(lane == SMALL_A + h, small, 0.0), axis=-1, keepdims=True)
    bet[...] = jnp.broadcast_to(jax.nn.sigmoid(beta_in), (n_rows, d))
    gl[...] = jnp.broadcast_to(-jnp.exp(scal_ref[0, h]) * _softplus(a_in + scal_ref[1, h]), (n_rows, d))

    f32 = jnp.float32
    bf16 = jnp.bfloat16
    nt_dims = (((1,), (1,)), ((), ()))
    ri = lax.broadcasted_iota(jnp.int32, (c, 2 * c), 0)
    lane2 = lax.broadcasted_iota(jnp.int32, (c, 2 * c), 1)
    first = lane2 < c
    cj = lane2 & (c - 1)
    incl = ri >= cj
    strict = ri > cj
    tril_b = (lax.broadcasted_iota(jnp.int32, (c, c), 0) >= lax.broadcasted_iota(jnp.int32, (c, c), 1)).astype(bf16)
    bs = DN_INV_BLOCK
    sh = bs.bit_length() - 1
    same_diag = (ri >> sh) == (cj >> sh)
    level_masks = []
    while (1 << sh) < c:
        level_masks.append(((ri >> (sh + 1)) == (cj >> (sh + 1))) & ((ri >> sh) > (cj >> sh)))
        sh += 1
    bd_r = lax.broadcasted_iota(jnp.int32, (2 * c, 2 * c), 0)
    bd_c = lax.broadcasted_iota(jnp.int32, (2 * c, 2 * c), 1)
    block_diag = (bd_r >= c) == (bd_c >= c)

    def dot_pair(a, b):
        bb = b.astype(bf16)
        bb = jnp.where(block_diag, jnp.concatenate([bb, bb], axis=0), jnp.zeros((), bf16))
        return jnp.dot(a.astype(bf16), bb, preferred_element_type=f32)

    def stack_diag(x0, x1):
        z = jnp.zeros_like(x0)
        return jnp.concatenate([jnp.concatenate([x0, z], axis=1), jnp.concatenate([z, x1], axis=1)], axis=0)

    def chunk_prep(it, carry):
        ids = [it * DN_PREP_UNROLL + cc for cc in range(DN_PREP_UNROLL)]
        rows = [pl.ds(pl.multiple_of(i * c, c), c) for i in ids]
        pairs = range(0, DN_PREP_UNROLL, 2)
        ks = [kn[r, :] for r in rows]
        betas = [bet[r, :] for r in rows]
        gcbs = [sum(jnp.dot(tril_b, part, preferred_element_type=f32)
                    for part in _split_bf16(gl[r, :], DN_DECAY_PARTS)) for r in rows]
        kbs = [k * beta for k, beta in zip(ks, betas)]
        kbfs = [k.astype(bf16) for k in ks]
        k2s = [jnp.concatenate([kbfs[j], kbfs[j + 1]], axis=0) for j in pairs]

        def against_pair_keys(xs):
            return [jnp.where(first,
                              lax.dot_general(xs[j].astype(bf16), k2, nt_dims, preferred_element_type=f32),
                              lax.dot_general(xs[j + 1].astype(bf16), k2, nt_dims, preferred_element_type=f32))
                    for j, k2 in zip(pairs, k2s)]

        a_kks = against_pair_keys(kbs)
        decays = []
        for j in pairs:
            gc_col = jnp.where(first, gcbs[j], gcbs[j + 1])
            gc_row = jnp.concatenate([gcbs[j], gcbs[j + 1]], axis=0).T[0:c, :]
            decays.append(jnp.where(incl, jnp.exp(jnp.where(incl, gc_col - gc_row, 0.0)), 0.0))
        lows = [jnp.where(strict, a * dec, 0.0) for a, dec in zip(a_kks, decays)]
        pws = [jnp.where(same_diag, -low, 0.0) for low in lows]
        es = list(pws)
        for _ in range(max(1, (bs - 1).bit_length()) - 1):
            pws = [dot_pair(pw, pw) for pw in pws]
            es = [e + pw + dot_pair(e, pw) for e, pw in zip(es, pws)]
        for below in level_masks:
            offs = [jnp.where(below, low, 0.0) for low in lows]
            xs = [off + dot_pair(e, off) for e, off in zip(es, offs)]
            es = [e - (x + dot_pair(x, e)) for e, x in zip(es, xs)]
        egcs = [jnp.exp(gcb) for gcb in gcbs]
        rhss = [jnp.concatenate([vn[r, :] * beta, kb * egc], axis=1)
                for r, beta, kb, egc in zip(rows, betas, kbs, egcs)]
        uwbs = []
        for e, j in zip(es, pairs):
            e_hi, e_lo = _split_bf16(e, 2)
            r2 = stack_diag(rhss[j], rhss[j + 1]).astype(bf16)
            er = jnp.dot(e_hi, r2, preferred_element_type=f32) + jnp.dot(e_lo, r2, preferred_element_type=f32)
            uwbs.append((rhss[j] + er[:, 0:2 * d]).astype(bf16))
            uwbs.append((rhss[j + 1] + er[:, 2 * d:4 * d]).astype(bf16))
        qs = [qn[r, :] for r in rows]
        a_qks = [a * dec for a, dec in zip(against_pair_keys(qs), decays)]
        g_lasts = [gcb[c - 1:c, :] for gcb in gcbs]
        kdec_ts = []
        for k, gcb, g_last in zip(ks, gcbs, g_lasts):
            kdec = k * jnp.exp(g_last - gcb)
            kdec_ts.append(jnp.concatenate([kdec, jnp.zeros_like(kdec)], axis=0).T[:, 0:c].astype(bf16))
        nms = [jnp.dot(kt, uwb, preferred_element_type=f32) for kt, uwb in zip(kdec_ts, uwbs)]
        oq2s = [jnp.dot(a.astype(bf16), stack_diag(uwbs[j], uwbs[j + 1]), preferred_element_type=f32)
                for a, j in zip(a_qks, pairs)]
        oqs = []
        for oq2 in oq2s:
            oqs += [oq2[:, 0:2 * d], oq2[:, 2 * d:4 * d]]
        for i, r, nm, oq, q, egc, g_last in zip(ids, rows, nms, oqs, qs, egcs, g_lasts):
            m0 = pl.ds(pl.multiple_of(i * d, d), d)
            nn[m0, :] = nm[:, 0:d]
            mm[m0, :] = nm[:, d:2 * d].astype(bf16)
            oo[r, :] = oq[:, 0:d]
            qq[r, :] = (q * egc - oq[:, d:2 * d]).astype(bf16)
            dd[pl.ds(pl.multiple_of(i * 8, 8), 8), :] = jnp.broadcast_to(jnp.exp(g_last), (8, d))
        return carry

    lax.fori_loop(0, nb * n_chunks // DN_PREP_UNROLL, chunk_prep, 0)

    def chunk_scan(i, states):
        ids = [b * n_chunks + i for b in range(nb)]
        blocks = [pl.ds(pl.multiple_of(j * d, d), d) for j in ids]
        sbs = [s.astype(bf16) for s in states]
        for blk, sb in zip(blocks, sbs):
            ss[blk, :] = sb
        prods = [jnp.dot(mm[blk, :], sb, preferred_element_type=jnp.float32) for blk, sb in zip(blocks, sbs)]
        return tuple(s * dd[pl.ds(pl.multiple_of(j * 8, 8), 1), :] - pr + nn[blk, :]
                     for s, j, blk, pr in zip(states, ids, blocks, prods))

    lax.fori_loop(0, n_chunks, chunk_scan, tuple(jnp.zeros((d, d), jnp.float32) for _ in range(nb)))

    nw = nw_ref[...]

    def chunk_out(it, carry):
        ids = [it * DN_OUT_UNROLL + cc for cc in range(DN_OUT_UNROLL)]
        rows = [pl.ds(pl.multiple_of(i * c, c), c) for i in ids]
        outs = [jnp.dot(qq[r, :], ss[pl.ds(pl.multiple_of(i * d, d), d), :], preferred_element_type=jnp.float32)
                + oo[r, :] for i, r in zip(ids, rows)]
        for r, o in zip(rows, outs):
            o = o * lax.rsqrt(jnp.mean(o * o, axis=-1, keepdims=True) + 1e-6) * nw
            z = z_ref[r, :]
            o_ref[r, :] = (o * (z * jax.nn.sigmoid(z))).astype(o_ref.dtype)
        return carry

    lax.fori_loop(0, nb * n_chunks // DN_OUT_UNROLL, chunk_out, 0)


def _deltanet(pf, conv_w, a_log, dt_bias, norm_w, bsz, seq):
    d = DN_HEAD_DIM
    nb = DN_BATCHES if bsz % DN_BATCHES == 0 else 1
    rows = nb * seq
    n_chunks = rows // DN_CHUNK
    qkv0 = PF_QKVB // d
    scal = jnp.stack([a_log, dt_bias]).astype(jnp.float32)
    f32 = jnp.float32
    bf16 = jnp.bfloat16
    cw4 = conv_w.astype(f32).reshape(DN_CONV, 3, DN_HEADS, d).transpose(2, 1, 0, 3)
    assert 2 * DN_CHUNK == d and n_chunks % DN_PREP_UNROLL == 0 and n_chunks % DN_OUT_UNROLL == 0
    return pl.pallas_call(
        functools.partial(_dn_kernel, seq=seq),
        grid=(bsz // nb, DN_HEADS),
        in_specs=[
            pl.BlockSpec(memory_space=pltpu.SMEM),
            pl.BlockSpec((rows, d), lambda b, h: (b, qkv0 + h)),
            pl.BlockSpec((rows, d), lambda b, h: (b, qkv0 + DN_HEADS + h)),
            pl.BlockSpec((rows, d), lambda b, h: (b, qkv0 + 2 * DN_HEADS + h)),
            pl.BlockSpec((rows, LANES), lambda b, h: (b, PF_SMALL // LANES)),
            pl.BlockSpec((rows, d), lambda b, h: (b, PF_ZB // d + h)),
            pl.BlockSpec((None, 3, DN_CONV, d), lambda b, h: (h, 0, 0, 0)),
            pl.BlockSpec((1, d), lambda b, h: (0, 0)),
        ],
        out_specs=pl.BlockSpec((rows, d), lambda b, h: (b, h)),
        out_shape=jax.ShapeDtypeStruct((bsz * seq, DN_WIDTH), jnp.bfloat16),
        scratch_shapes=[
            pltpu.VMEM((rows, d), f32), pltpu.VMEM((rows, d), f32), pltpu.VMEM((rows, d), f32),
            pltpu.VMEM((rows, d), f32), pltpu.VMEM((rows, d), f32),
            pltpu.VMEM((n_chunks * d, d), bf16), pltpu.VMEM((n_chunks * d, d), f32),
            pltpu.VMEM((rows, d), bf16), pltpu.VMEM((rows, d), f32),
            pltpu.VMEM((n_chunks * 8, d), f32),
            pltpu.VMEM((n_chunks * d, d), bf16),
        ],
        compiler_params=_cparams(2), name="deltanet",
    )(scal, pf, pf, pf, pf, pf, cw4, norm_w.astype(f32).reshape(1, d))


def _out_kernel(oa_ref, ob_ref, gma_ref, gmb_ref, x_ref, p_ref, wa_ref, wb_ref, wo_ref, wpg_ref, wp_ref,
                lng_ref, lnb_ref, o_ref):
    f32 = jnp.float32
    bf16 = jnp.bfloat16
    tm = o_ref.shape[0]
    sub = tm // OUT_SUBTILES
    parts = [slice(i * sub, (i + 1) * sub) for i in range(OUT_SUBTILES)]
    y_a = [jnp.dot(oa_ref[r, :], wa_ref[...], preferred_element_type=f32) for r in parts]
    y_b = [jnp.dot(ob_ref[r, :], wb_ref[...], preferred_element_type=f32) for r in parts]
    pw = [jnp.dot(p_ref[r, :].astype(bf16), wp_ref[...], preferred_element_type=f32) for r in parts]
    mix = [(jax.nn.sigmoid(gma_ref[r, :]) * ya + jax.nn.sigmoid(gmb_ref[r, :]) * yb).astype(bf16)
           for r, ya, yb in zip(parts, y_a, y_b)]
    h = [DEEPNORM_ALPHA * x_ref[r, :] + jnp.dot(mx, wo_ref[...], preferred_element_type=f32)
         for r, mx in zip(parts, mix)]
    gate = [jnp.dot(hh.astype(bf16), wpg_ref[...], preferred_element_type=f32) for hh in h]
    for r, hh, gt, pp in zip(parts, h, gate, pw):
        hh = hh + jax.nn.sigmoid(gt) * pp
        mu = jnp.mean(hh, axis=-1, keepdims=True)
        hc = hh - mu
        var = jnp.mean(hc * hc, axis=-1, keepdims=True)
        o_ref[r, :] = (hc * lax.rsqrt(var + 1e-5) * lng_ref[...] + lnb_ref[...]).astype(o_ref.dtype)


def _out_block(o_a, o_b, pf, x2, p2, wa, wb, wo, wpg, wp, ln_g, ln_b, tm):
    t = x2.shape[0]
    bf = jnp.bfloat16

    def full(shape):
        return pl.BlockSpec(shape, lambda i: (0, 0))

    return pl.pallas_call(
        _out_kernel,
        grid=(t // tm,),
        in_specs=[
            pl.BlockSpec((tm, NSA_WIDTH), lambda i: (i, 0)),
            pl.BlockSpec((tm, DN_WIDTH), lambda i: (i, 0)),
            pl.BlockSpec((tm, D_MODEL), lambda i: (i, PF_GM // D_MODEL)),
            pl.BlockSpec((tm, D_MODEL), lambda i: (i, PF_GM // D_MODEL + 1)),
            pl.BlockSpec((tm, D_MODEL), lambda i: (i, 0)),
            pl.BlockSpec((tm, PLE_DIM), lambda i: (i, 0)),
            full((NSA_WIDTH, D_MODEL)), full((DN_WIDTH, D_MODEL)), full((D_MODEL, D_MODEL)),
            full((D_MODEL, D_MODEL)), full((PLE_DIM, D_MODEL)), full((1, D_MODEL)), full((1, D_MODEL)),
        ],
        out_specs=pl.BlockSpec((tm, D_MODEL), lambda i: (i, 0)),
        out_shape=jax.ShapeDtypeStruct((t, D_MODEL), x2.dtype),
        compiler_params=_cparams(1), name="out_block",
    )(o_a, o_b, pf, pf, x2, p2, wa.astype(bf), wb.astype(bf), wo.astype(bf), wpg.astype(bf), wp.astype(bf),
      ln_g.astype(jnp.float32).reshape(1, D_MODEL), ln_b.astype(jnp.float32).reshape(1, D_MODEL))


def _layer(x, p, w_in, pos_k, pos_v, w1_k, w2_k, w1_v, w2_v, bias_tabs, conv_w, a_log, dt_bias, norm_w,
           w_a, w_b, w_o, w_ple, w_pg, ln_g, ln_b):
    bsz, seq, _ = x.shape
    t = bsz * seq
    x2 = x.reshape(t, D_MODEL)
    wb16, wf16 = _prep_w_in(w_in)
    pb, pf = _proj(x2, wb16, wf16, PROJ_TM if t % PROJ_TM == 0 else seq)

    pos2, w2p = _prep_compress_weights(pos_k, pos_v, w2_k, w2_v)
    kcvc = _compress(pf, pos2, w1_k, w1_v, w2p, bsz, seq)

    bias_sel, bias_win, bias_cmp = bias_tabs
    o_a = _nsa(pb, pf, kcvc, bias_sel, bias_win, bias_cmp, bsz, seq)
    o_b = _deltanet(pf, conv_w, a_log, dt_bias, norm_w, bsz, seq)
    out = _out_block(o_a, o_b, pf, x2, p.reshape(t, PLE_DIM), w_a, w_b, w_o, w_pg, w_ple, ln_g, ln_b,
                     OUT_TM if t % OUT_TM == 0 else seq)
    return out.reshape(bsz, seq, D_MODEL)


def kernel(x, p, w_in, cmp_pos_k, cmp_pos_v, cmp_w1_k, cmp_w2_k, cmp_w1_v, cmp_w2_v, rel_bias, dn_conv_w,
           dn_a_log, dn_dt_bias, dn_norm_w, w_branch_a, w_branch_b, w_out, w_ple, w_ple_gate, ln_g, ln_b):
    depth = w_in.shape[0]
    bias_tabs = _bias_tables(rel_bias, x.shape[1])
    for i in range(depth):
        x = _layer(x, p[i], w_in[i], cmp_pos_k[i], cmp_pos_v[i], cmp_w1_k[i], cmp_w2_k[i], cmp_w1_v[i],
                   cmp_w2_v[i], bias_tabs, dn_conv_w[i], dn_a_log[i], dn_dt_bias[i], dn_norm_w[i],
                   w_branch_a[i], w_branch_b[i], w_out[i], w_ple[i], w_ple_gate[i], ln_g[i], ln_b[i])
    return x
```

```python
import functools
import math

import numpy as np
import jax
import jax.numpy as jnp
from jax import lax
from jax.experimental import pallas as pl
from jax.experimental.pallas import tpu as pltpu

D_MODEL = 1024
PLE_DIM = 256
NSA_HEADS = 8
NSA_GROUPS = 2
NSA_HPG = NSA_HEADS // NSA_GROUPS
NSA_HEAD_DIM = 64
NSA_WIDTH = NSA_HEADS * NSA_HEAD_DIM
NSA_KV = NSA_GROUPS * NSA_HEAD_DIM
CMP_BLOCK = 32
CMP_STRIDE = 16
CMP_HIDDEN = 256
SEL_BLOCK = 64
SEL_TOPK = 8
WINDOW = 512
DN_HEADS = 4
DN_HEAD_DIM = 128
DN_WIDTH = DN_HEADS * DN_HEAD_DIM
DN_CONV = 4
DN_CHUNK = 64
NUM_BUCKETS = 32
REL_MAX_DIST = 1024
DEEPNORM_ALPHA = 2.0 ** 0.25
NEG = -1e30
FORCE = 1e6
LOG2E = 1.4426950408889634

LANES = 128
TILE = 128
SEL_LANE0 = 64
SEL_GROUP = 4
NSA_TILES_PER_STEP = 4
NSA_V_ROWS = NSA_HEAD_DIM + 16
VMEM_LIMIT = 56 * 1024 * 1024
PROJ_TM = 512
PROJ_TN = 1024
OUT_TM = 512
OUT_SUBTILES = 2
DN_INV_BLOCK = 16
DN_BATCHES = 2
DN_OUT_UNROLL = 8
DN_DECAY_PARTS = 2
DN_PREP_UNROLL = 32

HIGHEST = lax.Precision.HIGHEST

PB_QP = 0
PB_KS = PB_QP + NSA_WIDTH
PB_KW = PB_KS + NSA_KV
PB_VS = PB_KW + NSA_KV
PB_VW = PB_VS + NSA_KV
PB_WIDTH = PB_VW + NSA_KV
PF_GM = 0
PF_QKVB = PF_GM + 2 * D_MODEL
PF_ZA = PF_QKVB + 3 * DN_WIDTH
PF_ZB = PF_ZA + NSA_WIDTH
PF_KC = PF_ZB + DN_WIDTH
PF_VC = PF_KC + NSA_KV
PF_SMALL = PF_VC + NSA_KV
PF_WIDTH = PF_SMALL + LANES
SMALL_BETA = 3 * NSA_HEADS
SMALL_A = SMALL_BETA + DN_HEADS


def _bucket_thresholds():
    max_exact = NUM_BUCKETS // 2
    span = NUM_BUCKETS - max_exact
    ratio = REL_MAX_DIST // max_exact
    thr = list(range(1, max_exact + 1))
    for k in range(1, span):
        n = max_exact
        while n ** span < max_exact ** span * ratio ** k:
            n += 1
        thr.append(n)
    return tuple(thr)


_THR = _bucket_thresholds()


def _cparams(n_axes):
    return pltpu.CompilerParams(dimension_semantics=("arbitrary",) * n_axes, vmem_limit_bytes=VMEM_LIMIT)


def _proj_kernel(x_ref, wb_ref, wf_ref, pb_ref, pf_ref):
    nt_dims = (((1,), (1,)), ((), ()))
    xb = x_ref[...].astype(jnp.bfloat16)
    pb_ref[...] = lax.dot_general(xb, wb_ref[...], nt_dims, preferred_element_type=jnp.float32).astype(pb_ref.dtype)
    n = pf_ref.shape[1]
    for c0 in range(0, n, PROJ_TN):
        c1 = min(c0 + PROJ_TN, n)
        pf_ref[:, c0:c1] = lax.dot_general(xb, wf_ref[c0:c1, :], nt_dims, preferred_element_type=jnp.float32)


def _proj(x2, wb, wf, tm):
    t, d = x2.shape
    resident = dict(pipeline_mode=pl.Buffered(1))
    return pl.pallas_call(
        _proj_kernel,
        grid=(t // tm,),
        in_specs=[pl.BlockSpec((tm, d), lambda i: (i, 0)),
                  pl.BlockSpec((PB_WIDTH, d), lambda i: (0, 0), **resident),
                  pl.BlockSpec((PF_WIDTH, d), lambda i: (0, 0), **resident)],
        out_specs=[pl.BlockSpec((tm, PB_WIDTH), lambda i: (i, 0)),
                   pl.BlockSpec((tm, PF_WIDTH), lambda i: (i, 0))],
        out_shape=[jax.ShapeDtypeStruct((t, PB_WIDTH), jnp.bfloat16),
                   jax.ShapeDtypeStruct((t, PF_WIDTH), jnp.float32)],
        compiler_params=_cparams(1), name="proj",
    )(x2, wb, wf)


def _prep_w_in(w):
    d = w.shape[0]
    wt = jnp.swapaxes(w, 0, 1)
    o = 0
    wq = wt[o:o + NSA_WIDTH]; o += NSA_WIDTH
    wkv = wt[o:o + 6 * NSA_KV]; o += 6 * NSA_KV
    wg = wt[o:o + 3 * NSA_HEADS]; o += 3 * NSA_HEADS
    wza = wt[o:o + NSA_WIDTH]; o += NSA_WIDTH
    wqkvb = wt[o:o + 3 * DN_WIDTH]; o += 3 * DN_WIDTH
    wbeta_a = wt[o:o + 2 * DN_HEADS]; o += 2 * DN_HEADS
    wzb = wt[o:o + DN_WIDTH]; o += DN_WIDTH
    wgm = wt[o:o + 2 * D_MODEL]
    wkcvc, wks, wvs, wkw, wvw = (wkv[0:2 * NSA_KV], wkv[2 * NSA_KV:3 * NSA_KV], wkv[3 * NSA_KV:4 * NSA_KV],
                                 wkv[4 * NSA_KV:5 * NSA_KV], wkv[5 * NSA_KV:6 * NSA_KV])
    wb = jnp.concatenate([wq * (NSA_HEAD_DIM ** -0.5 * LOG2E), wks, wkw, wvs, wvw], axis=0).astype(jnp.bfloat16)
    pad = jnp.zeros((LANES - 3 * NSA_HEADS - 2 * DN_HEADS, d), w.dtype)
    wf = jnp.concatenate([wgm, wqkvb, wza, wzb, wkcvc, wg, wbeta_a, pad], axis=0).astype(jnp.bfloat16)
    return wb, wf


def _bias_kernel(tab_ref, sel_ref, win_ref, cmp_ref, *, n_cmp):
    h = pl.program_id(0)

    def lookup(n):
        val = jnp.full(n.shape, tab_ref[0, h], jnp.float32)
        for b in range(1, NUM_BUCKETS):
            val = jnp.where(n >= _THR[b - 1], tab_ref[b, h], val)
        return val * LOG2E

    kj = lax.broadcasted_iota(jnp.int32, (TILE, TILE), 0)
    qi = lax.broadcasted_iota(jnp.int32, (TILE, TILE), 1)
    n_sel_tiles = sel_ref.shape[1] - 1
    n_win_tiles = win_ref.shape[1] - 1
    for dt in range(max(n_sel_tiles, n_win_tiles)):
        dist = dt * TILE + qi - kj
        v = lookup(jnp.maximum(dist, 0))
        if dt < n_sel_tiles:
            sel_ref[0, dt] = jnp.where(dist >= 0, v, NEG)
        if dt < n_win_tiles:
            win_ref[0, dt] = jnp.where((dist >= 0) & (dist < WINDOW), v, NEG)
    sel_ref[0, n_sel_tiles] = jnp.full((TILE, TILE), NEG, jnp.float32)
    win_ref[0, n_win_tiles] = jnp.full((TILE, TILE), NEG, jnp.float32)
    c = lax.broadcasted_iota(jnp.int32, cmp_ref.shape[2:], 0)
    for t in range(cmp_ref.shape[1]):
        s = t * TILE + lax.broadcasted_iota(jnp.int32, cmp_ref.shape[2:], 1)
        dist = s - (c * CMP_STRIDE + CMP_BLOCK - 1)
        cmp_ref[0, t] = jnp.where((dist >= 0) & (c < n_cmp), lookup(jnp.maximum(dist, 0)), NEG)


def _bias_tables(rel_bias, seq):
    nq = seq // TILE
    nwin = WINDOW // TILE + 1
    nc = seq // CMP_STRIDE
    n_cmp = nc - CMP_BLOCK // CMP_STRIDE + 1
    return pl.pallas_call(
        functools.partial(_bias_kernel, n_cmp=n_cmp),
        grid=(NSA_HEADS,),
        in_specs=[pl.BlockSpec(memory_space=pltpu.SMEM)],
        out_specs=[pl.BlockSpec((1, nq + 1, TILE, TILE), lambda h: (h, 0, 0, 0)),
                   pl.BlockSpec((1, nwin + 1, TILE, TILE), lambda h: (h, 0, 0, 0)),
                   pl.BlockSpec((1, nq, nc, TILE), lambda h: (h, 0, 0, 0))],
        out_shape=[jax.ShapeDtypeStruct((NSA_HEADS, nq + 1, TILE, TILE), jnp.float32),
                   jax.ShapeDtypeStruct((NSA_HEADS, nwin + 1, TILE, TILE), jnp.float32),
                   jax.ShapeDtypeStruct((NSA_HEADS, nq, nc, TILE), jnp.float32)],
        compiler_params=_cparams(1), name="bias_tables",
    )(rel_bias.astype(jnp.float32))


def _gelu_tanh(x):
    return x * (0.5 * (1.0 + jnp.tanh(math.sqrt(2.0 / math.pi) * (x + 0.044715 * (x * x * x)))))


def _compress_kernel(x_ref, pos_ref, w1k_ref, w1v_ref, w2_ref, o_ref, wbd):
    kv = pl.program_id(0)
    nc = o_ref.shape[2]
    half = CMP_BLOCK // 2
    dh = NSA_HEAD_DIM
    bf16 = jnp.bfloat16

    @pl.when(pl.program_id(1) == 0)
    def _():
        def build(w1_ref):
            zero = jnp.zeros((dh, CMP_HIDDEN), bf16)
            for a in range(2):
                for l in range(half):
                    r0 = (a * half + l) * dh
                    wl = w1_ref[r0:r0 + dh, :].astype(bf16)
                    wbd[a, l * NSA_KV:(l + 1) * NSA_KV, :] = jnp.concatenate(
                        [jnp.concatenate([wl, zero], axis=1), jnp.concatenate([zero, wl], axis=1)], axis=0)
        pl.when(kv == 0)(functools.partial(build, w1k_ref))
        pl.when(kv == 1)(functools.partial(build, w1v_ref))

    r = jnp.concatenate([x_ref[pl.ds(l, nc, stride=CMP_STRIDE), :] for l in range(CMP_STRIDE)], axis=1)
    a = jnp.dot((r + pos_ref[0, 0:1, :]).astype(bf16), wbd[0], preferred_element_type=jnp.float32)
    b = jnp.dot((r + pos_ref[0, 1:2, :]).astype(bf16), wbd[1], preferred_element_type=jnp.float32)
    hid = a + pltpu.roll(b, nc - 1, 0)
    o_ref[0, 0] = jnp.dot(_gelu_tanh(hid).astype(bf16), w2_ref[0], preferred_element_type=jnp.float32)


def _compress(pf, pos2, w1_k, w1_v, w2p, bsz, seq):
    nc = seq // CMP_STRIDE
    width = CMP_STRIDE * NSA_KV
    hid = NSA_GROUPS * CMP_HIDDEN
    assert CMP_BLOCK == 2 * CMP_STRIDE
    return pl.pallas_call(
        _compress_kernel,
        grid=(2, bsz),
        in_specs=[pl.BlockSpec((seq, NSA_KV), lambda k, b: (b, PF_KC // NSA_KV + k)),
                  pl.BlockSpec((1, 2, width), lambda k, b: (k, 0, 0)),
                  pl.BlockSpec(w1_k.shape, lambda k, b: (0, 0)),
                  pl.BlockSpec(w1_v.shape, lambda k, b: (0, 0)),
                  pl.BlockSpec((1, hid, LANES), lambda k, b: (k, 0, 0))],
        out_specs=pl.BlockSpec((1, 1, nc, LANES), lambda k, b: (b, k, 0, 0)),
        out_shape=jax.ShapeDtypeStruct((bsz, 2, nc, LANES), jnp.float32),
        scratch_shapes=[pltpu.VMEM((2, width, hid), jnp.bfloat16)],
        compiler_params=_cparams(2), name="compress",
    )(pf, pos2, w1_k, w1_v, w2p)


def _prep_compress_weights(pos_k, pos_v, w2_k, w2_v):
    eye = jnp.eye(NSA_GROUPS, dtype=jnp.float32)
    half = CMP_BLOCK // 2

    def w2_both(w2):
        return jnp.einsum('jd,gh->gjhd', w2, eye).reshape(NSA_GROUPS * CMP_HIDDEN, NSA_GROUPS * NSA_HEAD_DIM)

    def pos_both(pos):
        p = pos.reshape(2, half, 1, NSA_HEAD_DIM)
        return jnp.broadcast_to(p, (2, half, NSA_GROUPS, NSA_HEAD_DIM)).reshape(2, half * NSA_KV)

    pos2 = jnp.stack([pos_both(pos_k), pos_both(pos_v)]).astype(jnp.float32)
    w2p = jnp.stack([w2_both(w2_k), w2_both(w2_v)]).astype(jnp.bfloat16)
    return pos2, w2p


def _pair_lanes(x, g):
    sw = pltpu.roll(x, LANES // 2, 1)
    lane = lax.broadcasted_iota(jnp.int32, x.shape, 1)
    own = (lane < LANES // 2) == (g == 0)
    return jnp.where(own, x, sw)


def _group_rows_t(x, g):
    xt = x.T
    half = LANES // 2
    return jnp.where(g == 0, xt[0:half, :], xt[half:LANES, :])


def _nsa_step_kernel(q_ref, qn_ref, ks_ref, kw_ref, vs_ref, vw_ref, kcvc_ref, bc_ref, bsel_ref, bwin_ref,
                     small_ref, za_ref, o_ref,
                     ksa, kwp, vst, vwt, kcs, vct, qa_s, qp_s, ocmp_s, sbuf, *, n_cmp, n_sel, n_top):
    g = pl.program_id(0)
    step_id = pl.program_id(2)
    tq = TILE
    dh = NSA_HEAD_DIM
    seq = ks_ref.shape[0]
    nq = seq // TILE
    nc = kcvc_ref.shape[2]
    n_wt = bwin_ref.shape[1] - 1
    heads = range(NSA_HPG)
    f32 = jnp.float32
    bf16 = jnp.bfloat16
    nt_dims = (((1,), (1,)), ((), ()))
    gk = SEL_GROUP * TILE

    def select_tile(src_ref, r0, tile, slot):
        q32 = src_ref[r0:r0 + tq, :].astype(f32)
        low_half = lax.broadcasted_iota(jnp.int32, (tq, LANES), 1) < NSA_HEAD_DIM
        qh = []
        for hh in heads:
            blk = q32[:, (hh // 2) * LANES:(hh // 2 + 1) * LANES]
            if hh % 2:
                blk = pltpu.roll(blk, LANES // 2, 1)
            qh.append(jnp.where(low_half, blk, 0.0).astype(bf16))
        for hh in heads:
            qp_s[slot, hh * tq:(hh + 1) * tq, :] = qh[hh]
        q_all = qp_s[slot]
        s_c = lax.dot_general(kcs[...], q_all, nt_dims, preferred_element_type=f32)
        bias_c = jnp.concatenate([bc_ref[hh, tile] for hh in heads], axis=1)
        valid = bias_c > 0.5 * NEG
        s_c = s_c + bias_c
        e = jnp.where(valid, jnp.exp2(s_c - jnp.max(s_c, axis=0, keepdims=True)), 0.0)
        den = jnp.maximum(jnp.sum(e, axis=0, keepdims=True), 1e-30)
        p_c = e * (1.0 / den)
        psum = sum(p_c[:, hh * tq:(hh + 1) * tq] for hh in heads)
        sj = lax.broadcasted_iota(jnp.int32, (n_sel, nc), 0)
        ci = lax.broadcasted_iota(jnp.int32, (n_sel, nc), 1)
        overlap = ((ci * CMP_STRIDE < (sj + 1) * SEL_BLOCK) & (ci * CMP_STRIDE + CMP_BLOCK > sj * SEL_BLOCK)
                   & (ci < n_cmp)).astype(bf16)
        imp_t = sum(jnp.dot(overlap, part, preferred_element_type=f32)
                    for part in _split_bf16(psum, 3))
        ocmp_s[slot] = jnp.dot(vct[...], p_c.astype(bf16), preferred_element_type=f32)
        blk = lax.broadcasted_iota(jnp.int32, (n_sel, tq), 0)
        cur = (tile * tq + lax.broadcasted_iota(jnp.int32, (n_sel, tq), 1)) >> 6
        forced = (blk == 0) | (blk == cur) | (blk == cur - 1)
        imp_t = jnp.where(forced, FORCE, jnp.where(blk > cur, -FORCE, imp_t))
        rank = jnp.zeros((n_sel, tq), jnp.int32)
        for j in range(n_sel):
            other = imp_t[j:j + 1, :]
            ahead = (other > imp_t) | ((other == imp_t) & (blk > j))
            rank = rank + ahead.astype(jnp.int32)
        selb = jnp.where(rank < n_top, 0.0, NEG)
        pieces = [jnp.zeros((SEL_LANE0, tq), f32), selb]
        if LANES - SEL_LANE0 - n_sel > 0:
            pieces.append(jnp.zeros((LANES - SEL_LANE0 - n_sel, tq), f32))
        selb_r = jnp.concatenate(pieces, axis=0).T.astype(bf16)
        for hh in heads:
            qa_s[slot, hh * tq:(hh + 1) * tq, :] = qh[hh] + selb_r

    @pl.when(step_id == 0)
    def _():
        row = lax.broadcasted_iota(jnp.int32, (seq, LANES), 0)
        lane = lax.broadcasted_iota(jnp.int32, (seq, LANES), 1)
        onehot = (lane - SEL_LANE0) == (row >> 6)
        own = lane < NSA_HEAD_DIM
        ks_g = jnp.where(own, _pair_lanes(ks_ref[...].astype(f32), g), 0.0)
        ksa[...] = jnp.where(onehot, 1.0, ks_g).astype(bf16)
        pad = (n_wt - 1) * TILE
        kwp[0:pad, :] = jnp.zeros((pad, LANES), bf16)
        kwp[pad:pad + seq, :] = jnp.where(own, _pair_lanes(kw_ref[...].astype(f32), g), 0.0).astype(bf16)
        for kt in range(n_wt - 1):
            vwt[kt] = jnp.zeros(vwt.shape[1:], bf16)
        extra = vst.shape[1] - dh
        ones_row = (lax.broadcasted_iota(jnp.int32, (extra, TILE), 0) == 0).astype(bf16)
        for kt in range(nq):
            rows = slice(kt * TILE, (kt + 1) * TILE)
            vst[kt, 0:dh, :] = _group_rows_t(vs_ref[rows, :].astype(f32), g).astype(bf16)
            vst[kt, dh:dh + extra, :] = ones_row
            vwt[kt + n_wt - 1, 0:dh, :] = _group_rows_t(vw_ref[rows, :].astype(f32), g).astype(bf16)
            vwt[kt + n_wt - 1, dh:dh + extra, :] = ones_row
        kcs[...] = _pair_lanes(kcvc_ref[0, 0], g).astype(bf16)
        for ct in range(nc // TILE):
            rows = slice(ct * TILE, (ct + 1) * TILE)
            vct[:, rows] = _group_rows_t(kcvc_ref[0, 1, rows, :], g).astype(bf16)
        select_tile(q_ref, 0, 0, 0)

    tps = NSA_TILES_PER_STEP

    def one_tile(n_groups, slot):
        qt = step_id * tps + slot
        rows = slice(slot * tq, (slot + 1) * tq)
        q_all = qp_s[slot]
        q_aug = qa_s[slot]
        kw_rows = kwp[pl.ds(pl.multiple_of(qt * TILE, TILE), n_wt * TILE), :]
        s_w = lax.dot_general(kw_rows, q_all, nt_dims, preferred_element_type=f32)

        def sel_scores(gi):
            s = lax.dot_general(ksa[gi * gk:(gi + 1) * gk, :], q_aug, nt_dims, preferred_element_type=f32)
            tiles = []
            for t in range(SEL_GROUP):
                dt = qt - (gi * SEL_GROUP + t)
                idx = jnp.where(dt >= 0, dt, nq)
                tiles.append(jnp.concatenate([bsel_ref[hh, idx] for hh in heads], axis=1))
            sbuf[gi % 2] = s + jnp.concatenate(tiles, axis=0)

        sel_scores(0)
        if slot + 1 < tps:
            select_tile(q_ref, (slot + 1) * tq, qt + 1, slot + 1)
        else:
            select_tile(qn_ref, 0, jnp.minimum(qt + 1, nq - 1), 0)

        w_tiles = []
        for t in range(n_wt):
            dt = n_wt - 1 - t
            idx = jnp.where(qt >= dt, dt, n_wt)
            w_tiles.append(jnp.concatenate([bwin_ref[hh, idx] for hh in heads], axis=1))
        s_w = s_w + jnp.concatenate(w_tiles, axis=0)
        p_w = jnp.exp2(s_w - jnp.max(s_w, axis=0, keepdims=True))
        v_w = jnp.concatenate([vwt[qt + t] for t in range(n_wt)], axis=1)
        acc_w = jnp.dot(v_w, p_w.astype(bf16), preferred_element_type=f32)
        o_win = acc_w[0:dh, :] * (1.0 / acc_w[dh:dh + 1, :])

        m = jnp.full((1, NSA_HPG * tq), NEG, f32)
        acc = jnp.zeros((vst.shape[1], NSA_HPG * tq), f32)
        for gi in range(n_groups):
            if gi + 1 < n_groups:
                sel_scores(gi + 1)
            s = sbuf[gi % 2]
            m_new = jnp.maximum(m, jnp.max(s, axis=0, keepdims=True))
            alpha = jnp.exp2(m - m_new)
            p = jnp.exp2(s - m_new)
            v_t = jnp.concatenate([vst[gi * SEL_GROUP + t] for t in range(SEL_GROUP)], axis=1)
            acc = alpha * acc + jnp.dot(v_t, p.astype(bf16), preferred_element_type=f32)
            m = m_new
        o_sel = acc[0:dh, :] * (1.0 / acc[dh:dh + 1, :])
        o_cmp = ocmp_s[slot]

        gates_t = jax.nn.sigmoid(small_ref[rows, :]).T
        mixed = []
        for hh in heads:
            def gate(br):
                r = br * NSA_HEADS + hh
                return jnp.where(g == 0, gates_t[r:r + 1, :], gates_t[r + NSA_HPG:r + NSA_HPG + 1, :])
            cols = slice(hh * tq, (hh + 1) * tq)
            mixed.append(gate(0) * o_cmp[:, cols] + gate(1) * o_sel[:, cols] + gate(2) * o_win[:, cols])
        o = jnp.concatenate([jnp.concatenate(mixed[2 * j:2 * j + 2], axis=0).T for j in range(NSA_HPG // 2)],
                            axis=1)
        z = za_ref[rows, :]
        o_ref[rows, :] = (o * (z * jax.nn.sigmoid(z))).astype(o_ref.dtype)

    def step(n_groups):
        for slot in range(tps):
            one_tile(n_groups, slot)

    for n_groups in range(1, nq // SEL_GROUP + 1):
        pl.when(step_id // (SEL_GROUP // tps) == n_groups - 1)(functools.partial(step, n_groups))


def _nsa(pb, pf, kcvc, bias_sel, bias_win, bias_cmp, bsz, seq):
    nq = seq // TILE
    nc = seq // CMP_STRIDE
    n_cmp = nc - CMP_BLOCK // CMP_STRIDE + 1
    n_sel = seq // SEL_BLOCK
    n_top = min(SEL_TOPK, n_sel)
    nwin = bias_win.shape[1]
    tps = NSA_TILES_PER_STEP
    assert nq % SEL_GROUP == 0 and nc % TILE == 0 and bias_sel.shape[1] == nq + 1
    assert tps >= 2 and SEL_GROUP % tps == 0
    ns = nq // tps
    rows = tps * TILE
    gw = NSA_HPG * NSA_HEAD_DIM
    kern = functools.partial(_nsa_step_kernel, n_cmp=n_cmp, n_sel=n_sel, n_top=n_top)
    return pl.pallas_call(
        kern,
        grid=(NSA_GROUPS, bsz, ns),
        in_specs=[
            pl.BlockSpec((rows, gw), lambda g, b, t: (b * ns + t, PB_QP // gw + g)),
            pl.BlockSpec((TILE, gw), lambda g, b, t: (b * nq + jnp.minimum((t + 1) * tps, nq - 1),
                                                      PB_QP // gw + g)),
            pl.BlockSpec((seq, LANES), lambda g, b, t: (b, PB_KS // LANES)),
            pl.BlockSpec((seq, LANES), lambda g, b, t: (b, PB_KW // LANES)),
            pl.BlockSpec((seq, LANES), lambda g, b, t: (b, PB_VS // LANES)),
            pl.BlockSpec((seq, LANES), lambda g, b, t: (b, PB_VW // LANES)),
            pl.BlockSpec((1, 2, nc, LANES), lambda g, b, t: (b, 0, 0, 0)),
            pl.BlockSpec((NSA_HPG, nq, nc, TILE), lambda g, b, t: (g, 0, 0, 0)),
            pl.BlockSpec((NSA_HPG, nq + 1, TILE, TILE), lambda g, b, t: (g, 0, 0, 0)),
            pl.BlockSpec((NSA_HPG, nwin, TILE, TILE), lambda g, b, t: (g, 0, 0, 0)),
            pl.BlockSpec((rows, LANES), lambda g, b, t: (b * ns + t, PF_SMALL // LANES)),
            pl.BlockSpec((rows, gw), lambda g, b, t: (b * ns + t, PF_ZA // gw + g)),
        ],
        out_specs=pl.BlockSpec((rows, gw), lambda g, b, t: (b * ns + t, g)),
        out_shape=jax.ShapeDtypeStruct((bsz * seq, NSA_WIDTH), jnp.bfloat16),
        scratch_shapes=[
            pltpu.VMEM((seq, LANES), jnp.bfloat16),
            pltpu.VMEM((seq + (nwin - 2) * TILE, LANES), jnp.bfloat16),
            pltpu.VMEM((nq, NSA_V_ROWS, TILE), jnp.bfloat16),
            pltpu.VMEM((nq + nwin - 2, NSA_V_ROWS, TILE), jnp.bfloat16),
            pltpu.VMEM((nc, LANES), jnp.bfloat16),
            pltpu.VMEM((NSA_HEAD_DIM, nc), jnp.bfloat16),
            pltpu.VMEM((tps, NSA_HPG * TILE, LANES), jnp.bfloat16),
            pltpu.VMEM((tps, NSA_HPG * TILE, LANES), jnp.bfloat16),
            pltpu.VMEM((tps, NSA_HEAD_DIM, NSA_HPG * TILE), jnp.float32),
            pltpu.VMEM((2, SEL_GROUP * TILE, NSA_HPG * TILE), jnp.float32),
        ],
        compiler_params=_cparams(3), name="nsa",
    )(pb, pb, pb, pb, pb, pb, kcvc, bias_cmp, bias_sel, bias_win, pf, pf)


def _split_bf16(a, n):
    parts = []
    for _ in range(n - 1):
        hi = a.astype(jnp.bfloat16)
        parts.append(hi)
        a = a - hi.astype(jnp.float32)
    parts.append(a.astype(jnp.bfloat16))
    return parts


def _dot3(a, b):
    ah, al = _split_bf16(a, 2)
    bh, bl = _split_bf16(b, 2)
    f32 = jnp.float32
    return (jnp.dot(ah, bh, preferred_element_type=f32) + jnp.dot(al, bh, preferred_element_type=f32)
            + jnp.dot(ah, bl, preferred_element_type=f32))


def _softplus(x):
    return jnp.maximum(x, 0.0) + jnp.log(1.0 + jnp.exp(-jnp.abs(x)))


def _dn_kernel(scal_ref, q_ref, k_ref, v_ref, small_ref, z_ref, cw_ref, nw_ref, o_ref,
               qn, kn, vn, bet, gl, mm, nn, qq, oo, dd, ss, *, seq):
    h = pl.program_id(1)
    n_rows = q_ref.shape[0]
    nb = n_rows // seq
    c = DN_CHUNK
    n_chunks = seq // c
    d = DN_HEAD_DIM

    head = 8

    def conv_body(x_ref, which):
        n = n_rows - head
        y = x_ref[pl.ds(head, n), :] * cw_ref[which, DN_CONV - 1:DN_CONV, :]
        for j in range(DN_CONV - 1):
            y = y + x_ref[pl.ds(head - (DN_CONV - 1 - j), n), :] * cw_ref[which, j:j + 1, :]
        return y

    def conv_head(x_ref, which, r0):
        x = x_ref[pl.ds(r0, head), :]
        rowi = lax.broadcasted_iota(jnp.int32, (head, d), 0)
        y = x * cw_ref[which, DN_CONV - 1:DN_CONV, :]
        for j in range(DN_CONV - 1):
            sh = DN_CONV - 1 - j
            y = y + jnp.where(rowi >= sh, pltpu.roll(x, sh, 0), 0.0) * cw_ref[which, j:j + 1, :]
        return y

    def silu(y):
        return y * jax.nn.sigmoid(y)

    def l2n(t):
        return t * lax.rsqrt(jnp.sum(t * t, axis=-1, keepdims=True) + 1e-6)

    finish = (lambda y: l2n(silu(y)) * (d ** -0.5), lambda y: l2n(silu(y)), silu)
    for which, (x_ref, dst) in enumerate(((q_ref, qn), (k_ref, kn), (v_ref, vn))):
        dst[pl.ds(head, n_rows - head), :] = finish[which](conv_body(x_ref, which))
        for b in range(nb):
            dst[pl.ds(b * seq, head), :] = finish[which](conv_head(x_ref, which, b * seq))
    small = small_ref[...]
    lane = lax.broadcasted_iota(jnp.int32, small.shape, 1)
    beta_in = jnp.sum(jnp.where(lane == SMALL_BETA + h, small, 0.0), axis=-1, keepdims=True)
    a_in = jnp.sum(jnp.where(lane == SMALL_A + h, small, 0.0), axis=-1, keepdims=True)
    bet[...] = jnp.broadcast_to(jax.nn.sigmoid(beta_in), (n_rows, d))
    gl[...] = jnp.broadcast_to(-jnp.exp(scal_ref[0, h]) * _softplus(a_in + scal_ref[1, h]), (n_rows, d))

    f32 = jnp.float32
    bf16 = jnp.bfloat16
    nt_dims = (((1,), (1,)), ((), ()))
    ri = lax.broadcasted_iota(jnp.int32, (c, 2 * c), 0)
    lane2 = lax.broadcasted_iota(jnp.int32, (c, 2 * c), 1)
    first = lane2 < c
    cj = lane2 & (c - 1)
    incl = ri >= cj
    strict = ri > cj
    tril_b = (lax.broadcasted_iota(jnp.int32, (c, c), 0) >= lax.broadcasted_iota(jnp.int32, (c, c), 1)).astype(bf16)
    bs = DN_INV_BLOCK
    sh = bs.bit_length() - 1
    same_diag = (ri >> sh) == (cj >> sh)
    level_masks = []
    while (1 << sh) < c:
        level_masks.append(((ri >> (sh + 1)) == (cj >> (sh + 1))) & ((ri >> sh) > (cj >> sh)))
        sh += 1
    bd_r = lax.broadcasted_iota(jnp.int32, (2 * c, 2 * c), 0)
    bd_c = lax.broadcasted_iota(jnp.int32, (2 * c, 2 * c), 1)
    block_diag = (bd_r >= c) == (bd_c >= c)

    def dot_pair(a, b):
        bb = b.astype(bf16)
        bb = jnp.where(block_diag, jnp.concatenate([bb, bb], axis=0), jnp.zeros((), bf16))
        return jnp.dot(a.astype(bf16), bb, preferred_element_type=f32)

    def stack_diag(x0, x1):
        z = jnp.zeros_like(x0)
        return jnp.concatenate([jnp.concatenate([x0, z], axis=1), jnp.concatenate([z, x1], axis=1)], axis=0)

    def chunk_prep(it, carry):
        ids = [it * DN_PREP_UNROLL + cc for cc in range(DN_PREP_UNROLL)]
        rows = [pl.ds(pl.multiple_of(i * c, c), c) for i in ids]
        pairs = range(0, DN_PREP_UNROLL, 2)
        ks = [kn[r, :] for r in rows]
        betas = [bet[r, :] for r in rows]
        gcbs = [sum(jnp.dot(tril_b, part, preferred_element_type=f32)
                    for part in _split_bf16(gl[r, :], DN_DECAY_PARTS)) for r in rows]
        kbs = [k * beta for k, beta in zip(ks, betas)]
        kbfs = [k.astype(bf16) for k in ks]
        k2s = [jnp.concatenate([kbfs[j], kbfs[j + 1]], axis=0) for j in pairs]

        def against_pair_keys(xs):
            return [jnp.where(first,
                              lax.dot_general(xs[j].astype(bf16), k2, nt_dims, preferred_element_type=f32),
                              lax.dot_general(xs[j + 1].astype(bf16), k2, nt_dims, preferred_element_type=f32))
                    for j, k2 in zip(pairs, k2s)]

        a_kks = against_pair_keys(kbs)
        decays = []
        for j in pairs:
            gc_col = jnp.where(first, gcbs[j], gcbs[j + 1])
            gc_row = jnp.concatenate([gcbs[j], gcbs[j + 1]], axis=0).T[0:c, :]
            decays.append(jnp.where(incl, jnp.exp(jnp.where(incl, gc_col - gc_row, 0.0)), 0.0))
        lows = [jnp.where(strict, a * dec, 0.0) for a, dec in zip(a_kks, decays)]
        pws = [jnp.where(same_diag, -low, 0.0) for low in lows]
        es = list(pws)
        for _ in range(max(1, (bs - 1).bit_length()) - 1):
            pws = [dot_pair(pw, pw) for pw in pws]
            es = [e + pw + dot_pair(e, pw) for e, pw in zip(es, pws)]
        for below in level_masks:
            offs = [jnp.where(below, low, 0.0) for low in lows]
            xs = [off + dot_pair(e, off) for e, off in zip(es, offs)]
            es = [e - (x + dot_pair(x, e)) for e, x in zip(es, xs)]
        egcs = [jnp.exp(gcb) for gcb in gcbs]
        rhss = [jnp.concatenate([vn[r, :] * beta, kb * egc], axis=1)
                for r, beta, kb, egc in zip(rows, betas, kbs, egcs)]
        uwbs = []
        for e, j in zip(es, pairs):
            e_hi, e_lo = _split_bf16(e, 2)
            r2 = stack_diag(rhss[j], rhss[j + 1]).astype(bf16)
            er = jnp.dot(e_hi, r2, preferred_element_type=f32) + jnp.dot(e_lo, r2, preferred_element_type=f32)
            uwbs.append((rhss[j] + er[:, 0:2 * d]).astype(bf16))
            uwbs.append((rhss[j + 1] + er[:, 2 * d:4 * d]).astype(bf16))
        qs = [qn[r, :] for r in rows]
        a_qks = [a * dec for a, dec in zip(against_pair_keys(qs), decays)]
        g_lasts = [gcb[c - 1:c, :] for gcb in gcbs]
        kdec_ts = []
        for k, gcb, g_last in zip(ks, gcbs, g_lasts):
            kdec = k * jnp.exp(g_last - gcb)
            kdec_ts.append(jnp.concatenate([kdec, jnp.zeros_like(kdec)], axis=0).T[:, 0:c].astype(bf16))
        nms = [jnp.dot(kt, uwb, preferred_element_type=f32) for kt, uwb in zip(kdec_ts, uwbs)]
        oq2s = [jnp.dot(a.astype(bf16), stack_diag(uwbs[j], uwbs[j + 1]), preferred_element_type=f32)
                for a, j in zip(a_qks, pairs)]
        oqs = []
        for oq2 in oq2s:
            oqs += [oq2[:, 0:2 * d], oq2[:, 2 * d:4 * d]]
        for i, r, nm, oq, q, egc, g_last in zip(ids, rows, nms, oqs, qs, egcs, g_lasts):
            m0 = pl.ds(pl.multiple_of(i * d, d), d)
            nn[m0, :] = nm[:, 0:d]
            mm[m0, :] = nm[:, d:2 * d].astype(bf16)
            oo[r, :] = oq[:, 0:d]
            qq[r, :] = (q * egc - oq[:, d:2 * d]).astype(bf16)
            dd[pl.ds(pl.multiple_of(i * 8, 8), 8), :] = jnp.broadcast_to(jnp.exp(g_last), (8, d))
        return carry

    lax.fori_loop(0, nb * n_chunks // DN_PREP_UNROLL, chunk_prep, 0)

    def chunk_scan(i, states):
        ids = [b * n_chunks + i for b in range(nb)]
        blocks = [pl.ds(pl.multiple_of(j * d, d), d) for j in ids]
        sbs = [s.astype(bf16) for s in states]
        for blk, sb in zip(blocks, sbs):
            ss[blk, :] = sb
        prods = [jnp.dot(mm[blk, :], sb, preferred_element_type=jnp.float32) for blk, sb in zip(blocks, sbs)]
        return tuple(s * dd[pl.ds(pl.multiple_of(j * 8, 8), 1), :] - pr + nn[blk, :]
                     for s, j, blk, pr in zip(states, ids, blocks, prods))

    lax.fori_loop(0, n_chunks, chunk_scan, tuple(jnp.zeros((d, d), jnp.float32) for _ in range(nb)))

    nw = nw_ref[...]

    def chunk_out(it, carry):
        ids = [it * DN_OUT_UNROLL + cc for cc in range(DN_OUT_UNROLL)]
        rows = [pl.ds(pl.multiple_of(i * c, c), c) for i in ids]
        outs = [jnp.dot(qq[r, :], ss[pl.ds(pl.multiple_of(i * d, d), d), :], preferred_element_type=jnp.float32)
                + oo[r, :] for i, r in zip(ids, rows)]
        for r, o in zip(rows, outs):
            o = o * lax.rsqrt(jnp.mean(o * o, axis=-1, keepdims=True) + 1e-6) * nw
            z = z_ref[r, :]
            o_ref[r, :] = (o * (z * jax.nn.sigmoid(z))).astype(o_ref.dtype)
        return carry

    lax.fori_loop(0, nb * n_chunks // DN_OUT_UNROLL, chunk_out, 0)


def _deltanet(pf, conv_w, a_log, dt_bias, norm_w, bsz, seq):
    d = DN_HEAD_DIM
    nb = DN_BATCHES if bsz % DN_BATCHES == 0 else 1
    rows = nb * seq
    n_chunks = rows // DN_CHUNK
    qkv0 = PF_QKVB // d
    scal = jnp.stack([a_log, dt_bias]).astype(jnp.float32)
    f32 = jnp.float32
    bf16 = jnp.bfloat16
    cw4 = conv_w.astype(f32).reshape(DN_CONV, 3, DN_HEADS, d).transpose(2, 1, 0, 3)
    assert 2 * DN_CHUNK == d and n_chunks % DN_PREP_UNROLL == 0 and n_chunks % DN_OUT_UNROLL == 0
    return pl.pallas_call(
        functools.partial(_dn_kernel, seq=seq),
        grid=(bsz // nb, DN_HEADS),
        in_specs=[
            pl.BlockSpec(memory_space=pltpu.SMEM),
            pl.BlockSpec((rows, d), lambda b, h: (b, qkv0 + h)),
            pl.BlockSpec((rows, d), lambda b, h: (b, qkv0 + DN_HEADS + h)),
            pl.BlockSpec((rows, d), lambda b, h: (b, qkv0 + 2 * DN_HEADS + h)),
            pl.BlockSpec((rows, LANES), lambda b, h: (b, PF_SMALL // LANES)),
            pl.BlockSpec((rows, d), lambda b, h: (b, PF_ZB // d + h)),
            pl.BlockSpec((None, 3, DN_CONV, d), lambda b, h: (h, 0, 0, 0)),
            pl.BlockSpec((1, d), lambda b, h: (0, 0)),
        ],
        out_specs=pl.BlockSpec((rows, d), lambda b, h: (b, h)),
        out_shape=jax.ShapeDtypeStruct((bsz * seq, DN_WIDTH), jnp.bfloat16),
        scratch_shapes=[
            pltpu.VMEM((rows, d), f32), pltpu.VMEM((rows, d), f32), pltpu.VMEM((rows, d), f32),
            pltpu.VMEM((rows, d), f32), pltpu.VMEM((rows, d), f32),
            pltpu.VMEM((n_chunks * d, d), bf16), pltpu.VMEM((n_chunks * d, d), f32),
            pltpu.VMEM((rows, d), bf16), pltpu.VMEM((rows, d), f32),
            pltpu.VMEM((n_chunks * 8, d), f32),
            pltpu.VMEM((n_chunks * d, d), bf16),
        ],
        compiler_params=_cparams(2), name="deltanet",
    )(scal, pf, pf, pf, pf, pf, cw4, norm_w.astype(f32).reshape(1, d))


def _out_kernel(oa_ref, ob_ref, gma_ref, gmb_ref, x_ref, p_ref, wa_ref, wb_ref, wo_ref, wpg_ref, wp_ref,
                lng_ref, lnb_ref, o_ref):
    f32 = jnp.float32
    bf16 = jnp.bfloat16
    tm = o_ref.shape[0]
    sub = tm // OUT_SUBTILES
    parts = [slice(i * sub, (i + 1) * sub) for i in range(OUT_SUBTILES)]
    y_a = [jnp.dot(oa_ref[r, :], wa_ref[...], preferred_element_type=f32) for r in parts]
    y_b = [jnp.dot(ob_ref[r, :], wb_ref[...], preferred_element_type=f32) for r in parts]
    pw = [jnp.dot(p_ref[r, :].astype(bf16), wp_ref[...], preferred_element_type=f32) for r in parts]
    mix = [(jax.nn.sigmoid(gma_ref[r, :]) * ya + jax.nn.sigmoid(gmb_ref[r, :]) * yb).astype(bf16)
           for r, ya, yb in zip(parts, y_a, y_b)]
    h = [DEEPNORM_ALPHA * x_ref[r, :] + jnp.dot(mx, wo_ref[...], preferred_element_type=f32)
         for r, mx in zip(parts, mix)]
    gate = [jnp.dot(hh.astype(bf16), wpg_ref[...], preferred_element_type=f32) for hh in h]
    for r, hh, gt, pp in zip(parts, h, gate, pw):
        hh = hh + jax.nn.sigmoid(gt) * pp
        mu = jnp.mean(hh, axis=-1, keepdims=True)
        hc = hh - mu
        var = jnp.mean(hc * hc, axis=-1, keepdims=True)
        o_ref[r, :] = (hc * lax.rsqrt(var + 1e-5) * lng_ref[...] + lnb_ref[...]).astype(o_ref.dtype)


def _out_block(o_a, o_b, pf, x2, p2, wa, wb, wo, wpg, wp, ln_g, ln_b, tm):
    t = x2.shape[0]
    bf = jnp.bfloat16

    def full(shape):
        return pl.BlockSpec(shape, lambda i: (0, 0))

    return pl.pallas_call(
        _out_kernel,
        grid=(t // tm,),
        in_specs=[
            pl.BlockSpec((tm, NSA_WIDTH), lambda i: (i, 0)),
            pl.BlockSpec((tm, DN_WIDTH), lambda i: (i, 0)),
            pl.BlockSpec((tm, D_MODEL), lambda i: (i, PF_GM // D_MODEL)),
            pl.BlockSpec((tm, D_MODEL), lambda i: (i, PF_GM // D_MODEL + 1)),
            pl.BlockSpec((tm, D_MODEL), lambda i: (i, 0)),
            pl.BlockSpec((tm, PLE_DIM), lambda i: (i, 0)),
            full((NSA_WIDTH, D_MODEL)), full((DN_WIDTH, D_MODEL)), full((D_MODEL, D_MODEL)),
            full((D_MODEL, D_MODEL)), full((PLE_DIM, D_MODEL)), full((1, D_MODEL)), full((1, D_MODEL)),
        ],
        out_specs=pl.BlockSpec((tm, D_MODEL), lambda i: (i, 0)),
        out_shape=jax.ShapeDtypeStruct((t, D_MODEL), x2.dtype),
        compiler_params=_cparams(1), name="out_block",
    )(o_a, o_b, pf, pf, x2, p2, wa.astype(bf), wb.astype(bf), wo.astype(bf), wpg.astype(bf), wp.astype(bf),
      ln_g.astype(jnp.float32).reshape(1, D_MODEL), ln_b.astype(jnp.float32).reshape(1, D_MODEL))


def _layer(x, p, w_in, pos_k, pos_v, w1_k, w2_k, w1_v, w2_v, bias_tabs, conv_w, a_log, dt_bias, norm_w,
           w_a, w_b, w_o, w_ple, w_pg, ln_g, ln_b):
    bsz, seq, _ = x.shape
    t = bsz * seq
    x2 = x.reshape(t, D_MODEL)
    wb16, wf16 = _prep_w_in(w_in)
    pb, pf = _proj(x2, wb16, wf16, PROJ_TM if t % PROJ_TM == 0 else seq)

    pos2, w2p = _prep_compress_weights(pos_k, pos_v, w2_k, w2_v)
    kcvc = _compress(pf, pos2, w1_k, w1_v, w2p, bsz, seq)

    bias_sel, bias_win, bias_cmp = bias_tabs
    o_a = _nsa(pb, pf, kcvc, bias_sel, bias_win, bias_cmp, bsz, seq)
    o_b = _deltanet(pf, conv_w, a_log, dt_bias, norm_w, bsz, seq)
    out = _out_block(o_a, o_b, pf, x2, p.reshape(t, PLE_DIM), w_a, w_b, w_o, w_pg, w_ple, ln_g, ln_b,
                     OUT_TM if t % OUT_TM == 0 else seq)
    return out.reshape(bsz, seq, D_MODEL)


def kernel(x, p, w_in, cmp_pos_k, cmp_pos_v, cmp_w1_k, cmp_w2_k, cmp_w1_v, cmp_w2_v, rel_bias, dn_conv_w,
           dn_a_log, dn_dt_bias, dn_norm_w, w_branch_a, w_branch_b, w_out, w_ple, w_ple_gate, ln_g, ln_b):
    depth = w_in.shape[0]
    bias_tabs = _bias_tables(rel_bias, x.shape[1])
    for i in range(depth):
        x = _layer(x, p[i], w_in[i], cmp_pos_k[i], cmp_pos_v[i], cmp_w1_k[i], cmp_w2_k[i], cmp_w1_v[i],
                   cmp_w2_v[i], bias_tabs, dn_conv_w[i], dn_a_log[i], dn_dt_bias[i], dn_norm_w[i],
                   w_branch_a[i], w_branch_b[i], w_out[i], w_ple[i], w_ple_gate[i], ln_g[i], ln_b[i])
    return x
```

```python
import functools
import math

import numpy as np
import jax
import jax.numpy as jnp
from jax import lax
from jax.experimental import pallas as pl
from jax.experimental.pallas import tpu as pltpu

D_MODEL = 1024
PLE_DIM = 256
NSA_HEADS = 8
NSA_GROUPS = 2
NSA_HPG = NSA_HEADS // NSA_GROUPS
NSA_HEAD_DIM = 64
NSA_WIDTH = NSA_HEADS * NSA_HEAD_DIM
NSA_KV = NSA_GROUPS * NSA_HEAD_DIM
CMP_BLOCK = 32
CMP_STRIDE = 16
CMP_HIDDEN = 256
SEL_BLOCK = 64
SEL_TOPK = 8
WINDOW = 512
DN_HEADS = 4
DN_HEAD_DIM = 128
DN_WIDTH = DN_HEADS * DN_HEAD_DIM
DN_CONV = 4
DN_CHUNK = 64
NUM_BUCKETS = 32
REL_MAX_DIST = 1024
DEEPNORM_ALPHA = 2.0 ** 0.25
NEG = -1e30
FORCE = 1e6
LOG2E = 1.4426950408889634

LANES = 128
TILE = 128
SEL_LANE0 = 64
SEL_GROUP = 4
NSA_TILES_PER_STEP = 4
NSA_V_ROWS = NSA_HEAD_DIM + 16
VMEM_LIMIT = 56 * 1024 * 1024
PROJ_TM = 512
PROJ_TN = 1024
OUT_TM = 512
OUT_SUBTILES = 2
DN_INV_BLOCK = 16
DN_BATCHES = 2
DN_OUT_UNROLL = 8
DN_DECAY_PARTS = 2
DN_PREP_UNROLL = 32

HIGHEST = lax.Precision.HIGHEST

PB_QP = 0
PB_KS = PB_QP + NSA_WIDTH
PB_KW = PB_KS + NSA_KV
PB_VS = PB_KW + NSA_KV
PB_VW = PB_VS + NSA_KV
PB_WIDTH = PB_VW + NSA_KV
PF_GM = 0
PF_QKVB = PF_GM + 2 * D_MODEL
PF_ZA = PF_QKVB + 3 * DN_WIDTH
PF_ZB = PF_ZA + NSA_WIDTH
PF_KC = PF_ZB + DN_WIDTH
PF_VC = PF_KC + NSA_KV
PF_SMALL = PF_VC + NSA_KV
PF_WIDTH = PF_SMALL + LANES
SMALL_BETA = 3 * NSA_HEADS
SMALL_A = SMALL_BETA + DN_HEADS


def _bucket_thresholds():
    max_exact = NUM_BUCKETS // 2
    span = NUM_BUCKETS - max_exact
    ratio = REL_MAX_DIST // max_exact
    thr = list(range(1, max_exact + 1))
    for k in range(1, span):
        n = max_exact
        while n ** span < max_exact ** span * ratio ** k:
            n += 1
        thr.append(n)
    return tuple(thr)


_THR = _bucket_thresholds()


def _cparams(n_axes):
    return pltpu.CompilerParams(dimension_semantics=("arbitrary",) * n_axes, vmem_limit_bytes=VMEM_LIMIT)


def _proj_kernel(x_ref, wb_ref, wf_ref, pb_ref, pf_ref):
    nt_dims = (((1,), (1,)), ((), ()))
    xb = x_ref[...].astype(jnp.bfloat16)
    pb_ref[...] = lax.dot_general(xb, wb_ref[...], nt_dims, preferred_element_type=jnp.float32).astype(pb_ref.dtype)
    n = pf_ref.shape[1]
    for c0 in range(0, n, PROJ_TN):
        c1 = min(c0 + PROJ_TN, n)
        pf_ref[:, c0:c1] = lax.dot_general(xb, wf_ref[c0:c1, :], nt_dims, preferred_element_type=jnp.float32)


def _proj(x2, wb, wf, tm):
    t, d = x2.shape
    resident = dict(pipeline_mode=pl.Buffered(1))
    return pl.pallas_call(
        _proj_kernel,
        grid=(t // tm,),
        in_specs=[pl.BlockSpec((tm, d), lambda i: (i, 0)),
                  pl.BlockSpec((PB_WIDTH, d), lambda i: (0, 0), **resident),
                  pl.BlockSpec((PF_WIDTH, d), lambda i: (0, 0), **resident)],
        out_specs=[pl.BlockSpec((tm, PB_WIDTH), lambda i: (i, 0)),
                   pl.BlockSpec((tm, PF_WIDTH), lambda i: (i, 0))],
        out_shape=[jax.ShapeDtypeStruct((t, PB_WIDTH), jnp.bfloat16),
                   jax.ShapeDtypeStruct((t, PF_WIDTH), jnp.float32)],
        compiler_params=_cparams(1), name="proj",
    )(x2, wb, wf)


def _prep_w_in(w):
    d = w.shape[0]
    wt = jnp.swapaxes(w, 0, 1)
    o = 0
    wq = wt[o:o + NSA_WIDTH]; o += NSA_WIDTH
    wkv = wt[o:o + 6 * NSA_KV]; o += 6 * NSA_KV
    wg = wt[o:o + 3 * NSA_HEADS]; o += 3 * NSA_HEADS
    wza = wt[o:o + NSA_WIDTH]; o += NSA_WIDTH
    wqkvb = wt[o:o + 3 * DN_WIDTH]; o += 3 * DN_WIDTH
    wbeta_a = wt[o:o + 2 * DN_HEADS]; o += 2 * DN_HEADS
    wzb = wt[o:o + DN_WIDTH]; o += DN_WIDTH
    wgm = wt[o:o + 2 * D_MODEL]
    wkcvc, wks, wvs, wkw, wvw = (wkv[0:2 * NSA_KV], wkv[2 * NSA_KV:3 * NSA_KV], wkv[3 * NSA_KV:4 * NSA_KV],
                                 wkv[4 * NSA_KV:5 * NSA_KV], wkv[5 * NSA_KV:6 * NSA_KV])
    wb = jnp.concatenate([wq * (NSA_HEAD_DIM ** -0.5 * LOG2E), wks, wkw, wvs, wvw], axis=0).astype(jnp.bfloat16)
    pad = jnp.zeros((LANES - 3 * NSA_HEADS - 2 * DN_HEADS, d), w.dtype)
    wf = jnp.concatenate([wgm, wqkvb, wza, wzb, wkcvc, wg, wbeta_a, pad], axis=0).astype(jnp.bfloat16)
    return wb, wf


def _bias_kernel(tab_ref, sel_ref, win_ref, cmp_ref, *, n_cmp):
    h = pl.program_id(0)

    def lookup(n):
        val = jnp.full(n.shape, tab_ref[0, h], jnp.float32)
        for b in range(1, NUM_BUCKETS):
            val = jnp.where(n >= _THR[b - 1], tab_ref[b, h], val)
        return val * LOG2E

    kj = lax.broadcasted_iota(jnp.int32, (TILE, TILE), 0)
    qi = lax.broadcasted_iota(jnp.int32, (TILE, TILE), 1)
    n_sel_tiles = sel_ref.shape[1] - 1
    n_win_tiles = win_ref.shape[1] - 1
    for dt in range(max(n_sel_tiles, n_win_tiles)):
        dist = dt * TILE + qi - kj
        v = lookup(jnp.maximum(dist, 0))
        if dt < n_sel_tiles:
            sel_ref[0, dt] = jnp.where(dist >= 0, v, NEG)
        if dt < n_win_tiles:
            win_ref[0, dt] = jnp.where((dist >= 0) & (dist < WINDOW), v, NEG)
    sel_ref[0, n_sel_tiles] = jnp.full((TILE, TILE), NEG, jnp.float32)
    win_ref[0, n_win_tiles] = jnp.full((TILE, TILE), NEG, jnp.float32)
    c = lax.broadcasted_iota(jnp.int32, cmp_ref.shape[2:], 0)
    for t in range(cmp_ref.shape[1]):
        s = t * TILE + lax.broadcasted_iota(jnp.int32, cmp_ref.shape[2:], 1)
        dist = s - (c * CMP_STRIDE + CMP_BLOCK - 1)
        cmp_ref[0, t] = jnp.where((dist >= 0) & (c < n_cmp), lookup(jnp.maximum(dist, 0)), NEG)


def _bias_tables(rel_bias, seq):
    nq = seq // TILE
    nwin = WINDOW // TILE + 1
    nc = seq // CMP_STRIDE
    n_cmp = nc - CMP_BLOCK // CMP_STRIDE + 1
    return pl.pallas_call(
        functools.partial(_bias_kernel, n_cmp=n_cmp),
        grid=(NSA_HEADS,),
        in_specs=[pl.BlockSpec(memory_space=pltpu.SMEM)],
        out_specs=[pl.BlockSpec((1, nq + 1, TILE, TILE), lambda h: (h, 0, 0, 0)),
                   pl.BlockSpec((1, nwin + 1, TILE, TILE), lambda h: (h, 0, 0, 0)),
                   pl.BlockSpec((1, nq, nc, TILE), lambda h: (h, 0, 0, 0))],
        out_shape=[jax.ShapeDtypeStruct((NSA_HEADS, nq + 1, TILE, TILE), jnp.float32),
                   jax.ShapeDtypeStruct((NSA_HEADS, nwin + 1, TILE, TILE), jnp.float32),
                   jax.ShapeDtypeStruct((NSA_HEADS, nq, nc, TILE), jnp.float32)],
        compiler_params=_cparams(1), name="bias_tables",
    )(rel_bias.astype(jnp.float32))


def _gelu_tanh(x):
    return x * (0.5 * (1.0 + jnp.tanh(math.sqrt(2.0 / math.pi) * (x + 0.044715 * (x * x * x)))))


def _compress_kernel(x_ref, pos_ref, w1k_ref, w1v_ref, w2_ref, o_ref, wbd):
    kv = pl.program_id(0)
    nc = o_ref.shape[2]
    half = CMP_BLOCK // 2
    dh = NSA_HEAD_DIM
    bf16 = jnp.bfloat16

    @pl.when(pl.program_id(1) == 0)
    def _():
        def build(w1_ref):
            zero = jnp.zeros((dh, CMP_HIDDEN), bf16)
            for a in range(2):
                for l in range(half):
                    r0 = (a * half + l) * dh
                    wl = w1_ref[r0:r0 + dh, :].astype(bf16)
                    wbd[a, l * NSA_KV:(l + 1) * NSA_KV, :] = jnp.concatenate(
                        [jnp.concatenate([wl, zero], axis=1), jnp.concatenate([zero, wl], axis=1)], axis=0)
        pl.when(kv == 0)(functools.partial(build, w1k_ref))
        pl.when(kv == 1)(functools.partial(build, w1v_ref))

    r = jnp.concatenate([x_ref[pl.ds(l, nc, stride=CMP_STRIDE), :] for l in range(CMP_STRIDE)], axis=1)
    a = jnp.dot((r + pos_ref[0, 0:1, :]).astype(bf16), wbd[0], preferred_element_type=jnp.float32)
    b = jnp.dot((r + pos_ref[0, 1:2, :]).astype(bf16), wbd[1], preferred_element_type=jnp.float32)
    hid = a + pltpu.roll(b, nc - 1, 0)
    o_ref[0, 0] = jnp.dot(_gelu_tanh(hid).astype(bf16), w2_ref[0], preferred_element_type=jnp.float32)


def _compress(pf, pos2, w1_k, w1_v, w2p, bsz, seq):
    nc = seq // CMP_STRIDE
    width = CMP_STRIDE * NSA_KV
    hid = NSA_GROUPS * CMP_HIDDEN
    assert CMP_BLOCK == 2 * CMP_STRIDE
    return pl.pallas_call(
        _compress_kernel,
        grid=(2, bsz),
        in_specs=[pl.BlockSpec((seq, NSA_KV), lambda k, b: (b, PF_KC // NSA_KV + k)),
                  pl.BlockSpec((1, 2, width), lambda k, b: (k, 0, 0)),
                  pl.BlockSpec(w1_k.shape, lambda k, b: (0, 0)),
                  pl.BlockSpec(w1_v.shape, lambda k, b: (0, 0)),
                  pl.BlockSpec((1, hid, LANES), lambda k, b: (k, 0, 0))],
        out_specs=pl.BlockSpec((1, 1, nc, LANES), lambda k, b: (b, k, 0, 0)),
        out_shape=jax.ShapeDtypeStruct((bsz, 2, nc, LANES), jnp.float32),
        scratch_shapes=[pltpu.VMEM((2, width, hid), jnp.bfloat16)],
        compiler_params=_cparams(2), name="compress",
    )(pf, pos2, w1_k, w1_v, w2p)


def _prep_compress_weights(pos_k, pos_v, w2_k, w2_v):
    eye = jnp.eye(NSA_GROUPS, dtype=jnp.float32)
    half = CMP_BLOCK // 2

    def w2_both(w2):
        return jnp.einsum('jd,gh->gjhd', w2, eye).reshape(NSA_GROUPS * CMP_HIDDEN, NSA_GROUPS * NSA_HEAD_DIM)

    def pos_both(pos):
        p = pos.reshape(2, half, 1, NSA_HEAD_DIM)
        return jnp.broadcast_to(p, (2, half, NSA_GROUPS, NSA_HEAD_DIM)).reshape(2, half * NSA_KV)

    pos2 = jnp.stack([pos_both(pos_k), pos_both(pos_v)]).astype(jnp.float32)
    w2p = jnp.stack([w2_both(w2_k), w2_both(w2_v)]).astype(jnp.bfloat16)
    return pos2, w2p


def _pair_lanes(x, g):
    sw = pltpu.roll(x, LANES // 2, 1)
    lane = lax.broadcasted_iota(jnp.int32, x.shape, 1)
    own = (lane < LANES // 2) == (g == 0)
    return jnp.where(own, x, sw)


def _group_rows_t(x, g):
    xt = x.T
    half = LANES // 2
    return jnp.where(g == 0, xt[0:half, :], xt[half:LANES, :])


def _nsa_step_kernel(q_ref, qn_ref, ks_ref, kw_ref, vs_ref, vw_ref, kcvc_ref, bc_ref, bsel_ref, bwin_ref,
                     small_ref, za_ref, o_ref,
                     ksa, kwp, vst, vwt, kcs, vct, qa_s, qp_s, ocmp_s, sbuf, *, n_cmp, n_sel, n_top):
    g = pl.program_id(0)
    step_id = pl.program_id(2)
    tq = TILE
    dh = NSA_HEAD_DIM
    seq = ks_ref.shape[0]
    nq = seq // TILE
    nc = kcvc_ref.shape[2]
    n_wt = bwin_ref.shape[1] - 1
    heads = range(NSA_HPG)
    f32 = jnp.float32
    bf16 = jnp.bfloat16
    nt_dims = (((1,), (1,)), ((), ()))
    gk = SEL_GROUP * TILE

    def select_tile(src_ref, r0, tile, slot):
        q32 = src_ref[r0:r0 + tq, :].astype(f32)
        q_t = jnp.concatenate([q32[:, j * LANES:(j + 1) * LANES].T for j in range(NSA_HPG * dh // LANES)], axis=0)
        qh_t = [q_t[hh * dh:(hh + 1) * dh, :] for hh in heads]
        zeros_t = jnp.zeros((LANES - dh, tq), f32)
        qp_s[slot] = jnp.concatenate([jnp.concatenate([qh_t[hh], zeros_t], axis=0) for hh in heads],
                                     axis=1).astype(bf16)
        q_all = qp_s[slot]
        s_c = jnp.dot(kcs[...], q_all, preferred_element_type=f32)
        bias_c = jnp.concatenate([bc_ref[hh, tile] for hh in heads], axis=1)
        valid = bias_c > 0.5 * NEG
        s_c = s_c + bias_c
        e = jnp.where(valid, jnp.exp2(s_c - jnp.max(s_c, axis=0, keepdims=True)), 0.0)
        den = jnp.maximum(jnp.sum(e, axis=0, keepdims=True), 1e-30)
        p_c = e * (1.0 / den)
        psum = sum(p_c[:, hh * tq:(hh + 1) * tq] for hh in heads)
        sj = lax.broadcasted_iota(jnp.int32, (n_sel, nc), 0)
        ci = lax.broadcasted_iota(jnp.int32, (n_sel, nc), 1)
        overlap = ((ci * CMP_STRIDE < (sj + 1) * SEL_BLOCK) & (ci * CMP_STRIDE + CMP_BLOCK > sj * SEL_BLOCK)
                   & (ci < n_cmp)).astype(bf16)
        imp_t = sum(jnp.dot(overlap, part, preferred_element_type=f32)
                    for part in _split_bf16(psum, 3))
        ocmp_s[slot] = jnp.dot(vct[...], p_c.astype(bf16), preferred_element_type=f32)
        blk = lax.broadcasted_iota(jnp.int32, (n_sel, tq), 0)
        cur = (tile * tq + lax.broadcasted_iota(jnp.int32, (n_sel, tq), 1)) >> 6
        forced = (blk == 0) | (blk == cur) | (blk == cur - 1)
        imp_t = jnp.where(forced, FORCE, jnp.where(blk > cur, -FORCE, imp_t))
        rank = jnp.zeros((n_sel, tq), jnp.int32)
        for j in range(n_sel):
            other = imp_t[j:j + 1, :]
            ahead = (other > imp_t) | ((other == imp_t) & (blk > j))
            rank = rank + ahead.astype(jnp.int32)
        selb = jnp.where(rank < n_top, 0.0, NEG)
        tail = [jnp.zeros((SEL_LANE0 - dh, tq), f32)] if SEL_LANE0 > dh else []
        tail.append(selb)
        if LANES - SEL_LANE0 - n_sel > 0:
            tail.append(jnp.zeros((LANES - SEL_LANE0 - n_sel, tq), f32))
        qa_s[slot] = jnp.concatenate([jnp.concatenate([qh_t[hh]] + tail, axis=0) for hh in heads],
                                     axis=1).astype(bf16)

    @pl.when(step_id == 0)
    def _():
        row = lax.broadcasted_iota(jnp.int32, (seq, LANES), 0)
        lane = lax.broadcasted_iota(jnp.int32, (seq, LANES), 1)
        onehot = (lane - SEL_LANE0) == (row >> 6)
        own = lane < NSA_HEAD_DIM
        ks_g = jnp.where(own, _pair_lanes(ks_ref[...].astype(f32), g), 0.0)
        ksa[...] = jnp.where(onehot, 1.0, ks_g).astype(bf16)
        pad = (n_wt - 1) * TILE
        kwp[0:pad, :] = jnp.zeros((pad, LANES), bf16)
        kwp[pad:pad + seq, :] = jnp.where(own, _pair_lanes(kw_ref[...].astype(f32), g), 0.0).astype(bf16)
        for kt in range(n_wt - 1):
            vwt[kt] = jnp.zeros(vwt.shape[1:], bf16)
        extra = vst.shape[1] - dh
        ones_row = (lax.broadcasted_iota(jnp.int32, (extra, TILE), 0) == 0).astype(bf16)
        for kt in range(nq):
            rows = slice(kt * TILE, (kt + 1) * TILE)
            vst[kt, 0:dh, :] = _group_rows_t(vs_ref[rows, :].astype(f32), g).astype(bf16)
            vst[kt, dh:dh + extra, :] = ones_row
            vwt[kt + n_wt - 1, 0:dh, :] = _group_rows_t(vw_ref[rows, :].astype(f32), g).astype(bf16)
            vwt[kt + n_wt - 1, dh:dh + extra, :] = ones_row
        kcs[...] = _pair_lanes(kcvc_ref[0, 0], g).astype(bf16)
        for ct in range(nc // TILE):
            rows = slice(ct * TILE, (ct + 1) * TILE)
            vct[:, rows] = _group_rows_t(kcvc_ref[0, 1, rows, :], g).astype(bf16)
        select_tile(q_ref, 0, 0, 0)

    tps = NSA_TILES_PER_STEP

    def one_tile(n_groups, slot):
        qt = step_id * tps + slot
        rows = slice(slot * tq, (slot + 1) * tq)
        q_all = qp_s[slot]
        q_aug = qa_s[slot]
        kw_rows = kwp[pl.ds(pl.multiple_of(qt * TILE, TILE), n_wt * TILE), :]
        s_w = jnp.dot(kw_rows, q_all, preferred_element_type=f32)

        def sel_scores(gi):
            s = jnp.dot(ksa[gi * gk:(gi + 1) * gk, :], q_aug, preferred_element_type=f32)
            tiles = []
            for t in range(SEL_GROUP):
                dt = qt - (gi * SEL_GROUP + t)
                idx = jnp.where(dt >= 0, dt, nq)
                tiles.append(jnp.concatenate([bsel_ref[hh, idx] for hh in heads], axis=1))
            sbuf[gi % 2] = s + jnp.concatenate(tiles, axis=0)

        sel_scores(0)
        if slot + 1 < tps:
            select_tile(q_ref, (slot + 1) * tq, qt + 1, slot + 1)
        else:
            select_tile(qn_ref, 0, jnp.minimum(qt + 1, nq - 1), 0)

        w_tiles = []
        for t in range(n_wt):
            dt = n_wt - 1 - t
            idx = jnp.where(qt >= dt, dt, n_wt)
            w_tiles.append(jnp.concatenate([bwin_ref[hh, idx] for hh in heads], axis=1))
        s_w = s_w + jnp.concatenate(w_tiles, axis=0)
        p_w = jnp.exp2(s_w - jnp.max(s_w, axis=0, keepdims=True))
        v_w = jnp.concatenate([vwt[qt + t] for t in range(n_wt)], axis=1)
        acc_w = jnp.dot(v_w, p_w.astype(bf16), preferred_element_type=f32)
        o_win = acc_w[0:dh, :] * (1.0 / acc_w[dh:dh + 1, :])

        m = jnp.full((1, NSA_HPG * tq), NEG, f32)
        acc = jnp.zeros((vst.shape[1], NSA_HPG * tq), f32)
        for gi in range(n_groups):
            if gi + 1 < n_groups:
                sel_scores(gi + 1)
            s = sbuf[gi % 2]
            m_new = jnp.maximum(m, jnp.max(s, axis=0, keepdims=True))
            alpha = jnp.exp2(m - m_new)
            p = jnp.exp2(s - m_new)
            v_t = jnp.concatenate([vst[gi * SEL_GROUP + t] for t in range(SEL_GROUP)], axis=1)
            acc = alpha * acc + jnp.dot(v_t, p.astype(bf16), preferred_element_type=f32)
            m = m_new
        o_sel = acc[0:dh, :] * (1.0 / acc[dh:dh + 1, :])
        o_cmp = ocmp_s[slot]

        gates_t = jax.nn.sigmoid(small_ref[rows, :]).T
        mixed = []
        for hh in heads:
            def gate(br):
                r = br * NSA_HEADS + hh
                return jnp.where(g == 0, gates_t[r:r + 1, :], gates_t[r + NSA_HPG:r + NSA_HPG + 1, :])
            cols = slice(hh * tq, (hh + 1) * tq)
            mixed.append(gate(0) * o_cmp[:, cols] + gate(1) * o_sel[:, cols] + gate(2) * o_win[:, cols])
        o = jnp.concatenate([jnp.concatenate(mixed[2 * j:2 * j + 2], axis=0).T for j in range(NSA_HPG // 2)],
                            axis=1)
        z = za_ref[rows, :]
        o_ref[rows, :] = (o * (z * jax.nn.sigmoid(z))).astype(o_ref.dtype)

    def step(n_groups):
        for slot in range(tps):
            one_tile(n_groups, slot)

    for n_groups in range(1, nq // SEL_GROUP + 1):
        pl.when(step_id // (SEL_GROUP // tps) == n_groups - 1)(functools.partial(step, n_groups))


def _nsa(pb, pf, kcvc, bias_sel, bias_win, bias_cmp, bsz, seq):
    nq = seq // TILE
    nc = seq // CMP_STRIDE
    n_cmp = nc - CMP_BLOCK // CMP_STRIDE + 1
    n_sel = seq // SEL_BLOCK
    n_top = min(SEL_TOPK, n_sel)
    nwin = bias_win.shape[1]
    tps = NSA_TILES_PER_STEP
    assert nq % SEL_GROUP == 0 and nc % TILE == 0 and bias_sel.shape[1] == nq + 1
    assert tps >= 2 and SEL_GROUP % tps == 0
    ns = nq // tps
    rows = tps * TILE
    gw = NSA_HPG * NSA_HEAD_DIM
    kern = functools.partial(_nsa_step_kernel, n_cmp=n_cmp, n_sel=n_sel, n_top=n_top)
    return pl.pallas_call(
        kern,
        grid=(NSA_GROUPS, bsz, ns),
        in_specs=[
            pl.BlockSpec((rows, gw), lambda g, b, t: (b * ns + t, PB_QP // gw + g)),
            pl.BlockSpec((TILE, gw), lambda g, b, t: (b * nq + jnp.minimum((t + 1) * tps, nq - 1),
                                                      PB_QP // gw + g)),
            pl.BlockSpec((seq, LANES), lambda g, b, t: (b, PB_KS // LANES)),
            pl.BlockSpec((seq, LANES), lambda g, b, t: (b, PB_KW // LANES)),
            pl.BlockSpec((seq, LANES), lambda g, b, t: (b, PB_VS // LANES)),
            pl.BlockSpec((seq, LANES), lambda g, b, t: (b, PB_VW // LANES)),
            pl.BlockSpec((1, 2, nc, LANES), lambda g, b, t: (b, 0, 0, 0)),
            pl.BlockSpec((NSA_HPG, nq, nc, TILE), lambda g, b, t: (g, 0, 0, 0)),
            pl.BlockSpec((NSA_HPG, nq + 1, TILE, TILE), lambda g, b, t: (g, 0, 0, 0)),
            pl.BlockSpec((NSA_HPG, nwin, TILE, TILE), lambda g, b, t: (g, 0, 0, 0)),
            pl.BlockSpec((rows, LANES), lambda g, b, t: (b * ns + t, PF_SMALL // LANES)),
            pl.BlockSpec((rows, gw), lambda g, b, t: (b * ns + t, PF_ZA // gw + g)),
        ],
        out_specs=pl.BlockSpec((rows, gw), lambda g, b, t: (b * ns + t, g)),
        out_shape=jax.ShapeDtypeStruct((bsz * seq, NSA_WIDTH), jnp.bfloat16),
        scratch_shapes=[
            pltpu.VMEM((seq, LANES), jnp.bfloat16),
            pltpu.VMEM((seq + (nwin - 2) * TILE, LANES), jnp.bfloat16),
            pltpu.VMEM((nq, NSA_V_ROWS, TILE), jnp.bfloat16),
            pltpu.VMEM((nq + nwin - 2, NSA_V_ROWS, TILE), jnp.bfloat16),
            pltpu.VMEM((nc, LANES), jnp.bfloat16),
            pltpu.VMEM((NSA_HEAD_DIM, nc), jnp.bfloat16),
            pltpu.VMEM((tps, LANES, NSA_HPG * TILE), jnp.bfloat16),
            pltpu.VMEM((tps, LANES, NSA_HPG * TILE), jnp.bfloat16),
            pltpu.VMEM((tps, NSA_HEAD_DIM, NSA_HPG * TILE), jnp.float32),
            pltpu.VMEM((2, SEL_GROUP * TILE, NSA_HPG * TILE), jnp.float32),
        ],
        compiler_params=_cparams(3), name="nsa",
    )(pb, pb, pb, pb, pb, pb, kcvc, bias_cmp, bias_sel, bias_win, pf, pf)


def _split_bf16(a, n):
    parts = []
    for _ in range(n - 1):
        hi = a.astype(jnp.bfloat16)
        parts.append(hi)
        a = a - hi.astype(jnp.float32)
    parts.append(a.astype(jnp.bfloat16))
    return parts


def _dot3(a, b):
    ah, al = _split_bf16(a, 2)
    bh, bl = _split_bf16(b, 2)
    f32 = jnp.float32
    return (jnp.dot(ah, bh, preferred_element_type=f32) + jnp.dot(al, bh, preferred_element_type=f32)
            + jnp.dot(ah, bl, preferred_element_type=f32))


def _softplus(x):
    return jnp.maximum(x, 0.0) + jnp.log(1.0 + jnp.exp(-jnp.abs(x)))


def _dn_kernel(scal_ref, q_ref, k_ref, v_ref, small_ref, z_ref, cw_ref, nw_ref, o_ref,
               qn, kn, vn, bet, gl, mm, nn, qq, oo, dd, ss, *, seq):
    h = pl.program_id(1)
    n_rows = q_ref.shape[0]
    nb = n_rows // seq
    c = DN_CHUNK
    n_chunks = seq // c
    d = DN_HEAD_DIM

    head = 8

    def conv_body(x_ref, which):
        n = n_rows - head
        y = x_ref[pl.ds(head, n), :] * cw_ref[which, DN_CONV - 1:DN_CONV, :]
        for j in range(DN_CONV - 1):
            y = y + x_ref[pl.ds(head - (DN_CONV - 1 - j), n), :] * cw_ref[which, j:j + 1, :]
        return y

    def conv_head(x_ref, which, r0):
        x = x_ref[pl.ds(r0, head), :]
        rowi = lax.broadcasted_iota(jnp.int32, (head, d), 0)
        y = x * cw_ref[which, DN_CONV - 1:DN_CONV, :]
        for j in range(DN_CONV - 1):
            sh = DN_CONV - 1 - j
            y = y + jnp.where(rowi >= sh, pltpu.roll(x, sh, 0), 0.0) * cw_ref[which, j:j + 1, :]
        return y

    def silu(y):
        return y * jax.nn.sigmoid(y)

    def l2n(t):
        return t * lax.rsqrt(jnp.sum(t * t, axis=-1, keepdims=True) + 1e-6)

    finish = (lambda y: l2n(silu(y)) * (d ** -0.5), lambda y: l2n(silu(y)), silu)
    for which, (x_ref, dst) in enumerate(((q_ref, qn), (k_ref, kn), (v_ref, vn))):
        dst[pl.ds(head, n_rows - head), :] = finish[which](conv_body(x_ref, which))
        for b in range(nb):
            dst[pl.ds(b * seq, head), :] = finish[which](conv_head(x_ref, which, b * seq))
    small = small_ref[...]
    lane = lax.broadcasted_iota(jnp.int32, small.shape, 1)
    beta_in = jnp.sum(jnp.where(lane == SMALL_BETA + h, small, 0.0), axis=-1, keepdims=True)
    a_in = jnp.sum(jnp.where(lane == SMALL_A + h, small, 0.0), axis=-1, keepdims=True)
    bet[...] = jnp.broadcast_to(jax.nn.sigmoid(beta_in), (n_rows, d))
    gl[...] = jnp.broadcast_to(-jnp.exp(scal_ref[0, h]) * _softplus(a_in + scal_ref[1, h]), (n_rows, d))

    f32 = jnp.float32
    bf16 = jnp.bfloat16
    nt_dims = (((1,), (1,)), ((), ()))
    ri = lax.broadcasted_iota(jnp.int32, (c, 2 * c), 0)
    lane2 = lax.broadcasted_iota(jnp.int32, (c, 2 * c), 1)
    first = lane2 < c
    cj = lane2 & (c - 1)
    incl = ri >= cj
    strict = ri > cj
    tril_b = (lax.broadcasted_iota(jnp.int32, (c, c), 0) >= lax.broadcasted_iota(jnp.int32, (c, c), 1)).astype(bf16)
    bs = DN_INV_BLOCK
    sh = bs.bit_length() - 1
    same_diag = (ri >> sh) == (cj >> sh)
    level_masks = []
    while (1 << sh) < c:
        level_masks.append(((ri >> (sh + 1)) == (cj >> (sh + 1))) & ((ri >> sh) > (cj >> sh)))
        sh += 1
    bd_r = lax.broadcasted_iota(jnp.int32, (2 * c, 2 * c), 0)
    bd_c = lax.broadcasted_iota(jnp.int32, (2 * c, 2 * c), 1)
    block_diag = (bd_r >= c) == (bd_c >= c)

    def dot_pair(a, b):
        bb = b.astype(bf16)
        bb = jnp.where(block_diag, jnp.concatenate([bb, bb], axis=0), jnp.zeros((), bf16))
        return jnp.dot(a.astype(bf16), bb, preferred_element_type=f32)

    def stack_diag(x0, x1):
        z = jnp.zeros_like(x0)
        return jnp.concatenate([jnp.concatenate([x0, z], axis=1), jnp.concatenate([z, x1], axis=1)], axis=0)

    def chunk_prep(it, carry):
        ids = [it * DN_PREP_UNROLL + cc for cc in range(DN_PREP_UNROLL)]
        rows = [pl.ds(pl.multiple_of(i * c, c), c) for i in ids]
        pairs = range(0, DN_PREP_UNROLL, 2)
        ks = [kn[r, :] for r in rows]
        betas = [bet[r, :] for r in rows]
        gcbs = [sum(jnp.dot(tril_b, part, preferred_element_type=f32)
                    for part in _split_bf16(gl[r, :], DN_DECAY_PARTS)) for r in rows]
        kbs = [k * beta for k, beta in zip(ks, betas)]
        k2ts = [jnp.concatenate([ks[j], ks[j + 1]], axis=0).T.astype(bf16) for j in pairs]

        def against_pair_keys(xs):
            return [jnp.where(first,
                              jnp.dot(xs[j].astype(bf16), k2t, preferred_element_type=f32),
                              jnp.dot(xs[j + 1].astype(bf16), k2t, preferred_element_type=f32))
                    for j, k2t in zip(pairs, k2ts)]

        a_kks = against_pair_keys(kbs)
        decays = []
        for j in pairs:
            gc_col = jnp.where(first, gcbs[j], gcbs[j + 1])
            gc_row = jnp.concatenate([gcbs[j], gcbs[j + 1]], axis=0).T[0:c, :]
            decays.append(jnp.where(incl, jnp.exp(jnp.where(incl, gc_col - gc_row, 0.0)), 0.0))
        lows = [jnp.where(strict, a * dec, 0.0) for a, dec in zip(a_kks, decays)]
        pws = [jnp.where(same_diag, -low, 0.0) for low in lows]
        es = list(pws)
        for _ in range(max(1, (bs - 1).bit_length()) - 1):
            pws = [dot_pair(pw, pw) for pw in pws]
            es = [e + pw + dot_pair(e, pw) for e, pw in zip(es, pws)]
        for below in level_masks:
            offs = [jnp.where(below, low, 0.0) for low in lows]
            xs = [off + dot_pair(e, off) for e, off in zip(es, offs)]
            es = [e - (x + dot_pair(x, e)) for e, x in zip(es, xs)]
        egcs = [jnp.exp(gcb) for gcb in gcbs]
        rhss = [jnp.concatenate([vn[r, :] * beta, kb * egc], axis=1)
                for r, beta, kb, egc in zip(rows, betas, kbs, egcs)]
        uwbs = []
        for e, j in zip(es, pairs):
            e_hi, e_lo = _split_bf16(e, 2)
            r2 = stack_diag(rhss[j], rhss[j + 1]).astype(bf16)
            er = jnp.dot(e_hi, r2, preferred_element_type=f32) + jnp.dot(e_lo, r2, preferred_element_type=f32)
            uwbs.append((rhss[j] + er[:, 0:2 * d]).astype(bf16))
            uwbs.append((rhss[j + 1] + er[:, 2 * d:4 * d]).astype(bf16))
        qs = [qn[r, :] for r in rows]
        a_qks = [a * dec for a, dec in zip(against_pair_keys(qs), decays)]
        g_lasts = [gcb[c - 1:c, :] for gcb in gcbs]
        kdec_ts = []
        for k, gcb, g_last in zip(ks, gcbs, g_lasts):
            kdec = k * jnp.exp(g_last - gcb)
            kdec_ts.append(jnp.concatenate([kdec, jnp.zeros_like(kdec)], axis=0).T[:, 0:c].astype(bf16))
        nms = [jnp.dot(kt, uwb, preferred_element_type=f32) for kt, uwb in zip(kdec_ts, uwbs)]
        oq2s = [jnp.dot(a.astype(bf16), stack_diag(uwbs[j], uwbs[j + 1]), preferred_element_type=f32)
                for a, j in zip(a_qks, pairs)]
        oqs = []
        for oq2 in oq2s:
            oqs += [oq2[:, 0:2 * d], oq2[:, 2 * d:4 * d]]
        for i, r, nm, oq, q, egc, g_last in zip(ids, rows, nms, oqs, qs, egcs, g_lasts):
            m0 = pl.ds(pl.multiple_of(i * d, d), d)
            nn[m0, :] = nm[:, 0:d]
            mm[m0, :] = nm[:, d:2 * d].astype(bf16)
            oo[r, :] = oq[:, 0:d]
            qq[r, :] = (q * egc - oq[:, d:2 * d]).astype(bf16)
            dd[pl.ds(pl.multiple_of(i * 8, 8), 8), :] = jnp.broadcast_to(jnp.exp(g_last), (8, d))
        return carry

    lax.fori_loop(0, nb * n_chunks // DN_PREP_UNROLL, chunk_prep, 0)

    def chunk_scan(i, states):
        ids = [b * n_chunks + i for b in range(nb)]
        blocks = [pl.ds(pl.multiple_of(j * d, d), d) for j in ids]
        sbs = [s.astype(bf16) for s in states]
        for blk, sb in zip(blocks, sbs):
            ss[blk, :] = sb
        prods = [jnp.dot(mm[blk, :], sb, preferred_element_type=jnp.float32) for blk, sb in zip(blocks, sbs)]
        return tuple(s * dd[pl.ds(pl.multiple_of(j * 8, 8), 1), :] - pr + nn[blk, :]
                     for s, j, blk, pr in zip(states, ids, blocks, prods))

    lax.fori_loop(0, n_chunks, chunk_scan, tuple(jnp.zeros((d, d), jnp.float32) for _ in range(nb)))

    nw = nw_ref[...]

    def chunk_out(it, carry):
        ids = [it * DN_OUT_UNROLL + cc for cc in range(DN_OUT_UNROLL)]
        rows = [pl.ds(pl.multiple_of(i * c, c), c) for i in ids]
        outs = [jnp.dot(qq[r, :], ss[pl.ds(pl.multiple_of(i * d, d), d), :], preferred_element_type=jnp.float32)
                + oo[r, :] for i, r in zip(ids, rows)]
        for r, o in zip(rows, outs):
            o = o * lax.rsqrt(jnp.mean(o * o, axis=-1, keepdims=True) + 1e-6) * nw
            z = z_ref[r, :]
            o_ref[r, :] = (o * (z * jax.nn.sigmoid(z))).astype(o_ref.dtype)
        return carry

    lax.fori_loop(0, nb * n_chunks // DN_OUT_UNROLL, chunk_out, 0)


def _deltanet(pf, conv_w, a_log, dt_bias, norm_w, bsz, seq):
    d = DN_HEAD_DIM
    nb = DN_BATCHES if bsz % DN_BATCHES == 0 else 1
    rows = nb * seq
    n_chunks = rows // DN_CHUNK
    qkv0 = PF_QKVB // d
    scal = jnp.stack([a_log, dt_bias]).astype(jnp.float32)
    f32 = jnp.float32
    bf16 = jnp.bfloat16
    cw4 = conv_w.astype(f32).reshape(DN_CONV, 3, DN_HEADS, d).transpose(2, 1, 0, 3)
    assert 2 * DN_CHUNK == d and n_chunks % DN_PREP_UNROLL == 0 and n_chunks % DN_OUT_UNROLL == 0
    return pl.pallas_call(
        functools.partial(_dn_kernel, seq=seq),
        grid=(bsz // nb, DN_HEADS),
        in_specs=[
            pl.BlockSpec(memory_space=pltpu.SMEM),
            pl.BlockSpec((rows, d), lambda b, h: (b, qkv0 + h)),
            pl.BlockSpec((rows, d), lambda b, h: (b, qkv0 + DN_HEADS + h)),
            pl.BlockSpec((rows, d), lambda b, h: (b, qkv0 + 2 * DN_HEADS + h)),
            pl.BlockSpec((rows, LANES), lambda b, h: (b, PF_SMALL // LANES)),
            pl.BlockSpec((rows, d), lambda b, h: (b, PF_ZB // d + h)),
            pl.BlockSpec((None, 3, DN_CONV, d), lambda b, h: (h, 0, 0, 0)),
            pl.BlockSpec((1, d), lambda b, h: (0, 0)),
        ],
        out_specs=pl.BlockSpec((rows, d), lambda b, h: (b, h)),
        out_shape=jax.ShapeDtypeStruct((bsz * seq, DN_WIDTH), jnp.bfloat16),
        scratch_shapes=[
            pltpu.VMEM((rows, d), f32), pltpu.VMEM((rows, d), f32), pltpu.VMEM((rows, d), f32),
            pltpu.VMEM((rows, d), f32), pltpu.VMEM((rows, d), f32),
            pltpu.VMEM((n_chunks * d, d), bf16), pltpu.VMEM((n_chunks * d, d), f32),
            pltpu.VMEM((rows, d), bf16), pltpu.VMEM((rows, d), f32),
            pltpu.VMEM((n_chunks * 8, d), f32),
            pltpu.VMEM((n_chunks * d, d), bf16),
        ],
        compiler_params=_cparams(2), name="deltanet",
    )(scal, pf, pf, pf, pf, pf, cw4, norm_w.astype(f32).reshape(1, d))


def _out_kernel(oa_ref, ob_ref, gma_ref, gmb_ref, x_ref, p_ref, wa_ref, wb_ref, wo_ref, wpg_ref, wp_ref,
                lng_ref, lnb_ref, o_ref):
    f32 = jnp.float32
    bf16 = jnp.bfloat16
    tm = o_ref.shape[0]
    sub = tm // OUT_SUBTILES
    parts = [slice(i * sub, (i + 1) * sub) for i in range(OUT_SUBTILES)]
    y_a = [jnp.dot(oa_ref[r, :], wa_ref[...], preferred_element_type=f32) for r in parts]
    y_b = [jnp.dot(ob_ref[r, :], wb_ref[...], preferred_element_type=f32) for r in parts]
    pw = [jnp.dot(p_ref[r, :].astype(bf16), wp_ref[...], preferred_element_type=f32) for r in parts]
    mix = [(jax.nn.sigmoid(gma_ref[r, :]) * ya + jax.nn.sigmoid(gmb_ref[r, :]) * yb).astype(bf16)
           for r, ya, yb in zip(parts, y_a, y_b)]
    h = [DEEPNORM_ALPHA * x_ref[r, :] + jnp.dot(mx, wo_ref[...], preferred_element_type=f32)
         for r, mx in zip(parts, mix)]
    gate = [jnp.dot(hh.astype(bf16), wpg_ref[...], preferred_element_type=f32) for hh in h]
    for r, hh, gt, pp in zip(parts, h, gate, pw):
        hh = hh + jax.nn.sigmoid(gt) * pp
        mu = jnp.mean(hh, axis=-1, keepdims=True)
        hc = hh - mu
        var = jnp.mean(hc * hc, axis=-1, keepdims=True)
        o_ref[r, :] = (hc * lax.rsqrt(var + 1e-5) * lng_ref[...] + lnb_ref[...]).astype(o_ref.dtype)


def _out_block(o_a, o_b, pf, x2, p2, wa, wb, wo, wpg, wp, ln_g, ln_b, tm):
    t = x2.shape[0]
    bf = jnp.bfloat16

    def full(shape):
        return pl.BlockSpec(shape, lambda i: (0, 0))

    return pl.pallas_call(
        _out_kernel,
        grid=(t // tm,),
        in_specs=[
            pl.BlockSpec((tm, NSA_WIDTH), lambda i: (i, 0)),
            pl.BlockSpec((tm, DN_WIDTH), lambda i: (i, 0)),
            pl.BlockSpec((tm, D_MODEL), lambda i: (i, PF_GM // D_MODEL)),
            pl.BlockSpec((tm, D_MODEL), lambda i: (i, PF_GM // D_MODEL + 1)),
            pl.BlockSpec((tm, D_MODEL), lambda i: (i, 0)),
            pl.BlockSpec((tm, PLE_DIM), lambda i: (i, 0)),
            full((NSA_WIDTH, D_MODEL)), full((DN_WIDTH, D_MODEL)), full((D_MODEL, D_MODEL)),
            full((D_MODEL, D_MODEL)), full((PLE_DIM, D_MODEL)), full((1, D_MODEL)), full((1, D_MODEL)),
        ],
        out_specs=pl.BlockSpec((tm, D_MODEL), lambda i: (i, 0)),
        out_shape=jax.ShapeDtypeStruct((t, D_MODEL), x2.dtype),
        compiler_params=_cparams(1), name="out_block",
    )(o_a, o_b, pf, pf, x2, p2, wa.astype(bf), wb.astype(bf), wo.astype(bf), wpg.astype(bf), wp.astype(bf),
      ln_g.astype(jnp.float32).reshape(1, D_MODEL), ln_b.astype(jnp.float32).reshape(1, D_MODEL))


def _layer(x, p, w_in, pos_k, pos_v, w1_k, w2_k, w1_v, w2_v, bias_tabs, conv_w, a_log, dt_bias, norm_w,
           w_a, w_b, w_o, w_ple, w_pg, ln_g, ln_b):
    bsz, seq, _ = x.shape
    t = bsz * seq
    x2 = x.reshape(t, D_MODEL)
    wb16, wf16 = _prep_w_in(w_in)
    pb, pf = _proj(x2, wb16, wf16, PROJ_TM if t % PROJ_TM == 0 else seq)

    pos2, w2p = _prep_compress_weights(pos_k, pos_v, w2_k, w2_v)
    kcvc = _compress(pf, pos2, w1_k, w1_v, w2p, bsz, seq)

    bias_sel, bias_win, bias_cmp = bias_tabs
    o_a = _nsa(pb, pf, kcvc, bias_sel, bias_win, bias_cmp, bsz, seq)
    o_b = _deltanet(pf, conv_w, a_log, dt_bias, norm_w, bsz, seq)
    out = _out_block(o_a, o_b, pf, x2, p.reshape(t, PLE_DIM), w_a, w_b, w_o, w_pg, w_ple, ln_g, ln_b,
                     OUT_TM if t % OUT_TM == 0 else seq)
    return out.reshape(bsz, seq, D_MODEL)


def kernel(x, p, w_in, cmp_pos_k, cmp_pos_v, cmp_w1_k, cmp_w2_k, cmp_w1_v, cmp_w2_v, rel_bias, dn_conv_w,
           dn_a_log, dn_dt_bias, dn_norm_w, w_branch_a, w_branch_b, w_out, w_ple, w_ple_gate, ln_g, ln_b):
    depth = w_in.shape[0]
    bias_tabs = _bias_tables(rel_bias, x.shape[1])
    for i in range(depth):
        x = _layer(x, p[i], w_in[i], cmp_pos_k[i], cmp_pos_v[i], cmp_w1_k[i], cmp_w2_k[i], cmp_w1_v[i],
                   cmp_w2_v[i], bias_tabs, dn_conv_w[i], dn_a_log[i], dn_dt_bias[i], dn_norm_w[i],
                   w_branch_a[i], w_branch_b[i], w_out[i], w_ple[i], w_ple_gate[i], ln_g[i], ln_b[i])
    return x
```

```python
import functools
import math

import numpy as np
import jax
import jax.numpy as jnp
from jax import lax
from jax.experimental import pallas as pl
from jax.experimental.pallas import tpu as pltpu

D_MODEL = 1024
PLE_DIM = 256
NSA_HEADS = 8
NSA_GROUPS = 2
NSA_HPG = NSA_HEADS // NSA_GROUPS
NSA_HEAD_DIM = 64
NSA_WIDTH = NSA_HEADS * NSA_HEAD_DIM
NSA_KV = NSA_GROUPS * NSA_HEAD_DIM
CMP_BLOCK = 32
CMP_STRIDE = 16
CMP_HIDDEN = 256
SEL_BLOCK = 64
SEL_TOPK = 8
WINDOW = 512
DN_HEADS = 4
DN_HEAD_DIM = 128
DN_WIDTH = DN_HEADS * DN_HEAD_DIM
DN_CONV = 4
DN_CHUNK = 64
NUM_BUCKETS = 32
REL_MAX_DIST = 1024
DEEPNORM_ALPHA = 2.0 ** 0.25
NEG = -1e30
FORCE = 1e6
LOG2E = 1.4426950408889634

LANES = 128
TILE = 128
SEL_LANE0 = 64
SEL_GROUP = 4
NSA_TILES_PER_STEP = 4
NSA_V_ROWS = NSA_HEAD_DIM + 16
VMEM_LIMIT = 56 * 1024 * 1024
PROJ_TM = 512
PROJ_TN = 1024
OUT_TM = 512
OUT_SUBTILES = 2
DN_INV_BLOCK = 16
DN_BATCHES = 2
DN_OUT_UNROLL = 8
DN_DECAY_PARTS = 2
DN_PREP_UNROLL = 32

HIGHEST = lax.Precision.HIGHEST

PB_QP = 0
PB_KS = PB_QP + NSA_WIDTH
PB_KW = PB_KS + NSA_KV
PB_VS = PB_KW + NSA_KV
PB_VW = PB_VS + NSA_KV
PB_WIDTH = PB_VW + NSA_KV
PF_GM = 0
PF_QKVB = PF_GM + 2 * D_MODEL
PF_ZA = PF_QKVB + 3 * DN_WIDTH
PF_ZB = PF_ZA + NSA_WIDTH
PF_KC = PF_ZB + DN_WIDTH
PF_VC = PF_KC + NSA_KV
PF_SMALL = PF_VC + NSA_KV
PF_WIDTH = PF_SMALL + LANES
SMALL_BETA = 3 * NSA_HEADS
SMALL_A = SMALL_BETA + DN_HEADS


def _bucket_thresholds():
    max_exact = NUM_BUCKETS // 2
    span = NUM_BUCKETS - max_exact
    ratio = REL_MAX_DIST // max_exact
    thr = list(range(1, max_exact + 1))
    for k in range(1, span):
        n = max_exact
        while n ** span < max_exact ** span * ratio ** k:
            n += 1
        thr.append(n)
    return tuple(thr)


_THR = _bucket_thresholds()


def _cparams(n_axes):
    return pltpu.CompilerParams(dimension_semantics=("arbitrary",) * n_axes, vmem_limit_bytes=VMEM_LIMIT)


def _proj_kernel(x_ref, wb_ref, wf_ref, pb_ref, pf_ref):
    nt_dims = (((1,), (1,)), ((), ()))
    xb = x_ref[...].astype(jnp.bfloat16)
    pb_ref[...] = lax.dot_general(xb, wb_ref[...], nt_dims, preferred_element_type=jnp.float32).astype(pb_ref.dtype)
    n = pf_ref.shape[1]
    for c0 in range(0, n, PROJ_TN):
        c1 = min(c0 + PROJ_TN, n)
        pf_ref[:, c0:c1] = lax.dot_general(xb, wf_ref[c0:c1, :], nt_dims, preferred_element_type=jnp.float32)


def _proj(x2, wb, wf, tm):
    t, d = x2.shape
    resident = dict(pipeline_mode=pl.Buffered(1))
    return pl.pallas_call(
        _proj_kernel,
        grid=(t // tm,),
        in_specs=[pl.BlockSpec((tm, d), lambda i: (i, 0)),
                  pl.BlockSpec((PB_WIDTH, d), lambda i: (0, 0), **resident),
                  pl.BlockSpec((PF_WIDTH, d), lambda i: (0, 0), **resident)],
        out_specs=[pl.BlockSpec((tm, PB_WIDTH), lambda i: (i, 0)),
                   pl.BlockSpec((tm, PF_WIDTH), lambda i: (i, 0))],
        out_shape=[jax.ShapeDtypeStruct((t, PB_WIDTH), jnp.bfloat16),
                   jax.ShapeDtypeStruct((t, PF_WIDTH), jnp.float32)],
        compiler_params=_cparams(1), name="proj",
    )(x2, wb, wf)


def _prep_w_in(w):
    d = w.shape[0]
    wt = jnp.swapaxes(w, 0, 1)
    o = 0
    wq = wt[o:o + NSA_WIDTH]; o += NSA_WIDTH
    wkv = wt[o:o + 6 * NSA_KV]; o += 6 * NSA_KV
    wg = wt[o:o + 3 * NSA_HEADS]; o += 3 * NSA_HEADS
    wza = wt[o:o + NSA_WIDTH]; o += NSA_WIDTH
    wqkvb = wt[o:o + 3 * DN_WIDTH]; o += 3 * DN_WIDTH
    wbeta_a = wt[o:o + 2 * DN_HEADS]; o += 2 * DN_HEADS
    wzb = wt[o:o + DN_WIDTH]; o += DN_WIDTH
    wgm = wt[o:o + 2 * D_MODEL]
    wkcvc, wks, wvs, wkw, wvw = (wkv[0:2 * NSA_KV], wkv[2 * NSA_KV:3 * NSA_KV], wkv[3 * NSA_KV:4 * NSA_KV],
                                 wkv[4 * NSA_KV:5 * NSA_KV], wkv[5 * NSA_KV:6 * NSA_KV])
    wb = jnp.concatenate([wq * (NSA_HEAD_DIM ** -0.5 * LOG2E), wks, wkw, wvs, wvw], axis=0).astype(jnp.bfloat16)
    pad = jnp.zeros((LANES - 3 * NSA_HEADS - 2 * DN_HEADS, d), w.dtype)
    wf = jnp.concatenate([wgm, wqkvb, wza, wzb, wkcvc, wg, wbeta_a, pad], axis=0).astype(jnp.bfloat16)
    return wb, wf


def _bias_kernel(tab_ref, sel_ref, win_ref, cmp_ref, *, n_cmp):
    h = pl.program_id(0)

    def lookup(n):
        val = jnp.full(n.shape, tab_ref[0, h], jnp.float32)
        for b in range(1, NUM_BUCKETS):
            val = jnp.where(n >= _THR[b - 1], tab_ref[b, h], val)
        return val * LOG2E

    kj = lax.broadcasted_iota(jnp.int32, (TILE, TILE), 0)
    qi = lax.broadcasted_iota(jnp.int32, (TILE, TILE), 1)
    n_sel_tiles = sel_ref.shape[1] - 1
    n_win_tiles = win_ref.shape[1] - 1
    for dt in range(max(n_sel_tiles, n_win_tiles)):
        dist = dt * TILE + qi - kj
        v = lookup(jnp.maximum(dist, 0))
        if dt < n_sel_tiles:
            sel_ref[0, dt] = jnp.where(dist >= 0, v, NEG)
        if dt < n_win_tiles:
            win_ref[0, dt] = jnp.where((dist >= 0) & (dist < WINDOW), v, NEG)
    sel_ref[0, n_sel_tiles] = jnp.full((TILE, TILE), NEG, jnp.float32)
    win_ref[0, n_win_tiles] = jnp.full((TILE, TILE), NEG, jnp.float32)
    c = lax.broadcasted_iota(jnp.int32, cmp_ref.shape[2:], 0)
    for t in range(cmp_ref.shape[1]):
        s = t * TILE + lax.broadcasted_iota(jnp.int32, cmp_ref.shape[2:], 1)
        dist = s - (c * CMP_STRIDE + CMP_BLOCK - 1)
        cmp_ref[0, t] = jnp.where((dist >= 0) & (c < n_cmp), lookup(jnp.maximum(dist, 0)), NEG)


def _bias_tables(rel_bias, seq):
    nq = seq // TILE
    nwin = WINDOW // TILE + 1
    nc = seq // CMP_STRIDE
    n_cmp = nc - CMP_BLOCK // CMP_STRIDE + 1
    return pl.pallas_call(
        functools.partial(_bias_kernel, n_cmp=n_cmp),
        grid=(NSA_HEADS,),
        in_specs=[pl.BlockSpec(memory_space=pltpu.SMEM)],
        out_specs=[pl.BlockSpec((1, nq + 1, TILE, TILE), lambda h: (h, 0, 0, 0)),
                   pl.BlockSpec((1, nwin + 1, TILE, TILE), lambda h: (h, 0, 0, 0)),
                   pl.BlockSpec((1, nq, nc, TILE), lambda h: (h, 0, 0, 0))],
        out_shape=[jax.ShapeDtypeStruct((NSA_HEADS, nq + 1, TILE, TILE), jnp.float32),
                   jax.ShapeDtypeStruct((NSA_HEADS, nwin + 1, TILE, TILE), jnp.float32),
                   jax.ShapeDtypeStruct((NSA_HEADS, nq, nc, TILE), jnp.float32)],
        compiler_params=_cparams(1), name="bias_tables",
    )(rel_bias.astype(jnp.float32))


def _gelu_tanh(x):
    return x * (0.5 * (1.0 + jnp.tanh(math.sqrt(2.0 / math.pi) * (x + 0.044715 * (x * x * x)))))


def _compress_kernel(x_ref, pos_ref, w1k_ref, w1v_ref, w2_ref, o_ref, wbd):
    kv = pl.program_id(0)
    nc = o_ref.shape[2]
    half = CMP_BLOCK // 2
    dh = NSA_HEAD_DIM
    bf16 = jnp.bfloat16

    @pl.when(pl.program_id(1) == 0)
    def _():
        def build(w1_ref):
            zero = jnp.zeros((dh, CMP_HIDDEN), bf16)
            for a in range(2):
                for l in range(half):
                    r0 = (a * half + l) * dh
                    wl = w1_ref[r0:r0 + dh, :].astype(bf16)
                    wbd[a, l * NSA_KV:(l + 1) * NSA_KV, :] = jnp.concatenate(
                        [jnp.concatenate([wl, zero], axis=1), jnp.concatenate([zero, wl], axis=1)], axis=0)
        pl.when(kv == 0)(functools.partial(build, w1k_ref))
        pl.when(kv == 1)(functools.partial(build, w1v_ref))

    r = jnp.concatenate([x_ref[pl.ds(l, nc, stride=CMP_STRIDE), :] for l in range(CMP_STRIDE)], axis=1)
    a = jnp.dot((r + pos_ref[0, 0:1, :]).astype(bf16), wbd[0], preferred_element_type=jnp.float32)
    b = jnp.dot((r + pos_ref[0, 1:2, :]).astype(bf16), wbd[1], preferred_element_type=jnp.float32)
    hid = a + pltpu.roll(b, nc - 1, 0)
    o_ref[0, 0] = jnp.dot(_gelu_tanh(hid).astype(bf16), w2_ref[0], preferred_element_type=jnp.float32)


def _compress(pf, pos2, w1_k, w1_v, w2p, bsz, seq):
    nc = seq // CMP_STRIDE
    width = CMP_STRIDE * NSA_KV
    hid = NSA_GROUPS * CMP_HIDDEN
    assert CMP_BLOCK == 2 * CMP_STRIDE
    return pl.pallas_call(
        _compress_kernel,
        grid=(2, bsz),
        in_specs=[pl.BlockSpec((seq, NSA_KV), lambda k, b: (b, PF_KC // NSA_KV + k)),
                  pl.BlockSpec((1, 2, width), lambda k, b: (k, 0, 0)),
                  pl.BlockSpec(w1_k.shape, lambda k, b: (0, 0)),
                  pl.BlockSpec(w1_v.shape, lambda k, b: (0, 0)),
                  pl.BlockSpec((1, hid, LANES), lambda k, b: (k, 0, 0))],
        out_specs=pl.BlockSpec((1, 1, nc, LANES), lambda k, b: (b, k, 0, 0)),
        out_shape=jax.ShapeDtypeStruct((bsz, 2, nc, LANES), jnp.float32),
        scratch_shapes=[pltpu.VMEM((2, width, hid), jnp.bfloat16)],
        compiler_params=_cparams(2), name="compress",
    )(pf, pos2, w1_k, w1_v, w2p)


def _prep_compress_weights(pos_k, pos_v, w2_k, w2_v):
    eye = jnp.eye(NSA_GROUPS, dtype=jnp.float32)
    half = CMP_BLOCK // 2

    def w2_both(w2):
        return jnp.einsum('jd,gh->gjhd', w2, eye).reshape(NSA_GROUPS * CMP_HIDDEN, NSA_GROUPS * NSA_HEAD_DIM)

    def pos_both(pos):
        p = pos.reshape(2, half, 1, NSA_HEAD_DIM)
        return jnp.broadcast_to(p, (2, half, NSA_GROUPS, NSA_HEAD_DIM)).reshape(2, half * NSA_KV)

    pos2 = jnp.stack([pos_both(pos_k), pos_both(pos_v)]).astype(jnp.float32)
    w2p = jnp.stack([w2_both(w2_k), w2_both(w2_v)]).astype(jnp.bfloat16)
    return pos2, w2p


def _pair_lanes(x, g):
    sw = pltpu.roll(x, LANES // 2, 1)
    lane = lax.broadcasted_iota(jnp.int32, x.shape, 1)
    own = (lane < LANES // 2) == (g == 0)
    return jnp.where(own, x, sw)


def _group_rows_t(x, g):
    xt = x.T
    half = LANES // 2
    return jnp.where(g == 0, xt[0:half, :], xt[half:LANES, :])


def _nsa_step_kernel(q_ref, qn_ref, ks_ref, kw_ref, vs_ref, vw_ref, kcvc_ref, bc_ref, bsel_ref, bwin_ref,
                     small_ref, za_ref, o_ref,
                     ksa, kwp, vst, vwt, kcs, vct, qa_s, qp_s, ocmp_s, sbuf, *, n_cmp, n_sel, n_top):
    g = pl.program_id(0)
    step_id = pl.program_id(2)
    tq = TILE
    dh = NSA_HEAD_DIM
    seq = ks_ref.shape[0]
    nq = seq // TILE
    nc = kcvc_ref.shape[2]
    n_wt = bwin_ref.shape[1] - 1
    heads = range(NSA_HPG)
    f32 = jnp.float32
    bf16 = jnp.bfloat16
    nt_dims = (((1,), (1,)), ((), ()))
    gk = SEL_GROUP * TILE

    def select_tile(src_ref, r0, tile, slot):
        top = lax.broadcasted_iota(jnp.int32, (LANES, tq), 0) < dh
        eye = (lax.broadcasted_iota(jnp.int32, (LANES, LANES), 0)
               == lax.broadcasted_iota(jnp.int32, (LANES, LANES), 1)).astype(bf16)
        qh_t = []
        for hh in heads:
            if hh % 2 == 0:
                blk_t = lax.dot_general(eye, src_ref[r0:r0 + tq, (hh // 2) * LANES:(hh // 2 + 1) * LANES], nt_dims,
                                        preferred_element_type=f32)
            qh_t.append(pltpu.roll(blk_t, LANES - dh, 0) if hh % 2 else blk_t)
        qp_s[slot] = jnp.concatenate([jnp.where(top, qh_t[hh], 0.0) for hh in heads], axis=1).astype(bf16)
        q_all = qp_s[slot]
        s_c = jnp.dot(kcs[...], q_all, preferred_element_type=f32)
        bias_c = jnp.concatenate([bc_ref[hh, tile] for hh in heads], axis=1)
        valid = bias_c > 0.5 * NEG
        s_c = s_c + bias_c
        e = jnp.where(valid, jnp.exp2(s_c - jnp.max(s_c, axis=0, keepdims=True)), 0.0)
        den = jnp.maximum(jnp.sum(e, axis=0, keepdims=True), 1e-30)
        p_c = e * (1.0 / den)
        psum = sum(p_c[:, hh * tq:(hh + 1) * tq] for hh in heads)
        sj = lax.broadcasted_iota(jnp.int32, (n_sel, nc), 0)
        ci = lax.broadcasted_iota(jnp.int32, (n_sel, nc), 1)
        overlap = ((ci * CMP_STRIDE < (sj + 1) * SEL_BLOCK) & (ci * CMP_STRIDE + CMP_BLOCK > sj * SEL_BLOCK)
                   & (ci < n_cmp)).astype(bf16)
        imp_t = sum(jnp.dot(overlap, part, preferred_element_type=f32)
                    for part in _split_bf16(psum, 3))
        ocmp_s[slot] = jnp.dot(vct[...], p_c.astype(bf16), preferred_element_type=f32)
        blk = lax.broadcasted_iota(jnp.int32, (n_sel, tq), 0)
        cur = (tile * tq + lax.broadcasted_iota(jnp.int32, (n_sel, tq), 1)) >> 6
        forced = (blk == 0) | (blk == cur) | (blk == cur - 1)
        imp_t = jnp.where(forced, FORCE, jnp.where(blk > cur, -FORCE, imp_t))
        rank = jnp.zeros((n_sel, tq), jnp.int32)
        for j in range(n_sel):
            other = imp_t[j:j + 1, :]
            ahead = (other > imp_t) | ((other == imp_t) & (blk > j))
            rank = rank + ahead.astype(jnp.int32)
        selb = jnp.where(rank < n_top, 0.0, NEG)
        pieces = [jnp.zeros((SEL_LANE0, tq), f32), selb]
        if LANES - SEL_LANE0 - n_sel > 0:
            pieces.append(jnp.zeros((LANES - SEL_LANE0 - n_sel, tq), f32))
        below = jnp.concatenate(pieces, axis=0)
        qa_s[slot] = jnp.concatenate([jnp.where(top, qh_t[hh], below) for hh in heads], axis=1).astype(bf16)

    @pl.when(step_id == 0)
    def _():
        row = lax.broadcasted_iota(jnp.int32, (seq, LANES), 0)
        lane = lax.broadcasted_iota(jnp.int32, (seq, LANES), 1)
        onehot = (lane - SEL_LANE0) == (row >> 6)
        own = lane < NSA_HEAD_DIM
        ks_g = jnp.where(own, _pair_lanes(ks_ref[...].astype(f32), g), 0.0)
        ksa[...] = jnp.where(onehot, 1.0, ks_g).astype(bf16)
        pad = (n_wt - 1) * TILE
        kwp[0:pad, :] = jnp.zeros((pad, LANES), bf16)
        kwp[pad:pad + seq, :] = jnp.where(own, _pair_lanes(kw_ref[...].astype(f32), g), 0.0).astype(bf16)
        for kt in range(n_wt - 1):
            vwt[kt] = jnp.zeros(vwt.shape[1:], bf16)
        extra = vst.shape[1] - dh
        ones_row = (lax.broadcasted_iota(jnp.int32, (extra, TILE), 0) == 0).astype(bf16)
        for kt in range(nq):
            rows = slice(kt * TILE, (kt + 1) * TILE)
            vst[kt, 0:dh, :] = _group_rows_t(vs_ref[rows, :].astype(f32), g).astype(bf16)
            vst[kt, dh:dh + extra, :] = ones_row
            vwt[kt + n_wt - 1, 0:dh, :] = _group_rows_t(vw_ref[rows, :].astype(f32), g).astype(bf16)
            vwt[kt + n_wt - 1, dh:dh + extra, :] = ones_row
        kcs[...] = _pair_lanes(kcvc_ref[0, 0], g).astype(bf16)
        for ct in range(nc // TILE):
            rows = slice(ct * TILE, (ct + 1) * TILE)
            vct[:, rows] = _group_rows_t(kcvc_ref[0, 1, rows, :], g).astype(bf16)
        select_tile(q_ref, 0, 0, 0)

    tps = NSA_TILES_PER_STEP

    def one_tile(n_groups, slot):
        qt = step_id * tps + slot
        rows = slice(slot * tq, (slot + 1) * tq)
        q_all = qp_s[slot]
        q_aug = qa_s[slot]
        kw_rows = kwp[pl.ds(pl.multiple_of(qt * TILE, TILE), n_wt * TILE), :]
        s_w = jnp.dot(kw_rows, q_all, preferred_element_type=f32)

        def sel_scores(gi):
            s = jnp.dot(ksa[gi * gk:(gi + 1) * gk, :], q_aug, preferred_element_type=f32)
            tiles = []
            for t in range(SEL_GROUP):
                dt = qt - (gi * SEL_GROUP + t)
                idx = jnp.where(dt >= 0, dt, nq)
                tiles.append(jnp.concatenate([bsel_ref[hh, idx] for hh in heads], axis=1))
            sbuf[gi % 2] = s + jnp.concatenate(tiles, axis=0)

        sel_scores(0)
        if slot + 1 < tps:
            select_tile(q_ref, (slot + 1) * tq, qt + 1, slot + 1)
        else:
            select_tile(qn_ref, 0, jnp.minimum(qt + 1, nq - 1), 0)

        w_tiles = []
        for t in range(n_wt):
            dt = n_wt - 1 - t
            idx = jnp.where(qt >= dt, dt, n_wt)
            w_tiles.append(jnp.concatenate([bwin_ref[hh, idx] for hh in heads], axis=1))
        s_w = s_w + jnp.concatenate(w_tiles, axis=0)
        p_w = jnp.exp2(s_w - jnp.max(s_w, axis=0, keepdims=True))
        v_w = jnp.concatenate([vwt[qt + t] for t in range(n_wt)], axis=1)
        acc_w = jnp.dot(v_w, p_w.astype(bf16), preferred_element_type=f32)
        o_win = acc_w[0:dh, :] * (1.0 / acc_w[dh:dh + 1, :])

        m = jnp.full((1, NSA_HPG * tq), NEG, f32)
        acc = jnp.zeros((vst.shape[1], NSA_HPG * tq), f32)
        for gi in range(n_groups):
            if gi + 1 < n_groups:
                sel_scores(gi + 1)
            s = sbuf[gi % 2]
            m_new = jnp.maximum(m, jnp.max(s, axis=0, keepdims=True))
            alpha = jnp.exp2(m - m_new)
            p = jnp.exp2(s - m_new)
            v_t = jnp.concatenate([vst[gi * SEL_GROUP + t] for t in range(SEL_GROUP)], axis=1)
            acc = alpha * acc + jnp.dot(v_t, p.astype(bf16), preferred_element_type=f32)
            m = m_new
        o_sel = acc[0:dh, :] * (1.0 / acc[dh:dh + 1, :])
        o_cmp = ocmp_s[slot]

        gates_t = jax.nn.sigmoid(small_ref[rows, :]).T
        mixed = []
        for hh in heads:
            def gate(br):
                r = br * NSA_HEADS + hh
                return jnp.where(g == 0, gates_t[r:r + 1, :], gates_t[r + NSA_HPG:r + NSA_HPG + 1, :])
            cols = slice(hh * tq, (hh + 1) * tq)
            mixed.append(gate(0) * o_cmp[:, cols] + gate(1) * o_sel[:, cols] + gate(2) * o_win[:, cols])
        o = jnp.concatenate([jnp.concatenate(mixed[2 * j:2 * j + 2], axis=0).T for j in range(NSA_HPG // 2)],
                            axis=1)
        z = za_ref[rows, :]
        o_ref[rows, :] = (o * (z * jax.nn.sigmoid(z))).astype(o_ref.dtype)

    def step(n_groups):
        for slot in range(tps):
            one_tile(n_groups, slot)

    for n_groups in range(1, nq // SEL_GROUP + 1):
        pl.when(step_id // (SEL_GROUP // tps) == n_groups - 1)(functools.partial(step, n_groups))


def _nsa(pb, pf, kcvc, bias_sel, bias_win, bias_cmp, bsz, seq):
    nq = seq // TILE
    nc = seq // CMP_STRIDE
    n_cmp = nc - CMP_BLOCK // CMP_STRIDE + 1
    n_sel = seq // SEL_BLOCK
    n_top = min(SEL_TOPK, n_sel)
    nwin = bias_win.shape[1]
    tps = NSA_TILES_PER_STEP
    assert nq % SEL_GROUP == 0 and nc % TILE == 0 and bias_sel.shape[1] == nq + 1
    assert tps >= 2 and SEL_GROUP % tps == 0
    ns = nq // tps
    rows = tps * TILE
    gw = NSA_HPG * NSA_HEAD_DIM
    kern = functools.partial(_nsa_step_kernel, n_cmp=n_cmp, n_sel=n_sel, n_top=n_top)
    return pl.pallas_call(
        kern,
        grid=(NSA_GROUPS, bsz, ns),
        in_specs=[
            pl.BlockSpec((rows, gw), lambda g, b, t: (b * ns + t, PB_QP // gw + g)),
            pl.BlockSpec((TILE, gw), lambda g, b, t: (b * nq + jnp.minimum((t + 1) * tps, nq - 1),
                                                      PB_QP // gw + g)),
            pl.BlockSpec((seq, LANES), lambda g, b, t: (b, PB_KS // LANES)),
            pl.BlockSpec((seq, LANES), lambda g, b, t: (b, PB_KW // LANES)),
            pl.BlockSpec((seq, LANES), lambda g, b, t: (b, PB_VS // LANES)),
            pl.BlockSpec((seq, LANES), lambda g, b, t: (b, PB_VW // LANES)),
            pl.BlockSpec((1, 2, nc, LANES), lambda g, b, t: (b, 0, 0, 0)),
            pl.BlockSpec((NSA_HPG, nq, nc, TILE), lambda g, b, t: (g, 0, 0, 0)),
            pl.BlockSpec((NSA_HPG, nq + 1, TILE, TILE), lambda g, b, t: (g, 0, 0, 0)),
            pl.BlockSpec((NSA_HPG, nwin, TILE, TILE), lambda g, b, t: (g, 0, 0, 0)),
            pl.BlockSpec((rows, LANES), lambda g, b, t: (b * ns + t, PF_SMALL // LANES)),
            pl.BlockSpec((rows, gw), lambda g, b, t: (b * ns + t, PF_ZA // gw + g)),
        ],
        out_specs=pl.BlockSpec((rows, gw), lambda g, b, t: (b * ns + t, g)),
        out_shape=jax.ShapeDtypeStruct((bsz * seq, NSA_WIDTH), jnp.bfloat16),
        scratch_shapes=[
            pltpu.VMEM((seq, LANES), jnp.bfloat16),
            pltpu.VMEM((seq + (nwin - 2) * TILE, LANES), jnp.bfloat16),
            pltpu.VMEM((nq, NSA_V_ROWS, TILE), jnp.bfloat16),
            pltpu.VMEM((nq + nwin - 2, NSA_V_ROWS, TILE), jnp.bfloat16),
            pltpu.VMEM((nc, LANES), jnp.bfloat16),
            pltpu.VMEM((NSA_HEAD_DIM, nc), jnp.bfloat16),
            pltpu.VMEM((tps, LANES, NSA_HPG * TILE), jnp.bfloat16),
            pltpu.VMEM((tps, LANES, NSA_HPG * TILE), jnp.bfloat16),
            pltpu.VMEM((tps, NSA_HEAD_DIM, NSA_HPG * TILE), jnp.float32),
            pltpu.VMEM((2, SEL_GROUP * TILE, NSA_HPG * TILE), jnp.float32),
        ],
        compiler_params=_cparams(3), name="nsa",
    )(pb, pb, pb, pb, pb, pb, kcvc, bias_cmp, bias_sel, bias_win, pf, pf)


def _split_bf16(a, n):
    parts = []
    for _ in range(n - 1):
        hi = a.astype(jnp.bfloat16)
        parts.append(hi)
        a = a - hi.astype(jnp.float32)
    parts.append(a.astype(jnp.bfloat16))
    return parts


def _dot3(a, b):
    ah, al = _split_bf16(a, 2)
    bh, bl = _split_bf16(b, 2)
    f32 = jnp.float32
    return (jnp.dot(ah, bh, preferred_element_type=f32) + jnp.dot(al, bh, preferred_element_type=f32)
            + jnp.dot(ah, bl, preferred_element_type=f32))


def _softplus(x):
    return jnp.maximum(x, 0.0) + jnp.log(1.0 + jnp.exp(-jnp.abs(x)))


def _dn_kernel(scal_ref, q_ref, k_ref, v_ref, small_ref, z_ref, cw_ref, nw_ref, o_ref,
               qn, kn, vn, bet, gl, mm, nn, qq, oo, dd, ss, *, seq):
    h = pl.program_id(1)
    n_rows = q_ref.shape[0]
    nb = n_rows // seq
    c = DN_CHUNK
    n_chunks = seq // c
    d = DN_HEAD_DIM

    head = 8

    def conv_body(x_ref, which):
        n = n_rows - head
        y = x_ref[pl.ds(head, n), :] * cw_ref[which, DN_CONV - 1:DN_CONV, :]
        for j in range(DN_CONV - 1):
            y = y + x_ref[pl.ds(head - (DN_CONV - 1 - j), n), :] * cw_ref[which, j:j + 1, :]
        return y

    def conv_head(x_ref, which, r0):
        x = x_ref[pl.ds(r0, head), :]
        rowi = lax.broadcasted_iota(jnp.int32, (head, d), 0)
        y = x * cw_ref[which, DN_CONV - 1:DN_CONV, :]
        for j in range(DN_CONV - 1):
            sh = DN_CONV - 1 - j
            y = y + jnp.where(rowi >= sh, pltpu.roll(x, sh, 0), 0.0) * cw_ref[which, j:j + 1, :]
        return y

    def silu(y):
        return y * jax.nn.sigmoid(y)

    def l2n(t):
        return t * lax.rsqrt(jnp.sum(t * t, axis=-1, keepdims=True) + 1e-6)

    finish = (lambda y: l2n(silu(y)) * (d ** -0.5), lambda y: l2n(silu(y)), silu)
    for which, (x_ref, dst) in enumerate(((q_ref, qn), (k_ref, kn), (v_ref, vn))):
        dst[pl.ds(head, n_rows - head), :] = finish[which](conv_body(x_ref, which))
        for b in range(nb):
            dst[pl.ds(b * seq, head), :] = finish[which](conv_head(x_ref, which, b * seq))
    small = small_ref[...]
    lane = lax.broadcasted_iota(jnp.int32, small.shape, 1)
    beta_in = jnp.sum(jnp.where(lane == SMALL_BETA + h, small, 0.0), axis=-1, keepdims=True)
    a_in = jnp.sum(jnp.where(lane == SMALL_A + h, small, 0.0), axis=-1, keepdims=True)
    bet[...] = jnp.broadcast_to(jax.nn.sigmoid(beta_in), (n_rows, d))
    gl[...] = jnp.broadcast_to(-jnp.exp(scal_ref[0, h]) * _softplus(a_in + scal_ref[1, h]), (n_rows, d))

    f32 = jnp.float32
    bf16 = jnp.bfloat16
    nt_dims = (((1,), (1,)), ((), ()))
    ri = lax.broadcasted_iota(jnp.int32, (c, 2 * c), 0)
    lane2 = lax.broadcasted_iota(jnp.int32, (c, 2 * c), 1)
    first = lane2 < c
    cj = lane2 & (c - 1)
    incl = ri >= cj
    strict = ri > cj
    tril_b = (lax.broadcasted_iota(jnp.int32, (c, c), 0) >= lax.broadcasted_iota(jnp.int32, (c, c), 1)).astype(bf16)
    bs = DN_INV_BLOCK
    sh = bs.bit_length() - 1
    same_diag = (ri >> sh) == (cj >> sh)
    level_masks = []
    while (1 << sh) < c:
        level_masks.append(((ri >> (sh + 1)) == (cj >> (sh + 1))) & ((ri >> sh) > (cj >> sh)))
        sh += 1
    bd_r = lax.broadcasted_iota(jnp.int32, (2 * c, 2 * c), 0)
    bd_c = lax.broadcasted_iota(jnp.int32, (2 * c, 2 * c), 1)
    block_diag = (bd_r >= c) == (bd_c >= c)

    def dot_pair(a, b):
        bb = b.astype(bf16)
        bb = jnp.where(block_diag, jnp.concatenate([bb, bb], axis=0), jnp.zeros((), bf16))
        return jnp.dot(a.astype(bf16), bb, preferred_element_type=f32)

    def stack_diag(x0, x1):
        z = jnp.zeros_like(x0)
        return jnp.concatenate([jnp.concatenate([x0, z], axis=1), jnp.concatenate([z, x1], axis=1)], axis=0)

    def chunk_prep(it, carry):
        ids = [it * DN_PREP_UNROLL + cc for cc in range(DN_PREP_UNROLL)]
        rows = [pl.ds(pl.multiple_of(i * c, c), c) for i in ids]
        pairs = range(0, DN_PREP_UNROLL, 2)
        ks = [kn[r, :] for r in rows]
        betas = [bet[r, :] for r in rows]
        gcbs = [sum(jnp.dot(tril_b, part, preferred_element_type=f32)
                    for part in _split_bf16(gl[r, :], DN_DECAY_PARTS)) for r in rows]
        kbs = [k * beta for k, beta in zip(ks, betas)]
        k2ts = [jnp.concatenate([ks[j], ks[j + 1]], axis=0).T.astype(bf16) for j in pairs]

        def against_pair_keys(xs):
            return [jnp.where(first,
                              jnp.dot(xs[j].astype(bf16), k2t, preferred_element_type=f32),
                              jnp.dot(xs[j + 1].astype(bf16), k2t, preferred_element_type=f32))
                    for j, k2t in zip(pairs, k2ts)]

        a_kks = against_pair_keys(kbs)
        decays = []
        for j in pairs:
            gc_col = jnp.where(first, gcbs[j], gcbs[j + 1])
            gc_row = jnp.concatenate([gcbs[j], gcbs[j + 1]], axis=0).T[0:c, :]
            decays.append(jnp.where(incl, jnp.exp(jnp.where(incl, gc_col - gc_row, 0.0)), 0.0))
        lows = [jnp.where(strict, a * dec, 0.0) for a, dec in zip(a_kks, decays)]
        pws = [jnp.where(same_diag, -low, 0.0) for low in lows]
        es = list(pws)
        for _ in range(max(1, (bs - 1).bit_length()) - 1):
            pws = [dot_pair(pw, pw) for pw in pws]
            es = [e + pw + dot_pair(e, pw) for e, pw in zip(es, pws)]
        for below in level_masks:
            offs = [jnp.where(below, low, 0.0) for low in lows]
            xs = [off + dot_pair(e, off) for e, off in zip(es, offs)]
            es = [e - (x + dot_pair(x, e)) for e, x in zip(es, xs)]
        egcs = [jnp.exp(gcb) for gcb in gcbs]
        rhss = [jnp.concatenate([vn[r, :] * beta, kb * egc], axis=1)
                for r, beta, kb, egc in zip(rows, betas, kbs, egcs)]
        uwbs = []
        for e, j in zip(es, pairs):
            e_hi, e_lo = _split_bf16(e, 2)
            r2 = stack_diag(rhss[j], rhss[j + 1]).astype(bf16)
            er = jnp.dot(e_hi, r2, preferred_element_type=f32) + jnp.dot(e_lo, r2, preferred_element_type=f32)
            uwbs.append((rhss[j] + er[:, 0:2 * d]).astype(bf16))
            uwbs.append((rhss[j + 1] + er[:, 2 * d:4 * d]).astype(bf16))
        qs = [qn[r, :] for r in rows]
        a_qks = [a * dec for a, dec in zip(against_pair_keys(qs), decays)]
        g_lasts = [gcb[c - 1:c, :] for gcb in gcbs]
        kdec_ts = []
        for k, gcb, g_last in zip(ks, gcbs, g_lasts):
            kdec = k * jnp.exp(g_last - gcb)
            kdec_ts.append(jnp.concatenate([kdec, jnp.zeros_like(kdec)], axis=0).T[:, 0:c].astype(bf16))
        nms = [jnp.dot(kt, uwb, preferred_element_type=f32) for kt, uwb in zip(kdec_ts, uwbs)]
        oq2s = [jnp.dot(a.astype(bf16), stack_diag(uwbs[j], uwbs[j + 1]), preferred_element_type=f32)
                for a, j in zip(a_qks, pairs)]
        oqs = []
        for oq2 in oq2s:
            oqs += [oq2[:, 0:2 * d], oq2[:, 2 * d:4 * d]]
        for i, r, nm, oq, q, egc, g_last in zip(ids, rows, nms, oqs, qs, egcs, g_lasts):
            m0 = pl.ds(pl.multiple_of(i * d, d), d)
            nn[m0, :] = nm[:, 0:d]
            mm[m0, :] = nm[:, d:2 * d].astype(bf16)
            oo[r, :] = oq[:, 0:d]
            qq[r, :] = (q * egc - oq[:, d:2 * d]).astype(bf16)
            dd[pl.ds(pl.multiple_of(i * 8, 8), 8), :] = jnp.broadcast_to(jnp.exp(g_last), (8, d))
        return carry

    lax.fori_loop(0, nb * n_chunks // DN_PREP_UNROLL, chunk_prep, 0)

    def chunk_scan(i, states):
        ids = [b * n_chunks + i for b in range(nb)]
        blocks = [pl.ds(pl.multiple_of(j * d, d), d) for j in ids]
        sbs = [s.astype(bf16) for s in states]
        for blk, sb in zip(blocks, sbs):
            ss[blk, :] = sb
        prods = [jnp.dot(mm[blk, :], sb, preferred_element_type=jnp.float32) for blk, sb in zip(blocks, sbs)]
        return tuple(s * dd[pl.ds(pl.multiple_of(j * 8, 8), 1), :] - pr + nn[blk, :]
                     for s, j, blk, pr in zip(states, ids, blocks, prods))

    lax.fori_loop(0, n_chunks, chunk_scan, tuple(jnp.zeros((d, d), jnp.float32) for _ in range(nb)))

    nw = nw_ref[...]

    def chunk_out(it, carry):
        ids = [it * DN_OUT_UNROLL + cc for cc in range(DN_OUT_UNROLL)]
        rows = [pl.ds(pl.multiple_of(i * c, c), c) for i in ids]
        outs = [jnp.dot(qq[r, :], ss[pl.ds(pl.multiple_of(i * d, d), d), :], preferred_element_type=jnp.float32)
                + oo[r, :] for i, r in zip(ids, rows)]
        for r, o in zip(rows, outs):
            o = o * lax.rsqrt(jnp.mean(o * o, axis=-1, keepdims=True) + 1e-6) * nw
            z = z_ref[r, :]
            o_ref[r, :] = (o * (z * jax.nn.sigmoid(z))).astype(o_ref.dtype)
        return carry

    lax.fori_loop(0, nb * n_chunks // DN_OUT_UNROLL, chunk_out, 0)


def _deltanet(pf, conv_w, a_log, dt_bias, norm_w, bsz, seq):
    d = DN_HEAD_DIM
    nb = DN_BATCHES if bsz % DN_BATCHES == 0 else 1
    rows = nb * seq
    n_chunks = rows // DN_CHUNK
    qkv0 = PF_QKVB // d
    scal = jnp.stack([a_log, dt_bias]).astype(jnp.float32)
    f32 = jnp.float32
    bf16 = jnp.bfloat16
    cw4 = conv_w.astype(f32).reshape(DN_CONV, 3, DN_HEADS, d).transpose(2, 1, 0, 3)
    assert 2 * DN_CHUNK == d and n_chunks % DN_PREP_UNROLL == 0 and n_chunks % DN_OUT_UNROLL == 0
    return pl.pallas_call(
        functools.partial(_dn_kernel, seq=seq),
        grid=(bsz // nb, DN_HEADS),
        in_specs=[
            pl.BlockSpec(memory_space=pltpu.SMEM),
            pl.BlockSpec((rows, d), lambda b, h: (b, qkv0 + h)),
            pl.BlockSpec((rows, d), lambda b, h: (b, qkv0 + DN_HEADS + h)),
            pl.BlockSpec((rows, d), lambda b, h: (b, qkv0 + 2 * DN_HEADS + h)),
            pl.BlockSpec((rows, LANES), lambda b, h: (b, PF_SMALL // LANES)),
            pl.BlockSpec((rows, d), lambda b, h: (b, PF_ZB // d + h)),
            pl.BlockSpec((None, 3, DN_CONV, d), lambda b, h: (h, 0, 0, 0)),
            pl.BlockSpec((1, d), lambda b, h: (0, 0)),
        ],
        out_specs=pl.BlockSpec((rows, d), lambda b, h: (b, h)),
        out_shape=jax.ShapeDtypeStruct((bsz * seq, DN_WIDTH), jnp.bfloat16),
        scratch_shapes=[
            pltpu.VMEM((rows, d), f32), pltpu.VMEM((rows, d), f32), pltpu.VMEM((rows, d), f32),
            pltpu.VMEM((rows, d), f32), pltpu.VMEM((rows, d), f32),
            pltpu.VMEM((n_chunks * d, d), bf16), pltpu.VMEM((n_chunks * d, d), f32),
            pltpu.VMEM((rows, d), bf16), pltpu.VMEM((rows, d), f32),
            pltpu.VMEM((n_chunks * 8, d), f32),
            pltpu.VMEM((n_chunks * d, d), bf16),
        ],
        compiler_params=_cparams(2), name="deltanet",
    )(scal, pf, pf, pf, pf, pf, cw4, norm_w.astype(f32).reshape(1, d))


def _out_kernel(oa_ref, ob_ref, gma_ref, gmb_ref, x_ref, p_ref, wa_ref, wb_ref, wo_ref, wpg_ref, wp_ref,
                lng_ref, lnb_ref, o_ref):
    f32 = jnp.float32
    bf16 = jnp.bfloat16
    tm = o_ref.shape[0]
    sub = tm // OUT_SUBTILES
    parts = [slice(i * sub, (i + 1) * sub) for i in range(OUT_SUBTILES)]
    y_a = [jnp.dot(oa_ref[r, :], wa_ref[...], preferred_element_type=f32) for r in parts]
    y_b = [jnp.dot(ob_ref[r, :], wb_ref[...], preferred_element_type=f32) for r in parts]
    pw = [jnp.dot(p_ref[r, :].astype(bf16), wp_ref[...], preferred_element_type=f32) for r in parts]
    mix = [(jax.nn.sigmoid(gma_ref[r, :]) * ya + jax.nn.sigmoid(gmb_ref[r, :]) * yb).astype(bf16)
           for r, ya, yb in zip(parts, y_a, y_b)]
    h = [DEEPNORM_ALPHA * x_ref[r, :] + jnp.dot(mx, wo_ref[...], preferred_element_type=f32)
         for r, mx in zip(parts, mix)]
    gate = [jnp.dot(hh.astype(bf16), wpg_ref[...], preferred_element_type=f32) for hh in h]
    for r, hh, gt, pp in zip(parts, h, gate, pw):
        hh = hh + jax.nn.sigmoid(gt) * pp
        mu = jnp.mean(hh, axis=-1, keepdims=True)
        hc = hh - mu
        var = jnp.mean(hc * hc, axis=-1, keepdims=True)
        o_ref[r, :] = (hc * lax.rsqrt(var + 1e-5) * lng_ref[...] + lnb_ref[...]).astype(o_ref.dtype)


def _out_block(o_a, o_b, pf, x2, p2, wa, wb, wo, wpg, wp, ln_g, ln_b, tm):
    t = x2.shape[0]
    bf = jnp.bfloat16

    def full(shape):
        return pl.BlockSpec(shape, lambda i: (0, 0))

    return pl.pallas_call(
        _out_kernel,
        grid=(t // tm,),
        in_specs=[
            pl.BlockSpec((tm, NSA_WIDTH), lambda i: (i, 0)),
            pl.BlockSpec((tm, DN_WIDTH), lambda i: (i, 0)),
            pl.BlockSpec((tm, D_MODEL), lambda i: (i, PF_GM // D_MODEL)),
            pl.BlockSpec((tm, D_MODEL), lambda i: (i, PF_GM // D_MODEL + 1)),
            pl.BlockSpec((tm, D_MODEL), lambda i: (i, 0)),
            pl.BlockSpec((tm, PLE_DIM), lambda i: (i, 0)),
            full((NSA_WIDTH, D_MODEL)), full((DN_WIDTH, D_MODEL)), full((D_MODEL, D_MODEL)),
            full((D_MODEL, D_MODEL)), full((PLE_DIM, D_MODEL)), full((1, D_MODEL)), full((1, D_MODEL)),
        ],
        out_specs=pl.BlockSpec((tm, D_MODEL), lambda i: (i, 0)),
        out_shape=jax.ShapeDtypeStruct((t, D_MODEL), x2.dtype),
        compiler_params=_cparams(1), name="out_block",
    )(o_a, o_b, pf, pf, x2, p2, wa.astype(bf), wb.astype(bf), wo.astype(bf), wpg.astype(bf), wp.astype(bf),
      ln_g.astype(jnp.float32).reshape(1, D_MODEL), ln_b.astype(jnp.float32).reshape(1, D_MODEL))


def _layer(x, p, w_in, pos_k, pos_v, w1_k, w2_k, w1_v, w2_v, bias_tabs, conv_w, a_log, dt_bias, norm_w,
           w_a, w_b, w_o, w_ple, w_pg, ln_g, ln_b):
    bsz, seq, _ = x.shape
    t = bsz * seq
    x2 = x.reshape(t, D_MODEL)
    wb16, wf16 = _prep_w_in(w_in)
    pb, pf = _proj(x2, wb16, wf16, PROJ_TM if t % PROJ_TM == 0 else seq)

    pos2, w2p = _prep_compress_weights(pos_k, pos_v, w2_k, w2_v)
    kcvc = _compress(pf, pos2, w1_k, w1_v, w2p, bsz, seq)

    bias_sel, bias_win, bias_cmp = bias_tabs
    o_a = _nsa(pb, pf, kcvc, bias_sel, bias_win, bias_cmp, bsz, seq)
    o_b = _deltanet(pf, conv_w, a_log, dt_bias, norm_w, bsz, seq)
    out = _out_block(o_a, o_b, pf, x2, p.reshape(t, PLE_DIM), w_a, w_b, w_o, w_pg, w_ple, ln_g, ln_b,
                     OUT_TM if t % OUT_TM == 0 else seq)
    return out.reshape(bsz, seq, D_MODEL)


def kernel(x, p, w_in, cmp_pos_k, cmp_pos_v, cmp_w1_k, cmp_w2_k, cmp_w1_v, cmp_w2_v, rel_bias, dn_conv_w,
           dn_a_log, dn_dt_bias, dn_norm_w, w_branch_a, w_branch_b, w_out, w_ple, w_ple_gate, ln_g, ln_b):
    depth = w_in.shape[0]
    bias_tabs = _bias_tables(rel_bias, x.shape[1])
    for i in range(depth):
        x = _layer(x, p[i], w_in[i], cmp_pos_k[i], cmp_pos_v[i], cmp_w1_k[i], cmp_w2_k[i], cmp_w1_v[i],
                   cmp_w2_v[i], bias_tabs, dn_conv_w[i], dn_a_log[i], dn_dt_bias[i], dn_norm_w[i],
                   w_branch_a[i], w_branch_b[i], w_out[i], w_ple[i], w_ple_gate[i], ln_g[i], ln_b[i])
    return x
```

```python
import functools
import math

import numpy as np
import jax
import jax.numpy as jnp
from jax import lax
from jax.experimental import pallas as pl
from jax.experimental.pallas import tpu as pltpu

D_MODEL = 1024
PLE_DIM = 256
NSA_HEADS = 8
NSA_GROUPS = 2
NSA_HPG = NSA_HEADS // NSA_GROUPS
NSA_HEAD_DIM = 64
NSA_WIDTH = NSA_HEADS * NSA_HEAD_DIM
NSA_KV = NSA_GROUPS * NSA_HEAD_DIM
CMP_BLOCK = 32
CMP_STRIDE = 16
CMP_HIDDEN = 256
SEL_BLOCK = 64
SEL_TOPK = 8
WINDOW = 512
DN_HEADS = 4
DN_HEAD_DIM = 128
DN_WIDTH = DN_HEADS * DN_HEAD_DIM
DN_CONV = 4
DN_CHUNK = 64
NUM_BUCKETS = 32
REL_MAX_DIST = 1024
DEEPNORM_ALPHA = 2.0 ** 0.25
NEG = -1e30
FORCE = 1e6
LOG2E = 1.4426950408889634

LANES = 128
TILE = 128
SEL_LANE0 = 64
SEL_GROUP = 4
NSA_TILES_PER_STEP = 4
NSA_V_ROWS = NSA_HEAD_DIM + 16
VMEM_LIMIT = 56 * 1024 * 1024
PROJ_TM = 512
PROJ_TN = 1024
OUT_TM = 512
OUT_SUBTILES = 2
DN_INV_BLOCK = 16
DN_BATCHES = 2
DN_OUT_UNROLL = 16
DN_PAIR_UNROLL = 8
DN_DECAY_PARTS = 2
DN_PREP_UNROLL = 32

HIGHEST = lax.Precision.HIGHEST

PB_QP = 0
PB_KS = PB_QP + NSA_WIDTH
PB_KW = PB_KS + NSA_KV
PB_VS = PB_KW + NSA_KV
PB_VW = PB_VS + NSA_KV
PB_WIDTH = PB_VW + NSA_KV
PF_GM = 0
PF_QKVB = PF_GM + 2 * D_MODEL
PF_ZA = PF_QKVB + 3 * DN_WIDTH
PF_ZB = PF_ZA + NSA_WIDTH
PF_KC = PF_ZB + DN_WIDTH
PF_VC = PF_KC + NSA_KV
PF_SMALL = PF_VC + NSA_KV
PF_WIDTH = PF_SMALL + LANES
SMALL_BETA = 3 * NSA_HEADS
SMALL_A = SMALL_BETA + DN_HEADS


def _bucket_thresholds():
    max_exact = NUM_BUCKETS // 2
    span = NUM_BUCKETS - max_exact
    ratio = REL_MAX_DIST // max_exact
    thr = list(range(1, max_exact + 1))
    for k in range(1, span):
        n = max_exact
        while n ** span < max_exact ** span * ratio ** k:
            n += 1
        thr.append(n)
    return tuple(thr)


_THR = _bucket_thresholds()


def _cparams(n_axes):
    return pltpu.CompilerParams(dimension_semantics=("arbitrary",) * n_axes, vmem_limit_bytes=VMEM_LIMIT)


def _proj_kernel(x_ref, wb_ref, wf_ref, pb_ref, pf_ref):
    nt_dims = (((1,), (1,)), ((), ()))
    xb = x_ref[...].astype(jnp.bfloat16)
    pb_ref[...] = lax.dot_general(xb, wb_ref[...], nt_dims, preferred_element_type=jnp.float32).astype(pb_ref.dtype)
    n = pf_ref.shape[1]
    for c0 in range(0, n, PROJ_TN):
        c1 = min(c0 + PROJ_TN, n)
        pf_ref[:, c0:c1] = lax.dot_general(xb, wf_ref[c0:c1, :], nt_dims, preferred_element_type=jnp.float32)


def _proj(x2, wb, wf, tm):
    t, d = x2.shape
    resident = dict(pipeline_mode=pl.Buffered(1))
    return pl.pallas_call(
        _proj_kernel,
        grid=(t // tm,),
        in_specs=[pl.BlockSpec((tm, d), lambda i: (i, 0)),
                  pl.BlockSpec((PB_WIDTH, d), lambda i: (0, 0), **resident),
                  pl.BlockSpec((PF_WIDTH, d), lambda i: (0, 0), **resident)],
        out_specs=[pl.BlockSpec((tm, PB_WIDTH), lambda i: (i, 0)),
                   pl.BlockSpec((tm, PF_WIDTH), lambda i: (i, 0))],
        out_shape=[jax.ShapeDtypeStruct((t, PB_WIDTH), jnp.bfloat16),
                   jax.ShapeDtypeStruct((t, PF_WIDTH), jnp.float32)],
        compiler_params=_cparams(1), name="proj",
    )(x2, wb, wf)


def _prep_w_in(w):
    d = w.shape[0]
    wt = jnp.swapaxes(w, 0, 1)
    o = 0
    wq = wt[o:o + NSA_WIDTH]; o += NSA_WIDTH
    wkv = wt[o:o + 6 * NSA_KV]; o += 6 * NSA_KV
    wg = wt[o:o + 3 * NSA_HEADS]; o += 3 * NSA_HEADS
    wza = wt[o:o + NSA_WIDTH]; o += NSA_WIDTH
    wqkvb = wt[o:o + 3 * DN_WIDTH]; o += 3 * DN_WIDTH
    wbeta_a = wt[o:o + 2 * DN_HEADS]; o += 2 * DN_HEADS
    wzb = wt[o:o + DN_WIDTH]; o += DN_WIDTH
    wgm = wt[o:o + 2 * D_MODEL]
    wkcvc, wks, wvs, wkw, wvw = (wkv[0:2 * NSA_KV], wkv[2 * NSA_KV:3 * NSA_KV], wkv[3 * NSA_KV:4 * NSA_KV],
                                 wkv[4 * NSA_KV:5 * NSA_KV], wkv[5 * NSA_KV:6 * NSA_KV])
    wb = jnp.concatenate([wq * (NSA_HEAD_DIM ** -0.5 * LOG2E), wks, wkw, wvs, wvw], axis=0).astype(jnp.bfloat16)
    pad = jnp.zeros((LANES - 3 * NSA_HEADS - 2 * DN_HEADS, d), w.dtype)
    wf = jnp.concatenate([wgm, wqkvb, wza, wzb, wkcvc, wg, wbeta_a, pad], axis=0).astype(jnp.bfloat16)
    return wb, wf


def _bias_kernel(tab_ref, sel_ref, win_ref, cmp_ref, *, n_cmp):
    h = pl.program_id(0)

    def lookup(n):
        val = jnp.full(n.shape, tab_ref[0, h], jnp.float32)
        for b in range(1, NUM_BUCKETS):
            val = jnp.where(n >= _THR[b - 1], tab_ref[b, h], val)
        return val * LOG2E

    kj = lax.broadcasted_iota(jnp.int32, (TILE, TILE), 0)
    qi = lax.broadcasted_iota(jnp.int32, (TILE, TILE), 1)
    n_sel_tiles = sel_ref.shape[1] - 1
    n_win_tiles = win_ref.shape[1] - 1
    for dt in range(max(n_sel_tiles, n_win_tiles)):
        dist = dt * TILE + qi - kj
        v = lookup(jnp.maximum(dist, 0))
        if dt < n_sel_tiles:
            sel_ref[0, dt] = jnp.where(dist >= 0, v, NEG)
        if dt < n_win_tiles:
            win_ref[0, dt] = jnp.where((dist >= 0) & (dist < WINDOW), v, NEG)
    sel_ref[0, n_sel_tiles] = jnp.full((TILE, TILE), NEG, jnp.float32)
    win_ref[0, n_win_tiles] = jnp.full((TILE, TILE), NEG, jnp.float32)
    c = lax.broadcasted_iota(jnp.int32, cmp_ref.shape[2:], 0)
    for t in range(cmp_ref.shape[1]):
        s = t * TILE + lax.broadcasted_iota(jnp.int32, cmp_ref.shape[2:], 1)
        dist = s - (c * CMP_STRIDE + CMP_BLOCK - 1)
        cmp_ref[0, t] = jnp.where((dist >= 0) & (c < n_cmp), lookup(jnp.maximum(dist, 0)), NEG)


def _bias_tables(rel_bias, seq):
    nq = seq // TILE
    nwin = WINDOW // TILE + 1
    nc = seq // CMP_STRIDE
    n_cmp = nc - CMP_BLOCK // CMP_STRIDE + 1
    return pl.pallas_call(
        functools.partial(_bias_kernel, n_cmp=n_cmp),
        grid=(NSA_HEADS,),
        in_specs=[pl.BlockSpec(memory_space=pltpu.SMEM)],
        out_specs=[pl.BlockSpec((1, nq + 1, TILE, TILE), lambda h: (h, 0, 0, 0)),
                   pl.BlockSpec((1, nwin + 1, TILE, TILE), lambda h: (h, 0, 0, 0)),
                   pl.BlockSpec((1, nq, nc, TILE), lambda h: (h, 0, 0, 0))],
        out_shape=[jax.ShapeDtypeStruct((NSA_HEADS, nq + 1, TILE, TILE), jnp.float32),
                   jax.ShapeDtypeStruct((NSA_HEADS, nwin + 1, TILE, TILE), jnp.float32),
                   jax.ShapeDtypeStruct((NSA_HEADS, nq, nc, TILE), jnp.float32)],
        compiler_params=_cparams(1), name="bias_tables",
    )(rel_bias.astype(jnp.float32))


def _gelu_tanh(x):
    return x * (0.5 * (1.0 + jnp.tanh(math.sqrt(2.0 / math.pi) * (x + 0.044715 * (x * x * x)))))


def _compress_kernel(x_ref, pos_ref, w1k_ref, w1v_ref, w2_ref, o_ref, wbd):
    kv = pl.program_id(0)
    nc = o_ref.shape[2]
    half = CMP_BLOCK // 2
    dh = NSA_HEAD_DIM
    bf16 = jnp.bfloat16

    @pl.when(pl.program_id(1) == 0)
    def _():
        def build(w1_ref):
            zero = jnp.zeros((dh, CMP_HIDDEN), bf16)
            for a in range(2):
                for l in range(half):
                    r0 = (a * half + l) * dh
                    wl = w1_ref[r0:r0 + dh, :].astype(bf16)
                    wbd[a, l * NSA_KV:(l + 1) * NSA_KV, :] = jnp.concatenate(
                        [jnp.concatenate([wl, zero], axis=1), jnp.concatenate([zero, wl], axis=1)], axis=0)
        pl.when(kv == 0)(functools.partial(build, w1k_ref))
        pl.when(kv == 1)(functools.partial(build, w1v_ref))

    r = jnp.concatenate([x_ref[pl.ds(l, nc, stride=CMP_STRIDE), :] for l in range(CMP_STRIDE)], axis=1)
    a = jnp.dot((r + pos_ref[0, 0:1, :]).astype(bf16), wbd[0], preferred_element_type=jnp.float32)
    b = jnp.dot((r + pos_ref[0, 1:2, :]).astype(bf16), wbd[1], preferred_element_type=jnp.float32)
    hid = a + pltpu.roll(b, nc - 1, 0)
    o_ref[0, 0] = jnp.dot(_gelu_tanh(hid).astype(bf16), w2_ref[0], preferred_element_type=jnp.float32)


def _compress(pf, pos2, w1_k, w1_v, w2p, bsz, seq):
    nc = seq // CMP_STRIDE
    width = CMP_STRIDE * NSA_KV
    hid = NSA_GROUPS * CMP_HIDDEN
    assert CMP_BLOCK == 2 * CMP_STRIDE
    return pl.pallas_call(
        _compress_kernel,
        grid=(2, bsz),
        in_specs=[pl.BlockSpec((seq, NSA_KV), lambda k, b: (b, PF_KC // NSA_KV + k)),
                  pl.BlockSpec((1, 2, width), lambda k, b: (k, 0, 0)),
                  pl.BlockSpec(w1_k.shape, lambda k, b: (0, 0)),
                  pl.BlockSpec(w1_v.shape, lambda k, b: (0, 0)),
                  pl.BlockSpec((1, hid, LANES), lambda k, b: (k, 0, 0))],
        out_specs=pl.BlockSpec((1, 1, nc, LANES), lambda k, b: (b, k, 0, 0)),
        out_shape=jax.ShapeDtypeStruct((bsz, 2, nc, LANES), jnp.float32),
        scratch_shapes=[pltpu.VMEM((2, width, hid), jnp.bfloat16)],
        compiler_params=_cparams(2), name="compress",
    )(pf, pos2, w1_k, w1_v, w2p)


def _prep_compress_weights(pos_k, pos_v, w2_k, w2_v):
    eye = jnp.eye(NSA_GROUPS, dtype=jnp.float32)
    half = CMP_BLOCK // 2

    def w2_both(w2):
        return jnp.einsum('jd,gh->gjhd', w2, eye).reshape(NSA_GROUPS * CMP_HIDDEN, NSA_GROUPS * NSA_HEAD_DIM)

    def pos_both(pos):
        p = pos.reshape(2, half, 1, NSA_HEAD_DIM)
        return jnp.broadcast_to(p, (2, half, NSA_GROUPS, NSA_HEAD_DIM)).reshape(2, half * NSA_KV)

    pos2 = jnp.stack([pos_both(pos_k), pos_both(pos_v)]).astype(jnp.float32)
    w2p = jnp.stack([w2_both(w2_k), w2_both(w2_v)]).astype(jnp.bfloat16)
    return pos2, w2p


def _pair_lanes(x, g):
    sw = pltpu.roll(x, LANES // 2, 1)
    lane = lax.broadcasted_iota(jnp.int32, x.shape, 1)
    own = (lane < LANES // 2) == (g == 0)
    return jnp.where(own, x, sw)


def _group_rows_t(x, g):
    xt = x.T
    half = LANES // 2
    return jnp.where(g == 0, xt[0:half, :], xt[half:LANES, :])


def _nsa_step_kernel(q_ref, qn_ref, ks_ref, kw_ref, vs_ref, vw_ref, kcvc_ref, bc_ref, bsel_ref, bwin_ref,
                     small_ref, za_ref, o_ref,
                     ksa, kwp, vst, vwt, kcs, vct, qa_s, qp_s, ocmp_s, sbuf, *, n_cmp, n_sel, n_top):
    g = pl.program_id(0)
    step_id = pl.program_id(2)
    tq = TILE
    dh = NSA_HEAD_DIM
    seq = ks_ref.shape[0]
    nq = seq // TILE
    nc = kcvc_ref.shape[2]
    n_wt = bwin_ref.shape[1] - 1
    heads = range(NSA_HPG)
    f32 = jnp.float32
    bf16 = jnp.bfloat16
    nt_dims = (((1,), (1,)), ((), ()))
    gk = SEL_GROUP * TILE

    def select_tile(src_ref, r0, tile, slot):
        top = lax.broadcasted_iota(jnp.int32, (LANES, tq), 0) < dh
        eye = (lax.broadcasted_iota(jnp.int32, (LANES, LANES), 0)
               == lax.broadcasted_iota(jnp.int32, (LANES, LANES), 1)).astype(bf16)
        qh_t = []
        for hh in heads:
            if hh % 2 == 0:
                blk_t = lax.dot_general(eye, src_ref[r0:r0 + tq, (hh // 2) * LANES:(hh // 2 + 1) * LANES], nt_dims,
                                        preferred_element_type=f32)
            qh_t.append(pltpu.roll(blk_t, LANES - dh, 0) if hh % 2 else blk_t)
        qp_s[slot] = jnp.concatenate([jnp.where(top, qh_t[hh], 0.0) for hh in heads], axis=1).astype(bf16)
        q_all = qp_s[slot]
        s_c = jnp.dot(kcs[...], q_all, preferred_element_type=f32)
        bias_c = jnp.concatenate([bc_ref[hh, tile] for hh in heads], axis=1)
        valid = bias_c > 0.5 * NEG
        s_c = s_c + bias_c
        e = jnp.where(valid, jnp.exp2(s_c - jnp.max(s_c, axis=0, keepdims=True)), 0.0)
        den = jnp.maximum(jnp.sum(e, axis=0, keepdims=True), 1e-30)
        p_c = e * (1.0 / den)
        psum = sum(p_c[:, hh * tq:(hh + 1) * tq] for hh in heads)
        sj = lax.broadcasted_iota(jnp.int32, (n_sel, nc), 0)
        ci = lax.broadcasted_iota(jnp.int32, (n_sel, nc), 1)
        overlap = ((ci * CMP_STRIDE < (sj + 1) * SEL_BLOCK) & (ci * CMP_STRIDE + CMP_BLOCK > sj * SEL_BLOCK)
                   & (ci < n_cmp)).astype(bf16)
        imp_t = sum(jnp.dot(overlap, part, preferred_element_type=f32)
                    for part in _split_bf16(psum, 3))
        ocmp_s[slot] = jnp.dot(vct[...], p_c.astype(bf16), preferred_element_type=f32)
        blk = lax.broadcasted_iota(jnp.int32, (n_sel, tq), 0)
        cur = (tile * tq + lax.broadcasted_iota(jnp.int32, (n_sel, tq), 1)) >> 6
        forced = (blk == 0) | (blk == cur) | (blk == cur - 1)
        imp_t = jnp.where(forced, FORCE, jnp.where(blk > cur, -FORCE, imp_t))
        rank = jnp.zeros((n_sel, tq), jnp.int32)
        for j in range(n_sel):
            other = imp_t[j:j + 1, :]
            ahead = (other > imp_t) | ((other == imp_t) & (blk > j))
            rank = rank + ahead.astype(jnp.int32)
        selb = jnp.where(rank < n_top, 0.0, NEG)
        pieces = [jnp.zeros((SEL_LANE0, tq), f32), selb]
        if LANES - SEL_LANE0 - n_sel > 0:
            pieces.append(jnp.zeros((LANES - SEL_LANE0 - n_sel, tq), f32))
        below = jnp.concatenate(pieces, axis=0)
        qa_s[slot] = jnp.concatenate([jnp.where(top, qh_t[hh], below) for hh in heads], axis=1).astype(bf16)

    @pl.when(step_id == 0)
    def _():
        row = lax.broadcasted_iota(jnp.int32, (seq, LANES), 0)
        lane = lax.broadcasted_iota(jnp.int32, (seq, LANES), 1)
        onehot = (lane - SEL_LANE0) == (row >> 6)
        own = lane < NSA_HEAD_DIM
        ks_g = jnp.where(own, _pair_lanes(ks_ref[...].astype(f32), g), 0.0)
        ksa[...] = jnp.where(onehot, 1.0, ks_g).astype(bf16)
        pad = (n_wt - 1) * TILE
        kwp[0:pad, :] = jnp.zeros((pad, LANES), bf16)
        kwp[pad:pad + seq, :] = jnp.where(own, _pair_lanes(kw_ref[...].astype(f32), g), 0.0).astype(bf16)
        for kt in range(n_wt - 1):
            vwt[kt] = jnp.zeros(vwt.shape[1:], bf16)
        extra = vst.shape[1] - dh
        ones_row = (lax.broadcasted_iota(jnp.int32, (extra, TILE), 0) == 0).astype(bf16)
        for kt in range(nq):
            rows = slice(kt * TILE, (kt + 1) * TILE)
            vst[kt, 0:dh, :] = _group_rows_t(vs_ref[rows, :].astype(f32), g).astype(bf16)
            vst[kt, dh:dh + extra, :] = ones_row
            vwt[kt + n_wt - 1, 0:dh, :] = _group_rows_t(vw_ref[rows, :].astype(f32), g).astype(bf16)
            vwt[kt + n_wt - 1, dh:dh + extra, :] = ones_row
        kcs[...] = _pair_lanes(kcvc_ref[0, 0], g).astype(bf16)
        for ct in range(nc // TILE):
            rows = slice(ct * TILE, (ct + 1) * TILE)
            vct[:, rows] = _group_rows_t(kcvc_ref[0, 1, rows, :], g).astype(bf16)
        select_tile(q_ref, 0, 0, 0)

    tps = NSA_TILES_PER_STEP

    def one_tile(n_groups, slot):
        qt = step_id * tps + slot
        rows = slice(slot * tq, (slot + 1) * tq)
        q_all = qp_s[slot]
        q_aug = qa_s[slot]
        kw_rows = kwp[pl.ds(pl.multiple_of(qt * TILE, TILE), n_wt * TILE), :]
        s_w = jnp.dot(kw_rows, q_all, preferred_element_type=f32)

        def sel_scores(gi):
            s = jnp.dot(ksa[gi * gk:(gi + 1) * gk, :], q_aug, preferred_element_type=f32)
            tiles = []
            for t in range(SEL_GROUP):
                dt = qt - (gi * SEL_GROUP + t)
                idx = jnp.where(dt >= 0, dt, nq)
                tiles.append(jnp.concatenate([bsel_ref[hh, idx] for hh in heads], axis=1))
            sbuf[gi % 2] = s + jnp.concatenate(tiles, axis=0)

        sel_scores(0)
        if slot + 1 < tps:
            select_tile(q_ref, (slot + 1) * tq, qt + 1, slot + 1)
        else:
            select_tile(qn_ref, 0, jnp.minimum(qt + 1, nq - 1), 0)

        w_tiles = []
        for t in range(n_wt):
            dt = n_wt - 1 - t
            idx = jnp.where(qt >= dt, dt, n_wt)
            w_tiles.append(jnp.concatenate([bwin_ref[hh, idx] for hh in heads], axis=1))
        s_w = s_w + jnp.concatenate(w_tiles, axis=0)
        p_w = jnp.exp2(s_w - jnp.max(s_w, axis=0, keepdims=True))
        v_w = jnp.concatenate([vwt[qt + t] for t in range(n_wt)], axis=1)
        acc_w = jnp.dot(v_w, p_w.astype(bf16), preferred_element_type=f32)
        o_win = acc_w[0:dh, :] * (1.0 / acc_w[dh:dh + 1, :])

        m = jnp.full((1, NSA_HPG * tq), NEG, f32)
        acc = jnp.zeros((vst.shape[1], NSA_HPG * tq), f32)
        for gi in range(n_groups):
            if gi + 1 < n_groups:
                sel_scores(gi + 1)
            s = sbuf[gi % 2]
            m_new = jnp.maximum(m, jnp.max(s, axis=0, keepdims=True))
            alpha = jnp.exp2(m - m_new)
            p = jnp.exp2(s - m_new)
            v_t = jnp.concatenate([vst[gi * SEL_GROUP + t] for t in range(SEL_GROUP)], axis=1)
            acc = alpha * acc + jnp.dot(v_t, p.astype(bf16), preferred_element_type=f32)
            m = m_new
        o_sel = acc[0:dh, :] * (1.0 / acc[dh:dh + 1, :])
        o_cmp = ocmp_s[slot]

        gates_t = jax.nn.sigmoid(small_ref[rows, :]).T
        mixed = []
        for hh in heads:
            def gate(br):
                r = br * NSA_HEADS + hh
                return jnp.where(g == 0, gates_t[r:r + 1, :], gates_t[r + NSA_HPG:r + NSA_HPG + 1, :])
            cols = slice(hh * tq, (hh + 1) * tq)
            mixed.append(gate(0) * o_cmp[:, cols] + gate(1) * o_sel[:, cols] + gate(2) * o_win[:, cols])
        o = jnp.concatenate([jnp.concatenate(mixed[2 * j:2 * j + 2], axis=0).T for j in range(NSA_HPG // 2)],
                            axis=1)
        z = za_ref[rows, :]
        o_ref[rows, :] = (o * (z * jax.nn.sigmoid(z))).astype(o_ref.dtype)

    def step(n_groups):
        for slot in range(tps):
            one_tile(n_groups, slot)

    for n_groups in range(1, nq // SEL_GROUP + 1):
        pl.when(step_id // (SEL_GROUP // tps) == n_groups - 1)(functools.partial(step, n_groups))


def _nsa(pb, pf, kcvc, bias_sel, bias_win, bias_cmp, bsz, seq):
    nq = seq // TILE
    nc = seq // CMP_STRIDE
    n_cmp = nc - CMP_BLOCK // CMP_STRIDE + 1
    n_sel = seq // SEL_BLOCK
    n_top = min(SEL_TOPK, n_sel)
    nwin = bias_win.shape[1]
    tps = NSA_TILES_PER_STEP
    assert nq % SEL_GROUP == 0 and nc % TILE == 0 and bias_sel.shape[1] == nq + 1
    assert tps >= 2 and SEL_GROUP % tps == 0
    ns = nq // tps
    rows = tps * TILE
    gw = NSA_HPG * NSA_HEAD_DIM
    kern = functools.partial(_nsa_step_kernel, n_cmp=n_cmp, n_sel=n_sel, n_top=n_top)
    return pl.pallas_call(
        kern,
        grid=(NSA_GROUPS, bsz, ns),
        in_specs=[
            pl.BlockSpec((rows, gw), lambda g, b, t: (b * ns + t, PB_QP // gw + g)),
            pl.BlockSpec((TILE, gw), lambda g, b, t: (b * nq + jnp.minimum((t + 1) * tps, nq - 1),
                                                      PB_QP // gw + g)),
            pl.BlockSpec((seq, LANES), lambda g, b, t: (b, PB_KS // LANES)),
            pl.BlockSpec((seq, LANES), lambda g, b, t: (b, PB_KW // LANES)),
            pl.BlockSpec((seq, LANES), lambda g, b, t: (b, PB_VS // LANES)),
            pl.BlockSpec((seq, LANES), lambda g, b, t: (b, PB_VW // LANES)),
            pl.BlockSpec((1, 2, nc, LANES), lambda g, b, t: (b, 0, 0, 0)),
            pl.BlockSpec((NSA_HPG, nq, nc, TILE), lambda g, b, t: (g, 0, 0, 0)),
            pl.BlockSpec((NSA_HPG, nq + 1, TILE, TILE), lambda g, b, t: (g, 0, 0, 0)),
            pl.BlockSpec((NSA_HPG, nwin, TILE, TILE), lambda g, b, t: (g, 0, 0, 0)),
            pl.BlockSpec((rows, LANES), lambda g, b, t: (b * ns + t, PF_SMALL // LANES)),
            pl.BlockSpec((rows, gw), lambda g, b, t: (b * ns + t, PF_ZA // gw + g)),
        ],
        out_specs=pl.BlockSpec((rows, gw), lambda g, b, t: (b * ns + t, g)),
        out_shape=jax.ShapeDtypeStruct((bsz * seq, NSA_WIDTH), jnp.bfloat16),
        scratch_shapes=[
            pltpu.VMEM((seq, LANES), jnp.bfloat16),
            pltpu.VMEM((seq + (nwin - 2) * TILE, LANES), jnp.bfloat16),
            pltpu.VMEM((nq, NSA_V_ROWS, TILE), jnp.bfloat16),
            pltpu.VMEM((nq + nwin - 2, NSA_V_ROWS, TILE), jnp.bfloat16),
            pltpu.VMEM((nc, LANES), jnp.bfloat16),
            pltpu.VMEM((NSA_HEAD_DIM, nc), jnp.bfloat16),
            pltpu.VMEM((tps, LANES, NSA_HPG * TILE), jnp.bfloat16),
            pltpu.VMEM((tps, LANES, NSA_HPG * TILE), jnp.bfloat16),
            pltpu.VMEM((tps, NSA_HEAD_DIM, NSA_HPG * TILE), jnp.float32),
            pltpu.VMEM((2, SEL_GROUP * TILE, NSA_HPG * TILE), jnp.float32),
        ],
        compiler_params=_cparams(3), name="nsa",
    )(pb, pb, pb, pb, pb, pb, kcvc, bias_cmp, bias_sel, bias_win, pf, pf)


def _split_bf16(a, n):
    parts = []
    for _ in range(n - 1):
        hi = a.astype(jnp.bfloat16)
        parts.append(hi)
        a = a - hi.astype(jnp.float32)
    parts.append(a.astype(jnp.bfloat16))
    return parts


def _dot3(a, b):
    ah, al = _split_bf16(a, 2)
    bh, bl = _split_bf16(b, 2)
    f32 = jnp.float32
    return (jnp.dot(ah, bh, preferred_element_type=f32) + jnp.dot(al, bh, preferred_element_type=f32)
            + jnp.dot(ah, bl, preferred_element_type=f32))


def _softplus(x):
    return jnp.maximum(x, 0.0) + jnp.log(1.0 + jnp.exp(-jnp.abs(x)))


def _dn_kernel(scal_ref, q_ref, k_ref, v_ref, small_ref, z_ref, cw_ref, nw_ref, o_ref,
               qn, kn, vn, bet, gl, mm, nn, qq, oo, dd, mm2, dd2, *, seq):
    h = pl.program_id(1)
    n_rows = q_ref.shape[0]
    nb = n_rows // seq
    c = DN_CHUNK
    n_chunks = seq // c
    d = DN_HEAD_DIM

    head = 8

    def conv_body(x_ref, which):
        n = n_rows - head
        y = x_ref[pl.ds(head, n), :] * cw_ref[which, DN_CONV - 1:DN_CONV, :]
        for j in range(DN_CONV - 1):
            y = y + x_ref[pl.ds(head - (DN_CONV - 1 - j), n), :] * cw_ref[which, j:j + 1, :]
        return y

    def conv_head(x_ref, which, r0):
        x = x_ref[pl.ds(r0, head), :]
        rowi = lax.broadcasted_iota(jnp.int32, (head, d), 0)
        y = x * cw_ref[which, DN_CONV - 1:DN_CONV, :]
        for j in range(DN_CONV - 1):
            sh = DN_CONV - 1 - j
            y = y + jnp.where(rowi >= sh, pltpu.roll(x, sh, 0), 0.0) * cw_ref[which, j:j + 1, :]
        return y

    def silu(y):
        return y * jax.nn.sigmoid(y)

    def l2n(t):
        return t * lax.rsqrt(jnp.sum(t * t, axis=-1, keepdims=True) + 1e-6)

    finish = (lambda y: l2n(silu(y)) * (d ** -0.5), lambda y: l2n(silu(y)), silu)
    for which, (x_ref, dst) in enumerate(((q_ref, qn), (k_ref, kn), (v_ref, vn))):
        dst[pl.ds(head, n_rows - head), :] = finish[which](conv_body(x_ref, which))
        for b in range(nb):
            dst[pl.ds(b * seq, head), :] = finish[which](conv_head(x_ref, which, b * seq))
    small = small_ref[...]
    lane = lax.broadcasted_iota(jnp.int32, small.shape, 1)
    beta_in = jnp.sum(jnp.where(lane == SMALL_BETA + h, small, 0.0), axis=-1, keepdims=True)
    a_in = jnp.sum(jnp.where(lane == SMALL_A + h, small, 0.0), axis=-1, keepdims=True)
    bet[...] = jnp.broadcast_to(jax.nn.sigmoid(beta_in), (n_rows, d))
    gl[...] = jnp.broadcast_to(-jnp.exp(scal_ref[0, h]) * _softplus(a_in + scal_ref[1, h]), (n_rows, d))

    f32 = jnp.float32
    bf16 = jnp.bfloat16
    nt_dims = (((1,), (1,)), ((), ()))
    ri = lax.broadcasted_iota(jnp.int32, (c, 2 * c), 0)
    lane2 = lax.broadcasted_iota(jnp.int32, (c, 2 * c), 1)
    first = lane2 < c
    cj = lane2 & (c - 1)
    incl = ri >= cj
    strict = ri > cj
    tril_b = (lax.broadcasted_iota(jnp.int32, (c, c), 0) >= lax.broadcasted_iota(jnp.int32, (c, c), 1)).astype(bf16)
    bs = DN_INV_BLOCK
    sh = bs.bit_length() - 1
    same_diag = (ri >> sh) == (cj >> sh)
    level_masks = []
    while (1 << sh) < c:
        level_masks.append(((ri >> (sh + 1)) == (cj >> (sh + 1))) & ((ri >> sh) > (cj >> sh)))
        sh += 1
    bd_r = lax.broadcasted_iota(jnp.int32, (2 * c, 2 * c), 0)
    bd_c = lax.broadcasted_iota(jnp.int32, (2 * c, 2 * c), 1)
    block_diag = (bd_r >= c) == (bd_c >= c)

    def dot_pair(a, b):
        bb = b.astype(bf16)
        bb = jnp.where(block_diag, jnp.concatenate([bb, bb], axis=0), jnp.zeros((), bf16))
        return jnp.dot(a.astype(bf16), bb, preferred_element_type=f32)

    def stack_diag(x0, x1):
        z = jnp.zeros_like(x0)
        return jnp.concatenate([jnp.concatenate([x0, z], axis=1), jnp.concatenate([z, x1], axis=1)], axis=0)

    def chunk_prep(it, carry):
        ids = [it * DN_PREP_UNROLL + cc for cc in range(DN_PREP_UNROLL)]
        rows = [pl.ds(pl.multiple_of(i * c, c), c) for i in ids]
        pairs = range(0, DN_PREP_UNROLL, 2)
        ks = [kn[r, :] for r in rows]
        betas = [bet[r, :] for r in rows]
        gcbs = [sum(jnp.dot(tril_b, part, preferred_element_type=f32)
                    for part in _split_bf16(gl[r, :], DN_DECAY_PARTS)) for r in rows]
        kbs = [k * beta for k, beta in zip(ks, betas)]
        k2ts = [jnp.concatenate([ks[j], ks[j + 1]], axis=0).T.astype(bf16) for j in pairs]

        def against_pair_keys(xs):
            return [jnp.where(first,
                              jnp.dot(xs[j].astype(bf16), k2t, preferred_element_type=f32),
                              jnp.dot(xs[j + 1].astype(bf16), k2t, preferred_element_type=f32))
                    for j, k2t in zip(pairs, k2ts)]

        a_kks = against_pair_keys(kbs)
        decays = []
        for j in pairs:
            gc_col = jnp.where(first, gcbs[j], gcbs[j + 1])
            gc_row = jnp.concatenate([gcbs[j], gcbs[j + 1]], axis=0).T[0:c, :]
            decays.append(jnp.where(incl, jnp.exp(jnp.where(incl, gc_col - gc_row, 0.0)), 0.0))
        lows = [jnp.where(strict, a * dec, 0.0) for a, dec in zip(a_kks, decays)]
        pws = [jnp.where(same_diag, -low, 0.0) for low in lows]
        es = list(pws)
        for _ in range(max(1, (bs - 1).bit_length()) - 1):
            pws = [dot_pair(pw, pw) for pw in pws]
            es = [e + pw + dot_pair(e, pw) for e, pw in zip(es, pws)]
        for below in level_masks:
            offs = [jnp.where(below, low, 0.0) for low in lows]
            xs = [off + dot_pair(e, off) for e, off in zip(es, offs)]
            es = [e - (x + dot_pair(x, e)) for e, x in zip(es, xs)]
        egcs = [jnp.exp(gcb) for gcb in gcbs]
        rhss = [jnp.concatenate([vn[r, :] * beta, kb * egc], axis=1)
                for r, beta, kb, egc in zip(rows, betas, kbs, egcs)]
        uwbs = []
        for e, j in zip(es, pairs):
            e_hi, e_lo = _split_bf16(e, 2)
            r2 = stack_diag(rhss[j], rhss[j + 1]).astype(bf16)
            er = jnp.dot(e_hi, r2, preferred_element_type=f32) + jnp.dot(e_lo, r2, preferred_element_type=f32)
            uwbs.append((rhss[j] + er[:, 0:2 * d]).astype(bf16))
            uwbs.append((rhss[j + 1] + er[:, 2 * d:4 * d]).astype(bf16))
        qs = [qn[r, :] for r in rows]
        a_qks = [a * dec for a, dec in zip(against_pair_keys(qs), decays)]
        g_lasts = [gcb[c - 1:c, :] for gcb in gcbs]
        kdec_ts = []
        for k, gcb, g_last in zip(ks, gcbs, g_lasts):
            kdec = k * jnp.exp(g_last - gcb)
            kdec_ts.append(jnp.concatenate([kdec, jnp.zeros_like(kdec)], axis=0).T[:, 0:c].astype(bf16))
        nms = [jnp.dot(kt, uwb, preferred_element_type=f32) for kt, uwb in zip(kdec_ts, uwbs)]
        oq2s = [jnp.dot(a.astype(bf16), stack_diag(uwbs[j], uwbs[j + 1]), preferred_element_type=f32)
                for a, j in zip(a_qks, pairs)]
        oqs = []
        for oq2 in oq2s:
            oqs += [oq2[:, 0:2 * d], oq2[:, 2 * d:4 * d]]
        for i, r, nm, oq, q, egc, g_last in zip(ids, rows, nms, oqs, qs, egcs, g_lasts):
            m0 = pl.ds(pl.multiple_of(i * d, d), d)
            nn[m0, :] = nm[:, 0:d]
            mm[m0, :] = nm[:, d:2 * d].astype(bf16)
            oo[r, :] = oq[:, 0:d]
            qq[r, :] = (q * egc - oq[:, d:2 * d]).astype(bf16)
            dd[pl.ds(pl.multiple_of(i * 8, 8), 8), :] = jnp.broadcast_to(jnp.exp(g_last), (8, d))
        return carry

    lax.fori_loop(0, nb * n_chunks // DN_PREP_UNROLL, chunk_prep, 0)

    nn2, ss = kn, vn

    def chunk_pair(it, carry):
        ids = [it * DN_PAIR_UNROLL + cc for cc in range(DN_PAIR_UNROLL)]
        blk1 = [pl.ds(pl.multiple_of(2 * p * d, d), d) for p in ids]
        blk2 = [pl.ds(pl.multiple_of((2 * p + 1) * d, d), d) for p in ids]
        d1s = [dd[pl.ds(pl.multiple_of(2 * p * 8, 8), 1), :] for p in ids]
        d2s = [dd[pl.ds(pl.multiple_of((2 * p + 1) * 8, 8), 1), :] for p in ids]
        m1s = [mm[b, :] for b in blk1]
        m2s = [mm[b, :] for b in blk2]
        n1s = [nn[b, :] for b in blk1]
        xs = [jnp.dot(m2, jnp.concatenate([m1, n1.astype(bf16)], axis=1), preferred_element_type=f32)
              for m1, m2, n1 in zip(m1s, m2s, n1s)]
        for p, b2, d1, d2, m1, m2, n1, x in zip(ids, blk2, d1s, d2s, m1s, m2s, n1s, xs):
            pblk = pl.ds(pl.multiple_of(p * d, d), d)
            mm2[pblk, :] = (d2 * m1.astype(f32) + d1 * m2.astype(f32) - x[:, 0:d]).astype(bf16)
            nn2[pblk, :] = d2 * n1 - x[:, d:2 * d] + nn[b2, :]
            dd2[pl.ds(pl.multiple_of(p * 8, 8), 8), :] = jnp.broadcast_to(d1 * d2, (8, d))
        return carry

    n_pairs = n_chunks // 2
    lax.fori_loop(0, nb * n_pairs // DN_PAIR_UNROLL, chunk_pair, 0)

    def pair_scan(i, states):
        ids = [b * n_pairs + i for b in range(nb)]
        blocks = [pl.ds(pl.multiple_of(j * d, d), d) for j in ids]
        for blk, s in zip(blocks, states):
            ss[blk, :] = s
        prods = [jnp.dot(mm2[blk, :], s.astype(bf16), preferred_element_type=f32) for blk, s in zip(blocks, states)]
        return tuple(s * dd2[pl.ds(pl.multiple_of(j * 8, 8), 1), :] - pr + nn2[blk, :]
                     for s, j, blk, pr in zip(states, ids, blocks, prods))

    lax.fori_loop(0, n_pairs, pair_scan, tuple(jnp.zeros((d, d), f32) for _ in range(nb)))

    nw = nw_ref[...]

    def chunk_out(it, carry):
        pids = [it * (DN_OUT_UNROLL // 2) + cc for cc in range(DN_OUT_UNROLL // 2)]
        s0 = [ss[pl.ds(pl.multiple_of(p * d, d), d), :] for p in pids]
        s0b = [s.astype(bf16) for s in s0]
        s1b = [(s * dd[pl.ds(pl.multiple_of(2 * p * 8, 8), 1), :]
                - jnp.dot(mm[pl.ds(pl.multiple_of(2 * p * d, d), d), :], sb, preferred_element_type=f32)
                + nn[pl.ds(pl.multiple_of(2 * p * d, d), d), :]).astype(bf16)
               for p, s, sb in zip(pids, s0, s0b)]
        rows, states = [], []
        for p, sb0, sb1 in zip(pids, s0b, s1b):
            rows += [pl.ds(pl.multiple_of(2 * p * c, c), c), pl.ds(pl.multiple_of((2 * p + 1) * c, c), c)]
            states += [sb0, sb1]
        outs = [jnp.dot(qq[r, :], sb, preferred_element_type=f32) + oo[r, :] for r, sb in zip(rows, states)]
        for r, o in zip(rows, outs):
            o = o * lax.rsqrt(jnp.mean(o * o, axis=-1, keepdims=True) + 1e-6) * nw
            z = z_ref[r, :]
            o_ref[r, :] = (o * (z * jax.nn.sigmoid(z))).astype(o_ref.dtype)
        return carry

    lax.fori_loop(0, nb * n_chunks // DN_OUT_UNROLL, chunk_out, 0)


def _deltanet(pf, conv_w, a_log, dt_bias, norm_w, bsz, seq):
    d = DN_HEAD_DIM
    nb = DN_BATCHES if bsz % DN_BATCHES == 0 else 1
    rows = nb * seq
    n_chunks = rows // DN_CHUNK
    qkv0 = PF_QKVB // d
    scal = jnp.stack([a_log, dt_bias]).astype(jnp.float32)
    f32 = jnp.float32
    bf16 = jnp.bfloat16
    cw4 = conv_w.astype(f32).reshape(DN_CONV, 3, DN_HEADS, d).transpose(2, 1, 0, 3)
    assert 2 * DN_CHUNK == d and n_chunks % DN_PREP_UNROLL == 0 and n_chunks % DN_OUT_UNROLL == 0
    assert (seq // DN_CHUNK) % 2 == 0 and (n_chunks // 2) % DN_PAIR_UNROLL == 0 and DN_OUT_UNROLL % 2 == 0
    return pl.pallas_call(
        functools.partial(_dn_kernel, seq=seq),
        grid=(bsz // nb, DN_HEADS),
        in_specs=[
            pl.BlockSpec(memory_space=pltpu.SMEM),
            pl.BlockSpec((rows, d), lambda b, h: (b, qkv0 + h)),
            pl.BlockSpec((rows, d), lambda b, h: (b, qkv0 + DN_HEADS + h)),
            pl.BlockSpec((rows, d), lambda b, h: (b, qkv0 + 2 * DN_HEADS + h)),
            pl.BlockSpec((rows, LANES), lambda b, h: (b, PF_SMALL // LANES)),
            pl.BlockSpec((rows, d), lambda b, h: (b, PF_ZB // d + h)),
            pl.BlockSpec((None, 3, DN_CONV, d), lambda b, h: (h, 0, 0, 0)),
            pl.BlockSpec((1, d), lambda b, h: (0, 0)),
        ],
        out_specs=pl.BlockSpec((rows, d), lambda b, h: (b, h)),
        out_shape=jax.ShapeDtypeStruct((bsz * seq, DN_WIDTH), jnp.bfloat16),
        scratch_shapes=[
            pltpu.VMEM((rows, d), f32), pltpu.VMEM((rows, d), f32), pltpu.VMEM((rows, d), f32),
            pltpu.VMEM((rows, d), f32), pltpu.VMEM((rows, d), f32),
            pltpu.VMEM((n_chunks * d, d), bf16), pltpu.VMEM((n_chunks * d, d), f32),
            pltpu.VMEM((rows, d), bf16), pltpu.VMEM((rows, d), f32),
            pltpu.VMEM((n_chunks * 8, d), f32),
            pltpu.VMEM((n_chunks // 2 * d, d), bf16),
            pltpu.VMEM((n_chunks // 2 * 8, d), f32),
        ],
        compiler_params=_cparams(2), name="deltanet",
    )(scal, pf, pf, pf, pf, pf, cw4, norm_w.astype(f32).reshape(1, d))


def _out_kernel(oa_ref, ob_ref, gma_ref, gmb_ref, x_ref, p_ref, wa_ref, wb_ref, wo_ref, wpg_ref, wp_ref,
                lng_ref, lnb_ref, o_ref):
    f32 = jnp.float32
    bf16 = jnp.bfloat16
    tm = o_ref.shape[0]
    sub = tm // OUT_SUBTILES
    parts = [slice(i * sub, (i + 1) * sub) for i in range(OUT_SUBTILES)]
    y_a = [jnp.dot(oa_ref[r, :], wa_ref[...], preferred_element_type=f32) for r in parts]
    y_b = [jnp.dot(ob_ref[r, :], wb_ref[...], preferred_element_type=f32) for r in parts]
    pw = [jnp.dot(p_ref[r, :].astype(bf16), wp_ref[...], preferred_element_type=f32) for r in parts]
    mix = [(jax.nn.sigmoid(gma_ref[r, :]) * ya + jax.nn.sigmoid(gmb_ref[r, :]) * yb).astype(bf16)
           for r, ya, yb in zip(parts, y_a, y_b)]
    h = [DEEPNORM_ALPHA * x_ref[r, :] + jnp.dot(mx, wo_ref[...], preferred_element_type=f32)
         for r, mx in zip(parts, mix)]
    gate = [jnp.dot(hh.astype(bf16), wpg_ref[...], preferred_element_type=f32) for hh in h]
    for r, hh, gt, pp in zip(parts, h, gate, pw):
        hh = hh + jax.nn.sigmoid(gt) * pp
        mu = jnp.mean(hh, axis=-1, keepdims=True)
        hc = hh - mu
        var = jnp.mean(hc * hc, axis=-1, keepdims=True)
        o_ref[r, :] = (hc * lax.rsqrt(var + 1e-5) * lng_ref[...] + lnb_ref[...]).astype(o_ref.dtype)


def _out_block(o_a, o_b, pf, x2, p2, wa, wb, wo, wpg, wp, ln_g, ln_b, tm):
    t = x2.shape[0]
    bf = jnp.bfloat16

    def full(shape):
        return pl.BlockSpec(shape, lambda i: (0, 0))

    return pl.pallas_call(
        _out_kernel,
        grid=(t // tm,),
        in_specs=[
            pl.BlockSpec((tm, NSA_WIDTH), lambda i: (i, 0)),
            pl.BlockSpec((tm, DN_WIDTH), lambda i: (i, 0)),
            pl.BlockSpec((tm, D_MODEL), lambda i: (i, PF_GM // D_MODEL)),
            pl.BlockSpec((tm, D_MODEL), lambda i: (i, PF_GM // D_MODEL + 1)),
            pl.BlockSpec((tm, D_MODEL), lambda i: (i, 0)),
            pl.BlockSpec((tm, PLE_DIM), lambda i: (i, 0)),
            full((NSA_WIDTH, D_MODEL)), full((DN_WIDTH, D_MODEL)), full((D_MODEL, D_MODEL)),
            full((D_MODEL, D_MODEL)), full((PLE_DIM, D_MODEL)), full((1, D_MODEL)), full((1, D_MODEL)),
        ],
        out_specs=pl.BlockSpec((tm, D_MODEL), lambda i: (i, 0)),
        out_shape=jax.ShapeDtypeStruct((t, D_MODEL), x2.dtype),
        compiler_params=_cparams(1), name="out_block",
    )(o_a, o_b, pf, pf, x2, p2, wa.astype(bf), wb.astype(bf), wo.astype(bf), wpg.astype(bf), wp.astype(bf),
      ln_g.astype(jnp.float32).reshape(1, D_MODEL), ln_b.astype(jnp.float32).reshape(1, D_MODEL))


def _layer(x, p, w_in, pos_k, pos_v, w1_k, w2_k, w1_v, w2_v, bias_tabs, conv_w, a_log, dt_bias, norm_w,
           w_a, w_b, w_o, w_ple, w_pg, ln_g, ln_b):
    bsz, seq, _ = x.shape
    t = bsz * seq
    x2 = x.reshape(t, D_MODEL)
    wb16, wf16 = _prep_w_in(w_in)
    pb, pf = _proj(x2, wb16, wf16, PROJ_TM if t % PROJ_TM == 0 else seq)

    pos2, w2p = _prep_compress_weights(pos_k, pos_v, w2_k, w2_v)
    kcvc = _compress(pf, pos2, w1_k, w1_v, w2p, bsz, seq)

    bias_sel, bias_win, bias_cmp = bias_tabs
    o_a = _nsa(pb, pf, kcvc, bias_sel, bias_win, bias_cmp, bsz, seq)
    o_b = _deltanet(pf, conv_w, a_log, dt_bias, norm_w, bsz, seq)
    out = _out_block(o_a, o_b, pf, x2, p.reshape(t, PLE_DIM), w_a, w_b, w_o, w_pg, w_ple, ln_g, ln_b,
                     OUT_TM if t % OUT_TM == 0 else seq)
    return out.reshape(bsz, seq, D_MODEL)


def kernel(x, p, w_in, cmp_pos_k, cmp_pos_v, cmp_w1_k, cmp_w2_k, cmp_w1_v, cmp_w2_v, rel_bias, dn_conv_w,
           dn_a_log, dn_dt_bias, dn_norm_w, w_branch_a, w_branch_b, w_out, w_ple, w_ple_gate, ln_g, ln_b):
    depth = w_in.shape[0]
    bias_tabs = _bias_tables(rel_bias, x.shape[1])
    for i in range(depth):
        x = _layer(x, p[i], w_in[i], cmp_pos_k[i], cmp_pos_v[i], cmp_w1_k[i], cmp_w2_k[i], cmp_w1_v[i],
                   cmp_w2_v[i], bias_tabs, dn_conv_w[i], dn_a_log[i], dn_dt_bias[i], dn_norm_w[i],
                   w_branch_a[i], w_branch_b[i], w_out[i], w_ple[i], w_ple_gate[i], ln_g[i], ln_b[i])
    return x
```

```python
import functools
import math

import numpy as np
import jax
import jax.numpy as jnp
from jax import lax
from jax.experimental import pallas as pl
from jax.experimental.pallas import tpu as pltpu

D_MODEL = 1024
PLE_DIM = 256
NSA_HEADS = 8
NSA_GROUPS = 2
NSA_HPG = NSA_HEADS // NSA_GROUPS
NSA_HEAD_DIM = 64
NSA_WIDTH = NSA_HEADS * NSA_HEAD_DIM
NSA_KV = NSA_GROUPS * NSA_HEAD_DIM
CMP_BLOCK = 32
CMP_STRIDE = 16
CMP_HIDDEN = 256
SEL_BLOCK = 64
SEL_TOPK = 8
WINDOW = 512
DN_HEADS = 4
DN_HEAD_DIM = 128
DN_WIDTH = DN_HEADS * DN_HEAD_DIM
DN_CONV = 4
DN_CHUNK = 64
NUM_BUCKETS = 32
REL_MAX_DIST = 1024
DEEPNORM_ALPHA = 2.0 ** 0.25
NEG = -1e30
FORCE = 1e6
LOG2E = 1.4426950408889634

LANES = 128
TILE = 128
SEL_LANE0 = 64
SEL_GROUP = 4
NSA_TILES_PER_STEP = 4
NSA_V_ROWS = NSA_HEAD_DIM + 16
VMEM_LIMIT = 56 * 1024 * 1024
PROJ_TM = 512
PROJ_TN = 1024
OUT_TM = 512
OUT_SUBTILES = 2
DN_INV_BLOCK = 16
DN_BATCHES = 2
DN_OUT_UNROLL = 16
DN_PAIR_UNROLL = 8
DN_DECAY_PARTS = 2
DN_PREP_UNROLL = 32

HIGHEST = lax.Precision.HIGHEST

PB_QP = 0
PB_KS = PB_QP + NSA_WIDTH
PB_KW = PB_KS + NSA_KV
PB_VS = PB_KW + NSA_KV
PB_VW = PB_VS + NSA_KV
PB_WIDTH = PB_VW + NSA_KV
PF_GM = 0
PF_QKVB = PF_GM + 2 * D_MODEL
PF_ZA = PF_QKVB + 3 * DN_WIDTH
PF_ZB = PF_ZA + NSA_WIDTH
PF_KC = PF_ZB + DN_WIDTH
PF_VC = PF_KC + NSA_KV
PF_SMALL = PF_VC + NSA_KV
PF_WIDTH = PF_SMALL + LANES
SMALL_BETA = 3 * NSA_HEADS
SMALL_A = SMALL_BETA + DN_HEADS


def _bucket_thresholds():
    max_exact = NUM_BUCKETS // 2
    span = NUM_BUCKETS - max_exact
    ratio = REL_MAX_DIST // max_exact
    thr = list(range(1, max_exact + 1))
    for k in range(1, span):
        n = max_exact
        while n ** span < max_exact ** span * ratio ** k:
            n += 1
        thr.append(n)
    return tuple(thr)


_THR = _bucket_thresholds()


def _cparams(n_axes):
    return pltpu.CompilerParams(dimension_semantics=("arbitrary",) * n_axes, vmem_limit_bytes=VMEM_LIMIT)


def _proj_kernel(x_ref, wb_ref, wf_ref, pb_ref, pf_ref):
    nt_dims = (((1,), (1,)), ((), ()))
    xb = x_ref[...].astype(jnp.bfloat16)
    pb_ref[...] = lax.dot_general(xb, wb_ref[...], nt_dims, preferred_element_type=jnp.float32).astype(pb_ref.dtype)
    n = pf_ref.shape[1]
    for c0 in range(0, n, PROJ_TN):
        c1 = min(c0 + PROJ_TN, n)
        pf_ref[:, c0:c1] = lax.dot_general(xb, wf_ref[c0:c1, :], nt_dims, preferred_element_type=jnp.float32)


def _proj(x2, wb, wf, tm):
    t, d = x2.shape
    resident = dict(pipeline_mode=pl.Buffered(1))
    return pl.pallas_call(
        _proj_kernel,
        grid=(t // tm,),
        in_specs=[pl.BlockSpec((tm, d), lambda i: (i, 0)),
                  pl.BlockSpec((PB_WIDTH, d), lambda i: (0, 0), **resident),
                  pl.BlockSpec((PF_WIDTH, d), lambda i: (0, 0), **resident)],
        out_specs=[pl.BlockSpec((tm, PB_WIDTH), lambda i: (i, 0)),
                   pl.BlockSpec((tm, PF_WIDTH), lambda i: (i, 0))],
        out_shape=[jax.ShapeDtypeStruct((t, PB_WIDTH), jnp.bfloat16),
                   jax.ShapeDtypeStruct((t, PF_WIDTH), jnp.float32)],
        compiler_params=_cparams(1), name="proj",
    )(x2, wb, wf)


def _prep_w_in(w):
    d = w.shape[0]
    wt = jnp.swapaxes(w, 0, 1)
    o = 0
    wq = wt[o:o + NSA_WIDTH]; o += NSA_WIDTH
    wkv = wt[o:o + 6 * NSA_KV]; o += 6 * NSA_KV
    wg = wt[o:o + 3 * NSA_HEADS]; o += 3 * NSA_HEADS
    wza = wt[o:o + NSA_WIDTH]; o += NSA_WIDTH
    wqkvb = wt[o:o + 3 * DN_WIDTH]; o += 3 * DN_WIDTH
    wbeta_a = wt[o:o + 2 * DN_HEADS]; o += 2 * DN_HEADS
    wzb = wt[o:o + DN_WIDTH]; o += DN_WIDTH
    wgm = wt[o:o + 2 * D_MODEL]
    wkcvc, wks, wvs, wkw, wvw = (wkv[0:2 * NSA_KV], wkv[2 * NSA_KV:3 * NSA_KV], wkv[3 * NSA_KV:4 * NSA_KV],
                                 wkv[4 * NSA_KV:5 * NSA_KV], wkv[5 * NSA_KV:6 * NSA_KV])
    wb = jnp.concatenate([wq * (NSA_HEAD_DIM ** -0.5 * LOG2E), wks, wkw, wvs, wvw], axis=0).astype(jnp.bfloat16)
    pad = jnp.zeros((LANES - 3 * NSA_HEADS - 2 * DN_HEADS, d), w.dtype)
    wf = jnp.concatenate([wgm, wqkvb, wza, wzb, wkcvc, wg, wbeta_a, pad], axis=0).astype(jnp.bfloat16)
    return wb, wf


def _bias_kernel(tab_ref, sel_ref, win_ref, cmp_ref, *, n_cmp):
    h = pl.program_id(0)

    def bucket_of(n):
        return sum(1 for t in _THR if n >= t)

    def lookup(n, lo, hi):
        b_lo, b_hi = bucket_of(max(lo, 0)), bucket_of(max(hi, 0))
        val = jnp.full(n.shape, tab_ref[b_lo, h], jnp.float32)
        for b in range(b_lo + 1, b_hi + 1):
            val = jnp.where(n >= _THR[b - 1], tab_ref[b, h], val)
        return val * LOG2E

    kj = lax.broadcasted_iota(jnp.int32, (TILE, TILE), 0)
    qi = lax.broadcasted_iota(jnp.int32, (TILE, TILE), 1)
    n_sel_tiles = sel_ref.shape[1] - 1
    n_win_tiles = win_ref.shape[1] - 1
    for dt in range(max(n_sel_tiles, n_win_tiles)):
        dist = dt * TILE + qi - kj
        v = lookup(jnp.maximum(dist, 0), dt * TILE - (TILE - 1), dt * TILE + (TILE - 1))
        if dt < n_sel_tiles:
            sel_ref[0, dt] = jnp.where(dist >= 0, v, NEG)
        if dt < n_win_tiles:
            win_ref[0, dt] = jnp.where((dist >= 0) & (dist < WINDOW), v, NEG)
    sel_ref[0, n_sel_tiles] = jnp.full((TILE, TILE), NEG, jnp.float32)
    win_ref[0, n_win_tiles] = jnp.full((TILE, TILE), NEG, jnp.float32)
    nc = cmp_ref.shape[2]
    band = 16
    for t in range(cmp_ref.shape[1]):
        for c0 in range(0, nc, band):
            c = c0 + lax.broadcasted_iota(jnp.int32, (band, TILE), 0)
            s = t * TILE + lax.broadcasted_iota(jnp.int32, (band, TILE), 1)
            dist = s - (c * CMP_STRIDE + CMP_BLOCK - 1)
            lo = t * TILE - ((c0 + band - 1) * CMP_STRIDE + CMP_BLOCK - 1)
            hi = t * TILE + TILE - 1 - (c0 * CMP_STRIDE + CMP_BLOCK - 1)
            cmp_ref[0, t, c0:c0 + band, :] = jnp.where((dist >= 0) & (c < n_cmp),
                                                       lookup(jnp.maximum(dist, 0), lo, hi), NEG)


def _bias_tables(rel_bias, seq):
    nq = seq // TILE
    nwin = WINDOW // TILE + 1
    nc = seq // CMP_STRIDE
    n_cmp = nc - CMP_BLOCK // CMP_STRIDE + 1
    return pl.pallas_call(
        functools.partial(_bias_kernel, n_cmp=n_cmp),
        grid=(NSA_HEADS,),
        in_specs=[pl.BlockSpec(memory_space=pltpu.SMEM)],
        out_specs=[pl.BlockSpec((1, nq + 1, TILE, TILE), lambda h: (h, 0, 0, 0)),
                   pl.BlockSpec((1, nwin + 1, TILE, TILE), lambda h: (h, 0, 0, 0)),
                   pl.BlockSpec((1, nq, nc, TILE), lambda h: (h, 0, 0, 0))],
        out_shape=[jax.ShapeDtypeStruct((NSA_HEADS, nq + 1, TILE, TILE), jnp.float32),
                   jax.ShapeDtypeStruct((NSA_HEADS, nwin + 1, TILE, TILE), jnp.float32),
                   jax.ShapeDtypeStruct((NSA_HEADS, nq, nc, TILE), jnp.float32)],
        compiler_params=_cparams(1), name="bias_tables",
    )(rel_bias.astype(jnp.float32))


def _gelu_tanh(x):
    return x * (0.5 * (1.0 + jnp.tanh(math.sqrt(2.0 / math.pi) * (x + 0.044715 * (x * x * x)))))


def _compress_kernel(x_ref, pos_ref, w1k_ref, w1v_ref, w2_ref, o_ref, wbd):
    kv = pl.program_id(0)
    nc = o_ref.shape[2]
    half = CMP_BLOCK // 2
    dh = NSA_HEAD_DIM
    bf16 = jnp.bfloat16

    @pl.when(pl.program_id(1) == 0)
    def _():
        def build(w1_ref):
            zero = jnp.zeros((dh, CMP_HIDDEN), bf16)
            for a in range(2):
                for l in range(half):
                    r0 = (a * half + l) * dh
                    wl = w1_ref[r0:r0 + dh, :].astype(bf16)
                    wbd[a, l * NSA_KV:(l + 1) * NSA_KV, :] = jnp.concatenate(
                        [jnp.concatenate([wl, zero], axis=1), jnp.concatenate([zero, wl], axis=1)], axis=0)
        pl.when(kv == 0)(functools.partial(build, w1k_ref))
        pl.when(kv == 1)(functools.partial(build, w1v_ref))

    r = jnp.concatenate([x_ref[pl.ds(l, nc, stride=CMP_STRIDE), :] for l in range(CMP_STRIDE)], axis=1)
    a = jnp.dot((r + pos_ref[0, 0:1, :]).astype(bf16), wbd[0], preferred_element_type=jnp.float32)
    b = jnp.dot((r + pos_ref[0, 1:2, :]).astype(bf16), wbd[1], preferred_element_type=jnp.float32)
    hid = a + pltpu.roll(b, nc - 1, 0)
    o_ref[0, 0] = jnp.dot(_gelu_tanh(hid).astype(bf16), w2_ref[0], preferred_element_type=jnp.float32)


def _compress(pf, pos2, w1_k, w1_v, w2p, bsz, seq):
    nc = seq // CMP_STRIDE
    width = CMP_STRIDE * NSA_KV
    hid = NSA_GROUPS * CMP_HIDDEN
    assert CMP_BLOCK == 2 * CMP_STRIDE
    return pl.pallas_call(
        _compress_kernel,
        grid=(2, bsz),
        in_specs=[pl.BlockSpec((seq, NSA_KV), lambda k, b: (b, PF_KC // NSA_KV + k)),
                  pl.BlockSpec((1, 2, width), lambda k, b: (k, 0, 0)),
                  pl.BlockSpec(w1_k.shape, lambda k, b: (0, 0)),
                  pl.BlockSpec(w1_v.shape, lambda k, b: (0, 0)),
                  pl.BlockSpec((1, hid, LANES), lambda k, b: (k, 0, 0))],
        out_specs=pl.BlockSpec((1, 1, nc, LANES), lambda k, b: (b, k, 0, 0)),
        out_shape=jax.ShapeDtypeStruct((bsz, 2, nc, LANES), jnp.float32),
        scratch_shapes=[pltpu.VMEM((2, width, hid), jnp.bfloat16)],
        compiler_params=_cparams(2), name="compress",
    )(pf, pos2, w1_k, w1_v, w2p)


def _prep_compress_weights(pos_k, pos_v, w2_k, w2_v):
    eye = jnp.eye(NSA_GROUPS, dtype=jnp.float32)
    half = CMP_BLOCK // 2

    def w2_both(w2):
        return jnp.einsum('jd,gh->gjhd', w2, eye).reshape(NSA_GROUPS * CMP_HIDDEN, NSA_GROUPS * NSA_HEAD_DIM)

    def pos_both(pos):
        p = pos.reshape(2, half, 1, NSA_HEAD_DIM)
        return jnp.broadcast_to(p, (2, half, NSA_GROUPS, NSA_HEAD_DIM)).reshape(2, half * NSA_KV)

    pos2 = jnp.stack([pos_both(pos_k), pos_both(pos_v)]).astype(jnp.float32)
    w2p = jnp.stack([w2_both(w2_k), w2_both(w2_v)]).astype(jnp.bfloat16)
    return pos2, w2p


def _pair_lanes(x, g):
    sw = pltpu.roll(x, LANES // 2, 1)
    lane = lax.broadcasted_iota(jnp.int32, x.shape, 1)
    own = (lane < LANES // 2) == (g == 0)
    return jnp.where(own, x, sw)


def _group_rows_t(x, g):
    xt = x.T
    half = LANES // 2
    return jnp.where(g == 0, xt[0:half, :], xt[half:LANES, :])


def _nsa_step_kernel(q_ref, qn_ref, ks_ref, kw_ref, vs_ref, vw_ref, kcvc_ref, bc_ref, bsel_ref, bwin_ref,
                     small_ref, za_ref, o_ref,
                     ksa, kwp, vst, vwt, kcs, vct, qa_s, qp_s, ocmp_s, sbuf, *, n_cmp, n_sel, n_top):
    g = pl.program_id(0)
    step_id = pl.program_id(2)
    tq = TILE
    dh = NSA_HEAD_DIM
    seq = ks_ref.shape[0]
    nq = seq // TILE
    nc = kcvc_ref.shape[2]
    n_wt = bwin_ref.shape[1] - 1
    heads = range(NSA_HPG)
    f32 = jnp.float32
    bf16 = jnp.bfloat16
    nt_dims = (((1,), (1,)), ((), ()))
    gk = SEL_GROUP * TILE

    def select_tile(src_ref, r0, tile, slot):
        q32 = src_ref[r0:r0 + tq, :].astype(f32)
        low_half = lax.broadcasted_iota(jnp.int32, (tq, LANES), 1) < NSA_HEAD_DIM
        qh = []
        for hh in heads:
            blk = q32[:, (hh // 2) * LANES:(hh // 2 + 1) * LANES]
            if hh % 2:
                blk = pltpu.roll(blk, LANES // 2, 1)
            qh.append(jnp.where(low_half, blk, 0.0).astype(bf16))
        for hh in heads:
            qp_s[slot, hh * tq:(hh + 1) * tq, :] = qh[hh]
        q_all = qp_s[slot]
        s_c = lax.dot_general(kcs[...], q_all, nt_dims, preferred_element_type=f32)
        bias_c = jnp.concatenate([bc_ref[hh, tile] for hh in heads], axis=1)
        valid = bias_c > 0.5 * NEG
        s_c = s_c + bias_c
        e = jnp.where(valid, jnp.exp2(s_c - jnp.max(s_c, axis=0, keepdims=True)), 0.0)
        den = jnp.maximum(jnp.sum(e, axis=0, keepdims=True), 1e-30)
        p_c = e * (1.0 / den)
        psum = sum(p_c[:, hh * tq:(hh + 1) * tq] for hh in heads)
        sj = lax.broadcasted_iota(jnp.int32, (n_sel, nc), 0)
        ci = lax.broadcasted_iota(jnp.int32, (n_sel, nc), 1)
        overlap = ((ci * CMP_STRIDE < (sj + 1) * SEL_BLOCK) & (ci * CMP_STRIDE + CMP_BLOCK > sj * SEL_BLOCK)
                   & (ci < n_cmp)).astype(bf16)
        imp_t = sum(jnp.dot(overlap, part, preferred_element_type=f32)
                    for part in _split_bf16(psum, 3))
        ocmp_s[slot] = jnp.dot(vct[...], p_c.astype(bf16), preferred_element_type=f32)
        blk = lax.broadcasted_iota(jnp.int32, (n_sel, tq), 0)
        cur = (tile * tq + lax.broadcasted_iota(jnp.int32, (n_sel, tq), 1)) >> 6
        forced = (blk == 0) | (blk == cur) | (blk == cur - 1)
        imp_t = jnp.where(forced, FORCE, jnp.where(blk > cur, -FORCE, imp_t))
        rank = jnp.zeros((n_sel, tq), jnp.int32)
        for j in range(n_sel):
            other = imp_t[j:j + 1, :]
            ahead = (other > imp_t) | ((other == imp_t) & (blk > j))
            rank = rank + ahead.astype(jnp.int32)
        selb = jnp.where(rank < n_top, 0.0, NEG)
        pieces = [jnp.zeros((SEL_LANE0, tq), f32), selb]
        if LANES - SEL_LANE0 - n_sel > 0:
            pieces.append(jnp.zeros((LANES - SEL_LANE0 - n_sel, tq), f32))
        selb_r = jnp.concatenate(pieces, axis=0).T.astype(bf16)
        for hh in heads:
            qa_s[slot, hh * tq:(hh + 1) * tq, :] = qh[hh] + selb_r

    @pl.when(step_id == 0)
    def _():
        row = lax.broadcasted_iota(jnp.int32, (seq, LANES), 0)
        lane = lax.broadcasted_iota(jnp.int32, (seq, LANES), 1)
        onehot = (lane - SEL_LANE0) == (row >> 6)
        own = lane < NSA_HEAD_DIM
        ks_g = jnp.where(own, _pair_lanes(ks_ref[...].astype(f32), g), 0.0)
        ksa[...] = jnp.where(onehot, 1.0, ks_g).astype(bf16)
        pad = (n_wt - 1) * TILE
        kwp[0:pad, :] = jnp.zeros((pad, LANES), bf16)
        kwp[pad:pad + seq, :] = jnp.where(own, _pair_lanes(kw_ref[...].astype(f32), g), 0.0).astype(bf16)
        for kt in range(n_wt - 1):
            vwt[kt] = jnp.zeros(vwt.shape[1:], bf16)
        extra = vst.shape[1] - dh
        ones_row = (lax.broadcasted_iota(jnp.int32, (extra, TILE), 0) == 0).astype(bf16)
        for kt in range(nq):
            rows = slice(kt * TILE, (kt + 1) * TILE)
            vst[kt, 0:dh, :] = _group_rows_t(vs_ref[rows, :].astype(f32), g).astype(bf16)
            vst[kt, dh:dh + extra, :] = ones_row
            vwt[kt + n_wt - 1, 0:dh, :] = _group_rows_t(vw_ref[rows, :].astype(f32), g).astype(bf16)
            vwt[kt + n_wt - 1, dh:dh + extra, :] = ones_row
        kcs[...] = _pair_lanes(kcvc_ref[0, 0], g).astype(bf16)
        for ct in range(nc // TILE):
            rows = slice(ct * TILE, (ct + 1) * TILE)
            vct[:, rows] = _group_rows_t(kcvc_ref[0, 1, rows, :], g).astype(bf16)
        select_tile(q_ref, 0, 0, 0)

    tps = NSA_TILES_PER_STEP

    def one_tile(n_groups, slot):
        qt = step_id * tps + slot
        rows = slice(slot * tq, (slot + 1) * tq)
        q_all = qp_s[slot]
        q_aug = qa_s[slot]
        kw_rows = kwp[pl.ds(pl.multiple_of(qt * TILE, TILE), n_wt * TILE), :]
        s_w = lax.dot_general(kw_rows, q_all, nt_dims, preferred_element_type=f32)

        def sel_scores(gi):
            s = lax.dot_general(ksa[gi * gk:(gi + 1) * gk, :], q_aug, nt_dims, preferred_element_type=f32)
            tiles = []
            for t in range(SEL_GROUP):
                dt = qt - (gi * SEL_GROUP + t)
                idx = jnp.where(dt >= 0, dt, nq)
                tiles.append(jnp.concatenate([bsel_ref[hh, idx] for hh in heads], axis=1))
            sbuf[gi % 2] = s + jnp.concatenate(tiles, axis=0)

        sel_scores(0)
        if slot + 1 < tps:
            select_tile(q_ref, (slot + 1) * tq, qt + 1, slot + 1)
        else:
            select_tile(qn_ref, 0, jnp.minimum(qt + 1, nq - 1), 0)

        w_tiles = []
        for t in range(n_wt):
            dt = n_wt - 1 - t
            idx = jnp.where(qt >= dt, dt, n_wt)
            w_tiles.append(jnp.concatenate([bwin_ref[hh, idx] for hh in heads], axis=1))
        s_w = s_w + jnp.concatenate(w_tiles, axis=0)
        p_w = jnp.exp2(s_w - jnp.max(s_w, axis=0, keepdims=True))
        v_w = jnp.concatenate([vwt[qt + t] for t in range(n_wt)], axis=1)
        acc_w = jnp.dot(v_w, p_w.astype(bf16), preferred_element_type=f32)
        o_win = acc_w[0:dh, :] * (1.0 / acc_w[dh:dh + 1, :])

        m = jnp.full((1, NSA_HPG * tq), NEG, f32)
        acc = jnp.zeros((vst.shape[1], NSA_HPG * tq), f32)
        for gi in range(n_groups):
            if gi + 1 < n_groups:
                sel_scores(gi + 1)
            s = sbuf[gi % 2]
            m_new = jnp.maximum(m, jnp.max(s, axis=0, keepdims=True))
            alpha = jnp.exp2(m - m_new)
            p = jnp.exp2(s - m_new)
            v_t = jnp.concatenate([vst[gi * SEL_GROUP + t] for t in range(SEL_GROUP)], axis=1)
            acc = alpha * acc + jnp.dot(v_t, p.astype(bf16), preferred_element_type=f32)
            m = m_new
        o_sel = acc[0:dh, :] * (1.0 / acc[dh:dh + 1, :])
        o_cmp = ocmp_s[slot]

        gates_t = jax.nn.sigmoid(small_ref[rows, :]).T
        mixed = []
        for hh in heads:
            def gate(br):
                r = br * NSA_HEADS + hh
                return jnp.where(g == 0, gates_t[r:r + 1, :], gates_t[r + NSA_HPG:r + NSA_HPG + 1, :])
            cols = slice(hh * tq, (hh + 1) * tq)
            mixed.append(gate(0) * o_cmp[:, cols] + gate(1) * o_sel[:, cols] + gate(2) * o_win[:, cols])
        o = jnp.concatenate([jnp.concatenate(mixed[2 * j:2 * j + 2], axis=0).T for j in range(NSA_HPG // 2)],
                            axis=1)
        z = za_ref[rows, :]
        o_ref[rows, :] = (o * (z * jax.nn.sigmoid(z))).astype(o_ref.dtype)

    def step(n_groups):
        for slot in range(tps):
            one_tile(n_groups, slot)

    for n_groups in range(1, nq // SEL_GROUP + 1):
        pl.when(step_id // (SEL_GROUP // tps) == n_groups - 1)(functools.partial(step, n_groups))


def _nsa(pb, pf, kcvc, bias_sel, bias_win, bias_cmp, bsz, seq):
    nq = seq // TILE
    nc = seq // CMP_STRIDE
    n_cmp = nc - CMP_BLOCK // CMP_STRIDE + 1
    n_sel = seq // SEL_BLOCK
    n_top = min(SEL_TOPK, n_sel)
    nwin = bias_win.shape[1]
    tps = NSA_TILES_PER_STEP
    assert nq % SEL_GROUP == 0 and nc % TILE == 0 and bias_sel.shape[1] == nq + 1
    assert tps >= 2 and SEL_GROUP % tps == 0
    ns = nq // tps
    rows = tps * TILE
    gw = NSA_HPG * NSA_HEAD_DIM
    kern = functools.partial(_nsa_step_kernel, n_cmp=n_cmp, n_sel=n_sel, n_top=n_top)
    return pl.pallas_call(
        kern,
        grid=(NSA_GROUPS, bsz, ns),
        in_specs=[
            pl.BlockSpec((rows, gw), lambda g, b, t: (b * ns + t, PB_QP // gw + g)),
            pl.BlockSpec((TILE, gw), lambda g, b, t: (b * nq + jnp.minimum((t + 1) * tps, nq - 1),
                                                      PB_QP // gw + g)),
            pl.BlockSpec((seq, LANES), lambda g, b, t: (b, PB_KS // LANES)),
            pl.BlockSpec((seq, LANES), lambda g, b, t: (b, PB_KW // LANES)),
            pl.BlockSpec((seq, LANES), lambda g, b, t: (b, PB_VS // LANES)),
            pl.BlockSpec((seq, LANES), lambda g, b, t: (b, PB_VW // LANES)),
            pl.BlockSpec((1, 2, nc, LANES), lambda g, b, t: (b, 0, 0, 0)),
            pl.BlockSpec((NSA_HPG, nq, nc, TILE), lambda g, b, t: (g, 0, 0, 0)),
            pl.BlockSpec((NSA_HPG, nq + 1, TILE, TILE), lambda g, b, t: (g, 0, 0, 0)),
            pl.BlockSpec((NSA_HPG, nwin, TILE, TILE), lambda g, b, t: (g, 0, 0, 0)),
            pl.BlockSpec((rows, LANES), lambda g, b, t: (b * ns + t, PF_SMALL // LANES)),
            pl.BlockSpec((rows, gw), lambda g, b, t: (b * ns + t, PF_ZA // gw + g)),
        ],
        out_specs=pl.BlockSpec((rows, gw), lambda g, b, t: (b * ns + t, g)),
        out_shape=jax.ShapeDtypeStruct((bsz * seq, NSA_WIDTH), jnp.bfloat16),
        scratch_shapes=[
            pltpu.VMEM((seq, LANES), jnp.bfloat16),
            pltpu.VMEM((seq + (nwin - 2) * TILE, LANES), jnp.bfloat16),
            pltpu.VMEM((nq, NSA_V_ROWS, TILE), jnp.bfloat16),
            pltpu.VMEM((nq + nwin - 2, NSA_V_ROWS, TILE), jnp.bfloat16),
            pltpu.VMEM((nc, LANES), jnp.bfloat16),
            pltpu.VMEM((NSA_HEAD_DIM, nc), jnp.bfloat16),
            pltpu.VMEM((tps, NSA_HPG * TILE, LANES), jnp.bfloat16),
            pltpu.VMEM((tps, NSA_HPG * TILE, LANES), jnp.bfloat16),
            pltpu.VMEM((tps, NSA_HEAD_DIM, NSA_HPG * TILE), jnp.float32),
            pltpu.VMEM((2, SEL_GROUP * TILE, NSA_HPG * TILE), jnp.float32),
        ],
        compiler_params=_cparams(3), name="nsa",
    )(pb, pb, pb, pb, pb, pb, kcvc, bias_cmp, bias_sel, bias_win, pf, pf)


def _split_bf16(a, n):
    parts = []
    for _ in range(n - 1):
        hi = a.astype(jnp.bfloat16)
        parts.append(hi)
        a = a - hi.astype(jnp.float32)
    parts.append(a.astype(jnp.bfloat16))
    return parts


def _dot3(a, b):
    ah, al = _split_bf16(a, 2)
    bh, bl = _split_bf16(b, 2)
    f32 = jnp.float32
    return (jnp.dot(ah, bh, preferred_element_type=f32) + jnp.dot(al, bh, preferred_element_type=f32)
            + jnp.dot(ah, bl, preferred_element_type=f32))


def _softplus(x):
    return jnp.maximum(x, 0.0) + jnp.log(1.0 + jnp.exp(-jnp.abs(x)))


def _dn_kernel(scal_ref, q_ref, k_ref, v_ref, small_ref, z_ref, cw_ref, nw_ref, o_ref,
               qn, kn, vn, bet, gl, mm, nn, qq, oo, dd, mm2, dd2, *, seq):
    h = pl.program_id(1)
    n_rows = q_ref.shape[0]
    nb = n_rows // seq
    c = DN_CHUNK
    n_chunks = seq // c
    d = DN_HEAD_DIM

    head = 8

    def conv_body(x_ref, which):
        n = n_rows - head
        y = x_ref[pl.ds(head, n), :] * cw_ref[which, DN_CONV - 1:DN_CONV, :]
        for j in range(DN_CONV - 1):
            y = y + x_ref[pl.ds(head - (DN_CONV - 1 - j), n), :] * cw_ref[which, j:j + 1, :]
        return y

    def conv_head(x_ref, which, r0):
        x = x_ref[pl.ds(r0, head), :]
        rowi = lax.broadcasted_iota(jnp.int32, (head, d), 0)
        y = x * cw_ref[which, DN_CONV - 1:DN_CONV, :]
        for j in range(DN_CONV - 1):
            sh = DN_CONV - 1 - j
            y = y + jnp.where(rowi >= sh, pltpu.roll(x, sh, 0), 0.0) * cw_ref[which, j:j + 1, :]
        return y

    def silu(y):
        return y * jax.nn.sigmoid(y)

    def l2n(t):
        return t * lax.rsqrt(jnp.sum(t * t, axis=-1, keepdims=True) + 1e-6)

    finish = (lambda y: l2n(silu(y)) * (d ** -0.5), lambda y: l2n(silu(y)), silu)
    for which, (x_ref, dst) in enumerate(((q_ref, qn), (k_ref, kn), (v_ref, vn))):
        dst[pl.ds(head, n_rows - head), :] = finish[which](conv_body(x_ref, which))
        for b in range(nb):
            dst[pl.ds(b * seq, head), :] = finish[which](conv_head(x_ref, which, b * seq))
    small = small_ref[...]
    lane = lax.broadcasted_iota(jnp.int32, small.shape, 1)
    beta_in = jnp.sum(jnp.where(lane == SMALL_BETA + h, small, 0.0), axis=-1, keepdims=True)
    a_in = jnp.sum(jnp.where(lane == SMALL_A + h, small, 0.0), axis=-1, keepdims=True)
    bet[...] = jnp.broadcast_to(jax.nn.sigmoid(beta_in), (n_rows, d))
    gl[...] = jnp.broadcast_to(-jnp.exp(scal_ref[0, h]) * _softplus(a_in + scal_ref[1, h]), (n_rows, d))

    f32 = jnp.float32
    bf16 = jnp.bfloat16
    nt_dims = (((1,), (1,)), ((), ()))
    ri = lax.broadcasted_iota(jnp.int32, (c, 2 * c), 0)
    lane2 = lax.broadcasted_iota(jnp.int32, (c, 2 * c), 1)
    first = lane2 < c
    cj = lane2 & (c - 1)
    incl = ri >= cj
    strict = ri > cj
    tril_b = (lax.broadcasted_iota(jnp.int32, (c, c), 0) >= lax.broadcasted_iota(jnp.int32, (c, c), 1)).astype(bf16)
    bs = DN_INV_BLOCK
    sh = bs.bit_length() - 1
    same_diag = (ri >> sh) == (cj >> sh)
    level_masks = []
    while (1 << sh) < c:
        level_masks.append(((ri >> (sh + 1)) == (cj >> (sh + 1))) & ((ri >> sh) > (cj >> sh)))
        sh += 1
    bd_r = lax.broadcasted_iota(jnp.int32, (2 * c, 2 * c), 0)
    bd_c = lax.broadcasted_iota(jnp.int32, (2 * c, 2 * c), 1)
    block_diag = (bd_r >= c) == (bd_c >= c)

    def dot_pair(a, b):
        bb = b.astype(bf16)
        bb = jnp.where(block_diag, jnp.concatenate([bb, bb], axis=0), jnp.zeros((), bf16))
        return jnp.dot(a.astype(bf16), bb, preferred_element_type=f32)

    def stack_diag(x0, x1):
        z = jnp.zeros_like(x0)
        return jnp.concatenate([jnp.concatenate([x0, z], axis=1), jnp.concatenate([z, x1], axis=1)], axis=0)

    def chunk_prep(it, carry):
        ids = [it * DN_PREP_UNROLL + cc for cc in range(DN_PREP_UNROLL)]
        rows = [pl.ds(pl.multiple_of(i * c, c), c) for i in ids]
        pairs = range(0, DN_PREP_UNROLL, 2)
        ks = [kn[r, :] for r in rows]
        betas = [bet[r, :] for r in rows]
        gcbs = [sum(jnp.dot(tril_b, part, preferred_element_type=f32)
                    for part in _split_bf16(gl[r, :], DN_DECAY_PARTS)) for r in rows]
        kbs = [k * beta for k, beta in zip(ks, betas)]
        k2ts = [jnp.concatenate([ks[j], ks[j + 1]], axis=0).T.astype(bf16) for j in pairs]

        def against_pair_keys(xs):
            return [jnp.where(first,
                              jnp.dot(xs[j].astype(bf16), k2t, preferred_element_type=f32),
                              jnp.dot(xs[j + 1].astype(bf16), k2t, preferred_element_type=f32))
                    for j, k2t in zip(pairs, k2ts)]

        a_kks = against_pair_keys(kbs)
        decays = []
        for j in pairs:
            gc_col = jnp.where(first, gcbs[j], gcbs[j + 1])
            gc_row = jnp.concatenate([gcbs[j], gcbs[j + 1]], axis=0).T[0:c, :]
            decays.append(jnp.where(incl, jnp.exp(jnp.where(incl, gc_col - gc_row, 0.0)), 0.0))
        lows = [jnp.where(strict, a * dec, 0.0) for a, dec in zip(a_kks, decays)]
        pws = [jnp.where(same_diag, -low, 0.0) for low in lows]
        es = list(pws)
        for _ in range(max(1, (bs - 1).bit_length()) - 1):
            pws = [dot_pair(pw, pw) for pw in pws]
            es = [e + pw + dot_pair(e, pw) for e, pw in zip(es, pws)]
        for below in level_masks:
            offs = [jnp.where(below, low, 0.0) for low in lows]
            xs = [off + dot_pair(e, off) for e, off in zip(es, offs)]
            es = [e - (x + dot_pair(x, e)) for e, x in zip(es, xs)]
        egcs = [jnp.exp(gcb) for gcb in gcbs]
        rhss = [jnp.concatenate([vn[r, :] * beta, kb * egc], axis=1)
                for r, beta, kb, egc in zip(rows, betas, kbs, egcs)]
        uwbs = []
        for e, j in zip(es, pairs):
            e_hi, e_lo = _split_bf16(e, 2)
            r2 = stack_diag(rhss[j], rhss[j + 1]).astype(bf16)
            er = jnp.dot(e_hi, r2, preferred_element_type=f32) + jnp.dot(e_lo, r2, preferred_element_type=f32)
            uwbs.append((rhss[j] + er[:, 0:2 * d]).astype(bf16))
            uwbs.append((rhss[j + 1] + er[:, 2 * d:4 * d]).astype(bf16))
        qs = [qn[r, :] for r in rows]
        a_qks = [a * dec for a, dec in zip(against_pair_keys(qs), decays)]
        g_lasts = [gcb[c - 1:c, :] for gcb in gcbs]
        kdec_ts = []
        for k, gcb, g_last in zip(ks, gcbs, g_lasts):
            kdec = k * jnp.exp(g_last - gcb)
            kdec_ts.append(jnp.concatenate([kdec, jnp.zeros_like(kdec)], axis=0).T[:, 0:c].astype(bf16))
        nms = [jnp.dot(kt, uwb, preferred_element_type=f32) for kt, uwb in zip(kdec_ts, uwbs)]
        oq2s = [jnp.dot(a.astype(bf16), stack_diag(uwbs[j], uwbs[j + 1]), preferred_element_type=f32)
                for a, j in zip(a_qks, pairs)]
        oqs = []
        for oq2 in oq2s:
            oqs += [oq2[:, 0:2 * d], oq2[:, 2 * d:4 * d]]
        for i, r, nm, oq, q, egc, g_last in zip(ids, rows, nms, oqs, qs, egcs, g_lasts):
            m0 = pl.ds(pl.multiple_of(i * d, d), d)
            nn[m0, :] = nm[:, 0:d]
            mm[m0, :] = nm[:, d:2 * d].astype(bf16)
            oo[r, :] = oq[:, 0:d]
            qq[r, :] = (q * egc - oq[:, d:2 * d]).astype(bf16)
            dd[pl.ds(pl.multiple_of(i * 8, 8), 8), :] = jnp.broadcast_to(jnp.exp(g_last), (8, d))
        return carry

    lax.fori_loop(0, nb * n_chunks // DN_PREP_UNROLL, chunk_prep, 0)

    nn2, ss = kn, vn

    def chunk_pair(it, carry):
        ids = [it * DN_PAIR_UNROLL + cc for cc in range(DN_PAIR_UNROLL)]
        blk1 = [pl.ds(pl.multiple_of(2 * p * d, d), d) for p in ids]
        blk2 = [pl.ds(pl.multiple_of((2 * p + 1) * d, d), d) for p in ids]
        d1s = [dd[pl.ds(pl.multiple_of(2 * p * 8, 8), 1), :] for p in ids]
        d2s = [dd[pl.ds(pl.multiple_of((2 * p + 1) * 8, 8), 1), :] for p in ids]
        m1s = [mm[b, :] for b in blk1]
        m2s = [mm[b, :] for b in blk2]
        n1s = [nn[b, :] for b in blk1]
        xs = [jnp.dot(m2, jnp.concatenate([m1, n1.astype(bf16)], axis=1), preferred_element_type=f32)
              for m1, m2, n1 in zip(m1s, m2s, n1s)]
        for p, b2, d1, d2, m1, m2, n1, x in zip(ids, blk2, d1s, d2s, m1s, m2s, n1s, xs):
            pblk = pl.ds(pl.multiple_of(p * d, d), d)
            mm2[pblk, :] = (d2 * m1.astype(f32) + d1 * m2.astype(f32) - x[:, 0:d]).astype(bf16)
            nn2[pblk, :] = d2 * n1 - x[:, d:2 * d] + nn[b2, :]
            dd2[pl.ds(pl.multiple_of(p * 8, 8), 8), :] = jnp.broadcast_to(d1 * d2, (8, d))
        return carry

    n_pairs = n_chunks // 2
    lax.fori_loop(0, nb * n_pairs // DN_PAIR_UNROLL, chunk_pair, 0)

    def pair_scan(i, states):
        ids = [b * n_pairs + i for b in range(nb)]
        blocks = [pl.ds(pl.multiple_of(j * d, d), d) for j in ids]
        for blk, s in zip(blocks, states):
            ss[blk, :] = s
        prods = [jnp.dot(mm2[blk, :], s.astype(bf16), preferred_element_type=f32) for blk, s in zip(blocks, states)]
        return tuple(s * dd2[pl.ds(pl.multiple_of(j * 8, 8), 1), :] - pr + nn2[blk, :]
                     for s, j, blk, pr in zip(states, ids, blocks, prods))

    lax.fori_loop(0, n_pairs, pair_scan, tuple(jnp.zeros((d, d), f32) for _ in range(nb)))

    nw = nw_ref[...]

    def chunk_out(it, carry):
        pids = [it * (DN_OUT_UNROLL // 2) + cc for cc in range(DN_OUT_UNROLL // 2)]
        s0 = [ss[pl.ds(pl.multiple_of(p * d, d), d), :] for p in pids]
        s0b = [s.astype(bf16) for s in s0]
        s1b = [(s * dd[pl.ds(pl.multiple_of(2 * p * 8, 8), 1), :]
                - jnp.dot(mm[pl.ds(pl.multiple_of(2 * p * d, d), d), :], sb, preferred_element_type=f32)
                + nn[pl.ds(pl.multiple_of(2 * p * d, d), d), :]).astype(bf16)
               for p, s, sb in zip(pids, s0, s0b)]
        rows, states = [], []
        for p, sb0, sb1 in zip(pids, s0b, s1b):
            rows += [pl.ds(pl.multiple_of(2 * p * c, c), c), pl.ds(pl.multiple_of((2 * p + 1) * c, c), c)]
            states += [sb0, sb1]
        outs = [jnp.dot(qq[r, :], sb, preferred_element_type=f32) + oo[r, :] for r, sb in zip(rows, states)]
        for r, o in zip(rows, outs):
            o = o * lax.rsqrt(jnp.mean(o * o, axis=-1, keepdims=True) + 1e-6) * nw
            z = z_ref[r, :]
            o_ref[r, :] = (o * (z * jax.nn.sigmoid(z))).astype(o_ref.dtype)
        return carry

    lax.fori_loop(0, nb * n_chunks // DN_OUT_UNROLL, chunk_out, 0)


def _deltanet(pf, conv_w, a_log, dt_bias, norm_w, bsz, seq):
    d = DN_HEAD_DIM
    nb = DN_BATCHES if bsz % DN_BATCHES == 0 else 1
    rows = nb * seq
    n_chunks = rows // DN_CHUNK
    qkv0 = PF_QKVB // d
    scal = jnp.stack([a_log, dt_bias]).astype(jnp.float32)
    f32 = jnp.float32
    bf16 = jnp.bfloat16
    cw4 = conv_w.astype(f32).reshape(DN_CONV, 3, DN_HEADS, d).transpose(2, 1, 0, 3)
    assert 2 * DN_CHUNK == d and n_chunks % DN_PREP_UNROLL == 0 and n_chunks % DN_OUT_UNROLL == 0
    assert (seq // DN_CHUNK) % 2 == 0 and (n_chunks // 2) % DN_PAIR_UNROLL == 0 and DN_OUT_UNROLL % 2 == 0
    return pl.pallas_call(
        functools.partial(_dn_kernel, seq=seq),
        grid=(bsz // nb, DN_HEADS),
        in_specs=[
            pl.BlockSpec(memory_space=pltpu.SMEM),
            pl.BlockSpec((rows, d), lambda b, h: (b, qkv0 + h)),
            pl.BlockSpec((rows, d), lambda b, h: (b, qkv0 + DN_HEADS + h)),
            pl.BlockSpec((rows, d), lambda b, h: (b, qkv0 + 2 * DN_HEADS + h)),
            pl.BlockSpec((rows, LANES), lambda b, h: (b, PF_SMALL // LANES)),
            pl.BlockSpec((rows, d), lambda b, h: (b, PF_ZB // d + h)),
            pl.BlockSpec((None, 3, DN_CONV, d), lambda b, h: (h, 0, 0, 0)),
            pl.BlockSpec((1, d), lambda b, h: (0, 0)),
        ],
        out_specs=pl.BlockSpec((rows, d), lambda b, h: (b, h)),
        out_shape=jax.ShapeDtypeStruct((bsz * seq, DN_WIDTH), jnp.bfloat16),
        scratch_shapes=[
            pltpu.VMEM((rows, d), f32), pltpu.VMEM((rows, d), f32), pltpu.VMEM((rows, d), f32),
            pltpu.VMEM((rows, d), f32), pltpu.VMEM((rows, d), f32),
            pltpu.VMEM((n_chunks * d, d), bf16), pltpu.VMEM((n_chunks * d, d), f32),
            pltpu.VMEM((rows, d), bf16), pltpu.VMEM((rows, d), f32),
            pltpu.VMEM((n_chunks * 8, d), f32),
            pltpu.VMEM((n_chunks // 2 * d, d), bf16),
            pltpu.VMEM((n_chunks // 2 * 8, d), f32),
        ],
        compiler_params=_cparams(2), name="deltanet",
    )(scal, pf, pf, pf, pf, pf, cw4, norm_w.astype(f32).reshape(1, d))


def _out_kernel(oa_ref, ob_ref, gma_ref, gmb_ref, x_ref, p_ref, wa_ref, wb_ref, wo_ref, wpg_ref, wp_ref,
                lng_ref, lnb_ref, o_ref):
    f32 = jnp.float32
    bf16 = jnp.bfloat16
    tm = o_ref.shape[0]
    sub = tm // OUT_SUBTILES
    parts = [slice(i * sub, (i + 1) * sub) for i in range(OUT_SUBTILES)]
    y_a = [jnp.dot(oa_ref[r, :], wa_ref[...], preferred_element_type=f32) for r in parts]
    y_b = [jnp.dot(ob_ref[r, :], wb_ref[...], preferred_element_type=f32) for r in parts]
    pw = [jnp.dot(p_ref[r, :].astype(bf16), wp_ref[...], preferred_element_type=f32) for r in parts]
    mix = [(jax.nn.sigmoid(gma_ref[r, :]) * ya + jax.nn.sigmoid(gmb_ref[r, :]) * yb).astype(bf16)
           for r, ya, yb in zip(parts, y_a, y_b)]
    h = [DEEPNORM_ALPHA * x_ref[r, :] + jnp.dot(mx, wo_ref[...], preferred_element_type=f32)
         for r, mx in zip(parts, mix)]
    gate = [jnp.dot(hh.astype(bf16), wpg_ref[...], preferred_element_type=f32) for hh in h]
    for r, hh, gt, pp in zip(parts, h, gate, pw):
        hh = hh + jax.nn.sigmoid(gt) * pp
        mu = jnp.mean(hh, axis=-1, keepdims=True)
        hc = hh - mu
        var = jnp.mean(hc * hc, axis=-1, keepdims=True)
        o_ref[r, :] = (hc * lax.rsqrt(var + 1e-5) * lng_ref[...] + lnb_ref[...]).astype(o_ref.dtype)


def _out_block(o_a, o_b, pf, x2, p2, wa, wb, wo, wpg, wp, ln_g, ln_b, tm):
    t = x2.shape[0]
    bf = jnp.bfloat16

    def full(shape):
        return pl.BlockSpec(shape, lambda i: (0, 0))

    return pl.pallas_call(
        _out_kernel,
        grid=(t // tm,),
        in_specs=[
            pl.BlockSpec((tm, NSA_WIDTH), lambda i: (i, 0)),
            pl.BlockSpec((tm, DN_WIDTH), lambda i: (i, 0)),
            pl.BlockSpec((tm, D_MODEL), lambda i: (i, PF_GM // D_MODEL)),
            pl.BlockSpec((tm, D_MODEL), lambda i: (i, PF_GM // D_MODEL + 1)),
            pl.BlockSpec((tm, D_MODEL), lambda i: (i, 0)),
            pl.BlockSpec((tm, PLE_DIM), lambda i: (i, 0)),
            full((NSA_WIDTH, D_MODEL)), full((DN_WIDTH, D_MODEL)), full((D_MODEL, D_MODEL)),
            full((D_MODEL, D_MODEL)), full((PLE_DIM, D_MODEL)), full((1, D_MODEL)), full((1, D_MODEL)),
        ],
        out_specs=pl.BlockSpec((tm, D_MODEL), lambda i: (i, 0)),
        out_shape=jax.ShapeDtypeStruct((t, D_MODEL), x2.dtype),
        compiler_params=_cparams(1), name="out_block",
    )(o_a, o_b, pf, pf, x2, p2, wa.astype(bf), wb.astype(bf), wo.astype(bf), wpg.astype(bf), wp.astype(bf),
      ln_g.astype(jnp.float32).reshape(1, D_MODEL), ln_b.astype(jnp.float32).reshape(1, D_MODEL))


def _layer(x, p, w_in, pos_k, pos_v, w1_k, w2_k, w1_v, w2_v, bias_tabs, conv_w, a_log, dt_bias, norm_w,
           w_a, w_b, w_o, w_ple, w_pg, ln_g, ln_b):
    bsz, seq, _ = x.shape
    t = bsz * seq
    x2 = x.reshape(t, D_MODEL)
    wb16, wf16 = _prep_w_in(w_in)
    pb, pf = _proj(x2, wb16, wf16, PROJ_TM if t % PROJ_TM == 0 else seq)

    pos2, w2p = _prep_compress_weights(pos_k, pos_v, w2_k, w2_v)
    kcvc = _compress(pf, pos2, w1_k, w1_v, w2p, bsz, seq)

    bias_sel, bias_win, bias_cmp = bias_tabs
    o_a = _nsa(pb, pf, kcvc, bias_sel, bias_win, bias_cmp, bsz, seq)
    o_b = _deltanet(pf, conv_w, a_log, dt_bias, norm_w, bsz, seq)
    out = _out_block(o_a, o_b, pf, x2, p.reshape(t, PLE_DIM), w_a, w_b, w_o, w_pg, w_ple, ln_g, ln_b,
                     OUT_TM if t % OUT_TM == 0 else seq)
    return out.reshape(bsz, seq, D_MODEL)


def kernel(x, p, w_in, cmp_pos_k, cmp_pos_v, cmp_w1_k, cmp_w2_k, cmp_w1_v, cmp_w2_v, rel_bias, dn_conv_w,
           dn_a_log, dn_dt_bias, dn_norm_w, w_branch_a, w_branch_b, w_out, w_ple, w_ple_gate, ln_g, ln_b):
    depth = w_in.shape[0]
    bias_tabs = _bias_tables(rel_bias, x.shape[1])
    for i in range(depth):
        x = _layer(x, p[i], w_in[i], cmp_pos_k[i], cmp_pos_v[i], cmp_w1_k[i], cmp_w2_k[i], cmp_w1_v[i],
                   cmp_w2_v[i], bias_tabs, dn_conv_w[i], dn_a_log[i], dn_dt_bias[i], dn_norm_w[i],
                   w_branch_a[i], w_branch_b[i], w_out[i], w_ple[i], w_ple_gate[i], ln_g[i], ln_b[i])
    return x
```

```python
import functools
import math

import numpy as np
import jax
import jax.numpy as jnp
from jax import lax
from jax.experimental import pallas as pl
from jax.experimental.pallas import tpu as pltpu

D_MODEL = 1024
PLE_DIM = 256
NSA_HEADS = 8
NSA_GROUPS = 2
NSA_HPG = NSA_HEADS // NSA_GROUPS
NSA_HEAD_DIM = 64
NSA_WIDTH = NSA_HEADS * NSA_HEAD_DIM
NSA_KV = NSA_GROUPS * NSA_HEAD_DIM
CMP_BLOCK = 32
CMP_STRIDE = 16
CMP_HIDDEN = 256
SEL_BLOCK = 64
SEL_TOPK = 8
WINDOW = 512
DN_HEADS = 4
DN_HEAD_DIM = 128
DN_WIDTH = DN_HEADS * DN_HEAD_DIM
DN_CONV = 4
DN_CHUNK = 64
NUM_BUCKETS = 32
REL_MAX_DIST = 1024
DEEPNORM_ALPHA = 2.0 ** 0.25
NEG = -1e30
FORCE = 1e6
LOG2E = 1.4426950408889634

LANES = 128
TILE = 128
SEL_LANE0 = 64
SEL_GROUP = 4
NSA_TILES_PER_STEP = 4
NSA_V_ROWS = NSA_HEAD_DIM + 16
VMEM_LIMIT = 56 * 1024 * 1024
PROJ_TM = 512
PROJ_TN = 1024
OUT_TM = 512
OUT_SUBTILES = 2
DN_INV_BLOCK = 16
DN_BATCHES = 2
DN_OUT_UNROLL = 16
DN_PAIR_UNROLL = 8
DN_DECAY_PARTS = 2
DN_PREP_UNROLL = 64

HIGHEST = lax.Precision.HIGHEST

PB_QP = 0
PB_KS = PB_QP + NSA_WIDTH
PB_KW = PB_KS + NSA_KV
PB_VS = PB_KW + NSA_KV
PB_VW = PB_VS + NSA_KV
PB_WIDTH = PB_VW + NSA_KV
PF_GM = 0
PF_QKVB = PF_GM + 2 * D_MODEL
PF_ZA = PF_QKVB + 3 * DN_WIDTH
PF_ZB = PF_ZA + NSA_WIDTH
PF_KC = PF_ZB + DN_WIDTH
PF_VC = PF_KC + NSA_KV
PF_SMALL = PF_VC + NSA_KV
PF_WIDTH = PF_SMALL + LANES
SMALL_BETA = 3 * NSA_HEADS
SMALL_A = SMALL_BETA + DN_HEADS


def _bucket_thresholds():
    max_exact = NUM_BUCKETS // 2
    span = NUM_BUCKETS - max_exact
    ratio = REL_MAX_DIST // max_exact
    thr = list(range(1, max_exact + 1))
    for k in range(1, span):
        n = max_exact
        while n ** span < max_exact ** span * ratio ** k:
            n += 1
        thr.append(n)
    return tuple(thr)


_THR = _bucket_thresholds()


def _cparams(n_axes):
    return pltpu.CompilerParams(dimension_semantics=("arbitrary",) * n_axes, vmem_limit_bytes=VMEM_LIMIT)


def _proj_kernel(x_ref, wb_ref, wf_ref, pb_ref, pf_ref):
    nt_dims = (((1,), (1,)), ((), ()))
    xb = x_ref[...].astype(jnp.bfloat16)
    pb_ref[...] = lax.dot_general(xb, wb_ref[...], nt_dims, preferred_element_type=jnp.float32).astype(pb_ref.dtype)
    n = pf_ref.shape[1]
    for c0 in range(0, n, PROJ_TN):
        c1 = min(c0 + PROJ_TN, n)
        pf_ref[:, c0:c1] = lax.dot_general(xb, wf_ref[c0:c1, :], nt_dims, preferred_element_type=jnp.float32)


def _proj(x2, wb, wf, tm):
    t, d = x2.shape
    resident = dict(pipeline_mode=pl.Buffered(1))
    return pl.pallas_call(
        _proj_kernel,
        grid=(t // tm,),
        in_specs=[pl.BlockSpec((tm, d), lambda i: (i, 0)),
                  pl.BlockSpec((PB_WIDTH, d), lambda i: (0, 0), **resident),
                  pl.BlockSpec((PF_WIDTH, d), lambda i: (0, 0), **resident)],
        out_specs=[pl.BlockSpec((tm, PB_WIDTH), lambda i: (i, 0)),
                   pl.BlockSpec((tm, PF_WIDTH), lambda i: (i, 0))],
        out_shape=[jax.ShapeDtypeStruct((t, PB_WIDTH), jnp.bfloat16),
                   jax.ShapeDtypeStruct((t, PF_WIDTH), jnp.float32)],
        compiler_params=_cparams(1), name="proj",
    )(x2, wb, wf)


def _prep_w_in(w):
    d = w.shape[0]
    wt = jnp.swapaxes(w, 0, 1)
    o = 0
    wq = wt[o:o + NSA_WIDTH]; o += NSA_WIDTH
    wkv = wt[o:o + 6 * NSA_KV]; o += 6 * NSA_KV
    wg = wt[o:o + 3 * NSA_HEADS]; o += 3 * NSA_HEADS
    wza = wt[o:o + NSA_WIDTH]; o += NSA_WIDTH
    wqkvb = wt[o:o + 3 * DN_WIDTH]; o += 3 * DN_WIDTH
    wbeta_a = wt[o:o + 2 * DN_HEADS]; o += 2 * DN_HEADS
    wzb = wt[o:o + DN_WIDTH]; o += DN_WIDTH
    wgm = wt[o:o + 2 * D_MODEL]
    wkcvc, wks, wvs, wkw, wvw = (wkv[0:2 * NSA_KV], wkv[2 * NSA_KV:3 * NSA_KV], wkv[3 * NSA_KV:4 * NSA_KV],
                                 wkv[4 * NSA_KV:5 * NSA_KV], wkv[5 * NSA_KV:6 * NSA_KV])
    wb = jnp.concatenate([wq * (NSA_HEAD_DIM ** -0.5 * LOG2E), wks, wkw, wvs, wvw], axis=0).astype(jnp.bfloat16)
    pad = jnp.zeros((LANES - 3 * NSA_HEADS - 2 * DN_HEADS, d), w.dtype)
    wf = jnp.concatenate([wgm, wqkvb, wza, wzb, wkcvc, wg, wbeta_a, pad], axis=0).astype(jnp.bfloat16)
    return wb, wf


def _bias_kernel(tab_ref, sel_ref, win_ref, cmp_ref, *, n_cmp):
    h = pl.program_id(0)

    def bucket_of(n):
        return sum(1 for t in _THR if n >= t)

    def lookup(n, lo, hi):
        b_lo, b_hi = bucket_of(max(lo, 0)), bucket_of(max(hi, 0))
        val = jnp.full(n.shape, tab_ref[b_lo, h], jnp.float32)
        for b in range(b_lo + 1, b_hi + 1):
            val = jnp.where(n >= _THR[b - 1], tab_ref[b, h], val)
        return val * LOG2E

    kj = lax.broadcasted_iota(jnp.int32, (TILE, TILE), 0)
    qi = lax.broadcasted_iota(jnp.int32, (TILE, TILE), 1)
    n_sel_tiles = sel_ref.shape[1] - 1
    n_win_tiles = win_ref.shape[1] - 1
    for dt in range(max(n_sel_tiles, n_win_tiles)):
        dist = dt * TILE + qi - kj
        v = lookup(jnp.maximum(dist, 0), dt * TILE - (TILE - 1), dt * TILE + (TILE - 1))
        if dt < n_sel_tiles:
            sel_ref[0, dt] = jnp.where(dist >= 0, v, NEG)
        if dt < n_win_tiles:
            win_ref[0, dt] = jnp.where((dist >= 0) & (dist < WINDOW), v, NEG)
    sel_ref[0, n_sel_tiles] = jnp.full((TILE, TILE), NEG, jnp.float32)
    win_ref[0, n_win_tiles] = jnp.full((TILE, TILE), NEG, jnp.float32)
    nc = cmp_ref.shape[2]
    band = 16
    for t in range(cmp_ref.shape[1]):
        for c0 in range(0, nc, band):
            c = c0 + lax.broadcasted_iota(jnp.int32, (band, TILE), 0)
            s = t * TILE + lax.broadcasted_iota(jnp.int32, (band, TILE), 1)
            dist = s - (c * CMP_STRIDE + CMP_BLOCK - 1)
            lo = t * TILE - ((c0 + band - 1) * CMP_STRIDE + CMP_BLOCK - 1)
            hi = t * TILE + TILE - 1 - (c0 * CMP_STRIDE + CMP_BLOCK - 1)
            cmp_ref[0, t, c0:c0 + band, :] = jnp.where((dist >= 0) & (c < n_cmp),
                                                       lookup(jnp.maximum(dist, 0), lo, hi), NEG)


def _bias_tables(rel_bias, seq):
    nq = seq // TILE
    nwin = WINDOW // TILE + 1
    nc = seq // CMP_STRIDE
    n_cmp = nc - CMP_BLOCK // CMP_STRIDE + 1
    return pl.pallas_call(
        functools.partial(_bias_kernel, n_cmp=n_cmp),
        grid=(NSA_HEADS,),
        in_specs=[pl.BlockSpec(memory_space=pltpu.SMEM)],
        out_specs=[pl.BlockSpec((1, nq + 1, TILE, TILE), lambda h: (h, 0, 0, 0)),
                   pl.BlockSpec((1, nwin + 1, TILE, TILE), lambda h: (h, 0, 0, 0)),
                   pl.BlockSpec((1, nq, nc, TILE), lambda h: (h, 0, 0, 0))],
        out_shape=[jax.ShapeDtypeStruct((NSA_HEADS, nq + 1, TILE, TILE), jnp.float32),
                   jax.ShapeDtypeStruct((NSA_HEADS, nwin + 1, TILE, TILE), jnp.float32),
                   jax.ShapeDtypeStruct((NSA_HEADS, nq, nc, TILE), jnp.float32)],
        compiler_params=_cparams(1), name="bias_tables",
    )(rel_bias.astype(jnp.float32))


def _gelu_tanh(x):
    return x * (0.5 * (1.0 + jnp.tanh(math.sqrt(2.0 / math.pi) * (x + 0.044715 * (x * x * x)))))


def _compress_kernel(xk_ref, xv_ref, pos_ref, w1k_ref, w1v_ref, w2_ref, o_ref, wbd):
    nc = o_ref.shape[2]
    half = CMP_BLOCK // 2
    dh = NSA_HEAD_DIM
    bf16 = jnp.bfloat16

    @pl.when(pl.program_id(0) == 0)
    def _():
        zero = jnp.zeros((dh, CMP_HIDDEN), bf16)
        for kv, w1_ref in enumerate((w1k_ref, w1v_ref)):
            for a in range(2):
                for l in range(half):
                    r0 = (a * half + l) * dh
                    wl = w1_ref[r0:r0 + dh, :].astype(bf16)
                    wbd[kv, a, l * NSA_KV:(l + 1) * NSA_KV, :] = jnp.concatenate(
                        [jnp.concatenate([wl, zero], axis=1), jnp.concatenate([zero, wl], axis=1)], axis=0)

    rs = [jnp.concatenate([x_ref[pl.ds(l, nc, stride=CMP_STRIDE), :] for l in range(CMP_STRIDE)], axis=1)
          for x_ref in (xk_ref, xv_ref)]
    a = [jnp.dot((r + pos_ref[kv, 0:1, :]).astype(bf16), wbd[kv, 0], preferred_element_type=jnp.float32)
         for kv, r in enumerate(rs)]
    b = [jnp.dot((r + pos_ref[kv, 1:2, :]).astype(bf16), wbd[kv, 1], preferred_element_type=jnp.float32)
         for kv, r in enumerate(rs)]
    for kv in range(2):
        hid = a[kv] + pltpu.roll(b[kv], nc - 1, 0)
        o_ref[0, kv] = jnp.dot(_gelu_tanh(hid).astype(bf16), w2_ref[kv], preferred_element_type=jnp.float32)


def _compress(pf, pos2, w1_k, w1_v, w2p, bsz, seq):
    nc = seq // CMP_STRIDE
    width = CMP_STRIDE * NSA_KV
    hid = NSA_GROUPS * CMP_HIDDEN
    assert CMP_BLOCK == 2 * CMP_STRIDE and PF_VC == PF_KC + NSA_KV
    return pl.pallas_call(
        _compress_kernel,
        grid=(bsz,),
        in_specs=[pl.BlockSpec((seq, NSA_KV), lambda b: (b, PF_KC // NSA_KV)),
                  pl.BlockSpec((seq, NSA_KV), lambda b: (b, PF_VC // NSA_KV)),
                  pl.BlockSpec((2, 2, width), lambda b: (0, 0, 0)),
                  pl.BlockSpec(w1_k.shape, lambda b: (0, 0)),
                  pl.BlockSpec(w1_v.shape, lambda b: (0, 0)),
                  pl.BlockSpec((2, hid, LANES), lambda b: (0, 0, 0))],
        out_specs=pl.BlockSpec((1, 2, nc, LANES), lambda b: (b, 0, 0, 0)),
        out_shape=jax.ShapeDtypeStruct((bsz, 2, nc, LANES), jnp.float32),
        scratch_shapes=[pltpu.VMEM((2, 2, width, hid), jnp.bfloat16)],
        compiler_params=_cparams(1), name="compress",
    )(pf, pf, pos2, w1_k, w1_v, w2p)


def _prep_compress_weights(pos_k, pos_v, w2_k, w2_v):
    eye = jnp.eye(NSA_GROUPS, dtype=jnp.float32)
    half = CMP_BLOCK // 2

    def w2_both(w2):
        return jnp.einsum('jd,gh->gjhd', w2, eye).reshape(NSA_GROUPS * CMP_HIDDEN, NSA_GROUPS * NSA_HEAD_DIM)

    def pos_both(pos):
        p = pos.reshape(2, half, 1, NSA_HEAD_DIM)
        return jnp.broadcast_to(p, (2, half, NSA_GROUPS, NSA_HEAD_DIM)).reshape(2, half * NSA_KV)

    pos2 = jnp.stack([pos_both(pos_k), pos_both(pos_v)]).astype(jnp.float32)
    w2p = jnp.stack([w2_both(w2_k), w2_both(w2_v)]).astype(jnp.bfloat16)
    return pos2, w2p


def _pair_lanes(x, g):
    sw = pltpu.roll(x, LANES // 2, 1)
    lane = lax.broadcasted_iota(jnp.int32, x.shape, 1)
    own = (lane < LANES // 2) == (g == 0)
    return jnp.where(own, x, sw)


def _group_rows_t(x, g):
    xt = x.T
    half = LANES // 2
    return jnp.where(g == 0, xt[0:half, :], xt[half:LANES, :])


def _nsa_step_kernel(q_ref, qn_ref, ks_ref, kw_ref, vs_ref, vw_ref, kcvc_ref, bc_ref, bsel_ref, bwin_ref,
                     small_ref, za_ref, o_ref,
                     ksa, kwp, vst, vwt, kcs, vct, qa_s, qp_s, ocmp_s, sbuf, *, n_cmp, n_sel, n_top):
    g = pl.program_id(0)
    step_id = pl.program_id(2)
    tq = TILE
    dh = NSA_HEAD_DIM
    seq = ks_ref.shape[0]
    nq = seq // TILE
    nc = kcvc_ref.shape[2]
    n_wt = bwin_ref.shape[1] - 1
    heads = range(NSA_HPG)
    f32 = jnp.float32
    bf16 = jnp.bfloat16
    nt_dims = (((1,), (1,)), ((), ()))
    gk = SEL_GROUP * TILE

    def select_tile(src_ref, r0, tile, slot):
        q32 = src_ref[r0:r0 + tq, :].astype(f32)
        low_half = lax.broadcasted_iota(jnp.int32, (tq, LANES), 1) < NSA_HEAD_DIM
        qh = []
        for hh in heads:
            blk = q32[:, (hh // 2) * LANES:(hh // 2 + 1) * LANES]
            if hh % 2:
                blk = pltpu.roll(blk, LANES // 2, 1)
            qh.append(jnp.where(low_half, blk, 0.0).astype(bf16))
        for hh in heads:
            qp_s[slot, hh * tq:(hh + 1) * tq, :] = qh[hh]
        q_all = qp_s[slot]
        s_c = lax.dot_general(kcs[...], q_all, nt_dims, preferred_element_type=f32)
        bias_c = jnp.concatenate([bc_ref[hh, tile] for hh in heads], axis=1)
        valid = bias_c > 0.5 * NEG
        s_c = s_c + bias_c
        e = jnp.where(valid, jnp.exp2(s_c - jnp.max(s_c, axis=0, keepdims=True)), 0.0)
        den = jnp.maximum(jnp.sum(e, axis=0, keepdims=True), 1e-30)
        p_c = e * (1.0 / den)
        psum = sum(p_c[:, hh * tq:(hh + 1) * tq] for hh in heads)
        sj = lax.broadcasted_iota(jnp.int32, (n_sel, nc), 0)
        ci = lax.broadcasted_iota(jnp.int32, (n_sel, nc), 1)
        overlap = ((ci * CMP_STRIDE < (sj + 1) * SEL_BLOCK) & (ci * CMP_STRIDE + CMP_BLOCK > sj * SEL_BLOCK)
                   & (ci < n_cmp)).astype(bf16)
        imp_t = sum(jnp.dot(overlap, part, preferred_element_type=f32)
                    for part in _split_bf16(psum, 3))
        ocmp_s[slot] = jnp.dot(vct[...], p_c.astype(bf16), preferred_element_type=f32)
        blk = lax.broadcasted_iota(jnp.int32, (n_sel, tq), 0)
        cur = (tile * tq + lax.broadcasted_iota(jnp.int32, (n_sel, tq), 1)) >> 6
        forced = (blk == 0) | (blk == cur) | (blk == cur - 1)
        imp_t = jnp.where(forced, FORCE, jnp.where(blk > cur, -FORCE, imp_t))
        rank = jnp.zeros((n_sel, tq), jnp.int32)
        for j in range(n_sel):
            other = imp_t[j:j + 1, :]
            ahead = (other > imp_t) | ((other == imp_t) & (blk > j))
            rank = rank + ahead.astype(jnp.int32)
        selb = jnp.where(rank < n_top, 0.0, NEG)
        pieces = [jnp.zeros((SEL_LANE0, tq), f32), selb]
        if LANES - SEL_LANE0 - n_sel > 0:
            pieces.append(jnp.zeros((LANES - SEL_LANE0 - n_sel, tq), f32))
        selb_r = jnp.concatenate(pieces, axis=0).T.astype(bf16)
        for hh in heads:
            qa_s[slot, hh * tq:(hh + 1) * tq, :] = qh[hh] + selb_r

    @pl.when(step_id == 0)
    def _():
        row = lax.broadcasted_iota(jnp.int32, (seq, LANES), 0)
        lane = lax.broadcasted_iota(jnp.int32, (seq, LANES), 1)
        onehot = (lane - SEL_LANE0) == (row >> 6)
        own = lane < NSA_HEAD_DIM
        ks_g = jnp.where(own, _pair_lanes(ks_ref[...].astype(f32), g), 0.0)
        ksa[...] = jnp.where(onehot, 1.0, ks_g).astype(bf16)
        pad = (n_wt - 1) * TILE
        kwp[0:pad, :] = jnp.zeros((pad, LANES), bf16)
        kwp[pad:pad + seq, :] = jnp.where(own, _pair_lanes(kw_ref[...].astype(f32), g), 0.0).astype(bf16)
        for kt in range(n_wt - 1):
            vwt[kt] = jnp.zeros(vwt.shape[1:], bf16)
        extra = vst.shape[1] - dh
        ones_row = (lax.broadcasted_iota(jnp.int32, (extra, TILE), 0) == 0).astype(bf16)
        for kt in range(nq):
            rows = slice(kt * TILE, (kt + 1) * TILE)
            vst[kt, 0:dh, :] = _group_rows_t(vs_ref[rows, :].astype(f32), g).astype(bf16)
            vst[kt, dh:dh + extra, :] = ones_row
            vwt[kt + n_wt - 1, 0:dh, :] = _group_rows_t(vw_ref[rows, :].astype(f32), g).astype(bf16)
            vwt[kt + n_wt - 1, dh:dh + extra, :] = ones_row
        kcs[...] = _pair_lanes(kcvc_ref[0, 0], g).astype(bf16)
        for ct in range(nc // TILE):
            rows = slice(ct * TILE, (ct + 1) * TILE)
            vct[:, rows] = _group_rows_t(kcvc_ref[0, 1, rows, :], g).astype(bf16)
        select_tile(q_ref, 0, 0, 0)

    tps = NSA_TILES_PER_STEP

    def one_tile(n_groups, slot):
        qt = step_id * tps + slot
        rows = slice(slot * tq, (slot + 1) * tq)
        q_all = qp_s[slot]
        q_aug = qa_s[slot]
        kw_rows = kwp[pl.ds(pl.multiple_of(qt * TILE, TILE), n_wt * TILE), :]
        s_w = lax.dot_general(kw_rows, q_all, nt_dims, preferred_element_type=f32)

        def sel_scores(gi):
            s = lax.dot_general(ksa[gi * gk:(gi + 1) * gk, :], q_aug, nt_dims, preferred_element_type=f32)
            tiles = []
            for t in range(SEL_GROUP):
                dt = qt - (gi * SEL_GROUP + t)
                idx = jnp.where(dt >= 0, dt, nq)
                tiles.append(jnp.concatenate([bsel_ref[hh, idx] for hh in heads], axis=1))
            sbuf[gi % 2] = s + jnp.concatenate(tiles, axis=0)

        sel_scores(0)
        if slot + 1 < tps:
            select_tile(q_ref, (slot + 1) * tq, qt + 1, slot + 1)
        else:
            select_tile(qn_ref, 0, jnp.minimum(qt + 1, nq - 1), 0)

        w_tiles = []
        for t in range(n_wt):
            dt = n_wt - 1 - t
            idx = jnp.where(qt >= dt, dt, n_wt)
            w_tiles.append(jnp.concatenate([bwin_ref[hh, idx] for hh in heads], axis=1))
        s_w = s_w + jnp.concatenate(w_tiles, axis=0)
        p_w = jnp.exp2(s_w - jnp.max(s_w, axis=0, keepdims=True))
        v_w = jnp.concatenate([vwt[qt + t] for t in range(n_wt)], axis=1)
        acc_w = jnp.dot(v_w, p_w.astype(bf16), preferred_element_type=f32)
        o_win = acc_w[0:dh, :] * (1.0 / acc_w[dh:dh + 1, :])

        m = jnp.full((1, NSA_HPG * tq), NEG, f32)
        acc = jnp.zeros((vst.shape[1], NSA_HPG * tq), f32)
        for gi in range(n_groups):
            if gi + 1 < n_groups:
                sel_scores(gi + 1)
            s = sbuf[gi % 2]
            m_new = jnp.maximum(m, jnp.max(s, axis=0, keepdims=True))
            alpha = jnp.exp2(m - m_new)
            p = jnp.exp2(s - m_new)
            v_t = jnp.concatenate([vst[gi * SEL_GROUP + t] for t in range(SEL_GROUP)], axis=1)
            acc = alpha * acc + jnp.dot(v_t, p.astype(bf16), preferred_element_type=f32)
            m = m_new
        o_sel = acc[0:dh, :] * (1.0 / acc[dh:dh + 1, :])
        o_cmp = ocmp_s[slot]

        gates_t = jax.nn.sigmoid(small_ref[rows, :]).T
        mixed = []
        for hh in heads:
            def gate(br):
                r = br * NSA_HEADS + hh
                return jnp.where(g == 0, gates_t[r:r + 1, :], gates_t[r + NSA_HPG:r + NSA_HPG + 1, :])
            cols = slice(hh * tq, (hh + 1) * tq)
            mixed.append(gate(0) * o_cmp[:, cols] + gate(1) * o_sel[:, cols] + gate(2) * o_win[:, cols])
        o = jnp.concatenate([jnp.concatenate(mixed[2 * j:2 * j + 2], axis=0).T for j in range(NSA_HPG // 2)],
                            axis=1)
        z = za_ref[rows, :]
        o_ref[rows, :] = (o * (z * jax.nn.sigmoid(z))).astype(o_ref.dtype)

    def step(n_groups):
        for slot in range(tps):
            one_tile(n_groups, slot)

    for n_groups in range(1, nq // SEL_GROUP + 1):
        pl.when(step_id // (SEL_GROUP // tps) == n_groups - 1)(functools.partial(step, n_groups))


def _nsa(pb, pf, kcvc, bias_sel, bias_win, bias_cmp, bsz, seq):
    nq = seq // TILE
    nc = seq // CMP_STRIDE
    n_cmp = nc - CMP_BLOCK // CMP_STRIDE + 1
    n_sel = seq // SEL_BLOCK
    n_top = min(SEL_TOPK, n_sel)
    nwin = bias_win.shape[1]
    tps = NSA_TILES_PER_STEP
    assert nq % SEL_GROUP == 0 and nc % TILE == 0 and bias_sel.shape[1] == nq + 1
    assert tps >= 2 and SEL_GROUP % tps == 0
    ns = nq // tps
    rows = tps * TILE
    gw = NSA_HPG * NSA_HEAD_DIM
    kern = functools.partial(_nsa_step_kernel, n_cmp=n_cmp, n_sel=n_sel, n_top=n_top)
    return pl.pallas_call(
        kern,
        grid=(NSA_GROUPS, bsz, ns),
        in_specs=[
            pl.BlockSpec((rows, gw), lambda g, b, t: (b * ns + t, PB_QP // gw + g)),
            pl.BlockSpec((TILE, gw), lambda g, b, t: (b * nq + jnp.minimum((t + 1) * tps, nq - 1),
                                                      PB_QP // gw + g)),
            pl.BlockSpec((seq, LANES), lambda g, b, t: (b, PB_KS // LANES)),
            pl.BlockSpec((seq, LANES), lambda g, b, t: (b, PB_KW // LANES)),
            pl.BlockSpec((seq, LANES), lambda g, b, t: (b, PB_VS // LANES)),
            pl.BlockSpec((seq, LANES), lambda g, b, t: (b, PB_VW // LANES)),
            pl.BlockSpec((1, 2, nc, LANES), lambda g, b, t: (b, 0, 0, 0)),
            pl.BlockSpec((NSA_HPG, nq, nc, TILE), lambda g, b, t: (g, 0, 0, 0)),
            pl.BlockSpec((NSA_HPG, nq + 1, TILE, TILE), lambda g, b, t: (g, 0, 0, 0)),
            pl.BlockSpec((NSA_HPG, nwin, TILE, TILE), lambda g, b, t: (g, 0, 0, 0)),
            pl.BlockSpec((rows, LANES), lambda g, b, t: (b * ns + t, PF_SMALL // LANES)),
            pl.BlockSpec((rows, gw), lambda g, b, t: (b * ns + t, PF_ZA // gw + g)),
        ],
        out_specs=pl.BlockSpec((rows, gw), lambda g, b, t: (b * ns + t, g)),
        out_shape=jax.ShapeDtypeStruct((bsz * seq, NSA_WIDTH), jnp.bfloat16),
        scratch_shapes=[
            pltpu.VMEM((seq, LANES), jnp.bfloat16),
            pltpu.VMEM((seq + (nwin - 2) * TILE, LANES), jnp.bfloat16),
            pltpu.VMEM((nq, NSA_V_ROWS, TILE), jnp.bfloat16),
            pltpu.VMEM((nq + nwin - 2, NSA_V_ROWS, TILE), jnp.bfloat16),
            pltpu.VMEM((nc, LANES), jnp.bfloat16),
            pltpu.VMEM((NSA_HEAD_DIM, nc), jnp.bfloat16),
            pltpu.VMEM((tps, NSA_HPG * TILE, LANES), jnp.bfloat16),
            pltpu.VMEM((tps, NSA_HPG * TILE, LANES), jnp.bfloat16),
            pltpu.VMEM((tps, NSA_HEAD_DIM, NSA_HPG * TILE), jnp.float32),
            pltpu.VMEM((2, SEL_GROUP * TILE, NSA_HPG * TILE), jnp.float32),
        ],
        compiler_params=_cparams(3), name="nsa",
    )(pb, pb, pb, pb, pb, pb, kcvc, bias_cmp, bias_sel, bias_win, pf, pf)


def _split_bf16(a, n):
    parts = []
    for _ in range(n - 1):
        hi = a.astype(jnp.bfloat16)
        parts.append(hi)
        a = a - hi.astype(jnp.float32)
    parts.append(a.astype(jnp.bfloat16))
    return parts


def _dot3(a, b):
    ah, al = _split_bf16(a, 2)
    bh, bl = _split_bf16(b, 2)
    f32 = jnp.float32
    return (jnp.dot(ah, bh, preferred_element_type=f32) + jnp.dot(al, bh, preferred_element_type=f32)
            + jnp.dot(ah, bl, preferred_element_type=f32))


def _softplus(x):
    return jnp.maximum(x, 0.0) + jnp.log(1.0 + jnp.exp(-jnp.abs(x)))


def _dn_kernel(scal_ref, q_ref, k_ref, v_ref, small_ref, z_ref, cw_ref, nw_ref, o_ref,
               qn, kn, vn, bet, gl, mm, nn, qq, oo, dd, mm2, dd2, *, seq):
    h = pl.program_id(1)
    n_rows = q_ref.shape[0]
    nb = n_rows // seq
    c = DN_CHUNK
    n_chunks = seq // c
    d = DN_HEAD_DIM

    head = 8

    def conv_body(x_ref, which):
        n = n_rows - head
        y = x_ref[pl.ds(head, n), :] * cw_ref[which, DN_CONV - 1:DN_CONV, :]
        for j in range(DN_CONV - 1):
            y = y + x_ref[pl.ds(head - (DN_CONV - 1 - j), n), :] * cw_ref[which, j:j + 1, :]
        return y

    def conv_head(x_ref, which, r0):
        x = x_ref[pl.ds(r0, head), :]
        rowi = lax.broadcasted_iota(jnp.int32, (head, d), 0)
        y = x * cw_ref[which, DN_CONV - 1:DN_CONV, :]
        for j in range(DN_CONV - 1):
            sh = DN_CONV - 1 - j
            y = y + jnp.where(rowi >= sh, pltpu.roll(x, sh, 0), 0.0) * cw_ref[which, j:j + 1, :]
        return y

    def silu(y):
        return y * jax.nn.sigmoid(y)

    def l2n(t):
        return t * lax.rsqrt(jnp.sum(t * t, axis=-1, keepdims=True) + 1e-6)

    finish = (lambda y: l2n(silu(y)) * (d ** -0.5), lambda y: l2n(silu(y)), silu)
    for which, (x_ref, dst) in enumerate(((q_ref, qn), (k_ref, kn), (v_ref, vn))):
        dst[pl.ds(head, n_rows - head), :] = finish[which](conv_body(x_ref, which))
        for b in range(nb):
            dst[pl.ds(b * seq, head), :] = finish[which](conv_head(x_ref, which, b * seq))
    small = small_ref[...]
    lane = lax.broadcasted_iota(jnp.int32, small.shape, 1)
    beta_in = jnp.sum(jnp.where(lane == SMALL_BETA + h, small, 0.0), axis=-1, keepdims=True)
    a_in = jnp.sum(jnp.where(lane == SMALL_A + h, small, 0.0), axis=-1, keepdims=True)
    bet[...] = jnp.broadcast_to(jax.nn.sigmoid(beta_in), (n_rows, d))
    gl[...] = jnp.broadcast_to(-jnp.exp(scal_ref[0, h]) * _softplus(a_in + scal_ref[1, h]), (n_rows, d))

    f32 = jnp.float32
    bf16 = jnp.bfloat16
    nt_dims = (((1,), (1,)), ((), ()))
    ri = lax.broadcasted_iota(jnp.int32, (c, 2 * c), 0)
    lane2 = lax.broadcasted_iota(jnp.int32, (c, 2 * c), 1)
    first = lane2 < c
    cj = lane2 & (c - 1)
    incl = ri >= cj
    strict = ri > cj
    tril_b = (lax.broadcasted_iota(jnp.int32, (c, c), 0) >= lax.broadcasted_iota(jnp.int32, (c, c), 1)).astype(bf16)
    bs = DN_INV_BLOCK
    sh = bs.bit_length() - 1
    same_diag = (ri >> sh) == (cj >> sh)
    level_masks = []
    while (1 << sh) < c:
        level_masks.append(((ri >> (sh + 1)) == (cj >> (sh + 1))) & ((ri >> sh) > (cj >> sh)))
        sh += 1
    bd_r = lax.broadcasted_iota(jnp.int32, (2 * c, 2 * c), 0)
    bd_c = lax.broadcasted_iota(jnp.int32, (2 * c, 2 * c), 1)
    block_diag = (bd_r >= c) == (bd_c >= c)

    def dot_pair(a, b):
        bb = b.astype(bf16)
        bb = jnp.where(block_diag, jnp.concatenate([bb, bb], axis=0), jnp.zeros((), bf16))
        return jnp.dot(a.astype(bf16), bb, preferred_element_type=f32)

    def stack_diag(x0, x1):
        z = jnp.zeros_like(x0)
        return jnp.concatenate([jnp.concatenate([x0, z], axis=1), jnp.concatenate([z, x1], axis=1)], axis=0)

    def chunk_prep(it, carry):
        ids = [it * DN_PREP_UNROLL + cc for cc in range(DN_PREP_UNROLL)]
        rows = [pl.ds(pl.multiple_of(i * c, c), c) for i in ids]
        pairs = range(0, DN_PREP_UNROLL, 2)
        ks = [kn[r, :] for r in rows]
        betas = [bet[r, :] for r in rows]
        gcbs = [sum(jnp.dot(tril_b, part, preferred_element_type=f32)
                    for part in _split_bf16(gl[r, :], DN_DECAY_PARTS)) for r in rows]
        kbs = [k * beta for k, beta in zip(ks, betas)]
        k2ts = [jnp.concatenate([ks[j], ks[j + 1]], axis=0).T.astype(bf16) for j in pairs]

        def against_pair_keys(xs):
            return [jnp.where(first,
                              jnp.dot(xs[j].astype(bf16), k2t, preferred_element_type=f32),
                              jnp.dot(xs[j + 1].astype(bf16), k2t, preferred_element_type=f32))
                    for j, k2t in zip(pairs, k2ts)]

        a_kks = against_pair_keys(kbs)
        decays = []
        for j in pairs:
            gc_col = jnp.where(first, gcbs[j], gcbs[j + 1])
            gc_row = jnp.concatenate([gcbs[j], gcbs[j + 1]], axis=0).T[0:c, :]
            decays.append(jnp.where(incl, jnp.exp(jnp.where(incl, gc_col - gc_row, 0.0)), 0.0))
        lows = [jnp.where(strict, a * dec, 0.0) for a, dec in zip(a_kks, decays)]
        pws = [jnp.where(same_diag, -low, 0.0) for low in lows]
        es = list(pws)
        for _ in range(max(1, (bs - 1).bit_length()) - 1):
            pws = [dot_pair(pw, pw) for pw in pws]
            es = [e + pw + dot_pair(e, pw) for e, pw in zip(es, pws)]
        for below in level_masks:
            offs = [jnp.where(below, low, 0.0) for low in lows]
            xs = [off + dot_pair(e, off) for e, off in zip(es, offs)]
            es = [e - (x + dot_pair(x, e)) for e, x in zip(es, xs)]
        egcs = [jnp.exp(gcb) for gcb in gcbs]
        rhss = [jnp.concatenate([vn[r, :] * beta, kb * egc], axis=1)
                for r, beta, kb, egc in zip(rows, betas, kbs, egcs)]
        uwbs = []
        for e, j in zip(es, pairs):
            e_hi, e_lo = _split_bf16(e, 2)
            r2 = stack_diag(rhss[j], rhss[j + 1]).astype(bf16)
            er = jnp.dot(e_hi, r2, preferred_element_type=f32) + jnp.dot(e_lo, r2, preferred_element_type=f32)
            uwbs.append((rhss[j] + er[:, 0:2 * d]).astype(bf16))
            uwbs.append((rhss[j + 1] + er[:, 2 * d:4 * d]).astype(bf16))
        qs = [qn[r, :] for r in rows]
        a_qks = [a * dec for a, dec in zip(against_pair_keys(qs), decays)]
        g_lasts = [gcb[c - 1:c, :] for gcb in gcbs]
        kdec_ts = []
        for k, gcb, g_last in zip(ks, gcbs, g_lasts):
            kdec = k * jnp.exp(g_last - gcb)
            kdec_ts.append(jnp.concatenate([kdec, jnp.zeros_like(kdec)], axis=0).T[:, 0:c].astype(bf16))
        nms = [jnp.dot(kt, uwb, preferred_element_type=f32) for kt, uwb in zip(kdec_ts, uwbs)]
        oq2s = [jnp.dot(a.astype(bf16), stack_diag(uwbs[j], uwbs[j + 1]), preferred_element_type=f32)
                for a, j in zip(a_qks, pairs)]
        oqs = []
        for oq2 in oq2s:
            oqs += [oq2[:, 0:2 * d], oq2[:, 2 * d:4 * d]]
        for i, r, nm, oq, q, egc, g_last in zip(ids, rows, nms, oqs, qs, egcs, g_lasts):
            m0 = pl.ds(pl.multiple_of(i * d, d), d)
            nn[m0, :] = nm[:, 0:d]
            mm[m0, :] = nm[:, d:2 * d].astype(bf16)
            oo[r, :] = oq[:, 0:d]
            qq[r, :] = (q * egc - oq[:, d:2 * d]).astype(bf16)
            dd[pl.ds(pl.multiple_of(i * 8, 8), 8), :] = jnp.broadcast_to(jnp.exp(g_last), (8, d))
        return carry

    lax.fori_loop(0, nb * n_chunks // DN_PREP_UNROLL, chunk_prep, 0)

    nn2, ss = kn, vn

    def chunk_pair(it, carry):
        ids = [it * DN_PAIR_UNROLL + cc for cc in range(DN_PAIR_UNROLL)]
        blk1 = [pl.ds(pl.multiple_of(2 * p * d, d), d) for p in ids]
        blk2 = [pl.ds(pl.multiple_of((2 * p + 1) * d, d), d) for p in ids]
        d1s = [dd[pl.ds(pl.multiple_of(2 * p * 8, 8), 1), :] for p in ids]
        d2s = [dd[pl.ds(pl.multiple_of((2 * p + 1) * 8, 8), 1), :] for p in ids]
        m1s = [mm[b, :] for b in blk1]
        m2s = [mm[b, :] for b in blk2]
        n1s = [nn[b, :] for b in blk1]
        xs = [jnp.dot(m2, jnp.concatenate([m1, n1.astype(bf16)], axis=1), preferred_element_type=f32)
              for m1, m2, n1 in zip(m1s, m2s, n1s)]
        for p, b2, d1, d2, m1, m2, n1, x in zip(ids, blk2, d1s, d2s, m1s, m2s, n1s, xs):
            pblk = pl.ds(pl.multiple_of(p * d, d), d)
            mm2[pblk, :] = (d2 * m1.astype(f32) + d1 * m2.astype(f32) - x[:, 0:d]).astype(bf16)
            nn2[pblk, :] = d2 * n1 - x[:, d:2 * d] + nn[b2, :]
            dd2[pl.ds(pl.multiple_of(p * 8, 8), 8), :] = jnp.broadcast_to(d1 * d2, (8, d))
        return carry

    n_pairs = n_chunks // 2
    lax.fori_loop(0, nb * n_pairs // DN_PAIR_UNROLL, chunk_pair, 0)

    def pair_scan(i, states):
        ids = [b * n_pairs + i for b in range(nb)]
        blocks = [pl.ds(pl.multiple_of(j * d, d), d) for j in ids]
        for blk, s in zip(blocks, states):
            ss[blk, :] = s
        prods = [jnp.dot(mm2[blk, :], s.astype(bf16), preferred_element_type=f32) for blk, s in zip(blocks, states)]
        return tuple(s * dd2[pl.ds(pl.multiple_of(j * 8, 8), 1), :] - pr + nn2[blk, :]
                     for s, j, blk, pr in zip(states, ids, blocks, prods))

    lax.fori_loop(0, n_pairs, pair_scan, tuple(jnp.zeros((d, d), f32) for _ in range(nb)))

    nw = nw_ref[...]

    def chunk_out(it, carry):
        pids = [it * (DN_OUT_UNROLL // 2) + cc for cc in range(DN_OUT_UNROLL // 2)]
        s0 = [ss[pl.ds(pl.multiple_of(p * d, d), d), :] for p in pids]
        s0b = [s.astype(bf16) for s in s0]
        s1b = [(s * dd[pl.ds(pl.multiple_of(2 * p * 8, 8), 1), :]
                - jnp.dot(mm[pl.ds(pl.multiple_of(2 * p * d, d), d), :], sb, preferred_element_type=f32)
                + nn[pl.ds(pl.multiple_of(2 * p * d, d), d), :]).astype(bf16)
               for p, s, sb in zip(pids, s0, s0b)]
        rows, states = [], []
        for p, sb0, sb1 in zip(pids, s0b, s1b):
            rows += [pl.ds(pl.multiple_of(2 * p * c, c), c), pl.ds(pl.multiple_of((2 * p + 1) * c, c), c)]
            states += [sb0, sb1]
        outs = [jnp.dot(qq[r, :], sb, preferred_element_type=f32) + oo[r, :] for r, sb in zip(rows, states)]
        for r, o in zip(rows, outs):
            o = o * lax.rsqrt(jnp.mean(o * o, axis=-1, keepdims=True) + 1e-6) * nw
            z = z_ref[r, :]
            o_ref[r, :] = (o * (z * jax.nn.sigmoid(z))).astype(o_ref.dtype)
        return carry

    lax.fori_loop(0, nb * n_chunks // DN_OUT_UNROLL, chunk_out, 0)


def _deltanet(pf, conv_w, a_log, dt_bias, norm_w, bsz, seq):
    d = DN_HEAD_DIM
    nb = DN_BATCHES if bsz % DN_BATCHES == 0 else 1
    rows = nb * seq
    n_chunks = rows // DN_CHUNK
    qkv0 = PF_QKVB // d
    scal = jnp.stack([a_log, dt_bias]).astype(jnp.float32)
    f32 = jnp.float32
    bf16 = jnp.bfloat16
    cw4 = conv_w.astype(f32).reshape(DN_CONV, 3, DN_HEADS, d).transpose(2, 1, 0, 3)
    assert 2 * DN_CHUNK == d and n_chunks % DN_PREP_UNROLL == 0 and n_chunks % DN_OUT_UNROLL == 0
    assert (seq // DN_CHUNK) % 2 == 0 and (n_chunks // 2) % DN_PAIR_UNROLL == 0 and DN_OUT_UNROLL % 2 == 0
    return pl.pallas_call(
        functools.partial(_dn_kernel, seq=seq),
        grid=(bsz // nb, DN_HEADS),
        in_specs=[
            pl.BlockSpec(memory_space=pltpu.SMEM),
            pl.BlockSpec((rows, d), lambda b, h: (b, qkv0 + h)),
            pl.BlockSpec((rows, d), lambda b, h: (b, qkv0 + DN_HEADS + h)),
            pl.BlockSpec((rows, d), lambda b, h: (b, qkv0 + 2 * DN_HEADS + h)),
            pl.BlockSpec((rows, LANES), lambda b, h: (b, PF_SMALL // LANES)),
            pl.BlockSpec((rows, d), lambda b, h: (b, PF_ZB // d + h)),
            pl.BlockSpec((None, 3, DN_CONV, d), lambda b, h: (h, 0, 0, 0)),
            pl.BlockSpec((1, d), lambda b, h: (0, 0)),
        ],
        out_specs=pl.BlockSpec((rows, d), lambda b, h: (b, h)),
        out_shape=jax.ShapeDtypeStruct((bsz * seq, DN_WIDTH), jnp.bfloat16),
        scratch_shapes=[
            pltpu.VMEM((rows, d), f32), pltpu.VMEM((rows, d), f32), pltpu.VMEM((rows, d), f32),
            pltpu.VMEM((rows, d), f32), pltpu.VMEM((rows, d), f32),
            pltpu.VMEM((n_chunks * d, d), bf16), pltpu.VMEM((n_chunks * d, d), f32),
            pltpu.VMEM((rows, d), bf16), pltpu.VMEM((rows, d), f32),
            pltpu.VMEM((n_chunks * 8, d), f32),
            pltpu.VMEM((n_chunks // 2 * d, d), bf16),
            pltpu.VMEM((n_chunks // 2 * 8, d), f32),
        ],
        compiler_params=_cparams(2), name="deltanet",
    )(scal, pf, pf, pf, pf, pf, cw4, norm_w.astype(f32).reshape(1, d))


def _out_kernel(oa_ref, ob_ref, gma_ref, gmb_ref, x_ref, p_ref, wa_ref, wb_ref, wo_ref, wpg_ref, wp_ref,
                lng_ref, lnb_ref, o_ref):
    f32 = jnp.float32
    bf16 = jnp.bfloat16
    tm = o_ref.shape[0]
    sub = tm // OUT_SUBTILES
    parts = [slice(i * sub, (i + 1) * sub) for i in range(OUT_SUBTILES)]
    y_a = [jnp.dot(oa_ref[r, :], wa_ref[...], preferred_element_type=f32) for r in parts]
    y_b = [jnp.dot(ob_ref[r, :], wb_ref[...], preferred_element_type=f32) for r in parts]
    pw = [jnp.dot(p_ref[r, :].astype(bf16), wp_ref[...], preferred_element_type=f32) for r in parts]
    mix = [(jax.nn.sigmoid(gma_ref[r, :]) * ya + jax.nn.sigmoid(gmb_ref[r, :]) * yb).astype(bf16)
           for r, ya, yb in zip(parts, y_a, y_b)]
    h = [DEEPNORM_ALPHA * x_ref[r, :] + jnp.dot(mx, wo_ref[...], preferred_element_type=f32)
         for r, mx in zip(parts, mix)]
    gate = [jnp.dot(hh.astype(bf16), wpg_ref[...], preferred_element_type=f32) for hh in h]
    for r, hh, gt, pp in zip(parts, h, gate, pw):
        hh = hh + jax.nn.sigmoid(gt) * pp
        mu = jnp.mean(hh, axis=-1, keepdims=True)
        hc = hh - mu
        var = jnp.mean(hc * hc, axis=-1, keepdims=True)
        o_ref[r, :] = (hc * lax.rsqrt(var + 1e-5) * lng_ref[...] + lnb_ref[...]).astype(o_ref.dtype)


def _out_block(o_a, o_b, pf, x2, p2, wa, wb, wo, wpg, wp, ln_g, ln_b, tm):
    t = x2.shape[0]
    bf = jnp.bfloat16

    def full(shape):
        return pl.BlockSpec(shape, lambda i: (0, 0))

    return pl.pallas_call(
        _out_kernel,
        grid=(t // tm,),
        in_specs=[
            pl.BlockSpec((tm, NSA_WIDTH), lambda i: (i, 0)),
            pl.BlockSpec((tm, DN_WIDTH), lambda i: (i, 0)),
            pl.BlockSpec((tm, D_MODEL), lambda i: (i, PF_GM // D_MODEL)),
            pl.BlockSpec((tm, D_MODEL), lambda i: (i, PF_GM // D_MODEL + 1)),
            pl.BlockSpec((tm, D_MODEL), lambda i: (i, 0)),
            pl.BlockSpec((tm, PLE_DIM), lambda i: (i, 0)),
            full((NSA_WIDTH, D_MODEL)), full((DN_WIDTH, D_MODEL)), full((D_MODEL, D_MODEL)),
            full((D_MODEL, D_MODEL)), full((PLE_DIM, D_MODEL)), full((1, D_MODEL)), full((1, D_MODEL)),
        ],
        out_specs=pl.BlockSpec((tm, D_MODEL), lambda i: (i, 0)),
        out_shape=jax.ShapeDtypeStruct((t, D_MODEL), x2.dtype),
        compiler_params=_cparams(1), name="out_block",
    )(o_a, o_b, pf, pf, x2, p2, wa.astype(bf), wb.astype(bf), wo.astype(bf), wpg.astype(bf), wp.astype(bf),
      ln_g.astype(jnp.float32).reshape(1, D_MODEL), ln_b.astype(jnp.float32).reshape(1, D_MODEL))


def _layer(x, p, w_in, pos_k, pos_v, w1_k, w2_k, w1_v, w2_v, bias_tabs, conv_w, a_log, dt_bias, norm_w,
           w_a, w_b, w_o, w_ple, w_pg, ln_g, ln_b):
    bsz, seq, _ = x.shape
    t = bsz * seq
    x2 = x.reshape(t, D_MODEL)
    wb16, wf16 = _prep_w_in(w_in)
    pb, pf = _proj(x2, wb16, wf16, PROJ_TM if t % PROJ_TM == 0 else seq)

    pos2, w2p = _prep_compress_weights(pos_k, pos_v, w2_k, w2_v)
    kcvc = _compress(pf, pos2, w1_k, w1_v, w2p, bsz, seq)

    bias_sel, bias_win, bias_cmp = bias_tabs
    o_a = _nsa(pb, pf, kcvc, bias_sel, bias_win, bias_cmp, bsz, seq)
    o_b = _deltanet(pf, conv_w, a_log, dt_bias, norm_w, bsz, seq)
    out = _out_block(o_a, o_b, pf, x2, p.reshape(t, PLE_DIM), w_a, w_b, w_o, w_pg, w_ple, ln_g, ln_b,
                     OUT_TM if t % OUT_TM == 0 else seq)
    return out.reshape(bsz, seq, D_MODEL)


def kernel(x, p, w_in, cmp_pos_k, cmp_pos_v, cmp_w1_k, cmp_w2_k, cmp_w1_v, cmp_w2_v, rel_bias, dn_conv_w,
           dn_a_log, dn_dt_bias, dn_norm_w, w_branch_a, w_branch_b, w_out, w_ple, w_ple_gate, ln_g, ln_b):
    depth = w_in.shape[0]
    bias_tabs = _bias_tables(rel_bias, x.shape[1])
    for i in range(depth):
        x = _layer(x, p[i], w_in[i], cmp_pos_k[i], cmp_pos_v[i], cmp_w1_k[i], cmp_w2_k[i], cmp_w1_v[i],
                   cmp_w2_v[i], bias_tabs, dn_conv_w[i], dn_a_log[i], dn_dt_bias[i], dn_norm_w[i],
                   w_branch_a[i], w_branch_b[i], w_out[i], w_ple[i], w_ple_gate[i], ln_g[i], ln_b[i])
    return x
```

```python
import functools
import math

import numpy as np
import jax
import jax.numpy as jnp
from jax import lax
from jax.experimental import pallas as pl
from jax.experimental.pallas import tpu as pltpu

D_MODEL = 1024
PLE_DIM = 256
NSA_HEADS = 8
NSA_GROUPS = 2
NSA_HPG = NSA_HEADS // NSA_GROUPS
NSA_HEAD_DIM = 64
NSA_WIDTH = NSA_HEADS * NSA_HEAD_DIM
NSA_KV = NSA_GROUPS * NSA_HEAD_DIM
CMP_BLOCK = 32
CMP_STRIDE = 16
CMP_HIDDEN = 256
SEL_BLOCK = 64
SEL_TOPK = 8
WINDOW = 512
DN_HEADS = 4
DN_HEAD_DIM = 128
DN_WIDTH = DN_HEADS * DN_HEAD_DIM
DN_CONV = 4
DN_CHUNK = 64
NUM_BUCKETS = 32
REL_MAX_DIST = 1024
DEEPNORM_ALPHA = 2.0 ** 0.25
NEG = -1e30
FORCE = 1e6
LOG2E = 1.4426950408889634

LANES = 128
TILE = 128
SEL_LANE0 = 64
SEL_GROUP = 4
NSA_TILES_PER_STEP = 8
NSA_V_ROWS = NSA_HEAD_DIM + 16
VMEM_LIMIT = 56 * 1024 * 1024
PROJ_TM = 512
PROJ_TN = 1024
OUT_TM = 512
OUT_SUBTILES = 2
DN_INV_BLOCK = 16
DN_BATCHES = 2
DN_OUT_UNROLL = 32
DN_PAIR_UNROLL = 16
DN_DECAY_PARTS = 2
DN_PREP_UNROLL = 64

HIGHEST = lax.Precision.HIGHEST

PB_QP = 0
PB_KS = PB_QP + NSA_WIDTH
PB_KW = PB_KS + NSA_KV
PB_VS = PB_KW + NSA_KV
PB_VW = PB_VS + NSA_KV
PB_WIDTH = PB_VW + NSA_KV
PF_GM = 0
PF_QKVB = PF_GM + 2 * D_MODEL
PF_ZA = PF_QKVB + 3 * DN_WIDTH
PF_ZB = PF_ZA + NSA_WIDTH
PF_KC = PF_ZB + DN_WIDTH
PF_VC = PF_KC + NSA_KV
PF_SMALL = PF_VC + NSA_KV
PF_WIDTH = PF_SMALL + LANES
SMALL_BETA = 3 * NSA_HEADS
SMALL_A = SMALL_BETA + DN_HEADS


def _bucket_thresholds():
    max_exact = NUM_BUCKETS // 2
    span = NUM_BUCKETS - max_exact
    ratio = REL_MAX_DIST // max_exact
    thr = list(range(1, max_exact + 1))
    for k in range(1, span):
        n = max_exact
        while n ** span < max_exact ** span * ratio ** k:
            n += 1
        thr.append(n)
    return tuple(thr)


_THR = _bucket_thresholds()


def _cparams(n_axes):
    return pltpu.CompilerParams(dimension_semantics=("arbitrary",) * n_axes, vmem_limit_bytes=VMEM_LIMIT)


def _proj_kernel(x_ref, wb_ref, wf_ref, pb_ref, pf_ref):
    nt_dims = (((1,), (1,)), ((), ()))
    xb = x_ref[...].astype(jnp.bfloat16)
    pb_ref[...] = lax.dot_general(xb, wb_ref[...], nt_dims, preferred_element_type=jnp.float32).astype(pb_ref.dtype)
    n = pf_ref.shape[1]
    for c0 in range(0, n, PROJ_TN):
        c1 = min(c0 + PROJ_TN, n)
        pf_ref[:, c0:c1] = lax.dot_general(xb, wf_ref[c0:c1, :], nt_dims, preferred_element_type=jnp.float32)


def _proj(x2, wb, wf, tm):
    t, d = x2.shape
    resident = dict(pipeline_mode=pl.Buffered(1))
    return pl.pallas_call(
        _proj_kernel,
        grid=(t // tm,),
        in_specs=[pl.BlockSpec((tm, d), lambda i: (i, 0)),
                  pl.BlockSpec((PB_WIDTH, d), lambda i: (0, 0), **resident),
                  pl.BlockSpec((PF_WIDTH, d), lambda i: (0, 0), **resident)],
        out_specs=[pl.BlockSpec((tm, PB_WIDTH), lambda i: (i, 0)),
                   pl.BlockSpec((tm, PF_WIDTH), lambda i: (i, 0))],
        out_shape=[jax.ShapeDtypeStruct((t, PB_WIDTH), jnp.bfloat16),
                   jax.ShapeDtypeStruct((t, PF_WIDTH), jnp.float32)],
        compiler_params=_cparams(1), name="proj",
    )(x2, wb, wf)


def _prep_w_in(w):
    d = w.shape[0]
    wt = jnp.swapaxes(w, 0, 1)
    o = 0
    wq = wt[o:o + NSA_WIDTH]; o += NSA_WIDTH
    wkv = wt[o:o + 6 * NSA_KV]; o += 6 * NSA_KV
    wg = wt[o:o + 3 * NSA_HEADS]; o += 3 * NSA_HEADS
    wza = wt[o:o + NSA_WIDTH]; o += NSA_WIDTH
    wqkvb = wt[o:o + 3 * DN_WIDTH]; o += 3 * DN_WIDTH
    wbeta_a = wt[o:o + 2 * DN_HEADS]; o += 2 * DN_HEADS
    wzb = wt[o:o + DN_WIDTH]; o += DN_WIDTH
    wgm = wt[o:o + 2 * D_MODEL]
    wkcvc, wks, wvs, wkw, wvw = (wkv[0:2 * NSA_KV], wkv[2 * NSA_KV:3 * NSA_KV], wkv[3 * NSA_KV:4 * NSA_KV],
                                 wkv[4 * NSA_KV:5 * NSA_KV], wkv[5 * NSA_KV:6 * NSA_KV])
    wb = jnp.concatenate([wq * (NSA_HEAD_DIM ** -0.5 * LOG2E), wks, wkw, wvs, wvw], axis=0).astype(jnp.bfloat16)
    pad = jnp.zeros((LANES - 3 * NSA_HEADS - 2 * DN_HEADS, d), w.dtype)
    wf = jnp.concatenate([wgm, wqkvb, wza, wzb, wkcvc, wg, wbeta_a, pad], axis=0).astype(jnp.bfloat16)
    return wb, wf


def _bias_kernel(tab_ref, sel_ref, win_ref, cmp_ref, *, n_cmp):
    h = pl.program_id(0)

    def bucket_of(n):
        return sum(1 for t in _THR if n >= t)

    def lookup(n, lo, hi):
        b_lo, b_hi = bucket_of(max(lo, 0)), bucket_of(max(hi, 0))
        val = jnp.full(n.shape, tab_ref[b_lo, h], jnp.float32)
        for b in range(b_lo + 1, b_hi + 1):
            val = jnp.where(n >= _THR[b - 1], tab_ref[b, h], val)
        return val * LOG2E

    kj = lax.broadcasted_iota(jnp.int32, (TILE, TILE), 0)
    qi = lax.broadcasted_iota(jnp.int32, (TILE, TILE), 1)
    n_sel_tiles = sel_ref.shape[1] - 1
    n_win_tiles = win_ref.shape[1] - 1
    for dt in range(max(n_sel_tiles, n_win_tiles)):
        dist = dt * TILE + qi - kj
        v = lookup(jnp.maximum(dist, 0), dt * TILE - (TILE - 1), dt * TILE + (TILE - 1))
        if dt < n_sel_tiles:
            sel_ref[0, dt] = jnp.where(dist >= 0, v, NEG)
        if dt < n_win_tiles:
            win_ref[0, dt] = jnp.where((dist >= 0) & (dist < WINDOW), v, NEG)
    sel_ref[0, n_sel_tiles] = jnp.full((TILE, TILE), NEG, jnp.float32)
    win_ref[0, n_win_tiles] = jnp.full((TILE, TILE), NEG, jnp.float32)
    nc = cmp_ref.shape[2]
    band = 16
    for t in range(cmp_ref.shape[1]):
        for c0 in range(0, nc, band):
            c = c0 + lax.broadcasted_iota(jnp.int32, (band, TILE), 0)
            s = t * TILE + lax.broadcasted_iota(jnp.int32, (band, TILE), 1)
            dist = s - (c * CMP_STRIDE + CMP_BLOCK - 1)
            lo = t * TILE - ((c0 + band - 1) * CMP_STRIDE + CMP_BLOCK - 1)
            hi = t * TILE + TILE - 1 - (c0 * CMP_STRIDE + CMP_BLOCK - 1)
            cmp_ref[0, t, c0:c0 + band, :] = jnp.where((dist >= 0) & (c < n_cmp),
                                                       lookup(jnp.maximum(dist, 0), lo, hi), NEG)


def _bias_tables(rel_bias, seq):
    nq = seq // TILE
    nwin = WINDOW // TILE + 1
    nc = seq // CMP_STRIDE
    n_cmp = nc - CMP_BLOCK // CMP_STRIDE + 1
    return pl.pallas_call(
        functools.partial(_bias_kernel, n_cmp=n_cmp),
        grid=(NSA_HEADS,),
        in_specs=[pl.BlockSpec(memory_space=pltpu.SMEM)],
        out_specs=[pl.BlockSpec((1, nq + 1, TILE, TILE), lambda h: (h, 0, 0, 0)),
                   pl.BlockSpec((1, nwin + 1, TILE, TILE), lambda h: (h, 0, 0, 0)),
                   pl.BlockSpec((1, nq, nc, TILE), lambda h: (h, 0, 0, 0))],
        out_shape=[jax.ShapeDtypeStruct((NSA_HEADS, nq + 1, TILE, TILE), jnp.float32),
                   jax.ShapeDtypeStruct((NSA_HEADS, nwin + 1, TILE, TILE), jnp.float32),
                   jax.ShapeDtypeStruct((NSA_HEADS, nq, nc, TILE), jnp.float32)],
        compiler_params=_cparams(1), name="bias_tables",
    )(rel_bias.astype(jnp.float32))


def _gelu_tanh(x):
    return x * (0.5 * (1.0 + jnp.tanh(math.sqrt(2.0 / math.pi) * (x + 0.044715 * (x * x * x)))))


def _compress_kernel(xk_ref, xv_ref, pos_ref, w1k_ref, w1v_ref, w2_ref, o_ref, wbd):
    nc = o_ref.shape[2]
    half = CMP_BLOCK // 2
    dh = NSA_HEAD_DIM
    bf16 = jnp.bfloat16

    @pl.when(pl.program_id(0) == 0)
    def _():
        zero = jnp.zeros((dh, CMP_HIDDEN), bf16)
        for kv, w1_ref in enumerate((w1k_ref, w1v_ref)):
            for a in range(2):
                for l in range(half):
                    r0 = (a * half + l) * dh
                    wl = w1_ref[r0:r0 + dh, :].astype(bf16)
                    wbd[kv, a, l * NSA_KV:(l + 1) * NSA_KV, :] = jnp.concatenate(
                        [jnp.concatenate([wl, zero], axis=1), jnp.concatenate([zero, wl], axis=1)], axis=0)

    rs = [jnp.concatenate([x_ref[pl.ds(l, nc, stride=CMP_STRIDE), :] for l in range(CMP_STRIDE)], axis=1)
          for x_ref in (xk_ref, xv_ref)]
    a = [jnp.dot((r + pos_ref[kv, 0:1, :]).astype(bf16), wbd[kv, 0], preferred_element_type=jnp.float32)
         for kv, r in enumerate(rs)]
    b = [jnp.dot((r + pos_ref[kv, 1:2, :]).astype(bf16), wbd[kv, 1], preferred_element_type=jnp.float32)
         for kv, r in enumerate(rs)]
    for kv in range(2):
        hid = a[kv] + pltpu.roll(b[kv], nc - 1, 0)
        o_ref[0, kv] = jnp.dot(_gelu_tanh(hid).astype(bf16), w2_ref[kv], preferred_element_type=jnp.float32)


def _compress(pf, pos2, w1_k, w1_v, w2p, bsz, seq):
    nc = seq // CMP_STRIDE
    width = CMP_STRIDE * NSA_KV
    hid = NSA_GROUPS * CMP_HIDDEN
    assert CMP_BLOCK == 2 * CMP_STRIDE and PF_VC == PF_KC + NSA_KV
    return pl.pallas_call(
        _compress_kernel,
        grid=(bsz,),
        in_specs=[pl.BlockSpec((seq, NSA_KV), lambda b: (b, PF_KC // NSA_KV)),
                  pl.BlockSpec((seq, NSA_KV), lambda b: (b, PF_VC // NSA_KV)),
                  pl.BlockSpec((2, 2, width), lambda b: (0, 0, 0)),
                  pl.BlockSpec(w1_k.shape, lambda b: (0, 0)),
                  pl.BlockSpec(w1_v.shape, lambda b: (0, 0)),
                  pl.BlockSpec((2, hid, LANES), lambda b: (0, 0, 0))],
        out_specs=pl.BlockSpec((1, 2, nc, LANES), lambda b: (b, 0, 0, 0)),
        out_shape=jax.ShapeDtypeStruct((bsz, 2, nc, LANES), jnp.float32),
        scratch_shapes=[pltpu.VMEM((2, 2, width, hid), jnp.bfloat16)],
        compiler_params=_cparams(1), name="compress",
    )(pf, pf, pos2, w1_k, w1_v, w2p)


def _prep_compress_weights(pos_k, pos_v, w2_k, w2_v):
    eye = jnp.eye(NSA_GROUPS, dtype=jnp.float32)
    half = CMP_BLOCK // 2

    def w2_both(w2):
        return jnp.einsum('jd,gh->gjhd', w2, eye).reshape(NSA_GROUPS * CMP_HIDDEN, NSA_GROUPS * NSA_HEAD_DIM)

    def pos_both(pos):
        p = pos.reshape(2, half, 1, NSA_HEAD_DIM)
        return jnp.broadcast_to(p, (2, half, NSA_GROUPS, NSA_HEAD_DIM)).reshape(2, half * NSA_KV)

    pos2 = jnp.stack([pos_both(pos_k), pos_both(pos_v)]).astype(jnp.float32)
    w2p = jnp.stack([w2_both(w2_k), w2_both(w2_v)]).astype(jnp.bfloat16)
    return pos2, w2p


def _pair_lanes(x, g):
    sw = pltpu.roll(x, LANES // 2, 1)
    lane = lax.broadcasted_iota(jnp.int32, x.shape, 1)
    own = (lane < LANES // 2) == (g == 0)
    return jnp.where(own, x, sw)


def _group_rows_t(x, g):
    xt = x.T
    half = LANES // 2
    return jnp.where(g == 0, xt[0:half, :], xt[half:LANES, :])


def _nsa_step_kernel(q_ref, qn_ref, ks_ref, kw_ref, vs_ref, vw_ref, kcvc_ref, bc_ref, bsel_ref, bwin_ref,
                     small_ref, za_ref, o_ref,
                     ksa, kwp, vst, vwt, kcs, vct, qa_s, qp_s, ocmp_s, sbuf, *, n_cmp, n_sel, n_top):
    g = pl.program_id(0)
    step_id = pl.program_id(2)
    tq = TILE
    dh = NSA_HEAD_DIM
    seq = ks_ref.shape[0]
    nq = seq // TILE
    nc = kcvc_ref.shape[2]
    n_wt = bwin_ref.shape[1] - 1
    heads = range(NSA_HPG)
    f32 = jnp.float32
    bf16 = jnp.bfloat16
    nt_dims = (((1,), (1,)), ((), ()))
    gk = SEL_GROUP * TILE

    def select_tile(src_ref, r0, tile, slot):
        q32 = src_ref[r0:r0 + tq, :].astype(f32)
        low_half = lax.broadcasted_iota(jnp.int32, (tq, LANES), 1) < NSA_HEAD_DIM
        qh = []
        for hh in heads:
            blk = q32[:, (hh // 2) * LANES:(hh // 2 + 1) * LANES]
            if hh % 2:
                blk = pltpu.roll(blk, LANES // 2, 1)
            qh.append(jnp.where(low_half, blk, 0.0).astype(bf16))
        for hh in heads:
            qp_s[slot, hh * tq:(hh + 1) * tq, :] = qh[hh]
        q_all = qp_s[slot]
        s_c = lax.dot_general(kcs[...], q_all, nt_dims, preferred_element_type=f32)
        bias_c = jnp.concatenate([bc_ref[hh, tile] for hh in heads], axis=1)
        valid = bias_c > 0.5 * NEG
        s_c = s_c + bias_c
        e = jnp.where(valid, jnp.exp2(s_c - jnp.max(s_c, axis=0, keepdims=True)), 0.0)
        den = jnp.maximum(jnp.sum(e, axis=0, keepdims=True), 1e-30)
        p_c = e * (1.0 / den)
        psum = sum(p_c[:, hh * tq:(hh + 1) * tq] for hh in heads)
        sj = lax.broadcasted_iota(jnp.int32, (n_sel, nc), 0)
        ci = lax.broadcasted_iota(jnp.int32, (n_sel, nc), 1)
        overlap = ((ci * CMP_STRIDE < (sj + 1) * SEL_BLOCK) & (ci * CMP_STRIDE + CMP_BLOCK > sj * SEL_BLOCK)
                   & (ci < n_cmp)).astype(bf16)
        imp_t = sum(jnp.dot(overlap, part, preferred_element_type=f32)
                    for part in _split_bf16(psum, 3))
        ocmp_s[slot] = jnp.dot(vct[...], p_c.astype(bf16), preferred_element_type=f32)
        blk = lax.broadcasted_iota(jnp.int32, (n_sel, tq), 0)
        cur = (tile * tq + lax.broadcasted_iota(jnp.int32, (n_sel, tq), 1)) >> 6
        forced = (blk == 0) | (blk == cur) | (blk == cur - 1)
        imp_t = jnp.where(forced, FORCE, jnp.where(blk > cur, -FORCE, imp_t))
        rank = jnp.zeros((n_sel, tq), jnp.int32)
        for j in range(n_sel):
            other = imp_t[j:j + 1, :]
            ahead = (other > imp_t) | ((other == imp_t) & (blk > j))
            rank = rank + ahead.astype(jnp.int32)
        selb = jnp.where(rank < n_top, 0.0, NEG)
        pieces = [jnp.zeros((SEL_LANE0, tq), f32), selb]
        if LANES - SEL_LANE0 - n_sel > 0:
            pieces.append(jnp.zeros((LANES - SEL_LANE0 - n_sel, tq), f32))
        selb_r = jnp.concatenate(pieces, axis=0).T.astype(bf16)
        for hh in heads:
            qa_s[slot, hh * tq:(hh + 1) * tq, :] = qh[hh] + selb_r

    @pl.when(step_id == 0)
    def _():
        row = lax.broadcasted_iota(jnp.int32, (seq, LANES), 0)
        lane = lax.broadcasted_iota(jnp.int32, (seq, LANES), 1)
        onehot = (lane - SEL_LANE0) == (row >> 6)
        own = lane < NSA_HEAD_DIM
        ks_g = jnp.where(own, _pair_lanes(ks_ref[...].astype(f32), g), 0.0)
        ksa[...] = jnp.where(onehot, 1.0, ks_g).astype(bf16)
        pad = (n_wt - 1) * TILE
        kwp[0:pad, :] = jnp.zeros((pad, LANES), bf16)
        kwp[pad:pad + seq, :] = jnp.where(own, _pair_lanes(kw_ref[...].astype(f32), g), 0.0).astype(bf16)
        for kt in range(n_wt - 1):
            vwt[kt] = jnp.zeros(vwt.shape[1:], bf16)
        extra = vst.shape[1] - dh
        ones_row = (lax.broadcasted_iota(jnp.int32, (extra, TILE), 0) == 0).astype(bf16)
        for kt in range(nq):
            rows = slice(kt * TILE, (kt + 1) * TILE)
            vst[kt, 0:dh, :] = _group_rows_t(vs_ref[rows, :].astype(f32), g).astype(bf16)
            vst[kt, dh:dh + extra, :] = ones_row
            vwt[kt + n_wt - 1, 0:dh, :] = _group_rows_t(vw_ref[rows, :].astype(f32), g).astype(bf16)
            vwt[kt + n_wt - 1, dh:dh + extra, :] = ones_row
        kcs[...] = _pair_lanes(kcvc_ref[0, 0], g).astype(bf16)
        for ct in range(nc // TILE):
            rows = slice(ct * TILE, (ct + 1) * TILE)
            vct[:, rows] = _group_rows_t(kcvc_ref[0, 1, rows, :], g).astype(bf16)
        select_tile(q_ref, 0, 0, 0)

    tps = NSA_TILES_PER_STEP

    def one_tile(n_groups, slot):
        qt = step_id * tps + slot
        rows = slice(slot * tq, (slot + 1) * tq)
        q_all = qp_s[slot]
        q_aug = qa_s[slot]
        kw_rows = kwp[pl.ds(pl.multiple_of(qt * TILE, TILE), n_wt * TILE), :]
        s_w = lax.dot_general(kw_rows, q_all, nt_dims, preferred_element_type=f32)

        def sel_scores(gi):
            s = lax.dot_general(ksa[gi * gk:(gi + 1) * gk, :], q_aug, nt_dims, preferred_element_type=f32)
            tiles = []
            for t in range(SEL_GROUP):
                dt = qt - (gi * SEL_GROUP + t)
                idx = jnp.where(dt >= 0, dt, nq)
                tiles.append(jnp.concatenate([bsel_ref[hh, idx] for hh in heads], axis=1))
            sbuf[gi % 2] = s + jnp.concatenate(tiles, axis=0)

        sel_scores(0)
        if slot + 1 < tps:
            select_tile(q_ref, (slot + 1) * tq, qt + 1, slot + 1)
        else:
            select_tile(qn_ref, 0, jnp.minimum(qt + 1, nq - 1), 0)

        w_tiles = []
        for t in range(n_wt):
            dt = n_wt - 1 - t
            idx = jnp.where(qt >= dt, dt, n_wt)
            w_tiles.append(jnp.concatenate([bwin_ref[hh, idx] for hh in heads], axis=1))
        s_w = s_w + jnp.concatenate(w_tiles, axis=0)
        p_w = jnp.exp2(s_w - jnp.max(s_w, axis=0, keepdims=True))
        v_w = jnp.concatenate([vwt[qt + t] for t in range(n_wt)], axis=1)
        acc_w = jnp.dot(v_w, p_w.astype(bf16), preferred_element_type=f32)
        o_win = acc_w[0:dh, :] * (1.0 / acc_w[dh:dh + 1, :])

        m = jnp.full((1, NSA_HPG * tq), NEG, f32)
        acc = jnp.zeros((vst.shape[1], NSA_HPG * tq), f32)
        for gi in range(n_groups):
            if gi + 1 < n_groups:
                sel_scores(gi + 1)
            s = sbuf[gi % 2]
            m_new = jnp.maximum(m, jnp.max(s, axis=0, keepdims=True))
            alpha = jnp.exp2(m - m_new)
            p = jnp.exp2(s - m_new)
            v_t = jnp.concatenate([vst[gi * SEL_GROUP + t] for t in range(SEL_GROUP)], axis=1)
            acc = alpha * acc + jnp.dot(v_t, p.astype(bf16), preferred_element_type=f32)
            m = m_new
        o_sel = acc[0:dh, :] * (1.0 / acc[dh:dh + 1, :])
        o_cmp = ocmp_s[slot]

        gates_t = jax.nn.sigmoid(small_ref[rows, :]).T
        mixed = []
        for hh in heads:
            def gate(br):
                r = br * NSA_HEADS + hh
                return jnp.where(g == 0, gates_t[r:r + 1, :], gates_t[r + NSA_HPG:r + NSA_HPG + 1, :])
            cols = slice(hh * tq, (hh + 1) * tq)
            mixed.append(gate(0) * o_cmp[:, cols] + gate(1) * o_sel[:, cols] + gate(2) * o_win[:, cols])
        o = jnp.concatenate([jnp.concatenate(mixed[2 * j:2 * j + 2], axis=0).T for j in range(NSA_HPG // 2)],
                            axis=1)
        z = za_ref[rows, :]
        o_ref[rows, :] = (o * (z * jax.nn.sigmoid(z))).astype(o_ref.dtype)

    def step(first_tile):
        for slot in range(tps):
            one_tile((first_tile + slot) // SEL_GROUP + 1, slot)

    variants = {}
    for s in range(nq // tps):
        key = tuple((s * tps + slot) // SEL_GROUP for slot in range(tps))
        variants.setdefault(key, []).append(s)
    for steps in variants.values():
        cond = functools.reduce(jnp.logical_or, [step_id == s for s in steps])
        pl.when(cond)(functools.partial(step, steps[0] * tps))


def _nsa(pb, pf, kcvc, bias_sel, bias_win, bias_cmp, bsz, seq):
    nq = seq // TILE
    nc = seq // CMP_STRIDE
    n_cmp = nc - CMP_BLOCK // CMP_STRIDE + 1
    n_sel = seq // SEL_BLOCK
    n_top = min(SEL_TOPK, n_sel)
    nwin = bias_win.shape[1]
    tps = NSA_TILES_PER_STEP
    assert nq % SEL_GROUP == 0 and nc % TILE == 0 and bias_sel.shape[1] == nq + 1
    assert tps >= 2 and nq % tps == 0
    ns = nq // tps
    rows = tps * TILE
    gw = NSA_HPG * NSA_HEAD_DIM
    kern = functools.partial(_nsa_step_kernel, n_cmp=n_cmp, n_sel=n_sel, n_top=n_top)
    return pl.pallas_call(
        kern,
        grid=(NSA_GROUPS, bsz, ns),
        in_specs=[
            pl.BlockSpec((rows, gw), lambda g, b, t: (b * ns + t, PB_QP // gw + g)),
            pl.BlockSpec((TILE, gw), lambda g, b, t: (b * nq + jnp.minimum((t + 1) * tps, nq - 1),
                                                      PB_QP // gw + g)),
            pl.BlockSpec((seq, LANES), lambda g, b, t: (b, PB_KS // LANES)),
            pl.BlockSpec((seq, LANES), lambda g, b, t: (b, PB_KW // LANES)),
            pl.BlockSpec((seq, LANES), lambda g, b, t: (b, PB_VS // LANES)),
            pl.BlockSpec((seq, LANES), lambda g, b, t: (b, PB_VW // LANES)),
            pl.BlockSpec((1, 2, nc, LANES), lambda g, b, t: (b, 0, 0, 0)),
            pl.BlockSpec((NSA_HPG, nq, nc, TILE), lambda g, b, t: (g, 0, 0, 0)),
            pl.BlockSpec((NSA_HPG, nq + 1, TILE, TILE), lambda g, b, t: (g, 0, 0, 0)),
            pl.BlockSpec((NSA_HPG, nwin, TILE, TILE), lambda g, b, t: (g, 0, 0, 0)),
            pl.BlockSpec((rows, LANES), lambda g, b, t: (b * ns + t, PF_SMALL // LANES)),
            pl.BlockSpec((rows, gw), lambda g, b, t: (b * ns + t, PF_ZA // gw + g)),
        ],
        out_specs=pl.BlockSpec((rows, gw), lambda g, b, t: (b * ns + t, g)),
        out_shape=jax.ShapeDtypeStruct((bsz * seq, NSA_WIDTH), jnp.bfloat16),
        scratch_shapes=[
            pltpu.VMEM((seq, LANES), jnp.bfloat16),
            pltpu.VMEM((seq + (nwin - 2) * TILE, LANES), jnp.bfloat16),
            pltpu.VMEM((nq, NSA_V_ROWS, TILE), jnp.bfloat16),
            pltpu.VMEM((nq + nwin - 2, NSA_V_ROWS, TILE), jnp.bfloat16),
            pltpu.VMEM((nc, LANES), jnp.bfloat16),
            pltpu.VMEM((NSA_HEAD_DIM, nc), jnp.bfloat16),
            pltpu.VMEM((tps, NSA_HPG * TILE, LANES), jnp.bfloat16),
            pltpu.VMEM((tps, NSA_HPG * TILE, LANES), jnp.bfloat16),
            pltpu.VMEM((tps, NSA_HEAD_DIM, NSA_HPG * TILE), jnp.float32),
            pltpu.VMEM((2, SEL_GROUP * TILE, NSA_HPG * TILE), jnp.float32),
        ],
        compiler_params=_cparams(3), name="nsa",
    )(pb, pb, pb, pb, pb, pb, kcvc, bias_cmp, bias_sel, bias_win, pf, pf)


def _split_bf16(a, n):
    parts = []
    for _ in range(n - 1):
        hi = a.astype(jnp.bfloat16)
        parts.append(hi)
        a = a - hi.astype(jnp.float32)
    parts.append(a.astype(jnp.bfloat16))
    return parts


def _dot3(a, b):
    ah, al = _split_bf16(a, 2)
    bh, bl = _split_bf16(b, 2)
    f32 = jnp.float32
    return (jnp.dot(ah, bh, preferred_element_type=f32) + jnp.dot(al, bh, preferred_element_type=f32)
            + jnp.dot(ah, bl, preferred_element_type=f32))


def _softplus(x):
    return jnp.maximum(x, 0.0) + jnp.log(1.0 + jnp.exp(-jnp.abs(x)))


def _dn_kernel(scal_ref, q_ref, k_ref, v_ref, small_ref, z_ref, cw_ref, nw_ref, o_ref,
               qn, kn, vn, bet, gl, mm, nn, qq, oo, dd, mm2, dd2, *, seq):
    h = pl.program_id(1)
    n_rows = q_ref.shape[0]
    nb = n_rows // seq
    c = DN_CHUNK
    n_chunks = seq // c
    d = DN_HEAD_DIM

    head = 8

    def conv_body(x_ref, which):
        n = n_rows - head
        y = x_ref[pl.ds(head, n), :] * cw_ref[which, DN_CONV - 1:DN_CONV, :]
        for j in range(DN_CONV - 1):
            y = y + x_ref[pl.ds(head - (DN_CONV - 1 - j), n), :] * cw_ref[which, j:j + 1, :]
        return y

    def conv_head(x_ref, which, r0):
        x = x_ref[pl.ds(r0, head), :]
        rowi = lax.broadcasted_iota(jnp.int32, (head, d), 0)
        y = x * cw_ref[which, DN_CONV - 1:DN_CONV, :]
        for j in range(DN_CONV - 1):
            sh = DN_CONV - 1 - j
            y = y + jnp.where(rowi >= sh, pltpu.roll(x, sh, 0), 0.0) * cw_ref[which, j:j + 1, :]
        return y

    def silu(y):
        return y * jax.nn.sigmoid(y)

    def l2n(t):
        return t * lax.rsqrt(jnp.sum(t * t, axis=-1, keepdims=True) + 1e-6)

    finish = (lambda y: l2n(silu(y)) * (d ** -0.5), lambda y: l2n(silu(y)), silu)
    for which, (x_ref, dst) in enumerate(((q_ref, qn), (k_ref, kn), (v_ref, vn))):
        dst[pl.ds(head, n_rows - head), :] = finish[which](conv_body(x_ref, which))
        for b in range(nb):
            dst[pl.ds(b * seq, head), :] = finish[which](conv_head(x_ref, which, b * seq))
    small = small_ref[...]
    lane = lax.broadcasted_iota(jnp.int32, small.shape, 1)
    beta_in = jnp.sum(jnp.where(lane == SMALL_BETA + h, small, 0.0), axis=-1, keepdims=True)
    a_in = jnp.sum(jnp.where(lane == SMALL_A + h, small, 0.0), axis=-1, keepdims=True)
    bet[...] = jnp.broadcast_to(jax.nn.sigmoid(beta_in), (n_rows, d))
    gl[...] = jnp.broadcast_to(-jnp.exp(scal_ref[0, h]) * _softplus(a_in + scal_ref[1, h]), (n_rows, d))

    f32 = jnp.float32
    bf16 = jnp.bfloat16
    nt_dims = (((1,), (1,)), ((), ()))
    ri = lax.broadcasted_iota(jnp.int32, (c, 2 * c), 0)
    lane2 = lax.broadcasted_iota(jnp.int32, (c, 2 * c), 1)
    first = lane2 < c
    cj = lane2 & (c - 1)
    incl = ri >= cj
    strict = ri > cj
    tril_b = (lax.broadcasted_iota(jnp.int32, (c, c), 0) >= lax.broadcasted_iota(jnp.int32, (c, c), 1)).astype(bf16)
    bs = DN_INV_BLOCK
    sh = bs.bit_length() - 1
    same_diag = (ri >> sh) == (cj >> sh)
    level_masks = []
    while (1 << sh) < c:
        level_masks.append(((ri >> (sh + 1)) == (cj >> (sh + 1))) & ((ri >> sh) > (cj >> sh)))
        sh += 1
    bd_r = lax.broadcasted_iota(jnp.int32, (2 * c, 2 * c), 0)
    bd_c = lax.broadcasted_iota(jnp.int32, (2 * c, 2 * c), 1)
    block_diag = (bd_r >= c) == (bd_c >= c)

    def dot_pair(a, b):
        bb = b.astype(bf16)
        bb = jnp.where(block_diag, jnp.concatenate([bb, bb], axis=0), jnp.zeros((), bf16))
        return jnp.dot(a.astype(bf16), bb, preferred_element_type=f32)

    def stack_diag(x0, x1):
        z = jnp.zeros_like(x0)
        return jnp.concatenate([jnp.concatenate([x0, z], axis=1), jnp.concatenate([z, x1], axis=1)], axis=0)

    def chunk_prep(it, carry):
        ids = [it * DN_PREP_UNROLL + cc for cc in range(DN_PREP_UNROLL)]
        rows = [pl.ds(pl.multiple_of(i * c, c), c) for i in ids]
        pairs = range(0, DN_PREP_UNROLL, 2)
        ks = [kn[r, :] for r in rows]
        betas = [bet[r, :] for r in rows]
        gcbs = [sum(jnp.dot(tril_b, part, preferred_element_type=f32)
                    for part in _split_bf16(gl[r, :], DN_DECAY_PARTS)) for r in rows]
        kbs = [k * beta for k, beta in zip(ks, betas)]
        k2ts = [jnp.concatenate([ks[j], ks[j + 1]], axis=0).T.astype(bf16) for j in pairs]

        def against_pair_keys(xs):
            return [jnp.where(first,
                              jnp.dot(xs[j].astype(bf16), k2t, preferred_element_type=f32),
                              jnp.dot(xs[j + 1].astype(bf16), k2t, preferred_element_type=f32))
                    for j, k2t in zip(pairs, k2ts)]

        a_kks = against_pair_keys(kbs)
        decays = []
        for j in pairs:
            gc_col = jnp.where(first, gcbs[j], gcbs[j + 1])
            gc_row = jnp.concatenate([gcbs[j], gcbs[j + 1]], axis=0).T[0:c, :]
            decays.append(jnp.where(incl, jnp.exp(jnp.where(incl, gc_col - gc_row, 0.0)), 0.0))
        lows = [jnp.where(strict, a * dec, 0.0) for a, dec in zip(a_kks, decays)]
        pws = [jnp.where(same_diag, -low, 0.0) for low in lows]
        es = list(pws)
        for _ in range(max(1, (bs - 1).bit_length()) - 1):
            pws = [dot_pair(pw, pw) for pw in pws]
            es = [e + pw + dot_pair(e, pw) for e, pw in zip(es, pws)]
        for below in level_masks:
            offs = [jnp.where(below, low, 0.0) for low in lows]
            xs = [off + dot_pair(e, off) for e, off in zip(es, offs)]
            es = [e - (x + dot_pair(x, e)) for e, x in zip(es, xs)]
        egcs = [jnp.exp(gcb) for gcb in gcbs]
        rhss = [jnp.concatenate([vn[r, :] * beta, kb * egc], axis=1)
                for r, beta, kb, egc in zip(rows, betas, kbs, egcs)]
        uwbs = []
        for e, j in zip(es, pairs):
            e_hi, e_lo = _split_bf16(e, 2)
            r2 = stack_diag(rhss[j], rhss[j + 1]).astype(bf16)
            er = jnp.dot(e_hi, r2, preferred_element_type=f32) + jnp.dot(e_lo, r2, preferred_element_type=f32)
            uwbs.append((rhss[j] + er[:, 0:2 * d]).astype(bf16))
            uwbs.append((rhss[j + 1] + er[:, 2 * d:4 * d]).astype(bf16))
        qs = [qn[r, :] for r in rows]
        a_qks = [a * dec for a, dec in zip(against_pair_keys(qs), decays)]
        g_lasts = [gcb[c - 1:c, :] for gcb in gcbs]
        kdec_ts = []
        for k, gcb, g_last in zip(ks, gcbs, g_lasts):
            kdec = k * jnp.exp(g_last - gcb)
            kdec_ts.append(jnp.concatenate([kdec, jnp.zeros_like(kdec)], axis=0).T[:, 0:c].astype(bf16))
        nms = [jnp.dot(kt, uwb, preferred_element_type=f32) for kt, uwb in zip(kdec_ts, uwbs)]
        oq2s = [jnp.dot(a.astype(bf16), stack_diag(uwbs[j], uwbs[j + 1]), preferred_element_type=f32)
                for a, j in zip(a_qks, pairs)]
        oqs = []
        for oq2 in oq2s:
            oqs += [oq2[:, 0:2 * d], oq2[:, 2 * d:4 * d]]
        for i, r, nm, oq, q, egc, g_last in zip(ids, rows, nms, oqs, qs, egcs, g_lasts):
            m0 = pl.ds(pl.multiple_of(i * d, d), d)
            nn[m0, :] = nm[:, 0:d]
            mm[m0, :] = nm[:, d:2 * d].astype(bf16)
            oo[r, :] = oq[:, 0:d]
            qq[r, :] = (q * egc - oq[:, d:2 * d]).astype(bf16)
            dd[pl.ds(pl.multiple_of(i * 8, 8), 8), :] = jnp.broadcast_to(jnp.exp(g_last), (8, d))
        return carry

    lax.fori_loop(0, nb * n_chunks // DN_PREP_UNROLL, chunk_prep, 0)

    nn2, ss = kn, vn

    def chunk_pair(it, carry):
        ids = [it * DN_PAIR_UNROLL + cc for cc in range(DN_PAIR_UNROLL)]
        blk1 = [pl.ds(pl.multiple_of(2 * p * d, d), d) for p in ids]
        blk2 = [pl.ds(pl.multiple_of((2 * p + 1) * d, d), d) for p in ids]
        d1s = [dd[pl.ds(pl.multiple_of(2 * p * 8, 8), 1), :] for p in ids]
        d2s = [dd[pl.ds(pl.multiple_of((2 * p + 1) * 8, 8), 1), :] for p in ids]
        m1s = [mm[b, :] for b in blk1]
        m2s = [mm[b, :] for b in blk2]
        n1s = [nn[b, :] for b in blk1]
        xs = [jnp.dot(m2, jnp.concatenate([m1, n1.astype(bf16)], axis=1), preferred_element_type=f32)
              for m1, m2, n1 in zip(m1s, m2s, n1s)]
        for p, b2, d1, d2, m1, m2, n1, x in zip(ids, blk2, d1s, d2s, m1s, m2s, n1s, xs):
            pblk = pl.ds(pl.multiple_of(p * d, d), d)
            mm2[pblk, :] = (d2 * m1.astype(f32) + d1 * m2.astype(f32) - x[:, 0:d]).astype(bf16)
            nn2[pblk, :] = d2 * n1 - x[:, d:2 * d] + nn[b2, :]
            dd2[pl.ds(pl.multiple_of(p * 8, 8), 8), :] = jnp.broadcast_to(d1 * d2, (8, d))
        return carry

    n_pairs = n_chunks // 2
    lax.fori_loop(0, nb * n_pairs // DN_PAIR_UNROLL, chunk_pair, 0)

    def pair_scan(i, states):
        ids = [b * n_pairs + i for b in range(nb)]
        blocks = [pl.ds(pl.multiple_of(j * d, d), d) for j in ids]
        for blk, s in zip(blocks, states):
            ss[blk, :] = s
        prods = [jnp.dot(mm2[blk, :], s.astype(bf16), preferred_element_type=f32) for blk, s in zip(blocks, states)]
        return tuple(s * dd2[pl.ds(pl.multiple_of(j * 8, 8), 1), :] - pr + nn2[blk, :]
                     for s, j, blk, pr in zip(states, ids, blocks, prods))

    lax.fori_loop(0, n_pairs, pair_scan, tuple(jnp.zeros((d, d), f32) for _ in range(nb)))

    nw = nw_ref[...]

    def chunk_out(it, carry):
        pids = [it * (DN_OUT_UNROLL // 2) + cc for cc in range(DN_OUT_UNROLL // 2)]
        s0 = [ss[pl.ds(pl.multiple_of(p * d, d), d), :] for p in pids]
        s0b = [s.astype(bf16) for s in s0]
        s1b = [(s * dd[pl.ds(pl.multiple_of(2 * p * 8, 8), 1), :]
                - jnp.dot(mm[pl.ds(pl.multiple_of(2 * p * d, d), d), :], sb, preferred_element_type=f32)
                + nn[pl.ds(pl.multiple_of(2 * p * d, d), d), :]).astype(bf16)
               for p, s, sb in zip(pids, s0, s0b)]
        rows, states = [], []
        for p, sb0, sb1 in zip(pids, s0b, s1b):
            rows += [pl.ds(pl.multiple_of(2 * p * c, c), c), pl.ds(pl.multiple_of((2 * p + 1) * c, c), c)]
            states += [sb0, sb1]
        outs = [jnp.dot(qq[r, :], sb, preferred_element_type=f32) + oo[r, :] for r, sb in zip(rows, states)]
        for r, o in zip(rows, outs):
            o = o * lax.rsqrt(jnp.mean(o * o, axis=-1, keepdims=True) + 1e-6) * nw
            z = z_ref[r, :]
            o_ref[r, :] = (o * (z * jax.nn.sigmoid(z))).astype(o_ref.dtype)
        return carry

    lax.fori_loop(0, nb * n_chunks // DN_OUT_UNROLL, chunk_out, 0)


def _deltanet(pf, conv_w, a_log, dt_bias, norm_w, bsz, seq):
    d = DN_HEAD_DIM
    nb = DN_BATCHES if bsz % DN_BATCHES == 0 else 1
    rows = nb * seq
    n_chunks = rows // DN_CHUNK
    qkv0 = PF_QKVB // d
    scal = jnp.stack([a_log, dt_bias]).astype(jnp.float32)
    f32 = jnp.float32
    bf16 = jnp.bfloat16
    cw4 = conv_w.astype(f32).reshape(DN_CONV, 3, DN_HEADS, d).transpose(2, 1, 0, 3)
    assert 2 * DN_CHUNK == d and n_chunks % DN_PREP_UNROLL == 0 and n_chunks % DN_OUT_UNROLL == 0
    assert (seq // DN_CHUNK) % 2 == 0 and (n_chunks // 2) % DN_PAIR_UNROLL == 0 and DN_OUT_UNROLL % 2 == 0
    return pl.pallas_call(
        functools.partial(_dn_kernel, seq=seq),
        grid=(bsz // nb, DN_HEADS),
        in_specs=[
            pl.BlockSpec(memory_space=pltpu.SMEM),
            pl.BlockSpec((rows, d), lambda b, h: (b, qkv0 + h)),
            pl.BlockSpec((rows, d), lambda b, h: (b, qkv0 + DN_HEADS + h)),
            pl.BlockSpec((rows, d), lambda b, h: (b, qkv0 + 2 * DN_HEADS + h)),
            pl.BlockSpec((rows, LANES), lambda b, h: (b, PF_SMALL // LANES)),
            pl.BlockSpec((rows, d), lambda b, h: (b, PF_ZB // d + h)),
            pl.BlockSpec((None, 3, DN_CONV, d), lambda b, h: (h, 0, 0, 0)),
            pl.BlockSpec((1, d), lambda b, h: (0, 0)),
        ],
        out_specs=pl.BlockSpec((rows, d), lambda b, h: (b, h)),
        out_shape=jax.ShapeDtypeStruct((bsz * seq, DN_WIDTH), jnp.bfloat16),
        scratch_shapes=[
            pltpu.VMEM((rows, d), f32), pltpu.VMEM((rows, d), f32), pltpu.VMEM((rows, d), f32),
            pltpu.VMEM((rows, d), f32), pltpu.VMEM((rows, d), f32),
            pltpu.VMEM((n_chunks * d, d), bf16), pltpu.VMEM((n_chunks * d, d), f32),
            pltpu.VMEM((rows, d), bf16), pltpu.VMEM((rows, d), f32),
            pltpu.VMEM((n_chunks * 8, d), f32),
            pltpu.VMEM((n_chunks // 2 * d, d), bf16),
            pltpu.VMEM((n_chunks // 2 * 8, d), f32),
        ],
        compiler_params=_cparams(2), name="deltanet",
    )(scal, pf, pf, pf, pf, pf, cw4, norm_w.astype(f32).reshape(1, d))


def _out_kernel(oa_ref, ob_ref, gma_ref, gmb_ref, x_ref, p_ref, wa_ref, wb_ref, wo_ref, wpg_ref, wp_ref,
                lng_ref, lnb_ref, o_ref):
    f32 = jnp.float32
    bf16 = jnp.bfloat16
    tm = o_ref.shape[0]
    sub = tm // OUT_SUBTILES
    parts = [slice(i * sub, (i + 1) * sub) for i in range(OUT_SUBTILES)]
    y_a = [jnp.dot(oa_ref[r, :], wa_ref[...], preferred_element_type=f32) for r in parts]
    y_b = [jnp.dot(ob_ref[r, :], wb_ref[...], preferred_element_type=f32) for r in parts]
    pw = [jnp.dot(p_ref[r, :].astype(bf16), wp_ref[...], preferred_element_type=f32) for r in parts]
    mix = [(jax.nn.sigmoid(gma_ref[r, :]) * ya + jax.nn.sigmoid(gmb_ref[r, :]) * yb).astype(bf16)
           for r, ya, yb in zip(parts, y_a, y_b)]
    h = [DEEPNORM_ALPHA * x_ref[r, :] + jnp.dot(mx, wo_ref[...], preferred_element_type=f32)
         for r, mx in zip(parts, mix)]
    gate = [jnp.dot(hh.astype(bf16), wpg_ref[...], preferred_element_type=f32) for hh in h]
    for r, hh, gt, pp in zip(parts, h, gate, pw):
        hh = hh + jax.nn.sigmoid(gt) * pp
        mu = jnp.mean(hh, axis=-1, keepdims=True)
        hc = hh - mu
        var = jnp.mean(hc * hc, axis=-1, keepdims=True)
        o_ref[r, :] = (hc * lax.rsqrt(var + 1e-5) * lng_ref[...] + lnb_ref[...]).astype(o_ref.dtype)


def _out_block(o_a, o_b, pf, x2, p2, wa, wb, wo, wpg, wp, ln_g, ln_b, tm):
    t = x2.shape[0]
    bf = jnp.bfloat16

    def full(shape):
        return pl.BlockSpec(shape, lambda i: (0, 0))

    return pl.pallas_call(
        _out_kernel,
        grid=(t // tm,),
        in_specs=[
            pl.BlockSpec((tm, NSA_WIDTH), lambda i: (i, 0)),
            pl.BlockSpec((tm, DN_WIDTH), lambda i: (i, 0)),
            pl.BlockSpec((tm, D_MODEL), lambda i: (i, PF_GM // D_MODEL)),
            pl.BlockSpec((tm, D_MODEL), lambda i: (i, PF_GM // D_MODEL + 1)),
            pl.BlockSpec((tm, D_MODEL), lambda i: (i, 0)),
            pl.BlockSpec((tm, PLE_DIM), lambda i: (i, 0)),
            full((NSA_WIDTH, D_MODEL)), full((DN_WIDTH, D_MODEL)), full((D_MODEL, D_MODEL)),
            full((D_MODEL, D_MODEL)), full((PLE_DIM, D_MODEL)), full((1, D_MODEL)), full((1, D_MODEL)),
        ],
        out_specs=pl.BlockSpec((tm, D_MODEL), lambda i: (i, 0)),
        out_shape=jax.ShapeDtypeStruct((t, D_MODEL), x2.dtype),
        compiler_params=_cparams(1), name="out_block",
    )(o_a, o_b, pf, pf, x2, p2, wa.astype(bf), wb.astype(bf), wo.astype(bf), wpg.astype(bf), wp.astype(bf),
      ln_g.astype(jnp.float32).reshape(1, D_MODEL), ln_b.astype(jnp.float32).reshape(1, D_MODEL))


def _layer(x, p, w_in, pos_k, pos_v, w1_k, w2_k, w1_v, w2_v, bias_tabs, conv_w, a_log, dt_bias, norm_w,
           w_a, w_b, w_o, w_ple, w_pg, ln_g, ln_b):
    bsz, seq, _ = x.shape
    t = bsz * seq
    x2 = x.reshape(t, D_MODEL)
    wb16, wf16 = _prep_w_in(w_in)
    pb, pf = _proj(x2, wb16, wf16, PROJ_TM if t % PROJ_TM == 0 else seq)

    pos2, w2p = _prep_compress_weights(pos_k, pos_v, w2_k, w2_v)
    kcvc = _compress(pf, pos2, w1_k, w1_v, w2p, bsz, seq)

    bias_sel, bias_win, bias_cmp = bias_tabs
    o_a = _nsa(pb, pf, kcvc, bias_sel, bias_win, bias_cmp, bsz, seq)
    o_b = _deltanet(pf, conv_w, a_log, dt_bias, norm_w, bsz, seq)
    out = _out_block(o_a, o_b, pf, x2, p.reshape(t, PLE_DIM), w_a, w_b, w_o, w_pg, w_ple, ln_g, ln_b,
                     OUT_TM if t % OUT_TM == 0 else seq)
    return out.reshape(bsz, seq, D_MODEL)


def kernel(x, p, w_in, cmp_pos_k, cmp_pos_v, cmp_w1_k, cmp_w2_k, cmp_w1_v, cmp_w2_v, rel_bias, dn_conv_w,
           dn_a_log, dn_dt_bias, dn_norm_w, w_branch_a, w_branch_b, w_out, w_ple, w_ple_gate, ln_g, ln_b):
    depth = w_in.shape[0]
    bias_tabs = _bias_tables(rel_bias, x.shape[1])
    for i in range(depth):
        x = _layer(x, p[i], w_in[i], cmp_pos_k[i], cmp_pos_v[i], cmp_w1_k[i], cmp_w2_k[i], cmp_w1_v[i],
                   cmp_w2_v[i], bias_tabs, dn_conv_w[i], dn_a_log[i], dn_dt_bias[i], dn_norm_w[i],
                   w_branch_a[i], w_branch_b[i], w_out[i], w_ple[i], w_ple_gate[i], ln_g[i], ln_b[i])
    return x
```

```python
import functools
import math

import numpy as np
import jax
import jax.numpy as jnp
from jax import lax
from jax.experimental import pallas as pl
from jax.experimental.pallas import tpu as pltpu

D_MODEL = 1024
PLE_DIM = 256
NSA_HEADS = 8
NSA_GROUPS = 2
NSA_HPG = NSA_HEADS // NSA_GROUPS
NSA_HEAD_DIM = 64
NSA_WIDTH = NSA_HEADS * NSA_HEAD_DIM
NSA_KV = NSA_GROUPS * NSA_HEAD_DIM
CMP_BLOCK = 32
CMP_STRIDE = 16
CMP_HIDDEN = 256
SEL_BLOCK = 64
SEL_TOPK = 8
WINDOW = 512
DN_HEADS = 4
DN_HEAD_DIM = 128
DN_WIDTH = DN_HEADS * DN_HEAD_DIM
DN_CONV = 4
DN_CHUNK = 64
NUM_BUCKETS = 32
REL_MAX_DIST = 1024
DEEPNORM_ALPHA = 2.0 ** 0.25
NEG = -1e30
FORCE = 1e6
LOG2E = 1.4426950408889634

LANES = 128
TILE = 128
SEL_LANE0 = 64
SEL_GROUP = 4
NSA_TILES_PER_STEP = 8
NSA_V_ROWS = NSA_HEAD_DIM + 16
VMEM_LIMIT = 56 * 1024 * 1024
PROJ_TM = 512
PROJ_TN = 1024
OUT_TM = 1024
OUT_SUBTILES = 4
DN_INV_BLOCK = 16
DN_BATCHES = 2
DN_OUT_UNROLL = 32
DN_PAIR_UNROLL = 16
DN_DECAY_PARTS = 2
DN_PREP_UNROLL = 64

HIGHEST = lax.Precision.HIGHEST

PB_QP = 0
PB_KS = PB_QP + NSA_WIDTH
PB_KW = PB_KS + NSA_KV
PB_VS = PB_KW + NSA_KV
PB_VW = PB_VS + NSA_KV
PB_WIDTH = PB_VW + NSA_KV
PF_GM = 0
PF_QKVB = PF_GM + 2 * D_MODEL
PF_ZA = PF_QKVB + 3 * DN_WIDTH
PF_ZB = PF_ZA + NSA_WIDTH
PF_KC = PF_ZB + DN_WIDTH
PF_VC = PF_KC + NSA_KV
PF_SMALL = PF_VC + NSA_KV
PF_WIDTH = PF_SMALL + LANES
SMALL_BETA = 3 * NSA_HEADS
SMALL_A = SMALL_BETA + DN_HEADS


def _bucket_thresholds():
    max_exact = NUM_BUCKETS // 2
    span = NUM_BUCKETS - max_exact
    ratio = REL_MAX_DIST // max_exact
    thr = list(range(1, max_exact + 1))
    for k in range(1, span):
        n = max_exact
        while n ** span < max_exact ** span * ratio ** k:
            n += 1
        thr.append(n)
    return tuple(thr)


_THR = _bucket_thresholds()


def _cparams(n_axes):
    return pltpu.CompilerParams(dimension_semantics=("arbitrary",) * n_axes, vmem_limit_bytes=VMEM_LIMIT)


def _proj_kernel(x_ref, wb_ref, wf_ref, pb_ref, pf_ref):
    nt_dims = (((1,), (1,)), ((), ()))
    xb = x_ref[...].astype(jnp.bfloat16)
    pb_ref[...] = lax.dot_general(xb, wb_ref[...], nt_dims, preferred_element_type=jnp.float32).astype(pb_ref.dtype)
    n = pf_ref.shape[1]
    for c0 in range(0, n, PROJ_TN):
        c1 = min(c0 + PROJ_TN, n)
        pf_ref[:, c0:c1] = lax.dot_general(xb, wf_ref[c0:c1, :], nt_dims, preferred_element_type=jnp.float32)


def _proj(x2, wb, wf, tm):
    t, d = x2.shape
    resident = dict(pipeline_mode=pl.Buffered(1))
    return pl.pallas_call(
        _proj_kernel,
        grid=(t // tm,),
        in_specs=[pl.BlockSpec((tm, d), lambda i: (i, 0)),
                  pl.BlockSpec((PB_WIDTH, d), lambda i: (0, 0), **resident),
                  pl.BlockSpec((PF_WIDTH, d), lambda i: (0, 0), **resident)],
        out_specs=[pl.BlockSpec((tm, PB_WIDTH), lambda i: (i, 0)),
                   pl.BlockSpec((tm, PF_WIDTH), lambda i: (i, 0))],
        out_shape=[jax.ShapeDtypeStruct((t, PB_WIDTH), jnp.bfloat16),
                   jax.ShapeDtypeStruct((t, PF_WIDTH), jnp.float32)],
        compiler_params=_cparams(1), name="proj",
    )(x2, wb, wf)


def _prep_w_in(w):
    d = w.shape[0]
    wt = jnp.swapaxes(w, 0, 1)
    o = 0
    wq = wt[o:o + NSA_WIDTH]; o += NSA_WIDTH
    wkv = wt[o:o + 6 * NSA_KV]; o += 6 * NSA_KV
    wg = wt[o:o + 3 * NSA_HEADS]; o += 3 * NSA_HEADS
    wza = wt[o:o + NSA_WIDTH]; o += NSA_WIDTH
    wqkvb = wt[o:o + 3 * DN_WIDTH]; o += 3 * DN_WIDTH
    wbeta_a = wt[o:o + 2 * DN_HEADS]; o += 2 * DN_HEADS
    wzb = wt[o:o + DN_WIDTH]; o += DN_WIDTH
    wgm = wt[o:o + 2 * D_MODEL]
    wkcvc, wks, wvs, wkw, wvw = (wkv[0:2 * NSA_KV], wkv[2 * NSA_KV:3 * NSA_KV], wkv[3 * NSA_KV:4 * NSA_KV],
                                 wkv[4 * NSA_KV:5 * NSA_KV], wkv[5 * NSA_KV:6 * NSA_KV])
    wb = jnp.concatenate([wq * (NSA_HEAD_DIM ** -0.5 * LOG2E), wks, wkw, wvs, wvw], axis=0).astype(jnp.bfloat16)
    pad = jnp.zeros((LANES - 3 * NSA_HEADS - 2 * DN_HEADS, d), w.dtype)
    wf = jnp.concatenate([wgm, wqkvb, wza, wzb, wkcvc, wg, wbeta_a, pad], axis=0).astype(jnp.bfloat16)
    return wb, wf


def _bias_kernel(tab_ref, sel_ref, win_ref, cmp_ref, *, n_cmp):
    h = pl.program_id(0)

    def bucket_of(n):
        return sum(1 for t in _THR if n >= t)

    def lookup(n, lo, hi):
        b_lo, b_hi = bucket_of(max(lo, 0)), bucket_of(max(hi, 0))
        val = jnp.full(n.shape, tab_ref[b_lo, h], jnp.float32)
        for b in range(b_lo + 1, b_hi + 1):
            val = jnp.where(n >= _THR[b - 1], tab_ref[b, h], val)
        return val * LOG2E

    kj = lax.broadcasted_iota(jnp.int32, (TILE, TILE), 0)
    qi = lax.broadcasted_iota(jnp.int32, (TILE, TILE), 1)
    n_sel_tiles = sel_ref.shape[1] - 1
    n_win_tiles = win_ref.shape[1] - 1
    for dt in range(max(n_sel_tiles, n_win_tiles)):
        dist = dt * TILE + qi - kj
        v = lookup(jnp.maximum(dist, 0), dt * TILE - (TILE - 1), dt * TILE + (TILE - 1))
        if dt < n_sel_tiles:
            sel_ref[0, dt] = jnp.where(dist >= 0, v, NEG)
        if dt < n_win_tiles:
            win_ref[0, dt] = jnp.where((dist >= 0) & (dist < WINDOW), v, NEG)
    sel_ref[0, n_sel_tiles] = jnp.full((TILE, TILE), NEG, jnp.float32)
    win_ref[0, n_win_tiles] = jnp.full((TILE, TILE), NEG, jnp.float32)
    nc = cmp_ref.shape[2]
    band = 16
    for t in range(cmp_ref.shape[1]):
        for c0 in range(0, nc, band):
            c = c0 + lax.broadcasted_iota(jnp.int32, (band, TILE), 0)
            s = t * TILE + lax.broadcasted_iota(jnp.int32, (band, TILE), 1)
            dist = s - (c * CMP_STRIDE + CMP_BLOCK - 1)
            lo = t * TILE - ((c0 + band - 1) * CMP_STRIDE + CMP_BLOCK - 1)
            hi = t * TILE + TILE - 1 - (c0 * CMP_STRIDE + CMP_BLOCK - 1)
            cmp_ref[0, t, c0:c0 + band, :] = jnp.where((dist >= 0) & (c < n_cmp),
                                                       lookup(jnp.maximum(dist, 0), lo, hi), NEG)


def _bias_tables(rel_bias, seq):
    nq = seq // TILE
    nwin = WINDOW // TILE + 1
    nc = seq // CMP_STRIDE
    n_cmp = nc - CMP_BLOCK // CMP_STRIDE + 1
    return pl.pallas_call(
        functools.partial(_bias_kernel, n_cmp=n_cmp),
        grid=(NSA_HEADS,),
        in_specs=[pl.BlockSpec(memory_space=pltpu.SMEM)],
        out_specs=[pl.BlockSpec((1, nq + 1, TILE, TILE), lambda h: (h, 0, 0, 0)),
                   pl.BlockSpec((1, nwin + 1, TILE, TILE), lambda h: (h, 0, 0, 0)),
                   pl.BlockSpec((1, nq, nc, TILE), lambda h: (h, 0, 0, 0))],
        out_shape=[jax.ShapeDtypeStruct((NSA_HEADS, nq + 1, TILE, TILE), jnp.float32),
                   jax.ShapeDtypeStruct((NSA_HEADS, nwin + 1, TILE, TILE), jnp.float32),
                   jax.ShapeDtypeStruct((NSA_HEADS, nq, nc, TILE), jnp.float32)],
        compiler_params=_cparams(1), name="bias_tables",
    )(rel_bias.astype(jnp.float32))


def _gelu_tanh(x):
    return x * (0.5 * (1.0 + jnp.tanh(math.sqrt(2.0 / math.pi) * (x + 0.044715 * (x * x * x)))))


def _compress_kernel(xk_ref, xv_ref, pos_ref, w1k_ref, w1v_ref, w2_ref, o_ref, wbd):
    nc = o_ref.shape[2]
    half = CMP_BLOCK // 2
    dh = NSA_HEAD_DIM
    bf16 = jnp.bfloat16

    @pl.when(pl.program_id(0) == 0)
    def _():
        zero = jnp.zeros((dh, CMP_HIDDEN), bf16)
        for kv, w1_ref in enumerate((w1k_ref, w1v_ref)):
            for a in range(2):
                for l in range(half):
                    r0 = (a * half + l) * dh
                    wl = w1_ref[r0:r0 + dh, :].astype(bf16)
                    wbd[kv, a, l * NSA_KV:(l + 1) * NSA_KV, :] = jnp.concatenate(
                        [jnp.concatenate([wl, zero], axis=1), jnp.concatenate([zero, wl], axis=1)], axis=0)

    rs = [jnp.concatenate([x_ref[pl.ds(l, nc, stride=CMP_STRIDE), :] for l in range(CMP_STRIDE)], axis=1)
          for x_ref in (xk_ref, xv_ref)]
    a = [jnp.dot((r + pos_ref[kv, 0:1, :]).astype(bf16), wbd[kv, 0], preferred_element_type=jnp.float32)
         for kv, r in enumerate(rs)]
    b = [jnp.dot((r + pos_ref[kv, 1:2, :]).astype(bf16), wbd[kv, 1], preferred_element_type=jnp.float32)
         for kv, r in enumerate(rs)]
    for kv in range(2):
        hid = a[kv] + pltpu.roll(b[kv], nc - 1, 0)
        o_ref[0, kv] = jnp.dot(_gelu_tanh(hid).astype(bf16), w2_ref[kv], preferred_element_type=jnp.float32)


def _compress(pf, pos2, w1_k, w1_v, w2p, bsz, seq):
    nc = seq // CMP_STRIDE
    width = CMP_STRIDE * NSA_KV
    hid = NSA_GROUPS * CMP_HIDDEN
    assert CMP_BLOCK == 2 * CMP_STRIDE and PF_VC == PF_KC + NSA_KV
    return pl.pallas_call(
        _compress_kernel,
        grid=(bsz,),
        in_specs=[pl.BlockSpec((seq, NSA_KV), lambda b: (b, PF_KC // NSA_KV)),
                  pl.BlockSpec((seq, NSA_KV), lambda b: (b, PF_VC // NSA_KV)),
                  pl.BlockSpec((2, 2, width), lambda b: (0, 0, 0)),
                  pl.BlockSpec(w1_k.shape, lambda b: (0, 0)),
                  pl.BlockSpec(w1_v.shape, lambda b: (0, 0)),
                  pl.BlockSpec((2, hid, LANES), lambda b: (0, 0, 0))],
        out_specs=pl.BlockSpec((1, 2, nc, LANES), lambda b: (b, 0, 0, 0)),
        out_shape=jax.ShapeDtypeStruct((bsz, 2, nc, LANES), jnp.float32),
        scratch_shapes=[pltpu.VMEM((2, 2, width, hid), jnp.bfloat16)],
        compiler_params=_cparams(1), name="compress",
    )(pf, pf, pos2, w1_k, w1_v, w2p)


def _prep_compress_weights(pos_k, pos_v, w2_k, w2_v):
    eye = jnp.eye(NSA_GROUPS, dtype=jnp.float32)
    half = CMP_BLOCK // 2

    def w2_both(w2):
        return jnp.einsum('jd,gh->gjhd', w2, eye).reshape(NSA_GROUPS * CMP_HIDDEN, NSA_GROUPS * NSA_HEAD_DIM)

    def pos_both(pos):
        p = pos.reshape(2, half, 1, NSA_HEAD_DIM)
        return jnp.broadcast_to(p, (2, half, NSA_GROUPS, NSA_HEAD_DIM)).reshape(2, half * NSA_KV)

    pos2 = jnp.stack([pos_both(pos_k), pos_both(pos_v)]).astype(jnp.float32)
    w2p = jnp.stack([w2_both(w2_k), w2_both(w2_v)]).astype(jnp.bfloat16)
    return pos2, w2p


def _pair_lanes(x, g):
    sw = pltpu.roll(x, LANES // 2, 1)
    lane = lax.broadcasted_iota(jnp.int32, x.shape, 1)
    own = (lane < LANES // 2) == (g == 0)
    return jnp.where(own, x, sw)


def _group_rows_t(x, g):
    xt = x.T
    half = LANES // 2
    return jnp.where(g == 0, xt[0:half, :], xt[half:LANES, :])


def _nsa_step_kernel(q_ref, qn_ref, ks_ref, kw_ref, vs_ref, vw_ref, kcvc_ref, bc_ref, bsel_ref, bwin_ref,
                     small_ref, za_ref, o_ref,
                     ksa, kwp, vst, vwt, kcs, vct, qa_s, qp_s, ocmp_s, sbuf, *, n_cmp, n_sel, n_top):
    g = pl.program_id(0)
    step_id = pl.program_id(2)
    tq = TILE
    dh = NSA_HEAD_DIM
    seq = ks_ref.shape[0]
    nq = seq // TILE
    nc = kcvc_ref.shape[2]
    n_wt = bwin_ref.shape[1] - 1
    heads = range(NSA_HPG)
    f32 = jnp.float32
    bf16 = jnp.bfloat16
    nt_dims = (((1,), (1,)), ((), ()))
    gk = SEL_GROUP * TILE

    def select_tile(src_ref, r0, tile, slot):
        q32 = src_ref[r0:r0 + tq, :].astype(f32)
        low_half = lax.broadcasted_iota(jnp.int32, (tq, LANES), 1) < NSA_HEAD_DIM
        qh = []
        for hh in heads:
            blk = q32[:, (hh // 2) * LANES:(hh // 2 + 1) * LANES]
            if hh % 2:
                blk = pltpu.roll(blk, LANES // 2, 1)
            qh.append(jnp.where(low_half, blk, 0.0).astype(bf16))
        for hh in heads:
            qp_s[slot, hh * tq:(hh + 1) * tq, :] = qh[hh]
        q_all = qp_s[slot]
        s_c = lax.dot_general(kcs[...], q_all, nt_dims, preferred_element_type=f32)
        bias_c = jnp.concatenate([bc_ref[hh, tile] for hh in heads], axis=1)
        valid = bias_c > 0.5 * NEG
        s_c = s_c + bias_c
        e = jnp.where(valid, jnp.exp2(s_c - jnp.max(s_c, axis=0, keepdims=True)), 0.0)
        den = jnp.maximum(jnp.sum(e, axis=0, keepdims=True), 1e-30)
        p_c = e * (1.0 / den)
        psum = sum(p_c[:, hh * tq:(hh + 1) * tq] for hh in heads)
        sj = lax.broadcasted_iota(jnp.int32, (n_sel, nc), 0)
        ci = lax.broadcasted_iota(jnp.int32, (n_sel, nc), 1)
        overlap = ((ci * CMP_STRIDE < (sj + 1) * SEL_BLOCK) & (ci * CMP_STRIDE + CMP_BLOCK > sj * SEL_BLOCK)
                   & (ci < n_cmp)).astype(bf16)
        imp_t = sum(jnp.dot(overlap, part, preferred_element_type=f32)
                    for part in _split_bf16(psum, 3))
        ocmp_s[slot] = jnp.dot(vct[...], p_c.astype(bf16), preferred_element_type=f32)
        blk = lax.broadcasted_iota(jnp.int32, (n_sel, tq), 0)
        cur = (tile * tq + lax.broadcasted_iota(jnp.int32, (n_sel, tq), 1)) >> 6
        forced = (blk == 0) | (blk == cur) | (blk == cur - 1)
        imp_t = jnp.where(forced, FORCE, jnp.where(blk > cur, -FORCE, imp_t))
        rank = jnp.zeros((n_sel, tq), jnp.int32)
        for j in range(n_sel):
            other = imp_t[j:j + 1, :]
            ahead = (other > imp_t) | ((other == imp_t) & (blk > j))
            rank = rank + ahead.astype(jnp.int32)
        selb = jnp.where(rank < n_top, 0.0, NEG)
        pieces = [jnp.zeros((SEL_LANE0, tq), f32), selb]
        if LANES - SEL_LANE0 - n_sel > 0:
            pieces.append(jnp.zeros((LANES - SEL_LANE0 - n_sel, tq), f32))
        selb_r = jnp.concatenate(pieces, axis=0).T.astype(bf16)
        for hh in heads:
            qa_s[slot, hh * tq:(hh + 1) * tq, :] = qh[hh] + selb_r

    @pl.when(step_id == 0)
    def _():
        row = lax.broadcasted_iota(jnp.int32, (seq, LANES), 0)
        lane = lax.broadcasted_iota(jnp.int32, (seq, LANES), 1)
        onehot = (lane - SEL_LANE0) == (row >> 6)
        own = lane < NSA_HEAD_DIM
        ks_g = jnp.where(own, _pair_lanes(ks_ref[...].astype(f32), g), 0.0)
        ksa[...] = jnp.where(onehot, 1.0, ks_g).astype(bf16)
        pad = (n_wt - 1) * TILE
        kwp[0:pad, :] = jnp.zeros((pad, LANES), bf16)
        kwp[pad:pad + seq, :] = jnp.where(own, _pair_lanes(kw_ref[...].astype(f32), g), 0.0).astype(bf16)
        for kt in range(n_wt - 1):
            vwt[kt] = jnp.zeros(vwt.shape[1:], bf16)
        extra = vst.shape[1] - dh
        ones_row = (lax.broadcasted_iota(jnp.int32, (extra, TILE), 0) == 0).astype(bf16)
        for kt in range(nq):
            rows = slice(kt * TILE, (kt + 1) * TILE)
            vst[kt, 0:dh, :] = _group_rows_t(vs_ref[rows, :].astype(f32), g).astype(bf16)
            vst[kt, dh:dh + extra, :] = ones_row
            vwt[kt + n_wt - 1, 0:dh, :] = _group_rows_t(vw_ref[rows, :].astype(f32), g).astype(bf16)
            vwt[kt + n_wt - 1, dh:dh + extra, :] = ones_row
        kcs[...] = _pair_lanes(kcvc_ref[0, 0], g).astype(bf16)
        for ct in range(nc // TILE):
            rows = slice(ct * TILE, (ct + 1) * TILE)
            vct[:, rows] = _group_rows_t(kcvc_ref[0, 1, rows, :], g).astype(bf16)
        select_tile(q_ref, 0, 0, 0)

    tps = NSA_TILES_PER_STEP

    def one_tile(n_groups, slot):
        qt = step_id * tps + slot
        rows = slice(slot * tq, (slot + 1) * tq)
        q_all = qp_s[slot]
        q_aug = qa_s[slot]
        kw_rows = kwp[pl.ds(pl.multiple_of(qt * TILE, TILE), n_wt * TILE), :]
        s_w = lax.dot_general(kw_rows, q_all, nt_dims, preferred_element_type=f32)

        def sel_scores(gi):
            s = lax.dot_general(ksa[gi * gk:(gi + 1) * gk, :], q_aug, nt_dims, preferred_element_type=f32)
            tiles = []
            for t in range(SEL_GROUP):
                dt = qt - (gi * SEL_GROUP + t)
                idx = jnp.where(dt >= 0, dt, nq)
                tiles.append(jnp.concatenate([bsel_ref[hh, idx] for hh in heads], axis=1))
            sbuf[gi % 2] = s + jnp.concatenate(tiles, axis=0)

        sel_scores(0)
        if slot + 1 < tps:
            select_tile(q_ref, (slot + 1) * tq, qt + 1, slot + 1)
        else:
            select_tile(qn_ref, 0, jnp.minimum(qt + 1, nq - 1), 0)

        w_tiles = []
        for t in range(n_wt):
            dt = n_wt - 1 - t
            idx = jnp.where(qt >= dt, dt, n_wt)
            w_tiles.append(jnp.concatenate([bwin_ref[hh, idx] for hh in heads], axis=1))
        s_w = s_w + jnp.concatenate(w_tiles, axis=0)
        p_w = jnp.exp2(s_w - jnp.max(s_w, axis=0, keepdims=True))
        v_w = jnp.concatenate([vwt[qt + t] for t in range(n_wt)], axis=1)
        acc_w = jnp.dot(v_w, p_w.astype(bf16), preferred_element_type=f32)
        o_win = acc_w[0:dh, :] * (1.0 / acc_w[dh:dh + 1, :])

        m = jnp.full((1, NSA_HPG * tq), NEG, f32)
        acc = jnp.zeros((vst.shape[1], NSA_HPG * tq), f32)
        for gi in range(n_groups):
            if gi + 1 < n_groups:
                sel_scores(gi + 1)
            s = sbuf[gi % 2]
            m_new = jnp.maximum(m, jnp.max(s, axis=0, keepdims=True))
            alpha = jnp.exp2(m - m_new)
            p = jnp.exp2(s - m_new)
            v_t = jnp.concatenate([vst[gi * SEL_GROUP + t] for t in range(SEL_GROUP)], axis=1)
            acc = alpha * acc + jnp.dot(v_t, p.astype(bf16), preferred_element_type=f32)
            m = m_new
        o_sel = acc[0:dh, :] * (1.0 / acc[dh:dh + 1, :])
        o_cmp = ocmp_s[slot]

        gates_t = jax.nn.sigmoid(small_ref[rows, :]).T
        mixed = []
        for hh in heads:
            def gate(br):
                r = br * NSA_HEADS + hh
                return jnp.where(g == 0, gates_t[r:r + 1, :], gates_t[r + NSA_HPG:r + NSA_HPG + 1, :])
            cols = slice(hh * tq, (hh + 1) * tq)
            mixed.append(gate(0) * o_cmp[:, cols] + gate(1) * o_sel[:, cols] + gate(2) * o_win[:, cols])
        o = jnp.concatenate([jnp.concatenate(mixed[2 * j:2 * j + 2], axis=0).T for j in range(NSA_HPG // 2)],
                            axis=1)
        z = za_ref[rows, :]
        o_ref[rows, :] = (o * (z * jax.nn.sigmoid(z))).astype(o_ref.dtype)

    def step(first_tile):
        for slot in range(tps):
            one_tile((first_tile + slot) // SEL_GROUP + 1, slot)

    variants = {}
    for s in range(nq // tps):
        key = tuple((s * tps + slot) // SEL_GROUP for slot in range(tps))
        variants.setdefault(key, []).append(s)
    for steps in variants.values():
        cond = functools.reduce(jnp.logical_or, [step_id == s for s in steps])
        pl.when(cond)(functools.partial(step, steps[0] * tps))


def _nsa(pb, pf, kcvc, bias_sel, bias_win, bias_cmp, bsz, seq):
    nq = seq // TILE
    nc = seq // CMP_STRIDE
    n_cmp = nc - CMP_BLOCK // CMP_STRIDE + 1
    n_sel = seq // SEL_BLOCK
    n_top = min(SEL_TOPK, n_sel)
    nwin = bias_win.shape[1]
    tps = NSA_TILES_PER_STEP
    assert nq % SEL_GROUP == 0 and nc % TILE == 0 and bias_sel.shape[1] == nq + 1
    assert tps >= 2 and nq % tps == 0
    ns = nq // tps
    rows = tps * TILE
    gw = NSA_HPG * NSA_HEAD_DIM
    kern = functools.partial(_nsa_step_kernel, n_cmp=n_cmp, n_sel=n_sel, n_top=n_top)
    return pl.pallas_call(
        kern,
        grid=(NSA_GROUPS, bsz, ns),
        in_specs=[
            pl.BlockSpec((rows, gw), lambda g, b, t: (b * ns + t, PB_QP // gw + g)),
            pl.BlockSpec((TILE, gw), lambda g, b, t: (b * nq + jnp.minimum((t + 1) * tps, nq - 1),
                                                      PB_QP // gw + g)),
            pl.BlockSpec((seq, LANES), lambda g, b, t: (b, PB_KS // LANES)),
            pl.BlockSpec((seq, LANES), lambda g, b, t: (b, PB_KW // LANES)),
            pl.BlockSpec((seq, LANES), lambda g, b, t: (b, PB_VS // LANES)),
            pl.BlockSpec((seq, LANES), lambda g, b, t: (b, PB_VW // LANES)),
            pl.BlockSpec((1, 2, nc, LANES), lambda g, b, t: (b, 0, 0, 0)),
            pl.BlockSpec((NSA_HPG, nq, nc, TILE), lambda g, b, t: (g, 0, 0, 0)),
            pl.BlockSpec((NSA_HPG, nq + 1, TILE, TILE), lambda g, b, t: (g, 0, 0, 0)),
            pl.BlockSpec((NSA_HPG, nwin, TILE, TILE), lambda g, b, t: (g, 0, 0, 0)),
            pl.BlockSpec((rows, LANES), lambda g, b, t: (b * ns + t, PF_SMALL // LANES)),
            pl.BlockSpec((rows, gw), lambda g, b, t: (b * ns + t, PF_ZA // gw + g)),
        ],
        out_specs=pl.BlockSpec((rows, gw), lambda g, b, t: (b * ns + t, g)),
        out_shape=jax.ShapeDtypeStruct((bsz * seq, NSA_WIDTH), jnp.bfloat16),
        scratch_shapes=[
            pltpu.VMEM((seq, LANES), jnp.bfloat16),
            pltpu.VMEM((seq + (nwin - 2) * TILE, LANES), jnp.bfloat16),
            pltpu.VMEM((nq, NSA_V_ROWS, TILE), jnp.bfloat16),
            pltpu.VMEM((nq + nwin - 2, NSA_V_ROWS, TILE), jnp.bfloat16),
            pltpu.VMEM((nc, LANES), jnp.bfloat16),
            pltpu.VMEM((NSA_HEAD_DIM, nc), jnp.bfloat16),
            pltpu.VMEM((tps, NSA_HPG * TILE, LANES), jnp.bfloat16),
            pltpu.VMEM((tps, NSA_HPG * TILE, LANES), jnp.bfloat16),
            pltpu.VMEM((tps, NSA_HEAD_DIM, NSA_HPG * TILE), jnp.float32),
            pltpu.VMEM((2, SEL_GROUP * TILE, NSA_HPG * TILE), jnp.float32),
        ],
        compiler_params=_cparams(3), name="nsa",
    )(pb, pb, pb, pb, pb, pb, kcvc, bias_cmp, bias_sel, bias_win, pf, pf)


def _split_bf16(a, n):
    parts = []
    for _ in range(n - 1):
        hi = a.astype(jnp.bfloat16)
        parts.append(hi)
        a = a - hi.astype(jnp.float32)
    parts.append(a.astype(jnp.bfloat16))
    return parts


def _dot3(a, b):
    ah, al = _split_bf16(a, 2)
    bh, bl = _split_bf16(b, 2)
    f32 = jnp.float32
    return (jnp.dot(ah, bh, preferred_element_type=f32) + jnp.dot(al, bh, preferred_element_type=f32)
            + jnp.dot(ah, bl, preferred_element_type=f32))


def _softplus(x):
    return jnp.maximum(x, 0.0) + jnp.log(1.0 + jnp.exp(-jnp.abs(x)))


def _dn_kernel(scal_ref, q_ref, k_ref, v_ref, small_ref, z_ref, cw_ref, nw_ref, o_ref,
               qn, kn, vn, bet, gl, mm, nn, qq, oo, dd, mm2, dd2, *, seq):
    h = pl.program_id(1)
    n_rows = q_ref.shape[0]
    nb = n_rows // seq
    c = DN_CHUNK
    n_chunks = seq // c
    d = DN_HEAD_DIM

    head = 8

    def conv_body(x_ref, which):
        n = n_rows - head
        y = x_ref[pl.ds(head, n), :] * cw_ref[which, DN_CONV - 1:DN_CONV, :]
        for j in range(DN_CONV - 1):
            y = y + x_ref[pl.ds(head - (DN_CONV - 1 - j), n), :] * cw_ref[which, j:j + 1, :]
        return y

    def conv_head(x_ref, which, r0):
        x = x_ref[pl.ds(r0, head), :]
        rowi = lax.broadcasted_iota(jnp.int32, (head, d), 0)
        y = x * cw_ref[which, DN_CONV - 1:DN_CONV, :]
        for j in range(DN_CONV - 1):
            sh = DN_CONV - 1 - j
            y = y + jnp.where(rowi >= sh, pltpu.roll(x, sh, 0), 0.0) * cw_ref[which, j:j + 1, :]
        return y

    def silu(y):
        return y * jax.nn.sigmoid(y)

    def l2n(t):
        return t * lax.rsqrt(jnp.sum(t * t, axis=-1, keepdims=True) + 1e-6)

    finish = (lambda y: l2n(silu(y)) * (d ** -0.5), lambda y: l2n(silu(y)), silu)
    for which, (x_ref, dst) in enumerate(((q_ref, qn), (k_ref, kn), (v_ref, vn))):
        dst[pl.ds(head, n_rows - head), :] = finish[which](conv_body(x_ref, which))
        for b in range(nb):
            dst[pl.ds(b * seq, head), :] = finish[which](conv_head(x_ref, which, b * seq))
    decay_rate = -jnp.exp(scal_ref[0, h])
    dt_bias = scal_ref[1, h]
    for blk in range(n_rows // LANES):
        tok = slice(blk * LANES, (blk + 1) * LANES)
        st = small_ref[tok, :].T
        b_row = st[SMALL_BETA:SMALL_BETA + 1, :]
        a_row = st[SMALL_A:SMALL_A + 1, :]
        for hh in range(1, DN_HEADS):
            b_row = jnp.where(h == hh, st[SMALL_BETA + hh:SMALL_BETA + hh + 1, :], b_row)
            a_row = jnp.where(h == hh, st[SMALL_A + hh:SMALL_A + hh + 1, :], a_row)
        bet[tok, :] = jnp.broadcast_to(jax.nn.sigmoid(b_row), (LANES, d)).T
        gl[tok, :] = jnp.broadcast_to(decay_rate * _softplus(a_row + dt_bias), (LANES, d)).T

    f32 = jnp.float32
    bf16 = jnp.bfloat16
    nt_dims = (((1,), (1,)), ((), ()))
    ri = lax.broadcasted_iota(jnp.int32, (c, 2 * c), 0)
    lane2 = lax.broadcasted_iota(jnp.int32, (c, 2 * c), 1)
    first = lane2 < c
    cj = lane2 & (c - 1)
    incl = ri >= cj
    strict = ri > cj
    tril_b = (lax.broadcasted_iota(jnp.int32, (c, c), 0) >= lax.broadcasted_iota(jnp.int32, (c, c), 1)).astype(bf16)
    bs = DN_INV_BLOCK
    sh = bs.bit_length() - 1
    same_diag = (ri >> sh) == (cj >> sh)
    level_masks = []
    while (1 << sh) < c:
        level_masks.append(((ri >> (sh + 1)) == (cj >> (sh + 1))) & ((ri >> sh) > (cj >> sh)))
        sh += 1
    bd_r = lax.broadcasted_iota(jnp.int32, (2 * c, 2 * c), 0)
    bd_c = lax.broadcasted_iota(jnp.int32, (2 * c, 2 * c), 1)
    block_diag = (bd_r >= c) == (bd_c >= c)

    def dot_pair(a, b):
        bb = b.astype(bf16)
        bb = jnp.where(block_diag, jnp.concatenate([bb, bb], axis=0), jnp.zeros((), bf16))
        return jnp.dot(a.astype(bf16), bb, preferred_element_type=f32)

    def stack_diag(x0, x1):
        z = jnp.zeros_like(x0)
        return jnp.concatenate([jnp.concatenate([x0, z], axis=1), jnp.concatenate([z, x1], axis=1)], axis=0)

    def chunk_prep(it, carry):
        ids = [it * DN_PREP_UNROLL + cc for cc in range(DN_PREP_UNROLL)]
        rows = [pl.ds(pl.multiple_of(i * c, c), c) for i in ids]
        pairs = range(0, DN_PREP_UNROLL, 2)
        ks = [kn[r, :] for r in rows]
        betas = [bet[r, :] for r in rows]
        gcbs = [sum(jnp.dot(tril_b, part, preferred_element_type=f32)
                    for part in _split_bf16(gl[r, :], DN_DECAY_PARTS)) for r in rows]
        kbs = [k * beta for k, beta in zip(ks, betas)]
        k2ts = [jnp.concatenate([ks[j], ks[j + 1]], axis=0).T.astype(bf16) for j in pairs]

        def against_pair_keys(xs):
            return [jnp.where(first,
                              jnp.dot(xs[j].astype(bf16), k2t, preferred_element_type=f32),
                              jnp.dot(xs[j + 1].astype(bf16), k2t, preferred_element_type=f32))
                    for j, k2t in zip(pairs, k2ts)]

        a_kks = against_pair_keys(kbs)
        decays = []
        for j in pairs:
            gc_col = jnp.where(first, gcbs[j], gcbs[j + 1])
            gc_row = jnp.concatenate([gcbs[j], gcbs[j + 1]], axis=0).T[0:c, :]
            decays.append(jnp.where(incl, jnp.exp(jnp.where(incl, gc_col - gc_row, 0.0)), 0.0))
        lows = [jnp.where(strict, a * dec, 0.0) for a, dec in zip(a_kks, decays)]
        pws = [jnp.where(same_diag, -low, 0.0) for low in lows]
        es = list(pws)
        for _ in range(max(1, (bs - 1).bit_length()) - 1):
            pws = [dot_pair(pw, pw) for pw in pws]
            es = [e + pw + dot_pair(e, pw) for e, pw in zip(es, pws)]
        for below in level_masks:
            offs = [jnp.where(below, low, 0.0) for low in lows]
            xs = [off + dot_pair(e, off) for e, off in zip(es, offs)]
            es = [e - (x + dot_pair(x, e)) for e, x in zip(es, xs)]
        egcs = [jnp.exp(gcb) for gcb in gcbs]
        rhss = [jnp.concatenate([vn[r, :] * beta, kb * egc], axis=1)
                for r, beta, kb, egc in zip(rows, betas, kbs, egcs)]
        uwbs = []
        for e, j in zip(es, pairs):
            e_hi, e_lo = _split_bf16(e, 2)
            r2 = stack_diag(rhss[j], rhss[j + 1]).astype(bf16)
            er = jnp.dot(e_hi, r2, preferred_element_type=f32) + jnp.dot(e_lo, r2, preferred_element_type=f32)
            uwbs.append((rhss[j] + er[:, 0:2 * d]).astype(bf16))
            uwbs.append((rhss[j + 1] + er[:, 2 * d:4 * d]).astype(bf16))
        qs = [qn[r, :] for r in rows]
        a_qks = [a * dec for a, dec in zip(against_pair_keys(qs), decays)]
        g_lasts = [gcb[c - 1:c, :] for gcb in gcbs]
        kdec_ts = []
        for k, gcb, g_last in zip(ks, gcbs, g_lasts):
            kdec = k * jnp.exp(g_last - gcb)
            kdec_ts.append(jnp.concatenate([kdec, jnp.zeros_like(kdec)], axis=0).T[:, 0:c].astype(bf16))
        nms = [jnp.dot(kt, uwb, preferred_element_type=f32) for kt, uwb in zip(kdec_ts, uwbs)]
        oq2s = [jnp.dot(a.astype(bf16), stack_diag(uwbs[j], uwbs[j + 1]), preferred_element_type=f32)
                for a, j in zip(a_qks, pairs)]
        oqs = []
        for oq2 in oq2s:
            oqs += [oq2[:, 0:2 * d], oq2[:, 2 * d:4 * d]]
        for i, r, nm, oq, q, egc, g_last in zip(ids, rows, nms, oqs, qs, egcs, g_lasts):
            m0 = pl.ds(pl.multiple_of(i * d, d), d)
            nn[m0, :] = nm[:, 0:d]
            mm[m0, :] = nm[:, d:2 * d].astype(bf16)
            oo[r, :] = oq[:, 0:d]
            qq[r, :] = (q * egc - oq[:, d:2 * d]).astype(bf16)
            dd[pl.ds(pl.multiple_of(i * 8, 8), 8), :] = jnp.broadcast_to(jnp.exp(g_last), (8, d))
        return carry

    lax.fori_loop(0, nb * n_chunks // DN_PREP_UNROLL, chunk_prep, 0)

    nn2, ss = kn, vn

    def chunk_pair(it, carry):
        ids = [it * DN_PAIR_UNROLL + cc for cc in range(DN_PAIR_UNROLL)]
        blk1 = [pl.ds(pl.multiple_of(2 * p * d, d), d) for p in ids]
        blk2 = [pl.ds(pl.multiple_of((2 * p + 1) * d, d), d) for p in ids]
        d1s = [dd[pl.ds(pl.multiple_of(2 * p * 8, 8), 1), :] for p in ids]
        d2s = [dd[pl.ds(pl.multiple_of((2 * p + 1) * 8, 8), 1), :] for p in ids]
        m1s = [mm[b, :] for b in blk1]
        m2s = [mm[b, :] for b in blk2]
        n1s = [nn[b, :] for b in blk1]
        xs = [jnp.dot(m2, jnp.concatenate([m1, n1.astype(bf16)], axis=1), preferred_element_type=f32)
              for m1, m2, n1 in zip(m1s, m2s, n1s)]
        for p, b2, d1, d2, m1, m2, n1, x in zip(ids, blk2, d1s, d2s, m1s, m2s, n1s, xs):
            pblk = pl.ds(pl.multiple_of(p * d, d), d)
            mm2[pblk, :] = (d2 * m1.astype(f32) + d1 * m2.astype(f32) - x[:, 0:d]).astype(bf16)
            nn2[pblk, :] = d2 * n1 - x[:, d:2 * d] + nn[b2, :]
            dd2[pl.ds(pl.multiple_of(p * 8, 8), 8), :] = jnp.broadcast_to(d1 * d2, (8, d))
        return carry

    n_pairs = n_chunks // 2
    lax.fori_loop(0, nb * n_pairs // DN_PAIR_UNROLL, chunk_pair, 0)

    def pair_scan(i, states):
        ids = [b * n_pairs + i for b in range(nb)]
        blocks = [pl.ds(pl.multiple_of(j * d, d), d) for j in ids]
        for blk, s in zip(blocks, states):
            ss[blk, :] = s
        prods = [jnp.dot(mm2[blk, :], s.astype(bf16), preferred_element_type=f32) for blk, s in zip(blocks, states)]
        return tuple(s * dd2[pl.ds(pl.multiple_of(j * 8, 8), 1), :] - pr + nn2[blk, :]
                     for s, j, blk, pr in zip(states, ids, blocks, prods))

    lax.fori_loop(0, n_pairs, pair_scan, tuple(jnp.zeros((d, d), f32) for _ in range(nb)))

    nw = nw_ref[...]

    def chunk_out(it, carry):
        pids = [it * (DN_OUT_UNROLL // 2) + cc for cc in range(DN_OUT_UNROLL // 2)]
        s0 = [ss[pl.ds(pl.multiple_of(p * d, d), d), :] for p in pids]
        s0b = [s.astype(bf16) for s in s0]
        s1b = [(s * dd[pl.ds(pl.multiple_of(2 * p * 8, 8), 1), :]
                - jnp.dot(mm[pl.ds(pl.multiple_of(2 * p * d, d), d), :], sb, preferred_element_type=f32)
                + nn[pl.ds(pl.multiple_of(2 * p * d, d), d), :]).astype(bf16)
               for p, s, sb in zip(pids, s0, s0b)]
        rows, states = [], []
        for p, sb0, sb1 in zip(pids, s0b, s1b):
            rows += [pl.ds(pl.multiple_of(2 * p * c, c), c), pl.ds(pl.multiple_of((2 * p + 1) * c, c), c)]
            states += [sb0, sb1]
        outs = [jnp.dot(qq[r, :], sb, preferred_element_type=f32) + oo[r, :] for r, sb in zip(rows, states)]
        for r, o in zip(rows, outs):
            o = o * lax.rsqrt(jnp.mean(o * o, axis=-1, keepdims=True) + 1e-6) * nw
            z = z_ref[r, :]
            o_ref[r, :] = (o * (z * jax.nn.sigmoid(z))).astype(o_ref.dtype)
        return carry

    lax.fori_loop(0, nb * n_chunks // DN_OUT_UNROLL, chunk_out, 0)


def _deltanet(pf, conv_w, a_log, dt_bias, norm_w, bsz, seq):
    d = DN_HEAD_DIM
    nb = DN_BATCHES if bsz % DN_BATCHES == 0 else 1
    rows = nb * seq
    n_chunks = rows // DN_CHUNK
    qkv0 = PF_QKVB // d
    scal = jnp.stack([a_log, dt_bias]).astype(jnp.float32)
    f32 = jnp.float32
    bf16 = jnp.bfloat16
    cw4 = conv_w.astype(f32).reshape(DN_CONV, 3, DN_HEADS, d).transpose(2, 1, 0, 3)
    assert 2 * DN_CHUNK == d and n_chunks % DN_PREP_UNROLL == 0 and n_chunks % DN_OUT_UNROLL == 0
    assert (seq // DN_CHUNK) % 2 == 0 and (n_chunks // 2) % DN_PAIR_UNROLL == 0 and DN_OUT_UNROLL % 2 == 0
    return pl.pallas_call(
        functools.partial(_dn_kernel, seq=seq),
        grid=(bsz // nb, DN_HEADS),
        in_specs=[
            pl.BlockSpec(memory_space=pltpu.SMEM),
            pl.BlockSpec((rows, d), lambda b, h: (b, qkv0 + h)),
            pl.BlockSpec((rows, d), lambda b, h: (b, qkv0 + DN_HEADS + h)),
            pl.BlockSpec((rows, d), lambda b, h: (b, qkv0 + 2 * DN_HEADS + h)),
            pl.BlockSpec((rows, LANES), lambda b, h: (b, PF_SMALL // LANES)),
            pl.BlockSpec((rows, d), lambda b, h: (b, PF_ZB // d + h)),
            pl.BlockSpec((None, 3, DN_CONV, d), lambda b, h: (h, 0, 0, 0)),
            pl.BlockSpec((1, d), lambda b, h: (0, 0)),
        ],
        out_specs=pl.BlockSpec((rows, d), lambda b, h: (b, h)),
        out_shape=jax.ShapeDtypeStruct((bsz * seq, DN_WIDTH), jnp.bfloat16),
        scratch_shapes=[
            pltpu.VMEM((rows, d), f32), pltpu.VMEM((rows, d), f32), pltpu.VMEM((rows, d), f32),
            pltpu.VMEM((rows, d), f32), pltpu.VMEM((rows, d), f32),
            pltpu.VMEM((n_chunks * d, d), bf16), pltpu.VMEM((n_chunks * d, d), f32),
            pltpu.VMEM((rows, d), bf16), pltpu.VMEM((rows, d), f32),
            pltpu.VMEM((n_chunks * 8, d), f32),
            pltpu.VMEM((n_chunks // 2 * d, d), bf16),
            pltpu.VMEM((n_chunks // 2 * 8, d), f32),
        ],
        compiler_params=_cparams(2), name="deltanet",
    )(scal, pf, pf, pf, pf, pf, cw4, norm_w.astype(f32).reshape(1, d))


def _out_kernel(oa_ref, ob_ref, gma_ref, gmb_ref, x_ref, p_ref, wa_ref, wb_ref, wo_ref, wpg_ref, wp_ref,
                lng_ref, lnb_ref, o_ref):
    f32 = jnp.float32
    bf16 = jnp.bfloat16
    tm = o_ref.shape[0]
    sub = tm // OUT_SUBTILES
    parts = [slice(i * sub, (i + 1) * sub) for i in range(OUT_SUBTILES)]
    y_a = [jnp.dot(oa_ref[r, :], wa_ref[...], preferred_element_type=f32) for r in parts]
    y_b = [jnp.dot(ob_ref[r, :], wb_ref[...], preferred_element_type=f32) for r in parts]
    pw = [jnp.dot(p_ref[r, :].astype(bf16), wp_ref[...], preferred_element_type=f32) for r in parts]
    mix = [(jax.nn.sigmoid(gma_ref[r, :]) * ya + jax.nn.sigmoid(gmb_ref[r, :]) * yb).astype(bf16)
           for r, ya, yb in zip(parts, y_a, y_b)]
    h = [DEEPNORM_ALPHA * x_ref[r, :] + jnp.dot(mx, wo_ref[...], preferred_element_type=f32)
         for r, mx in zip(parts, mix)]
    gate = [jnp.dot(hh.astype(bf16), wpg_ref[...], preferred_element_type=f32) for hh in h]
    for r, hh, gt, pp in zip(parts, h, gate, pw):
        hh = hh + jax.nn.sigmoid(gt) * pp
        mu = jnp.mean(hh, axis=-1, keepdims=True)
        hc = hh - mu
        var = jnp.mean(hc * hc, axis=-1, keepdims=True)
        o_ref[r, :] = (hc * lax.rsqrt(var + 1e-5) * lng_ref[...] + lnb_ref[...]).astype(o_ref.dtype)


def _out_block(o_a, o_b, pf, x2, p2, wa, wb, wo, wpg, wp, ln_g, ln_b, tm):
    t = x2.shape[0]
    bf = jnp.bfloat16

    def full(shape):
        return pl.BlockSpec(shape, lambda i: (0, 0), pipeline_mode=pl.Buffered(1))

    return pl.pallas_call(
        _out_kernel,
        grid=(t // tm,),
        in_specs=[
            pl.BlockSpec((tm, NSA_WIDTH), lambda i: (i, 0)),
            pl.BlockSpec((tm, DN_WIDTH), lambda i: (i, 0)),
            pl.BlockSpec((tm, D_MODEL), lambda i: (i, PF_GM // D_MODEL)),
            pl.BlockSpec((tm, D_MODEL), lambda i: (i, PF_GM // D_MODEL + 1)),
            pl.BlockSpec((tm, D_MODEL), lambda i: (i, 0)),
            pl.BlockSpec((tm, PLE_DIM), lambda i: (i, 0)),
            full((NSA_WIDTH, D_MODEL)), full((DN_WIDTH, D_MODEL)), full((D_MODEL, D_MODEL)),
            full((D_MODEL, D_MODEL)), full((PLE_DIM, D_MODEL)), full((1, D_MODEL)), full((1, D_MODEL)),
        ],
        out_specs=pl.BlockSpec((tm, D_MODEL), lambda i: (i, 0)),
        out_shape=jax.ShapeDtypeStruct((t, D_MODEL), x2.dtype),
        compiler_params=_cparams(1), name="out_block",
    )(o_a, o_b, pf, pf, x2, p2, wa.astype(bf), wb.astype(bf), wo.astype(bf), wpg.astype(bf), wp.astype(bf),
      ln_g.astype(jnp.float32).reshape(1, D_MODEL), ln_b.astype(jnp.float32).reshape(1, D_MODEL))


def _layer(x, p, w_in, pos_k, pos_v, w1_k, w2_k, w1_v, w2_v, bias_tabs, conv_w, a_log, dt_bias, norm_w,
           w_a, w_b, w_o, w_ple, w_pg, ln_g, ln_b):
    bsz, seq, _ = x.shape
    t = bsz * seq
    x2 = x.reshape(t, D_MODEL)
    wb16, wf16 = _prep_w_in(w_in)
    pb, pf = _proj(x2, wb16, wf16, PROJ_TM if t % PROJ_TM == 0 else seq)

    pos2, w2p = _prep_compress_weights(pos_k, pos_v, w2_k, w2_v)
    kcvc = _compress(pf, pos2, w1_k, w1_v, w2p, bsz, seq)

    bias_sel, bias_win, bias_cmp = bias_tabs
    o_a = _nsa(pb, pf, kcvc, bias_sel, bias_win, bias_cmp, bsz, seq)
    o_b = _deltanet(pf, conv_w, a_log, dt_bias, norm_w, bsz, seq)
    out = _out_block(o_a, o_b, pf, x2, p.reshape(t, PLE_DIM), w_a, w_b, w_o, w_pg, w_ple, ln_g, ln_b,
                     OUT_TM if t % OUT_TM == 0 else seq)
    return out.reshape(bsz, seq, D_MODEL)


def kernel(x, p, w_in, cmp_pos_k, cmp_pos_v, cmp_w1_k, cmp_w2_k, cmp_w1_v, cmp_w2_v, rel_bias, dn_conv_w,
           dn_a_log, dn_dt_bias, dn_norm_w, w_branch_a, w_branch_b, w_out, w_ple, w_ple_gate, ln_g, ln_b):
    depth = w_in.shape[0]
    bias_tabs = _bias_tables(rel_bias, x.shape[1])
    for i in range(depth):
        x = _layer(x, p[i], w_in[i], cmp_pos_k[i], cmp_pos_v[i], cmp_w1_k[i], cmp_w2_k[i], cmp_w1_v[i],
                   cmp_w2_v[i], bias_tabs, dn_conv_w[i], dn_a_log[i], dn_dt_bias[i], dn_norm_w[i],
                   w_branch_a[i], w_branch_b[i], w_out[i], w_ple[i], w_ple_gate[i], ln_g[i], ln_b[i])
    return x
```

```python
import functools
import math

import jax
import jax.numpy as jnp
from jax import lax
from jax.experimental import pallas as pl
from jax.experimental.pallas import tpu as pltpu

D_MODEL = 1024
PLE_DIM = 256
NSA_HEADS = 8
NSA_GROUPS = 2
NSA_HPG = NSA_HEADS // NSA_GROUPS
NSA_HEAD_DIM = 64
NSA_WIDTH = NSA_HEADS * NSA_HEAD_DIM
NSA_KV = NSA_GROUPS * NSA_HEAD_DIM
CMP_BLOCK = 32
CMP_STRIDE = 16
CMP_HIDDEN = 256
SEL_BLOCK = 64
SEL_TOPK = 8
WINDOW = 512
DN_HEADS = 4
DN_HEAD_DIM = 128
DN_WIDTH = DN_HEADS * DN_HEAD_DIM
DN_CONV = 4
DN_CHUNK = 64
NUM_BUCKETS = 32
REL_MAX_DIST = 1024
DEEPNORM_ALPHA = 2.0 ** 0.25
NEG = -1e30
FORCE = 1e6
LOG2E = 1.4426950408889634

LANES = 128
TILE = 128
SEL_LANE0 = 64
SEL_GROUP = 4
NSA_TILES_PER_STEP = 8
NSA_V_ROWS = NSA_HEAD_DIM + 16
VMEM_LIMIT = 56 * 1024 * 1024
PROJ_TM = 512
PROJ_TN = 1024
OUT_TM = 1024
OUT_SUBTILES = 4
DN_INV_BLOCK = 16
DN_BATCHES = 2
DN_OUT_UNROLL = 64
DN_PAIR_UNROLL = 32
DN_DECAY_PARTS = 2
DN_PREP_UNROLL = 64

PB_QP = 0
PB_KS = PB_QP + NSA_WIDTH
PB_KW = PB_KS + NSA_KV
PB_VS = PB_KW + NSA_KV
PB_VW = PB_VS + NSA_KV
PB_WIDTH = PB_VW + NSA_KV
PF_GM = 0
PF_QKVB = PF_GM + 2 * D_MODEL
PF_ZA = PF_QKVB + 3 * DN_WIDTH
PF_ZB = PF_ZA + NSA_WIDTH
PF_KC = PF_ZB + DN_WIDTH
PF_VC = PF_KC + NSA_KV
PF_SMALL = PF_VC + NSA_KV
PF_WIDTH = PF_SMALL + LANES
SMALL_BETA = 3 * NSA_HEADS
SMALL_A = SMALL_BETA + DN_HEADS


def _bucket_thresholds():
    max_exact = NUM_BUCKETS // 2
    span = NUM_BUCKETS - max_exact
    ratio = REL_MAX_DIST // max_exact
    thr = list(range(1, max_exact + 1))
    for k in range(1, span):
        n = max_exact
        while n ** span < max_exact ** span * ratio ** k:
            n += 1
        thr.append(n)
    return tuple(thr)


_THR = _bucket_thresholds()


def _cparams(n_axes):
    return pltpu.CompilerParams(dimension_semantics=("arbitrary",) * n_axes, vmem_limit_bytes=VMEM_LIMIT)


def _proj_kernel(x_ref, wb_ref, wf_ref, pb_ref, pf_ref):
    nt_dims = (((1,), (1,)), ((), ()))
    xb = x_ref[...].astype(jnp.bfloat16)
    pb_ref[...] = lax.dot_general(xb, wb_ref[...], nt_dims, preferred_element_type=jnp.float32).astype(pb_ref.dtype)
    n = pf_ref.shape[1]
    for c0 in range(0, n, PROJ_TN):
        c1 = min(c0 + PROJ_TN, n)
        pf_ref[:, c0:c1] = lax.dot_general(xb, wf_ref[c0:c1, :], nt_dims, preferred_element_type=jnp.float32)


def _proj(x2, wb, wf, tm):
    t, d = x2.shape
    resident = dict(pipeline_mode=pl.Buffered(1))
    return pl.pallas_call(
        _proj_kernel,
        grid=(t // tm,),
        in_specs=[pl.BlockSpec((tm, d), lambda i: (i, 0)),
                  pl.BlockSpec((PB_WIDTH, d), lambda i: (0, 0), **resident),
                  pl.BlockSpec((PF_WIDTH, d), lambda i: (0, 0), **resident)],
        out_specs=[pl.BlockSpec((tm, PB_WIDTH), lambda i: (i, 0)),
                   pl.BlockSpec((tm, PF_WIDTH), lambda i: (i, 0))],
        out_shape=[jax.ShapeDtypeStruct((t, PB_WIDTH), jnp.bfloat16),
                   jax.ShapeDtypeStruct((t, PF_WIDTH), jnp.float32)],
        compiler_params=_cparams(1), name="proj",
    )(x2, wb, wf)


def _prep_w_in(w):
    d = w.shape[0]
    wt = jnp.swapaxes(w, 0, 1)
    o = 0
    wq = wt[o:o + NSA_WIDTH]; o += NSA_WIDTH
    wkv = wt[o:o + 6 * NSA_KV]; o += 6 * NSA_KV
    wg = wt[o:o + 3 * NSA_HEADS]; o += 3 * NSA_HEADS
    wza = wt[o:o + NSA_WIDTH]; o += NSA_WIDTH
    wqkvb = wt[o:o + 3 * DN_WIDTH]; o += 3 * DN_WIDTH
    wbeta_a = wt[o:o + 2 * DN_HEADS]; o += 2 * DN_HEADS
    wzb = wt[o:o + DN_WIDTH]; o += DN_WIDTH
    wgm = wt[o:o + 2 * D_MODEL]
    wkcvc, wks, wvs, wkw, wvw = (wkv[0:2 * NSA_KV], wkv[2 * NSA_KV:3 * NSA_KV], wkv[3 * NSA_KV:4 * NSA_KV],
                                 wkv[4 * NSA_KV:5 * NSA_KV], wkv[5 * NSA_KV:6 * NSA_KV])
    wb = jnp.concatenate([wq * (NSA_HEAD_DIM ** -0.5 * LOG2E), wks, wkw, wvs, wvw], axis=0).astype(jnp.bfloat16)
    pad = jnp.zeros((LANES - 3 * NSA_HEADS - 2 * DN_HEADS, d), w.dtype)
    wf = jnp.concatenate([wgm, wqkvb, wza, wzb, wkcvc, wg, wbeta_a, pad], axis=0).astype(jnp.bfloat16)
    return wb, wf


def _bias_kernel(tab_ref, sel_ref, win_ref, cmp_ref, *, n_cmp):
    h = pl.program_id(0)

    def bucket_of(n):
        return sum(1 for t in _THR if n >= t)

    def lookup(n, lo, hi):
        b_lo, b_hi = bucket_of(max(lo, 0)), bucket_of(max(hi, 0))
        val = jnp.full(n.shape, tab_ref[b_lo, h], jnp.float32)
        for b in range(b_lo + 1, b_hi + 1):
            val = jnp.where(n >= _THR[b - 1], tab_ref[b, h], val)
        return val * LOG2E

    kj = lax.broadcasted_iota(jnp.int32, (TILE, TILE), 0)
    qi = lax.broadcasted_iota(jnp.int32, (TILE, TILE), 1)
    n_sel_tiles = sel_ref.shape[1] - 1
    n_win_tiles = win_ref.shape[1] - 1
    for dt in range(max(n_sel_tiles, n_win_tiles)):
        dist = dt * TILE + qi - kj
        v = lookup(jnp.maximum(dist, 0), dt * TILE - (TILE - 1), dt * TILE + (TILE - 1))
        if dt < n_sel_tiles:
            sel_ref[0, dt] = jnp.where(dist >= 0, v, NEG)
        if dt < n_win_tiles:
            win_ref[0, dt] = jnp.where((dist >= 0) & (dist < WINDOW), v, NEG)
    sel_ref[0, n_sel_tiles] = jnp.full((TILE, TILE), NEG, jnp.float32)
    win_ref[0, n_win_tiles] = jnp.full((TILE, TILE), NEG, jnp.float32)
    nc = cmp_ref.shape[2]
    band = 16
    for t in range(cmp_ref.shape[1]):
        for c0 in range(0, nc, band):
            c = c0 + lax.broadcasted_iota(jnp.int32, (band, TILE), 0)
            s = t * TILE + lax.broadcasted_iota(jnp.int32, (band, TILE), 1)
            dist = s - (c * CMP_STRIDE + CMP_BLOCK - 1)
            lo = t * TILE - ((c0 + band - 1) * CMP_STRIDE + CMP_BLOCK - 1)
            hi = t * TILE + TILE - 1 - (c0 * CMP_STRIDE + CMP_BLOCK - 1)
            cmp_ref[0, t, c0:c0 + band, :] = jnp.where((dist >= 0) & (c < n_cmp),
                                                       lookup(jnp.maximum(dist, 0), lo, hi), NEG)


def _bias_tables(rel_bias, seq):
    nq = seq // TILE
    nwin = WINDOW // TILE + 1
    nc = seq // CMP_STRIDE
    n_cmp = nc - CMP_BLOCK // CMP_STRIDE + 1
    return pl.pallas_call(
        functools.partial(_bias_kernel, n_cmp=n_cmp),
        grid=(NSA_HEADS,),
        in_specs=[pl.BlockSpec(memory_space=pltpu.SMEM)],
        out_specs=[pl.BlockSpec((1, nq + 1, TILE, TILE), lambda h: (h, 0, 0, 0)),
                   pl.BlockSpec((1, nwin + 1, TILE, TILE), lambda h: (h, 0, 0, 0)),
                   pl.BlockSpec((1, nq, nc, TILE), lambda h: (h, 0, 0, 0))],
        out_shape=[jax.ShapeDtypeStruct((NSA_HEADS, nq + 1, TILE, TILE), jnp.float32),
                   jax.ShapeDtypeStruct((NSA_HEADS, nwin + 1, TILE, TILE), jnp.float32),
                   jax.ShapeDtypeStruct((NSA_HEADS, nq, nc, TILE), jnp.float32)],
        compiler_params=_cparams(1), name="bias_tables",
    )(rel_bias.astype(jnp.float32))


def _gelu_tanh(x):
    return x * (0.5 * (1.0 + jnp.tanh(math.sqrt(2.0 / math.pi) * (x + 0.044715 * (x * x * x)))))


def _compress_kernel(xk_ref, xv_ref, pos_ref, w1k_ref, w1v_ref, w2_ref, o_ref, wbd):
    nc = o_ref.shape[2]
    half = CMP_BLOCK // 2
    dh = NSA_HEAD_DIM
    bf16 = jnp.bfloat16

    @pl.when(pl.program_id(0) == 0)
    def _():
        zero = jnp.zeros((dh, CMP_HIDDEN), bf16)
        for kv, w1_ref in enumerate((w1k_ref, w1v_ref)):
            for a in range(2):
                for l in range(half):
                    r0 = (a * half + l) * dh
                    wl = w1_ref[r0:r0 + dh, :].astype(bf16)
                    wbd[kv, a, l * NSA_KV:(l + 1) * NSA_KV, :] = jnp.concatenate(
                        [jnp.concatenate([wl, zero], axis=1), jnp.concatenate([zero, wl], axis=1)], axis=0)

    rs = [jnp.concatenate([x_ref[pl.ds(l, nc, stride=CMP_STRIDE), :] for l in range(CMP_STRIDE)], axis=1)
          for x_ref in (xk_ref, xv_ref)]
    a = [jnp.dot((r + pos_ref[kv, 0:1, :]).astype(bf16), wbd[kv, 0], preferred_element_type=jnp.float32)
         for kv, r in enumerate(rs)]
    b = [jnp.dot((r + pos_ref[kv, 1:2, :]).astype(bf16), wbd[kv, 1], preferred_element_type=jnp.float32)
         for kv, r in enumerate(rs)]
    for kv in range(2):
        hid = a[kv] + pltpu.roll(b[kv], nc - 1, 0)
        o_ref[0, kv] = jnp.dot(_gelu_tanh(hid).astype(bf16), w2_ref[kv], preferred_element_type=jnp.float32)


def _compress(pf, pos2, w1_k, w1_v, w2p, bsz, seq):
    nc = seq // CMP_STRIDE
    width = CMP_STRIDE * NSA_KV
    hid = NSA_GROUPS * CMP_HIDDEN
    assert CMP_BLOCK == 2 * CMP_STRIDE and PF_VC == PF_KC + NSA_KV
    return pl.pallas_call(
        _compress_kernel,
        grid=(bsz,),
        in_specs=[pl.BlockSpec((seq, NSA_KV), lambda b: (b, PF_KC // NSA_KV)),
                  pl.BlockSpec((seq, NSA_KV), lambda b: (b, PF_VC // NSA_KV)),
                  pl.BlockSpec((2, 2, width), lambda b: (0, 0, 0)),
                  pl.BlockSpec(w1_k.shape, lambda b: (0, 0)),
                  pl.BlockSpec(w1_v.shape, lambda b: (0, 0)),
                  pl.BlockSpec((2, hid, LANES), lambda b: (0, 0, 0))],
        out_specs=pl.BlockSpec((1, 2, nc, LANES), lambda b: (b, 0, 0, 0)),
        out_shape=jax.ShapeDtypeStruct((bsz, 2, nc, LANES), jnp.float32),
        scratch_shapes=[pltpu.VMEM((2, 2, width, hid), jnp.bfloat16)],
        compiler_params=_cparams(1), name="compress",
    )(pf, pf, pos2, w1_k, w1_v, w2p)


def _prep_compress_weights(pos_k, pos_v, w2_k, w2_v):
    eye = jnp.eye(NSA_GROUPS, dtype=jnp.float32)
    half = CMP_BLOCK // 2

    def w2_both(w2):
        return jnp.einsum('jd,gh->gjhd', w2, eye).reshape(NSA_GROUPS * CMP_HIDDEN, NSA_GROUPS * NSA_HEAD_DIM)

    def pos_both(pos):
        p = pos.reshape(2, half, 1, NSA_HEAD_DIM)
        return jnp.broadcast_to(p, (2, half, NSA_GROUPS, NSA_HEAD_DIM)).reshape(2, half * NSA_KV)

    pos2 = jnp.stack([pos_both(pos_k), pos_both(pos_v)]).astype(jnp.float32)
    w2p = jnp.stack([w2_both(w2_k), w2_both(w2_v)]).astype(jnp.bfloat16)
    return pos2, w2p


def _pair_lanes(x, g):
    sw = pltpu.roll(x, LANES // 2, 1)
    lane = lax.broadcasted_iota(jnp.int32, x.shape, 1)
    own = (lane < LANES // 2) == (g == 0)
    return jnp.where(own, x, sw)


def _group_rows_t(x, g):
    xt = x.T
    half = LANES // 2
    return jnp.where(g == 0, xt[0:half, :], xt[half:LANES, :])


def _nsa_step_kernel(q_ref, qn_ref, ks_ref, kw_ref, vs_ref, vw_ref, kcvc_ref, bc_ref, bsel_ref, bwin_ref,
                     small_ref, za_ref, o_ref,
                     ksa, kwp, vst, vwt, kcs, vct, qa_s, qp_s, ocmp_s, sbuf, *, n_cmp, n_sel, n_top):
    g = pl.program_id(0)
    step_id = pl.program_id(2)
    tq = TILE
    dh = NSA_HEAD_DIM
    seq = ks_ref.shape[0]
    nq = seq // TILE
    nc = kcvc_ref.shape[2]
    n_wt = bwin_ref.shape[1] - 1
    heads = range(NSA_HPG)
    f32 = jnp.float32
    bf16 = jnp.bfloat16
    nt_dims = (((1,), (1,)), ((), ()))
    gk = SEL_GROUP * TILE

    def select_tile(src_ref, r0, tile, slot):
        q32 = src_ref[r0:r0 + tq, :].astype(f32)
        low_half = lax.broadcasted_iota(jnp.int32, (tq, LANES), 1) < NSA_HEAD_DIM
        qh = []
        for hh in heads:
            blk = q32[:, (hh // 2) * LANES:(hh // 2 + 1) * LANES]
            if hh % 2:
                blk = pltpu.roll(blk, LANES // 2, 1)
            qh.append(jnp.where(low_half, blk, 0.0).astype(bf16))
        for hh in heads:
            qp_s[slot, hh * tq:(hh + 1) * tq, :] = qh[hh]
        q_all = qp_s[slot]
        s_c = lax.dot_general(kcs[...], q_all, nt_dims, preferred_element_type=f32)
        bias_c = jnp.concatenate([bc_ref[hh, tile] for hh in heads], axis=1)
        valid = bias_c > 0.5 * NEG
        s_c = s_c + bias_c
        e = jnp.where(valid, jnp.exp2(s_c - jnp.max(s_c, axis=0, keepdims=True)), 0.0)
        den = jnp.maximum(jnp.sum(e, axis=0, keepdims=True), 1e-30)
        p_c = e * (1.0 / den)
        psum = sum(p_c[:, hh * tq:(hh + 1) * tq] for hh in heads)
        sj = lax.broadcasted_iota(jnp.int32, (n_sel, nc), 0)
        ci = lax.broadcasted_iota(jnp.int32, (n_sel, nc), 1)
        overlap = ((ci * CMP_STRIDE < (sj + 1) * SEL_BLOCK) & (ci * CMP_STRIDE + CMP_BLOCK > sj * SEL_BLOCK)
                   & (ci < n_cmp)).astype(bf16)
        imp_t = sum(jnp.dot(overlap, part, preferred_element_type=f32)
                    for part in _split_bf16(psum, 3))
        ocmp_s[slot] = jnp.dot(vct[...], p_c.astype(bf16), preferred_element_type=f32)
        blk = lax.broadcasted_iota(jnp.int32, (n_sel, tq), 0)
        cur = (tile * tq + lax.broadcasted_iota(jnp.int32, (n_sel, tq), 1)) >> 6
        forced = (blk == 0) | (blk == cur) | (blk == cur - 1)
        imp_t = jnp.where(forced, FORCE, jnp.where(blk > cur, -FORCE, imp_t))
        rank = jnp.zeros((n_sel, tq), jnp.int32)
        for j in range(n_sel):
            other = imp_t[j:j + 1, :]
            ahead = (other > imp_t) | ((other == imp_t) & (blk > j))
            rank = rank + ahead.astype(jnp.int32)
        selb = jnp.where(rank < n_top, 0.0, NEG)
        pieces = [jnp.zeros((SEL_LANE0, tq), f32), selb]
        if LANES - SEL_LANE0 - n_sel > 0:
            pieces.append(jnp.zeros((LANES - SEL_LANE0 - n_sel, tq), f32))
        selb_r = jnp.concatenate(pieces, axis=0).T.astype(bf16)
        for hh in heads:
            qa_s[slot, hh * tq:(hh + 1) * tq, :] = qh[hh] + selb_r

    @pl.when(step_id == 0)
    def _():
        row = lax.broadcasted_iota(jnp.int32, (seq, LANES), 0)
        lane = lax.broadcasted_iota(jnp.int32, (seq, LANES), 1)
        onehot = (lane - SEL_LANE0) == (row >> 6)
        own = lane < NSA_HEAD_DIM
        ks_g = jnp.where(own, _pair_lanes(ks_ref[...].astype(f32), g), 0.0)
        ksa[...] = jnp.where(onehot, 1.0, ks_g).astype(bf16)
        pad = (n_wt - 1) * TILE
        kwp[0:pad, :] = jnp.zeros((pad, LANES), bf16)
        kwp[pad:pad + seq, :] = jnp.where(own, _pair_lanes(kw_ref[...].astype(f32), g), 0.0).astype(bf16)
        for kt in range(n_wt - 1):
            vwt[kt] = jnp.zeros(vwt.shape[1:], bf16)
        extra = vst.shape[1] - dh
        ones_row = (lax.broadcasted_iota(jnp.int32, (extra, TILE), 0) == 0).astype(bf16)
        for kt in range(nq):
            rows = slice(kt * TILE, (kt + 1) * TILE)
            vst[kt, 0:dh, :] = _group_rows_t(vs_ref[rows, :].astype(f32), g).astype(bf16)
            vst[kt, dh:dh + extra, :] = ones_row
            vwt[kt + n_wt - 1, 0:dh, :] = _group_rows_t(vw_ref[rows, :].astype(f32), g).astype(bf16)
            vwt[kt + n_wt - 1, dh:dh + extra, :] = ones_row
        kcs[...] = _pair_lanes(kcvc_ref[0, 0], g).astype(bf16)
        for ct in range(nc // TILE):
            rows = slice(ct * TILE, (ct + 1) * TILE)
            vct[:, rows] = _group_rows_t(kcvc_ref[0, 1, rows, :], g).astype(bf16)
        select_tile(q_ref, 0, 0, 0)

    tps = NSA_TILES_PER_STEP

    def one_tile(n_groups, slot):
        qt = step_id * tps + slot
        rows = slice(slot * tq, (slot + 1) * tq)
        q_all = qp_s[slot]
        q_aug = qa_s[slot]
        kw_rows = kwp[pl.ds(pl.multiple_of(qt * TILE, TILE), n_wt * TILE), :]
        s_w = lax.dot_general(kw_rows, q_all, nt_dims, preferred_element_type=f32)

        def sel_scores(gi):
            s = lax.dot_general(ksa[gi * gk:(gi + 1) * gk, :], q_aug, nt_dims, preferred_element_type=f32)
            tiles = []
            for t in range(SEL_GROUP):
                dt = qt - (gi * SEL_GROUP + t)
                idx = jnp.where(dt >= 0, dt, nq)
                tiles.append(jnp.concatenate([bsel_ref[hh, idx] for hh in heads], axis=1))
            sbuf[gi % 2] = s + jnp.concatenate(tiles, axis=0)

        sel_scores(0)
        if slot + 1 < tps:
            select_tile(q_ref, (slot + 1) * tq, qt + 1, slot + 1)
        else:
            select_tile(qn_ref, 0, jnp.minimum(qt + 1, nq - 1), 0)

        w_tiles = []
        for t in range(n_wt):
            dt = n_wt - 1 - t
            idx = jnp.where(qt >= dt, dt, n_wt)
            w_tiles.append(jnp.concatenate([bwin_ref[hh, idx] for hh in heads], axis=1))
        s_w = s_w + jnp.concatenate(w_tiles, axis=0)
        p_w = jnp.exp2(s_w - jnp.max(s_w, axis=0, keepdims=True))
        v_w = jnp.concatenate([vwt[qt + t] for t in range(n_wt)], axis=1)
        acc_w = jnp.dot(v_w, p_w.astype(bf16), preferred_element_type=f32)
        o_win = acc_w[0:dh, :] * (1.0 / acc_w[dh:dh + 1, :])

        m = jnp.full((1, NSA_HPG * tq), NEG, f32)
        acc = jnp.zeros((vst.shape[1], NSA_HPG * tq), f32)
        for gi in range(n_groups):
            if gi + 1 < n_groups:
                sel_scores(gi + 1)
            s = sbuf[gi % 2]
            m_new = jnp.maximum(m, jnp.max(s, axis=0, keepdims=True))
            alpha = jnp.exp2(m - m_new)
            p = jnp.exp2(s - m_new)
            v_t = jnp.concatenate([vst[gi * SEL_GROUP + t] for t in range(SEL_GROUP)], axis=1)
            acc = alpha * acc + jnp.dot(v_t, p.astype(bf16), preferred_element_type=f32)
            m = m_new
        o_sel = acc[0:dh, :] * (1.0 / acc[dh:dh + 1, :])
        o_cmp = ocmp_s[slot]

        gates_t = jax.nn.sigmoid(small_ref[rows, :]).T
        mixed = []
        for hh in heads:
            def gate(br):
                r = br * NSA_HEADS + hh
                return jnp.where(g == 0, gates_t[r:r + 1, :], gates_t[r + NSA_HPG:r + NSA_HPG + 1, :])
            cols = slice(hh * tq, (hh + 1) * tq)
            mixed.append(gate(0) * o_cmp[:, cols] + gate(1) * o_sel[:, cols] + gate(2) * o_win[:, cols])
        o = jnp.concatenate([jnp.concatenate(mixed[2 * j:2 * j + 2], axis=0).T for j in range(NSA_HPG // 2)],
                            axis=1)
        z = za_ref[rows, :]
        o_ref[rows, :] = (o * (z * jax.nn.sigmoid(z))).astype(o_ref.dtype)

    def step(first_tile):
        for slot in range(tps):
            one_tile((first_tile + slot) // SEL_GROUP + 1, slot)

    variants = {}
    for s in range(nq // tps):
        key = tuple((s * tps + slot) // SEL_GROUP for slot in range(tps))
        variants.setdefault(key, []).append(s)
    for steps in variants.values():
        cond = functools.reduce(jnp.logical_or, [step_id == s for s in steps])
        pl.when(cond)(functools.partial(step, steps[0] * tps))


def _nsa(pb, pf, kcvc, bias_sel, bias_win, bias_cmp, bsz, seq):
    nq = seq // TILE
    nc = seq // CMP_STRIDE
    n_cmp = nc - CMP_BLOCK // CMP_STRIDE + 1
    n_sel = seq // SEL_BLOCK
    n_top = min(SEL_TOPK, n_sel)
    nwin = bias_win.shape[1]
    tps = NSA_TILES_PER_STEP
    assert nq % SEL_GROUP == 0 and nc % TILE == 0 and bias_sel.shape[1] == nq + 1
    assert tps >= 2 and nq % tps == 0
    ns = nq // tps
    rows = tps * TILE
    gw = NSA_HPG * NSA_HEAD_DIM
    kern = functools.partial(_nsa_step_kernel, n_cmp=n_cmp, n_sel=n_sel, n_top=n_top)
    return pl.pallas_call(
        kern,
        grid=(NSA_GROUPS, bsz, ns),
        in_specs=[
            pl.BlockSpec((rows, gw), lambda g, b, t: (b * ns + t, PB_QP // gw + g)),
            pl.BlockSpec((TILE, gw), lambda g, b, t: (b * nq + jnp.minimum((t + 1) * tps, nq - 1),
                                                      PB_QP // gw + g)),
            pl.BlockSpec((seq, LANES), lambda g, b, t: (b, PB_KS // LANES)),
            pl.BlockSpec((seq, LANES), lambda g, b, t: (b, PB_KW // LANES)),
            pl.BlockSpec((seq, LANES), lambda g, b, t: (b, PB_VS // LANES)),
            pl.BlockSpec((seq, LANES), lambda g, b, t: (b, PB_VW // LANES)),
            pl.BlockSpec((1, 2, nc, LANES), lambda g, b, t: (b, 0, 0, 0)),
            pl.BlockSpec((NSA_HPG, nq, nc, TILE), lambda g, b, t: (g, 0, 0, 0)),
            pl.BlockSpec((NSA_HPG, nq + 1, TILE, TILE), lambda g, b, t: (g, 0, 0, 0)),
            pl.BlockSpec((NSA_HPG, nwin, TILE, TILE), lambda g, b, t: (g, 0, 0, 0)),
            pl.BlockSpec((rows, LANES), lambda g, b, t: (b * ns + t, PF_SMALL // LANES)),
            pl.BlockSpec((rows, gw), lambda g, b, t: (b * ns + t, PF_ZA // gw + g)),
        ],
        out_specs=pl.BlockSpec((rows, gw), lambda g, b, t: (b * ns + t, g)),
        out_shape=jax.ShapeDtypeStruct((bsz * seq, NSA_WIDTH), jnp.bfloat16),
        scratch_shapes=[
            pltpu.VMEM((seq, LANES), jnp.bfloat16),
            pltpu.VMEM((seq + (nwin - 2) * TILE, LANES), jnp.bfloat16),
            pltpu.VMEM((nq, NSA_V_ROWS, TILE), jnp.bfloat16),
            pltpu.VMEM((nq + nwin - 2, NSA_V_ROWS, TILE), jnp.bfloat16),
            pltpu.VMEM((nc, LANES), jnp.bfloat16),
            pltpu.VMEM((NSA_HEAD_DIM, nc), jnp.bfloat16),
            pltpu.VMEM((tps, NSA_HPG * TILE, LANES), jnp.bfloat16),
            pltpu.VMEM((tps, NSA_HPG * TILE, LANES), jnp.bfloat16),
            pltpu.VMEM((tps, NSA_HEAD_DIM, NSA_HPG * TILE), jnp.float32),
            pltpu.VMEM((2, SEL_GROUP * TILE, NSA_HPG * TILE), jnp.float32),
        ],
        compiler_params=_cparams(3), name="nsa",
    )(pb, pb, pb, pb, pb, pb, kcvc, bias_cmp, bias_sel, bias_win, pf, pf)


def _split_bf16(a, n):
    parts = []
    for _ in range(n - 1):
        hi = a.astype(jnp.bfloat16)
        parts.append(hi)
        a = a - hi.astype(jnp.float32)
    parts.append(a.astype(jnp.bfloat16))
    return parts


def _softplus(x):
    return jnp.maximum(x, 0.0) + jnp.log(1.0 + jnp.exp(-jnp.abs(x)))


def _dn_kernel(scal_ref, q_ref, k_ref, v_ref, small_ref, z_ref, cw_ref, nw_ref, o_ref,
               qn, kn, vn, bet, gl, mm, nn, qq, oo, dd, mm2, dd2, *, seq):
    h = pl.program_id(1)
    n_rows = q_ref.shape[0]
    nb = n_rows // seq
    c = DN_CHUNK
    n_chunks = seq // c
    d = DN_HEAD_DIM
    prep_unroll = math.gcd(DN_PREP_UNROLL, nb * n_chunks)
    pair_unroll = math.gcd(DN_PAIR_UNROLL, nb * n_chunks // 2)
    out_unroll = math.gcd(DN_OUT_UNROLL, nb * n_chunks)

    head = 8

    def conv_body(x_ref, which):
        n = n_rows - head
        y = x_ref[pl.ds(head, n), :] * cw_ref[which, DN_CONV - 1:DN_CONV, :]
        for j in range(DN_CONV - 1):
            y = y + x_ref[pl.ds(head - (DN_CONV - 1 - j), n), :] * cw_ref[which, j:j + 1, :]
        return y

    def conv_head(x_ref, which, r0):
        x = x_ref[pl.ds(r0, head), :]
        rowi = lax.broadcasted_iota(jnp.int32, (head, d), 0)
        y = x * cw_ref[which, DN_CONV - 1:DN_CONV, :]
        for j in range(DN_CONV - 1):
            sh = DN_CONV - 1 - j
            y = y + jnp.where(rowi >= sh, pltpu.roll(x, sh, 0), 0.0) * cw_ref[which, j:j + 1, :]
        return y

    def silu(y):
        return y * jax.nn.sigmoid(y)

    def l2n(t):
        return t * lax.rsqrt(jnp.sum(t * t, axis=-1, keepdims=True) + 1e-6)

    finish = (lambda y: l2n(silu(y)) * (d ** -0.5), lambda y: l2n(silu(y)), silu)
    for which, (x_ref, dst) in enumerate(((q_ref, qn), (k_ref, kn), (v_ref, vn))):
        dst[pl.ds(head, n_rows - head), :] = finish[which](conv_body(x_ref, which))
        for b in range(nb):
            dst[pl.ds(b * seq, head), :] = finish[which](conv_head(x_ref, which, b * seq))
    decay_rate = -jnp.exp(scal_ref[0, h])
    dt_bias = scal_ref[1, h]
    for blk in range(n_rows // LANES):
        tok = slice(blk * LANES, (blk + 1) * LANES)
        st = small_ref[tok, :].T
        b_row = st[SMALL_BETA:SMALL_BETA + 1, :]
        a_row = st[SMALL_A:SMALL_A + 1, :]
        for hh in range(1, DN_HEADS):
            b_row = jnp.where(h == hh, st[SMALL_BETA + hh:SMALL_BETA + hh + 1, :], b_row)
            a_row = jnp.where(h == hh, st[SMALL_A + hh:SMALL_A + hh + 1, :], a_row)
        bet[tok, :] = jnp.broadcast_to(jax.nn.sigmoid(b_row), (LANES, d)).T
        gl[tok, :] = jnp.broadcast_to(decay_rate * _softplus(a_row + dt_bias), (LANES, d)).T

    f32 = jnp.float32
    bf16 = jnp.bfloat16
    nt_dims = (((1,), (1,)), ((), ()))
    ri = lax.broadcasted_iota(jnp.int32, (c, 2 * c), 0)
    lane2 = lax.broadcasted_iota(jnp.int32, (c, 2 * c), 1)
    first = lane2 < c
    cj = lane2 & (c - 1)
    incl = ri >= cj
    strict = ri > cj
    tril_b = (lax.broadcasted_iota(jnp.int32, (c, c), 0) >= lax.broadcasted_iota(jnp.int32, (c, c), 1)).astype(bf16)
    bs = DN_INV_BLOCK
    sh = bs.bit_length() - 1
    same_diag = (ri >> sh) == (cj >> sh)
    level_masks = []
    while (1 << sh) < c:
        level_masks.append(((ri >> (sh + 1)) == (cj >> (sh + 1))) & ((ri >> sh) > (cj >> sh)))
        sh += 1
    bd_r = lax.broadcasted_iota(jnp.int32, (2 * c, 2 * c), 0)
    bd_c = lax.broadcasted_iota(jnp.int32, (2 * c, 2 * c), 1)
    block_diag = (bd_r >= c) == (bd_c >= c)

    def dot_pair(a, b):
        bb = b.astype(bf16)
        bb = jnp.where(block_diag, jnp.concatenate([bb, bb], axis=0), jnp.zeros((), bf16))
        return jnp.dot(a.astype(bf16), bb, preferred_element_type=f32)

    def stack_diag(x0, x1):
        z = jnp.zeros_like(x0)
        return jnp.concatenate([jnp.concatenate([x0, z], axis=1), jnp.concatenate([z, x1], axis=1)], axis=0)

    def chunk_prep(it, carry):
        ids = [it * prep_unroll + cc for cc in range(prep_unroll)]
        rows = [pl.ds(pl.multiple_of(i * c, c), c) for i in ids]
        pairs = range(0, prep_unroll, 2)
        ks = [kn[r, :] for r in rows]
        betas = [bet[r, :] for r in rows]
        gcbs = [sum(jnp.dot(tril_b, part, preferred_element_type=f32)
                    for part in _split_bf16(gl[r, :], DN_DECAY_PARTS)) for r in rows]
        kbs = [k * beta for k, beta in zip(ks, betas)]
        k2ts = [jnp.concatenate([ks[j], ks[j + 1]], axis=0).T.astype(bf16) for j in pairs]

        def against_pair_keys(xs):
            return [jnp.where(first,
                              jnp.dot(xs[j].astype(bf16), k2t, preferred_element_type=f32),
                              jnp.dot(xs[j + 1].astype(bf16), k2t, preferred_element_type=f32))
                    for j, k2t in zip(pairs, k2ts)]

        a_kks = against_pair_keys(kbs)
        decays = []
        for j in pairs:
            gc_col = jnp.where(first, gcbs[j], gcbs[j + 1])
            gc_row = jnp.concatenate([gcbs[j], gcbs[j + 1]], axis=0).T[0:c, :]
            decays.append(jnp.where(incl, jnp.exp(jnp.where(incl, gc_col - gc_row, 0.0)), 0.0))
        lows = [jnp.where(strict, a * dec, 0.0) for a, dec in zip(a_kks, decays)]
        pws = [jnp.where(same_diag, -low, 0.0) for low in lows]
        es = list(pws)
        for _ in range(max(1, (bs - 1).bit_length()) - 1):
            pws = [dot_pair(pw, pw) for pw in pws]
            es = [e + pw + dot_pair(e, pw) for e, pw in zip(es, pws)]
        for below in level_masks:
            offs = [jnp.where(below, low, 0.0) for low in lows]
            xs = [off + dot_pair(e, off) for e, off in zip(es, offs)]
            es = [e - (x + dot_pair(x, e)) for e, x in zip(es, xs)]
        egcs = [jnp.exp(gcb) for gcb in gcbs]
        rhss = [jnp.concatenate([vn[r, :] * beta, kb * egc], axis=1)
                for r, beta, kb, egc in zip(rows, betas, kbs, egcs)]
        uwbs = []
        for e, j in zip(es, pairs):
            e_hi, e_lo = _split_bf16(e, 2)
            r2 = stack_diag(rhss[j], rhss[j + 1]).astype(bf16)
            er = jnp.dot(e_hi, r2, preferred_element_type=f32) + jnp.dot(e_lo, r2, preferred_element_type=f32)
            uwbs.append((rhss[j] + er[:, 0:2 * d]).astype(bf16))
            uwbs.append((rhss[j + 1] + er[:, 2 * d:4 * d]).astype(bf16))
        qs = [qn[r, :] for r in rows]
        a_qks = [a * dec for a, dec in zip(against_pair_keys(qs), decays)]
        g_lasts = [gcb[c - 1:c, :] for gcb in gcbs]
        kdec_ts = []
        for k, gcb, g_last in zip(ks, gcbs, g_lasts):
            kdec = k * jnp.exp(g_last - gcb)
            kdec_ts.append(jnp.concatenate([kdec, jnp.zeros_like(kdec)], axis=0).T[:, 0:c].astype(bf16))
        nms = [jnp.dot(kt, uwb, preferred_element_type=f32) for kt, uwb in zip(kdec_ts, uwbs)]
        oq2s = [jnp.dot(a.astype(bf16), stack_diag(uwbs[j], uwbs[j + 1]), preferred_element_type=f32)
                for a, j in zip(a_qks, pairs)]
        oqs = []
        for oq2 in oq2s:
            oqs += [oq2[:, 0:2 * d], oq2[:, 2 * d:4 * d]]
        for i, r, nm, oq, q, egc, g_last in zip(ids, rows, nms, oqs, qs, egcs, g_lasts):
            m0 = pl.ds(pl.multiple_of(i * d, d), d)
            nn[m0, :] = nm[:, 0:d]
            mm[m0, :] = nm[:, d:2 * d].astype(bf16)
            oo[r, :] = oq[:, 0:d]
            qq[r, :] = (q * egc - oq[:, d:2 * d]).astype(bf16)
            dd[pl.ds(pl.multiple_of(i * 8, 8), 8), :] = jnp.broadcast_to(jnp.exp(g_last), (8, d))
        return carry

    lax.fori_loop(0, nb * n_chunks // prep_unroll, chunk_prep, 0)

    nn2, ss = kn, vn

    def chunk_pair(it, carry):
        ids = [it * pair_unroll + cc for cc in range(pair_unroll)]
        blk1 = [pl.ds(pl.multiple_of(2 * p * d, d), d) for p in ids]
        blk2 = [pl.ds(pl.multiple_of((2 * p + 1) * d, d), d) for p in ids]
        d1s = [dd[pl.ds(pl.multiple_of(2 * p * 8, 8), 1), :] for p in ids]
        d2s = [dd[pl.ds(pl.multiple_of((2 * p + 1) * 8, 8), 1), :] for p in ids]
        m1s = [mm[b, :] for b in blk1]
        m2s = [mm[b, :] for b in blk2]
        n1s = [nn[b, :] for b in blk1]
        xs = [jnp.dot(m2, jnp.concatenate([m1, n1.astype(bf16)], axis=1), preferred_element_type=f32)
              for m1, m2, n1 in zip(m1s, m2s, n1s)]
        for p, b2, d1, d2, m1, m2, n1, x in zip(ids, blk2, d1s, d2s, m1s, m2s, n1s, xs):
            pblk = pl.ds(pl.multiple_of(p * d, d), d)
            mm2[pblk, :] = (d2 * m1.astype(f32) + d1 * m2.astype(f32) - x[:, 0:d]).astype(bf16)
            nn2[pblk, :] = d2 * n1 - x[:, d:2 * d] + nn[b2, :]
            dd2[pl.ds(pl.multiple_of(p * 8, 8), 8), :] = jnp.broadcast_to(d1 * d2, (8, d))
        return carry

    n_pairs = n_chunks // 2
    lax.fori_loop(0, nb * n_pairs // pair_unroll, chunk_pair, 0)

    def pair_scan(i, states):
        ids = [b * n_pairs + i for b in range(nb)]
        blocks = [pl.ds(pl.multiple_of(j * d, d), d) for j in ids]
        for blk, s in zip(blocks, states):
            ss[blk, :] = s
        prods = [jnp.dot(mm2[blk, :], s.astype(bf16), preferred_element_type=f32) for blk, s in zip(blocks, states)]
        return tuple(s * dd2[pl.ds(pl.multiple_of(j * 8, 8), 1), :] - pr + nn2[blk, :]
                     for s, j, blk, pr in zip(states, ids, blocks, prods))

    lax.fori_loop(0, n_pairs, pair_scan, tuple(jnp.zeros((d, d), f32) for _ in range(nb)))

    nw = nw_ref[...]

    def chunk_out(it, carry):
        pids = [it * (out_unroll // 2) + cc for cc in range(out_unroll // 2)]
        s0 = [ss[pl.ds(pl.multiple_of(p * d, d), d), :] for p in pids]
        s0b = [s.astype(bf16) for s in s0]
        s1b = [(s * dd[pl.ds(pl.multiple_of(2 * p * 8, 8), 1), :]
                - jnp.dot(mm[pl.ds(pl.multiple_of(2 * p * d, d), d), :], sb, preferred_element_type=f32)
                + nn[pl.ds(pl.multiple_of(2 * p * d, d), d), :]).astype(bf16)
               for p, s, sb in zip(pids, s0, s0b)]
        rows, states = [], []
        for p, sb0, sb1 in zip(pids, s0b, s1b):
            rows += [pl.ds(pl.multiple_of(2 * p * c, c), c), pl.ds(pl.multiple_of((2 * p + 1) * c, c), c)]
            states += [sb0, sb1]
        outs = [jnp.dot(qq[r, :], sb, preferred_element_type=f32) + oo[r, :] for r, sb in zip(rows, states)]
        for r, o in zip(rows, outs):
            o = o * lax.rsqrt(jnp.mean(o * o, axis=-1, keepdims=True) + 1e-6) * nw
            z = z_ref[r, :]
            o_ref[r, :] = (o * (z * jax.nn.sigmoid(z))).astype(o_ref.dtype)
        return carry

    lax.fori_loop(0, nb * n_chunks // out_unroll, chunk_out, 0)


def _deltanet(pf, conv_w, a_log, dt_bias, norm_w, bsz, seq):
    d = DN_HEAD_DIM
    nb = DN_BATCHES if bsz % DN_BATCHES == 0 else 1
    rows = nb * seq
    n_chunks = rows // DN_CHUNK
    qkv0 = PF_QKVB // d
    scal = jnp.stack([a_log, dt_bias]).astype(jnp.float32)
    f32 = jnp.float32
    bf16 = jnp.bfloat16
    cw4 = conv_w.astype(f32).reshape(DN_CONV, 3, DN_HEADS, d).transpose(2, 1, 0, 3)
    assert 2 * DN_CHUNK == d and (seq // DN_CHUNK) % 2 == 0
    assert DN_PREP_UNROLL % 2 == 0 and DN_OUT_UNROLL % 2 == 0
    return pl.pallas_call(
        functools.partial(_dn_kernel, seq=seq),
        grid=(bsz // nb, DN_HEADS),
        in_specs=[
            pl.BlockSpec(memory_space=pltpu.SMEM),
            pl.BlockSpec((rows, d), lambda b, h: (b, qkv0 + h)),
            pl.BlockSpec((rows, d), lambda b, h: (b, qkv0 + DN_HEADS + h)),
            pl.BlockSpec((rows, d), lambda b, h: (b, qkv0 + 2 * DN_HEADS + h)),
            pl.BlockSpec((rows, LANES), lambda b, h: (b, PF_SMALL // LANES)),
            pl.BlockSpec((rows, d), lambda b, h: (b, PF_ZB // d + h)),
            pl.BlockSpec((None, 3, DN_CONV, d), lambda b, h: (h, 0, 0, 0)),
            pl.BlockSpec((1, d), lambda b, h: (0, 0)),
        ],
        out_specs=pl.BlockSpec((rows, d), lambda b, h: (b, h)),
        out_shape=jax.ShapeDtypeStruct((bsz * seq, DN_WIDTH), jnp.bfloat16),
        scratch_shapes=[
            pltpu.VMEM((rows, d), f32), pltpu.VMEM((rows, d), f32), pltpu.VMEM((rows, d), f32),
            pltpu.VMEM((rows, d), f32), pltpu.VMEM((rows, d), f32),
            pltpu.VMEM((n_chunks * d, d), bf16), pltpu.VMEM((n_chunks * d, d), f32),
            pltpu.VMEM((rows, d), bf16), pltpu.VMEM((rows, d), f32),
            pltpu.VMEM((n_chunks * 8, d), f32),
            pltpu.VMEM((n_chunks // 2 * d, d), bf16),
            pltpu.VMEM((n_chunks // 2 * 8, d), f32),
        ],
        compiler_params=_cparams(2), name="deltanet",
    )(scal, pf, pf, pf, pf, pf, cw4, norm_w.astype(f32).reshape(1, d))


def _out_kernel(oa_ref, ob_ref, gma_ref, gmb_ref, x_ref, p_ref, wa_ref, wb_ref, wo_ref, wpg_ref, wp_ref,
                lng_ref, lnb_ref, o_ref):
    f32 = jnp.float32
    bf16 = jnp.bfloat16
    tm = o_ref.shape[0]
    sub = tm // OUT_SUBTILES
    parts = [slice(i * sub, (i + 1) * sub) for i in range(OUT_SUBTILES)]
    y_a = [jnp.dot(oa_ref[r, :], wa_ref[...], preferred_element_type=f32) for r in parts]
    y_b = [jnp.dot(ob_ref[r, :], wb_ref[...], preferred_element_type=f32) for r in parts]
    pw = [jnp.dot(p_ref[r, :].astype(bf16), wp_ref[...], preferred_element_type=f32) for r in parts]
    mix = [(jax.nn.sigmoid(gma_ref[r, :]) * ya + jax.nn.sigmoid(gmb_ref[r, :]) * yb).astype(bf16)
           for r, ya, yb in zip(parts, y_a, y_b)]
    h = [DEEPNORM_ALPHA * x_ref[r, :] + jnp.dot(mx, wo_ref[...], preferred_element_type=f32)
         for r, mx in zip(parts, mix)]
    gate = [jnp.dot(hh.astype(bf16), wpg_ref[...], preferred_element_type=f32) for hh in h]
    for r, hh, gt, pp in zip(parts, h, gate, pw):
        hh = hh + jax.nn.sigmoid(gt) * pp
        mu = jnp.mean(hh, axis=-1, keepdims=True)
        hc = hh - mu
        var = jnp.mean(hc * hc, axis=-1, keepdims=True)
        o_ref[r, :] = (hc * lax.rsqrt(var + 1e-5) * lng_ref[...] + lnb_ref[...]).astype(o_ref.dtype)


def _out_block(o_a, o_b, pf, x2, p2, wa, wb, wo, wpg, wp, ln_g, ln_b, tm):
    t = x2.shape[0]
    bf = jnp.bfloat16

    def full(shape):
        return pl.BlockSpec(shape, lambda i: (0, 0), pipeline_mode=pl.Buffered(1))

    return pl.pallas_call(
        _out_kernel,
        grid=(t // tm,),
        in_specs=[
            pl.BlockSpec((tm, NSA_WIDTH), lambda i: (i, 0)),
            pl.BlockSpec((tm, DN_WIDTH), lambda i: (i, 0)),
            pl.BlockSpec((tm, D_MODEL), lambda i: (i, PF_GM // D_MODEL)),
            pl.BlockSpec((tm, D_MODEL), lambda i: (i, PF_GM // D_MODEL + 1)),
            pl.BlockSpec((tm, D_MODEL), lambda i: (i, 0)),
            pl.BlockSpec((tm, PLE_DIM), lambda i: (i, 0)),
            full((NSA_WIDTH, D_MODEL)), full((DN_WIDTH, D_MODEL)), full((D_MODEL, D_MODEL)),
            full((D_MODEL, D_MODEL)), full((PLE_DIM, D_MODEL)), full((1, D_MODEL)), full((1, D_MODEL)),
        ],
        out_specs=pl.BlockSpec((tm, D_MODEL), lambda i: (i, 0)),
        out_shape=jax.ShapeDtypeStruct((t, D_MODEL), x2.dtype),
        compiler_params=_cparams(1), name="out_block",
    )(o_a, o_b, pf, pf, x2, p2, wa.astype(bf), wb.astype(bf), wo.astype(bf), wpg.astype(bf), wp.astype(bf),
      ln_g.astype(jnp.float32).reshape(1, D_MODEL), ln_b.astype(jnp.float32).reshape(1, D_MODEL))


def _layer(x, p, w_in, pos_k, pos_v, w1_k, w2_k, w1_v, w2_v, bias_tabs, conv_w, a_log, dt_bias, norm_w,
           w_a, w_b, w_o, w_ple, w_pg, ln_g, ln_b):
    bsz, seq, _ = x.shape
    t = bsz * seq
    x2 = x.reshape(t, D_MODEL)
    wb16, wf16 = _prep_w_in(w_in)
    pb, pf = _proj(x2, wb16, wf16, PROJ_TM if t % PROJ_TM == 0 else seq)

    pos2, w2p = _prep_compress_weights(pos_k, pos_v, w2_k, w2_v)
    kcvc = _compress(pf, pos2, w1_k, w1_v, w2p, bsz, seq)

    bias_sel, bias_win, bias_cmp = bias_tabs
    o_a = _nsa(pb, pf, kcvc, bias_sel, bias_win, bias_cmp, bsz, seq)
    o_b = _deltanet(pf, conv_w, a_log, dt_bias, norm_w, bsz, seq)
    out = _out_block(o_a, o_b, pf, x2, p.reshape(t, PLE_DIM), w_a, w_b, w_o, w_pg, w_ple, ln_g, ln_b,
                     OUT_TM if t % OUT_TM == 0 else seq)
    return out.reshape(bsz, seq, D_MODEL)


def kernel(x, p, w_in, cmp_pos_k, cmp_pos_v, cmp_w1_k, cmp_w2_k, cmp_w1_v, cmp_w2_v, rel_bias, dn_conv_w,
           dn_a_log, dn_dt_bias, dn_norm_w, w_branch_a, w_branch_b, w_out, w_ple, w_ple_gate, ln_g, ln_b):
    depth = w_in.shape[0]
    bias_tabs = _bias_tables(rel_bias, x.shape[1])
    for i in range(depth):
        x = _layer(x, p[i], w_in[i], cmp_pos_k[i], cmp_pos_v[i], cmp_w1_k[i], cmp_w2_k[i], cmp_w1_v[i],
                   cmp_w2_v[i], bias_tabs, dn_conv_w[i], dn_a_log[i], dn_dt_bias[i], dn_norm_w[i],
                   w_branch_a[i], w_branch_b[i], w_out[i], w_ple[i], w_ple_gate[i], ln_g[i], ln_b[i])
    return x
```
